```python
import jax, jax.numpy as jnp
from jax import lax
import numpy as np

D_MODEL = 1024
BATCH = 2
SEQ = 8192
DEPTH = 1

CHUNK = 64
D_BRANCH = D_MODEL // 2
D_CONV = D_BRANCH
CONV_WIDTH = 3
D_RWKV = D_BRANCH
RWKV_HEAD_DIM = 64
RWKV_HEADS = D_RWKV // RWKV_HEAD_DIM
DECAY_LORA = 64
AICL_LORA = 64
GATE_LORA = 128
MEM_LEN = 256
MEM_HEADS = 4
D_MEM = D_BRANCH
MEM_HEAD_DIM = D_MEM // MEM_HEADS
N_BRANCH = 3
D_IN_PROJ = 3 * D_CONV + 3 * D_RWKV + D_MEM
N_GROUPS = 8
EXPERTS_PER_GROUP = 8
N_EXPERTS = N_GROUPS * EXPERTS_PER_GROUP
TOP_K = 2
D_EXPERT = 512
ROUTE_BLOCK = 128
NORM_EPS = 1e-6
GN_EPS = 64e-5

kernel_name = "hybrid_conv_rwkv7_memxattn_hmoe_block"


def _rmsnorm(x, g):
    xf = x.astype(jnp.float32)
    y = xf * lax.rsqrt(jnp.mean(xf * xf, axis=-1, keepdims=True) + NORM_EPS)
    return (y * g.astype(jnp.float32)).astype(x.dtype)


def _shift(u):
    return jnp.pad(u, ((0, 0), (1, 0), (0, 0)))[:, :-1]


def _split_in_proj(p):
    sizes = [D_CONV] * 3 + [D_RWKV] * 3 + [D_MEM]
    offsets = np.cumsum(sizes)[:-1].tolist()
    return jnp.split(p, offsets, axis=-1)


def _short_conv_mixer(bg, cg, u, conv_w):
    seq = u.shape[1]
    cu = cg * u
    p = jnp.pad(cu, ((0, 0), (CONV_WIDTH - 1, 0), (0, 0)))
    conv = sum(p[:, j:j + seq] * conv_w[:, j] for j in range(CONV_WIDTH))
    return bg * conv


def _rwkv7_scan(r, w, k, v, kk, a):
    b, _, h, n = r.shape

    def step(state, inp):
        r_t, w_t, k_t, v_t, kk_t, a_t = inp
        sa = jnp.einsum('bhij,bhj->bhi', state, -kk_t)
        state = (state * w_t[:, :, None, :]
                 + sa[..., None] * (kk_t * a_t)[:, :, None, :]
                 + v_t[..., None] * k_t[:, :, None, :])
        y_t = jnp.einsum('bhij,bhj->bhi', state, r_t)
        return state, y_t

    xs = tuple(jnp.moveaxis(t, 1, 0) for t in (r, w, k, v, kk, a))
    s0 = jnp.zeros((b, h, n, n), jnp.float32)
    _, ys = lax.scan(step, s0, xs)
    return jnp.moveaxis(ys, 0, 1)


def _rwkv7_mixer(h, rp, kp, vp, mu_rkv, mu_wag, w_lora1, w_lora2, w0,
                 a_lora1, a_lora2, a0, g_lora1, g_lora2, k_k, k_a, r_k,
                 ln_x_w, ln_x_b):
    b, s, _ = h.shape
    r = rp + (_shift(rp) - rp) * mu_rkv[0]
    k = kp + (_shift(kp) - kp) * mu_rkv[1]
    v = vp + (_shift(vp) - vp) * mu_rkv[2]
    dh = _shift(h) - h
    xw = h + dh * mu_wag[0]
    xa = h + dh * mu_wag[1]
    xg = h + dh * mu_wag[2]
    w_log = -jax.nn.softplus(-(w0 + jnp.tanh(xw @ w_lora1) @ w_lora2)) - 0.5
    decay = jnp.exp(-jnp.exp(w_log.astype(jnp.float32)))
    a = jax.nn.sigmoid(a0 + (xa @ a_lora1) @ a_lora2)
    g = jax.nn.sigmoid(xg @ g_lora1) @ g_lora2

    def heads(t):
        return t.reshape(b, s, RWKV_HEADS, RWKV_HEAD_DIM).astype(jnp.float32)

    kk = heads(k * k_k)
    kk = kk * lax.rsqrt(jnp.maximum(jnp.sum(kk * kk, axis=-1, keepdims=True), 1e-24))
    k = k * (1 + (a - 1) * k_a)
    rh, kh, vh, ah, wh = heads(r), heads(k), heads(v), heads(a), heads(decay)
    y = _rwkv7_scan(rh, wh, kh, vh, kk, ah)
    mu = jnp.mean(y, axis=-1, keepdims=True)
    var = jnp.mean(jnp.square(y - mu), axis=-1, keepdims=True)
    y = (y - mu) * lax.rsqrt(var + GN_EPS)
    y = (y * ln_x_w.reshape(RWKV_HEADS, RWKV_HEAD_DIM).astype(jnp.float32)
         + ln_x_b.reshape(RWKV_HEADS, RWKV_HEAD_DIM).astype(jnp.float32))
    bonus = jnp.sum(rh * kh * r_k.astype(jnp.float32), axis=-1, keepdims=True) * vh
    return (y + bonus).reshape(b, s, D_RWKV).astype(h.dtype) * g


def _memory_attention(q, mem_n, w_kv_mem):
    b, s, _ = q.shape
    kv = mem_n @ w_kv_mem
    km, vm = jnp.split(kv, 2, axis=-1)
    qh = q.reshape(b, s, MEM_HEADS, MEM_HEAD_DIM).astype(jnp.float32)
    kh = km.reshape(b, -1, MEM_HEADS, MEM_HEAD_DIM).astype(jnp.float32)
    vh = vm.reshape(b, -1, MEM_HEADS, MEM_HEAD_DIM).astype(jnp.float32)
    scores = jnp.einsum('bshd,bmhd->bhsm', qh, kh) * (MEM_HEAD_DIM ** -0.5)
    p = jax.nn.softmax(scores, axis=-1)
    o = jnp.einsum('bhsm,bmhd->bshd', p, vh)
    return o.reshape(b, s, D_MEM).astype(q.dtype)


def _hierarchical_moe(h, w_router_group, b_router_group, w_router_expert,
                      b_router_expert, w_exp_gate, w_exp_up, w_exp_down):
    b, s, d = h.shape
    t = b * s
    ht = h.reshape(t, d)
    tok = jnp.arange(t)
    gp = jax.nn.softmax((ht @ w_router_group + b_router_group).astype(jnp.float32), axis=-1)
    g_sel = jnp.argmax(gp, axis=-1)
    g_w = jnp.max(gp, axis=-1, keepdims=True)
    elog = (ht @ w_router_expert + b_router_expert).astype(jnp.float32)
    elog = elog.reshape(t, N_GROUPS, EXPERTS_PER_GROUP)[tok, g_sel]
    ep = jax.nn.softmax(elog, axis=-1)
    top_p, top_i = lax.top_k(ep, TOP_K)
    comb = (g_w * top_p / jnp.sum(top_p, axis=-1, keepdims=True)).astype(h.dtype)
    expert_idx = g_sel[:, None] * EXPERTS_PER_GROUP + top_i

    n_assign = t * TOP_K
    e_flat = expert_idx.reshape(n_assign)
    tok_flat = jnp.arange(n_assign) // TOP_K
    order = jnp.argsort(e_flat)
    e_sorted = e_flat[order]
    tok_sorted = tok_flat[order]
    counts = jnp.bincount(e_flat, length=N_EXPERTS)
    starts = jnp.cumsum(counts) - counts
    padded = ((counts + ROUTE_BLOCK - 1) // ROUTE_BLOCK) * ROUTE_BLOCK
    pad_end = jnp.cumsum(padded)
    pad_start = pad_end - padded
    dest = pad_start[e_sorted] + (jnp.arange(n_assign) - starts[e_sorted])
    n_blocks = -(-n_assign // ROUTE_BLOCK) + N_EXPERTS
    buf = jnp.zeros((n_blocks * ROUTE_BLOCK, d), h.dtype).at[dest].set(ht[tok_sorted])
    blk_start = jnp.arange(n_blocks) * ROUTE_BLOCK
    blk_expert = jnp.minimum(jnp.searchsorted(pad_end, blk_start, side='right'), N_EXPERTS - 1)

    def run_block(args):
        xb, e = args
        hid = jax.nn.silu(xb @ w_exp_gate[e]) * (xb @ w_exp_up[e])
        return hid @ w_exp_down[e]

    yb = lax.map(run_block, (buf.reshape(n_blocks, ROUTE_BLOCK, d), blk_expert))
    y_rows = yb.reshape(n_blocks * ROUTE_BLOCK, d)[dest]
    w_sorted = comb.reshape(n_assign)[order]
    out = jax.ops.segment_sum(y_rows * w_sorted[:, None], tok_sorted, num_segments=t)
    return out.reshape(b, s, d)


def _hybrid_layer(x, mem, g_mix, g_mem, w_in, conv_w, mu_rkv, mu_wag, w_lora1,
                  w_lora2, w0, a_lora1, a_lora2, a0, g_lora1, g_lora2, k_k, k_a,
                  r_k, ln_x_w, ln_x_b, w_kv_mem, w_branch, w_gate, b_gate, w_o,
                  g_ffn, w_router_group, b_router_group, w_router_expert,
                  b_router_expert, w_exp_gate, w_exp_up, w_exp_down):
    b, s, _ = x.shape
    h = _rmsnorm(x, g_mix)
    bg, cg, u, rp, kp, vp, q = _split_in_proj(h @ w_in)
    y_conv = _short_conv_mixer(bg, cg, u, conv_w)
    y_rwkv = _rwkv7_mixer(h, rp, kp, vp, mu_rkv, mu_wag, w_lora1, w_lora2, w0,
                          a_lora1, a_lora2, a0, g_lora1, g_lora2, k_k, k_a, r_k,
                          ln_x_w, ln_x_b)
    y_mem = _memory_attention(q, _rmsnorm(mem, g_mem), w_kv_mem)
    branches = jnp.stack([y_conv, y_rwkv, y_mem], axis=2)
    proj = jnp.einsum('bsnc,ncd->bsnd', branches, w_branch)
    gates = jax.nn.sigmoid(h @ w_gate + b_gate).reshape(b, s, N_BRANCH, D_MODEL)
    z = jnp.sum(gates * proj, axis=2)
    x = x + z @ w_o
    h2 = _rmsnorm(x, g_ffn)
    return x + _hierarchical_moe(h2, w_router_group, b_router_group, w_router_expert,
                                 b_router_expert, w_exp_gate, w_exp_up, w_exp_down)


def setup_inputs(seed: int = 0) -> dict:
    key = jax.random.key(seed)
    ks = iter(jax.random.split(key, 40))

    def nrm(shape, scale):
        return jax.random.normal(next(ks), shape, jnp.float32) * scale

    def uni(shape, lo, hi):
        return jax.random.uniform(next(ks), shape, jnp.float32, lo, hi)

    L = DEPTH
    return {
        "x": nrm((BATCH, SEQ, D_MODEL), 1.0),
        "mem": nrm((BATCH, MEM_LEN, D_MODEL), 1.0),
        "g_mix": 1.0 + nrm((L, D_MODEL), 0.02),
        "g_mem": 1.0 + nrm((L, D_MODEL), 0.02),
        "w_in": nrm((L, D_MODEL, D_IN_PROJ), D_MODEL ** -0.5),
        "conv_w": nrm((L, D_CONV, CONV_WIDTH), CONV_WIDTH ** -0.5),
        "mu_rkv": uni((L, 3, D_RWKV), 0.0, 1.0),
        "mu_wag": uni((L, 3, D_MODEL), 0.0, 1.0),
        "w_lora1": nrm((L, D_MODEL, DECAY_LORA), D_MODEL ** -0.5),
        "w_lora2": nrm((L, DECAY_LORA, D_RWKV), DECAY_LORA ** -0.5),
        "w0": uni((L, D_RWKV), -2.0, 1.0),
        "a_lora1": nrm((L, D_MODEL, AICL_LORA), D_MODEL ** -0.5),
        "a_lora2": nrm((L, AICL_LORA, D_RWKV), AICL_LORA ** -0.5),
        "a0": uni((L, D_RWKV), -1.0, 1.0),
        "g_lora1": nrm((L, D_MODEL, GATE_LORA), D_MODEL ** -0.5),
        "g_lora2": nrm((L, GATE_LORA, D_RWKV), GATE_LORA ** -0.5),
        "k_k": 0.85 + nrm((L, D_RWKV), 0.05),
        "k_a": 1.0 + nrm((L, D_RWKV), 0.05),
        "r_k": nrm((L, RWKV_HEADS, RWKV_HEAD_DIM), 0.1),
        "ln_x_w": 1.0 + nrm((L, D_RWKV), 0.02),
        "ln_x_b": nrm((L, D_RWKV), 0.02),
        "w_kv_mem": nrm((L, D_MODEL, 2 * D_MEM), D_MODEL ** -0.5),
        "w_branch": nrm((L, N_BRANCH, D_BRANCH, D_MODEL), D_BRANCH ** -0.5),
        "w_gate": nrm((L, D_MODEL, N_BRANCH * D_MODEL), D_MODEL ** -0.5),
        "b_gate": nrm((L, N_BRANCH * D_MODEL), 0.02),
        "w_o": nrm((L, D_MODEL, D_MODEL), D_MODEL ** -0.5),
        "g_ffn": 1.0 + nrm((L, D_MODEL), 0.02),
        "w_router_group": nrm((L, D_MODEL, N_GROUPS), D_MODEL ** -0.5),
        "b_router_group": nrm((L, N_GROUPS), 0.01),
        "w_router_expert": nrm((L, D_MODEL, N_EXPERTS), D_MODEL ** -0.5),
        "b_router_expert": nrm((L, N_EXPERTS), 0.01),
        "w_exp_gate": nrm((L, N_EXPERTS, D_MODEL, D_EXPERT), D_MODEL ** -0.5),
        "w_exp_up": nrm((L, N_EXPERTS, D_MODEL, D_EXPERT), D_MODEL ** -0.5),
        "w_exp_down": nrm((L, N_EXPERTS, D_EXPERT, D_MODEL), D_EXPERT ** -0.5),
        "g_final": 1.0 + nrm((D_MODEL,), 0.02),
    }


def reference(x, mem, g_mix, g_mem, w_in, conv_w, mu_rkv, mu_wag, w_lora1, w_lora2,
              w0, a_lora1, a_lora2, a0, g_lora1, g_lora2, k_k, k_a, r_k, ln_x_w,
              ln_x_b, w_kv_mem, w_branch, w_gate, b_gate, w_o, g_ffn,
              w_router_group, b_router_group, w_router_expert, b_router_expert,
              w_exp_gate, w_exp_up, w_exp_down, g_final):
    for l in range(DEPTH):
        x = _hybrid_layer(x, mem, g_mix[l], g_mem[l], w_in[l], conv_w[l], mu_rkv[l],
                          mu_wag[l], w_lora1[l], w_lora2[l], w0[l], a_lora1[l],
                          a_lora2[l], a0[l], g_lora1[l], g_lora2[l], k_k[l], k_a[l],
                          r_k[l], ln_x_w[l], ln_x_b[l], w_kv_mem[l], w_branch[l],
                          w_gate[l], b_gate[l], w_o[l], g_ffn[l], w_router_group[l],
                          b_router_group[l], w_router_expert[l], b_router_expert[l],
                          w_exp_gate[l], w_exp_up[l], w_exp_down[l])
    return _rmsnorm(x, g_final)
```

```python
import functools

import jax
import jax.numpy as jnp
from jax import lax
from jax.experimental import pallas as pl
from jax.experimental.pallas import tpu as pltpu

F32 = jnp.float32
BF16 = jnp.bfloat16
HIGHEST = lax.Precision.HIGHEST

NORM_EPS = 1e-6
GN_EPS = 64e-5
D_BRANCH = 512
HEAD_DIM = 64
N_HEADS = 8
CHUNK = 64
MEM_HEADS = 4
MEM_HEAD_DIM = 128
N_GROUPS = 8
EXPERTS_PER_GROUP = 8
N_EXPERTS = 64
TOP_K = 2
ROW_BLOCK = 128
LANES = 128
VMEM_LIMIT = 56 * 1024 * 1024

TM_PROLOGUE = 256
TB_RWKV = 256
TM_MERGE = 256
TS_SCATTER = 256
TE_COMBINE = 256


def _bdot(a, b):
    return jnp.dot(a.astype(BF16), b.astype(BF16), preferred_element_type=F32)


def _bdot_nt(a, b):
    return lax.dot_general(a.astype(BF16), b.astype(BF16), (((1,), (1,)), ((), ())),
                           preferred_element_type=F32)


def _split_dot(x, m_bf16):
    hi = x.astype(BF16)
    lo = (x - hi.astype(F32)).astype(BF16)
    return (jnp.dot(hi, m_bf16, preferred_element_type=F32)
            + jnp.dot(lo, m_bf16, preferred_element_type=F32))


def _rms(x, g):
    return x * lax.rsqrt(jnp.mean(x * x, axis=-1, keepdims=True) + NORM_EPS) * g


def _sigmoid(x):
    return 1.0 / (1.0 + jnp.exp(-x))


def _const_spec(shape):
    n = len(shape)
    return pl.BlockSpec(shape, lambda *_: (0,) * n)


def _memkv_kernel(mem_ref, g_ref, w_ref, k_ref, v_ref):
    mn = _rms(mem_ref[0], g_ref[...])
    kv = _bdot(mn, w_ref[...])
    k_ref[0] = kv[:, :D_BRANCH].astype(BF16)
    v_ref[0] = kv[:, D_BRANCH:].astype(BF16)


def _memkv(mem, g_mem, w_kv):
    b, m, d = mem.shape
    return pl.pallas_call(
        _memkv_kernel,
        grid=(b,),
        in_specs=[pl.BlockSpec((1, m, d), lambda i: (i, 0, 0)),
                  _const_spec((1, d)), _const_spec((d, 2 * D_BRANCH))],
        out_specs=[pl.BlockSpec((1, m, D_BRANCH), lambda i: (i, 0, 0)),
                   pl.BlockSpec((1, m, D_BRANCH), lambda i: (i, 0, 0))],
        out_shape=[jax.ShapeDtypeStruct((b, m, D_BRANCH), BF16)] * 2,
        compiler_params=pltpu.CompilerParams(dimension_semantics=("arbitrary",),
                                             vmem_limit_bytes=VMEM_LIMIT),
        name="memkv",
    )(mem, g_mem, w_kv)


def _prologue_kernel(x_ref, gmix_ref, win_ref, convw_ref, murkv_ref, muwag_ref,
                     wl1_ref, wl2_ref, w0_ref, al1_ref, al2_ref, a0_ref, gl1_ref, gl2_ref,
                     kk_ref, ka_ref, seg_ref, km_ref, vm_ref,
                     yconv_ref, ymem_ref, r_ref, k_ref, v_ref, kkn_ref, a_ref, lw_ref, g_ref,
                     prev_h, prev_p, prev_cu):
    tm = x_ref.shape[1]
    db = D_BRANCH

    @pl.when(pl.program_id(1) == 0)
    def _():
        prev_h[...] = jnp.zeros_like(prev_h)
        prev_p[...] = jnp.zeros_like(prev_p)
        prev_cu[...] = jnp.zeros_like(prev_cu)

    rows = lax.broadcasted_iota(jnp.int32, (tm, 1), 0)

    def shift1(u, prev_row):
        return jnp.where(rows == 0, prev_row, pltpu.roll(u, 1, axis=0))

    h = _rms(x_ref[0], gmix_ref[...])
    proj = _bdot(h, win_ref[...])

    bg, cg, u = proj[:, :db], proj[:, db:2 * db], proj[:, 2 * db:3 * db]
    cu = cg * u
    cu1 = shift1(cu, prev_cu[1:2, :])
    cu2 = jnp.where(rows == 0, prev_cu[0:1, :],
                    jnp.where(rows == 1, prev_cu[1:2, :], pltpu.roll(cu, 2, axis=0)))
    conv = cu2 * convw_ref[0:1, :] + cu1 * convw_ref[1:2, :] + cu * convw_ref[2:3, :]
    yconv_ref[0] = (bg * conv).astype(BF16)
    prev_cu[...] = cu[tm - 2:tm, :]

    pr = proj[:, 3 * db:6 * db]
    prs = shift1(pr, prev_p[...])
    mixed = pr + (prs - pr) * murkv_ref[...]
    prev_p[...] = pr[tm - 1:tm, :]
    r, k, v = mixed[:, :db], mixed[:, db:2 * db], mixed[:, 2 * db:]

    dh = shift1(h, prev_h[...]) - h
    prev_h[...] = h[tm - 1:tm, :]
    xw = h + dh * muwag_ref[0:1, :]
    xa = h + dh * muwag_ref[1:2, :]
    xg = h + dh * muwag_ref[2:3, :]
    zz = w0_ref[...] + _bdot(jnp.tanh(_bdot(xw, wl1_ref[...])), wl2_ref[...])
    softplus = jnp.maximum(-zz, 0.0) + jnp.log(1.0 + jnp.exp(-jnp.abs(zz)))
    lw_ref[0] = -jnp.exp(-softplus - 0.5)
    a = _sigmoid(a0_ref[...] + _bdot(_bdot(xa, al1_ref[...]), al2_ref[...]))
    g_ref[0] = _bdot(_sigmoid(_bdot(xg, gl1_ref[...])), gl2_ref[...])

    kk = k * kk_ref[...]
    ss = _split_dot(kk * kk, seg_ref[...])
    kkn_ref[0] = kk * lax.rsqrt(jnp.maximum(ss, 1e-24))
    k_ref[0] = k * (1.0 + (a - 1.0) * ka_ref[...])
    r_ref[0] = r
    v_ref[0] = v
    a_ref[0] = a

    q = proj[:, 6 * db:]
    scale = MEM_HEAD_DIM ** -0.5
    for hh in range(MEM_HEADS):
        sl = slice(hh * MEM_HEAD_DIM, (hh + 1) * MEM_HEAD_DIM)
        s = _bdot_nt(q[:, sl], km_ref[0, :, sl]) * scale
        p = jnp.exp(s - jnp.max(s, axis=-1, keepdims=True))
        o = _bdot(p, vm_ref[0, :, sl]) / jnp.sum(p, axis=-1, keepdims=True)
        ymem_ref[0, :, sl] = o.astype(BF16)


def _prologue(x, km, vm, p):
    b, s, d = x.shape
    tm = TM_PROLOGUE
    db = D_BRANCH
    m = km.shape[1]
    tok = lambda c: pl.BlockSpec((1, tm, c), lambda bi, i: (bi, i, 0))
    consts = [p["g_mix"], p["w_in"], p["conv_w"], p["mu_rkv"], p["mu_wag"],
              p["w_lora1"], p["w_lora2"], p["w0"], p["a_lora1"], p["a_lora2"], p["a0"],
              p["g_lora1"], p["g_lora2"], p["k_k"], p["k_a"], p["seg_ones"]]
    out_shapes = ([jax.ShapeDtypeStruct((b, s, db), BF16)] * 2
                  + [jax.ShapeDtypeStruct((b, s, db), F32)] * 7)
    return pl.pallas_call(
        _prologue_kernel,
        grid=(b, s // tm),
        in_specs=[tok(d)] + [_const_spec(c.shape) for c in consts]
                 + [pl.BlockSpec((1, m, db), lambda bi, i: (bi, 0, 0))] * 2,
        out_specs=[tok(db)] * 9,
        out_shape=out_shapes,
        scratch_shapes=[pltpu.VMEM((1, d), F32), pltpu.VMEM((1, 3 * db), F32),
                        pltpu.VMEM((2, db), F32)],
        compiler_params=pltpu.CompilerParams(dimension_semantics=("arbitrary", "arbitrary"),
                                             vmem_limit_bytes=VMEM_LIMIT),
        name="prologue",
    )(x, *consts, km, vm)


def _rwkv_kernel(r_ref, k_ref, v_ref, kk_ref, a_ref, lw_ref, g_ref, rk_ref, lnw_ref, lnb_ref,
                 tri_ref, seg_ref, out_ref, h_scr, y_scr):
    tb = r_ref.shape[1]
    n = HEAD_DIM
    c_len = CHUNK

    @pl.when(pl.program_id(1) == 0)
    def _():
        h_scr[...] = jnp.zeros_like(h_scr)

    row2 = lax.broadcasted_iota(jnp.int32, (c_len, 2 * c_len), 0)
    col2 = lax.broadcasted_iota(jnp.int32, (c_len, 2 * c_len), 1) & (c_len - 1)
    strict2 = col2 < row2
    incl2 = col2 <= row2
    eye = (lax.broadcasted_iota(jnp.int32, (c_len, n), 0)
           == lax.broadcasted_iota(jnp.int32, (c_len, n), 1)).astype(F32)
    zeros = jnp.zeros((c_len, n), F32)

    def chunk_body(c, carry):
        rows = pl.ds(pl.multiple_of(c * c_len, c_len), c_len)
        r = r_ref[0, rows, :]
        k = k_ref[0, rows, :]
        v = v_ref[0, rows, :]
        kk = kk_ref[0, rows, :]
        a = a_ref[0, rows, :]
        lw = lw_ref[0, rows, :]
        gcum = jnp.dot(tri_ref[...], lw, precision=HIGHEST, preferred_element_type=F32)
        glast = gcum[c_len - 1:c_len, :]
        e_pos = jnp.exp(gcum)
        e_neg = jnp.exp(-gcum)
        p_last = jnp.exp(glast)
        rb = r * e_pos
        ab = -kk * jnp.exp(gcum - lw)
        bb = kk * a * e_neg
        kb = k * e_neg
        bk_t = jnp.concatenate([bb * p_last, kb * p_last], axis=0).T

        for hd in range(N_HEADS):
            ls = slice(hd * n, (hd + 1) * n)
            al, rr, vv = ab[:, ls], rb[:, ls], v[:, ls]
            aa = _bdot_nt(jnp.concatenate([al, rr], axis=0),
                          jnp.concatenate([bb[:, ls], kb[:, ls]], axis=0))
            top = jnp.where(strict2, aa[:c_len], 0.0)
            bot = jnp.where(incl2, aa[c_len:], 0.0)
            a_ab = top[:, :c_len]
            t_inv = eye + a_ab
            x_pow = _bdot(a_ab, a_ab)
            for lvl in range(5):
                if lvl < 4:
                    z = _bdot(jnp.concatenate([t_inv, x_pow], axis=0), x_pow)
                    t_inv = t_inv + z[:c_len]
                    x_pow = z[c_len:]
                else:
                    t_inv = t_inv + _bdot(t_inv, x_pow)
            zero_v = jnp.concatenate([zeros, vv], axis=0)
            av = _bdot(top, zero_v)
            w12 = _bdot(t_inv, jnp.concatenate([al, av], axis=1))
            rhs2 = jnp.concatenate([w12, jnp.concatenate([zeros, vv], axis=1)], axis=0)
            lhs3 = jnp.concatenate([bk_t[hd * n:(hd + 1) * n, :], bot], axis=0)
            z2 = _bdot(lhs3, rhs2)
            mq = z2[:, :n] + jnp.concatenate([zeros, rr], axis=0)
            h_old = h_scr[hd]
            out = _bdot(mq, h_old) + z2[:, n:]
            p_col = jnp.sum(eye * p_last[:, ls], axis=1, keepdims=True)
            h_scr[hd] = p_col * h_old + out[:c_len]
            y_scr[rows, ls] = out[c_len:]
        return carry

    lax.fori_loop(0, tb // c_len, chunk_body, 0)

    y = y_scr[...]
    seg = seg_ref[...]
    inv_n = 1.0 / n
    mu = _split_dot(y, seg) * inv_n
    yc = y - mu
    var = _split_dot(yc * yc, seg) * inv_n
    yn = yc * lax.rsqrt(var + GN_EPS) * lnw_ref[...] + lnb_ref[...]
    r = r_ref[0]
    bonus = _split_dot(r * k_ref[0] * rk_ref[...], seg) * v_ref[0]
    out_ref[0] = ((yn + bonus) * g_ref[0]).astype(BF16)


def _rwkv(r, k, v, kkn, a, lw, g, p):
    b, s, db = r.shape
    tb = TB_RWKV
    tok = pl.BlockSpec((1, tb, db), lambda bi, i: (bi, i, 0))
    consts = [p["r_k"], p["ln_x_w"], p["ln_x_b"], p["tri"], p["seg_ones"]]
    return pl.pallas_call(
        _rwkv_kernel,
        grid=(b, s // tb),
        in_specs=[tok] * 7 + [_const_spec(c.shape) for c in consts],
        out_specs=tok,
        out_shape=jax.ShapeDtypeStruct((b, s, db), BF16),
        scratch_shapes=[pltpu.VMEM((N_HEADS, HEAD_DIM, HEAD_DIM), F32),
                        pltpu.VMEM((tb, db), F32)],
        compiler_params=pltpu.CompilerParams(dimension_semantics=("arbitrary", "arbitrary"),
                                             vmem_limit_bytes=VMEM_LIMIT),
        name="rwkv",
    )(r, k, v, kkn, a, lw, g, *consts)


def _merge_kernel(x_ref, yc_ref, yr_ref, ym_ref, gmix_ref, wgate_ref, bgate_ref, wbr_ref, wo_ref,
                  gffn_ref, wrt_ref, brt_ref, tril_ref,
                  x1_ref, meta_ref, cnt_ref, base_scr):
    tm, d = x_ref.shape

    @pl.when(pl.program_id(0) == 0)
    def _():
        base_scr[...] = jnp.zeros_like(base_scr)

    x = x_ref[...]
    hb = _rms(x, gmix_ref[...]).astype(BF16)
    z = jnp.zeros((tm, d), F32)
    for i, y_ref in enumerate((yc_ref, yr_ref, ym_ref)):
        cs = slice(i * d, (i + 1) * d)
        gate = _sigmoid(jnp.dot(hb, wgate_ref[:, cs], preferred_element_type=F32) + bgate_ref[:, cs])
        z = z + gate * jnp.dot(y_ref[...], wbr_ref[i], preferred_element_type=F32)
    x1 = x + _bdot(z, wo_ref[...])
    x1_ref[...] = x1

    h2 = _rms(x1, gffn_ref[...])
    logits = jnp.dot(h2, wrt_ref[...], precision=HIGHEST, preferred_element_type=F32) + brt_ref[...]
    lane = lax.broadcasted_iota(jnp.int32, (tm, LANES), 1)
    neg = jnp.float32(-jnp.inf)
    big = jnp.int32(1 << 20)
    gmask = (lane >= N_EXPERTS) & (lane < N_EXPERTS + N_GROUPS)
    glv = jnp.where(gmask, logits, neg)
    gmax = jnp.max(glv, axis=-1, keepdims=True)
    g_sel = jnp.min(jnp.where(glv == gmax, lane - N_EXPERTS, big), axis=-1, keepdims=True)
    g_w = 1.0 / jnp.sum(jnp.exp(glv - gmax), axis=-1, keepdims=True)
    emask = (lane < N_EXPERTS) & ((lane >> 3) == g_sel)
    elv = jnp.where(emask, logits, neg)
    emax = jnp.max(elv, axis=-1, keepdims=True)
    esum = jnp.sum(jnp.exp(elv - emax), axis=-1, keepdims=True)
    i1 = jnp.min(jnp.where(elv == emax, lane, big), axis=-1, keepdims=True)
    elv2 = jnp.where(lane == i1, neg, elv)
    m2 = jnp.max(elv2, axis=-1, keepdims=True)
    i2 = jnp.min(jnp.where(elv2 == m2, lane, big), axis=-1, keepdims=True)
    p1 = 1.0 / esum
    p2 = jnp.exp(m2 - emax) / esum
    c1 = g_w * p1 / (p1 + p2)
    c2 = g_w * p2 / (p1 + p2)

    oh1 = lane == i1
    oh2 = lane == i2
    onehot = jnp.where(oh1 | oh2, 1.0, 0.0)
    before = jnp.dot(tril_ref[...], onehot.astype(BF16), preferred_element_type=F32) + base_scr[...]
    rank1 = jnp.sum(jnp.where(oh1, before, 0.0), axis=-1, keepdims=True)
    rank2 = jnp.sum(jnp.where(oh2, before, 0.0), axis=-1, keepdims=True)
    new_base = base_scr[...] + jnp.sum(onehot, axis=0, keepdims=True)
    base_scr[...] = new_base
    cnt_ref[...] = jnp.broadcast_to(new_base, cnt_ref.shape)

    col = lax.broadcasted_iota(jnp.int32, (tm, 8), 1)
    meta = jnp.where(col == 0, i1.astype(F32),
           jnp.where(col == 1, i2.astype(F32),
           jnp.where(col == 2, rank1,
           jnp.where(col == 3, rank2,
           jnp.where(col == 4, c1,
           jnp.where(col == 5, c2, 0.0))))))
    meta_ref[...] = meta


def _merge(x2, yc, yr, ym, p):
    t, d = x2.shape
    tm = TM_MERGE
    db = D_BRANCH
    tok = lambda c: pl.BlockSpec((tm, c), lambda i: (i, 0))
    consts = [p["g_mix"], p["w_gate"], p["b_gate"], p["w_branch"], p["w_o"], p["g_ffn"],
              p["w_router"], p["b_router"], p["tril_strict"]]
    return pl.pallas_call(
        _merge_kernel,
        grid=(t // tm,),
        in_specs=[tok(d), tok(db), tok(db), tok(db)] + [_const_spec(c.shape) for c in consts],
        out_specs=[tok(d), tok(8), _const_spec((8, LANES))],
        out_shape=[jax.ShapeDtypeStruct((t, d), F32), jax.ShapeDtypeStruct((t, 8), F32),
                   jax.ShapeDtypeStruct((8, LANES), F32)],
        scratch_shapes=[pltpu.VMEM((1, LANES), F32)],
        compiler_params=pltpu.CompilerParams(dimension_semantics=("arbitrary",),
                                             vmem_limit_bytes=VMEM_LIMIT),
        name="merge",
    )(x2, yc, yr, ym, *consts)


def _scatter_kernel(dest_ref, x1_ref, gffn_ref, xs_ref, hbuf, sem):
    ts = x1_ref.shape[0]
    s = pl.program_id(0)
    slot = s % 2

    def wait_slot(sl):
        for _ in range(TOP_K):
            pltpu.make_async_copy(hbuf.at[sl], xs_ref.at[pl.ds(0, ts), :], sem.at[sl]).wait()

    @pl.when(s >= 2)
    def _():
        wait_slot(slot)

    hbuf[slot] = _rms(x1_ref[...], gffn_ref[...])

    def issue(t, carry):
        base = (s * ts + t) * TOP_K
        for kslot in range(TOP_K):
            d = dest_ref[base + kslot]
            pltpu.make_async_copy(hbuf.at[slot, pl.ds(t, 1), :], xs_ref.at[pl.ds(d, 1), :],
                                  sem.at[slot]).start()
        return carry

    lax.fori_loop(0, ts, issue, 0)

    @pl.when(s == pl.num_programs(0) - 1)
    def _():
        @pl.when(s >= 1)
        def _():
            wait_slot(1 - slot)
        wait_slot(slot)


def _scatter(dest, x1, g_ffn, n_rows):
    t, d = x1.shape
    ts = TS_SCATTER
    return pl.pallas_call(
        _scatter_kernel,
        grid_spec=pltpu.PrefetchScalarGridSpec(
            num_scalar_prefetch=1,
            grid=(t // ts,),
            in_specs=[pl.BlockSpec((ts, d), lambda i, dest: (i, 0)),
                      pl.BlockSpec((1, d), lambda i, dest: (0, 0))],
            out_specs=pl.BlockSpec(memory_space=pl.ANY),
            scratch_shapes=[pltpu.VMEM((2, ts, d), F32), pltpu.SemaphoreType.DMA((2,))],
        ),
        out_shape=jax.ShapeDtypeStruct((n_rows, d), F32),
        compiler_params=pltpu.CompilerParams(dimension_semantics=("arbitrary",),
                                             vmem_limit_bytes=VMEM_LIMIT),
        name="scatter",
    )(dest, x1, g_ffn)


def _experts_kernel(be_ref, nused_ref, xs_ref, wg_ref, wu_ref, wd_ref, ys_ref, wg_s, wu_s, wd_s):
    i = pl.program_id(0)
    prev = be_ref[jnp.maximum(i - 1, 0)]
    active = i < nused_ref[0]

    @pl.when(active & ((i == 0) | (be_ref[i] != prev)))
    def _():
        wg_s[...] = wg_ref[0].astype(BF16)
        wu_s[...] = wu_ref[0].astype(BF16)
        wd_s[...] = wd_ref[0].astype(BF16)

    @pl.when(active)
    def _():
        xb = xs_ref[...].astype(BF16)
        gate = jnp.dot(xb, wg_s[...], preferred_element_type=F32)
        up = jnp.dot(xb, wu_s[...], preferred_element_type=F32)
        hid = gate * _sigmoid(gate) * up
        ys_ref[...] = jnp.dot(hid.astype(BF16), wd_s[...], preferred_element_type=F32)


def _experts(blk_expert, n_used, xs, w_gate, w_up, w_down):
    n_rows, d = xs.shape
    nb = n_rows // ROW_BLOCK
    de = w_gate.shape[-1]

    def row_map(i, be, nu):
        return (jnp.minimum(i, nu[0] - 1), 0)

    def w_map(i, be, nu):
        return (be[jnp.minimum(i, nu[0] - 1)], 0, 0)

    return pl.pallas_call(
        _experts_kernel,
        grid_spec=pltpu.PrefetchScalarGridSpec(
            num_scalar_prefetch=2,
            grid=(nb,),
            in_specs=[pl.BlockSpec((ROW_BLOCK, d), row_map),
                      pl.BlockSpec((1, d, de), w_map),
                      pl.BlockSpec((1, d, de), w_map),
                      pl.BlockSpec((1, de, d), w_map)],
            out_specs=pl.BlockSpec((ROW_BLOCK, d), row_map),
            scratch_shapes=[pltpu.VMEM((d, de), BF16), pltpu.VMEM((d, de), BF16),
                            pltpu.VMEM((de, d), BF16)],
        ),
        out_shape=jax.ShapeDtypeStruct((n_rows, d), F32),
        compiler_params=pltpu.CompilerParams(dimension_semantics=("arbitrary",),
                                             vmem_limit_bytes=VMEM_LIMIT),
        name="experts",
    )(blk_expert, n_used, xs, w_gate, w_up, w_down)


def _combine_kernel(dest_ref, x1_ref, meta_ref, gfin_ref, ys_ref, out_ref, ybuf, sem):
    te = x1_ref.shape[0]
    s = pl.program_id(0)
    nsteps = pl.num_programs(0)
    slot = s % 2

    def issue_step(step, sl):
        def issue(t, carry):
            base = (step * te + t) * TOP_K
            for kslot in range(TOP_K):
                d = dest_ref[base + kslot]
                pltpu.make_async_copy(ys_ref.at[pl.ds(d, 1), :], ybuf.at[sl, kslot, pl.ds(t, 1), :],
                                      sem.at[sl]).start()
            return carry
        lax.fori_loop(0, te, issue, 0)

    @pl.when(s == 0)
    def _():
        issue_step(0, 0)

    @pl.when(s + 1 < nsteps)
    def _():
        issue_step(s + 1, 1 - slot)

    for kslot in range(TOP_K):
        pltpu.make_async_copy(ys_ref.at[pl.ds(0, te), :], ybuf.at[slot, kslot], sem.at[slot]).wait()

    meta = meta_ref[...]
    x2 = x1_ref[...] + ybuf[slot, 0] * meta[:, 4:5] + ybuf[slot, 1] * meta[:, 5:6]
    out_ref[...] = _rms(x2, gfin_ref[...])


def _combine(dest, x1, meta, g_final, ys):
    t, d = x1.shape
    te = TE_COMBINE
    return pl.pallas_call(
        _combine_kernel,
        grid_spec=pltpu.PrefetchScalarGridSpec(
            num_scalar_prefetch=1,
            grid=(t // te,),
            in_specs=[pl.BlockSpec((te, d), lambda i, dest: (i, 0)),
                      pl.BlockSpec((te, 8), lambda i, dest: (i, 0)),
                      pl.BlockSpec((1, d), lambda i, dest: (0, 0)),
                      pl.BlockSpec(memory_space=pl.ANY)],
            out_specs=pl.BlockSpec((te, d), lambda i, dest: (i, 0)),
            scratch_shapes=[pltpu.VMEM((2, TOP_K, te, d), F32), pltpu.SemaphoreType.DMA((2,))],
        ),
        out_shape=jax.ShapeDtypeStruct((t, d), F32),
        compiler_params=pltpu.CompilerParams(dimension_semantics=("arbitrary",),
                                             vmem_limit_bytes=VMEM_LIMIT),
        name="combine",
    )(dest, x1, meta, g_final, ys)


def _constants(tm_merge):
    n = CHUNK
    tri = (jnp.arange(n)[:, None] >= jnp.arange(n)[None, :]).astype(F32)
    head = jnp.arange(D_BRANCH) // HEAD_DIM
    seg_ones = (head[:, None] == head[None, :]).astype(BF16)
    tril_strict = (jnp.arange(tm_merge)[:, None] > jnp.arange(tm_merge)[None, :]).astype(BF16)
    return tri, seg_ones, tril_strict


def kernel(x, mem, g_mix, g_mem, w_in, conv_w, mu_rkv, mu_wag, w_lora1, w_lora2, w0, a_lora1, a_lora2, a0, g_lora1, g_lora2, k_k, k_a, r_k, ln_x_w, ln_x_b, w_kv_mem, w_branch, w_gate, b_gate, w_o, g_ffn, w_router_group, b_router_group, w_router_expert, b_router_expert, w_exp_gate, w_exp_up, w_exp_down, g_final):
    assert g_mix.shape[0] == 1, "single-layer block"
    b, s, d = x.shape
    t = b * s
    db = D_BRANCH
    tri, seg_ones, tril_strict = _constants(TM_MERGE)
    row = lambda a: a.reshape(1, -1)
    pad_r = LANES - N_EXPERTS - N_GROUPS
    p = {
        "g_mix": row(g_mix[0]), "w_in": w_in[0].astype(BF16), "conv_w": conv_w[0].T,
        "mu_rkv": row(mu_rkv[0]), "mu_wag": mu_wag[0],
        "w_lora1": w_lora1[0].astype(BF16), "w_lora2": w_lora2[0].astype(BF16), "w0": row(w0[0]),
        "a_lora1": a_lora1[0].astype(BF16), "a_lora2": a_lora2[0].astype(BF16), "a0": row(a0[0]),
        "g_lora1": g_lora1[0].astype(BF16), "g_lora2": g_lora2[0].astype(BF16),
        "k_k": row(k_k[0]), "k_a": row(k_a[0]), "r_k": row(r_k[0]),
        "ln_x_w": row(ln_x_w[0]), "ln_x_b": row(ln_x_b[0]),
        "w_gate": w_gate[0].astype(BF16), "b_gate": row(b_gate[0]),
        "w_branch": w_branch[0].astype(BF16), "w_o": w_o[0].astype(BF16), "g_ffn": row(g_ffn[0]),
        "w_router": jnp.concatenate([w_router_expert[0], w_router_group[0],
                                     jnp.zeros((d, pad_r), F32)], axis=1),
        "b_router": row(jnp.concatenate([b_router_expert[0], b_router_group[0],
                                         jnp.zeros((pad_r,), F32)])),
        "tri": tri, "seg_ones": seg_ones, "tril_strict": tril_strict,
    }

    km, vm = _memkv(mem, row(g_mem[0]), w_kv_mem[0].astype(BF16))
    yconv, ymem, r, k, v, kkn, a, lw, g = _prologue(x, km, vm, p)
    yrwkv = _rwkv(r, k, v, kkn, a, lw, g, p)
    x1, meta, cnt = _merge(x.reshape(t, d), yconv.reshape(t, db), yrwkv.reshape(t, db),
                           ymem.reshape(t, db), p)

    counts = cnt[0, :N_EXPERTS].astype(jnp.int32)
    padded = ((counts + ROW_BLOCK - 1) // ROW_BLOCK) * ROW_BLOCK
    pad_end = jnp.cumsum(padded)
    pad_start = pad_end - padded
    n_blocks = (t * TOP_K) // ROW_BLOCK + N_EXPERTS
    e_idx = meta[:, 0:TOP_K].astype(jnp.int32)
    onehot = e_idx[:, :, None] == jnp.arange(N_EXPERTS, dtype=jnp.int32)[None, None, :]
    dest = (jnp.sum(jnp.where(onehot, pad_start[None, None, :], 0), axis=-1)
            + meta[:, TOP_K:2 * TOP_K].astype(jnp.int32)).reshape(t * TOP_K)
    blk_start = jnp.arange(n_blocks, dtype=jnp.int32) * ROW_BLOCK
    blk_expert = jnp.minimum(jnp.searchsorted(pad_end, blk_start, side="right"),
                             N_EXPERTS - 1).astype(jnp.int32)
    n_used = (pad_end[-1:] // ROW_BLOCK).astype(jnp.int32)

    xs = _scatter(dest, x1, p["g_ffn"], n_blocks * ROW_BLOCK)
    ys = _experts(blk_expert, n_used, xs, w_exp_gate[0], w_exp_up[0], w_exp_down[0])
    out = _combine(dest, x1, meta, row(g_final), ys)
    return out.reshape(b, s, d)
```

```python
import functools

import jax
import jax.numpy as jnp
from jax import lax
from jax.experimental import pallas as pl
from jax.experimental.pallas import tpu as pltpu

F32 = jnp.float32
BF16 = jnp.bfloat16
HIGHEST = lax.Precision.HIGHEST

NORM_EPS = 1e-6
GN_EPS = 64e-5
D_BRANCH = 512
HEAD_DIM = 64
N_HEADS = 8
CHUNK = 64
MEM_HEADS = 4
MEM_HEAD_DIM = 128
N_GROUPS = 8
EXPERTS_PER_GROUP = 8
N_EXPERTS = 64
TOP_K = 2
ROW_BLOCK = 128
LANES = 128
VMEM_LIMIT = 56 * 1024 * 1024

TM_PROLOGUE = 256
TB_RWKV = 256
TM_MERGE = 256
TS_SCATTER = 256
TE_COMBINE = 256


def _bdot(a, b):
    return jnp.dot(a.astype(BF16), b.astype(BF16), preferred_element_type=F32)


def _bdot_nt(a, b):
    return lax.dot_general(a.astype(BF16), b.astype(BF16), (((1,), (1,)), ((), ())),
                           preferred_element_type=F32)


def _split_dot(x, m_bf16):
    hi = x.astype(BF16)
    lo = (x - hi.astype(F32)).astype(BF16)
    return (jnp.dot(hi, m_bf16, preferred_element_type=F32)
            + jnp.dot(lo, m_bf16, preferred_element_type=F32))


def _rms(x, g):
    return x * lax.rsqrt(jnp.mean(x * x, axis=-1, keepdims=True) + NORM_EPS) * g


def _sigmoid(x):
    return 1.0 / (1.0 + jnp.exp(-x))


def _const_spec(shape):
    n = len(shape)
    return pl.BlockSpec(shape, lambda *_: (0,) * n)


def _memkv_kernel(mem_ref, g_ref, w_ref, k_ref, v_ref):
    mn = _rms(mem_ref[0], g_ref[...])
    kv = _bdot(mn, w_ref[...])
    k_ref[0] = kv[:, :D_BRANCH].astype(BF16)
    v_ref[0] = kv[:, D_BRANCH:].astype(BF16)


def _memkv(mem, g_mem, w_kv):
    b, m, d = mem.shape
    return pl.pallas_call(
        _memkv_kernel,
        grid=(b,),
        in_specs=[pl.BlockSpec((1, m, d), lambda i: (i, 0, 0)),
                  _const_spec((1, d)), _const_spec((d, 2 * D_BRANCH))],
        out_specs=[pl.BlockSpec((1, m, D_BRANCH), lambda i: (i, 0, 0)),
                   pl.BlockSpec((1, m, D_BRANCH), lambda i: (i, 0, 0))],
        out_shape=[jax.ShapeDtypeStruct((b, m, D_BRANCH), BF16)] * 2,
        compiler_params=pltpu.CompilerParams(dimension_semantics=("arbitrary",),
                                             vmem_limit_bytes=VMEM_LIMIT),
        name="memkv",
    )(mem, g_mem, w_kv)


def _prologue_kernel(x_ref, gmix_ref, win_ref, convw_ref, murkv_ref, muwag_ref,
                     wl1_ref, wl2_ref, w0_ref, al1_ref, al2_ref, a0_ref, gl1_ref, gl2_ref,
                     kk_ref, ka_ref, seg_ref, km_ref, vm_ref,
                     yconv_ref, ymem_ref, r_ref, k_ref, v_ref, kkn_ref, a_ref, lw_ref, g_ref,
                     prev_h, prev_p, prev_cu):
    tm = x_ref.shape[1]
    db = D_BRANCH

    @pl.when(pl.program_id(1) == 0)
    def _():
        prev_h[...] = jnp.zeros_like(prev_h)
        prev_p[...] = jnp.zeros_like(prev_p)
        prev_cu[...] = jnp.zeros_like(prev_cu)

    rows = lax.broadcasted_iota(jnp.int32, (tm, 1), 0)

    def shift1(u, prev_row):
        return jnp.where(rows == 0, prev_row, pltpu.roll(u, 1, axis=0))

    h = _rms(x_ref[0], gmix_ref[...])
    proj = _bdot(h, win_ref[...])

    bg, cg, u = proj[:, :db], proj[:, db:2 * db], proj[:, 2 * db:3 * db]
    cu = cg * u
    cu1 = shift1(cu, prev_cu[1:2, :])
    cu2 = jnp.where(rows == 0, prev_cu[0:1, :],
                    jnp.where(rows == 1, prev_cu[1:2, :], pltpu.roll(cu, 2, axis=0)))
    conv = cu2 * convw_ref[0:1, :] + cu1 * convw_ref[1:2, :] + cu * convw_ref[2:3, :]
    yconv_ref[0] = (bg * conv).astype(BF16)
    prev_cu[...] = cu[tm - 2:tm, :]

    pr = proj[:, 3 * db:6 * db]
    prs = shift1(pr, prev_p[...])
    mixed = pr + (prs - pr) * murkv_ref[...]
    prev_p[...] = pr[tm - 1:tm, :]
    r, k, v = mixed[:, :db], mixed[:, db:2 * db], mixed[:, 2 * db:]

    dh = shift1(h, prev_h[...]) - h
    prev_h[...] = h[tm - 1:tm, :]
    xw = h + dh * muwag_ref[0:1, :]
    xa = h + dh * muwag_ref[1:2, :]
    xg = h + dh * muwag_ref[2:3, :]
    zz = w0_ref[...] + _bdot(jnp.tanh(_bdot(xw, wl1_ref[...])), wl2_ref[...])
    softplus = jnp.maximum(-zz, 0.0) + jnp.log(1.0 + jnp.exp(-jnp.abs(zz)))
    lw_ref[0] = -jnp.exp(-softplus - 0.5)
    a = _sigmoid(a0_ref[...] + _bdot(_bdot(xa, al1_ref[...]), al2_ref[...]))
    g_ref[0] = _bdot(_sigmoid(_bdot(xg, gl1_ref[...])), gl2_ref[...])

    kk = k * kk_ref[...]
    ss = _split_dot(kk * kk, seg_ref[...])
    kkn_ref[0] = kk * lax.rsqrt(jnp.maximum(ss, 1e-24))
    k_ref[0] = k * (1.0 + (a - 1.0) * ka_ref[...])
    r_ref[0] = r
    v_ref[0] = v
    a_ref[0] = a

    q = proj[:, 6 * db:]
    scale = MEM_HEAD_DIM ** -0.5
    for hh in range(MEM_HEADS):
        sl = slice(hh * MEM_HEAD_DIM, (hh + 1) * MEM_HEAD_DIM)
        s = _bdot_nt(q[:, sl], km_ref[0, :, sl]) * scale
        p = jnp.exp(s - jnp.max(s, axis=-1, keepdims=True))
        o = _bdot(p, vm_ref[0, :, sl]) / jnp.sum(p, axis=-1, keepdims=True)
        ymem_ref[0, :, sl] = o.astype(BF16)


def _prologue(x, km, vm, p):
    b, s, d = x.shape
    tm = TM_PROLOGUE
    db = D_BRANCH
    m = km.shape[1]
    tok = lambda c: pl.BlockSpec((1, tm, c), lambda bi, i: (bi, i, 0))
    consts = [p["g_mix"], p["w_in"], p["conv_w"], p["mu_rkv"], p["mu_wag"],
              p["w_lora1"], p["w_lora2"], p["w0"], p["a_lora1"], p["a_lora2"], p["a0"],
              p["g_lora1"], p["g_lora2"], p["k_k"], p["k_a"], p["seg_ones"]]
    out_shapes = ([jax.ShapeDtypeStruct((b, s, db), BF16)] * 2
                  + [jax.ShapeDtypeStruct((b, s, db), F32)] * 7)
    return pl.pallas_call(
        _prologue_kernel,
        grid=(b, s // tm),
        in_specs=[tok(d)] + [_const_spec(c.shape) for c in consts]
                 + [pl.BlockSpec((1, m, db), lambda bi, i: (bi, 0, 0))] * 2,
        out_specs=[tok(db)] * 9,
        out_shape=out_shapes,
        scratch_shapes=[pltpu.VMEM((1, d), F32), pltpu.VMEM((1, 3 * db), F32),
                        pltpu.VMEM((2, db), F32)],
        compiler_params=pltpu.CompilerParams(dimension_semantics=("arbitrary", "arbitrary"),
                                             vmem_limit_bytes=VMEM_LIMIT),
        name="prologue",
    )(x, *consts, km, vm)


def _rwkv_kernel(r_ref, k_ref, v_ref, kk_ref, a_ref, lw_ref, g_ref, rk_ref, lnw_ref, lnb_ref,
                 tri_ref, seg_ref, out_ref, h_scr, y_scr):
    tb = r_ref.shape[1]
    n = HEAD_DIM
    c_len = CHUNK

    @pl.when(pl.program_id(1) == 0)
    def _():
        h_scr[...] = jnp.zeros_like(h_scr)

    row2 = lax.broadcasted_iota(jnp.int32, (c_len, 2 * c_len), 0)
    col2 = lax.broadcasted_iota(jnp.int32, (c_len, 2 * c_len), 1) & (c_len - 1)
    strict2 = col2 < row2
    incl2 = col2 <= row2
    eye = (lax.broadcasted_iota(jnp.int32, (c_len, n), 0)
           == lax.broadcasted_iota(jnp.int32, (c_len, n), 1)).astype(F32)
    zeros = jnp.zeros((c_len, n), F32)

    def chunk_body(c, carry):
        rows = pl.ds(pl.multiple_of(c * c_len, c_len), c_len)
        r = r_ref[0, rows, :]
        k = k_ref[0, rows, :]
        v = v_ref[0, rows, :]
        kk = kk_ref[0, rows, :]
        a = a_ref[0, rows, :]
        lw = lw_ref[0, rows, :]
        gcum = jnp.dot(tri_ref[...], lw, precision=HIGHEST, preferred_element_type=F32)
        glast = gcum[c_len - 1:c_len, :]
        e_pos = jnp.exp(gcum)
        e_neg = jnp.exp(-gcum)
        p_last = jnp.exp(glast)
        rb = r * e_pos
        ab = -kk * jnp.exp(gcum - lw)
        bb = kk * a * e_neg
        kb = k * e_neg
        bk_t = jnp.concatenate([bb * p_last, kb * p_last], axis=0).T

        heads = range(N_HEADS)
        ls = [slice(hd * n, (hd + 1) * n) for hd in heads]
        al = [ab[:, ls[hd]] for hd in heads]
        rr = [rb[:, ls[hd]] for hd in heads]
        vv = [v[:, ls[hd]] for hd in heads]
        aa = [_bdot_nt(jnp.concatenate([al[hd], rr[hd]], axis=0),
                       jnp.concatenate([bb[:, ls[hd]], kb[:, ls[hd]]], axis=0)) for hd in heads]
        top = [jnp.where(strict2, aa[hd][:c_len], 0.0) for hd in heads]
        bot = [jnp.where(incl2, aa[hd][c_len:], 0.0) for hd in heads]
        a_ab = [top[hd][:, :c_len] for hd in heads]
        t_inv = [eye + a_ab[hd] for hd in heads]
        x_pow = [_bdot(a_ab[hd], a_ab[hd]) for hd in heads]
        av = [_bdot(top[hd], jnp.concatenate([zeros, vv[hd]], axis=0)) for hd in heads]
        for lvl in range(5):
            if lvl < 4:
                z = [_bdot(jnp.concatenate([t_inv[hd], x_pow[hd]], axis=0), x_pow[hd]) for hd in heads]
                t_inv = [t_inv[hd] + z[hd][:c_len] for hd in heads]
                x_pow = [z[hd][c_len:] for hd in heads]
            else:
                t_inv = [t_inv[hd] + _bdot(t_inv[hd], x_pow[hd]) for hd in heads]
        w12 = [_bdot(t_inv[hd], jnp.concatenate([al[hd], av[hd]], axis=1)) for hd in heads]
        z2 = []
        for hd in heads:
            rhs2 = jnp.concatenate([w12[hd], jnp.concatenate([zeros, vv[hd]], axis=1)], axis=0)
            lhs3 = jnp.concatenate([bk_t[hd * n:(hd + 1) * n, :], bot[hd]], axis=0)
            z2.append(_bdot(lhs3, rhs2))
        for hd in heads:
            mq = z2[hd][:, :n] + jnp.concatenate([zeros, rr[hd]], axis=0)
            h_old = h_scr[hd]
            out = _bdot(mq, h_old) + z2[hd][:, n:]
            p_col = jnp.sum(eye * p_last[:, ls[hd]], axis=1, keepdims=True)
            h_scr[hd] = p_col * h_old + out[:c_len]
            y_scr[rows, ls[hd]] = out[c_len:]
        return carry

    lax.fori_loop(0, tb // c_len, chunk_body, 0)

    y = y_scr[...]
    seg = seg_ref[...]
    inv_n = 1.0 / n
    mu = _split_dot(y, seg) * inv_n
    yc = y - mu
    var = _split_dot(yc * yc, seg) * inv_n
    yn = yc * lax.rsqrt(var + GN_EPS) * lnw_ref[...] + lnb_ref[...]
    r = r_ref[0]
    bonus = _split_dot(r * k_ref[0] * rk_ref[...], seg) * v_ref[0]
    out_ref[0] = ((yn + bonus) * g_ref[0]).astype(BF16)


def _rwkv(r, k, v, kkn, a, lw, g, p):
    b, s, db = r.shape
    tb = TB_RWKV
    tok = pl.BlockSpec((1, tb, db), lambda bi, i: (bi, i, 0))
    consts = [p["r_k"], p["ln_x_w"], p["ln_x_b"], p["tri"], p["seg_ones"]]
    return pl.pallas_call(
        _rwkv_kernel,
        grid=(b, s // tb),
        in_specs=[tok] * 7 + [_const_spec(c.shape) for c in consts],
        out_specs=tok,
        out_shape=jax.ShapeDtypeStruct((b, s, db), BF16),
        scratch_shapes=[pltpu.VMEM((N_HEADS, HEAD_DIM, HEAD_DIM), F32),
                        pltpu.VMEM((tb, db), F32)],
        compiler_params=pltpu.CompilerParams(dimension_semantics=("arbitrary", "arbitrary"),
                                             vmem_limit_bytes=VMEM_LIMIT),
        name="rwkv",
    )(r, k, v, kkn, a, lw, g, *consts)


def _merge_kernel(x_ref, yc_ref, yr_ref, ym_ref, gmix_ref, wgate_ref, bgate_ref, wbr_ref, wo_ref,
                  gffn_ref, wrt_ref, brt_ref, tril_ref,
                  x1_ref, meta_ref, cnt_ref, base_scr):
    tm, d = x_ref.shape

    @pl.when(pl.program_id(0) == 0)
    def _():
        base_scr[...] = jnp.zeros_like(base_scr)

    x = x_ref[...]
    hb = _rms(x, gmix_ref[...]).astype(BF16)
    z = jnp.zeros((tm, d), F32)
    for i, y_ref in enumerate((yc_ref, yr_ref, ym_ref)):
        cs = slice(i * d, (i + 1) * d)
        gate = _sigmoid(jnp.dot(hb, wgate_ref[:, cs], preferred_element_type=F32) + bgate_ref[:, cs])
        z = z + gate * jnp.dot(y_ref[...], wbr_ref[i], preferred_element_type=F32)
    x1 = x + _bdot(z, wo_ref[...])
    x1_ref[...] = x1

    h2 = _rms(x1, gffn_ref[...])
    logits = jnp.dot(h2, wrt_ref[...], precision=HIGHEST, preferred_element_type=F32) + brt_ref[...]
    lane = lax.broadcasted_iota(jnp.int32, (tm, LANES), 1)
    neg = jnp.float32(-jnp.inf)
    big = jnp.int32(1 << 20)
    gmask = (lane >= N_EXPERTS) & (lane < N_EXPERTS + N_GROUPS)
    glv = jnp.where(gmask, logits, neg)
    gmax = jnp.max(glv, axis=-1, keepdims=True)
    g_sel = jnp.min(jnp.where(glv == gmax, lane - N_EXPERTS, big), axis=-1, keepdims=True)
    g_w = 1.0 / jnp.sum(jnp.exp(glv - gmax), axis=-1, keepdims=True)
    emask = (lane < N_EXPERTS) & ((lane >> 3) == g_sel)
    elv = jnp.where(emask, logits, neg)
    emax = jnp.max(elv, axis=-1, keepdims=True)
    esum = jnp.sum(jnp.exp(elv - emax), axis=-1, keepdims=True)
    i1 = jnp.min(jnp.where(elv == emax, lane, big), axis=-1, keepdims=True)
    elv2 = jnp.where(lane == i1, neg, elv)
    m2 = jnp.max(elv2, axis=-1, keepdims=True)
    i2 = jnp.min(jnp.where(elv2 == m2, lane, big), axis=-1, keepdims=True)
    p1 = 1.0 / esum
    p2 = jnp.exp(m2 - emax) / esum
    c1 = g_w * p1 / (p1 + p2)
    c2 = g_w * p2 / (p1 + p2)

    oh1 = lane == i1
    oh2 = lane == i2
    onehot = jnp.where(oh1 | oh2, 1.0, 0.0)
    before = jnp.dot(tril_ref[...], onehot.astype(BF16), preferred_element_type=F32) + base_scr[...]
    rank1 = jnp.sum(jnp.where(oh1, before, 0.0), axis=-1, keepdims=True)
    rank2 = jnp.sum(jnp.where(oh2, before, 0.0), axis=-1, keepdims=True)
    new_base = base_scr[...] + jnp.sum(onehot, axis=0, keepdims=True)
    base_scr[...] = new_base
    cnt_ref[...] = jnp.broadcast_to(new_base, cnt_ref.shape)

    col = lax.broadcasted_iota(jnp.int32, (tm, 8), 1)
    meta = jnp.where(col == 0, i1.astype(F32),
           jnp.where(col == 1, i2.astype(F32),
           jnp.where(col == 2, rank1,
           jnp.where(col == 3, rank2,
           jnp.where(col == 4, c1,
           jnp.where(col == 5, c2, 0.0))))))
    meta_ref[...] = meta


def _merge(x2, yc, yr, ym, p):
    t, d = x2.shape
    tm = TM_MERGE
    db = D_BRANCH
    tok = lambda c: pl.BlockSpec((tm, c), lambda i: (i, 0))
    consts = [p["g_mix"], p["w_gate"], p["b_gate"], p["w_branch"], p["w_o"], p["g_ffn"],
              p["w_router"], p["b_router"], p["tril_strict"]]
    return pl.pallas_call(
        _merge_kernel,
        grid=(t // tm,),
        in_specs=[tok(d), tok(db), tok(db), tok(db)] + [_const_spec(c.shape) for c in consts],
        out_specs=[tok(d), tok(8), _const_spec((8, LANES))],
        out_shape=[jax.ShapeDtypeStruct((t, d), F32), jax.ShapeDtypeStruct((t, 8), F32),
                   jax.ShapeDtypeStruct((8, LANES), F32)],
        scratch_shapes=[pltpu.VMEM((1, LANES), F32)],
        compiler_params=pltpu.CompilerParams(dimension_semantics=("arbitrary",),
                                             vmem_limit_bytes=VMEM_LIMIT),
        name="merge",
    )(x2, yc, yr, ym, *consts)


def _scatter_kernel(dest_ref, x1_ref, gffn_ref, xs_ref, hbuf, sem):
    ts = x1_ref.shape[0]
    s = pl.program_id(0)
    slot = s % 2

    def wait_slot(sl):
        for _ in range(TOP_K):
            pltpu.make_async_copy(hbuf.at[sl], xs_ref.at[pl.ds(0, ts), :], sem.at[sl]).wait()

    @pl.when(s >= 2)
    def _():
        wait_slot(slot)

    hbuf[slot] = _rms(x1_ref[...], gffn_ref[...])

    def issue(t, carry):
        base = (s * ts + t) * TOP_K
        for kslot in range(TOP_K):
            d = dest_ref[base + kslot]
            pltpu.make_async_copy(hbuf.at[slot, pl.ds(t, 1), :], xs_ref.at[pl.ds(d, 1), :],
                                  sem.at[slot]).start()
        return carry

    lax.fori_loop(0, ts, issue, 0)

    @pl.when(s == pl.num_programs(0) - 1)
    def _():
        @pl.when(s >= 1)
        def _():
            wait_slot(1 - slot)
        wait_slot(slot)


def _scatter(dest, x1, g_ffn, n_rows):
    t, d = x1.shape
    ts = TS_SCATTER
    return pl.pallas_call(
        _scatter_kernel,
        grid_spec=pltpu.PrefetchScalarGridSpec(
            num_scalar_prefetch=1,
            grid=(t // ts,),
            in_specs=[pl.BlockSpec((ts, d), lambda i, dest: (i, 0)),
                      pl.BlockSpec((1, d), lambda i, dest: (0, 0))],
            out_specs=pl.BlockSpec(memory_space=pl.ANY),
            scratch_shapes=[pltpu.VMEM((2, ts, d), F32), pltpu.SemaphoreType.DMA((2,))],
        ),
        out_shape=jax.ShapeDtypeStruct((n_rows, d), F32),
        compiler_params=pltpu.CompilerParams(dimension_semantics=("arbitrary",),
                                             vmem_limit_bytes=VMEM_LIMIT),
        name="scatter",
    )(dest, x1, g_ffn)


def _experts_kernel(be_ref, nused_ref, xs_ref, wg_ref, wu_ref, wd_ref, ys_ref, wg_s, wu_s, wd_s):
    i = pl.program_id(0)
    prev = be_ref[jnp.maximum(i - 1, 0)]
    active = i < nused_ref[0]

    @pl.when(active & ((i == 0) | (be_ref[i] != prev)))
    def _():
        wg_s[...] = wg_ref[0].astype(BF16)
        wu_s[...] = wu_ref[0].astype(BF16)
        wd_s[...] = wd_ref[0].astype(BF16)

    @pl.when(active)
    def _():
        xb = xs_ref[...].astype(BF16)
        gate = jnp.dot(xb, wg_s[...], preferred_element_type=F32)
        up = jnp.dot(xb, wu_s[...], preferred_element_type=F32)
        hid = gate * _sigmoid(gate) * up
        ys_ref[...] = jnp.dot(hid.astype(BF16), wd_s[...], preferred_element_type=F32)


def _experts(blk_expert, n_used, xs, w_gate, w_up, w_down):
    n_rows, d = xs.shape
    nb = n_rows // ROW_BLOCK
    de = w_gate.shape[-1]

    def row_map(i, be, nu):
        return (jnp.minimum(i, nu[0] - 1), 0)

    def w_map(i, be, nu):
        return (be[jnp.minimum(i, nu[0] - 1)], 0, 0)

    return pl.pallas_call(
        _experts_kernel,
        grid_spec=pltpu.PrefetchScalarGridSpec(
            num_scalar_prefetch=2,
            grid=(nb,),
            in_specs=[pl.BlockSpec((ROW_BLOCK, d), row_map),
                      pl.BlockSpec((1, d, de), w_map),
                      pl.BlockSpec((1, d, de), w_map),
                      pl.BlockSpec((1, de, d), w_map)],
            out_specs=pl.BlockSpec((ROW_BLOCK, d), row_map),
            scratch_shapes=[pltpu.VMEM((d, de), BF16), pltpu.VMEM((d, de), BF16),
                            pltpu.VMEM((de, d), BF16)],
        ),
        out_shape=jax.ShapeDtypeStruct((n_rows, d), F32),
        compiler_params=pltpu.CompilerParams(dimension_semantics=("arbitrary",),
                                             vmem_limit_bytes=VMEM_LIMIT),
        name="experts",
    )(blk_expert, n_used, xs, w_gate, w_up, w_down)


def _combine_kernel(dest_ref, x1_ref, meta_ref, gfin_ref, ys_ref, out_ref, ybuf, sem):
    te = x1_ref.shape[0]
    s = pl.program_id(0)
    nsteps = pl.num_programs(0)
    slot = s % 2

    def issue_step(step, sl):
        def issue(t, carry):
            base = (step * te + t) * TOP_K
            for kslot in range(TOP_K):
                d = dest_ref[base + kslot]
                pltpu.make_async_copy(ys_ref.at[pl.ds(d, 1), :], ybuf.at[sl, kslot, pl.ds(t, 1), :],
                                      sem.at[sl]).start()
            return carry
        lax.fori_loop(0, te, issue, 0)

    @pl.when(s == 0)
    def _():
        issue_step(0, 0)

    @pl.when(s + 1 < nsteps)
    def _():
        issue_step(s + 1, 1 - slot)

    for kslot in range(TOP_K):
        pltpu.make_async_copy(ys_ref.at[pl.ds(0, te), :], ybuf.at[slot, kslot], sem.at[slot]).wait()

    meta = meta_ref[...]
    x2 = x1_ref[...] + ybuf[slot, 0] * meta[:, 4:5] + ybuf[slot, 1] * meta[:, 5:6]
    out_ref[...] = _rms(x2, gfin_ref[...])


def _combine(dest, x1, meta, g_final, ys):
    t, d = x1.shape
    te = TE_COMBINE
    return pl.pallas_call(
        _combine_kernel,
        grid_spec=pltpu.PrefetchScalarGridSpec(
            num_scalar_prefetch=1,
            grid=(t // te,),
            in_specs=[pl.BlockSpec((te, d), lambda i, dest: (i, 0)),
                      pl.BlockSpec((te, 8), lambda i, dest: (i, 0)),
                      pl.BlockSpec((1, d), lambda i, dest: (0, 0)),
                      pl.BlockSpec(memory_space=pl.ANY)],
            out_specs=pl.BlockSpec((te, d), lambda i, dest: (i, 0)),
            scratch_shapes=[pltpu.VMEM((2, TOP_K, te, d), F32), pltpu.SemaphoreType.DMA((2,))],
        ),
        out_shape=jax.ShapeDtypeStruct((t, d), F32),
        compiler_params=pltpu.CompilerParams(dimension_semantics=("arbitrary",),
                                             vmem_limit_bytes=VMEM_LIMIT),
        name="combine",
    )(dest, x1, meta, g_final, ys)


def _constants(tm_merge):
    n = CHUNK
    tri = (jnp.arange(n)[:, None] >= jnp.arange(n)[None, :]).astype(F32)
    head = jnp.arange(D_BRANCH) // HEAD_DIM
    seg_ones = (head[:, None] == head[None, :]).astype(BF16)
    tril_strict = (jnp.arange(tm_merge)[:, None] > jnp.arange(tm_merge)[None, :]).astype(BF16)
    return tri, seg_ones, tril_strict


def kernel(x, mem, g_mix, g_mem, w_in, conv_w, mu_rkv, mu_wag, w_lora1, w_lora2, w0, a_lora1, a_lora2, a0, g_lora1, g_lora2, k_k, k_a, r_k, ln_x_w, ln_x_b, w_kv_mem, w_branch, w_gate, b_gate, w_o, g_ffn, w_router_group, b_router_group, w_router_expert, b_router_expert, w_exp_gate, w_exp_up, w_exp_down, g_final):
    assert g_mix.shape[0] == 1, "single-layer block"
    b, s, d = x.shape
    t = b * s
    db = D_BRANCH
    tri, seg_ones, tril_strict = _constants(TM_MERGE)
    row = lambda a: a.reshape(1, -1)
    pad_r = LANES - N_EXPERTS - N_GROUPS
    p = {
        "g_mix": row(g_mix[0]), "w_in": w_in[0].astype(BF16), "conv_w": conv_w[0].T,
        "mu_rkv": row(mu_rkv[0]), "mu_wag": mu_wag[0],
        "w_lora1": w_lora1[0].astype(BF16), "w_lora2": w_lora2[0].astype(BF16), "w0": row(w0[0]),
        "a_lora1": a_lora1[0].astype(BF16), "a_lora2": a_lora2[0].astype(BF16), "a0": row(a0[0]),
        "g_lora1": g_lora1[0].astype(BF16), "g_lora2": g_lora2[0].astype(BF16),
        "k_k": row(k_k[0]), "k_a": row(k_a[0]), "r_k": row(r_k[0]),
        "ln_x_w": row(ln_x_w[0]), "ln_x_b": row(ln_x_b[0]),
        "w_gate": w_gate[0].astype(BF16), "b_gate": row(b_gate[0]),
        "w_branch": w_branch[0].astype(BF16), "w_o": w_o[0].astype(BF16), "g_ffn": row(g_ffn[0]),
        "w_router": jnp.concatenate([w_router_expert[0], w_router_group[0],
                                     jnp.zeros((d, pad_r), F32)], axis=1),
        "b_router": row(jnp.concatenate([b_router_expert[0], b_router_group[0],
                                         jnp.zeros((pad_r,), F32)])),
        "tri": tri, "seg_ones": seg_ones, "tril_strict": tril_strict,
    }

    km, vm = _memkv(mem, row(g_mem[0]), w_kv_mem[0].astype(BF16))
    yconv, ymem, r, k, v, kkn, a, lw, g = _prologue(x, km, vm, p)
    yrwkv = _rwkv(r, k, v, kkn, a, lw, g, p)
    x1, meta, cnt = _merge(x.reshape(t, d), yconv.reshape(t, db), yrwkv.reshape(t, db),
                           ymem.reshape(t, db), p)

    counts = cnt[0, :N_EXPERTS].astype(jnp.int32)
    padded = ((counts + ROW_BLOCK - 1) // ROW_BLOCK) * ROW_BLOCK
    pad_end = jnp.cumsum(padded)
    pad_start = pad_end - padded
    n_blocks = (t * TOP_K) // ROW_BLOCK + N_EXPERTS
    e_idx = meta[:, 0:TOP_K].astype(jnp.int32)
    onehot = e_idx[:, :, None] == jnp.arange(N_EXPERTS, dtype=jnp.int32)[None, None, :]
    dest = (jnp.sum(jnp.where(onehot, pad_start[None, None, :], 0), axis=-1)
            + meta[:, TOP_K:2 * TOP_K].astype(jnp.int32)).reshape(t * TOP_K)
    blk_start = jnp.arange(n_blocks, dtype=jnp.int32) * ROW_BLOCK
    blk_expert = jnp.minimum(jnp.sum((pad_end[None, :] <= blk_start[:, None]).astype(jnp.int32), axis=1),
                             N_EXPERTS - 1)
    n_used = (pad_end[-1:] // ROW_BLOCK).astype(jnp.int32)

    xs = _scatter(dest, x1, p["g_ffn"], n_blocks * ROW_BLOCK)
    ys = _experts(blk_expert, n_used, xs, w_exp_gate[0], w_exp_up[0], w_exp_down[0])
    out = _combine(dest, x1, meta, row(g_final), ys)
    return out.reshape(b, s, d)
```

```python
import functools

import jax
import jax.numpy as jnp
from jax import lax
from jax.experimental import pallas as pl
from jax.experimental.pallas import tpu as pltpu

F32 = jnp.float32
BF16 = jnp.bfloat16

NORM_EPS = 1e-6
GN_EPS = 64e-5
D_BRANCH = 512
HEAD_DIM = 64
N_HEADS = 8
CHUNK = 64
MEM_HEADS = 4
MEM_HEAD_DIM = 128
N_GROUPS = 8
EXPERTS_PER_GROUP = 8
N_EXPERTS = 64
TOP_K = 2
ROW_BLOCK = 128
LANES = 128
VMEM_LIMIT = 56 * 1024 * 1024

TM_PROLOGUE = 256
TB_RWKV = 256
TM_MERGE = 256
TS_SCATTER = 256
TE_COMBINE = 256
SUBLANES = 8
ISSUE_UNROLL = 2


def _bdot(a, b):
    return jnp.dot(a.astype(BF16), b.astype(BF16), preferred_element_type=F32)


def _bdot_nt(a, b):
    return lax.dot_general(a.astype(BF16), b.astype(BF16), (((1,), (1,)), ((), ())),
                           preferred_element_type=F32)


def _split_terms(x, n_terms):
    terms = []
    for _ in range(n_terms):
        t = x.astype(BF16)
        terms.append(t)
        x = x - t.astype(F32)
    return terms


def _split_dot_left(m_bf16, x, n_terms):
    return sum(jnp.dot(m_bf16, t, preferred_element_type=F32) for t in _split_terms(x, n_terms))


def _head_sums(x, seg_bf16):
    w = seg_bf16.shape[0]
    terms = _split_terms(x, 2)
    halves = [sum(jnp.dot(t[:, c:c + w], seg_bf16, preferred_element_type=F32) for t in terms)
              for c in range(0, x.shape[1], w)]
    return jnp.concatenate(halves, axis=1)


def _rms(x, g):
    return x * lax.rsqrt(jnp.mean(x * x, axis=-1, keepdims=True) + NORM_EPS) * g


def _sigmoid(x):
    return 1.0 / (1.0 + jnp.exp(-x))


def _const_spec(shape):
    n = len(shape)
    return pl.BlockSpec(shape, lambda *_: (0,) * n)


def _memkv_kernel(mem_ref, g_ref, w_ref, k_ref, v_ref):
    mn = _rms(mem_ref[0], g_ref[...])
    kv = _bdot(mn, w_ref[...])
    k_ref[0] = kv[:, :D_BRANCH].astype(BF16)
    v_ref[0] = kv[:, D_BRANCH:].astype(BF16)


def _memkv(mem, g_mem, w_kv):
    b, m, d = mem.shape
    return pl.pallas_call(
        _memkv_kernel,
        grid=(b,),
        in_specs=[pl.BlockSpec((1, m, d), lambda i: (i, 0, 0)),
                  _const_spec((1, d)), _const_spec((d, 2 * D_BRANCH))],
        out_specs=[pl.BlockSpec((1, m, D_BRANCH), lambda i: (i, 0, 0)),
                   pl.BlockSpec((1, m, D_BRANCH), lambda i: (i, 0, 0))],
        out_shape=[jax.ShapeDtypeStruct((b, m, D_BRANCH), BF16)] * 2,
        compiler_params=pltpu.CompilerParams(dimension_semantics=("arbitrary",),
                                             vmem_limit_bytes=VMEM_LIMIT),
        name="memkv",
    )(mem, g_mem, w_kv)


def _prologue_kernel(x_ref, gmix_ref, win_ref, convw_ref, murkv_ref, muwag_ref,
                     wl1_ref, wl2_ref, w0_ref, al1_ref, al2_ref, a0_ref, gl1_ref, gl2_ref,
                     kk_ref, ka_ref, seg_ref, km_ref, vm_ref,
                     yconv_ref, ymem_ref, r_ref, k_ref, v_ref, kkn_ref, a_ref, lw_ref, g_ref,
                     prev_h, prev_p, prev_cu):
    tm = x_ref.shape[1]
    db = D_BRANCH

    @pl.when(pl.program_id(1) == 0)
    def _():
        prev_h[...] = jnp.zeros_like(prev_h)
        prev_p[...] = jnp.zeros_like(prev_p)
        prev_cu[...] = jnp.zeros_like(prev_cu)

    rows = lax.broadcasted_iota(jnp.int32, (tm, 1), 0)

    def shift1(u, prev_row):
        return jnp.where(rows == 0, prev_row, pltpu.roll(u, 1, axis=0))

    h = _rms(x_ref[0], gmix_ref[...])
    proj = _bdot(h, win_ref[...])

    bg, cg, u = proj[:, :db], proj[:, db:2 * db], proj[:, 2 * db:3 * db]
    cu = cg * u
    cu1 = shift1(cu, prev_cu[1:2, :])
    cu2 = jnp.where(rows == 0, prev_cu[0:1, :],
                    jnp.where(rows == 1, prev_cu[1:2, :], pltpu.roll(cu, 2, axis=0)))
    conv = cu2 * convw_ref[0:1, :] + cu1 * convw_ref[1:2, :] + cu * convw_ref[2:3, :]
    yconv_ref[0] = (bg * conv).astype(BF16)
    prev_cu[...] = cu[tm - 2:tm, :]

    pr = proj[:, 3 * db:6 * db]
    prs = shift1(pr, prev_p[...])
    mixed = pr + (prs - pr) * murkv_ref[...]
    prev_p[...] = pr[tm - 1:tm, :]
    r, k, v = mixed[:, :db], mixed[:, db:2 * db], mixed[:, 2 * db:]

    dh = shift1(h, prev_h[...]) - h
    prev_h[...] = h[tm - 1:tm, :]
    xw = h + dh * muwag_ref[0:1, :]
    xa = h + dh * muwag_ref[1:2, :]
    xg = h + dh * muwag_ref[2:3, :]
    zz = w0_ref[...] + _bdot(jnp.tanh(_bdot(xw, wl1_ref[...])), wl2_ref[...])
    softplus = jnp.maximum(-zz, 0.0) + jnp.log(1.0 + jnp.exp(-jnp.abs(zz)))
    lw_ref[0] = -jnp.exp(-softplus - 0.5)
    a = _sigmoid(a0_ref[...] + _bdot(_bdot(xa, al1_ref[...]), al2_ref[...]))
    g_ref[0] = _bdot(_sigmoid(_bdot(xg, gl1_ref[...])), gl2_ref[...])

    kk = k * kk_ref[...]
    ss = _head_sums(kk * kk, seg_ref[...])
    kkn_ref[0] = kk * lax.rsqrt(jnp.maximum(ss, 1e-24))
    k_ref[0] = k * (1.0 + (a - 1.0) * ka_ref[...])
    r_ref[0] = r
    v_ref[0] = v
    a_ref[0] = a

    q = proj[:, 6 * db:]
    scale = MEM_HEAD_DIM ** -0.5
    for hh in range(MEM_HEADS):
        sl = slice(hh * MEM_HEAD_DIM, (hh + 1) * MEM_HEAD_DIM)
        s = _bdot_nt(q[:, sl], km_ref[0, :, sl]) * scale
        p = jnp.exp(s - jnp.max(s, axis=-1, keepdims=True))
        o = _bdot(p, vm_ref[0, :, sl]) / jnp.sum(p, axis=-1, keepdims=True)
        ymem_ref[0, :, sl] = o.astype(BF16)


def _prologue(x, km, vm, p):
    b, s, d = x.shape
    tm = TM_PROLOGUE
    db = D_BRANCH
    m = km.shape[1]
    tok = lambda c: pl.BlockSpec((1, tm, c), lambda bi, i: (bi, i, 0))
    consts = [p["g_mix"], p["w_in"], p["conv_w"], p["mu_rkv"], p["mu_wag"],
              p["w_lora1"], p["w_lora2"], p["w0"], p["a_lora1"], p["a_lora2"], p["a0"],
              p["g_lora1"], p["g_lora2"], p["k_k"], p["k_a"], p["seg_ones"]]
    out_shapes = ([jax.ShapeDtypeStruct((b, s, db), BF16)] * 2
                  + [jax.ShapeDtypeStruct((b, s, db), F32)] * 7)
    return pl.pallas_call(
        _prologue_kernel,
        grid=(b, s // tm),
        in_specs=[tok(d)] + [_const_spec(c.shape) for c in consts]
                 + [pl.BlockSpec((1, m, db), lambda bi, i: (bi, 0, 0))] * 2,
        out_specs=[tok(db)] * 9,
        out_shape=out_shapes,
        scratch_shapes=[pltpu.VMEM((1, d), F32), pltpu.VMEM((1, 3 * db), F32),
                        pltpu.VMEM((2, db), F32)],
        compiler_params=pltpu.CompilerParams(dimension_semantics=("arbitrary", "arbitrary"),
                                             vmem_limit_bytes=VMEM_LIMIT),
        name="prologue",
    )(x, *consts, km, vm)


def _rwkv_kernel(r_ref, k_ref, v_ref, kk_ref, a_ref, lw_ref, g_ref, rk_ref, lnw_ref, lnb_ref,
                 tri_ref, seg_ref, out_ref, h_scr, y_scr):
    tb = r_ref.shape[1]
    n = HEAD_DIM
    c_len = CHUNK

    @pl.when(pl.program_id(1) == 0)
    def _():
        h_scr[...] = jnp.zeros_like(h_scr)

    row2 = lax.broadcasted_iota(jnp.int32, (c_len, 2 * c_len), 0)
    col2 = lax.broadcasted_iota(jnp.int32, (c_len, 2 * c_len), 1) & (c_len - 1)
    strict2 = col2 < row2
    incl2 = col2 <= row2
    eye = (lax.broadcasted_iota(jnp.int32, (c_len, n), 0)
           == lax.broadcasted_iota(jnp.int32, (c_len, n), 1)).astype(F32)
    zeros = jnp.zeros((c_len, n), F32)

    def chunk_body(c, carry):
        rows = pl.ds(pl.multiple_of(c * c_len, c_len), c_len)
        r = r_ref[0, rows, :]
        k = k_ref[0, rows, :]
        v = v_ref[0, rows, :]
        kk = kk_ref[0, rows, :]
        a = a_ref[0, rows, :]
        lw = lw_ref[0, rows, :]
        gcum = _split_dot_left(tri_ref[...], lw, 3)
        glast = gcum[c_len - 1:c_len, :]
        e_pos = jnp.exp(gcum)
        e_neg = jnp.exp(-gcum)
        p_last = jnp.exp(glast)
        rb = r * e_pos
        ab = -kk * jnp.exp(gcum - lw)
        bb = kk * a * e_neg
        kb = k * e_neg
        bk_t = jnp.concatenate([bb * p_last, kb * p_last], axis=0).T

        heads = range(N_HEADS)
        ls = [slice(hd * n, (hd + 1) * n) for hd in heads]
        al = [ab[:, ls[hd]] for hd in heads]
        rr = [rb[:, ls[hd]] for hd in heads]
        vv = [v[:, ls[hd]] for hd in heads]
        aa = [_bdot_nt(jnp.concatenate([al[hd], rr[hd]], axis=0),
                       jnp.concatenate([bb[:, ls[hd]], kb[:, ls[hd]]], axis=0)) for hd in heads]
        top = [jnp.where(strict2, aa[hd][:c_len], 0.0) for hd in heads]
        bot = [jnp.where(incl2, aa[hd][c_len:], 0.0) for hd in heads]
        a_ab = [top[hd][:, :c_len] for hd in heads]
        t_inv = [eye + a_ab[hd] for hd in heads]
        x_pow = [_bdot(a_ab[hd], a_ab[hd]) for hd in heads]
        av = [_bdot(top[hd], jnp.concatenate([zeros, vv[hd]], axis=0)) for hd in heads]
        for lvl in range(5):
            if lvl < 4:
                z = [_bdot(jnp.concatenate([t_inv[hd], x_pow[hd]], axis=0), x_pow[hd]) for hd in heads]
                t_inv = [t_inv[hd] + z[hd][:c_len] for hd in heads]
                x_pow = [z[hd][c_len:] for hd in heads]
            else:
                t_inv = [t_inv[hd] + _bdot(t_inv[hd], x_pow[hd]) for hd in heads]
        w12 = [_bdot(t_inv[hd], jnp.concatenate([al[hd], av[hd]], axis=1)) for hd in heads]
        z2 = []
        for hd in heads:
            rhs2 = jnp.concatenate([w12[hd], jnp.concatenate([zeros, vv[hd]], axis=1)], axis=0)
            lhs3 = jnp.concatenate([bk_t[hd * n:(hd + 1) * n, :], bot[hd]], axis=0)
            z2.append(_bdot(lhs3, rhs2))
        for hd in heads:
            mq = z2[hd][:, :n] + jnp.concatenate([zeros, rr[hd]], axis=0)
            h_old = h_scr[hd]
            out = _bdot(mq, h_old) + z2[hd][:, n:]
            p_col = jnp.sum(eye * p_last[:, ls[hd]], axis=1, keepdims=True)
            h_scr[hd] = p_col * h_old + out[:c_len]
            y_scr[rows, ls[hd]] = out[c_len:]
        return carry

    lax.fori_loop(0, tb // c_len, chunk_body, 0)

    y = y_scr[...]
    seg = seg_ref[...]
    inv_n = 1.0 / n
    mu = _head_sums(y, seg) * inv_n
    yc = y - mu
    var = _head_sums(yc * yc, seg) * inv_n
    yn = yc * lax.rsqrt(var + GN_EPS) * lnw_ref[...] + lnb_ref[...]
    r = r_ref[0]
    bonus = _head_sums(r * k_ref[0] * rk_ref[...], seg) * v_ref[0]
    out_ref[0] = ((yn + bonus) * g_ref[0]).astype(BF16)


def _rwkv(r, k, v, kkn, a, lw, g, p):
    b, s, db = r.shape
    tb = TB_RWKV
    tok = pl.BlockSpec((1, tb, db), lambda bi, i: (bi, i, 0))
    consts = [p["r_k"], p["ln_x_w"], p["ln_x_b"], p["tri"], p["seg_ones"]]
    return pl.pallas_call(
        _rwkv_kernel,
        grid=(b, s // tb),
        in_specs=[tok] * 7 + [_const_spec(c.shape) for c in consts],
        out_specs=tok,
        out_shape=jax.ShapeDtypeStruct((b, s, db), BF16),
        scratch_shapes=[pltpu.VMEM((N_HEADS, HEAD_DIM, HEAD_DIM), F32),
                        pltpu.VMEM((tb, db), F32)],
        compiler_params=pltpu.CompilerParams(dimension_semantics=("arbitrary", "arbitrary"),
                                             vmem_limit_bytes=VMEM_LIMIT),
        name="rwkv",
    )(r, k, v, kkn, a, lw, g, *consts)


def _merge_kernel(x_ref, yc_ref, yr_ref, ym_ref, gmix_ref, wgate_ref, bgate_ref, wbr_ref, wo_ref,
                  gffn_ref, wrt_ref, brt_ref, tril_ref,
                  x1_ref, meta_ref, cnt_ref, base_scr):
    tm, d = x_ref.shape

    @pl.when(pl.program_id(0) == 0)
    def _():
        base_scr[...] = jnp.zeros_like(base_scr)

    x = x_ref[...]
    hb = _rms(x, gmix_ref[...]).astype(BF16)
    z = jnp.zeros((tm, d), F32)
    for i, y_ref in enumerate((yc_ref, yr_ref, ym_ref)):
        cs = slice(i * d, (i + 1) * d)
        gate = _sigmoid(jnp.dot(hb, wgate_ref[:, cs], preferred_element_type=F32) + bgate_ref[:, cs])
        z = z + gate * jnp.dot(y_ref[...], wbr_ref[i], preferred_element_type=F32)
    x1 = x + _bdot(z, wo_ref[...])
    x1_ref[...] = x1

    h2 = _rms(x1, gffn_ref[...])
    h_hi, h_lo = _split_terms(h2, 2)
    w_hi, w_lo = _split_terms(wrt_ref[...], 2)
    logits = (jnp.dot(h_hi, w_hi, preferred_element_type=F32)
              + (jnp.dot(h_lo, w_hi, preferred_element_type=F32)
                 + jnp.dot(h_hi, w_lo, preferred_element_type=F32))) + brt_ref[...]
    lane = lax.broadcasted_iota(jnp.int32, (tm, LANES), 1)
    neg = jnp.float32(-jnp.inf)
    big = jnp.int32(1 << 20)
    gmask = (lane >= N_EXPERTS) & (lane < N_EXPERTS + N_GROUPS)
    glv = jnp.where(gmask, logits, neg)
    gmax = jnp.max(glv, axis=-1, keepdims=True)
    g_sel = jnp.min(jnp.where(glv == gmax, lane - N_EXPERTS, big), axis=-1, keepdims=True)
    g_w = 1.0 / jnp.sum(jnp.exp(glv - gmax), axis=-1, keepdims=True)
    emask = (lane < N_EXPERTS) & ((lane >> 3) == g_sel)
    elv = jnp.where(emask, logits, neg)
    emax = jnp.max(elv, axis=-1, keepdims=True)
    esum = jnp.sum(jnp.exp(elv - emax), axis=-1, keepdims=True)
    i1 = jnp.min(jnp.where(elv == emax, lane, big), axis=-1, keepdims=True)
    elv2 = jnp.where(lane == i1, neg, elv)
    m2 = jnp.max(elv2, axis=-1, keepdims=True)
    i2 = jnp.min(jnp.where(elv2 == m2, lane, big), axis=-1, keepdims=True)
    p1 = 1.0 / esum
    p2 = jnp.exp(m2 - emax) / esum
    c1 = g_w * p1 / (p1 + p2)
    c2 = g_w * p2 / (p1 + p2)

    oh1 = lane == i1
    oh2 = lane == i2
    onehot = jnp.where(oh1 | oh2, 1.0, 0.0)
    before = jnp.dot(tril_ref[...], onehot.astype(BF16), preferred_element_type=F32) + base_scr[...]
    rank1 = jnp.sum(jnp.where(oh1, before, 0.0), axis=-1, keepdims=True)
    rank2 = jnp.sum(jnp.where(oh2, before, 0.0), axis=-1, keepdims=True)
    new_base = base_scr[...] + jnp.sum(onehot, axis=0, keepdims=True)
    base_scr[...] = new_base
    cnt_ref[...] = jnp.broadcast_to(new_base, cnt_ref.shape)

    col = lax.broadcasted_iota(jnp.int32, (tm, 8), 1)
    meta = jnp.where(col == 0, i1.astype(F32),
           jnp.where(col == 1, i2.astype(F32),
           jnp.where(col == 2, rank1,
           jnp.where(col == 3, rank2,
           jnp.where(col == 4, c1,
           jnp.where(col == 5, c2, 0.0))))))
    meta_ref[...] = meta


def _merge(x2, yc, yr, ym, p):
    t, d = x2.shape
    tm = TM_MERGE
    db = D_BRANCH
    tok = lambda c: pl.BlockSpec((tm, c), lambda i: (i, 0))
    consts = [p["g_mix"], p["w_gate"], p["b_gate"], p["w_branch"], p["w_o"], p["g_ffn"],
              p["w_router"], p["b_router"], p["tril_strict"]]
    return pl.pallas_call(
        _merge_kernel,
        grid=(t // tm,),
        in_specs=[tok(d), tok(db), tok(db), tok(db)] + [_const_spec(c.shape) for c in consts],
        out_specs=[tok(d), tok(8), _const_spec((8, LANES))],
        out_shape=[jax.ShapeDtypeStruct((t, d), F32), jax.ShapeDtypeStruct((t, 8), F32),
                   jax.ShapeDtypeStruct((8, LANES), F32)],
        scratch_shapes=[pltpu.VMEM((1, LANES), F32)],
        compiler_params=pltpu.CompilerParams(dimension_semantics=("arbitrary",),
                                             vmem_limit_bytes=VMEM_LIMIT),
        name="merge",
    )(x2, yc, yr, ym, *consts)


def _scatter_kernel(dest_ref, x1_ref, gffn_ref, xs_ref, hbuf, sem):
    ts = x1_ref.shape[0]
    s = pl.program_id(0)
    slot = s % 2

    def wait_slot(sl):
        for _ in range(TOP_K * ts // SUBLANES):
            pltpu.make_async_copy(hbuf.at[sl, 0], xs_ref.at[pl.ds(0, SUBLANES), :], sem.at[sl]).wait()

    @pl.when(s >= 2)
    def _():
        wait_slot(slot)

    d_model = x1_ref.shape[1]
    hbuf[slot] = _rms(x1_ref[...], gffn_ref[...]).reshape(ts // SUBLANES, SUBLANES, d_model)

    def issue(grp, carry):
        for j in range(SUBLANES):
            base = (s * ts + grp * SUBLANES + j) * TOP_K
            for kslot in range(TOP_K):
                d = dest_ref[base + kslot]
                pltpu.make_async_copy(hbuf.at[slot, grp, pl.ds(j, 1), :], xs_ref.at[pl.ds(d, 1), :],
                                      sem.at[slot]).start(priority=kslot)
        return carry

    lax.fori_loop(0, ts // SUBLANES, issue, 0, unroll=ISSUE_UNROLL)

    @pl.when(s == pl.num_programs(0) - 1)
    def _():
        @pl.when(s >= 1)
        def _():
            wait_slot(1 - slot)
        wait_slot(slot)


def _scatter(dest, x1, g_ffn, n_rows):
    t, d = x1.shape
    ts = TS_SCATTER
    return pl.pallas_call(
        _scatter_kernel,
        grid_spec=pltpu.PrefetchScalarGridSpec(
            num_scalar_prefetch=1,
            grid=(t // ts,),
            in_specs=[pl.BlockSpec((ts, d), lambda i, dest: (i, 0)),
                      pl.BlockSpec((1, d), lambda i, dest: (0, 0))],
            out_specs=pl.BlockSpec(memory_space=pl.ANY),
            scratch_shapes=[pltpu.VMEM((2, ts // SUBLANES, SUBLANES, d), F32),
                            pltpu.SemaphoreType.DMA((2,))],
        ),
        out_shape=jax.ShapeDtypeStruct((n_rows, d), F32),
        compiler_params=pltpu.CompilerParams(dimension_semantics=("arbitrary",),
                                             vmem_limit_bytes=VMEM_LIMIT),
        name="scatter",
    )(dest, x1, g_ffn)


def _experts_kernel(be_ref, nused_ref, nexte_ref, xs_ref, wg_hbm, wu_hbm, wd_hbm, ys_ref,
                    wg_f, wu_f, wd_f, wg_s, wu_s, wd_s, sem):
    i = pl.program_id(0)
    e = be_ref[i]
    prev = be_ref[jnp.maximum(i - 1, 0)]
    active = i < nused_ref[0]

    def weight_copies(ex):
        return (pltpu.make_async_copy(wg_hbm.at[ex], wg_f, sem.at[0]),
                pltpu.make_async_copy(wu_hbm.at[ex], wu_f, sem.at[1]),
                pltpu.make_async_copy(wd_hbm.at[ex], wd_f, sem.at[2]))

    @pl.when(i == 0)
    def _():
        for cp in weight_copies(e):
            cp.start()

    @pl.when(active & ((i == 0) | (e != prev)))
    def _():
        for cp in weight_copies(e):
            cp.wait()
        wg_s[...] = wg_f[...].astype(BF16)
        wu_s[...] = wu_f[...].astype(BF16)
        wd_s[...] = wd_f[...].astype(BF16)
        nxt = nexte_ref[e]

        @pl.when(nxt < N_EXPERTS)
        def _():
            for cp in weight_copies(nxt):
                cp.start()

    @pl.when(active)
    def _():
        xb = xs_ref[...].astype(BF16)
        gate = jnp.dot(xb, wg_s[...], preferred_element_type=F32)
        up = jnp.dot(xb, wu_s[...], preferred_element_type=F32)
        hid = gate * _sigmoid(gate) * up
        ys_ref[...] = jnp.dot(hid.astype(BF16), wd_s[...], preferred_element_type=F32)


def _experts(blk_expert, n_used, next_expert, xs, w_gate, w_up, w_down):
    n_rows, d = xs.shape
    nb = n_rows // ROW_BLOCK
    de = w_gate.shape[-1]

    def row_map(i, be, nu, ne):
        return (jnp.minimum(i, nu[0] - 1), 0)

    return pl.pallas_call(
        _experts_kernel,
        grid_spec=pltpu.PrefetchScalarGridSpec(
            num_scalar_prefetch=3,
            grid=(nb,),
            in_specs=[pl.BlockSpec((ROW_BLOCK, d), row_map)] + [pl.BlockSpec(memory_space=pl.ANY)] * 3,
            out_specs=pl.BlockSpec((ROW_BLOCK, d), row_map),
            scratch_shapes=[pltpu.VMEM((d, de), F32), pltpu.VMEM((d, de), F32), pltpu.VMEM((de, d), F32),
                            pltpu.VMEM((d, de), BF16), pltpu.VMEM((d, de), BF16), pltpu.VMEM((de, d), BF16),
                            pltpu.SemaphoreType.DMA((3,))],
        ),
        out_shape=jax.ShapeDtypeStruct((n_rows, d), F32),
        compiler_params=pltpu.CompilerParams(dimension_semantics=("arbitrary",),
                                             vmem_limit_bytes=VMEM_LIMIT),
        name="experts",
    )(blk_expert, n_used, next_expert, xs, w_gate, w_up, w_down)


def _combine_kernel(dest_ref, x1_ref, meta_ref, gfin_ref, ys_ref, out_ref, ybuf, sem):
    te = x1_ref.shape[0]
    s = pl.program_id(0)
    nsteps = pl.num_programs(0)
    slot = s % 2

    d_model = x1_ref.shape[1]

    def issue_step(step, sl):
        def issue(grp, carry):
            for j in range(SUBLANES):
                base = (step * te + grp * SUBLANES + j) * TOP_K
                for kslot in range(TOP_K):
                    d = dest_ref[base + kslot]
                    pltpu.make_async_copy(ys_ref.at[pl.ds(d, 1), :],
                                          ybuf.at[sl, kslot, grp, pl.ds(j, 1), :],
                                          sem.at[sl]).start(priority=kslot)
            return carry
        lax.fori_loop(0, te // SUBLANES, issue, 0, unroll=ISSUE_UNROLL)

    @pl.when(s == 0)
    def _():
        issue_step(0, 0)

    @pl.when(s + 1 < nsteps)
    def _():
        issue_step(s + 1, 1 - slot)

    for _ in range(TOP_K * te // SUBLANES):
        pltpu.make_async_copy(ys_ref.at[pl.ds(0, SUBLANES), :], ybuf.at[slot, 0, 0], sem.at[slot]).wait()

    meta = meta_ref[...]
    y0 = ybuf[slot, 0].reshape(te, d_model)
    y1 = ybuf[slot, 1].reshape(te, d_model)
    x2 = x1_ref[...] + y0 * meta[:, 4:5] + y1 * meta[:, 5:6]
    out_ref[...] = _rms(x2, gfin_ref[...])


def _combine(dest, x1, meta, g_final, ys):
    t, d = x1.shape
    te = TE_COMBINE
    return pl.pallas_call(
        _combine_kernel,
        grid_spec=pltpu.PrefetchScalarGridSpec(
            num_scalar_prefetch=1,
            grid=(t // te,),
            in_specs=[pl.BlockSpec((te, d), lambda i, dest: (i, 0)),
                      pl.BlockSpec((te, 8), lambda i, dest: (i, 0)),
                      pl.BlockSpec((1, d), lambda i, dest: (0, 0)),
                      pl.BlockSpec(memory_space=pl.ANY)],
            out_specs=pl.BlockSpec((te, d), lambda i, dest: (i, 0)),
            scratch_shapes=[pltpu.VMEM((2, TOP_K, te // SUBLANES, SUBLANES, d), F32),
                            pltpu.SemaphoreType.DMA((2,))],
        ),
        out_shape=jax.ShapeDtypeStruct((t, d), F32),
        compiler_params=pltpu.CompilerParams(dimension_semantics=("arbitrary",),
                                             vmem_limit_bytes=VMEM_LIMIT),
        name="combine",
    )(dest, x1, meta, g_final, ys)


def _constants(tm_merge):
    n = CHUNK
    tri = (jnp.arange(n)[:, None] >= jnp.arange(n)[None, :]).astype(BF16)
    head = jnp.arange(2 * LANES) // HEAD_DIM
    seg_ones = (head[:, None] == head[None, :]).astype(BF16)
    tril_strict = (jnp.arange(tm_merge)[:, None] > jnp.arange(tm_merge)[None, :]).astype(BF16)
    return tri, seg_ones, tril_strict


def kernel(x, mem, g_mix, g_mem, w_in, conv_w, mu_rkv, mu_wag, w_lora1, w_lora2, w0, a_lora1, a_lora2, a0, g_lora1, g_lora2, k_k, k_a, r_k, ln_x_w, ln_x_b, w_kv_mem, w_branch, w_gate, b_gate, w_o, g_ffn, w_router_group, b_router_group, w_router_expert, b_router_expert, w_exp_gate, w_exp_up, w_exp_down, g_final):
    assert g_mix.shape[0] == 1, "single-layer block"
    b, s, d = x.shape
    t = b * s
    db = D_BRANCH
    tri, seg_ones, tril_strict = _constants(TM_MERGE)
    row = lambda a: a.reshape(1, -1)
    pad_r = LANES - N_EXPERTS - N_GROUPS
    p = {
        "g_mix": row(g_mix[0]), "w_in": w_in[0].astype(BF16), "conv_w": conv_w[0].T,
        "mu_rkv": row(mu_rkv[0]), "mu_wag": mu_wag[0],
        "w_lora1": w_lora1[0].astype(BF16), "w_lora2": w_lora2[0].astype(BF16), "w0": row(w0[0]),
        "a_lora1": a_lora1[0].astype(BF16), "a_lora2": a_lora2[0].astype(BF16), "a0": row(a0[0]),
        "g_lora1": g_lora1[0].astype(BF16), "g_lora2": g_lora2[0].astype(BF16),
        "k_k": row(k_k[0]), "k_a": row(k_a[0]), "r_k": row(r_k[0]),
        "ln_x_w": row(ln_x_w[0]), "ln_x_b": row(ln_x_b[0]),
        "w_gate": w_gate[0].astype(BF16), "b_gate": row(b_gate[0]),
        "w_branch": w_branch[0].astype(BF16), "w_o": w_o[0].astype(BF16), "g_ffn": row(g_ffn[0]),
        "w_router": jnp.concatenate([w_router_expert[0], w_router_group[0],
                                     jnp.zeros((d, pad_r), F32)], axis=1),
        "b_router": row(jnp.concatenate([b_router_expert[0], b_router_group[0],
                                         jnp.zeros((pad_r,), F32)])),
        "tri": tri, "seg_ones": seg_ones, "tril_strict": tril_strict,
    }

    km, vm = _memkv(mem, row(g_mem[0]), w_kv_mem[0].astype(BF16))
    yconv, ymem, r, k, v, kkn, a, lw, g = _prologue(x, km, vm, p)
    yrwkv = _rwkv(r, k, v, kkn, a, lw, g, p)
    x1, meta, cnt = _merge(x.reshape(t, d), yconv.reshape(t, db), yrwkv.reshape(t, db),
                           ymem.reshape(t, db), p)

    counts = cnt[0, :N_EXPERTS].astype(jnp.int32)
    padded = ((counts + ROW_BLOCK - 1) // ROW_BLOCK) * ROW_BLOCK
    pad_end = jnp.cumsum(padded)
    pad_start = pad_end - padded
    n_blocks = (t * TOP_K) // ROW_BLOCK + N_EXPERTS
    e_idx = meta[:, 0:TOP_K].astype(jnp.int32)
    onehot = e_idx[:, :, None] == jnp.arange(N_EXPERTS, dtype=jnp.int32)[None, None, :]
    dest = (jnp.sum(jnp.where(onehot, pad_start[None, None, :], 0), axis=-1)
            + meta[:, TOP_K:2 * TOP_K].astype(jnp.int32)).reshape(t * TOP_K)
    blk_start = jnp.arange(n_blocks, dtype=jnp.int32) * ROW_BLOCK
    blk_expert = jnp.minimum(jnp.sum((pad_end[None, :] <= blk_start[:, None]).astype(jnp.int32), axis=1),
                             N_EXPERTS - 1)
    n_used = (pad_end[-1:] // ROW_BLOCK).astype(jnp.int32)
    eids = jnp.arange(N_EXPERTS, dtype=jnp.int32)
    later_nonempty = (eids[None, :] > eids[:, None]) & (counts[None, :] > 0)
    next_expert = jnp.min(jnp.where(later_nonempty, eids[None, :], N_EXPERTS), axis=1)

    xs = _scatter(dest, x1, p["g_ffn"], n_blocks * ROW_BLOCK)
    ys = _experts(blk_expert, n_used, next_expert, xs, w_exp_gate[0], w_exp_up[0], w_exp_down[0])
    out = _combine(dest, x1, meta, row(g_final), ys)
    return out.reshape(b, s, d)
```

```python
import functools

import jax
import jax.numpy as jnp
from jax import lax
from jax.experimental import pallas as pl
from jax.experimental.pallas import tpu as pltpu

F32 = jnp.float32
BF16 = jnp.bfloat16

NORM_EPS = 1e-6
GN_EPS = 64e-5
D_BRANCH = 512
HEAD_DIM = 64
N_HEADS = 8
CHUNK = 64
CHUNKS_PER_ITER = 4
MEM_HEADS = 4
MEM_HEAD_DIM = 128
N_GROUPS = 8
EXPERTS_PER_GROUP = 8
N_EXPERTS = 64
TOP_K = 2
ROW_BLOCK = 128
LANES = 128
VMEM_LIMIT = 56 * 1024 * 1024

TM_PROLOGUE = 512
TB_RWKV = 256
TM_MERGE = 512
TS_SCATTER = 256
TE_COMBINE = 256
SUBLANES = 8
ISSUE_UNROLL = 2


def _bdot(a, b):
    return jnp.dot(a.astype(BF16), b.astype(BF16), preferred_element_type=F32)


def _bdot_nt(a, b):
    return lax.dot_general(a.astype(BF16), b.astype(BF16), (((1,), (1,)), ((), ())),
                           preferred_element_type=F32)


def _split_terms(x, n_terms):
    terms = []
    for _ in range(n_terms):
        t = x.astype(BF16)
        terms.append(t)
        x = x - t.astype(F32)
    return terms


def _split_dot_left(m_bf16, x, n_terms):
    return sum(jnp.dot(m_bf16, t, preferred_element_type=F32) for t in _split_terms(x, n_terms))


def _head_sums(x, seg_bf16):
    w = seg_bf16.shape[0]
    terms = _split_terms(x, 2)
    halves = [sum(jnp.dot(t[:, c:c + w], seg_bf16, preferred_element_type=F32) for t in terms)
              for c in range(0, x.shape[1], w)]
    return jnp.concatenate(halves, axis=1)


def _rms(x, g):
    return x * lax.rsqrt(jnp.mean(x * x, axis=-1, keepdims=True) + NORM_EPS) * g


def _sigmoid(x):
    return 1.0 / (1.0 + jnp.exp(-x))


def _const_spec(shape):
    n = len(shape)
    return pl.BlockSpec(shape, lambda *_: (0,) * n)


def _memkv_kernel(mem_ref, g_ref, w_ref, k_ref, v_ref):
    mn = _rms(mem_ref[0], g_ref[...])
    kv = _bdot(mn, w_ref[...])
    k_ref[0] = kv[:, :D_BRANCH].astype(BF16)
    v_ref[0] = kv[:, D_BRANCH:].astype(BF16)


def _memkv(mem, g_mem, w_kv):
    b, m, d = mem.shape
    return pl.pallas_call(
        _memkv_kernel,
        grid=(b,),
        in_specs=[pl.BlockSpec((1, m, d), lambda i: (i, 0, 0)),
                  _const_spec((1, d)), _const_spec((d, 2 * D_BRANCH))],
        out_specs=[pl.BlockSpec((1, m, D_BRANCH), lambda i: (i, 0, 0)),
                   pl.BlockSpec((1, m, D_BRANCH), lambda i: (i, 0, 0))],
        out_shape=[jax.ShapeDtypeStruct((b, m, D_BRANCH), BF16)] * 2,
        compiler_params=pltpu.CompilerParams(dimension_semantics=("arbitrary",),
                                             vmem_limit_bytes=VMEM_LIMIT),
        name="memkv",
    )(mem, g_mem, w_kv)


def _prologue_kernel(x_ref, gmix_ref, win_ref, convw_ref, murkv_ref, muwag_ref,
                     wl1_ref, wl2_ref, w0_ref, al1_ref, al2_ref, a0_ref, gl1_ref, gl2_ref,
                     kk_ref, ka_ref, seg_ref, km_ref, vm_ref,
                     yconv_ref, ymem_ref, r_ref, k_ref, v_ref, kkn_ref, a_ref, lw_ref, g_ref,
                     prev_h, prev_p, prev_cu):
    tm = x_ref.shape[1]
    db = D_BRANCH

    @pl.when(pl.program_id(1) == 0)
    def _():
        prev_h[...] = jnp.zeros_like(prev_h)
        prev_p[...] = jnp.zeros_like(prev_p)
        prev_cu[...] = jnp.zeros_like(prev_cu)

    rows = lax.broadcasted_iota(jnp.int32, (tm, 1), 0)

    def shift1(u, prev_row):
        return jnp.where(rows == 0, prev_row, pltpu.roll(u, 1, axis=0))

    h = _rms(x_ref[0], gmix_ref[...])
    proj = _bdot(h, win_ref[...])

    bg, cg, u = proj[:, :db], proj[:, db:2 * db], proj[:, 2 * db:3 * db]
    cu = cg * u
    cu1 = shift1(cu, prev_cu[1:2, :])
    cu2 = jnp.where(rows == 0, prev_cu[0:1, :],
                    jnp.where(rows == 1, prev_cu[1:2, :], pltpu.roll(cu, 2, axis=0)))
    conv = cu2 * convw_ref[0:1, :] + cu1 * convw_ref[1:2, :] + cu * convw_ref[2:3, :]
    yconv_ref[0] = (bg * conv).astype(BF16)
    prev_cu[...] = cu[tm - 2:tm, :]

    pr = proj[:, 3 * db:6 * db]
    prs = shift1(pr, prev_p[...])
    mixed = pr + (prs - pr) * murkv_ref[...]
    prev_p[...] = pr[tm - 1:tm, :]
    r, k, v = mixed[:, :db], mixed[:, db:2 * db], mixed[:, 2 * db:]

    dh = shift1(h, prev_h[...]) - h
    prev_h[...] = h[tm - 1:tm, :]
    xw = h + dh * muwag_ref[0:1, :]
    xa = h + dh * muwag_ref[1:2, :]
    xg = h + dh * muwag_ref[2:3, :]
    zz = w0_ref[...] + _bdot(jnp.tanh(_bdot(xw, wl1_ref[...])), wl2_ref[...])
    softplus = jnp.maximum(-zz, 0.0) + jnp.log(1.0 + jnp.exp(-jnp.abs(zz)))
    lw_ref[0] = -jnp.exp(-softplus - 0.5)
    a = _sigmoid(a0_ref[...] + _bdot(_bdot(xa, al1_ref[...]), al2_ref[...]))
    g_ref[0] = _bdot(_sigmoid(_bdot(xg, gl1_ref[...])), gl2_ref[...])

    kk = k * kk_ref[...]
    ss = _head_sums(kk * kk, seg_ref[...])
    kkn_ref[0] = kk * lax.rsqrt(jnp.maximum(ss, 1e-24))
    k_ref[0] = k * (1.0 + (a - 1.0) * ka_ref[...])
    r_ref[0] = r
    v_ref[0] = v
    a_ref[0] = a

    q = proj[:, 6 * db:]
    scale = MEM_HEAD_DIM ** -0.5
    for hh in range(MEM_HEADS):
        sl = slice(hh * MEM_HEAD_DIM, (hh + 1) * MEM_HEAD_DIM)
        s = _bdot_nt(q[:, sl], km_ref[0, :, sl]) * scale
        p = jnp.exp(s - jnp.max(s, axis=-1, keepdims=True))
        o = _bdot(p, vm_ref[0, :, sl]) / jnp.sum(p, axis=-1, keepdims=True)
        ymem_ref[0, :, sl] = o.astype(BF16)


def _prologue(x, km, vm, p):
    b, s, d = x.shape
    tm = TM_PROLOGUE
    db = D_BRANCH
    m = km.shape[1]
    tok = lambda c: pl.BlockSpec((1, tm, c), lambda bi, i: (bi, i, 0))
    consts = [p["g_mix"], p["w_in"], p["conv_w"], p["mu_rkv"], p["mu_wag"],
              p["w_lora1"], p["w_lora2"], p["w0"], p["a_lora1"], p["a_lora2"], p["a0"],
              p["g_lora1"], p["g_lora2"], p["k_k"], p["k_a"], p["seg_ones"]]
    out_shapes = ([jax.ShapeDtypeStruct((b, s, db), BF16)] * 2
                  + [jax.ShapeDtypeStruct((b, s, db), F32)] * 7)
    return pl.pallas_call(
        _prologue_kernel,
        grid=(b, s // tm),
        in_specs=[tok(d)] + [_const_spec(c.shape) for c in consts]
                 + [pl.BlockSpec((1, m, db), lambda bi, i: (bi, 0, 0))] * 2,
        out_specs=[tok(db)] * 9,
        out_shape=out_shapes,
        scratch_shapes=[pltpu.VMEM((1, d), F32), pltpu.VMEM((1, 3 * db), F32),
                        pltpu.VMEM((2, db), F32)],
        compiler_params=pltpu.CompilerParams(dimension_semantics=("arbitrary", "arbitrary"),
                                             vmem_limit_bytes=VMEM_LIMIT),
        name="prologue",
    )(x, *consts, km, vm)


def _rwkv_kernel(r_ref, k_ref, v_ref, kk_ref, a_ref, lw_ref, g_ref, rk_ref, lnw_ref, lnb_ref,
                 tri_ref, seg_ref, out_ref, h_scr, y_scr):
    tb = r_ref.shape[1]
    n = HEAD_DIM
    c_len = CHUNK

    @pl.when(pl.program_id(1) == 0)
    def _():
        h_scr[...] = jnp.zeros_like(h_scr)

    row2 = lax.broadcasted_iota(jnp.int32, (c_len, 2 * c_len), 0)
    col2 = lax.broadcasted_iota(jnp.int32, (c_len, 2 * c_len), 1) & (c_len - 1)
    strict2 = col2 < row2
    incl2 = col2 <= row2
    eye = (lax.broadcasted_iota(jnp.int32, (c_len, n), 0)
           == lax.broadcasted_iota(jnp.int32, (c_len, n), 1)).astype(F32)
    zeros = jnp.zeros((c_len, n), F32)

    def chunk_inputs(c):
        rows = pl.ds(pl.multiple_of(c * c_len, c_len), c_len)
        r = r_ref[0, rows, :]
        k = k_ref[0, rows, :]
        v = v_ref[0, rows, :]
        kk = kk_ref[0, rows, :]
        a = a_ref[0, rows, :]
        lw = lw_ref[0, rows, :]
        gcum = _split_dot_left(tri_ref[...], lw, 3)
        e_pos = jnp.exp(gcum)
        e_neg = jnp.exp(-gcum)
        p_last = jnp.exp(gcum[c_len - 1:c_len, :])
        bb = kk * a * e_neg
        kb = k * e_neg
        return dict(rows=rows, v=v, p_last=p_last, rb=r * e_pos, ab=-kk * jnp.exp(gcum - lw), bb=bb, kb=kb,
                    bk_t=jnp.concatenate([bb * p_last, kb * p_last], axis=0).T)

    def chunk_group(it, carry):
        chunks = [chunk_inputs(it * CHUNKS_PER_ITER + ci) for ci in range(CHUNKS_PER_ITER)]
        units = [(ci, hd) for ci in range(CHUNKS_PER_ITER) for hd in range(N_HEADS)]
        nu = range(len(units))
        ls = [slice(hd * n, (hd + 1) * n) for _, hd in units]
        ch = [chunks[ci] for ci, _ in units]
        al = [ch[u]["ab"][:, ls[u]] for u in nu]
        rr = [ch[u]["rb"][:, ls[u]] for u in nu]
        vv = [ch[u]["v"][:, ls[u]] for u in nu]
        aa = [_bdot_nt(jnp.concatenate([al[u], rr[u]], axis=0),
                       jnp.concatenate([ch[u]["bb"][:, ls[u]], ch[u]["kb"][:, ls[u]]], axis=0)) for u in nu]
        top = [jnp.where(strict2, aa[u][:c_len], 0.0) for u in nu]
        bot = [jnp.where(incl2, aa[u][c_len:], 0.0) for u in nu]
        a_ab = [top[u][:, :c_len] for u in nu]
        t_inv = [eye + a_ab[u] for u in nu]
        x_pow = [_bdot(a_ab[u], a_ab[u]) for u in nu]
        av = [_bdot(top[u], jnp.concatenate([zeros, vv[u]], axis=0)) for u in nu]
        for lvl in range(5):
            if lvl < 4:
                z = [_bdot(jnp.concatenate([t_inv[u], x_pow[u]], axis=0), x_pow[u]) for u in nu]
                t_inv = [t_inv[u] + z[u][:c_len] for u in nu]
                x_pow = [z[u][c_len:] for u in nu]
            else:
                t_inv = [t_inv[u] + _bdot(t_inv[u], x_pow[u]) for u in nu]
        w12 = [_bdot(t_inv[u], jnp.concatenate([al[u], av[u]], axis=1)) for u in nu]
        z2 = []
        for u in nu:
            hd = units[u][1]
            rhs2 = jnp.concatenate([w12[u], jnp.concatenate([zeros, vv[u]], axis=1)], axis=0)
            lhs3 = jnp.concatenate([ch[u]["bk_t"][hd * n:(hd + 1) * n, :], bot[u]], axis=0)
            z2.append(_bdot(lhs3, rhs2))
        state = [h_scr[hd] for hd in range(N_HEADS)]
        for u in nu:
            hd = units[u][1]
            mq = z2[u][:, :n] + jnp.concatenate([zeros, rr[u]], axis=0)
            out = _bdot(mq, state[hd]) + z2[u][:, n:]
            p_col = jnp.sum(eye * ch[u]["p_last"][:, ls[u]], axis=1, keepdims=True)
            state[hd] = p_col * state[hd] + out[:c_len]
            y_scr[ch[u]["rows"], ls[u]] = out[c_len:]
        for hd in range(N_HEADS):
            h_scr[hd] = state[hd]
        return carry

    lax.fori_loop(0, tb // (c_len * CHUNKS_PER_ITER), chunk_group, 0)

    y = y_scr[...]
    seg = seg_ref[...]
    inv_n = 1.0 / n
    mu = _head_sums(y, seg) * inv_n
    yc = y - mu
    var = _head_sums(yc * yc, seg) * inv_n
    yn = yc * lax.rsqrt(var + GN_EPS) * lnw_ref[...] + lnb_ref[...]
    r = r_ref[0]
    bonus = _head_sums(r * k_ref[0] * rk_ref[...], seg) * v_ref[0]
    out_ref[0] = ((yn + bonus) * g_ref[0]).astype(BF16)


def _rwkv(r, k, v, kkn, a, lw, g, p):
    b, s, db = r.shape
    tb = TB_RWKV
    tok = pl.BlockSpec((1, tb, db), lambda bi, i: (bi, i, 0))
    consts = [p["r_k"], p["ln_x_w"], p["ln_x_b"], p["tri"], p["seg_ones"]]
    return pl.pallas_call(
        _rwkv_kernel,
        grid=(b, s // tb),
        in_specs=[tok] * 7 + [_const_spec(c.shape) for c in consts],
        out_specs=tok,
        out_shape=jax.ShapeDtypeStruct((b, s, db), BF16),
        scratch_shapes=[pltpu.VMEM((N_HEADS, HEAD_DIM, HEAD_DIM), F32),
                        pltpu.VMEM((tb, db), F32)],
        compiler_params=pltpu.CompilerParams(dimension_semantics=("arbitrary", "arbitrary"),
                                             vmem_limit_bytes=VMEM_LIMIT),
        name="rwkv",
    )(r, k, v, kkn, a, lw, g, *consts)


def _merge_kernel(x_ref, yc_ref, yr_ref, ym_ref, gmix_ref, wgate_ref, bgate_ref, wbr_ref, wo_ref,
                  gffn_ref, wrt_ref, brt_ref, tril_ref,
                  x1_ref, meta_ref, cnt_ref, base_scr):
    tm, d = x_ref.shape

    @pl.when(pl.program_id(0) == 0)
    def _():
        base_scr[...] = jnp.zeros_like(base_scr)

    x = x_ref[...]
    hb = _rms(x, gmix_ref[...]).astype(BF16)
    z = jnp.zeros((tm, d), F32)
    for i, y_ref in enumerate((yc_ref, yr_ref, ym_ref)):
        cs = slice(i * d, (i + 1) * d)
        gate = _sigmoid(jnp.dot(hb, wgate_ref[:, cs], preferred_element_type=F32) + bgate_ref[:, cs])
        z = z + gate * jnp.dot(y_ref[...], wbr_ref[i], preferred_element_type=F32)
    x1 = x + _bdot(z, wo_ref[...])
    x1_ref[...] = x1

    h2 = _rms(x1, gffn_ref[...])
    h_hi, h_lo = _split_terms(h2, 2)
    w_hi, w_lo = _split_terms(wrt_ref[...], 2)
    logits = (jnp.dot(h_hi, w_hi, preferred_element_type=F32)
              + (jnp.dot(h_lo, w_hi, preferred_element_type=F32)
                 + jnp.dot(h_hi, w_lo, preferred_element_type=F32))) + brt_ref[...]
    lane = lax.broadcasted_iota(jnp.int32, (tm, LANES), 1)
    neg = jnp.float32(-jnp.inf)
    big = jnp.int32(1 << 20)
    gmask = (lane >= N_EXPERTS) & (lane < N_EXPERTS + N_GROUPS)
    glv = jnp.where(gmask, logits, neg)
    gmax = jnp.max(glv, axis=-1, keepdims=True)
    g_sel = jnp.min(jnp.where(glv == gmax, lane - N_EXPERTS, big), axis=-1, keepdims=True)
    g_w = 1.0 / jnp.sum(jnp.exp(glv - gmax), axis=-1, keepdims=True)
    emask = (lane < N_EXPERTS) & ((lane >> 3) == g_sel)
    elv = jnp.where(emask, logits, neg)
    emax = jnp.max(elv, axis=-1, keepdims=True)
    esum = jnp.sum(jnp.exp(elv - emax), axis=-1, keepdims=True)
    i1 = jnp.min(jnp.where(elv == emax, lane, big), axis=-1, keepdims=True)
    elv2 = jnp.where(lane == i1, neg, elv)
    m2 = jnp.max(elv2, axis=-1, keepdims=True)
    i2 = jnp.min(jnp.where(elv2 == m2, lane, big), axis=-1, keepdims=True)
    p1 = 1.0 / esum
    p2 = jnp.exp(m2 - emax) / esum
    c1 = g_w * p1 / (p1 + p2)
    c2 = g_w * p2 / (p1 + p2)

    oh1 = lane == i1
    oh2 = lane == i2
    onehot = jnp.where(oh1 | oh2, 1.0, 0.0)
    before = jnp.dot(tril_ref[...], onehot.astype(BF16), preferred_element_type=F32) + base_scr[...]
    rank1 = jnp.sum(jnp.where(oh1, before, 0.0), axis=-1, keepdims=True)
    rank2 = jnp.sum(jnp.where(oh2, before, 0.0), axis=-1, keepdims=True)
    new_base = base_scr[...] + jnp.sum(onehot, axis=0, keepdims=True)
    base_scr[...] = new_base
    cnt_ref[...] = jnp.broadcast_to(new_base, cnt_ref.shape)

    col = lax.broadcasted_iota(jnp.int32, (tm, 8), 1)
    meta = jnp.where(col == 0, i1.astype(F32),
           jnp.where(col == 1, i2.astype(F32),
           jnp.where(col == 2, rank1,
           jnp.where(col == 3, rank2,
           jnp.where(col == 4, c1,
           jnp.where(col == 5, c2, 0.0))))))
    meta_ref[...] = meta


def _merge(x2, yc, yr, ym, p):
    t, d = x2.shape
    tm = TM_MERGE
    db = D_BRANCH
    tok = lambda c: pl.BlockSpec((tm, c), lambda i: (i, 0))
    consts = [p["g_mix"], p["w_gate"], p["b_gate"], p["w_branch"], p["w_o"], p["g_ffn"],
              p["w_router"], p["b_router"], p["tril_strict"]]
    return pl.pallas_call(
        _merge_kernel,
        grid=(t // tm,),
        in_specs=[tok(d), tok(db), tok(db), tok(db)] + [_const_spec(c.shape) for c in consts],
        out_specs=[tok(d), tok(8), _const_spec((8, LANES))],
        out_shape=[jax.ShapeDtypeStruct((t, d), F32), jax.ShapeDtypeStruct((t, 8), F32),
                   jax.ShapeDtypeStruct((8, LANES), F32)],
        scratch_shapes=[pltpu.VMEM((1, LANES), F32)],
        compiler_params=pltpu.CompilerParams(dimension_semantics=("arbitrary",),
                                             vmem_limit_bytes=VMEM_LIMIT),
        name="merge",
    )(x2, yc, yr, ym, *consts)


def _scatter_kernel(dest_ref, x1_ref, gffn_ref, xs_ref, hbuf, sem):
    ts = x1_ref.shape[0]
    s = pl.program_id(0)
    slot = s % 2

    def wait_slot(sl):
        for _ in range(TOP_K * ts // SUBLANES):
            pltpu.make_async_copy(hbuf.at[sl, 0], xs_ref.at[pl.ds(0, SUBLANES), :], sem.at[sl]).wait()

    @pl.when(s >= 2)
    def _():
        wait_slot(slot)

    d_model = x1_ref.shape[1]
    hbuf[slot] = _rms(x1_ref[...], gffn_ref[...]).reshape(ts // SUBLANES, SUBLANES, d_model)

    def issue(grp, carry):
        for j in range(SUBLANES):
            base = (s * ts + grp * SUBLANES + j) * TOP_K
            for kslot in range(TOP_K):
                d = dest_ref[base + kslot]
                pltpu.make_async_copy(hbuf.at[slot, grp, pl.ds(j, 1), :], xs_ref.at[pl.ds(d, 1), :],
                                      sem.at[slot]).start(priority=kslot)
        return carry

    lax.fori_loop(0, ts // SUBLANES, issue, 0, unroll=ISSUE_UNROLL)

    @pl.when(s == pl.num_programs(0) - 1)
    def _():
        @pl.when(s >= 1)
        def _():
            wait_slot(1 - slot)
        wait_slot(slot)


def _scatter(dest, x1, g_ffn, n_rows):
    t, d = x1.shape
    ts = TS_SCATTER
    return pl.pallas_call(
        _scatter_kernel,
        grid_spec=pltpu.PrefetchScalarGridSpec(
            num_scalar_prefetch=1,
            grid=(t // ts,),
            in_specs=[pl.BlockSpec((ts, d), lambda i, dest: (i, 0)),
                      pl.BlockSpec((1, d), lambda i, dest: (0, 0))],
            out_specs=pl.BlockSpec(memory_space=pl.ANY),
            scratch_shapes=[pltpu.VMEM((2, ts // SUBLANES, SUBLANES, d), F32),
                            pltpu.SemaphoreType.DMA((2,))],
        ),
        out_shape=jax.ShapeDtypeStruct((n_rows, d), F32),
        compiler_params=pltpu.CompilerParams(dimension_semantics=("arbitrary",),
                                             vmem_limit_bytes=VMEM_LIMIT),
        name="scatter",
    )(dest, x1, g_ffn)


def _experts_kernel(be_ref, nused_ref, nexte_ref, xs_ref, wg_hbm, wu_hbm, wd_hbm, ys_ref,
                    wg_f, wu_f, wd_f, wg_s, wu_s, wd_s, sem):
    i = pl.program_id(0)
    e = be_ref[i]
    prev = be_ref[jnp.maximum(i - 1, 0)]
    active = i < nused_ref[0]

    def weight_copies(ex):
        return (pltpu.make_async_copy(wg_hbm.at[ex], wg_f, sem.at[0]),
                pltpu.make_async_copy(wu_hbm.at[ex], wu_f, sem.at[1]),
                pltpu.make_async_copy(wd_hbm.at[ex], wd_f, sem.at[2]))

    @pl.when(i == 0)
    def _():
        for cp in weight_copies(e):
            cp.start()

    @pl.when(active & ((i == 0) | (e != prev)))
    def _():
        for cp in weight_copies(e):
            cp.wait()
        wg_s[...] = wg_f[...].astype(BF16)
        wu_s[...] = wu_f[...].astype(BF16)
        wd_s[...] = wd_f[...].astype(BF16)
        nxt = nexte_ref[e]

        @pl.when(nxt < N_EXPERTS)
        def _():
            for cp in weight_copies(nxt):
                cp.start()

    @pl.when(active)
    def _():
        xb = xs_ref[...].astype(BF16)
        gate = jnp.dot(xb, wg_s[...], preferred_element_type=F32)
        up = jnp.dot(xb, wu_s[...], preferred_element_type=F32)
        hid = gate * _sigmoid(gate) * up
        ys_ref[...] = jnp.dot(hid.astype(BF16), wd_s[...], preferred_element_type=F32)


def _experts(blk_expert, n_used, next_expert, xs, w_gate, w_up, w_down):
    n_rows, d = xs.shape
    nb = n_rows // ROW_BLOCK
    de = w_gate.shape[-1]

    def row_map(i, be, nu, ne):
        return (jnp.minimum(i, nu[0] - 1), 0)

    return pl.pallas_call(
        _experts_kernel,
        grid_spec=pltpu.PrefetchScalarGridSpec(
            num_scalar_prefetch=3,
            grid=(nb,),
            in_specs=[pl.BlockSpec((ROW_BLOCK, d), row_map)] + [pl.BlockSpec(memory_space=pl.ANY)] * 3,
            out_specs=pl.BlockSpec((ROW_BLOCK, d), row_map),
            scratch_shapes=[pltpu.VMEM((d, de), F32), pltpu.VMEM((d, de), F32), pltpu.VMEM((de, d), F32),
                            pltpu.VMEM((d, de), BF16), pltpu.VMEM((d, de), BF16), pltpu.VMEM((de, d), BF16),
                            pltpu.SemaphoreType.DMA((3,))],
        ),
        out_shape=jax.ShapeDtypeStruct((n_rows, d), F32),
        compiler_params=pltpu.CompilerParams(dimension_semantics=("arbitrary",),
                                             vmem_limit_bytes=VMEM_LIMIT),
        name="experts",
    )(blk_expert, n_used, next_expert, xs, w_gate, w_up, w_down)


def _combine_kernel(dest_ref, x1_ref, meta_ref, gfin_ref, ys_ref, out_ref, ybuf, sem):
    te = x1_ref.shape[0]
    s = pl.program_id(0)
    nsteps = pl.num_programs(0)
    slot = s % 2

    d_model = x1_ref.shape[1]

    def issue_step(step, sl):
        def issue(grp, carry):
            for j in range(SUBLANES):
                base = (step * te + grp * SUBLANES + j) * TOP_K
                for kslot in range(TOP_K):
                    d = dest_ref[base + kslot]
                    pltpu.make_async_copy(ys_ref.at[pl.ds(d, 1), :],
                                          ybuf.at[sl, kslot, grp, pl.ds(j, 1), :],
                                          sem.at[sl]).start(priority=kslot)
            return carry
        lax.fori_loop(0, te // SUBLANES, issue, 0, unroll=ISSUE_UNROLL)

    @pl.when(s == 0)
    def _():
        issue_step(0, 0)

    @pl.when(s + 1 < nsteps)
    def _():
        issue_step(s + 1, 1 - slot)

    for _ in range(TOP_K * te // SUBLANES):
        pltpu.make_async_copy(ys_ref.at[pl.ds(0, SUBLANES), :], ybuf.at[slot, 0, 0], sem.at[slot]).wait()

    meta = meta_ref[...]
    y0 = ybuf[slot, 0].reshape(te, d_model)
    y1 = ybuf[slot, 1].reshape(te, d_model)
    x2 = x1_ref[...] + y0 * meta[:, 4:5] + y1 * meta[:, 5:6]
    out_ref[...] = _rms(x2, gfin_ref[...])


def _combine(dest, x1, meta, g_final, ys):
    t, d = x1.shape
    te = TE_COMBINE
    return pl.pallas_call(
        _combine_kernel,
        grid_spec=pltpu.PrefetchScalarGridSpec(
            num_scalar_prefetch=1,
            grid=(t // te,),
            in_specs=[pl.BlockSpec((te, d), lambda i, dest: (i, 0)),
                      pl.BlockSpec((te, 8), lambda i, dest: (i, 0)),
                      pl.BlockSpec((1, d), lambda i, dest: (0, 0)),
                      pl.BlockSpec(memory_space=pl.ANY)],
            out_specs=pl.BlockSpec((te, d), lambda i, dest: (i, 0)),
            scratch_shapes=[pltpu.VMEM((2, TOP_K, te // SUBLANES, SUBLANES, d), F32),
                            pltpu.SemaphoreType.DMA((2,))],
        ),
        out_shape=jax.ShapeDtypeStruct((t, d), F32),
        compiler_params=pltpu.CompilerParams(dimension_semantics=("arbitrary",),
                                             vmem_limit_bytes=VMEM_LIMIT),
        name="combine",
    )(dest, x1, meta, g_final, ys)


def _constants(tm_merge):
    n = CHUNK
    tri = (jnp.arange(n)[:, None] >= jnp.arange(n)[None, :]).astype(BF16)
    head = jnp.arange(2 * LANES) // HEAD_DIM
    seg_ones = (head[:, None] == head[None, :]).astype(BF16)
    tril_strict = (jnp.arange(tm_merge)[:, None] > jnp.arange(tm_merge)[None, :]).astype(BF16)
    return tri, seg_ones, tril_strict


def kernel(x, mem, g_mix, g_mem, w_in, conv_w, mu_rkv, mu_wag, w_lora1, w_lora2, w0, a_lora1, a_lora2, a0, g_lora1, g_lora2, k_k, k_a, r_k, ln_x_w, ln_x_b, w_kv_mem, w_branch, w_gate, b_gate, w_o, g_ffn, w_router_group, b_router_group, w_router_expert, b_router_expert, w_exp_gate, w_exp_up, w_exp_down, g_final):
    assert g_mix.shape[0] == 1, "single-layer block"
    b, s, d = x.shape
    t = b * s
    db = D_BRANCH
    tri, seg_ones, tril_strict = _constants(TM_MERGE)
    row = lambda a: a.reshape(1, -1)
    pad_r = LANES - N_EXPERTS - N_GROUPS
    p = {
        "g_mix": row(g_mix[0]), "w_in": w_in[0].astype(BF16), "conv_w": conv_w[0].T,
        "mu_rkv": row(mu_rkv[0]), "mu_wag": mu_wag[0],
        "w_lora1": w_lora1[0].astype(BF16), "w_lora2": w_lora2[0].astype(BF16), "w0": row(w0[0]),
        "a_lora1": a_lora1[0].astype(BF16), "a_lora2": a_lora2[0].astype(BF16), "a0": row(a0[0]),
        "g_lora1": g_lora1[0].astype(BF16), "g_lora2": g_lora2[0].astype(BF16),
        "k_k": row(k_k[0]), "k_a": row(k_a[0]), "r_k": row(r_k[0]),
        "ln_x_w": row(ln_x_w[0]), "ln_x_b": row(ln_x_b[0]),
        "w_gate": w_gate[0].astype(BF16), "b_gate": row(b_gate[0]),
        "w_branch": w_branch[0].astype(BF16), "w_o": w_o[0].astype(BF16), "g_ffn": row(g_ffn[0]),
        "w_router": jnp.concatenate([w_router_expert[0], w_router_group[0],
                                     jnp.zeros((d, pad_r), F32)], axis=1),
        "b_router": row(jnp.concatenate([b_router_expert[0], b_router_group[0],
                                         jnp.zeros((pad_r,), F32)])),
        "tri": tri, "seg_ones": seg_ones, "tril_strict": tril_strict,
    }

    km, vm = _memkv(mem, row(g_mem[0]), w_kv_mem[0].astype(BF16))
    yconv, ymem, r, k, v, kkn, a, lw, g = _prologue(x, km, vm, p)
    yrwkv = _rwkv(r, k, v, kkn, a, lw, g, p)
    x1, meta, cnt = _merge(x.reshape(t, d), yconv.reshape(t, db), yrwkv.reshape(t, db),
                           ymem.reshape(t, db), p)

    counts = cnt[0, :N_EXPERTS].astype(jnp.int32)
    padded = ((counts + ROW_BLOCK - 1) // ROW_BLOCK) * ROW_BLOCK
    pad_end = jnp.cumsum(padded)
    pad_start = pad_end - padded
    n_blocks = (t * TOP_K) // ROW_BLOCK + N_EXPERTS
    e_idx = meta[:, 0:TOP_K].astype(jnp.int32)
    onehot = e_idx[:, :, None] == jnp.arange(N_EXPERTS, dtype=jnp.int32)[None, None, :]
    dest = (jnp.sum(jnp.where(onehot, pad_start[None, None, :], 0), axis=-1)
            + meta[:, TOP_K:2 * TOP_K].astype(jnp.int32)).reshape(t * TOP_K)
    blk_start = jnp.arange(n_blocks, dtype=jnp.int32) * ROW_BLOCK
    blk_expert = jnp.minimum(jnp.sum((pad_end[None, :] <= blk_start[:, None]).astype(jnp.int32), axis=1),
                             N_EXPERTS - 1)
    n_used = (pad_end[-1:] // ROW_BLOCK).astype(jnp.int32)
    eids = jnp.arange(N_EXPERTS, dtype=jnp.int32)
    later_nonempty = (eids[None, :] > eids[:, None]) & (counts[None, :] > 0)
    next_expert = jnp.min(jnp.where(later_nonempty, eids[None, :], N_EXPERTS), axis=1)

    xs = _scatter(dest, x1, p["g_ffn"], n_blocks * ROW_BLOCK)
    ys = _experts(blk_expert, n_used, next_expert, xs, w_exp_gate[0], w_exp_up[0], w_exp_down[0])
    out = _combine(dest, x1, meta, row(g_final), ys)
    return out.reshape(b, s, d)
```

```python
import functools

import jax
import jax.numpy as jnp
from jax import lax
from jax.experimental import pallas as pl
from jax.experimental.pallas import tpu as pltpu

F32 = jnp.float32
BF16 = jnp.bfloat16

NORM_EPS = 1e-6
GN_EPS = 64e-5
D_BRANCH = 512
HEAD_DIM = 64
N_HEADS = 8
CHUNK = 64
CHUNKS_PER_ITER = 4
MEM_HEADS = 4
MEM_HEAD_DIM = 128
N_GROUPS = 8
EXPERTS_PER_GROUP = 8
N_EXPERTS = 64
TOP_K = 2
ROW_BLOCK = 128
LANES = 128
VMEM_LIMIT = 56 * 1024 * 1024

TM_PROLOGUE = 512
TB_RWKV = 256
TM_MERGE = 512
TS_SCATTER = 256
TE_COMBINE = 256
SUBLANES = 8
ISSUE_UNROLL = 2


def _bdot(a, b):
    return jnp.dot(a.astype(BF16), b.astype(BF16), preferred_element_type=F32)


def _bdot_nt(a, b):
    return lax.dot_general(a.astype(BF16), b.astype(BF16), (((1,), (1,)), ((), ())),
                           preferred_element_type=F32)


def _split_terms(x, n_terms):
    terms = []
    for _ in range(n_terms):
        t = x.astype(BF16)
        terms.append(t)
        x = x - t.astype(F32)
    return terms


def _split_dot_left(m_bf16, x, n_terms):
    return sum(jnp.dot(m_bf16, t, preferred_element_type=F32) for t in _split_terms(x, n_terms))


def _head_sums(x, seg_bf16, n_terms):
    w = seg_bf16.shape[0]
    terms = _split_terms(x, n_terms)
    halves = [sum(jnp.dot(t[:, c:c + w], seg_bf16, preferred_element_type=F32) for t in terms)
              for c in range(0, x.shape[1], w)]
    return jnp.concatenate(halves, axis=1)


def _rms(x, g):
    return x * lax.rsqrt(jnp.mean(x * x, axis=-1, keepdims=True) + NORM_EPS) * g


def _sigmoid(x):
    return 1.0 / (1.0 + jnp.exp(-x))


def _const_spec(shape):
    n = len(shape)
    return pl.BlockSpec(shape, lambda *_: (0,) * n)


def _memkv_kernel(mem_ref, g_ref, w_ref, k_ref, v_ref):
    mn = _rms(mem_ref[0], g_ref[...])
    kv = _bdot(mn, w_ref[...])
    k_ref[0] = kv[:, :D_BRANCH].astype(BF16)
    v_ref[0] = kv[:, D_BRANCH:].astype(BF16)


def _memkv(mem, g_mem, w_kv):
    b, m, d = mem.shape
    return pl.pallas_call(
        _memkv_kernel,
        grid=(b,),
        in_specs=[pl.BlockSpec((1, m, d), lambda i: (i, 0, 0)),
                  _const_spec((1, d)), _const_spec((d, 2 * D_BRANCH))],
        out_specs=[pl.BlockSpec((1, m, D_BRANCH), lambda i: (i, 0, 0)),
                   pl.BlockSpec((1, m, D_BRANCH), lambda i: (i, 0, 0))],
        out_shape=[jax.ShapeDtypeStruct((b, m, D_BRANCH), BF16)] * 2,
        compiler_params=pltpu.CompilerParams(dimension_semantics=("arbitrary",),
                                             vmem_limit_bytes=VMEM_LIMIT),
        name="memkv",
    )(mem, g_mem, w_kv)


def _prologue_kernel(x_ref, gmix_ref, win_ref, convw_ref, murkv_ref, muwag_ref,
                     wl1_ref, wl2_ref, w0_ref, al1_ref, al2_ref, a0_ref, gl1_ref, gl2_ref,
                     kk_ref, ka_ref, seg_ref, km_ref, vm_ref,
                     yconv_ref, ymem_ref, r_ref, k_ref, v_ref, kkn_ref, a_ref, lw_ref, g_ref,
                     prev_h, prev_p, prev_cu):
    tm = x_ref.shape[1]
    db = D_BRANCH

    @pl.when(pl.program_id(1) == 0)
    def _():
        prev_h[...] = jnp.zeros_like(prev_h)
        prev_p[...] = jnp.zeros_like(prev_p)
        prev_cu[...] = jnp.zeros_like(prev_cu)

    rows = lax.broadcasted_iota(jnp.int32, (tm, 1), 0)

    def shift1(u, prev_row):
        return jnp.where(rows == 0, prev_row, pltpu.roll(u, 1, axis=0))

    h = _rms(x_ref[0], gmix_ref[...])
    proj = _bdot(h, win_ref[...])

    bg, cg, u = proj[:, :db], proj[:, db:2 * db], proj[:, 2 * db:3 * db]
    cu = cg * u
    cu1 = shift1(cu, prev_cu[1:2, :])
    cu2 = jnp.where(rows == 0, prev_cu[0:1, :],
                    jnp.where(rows == 1, prev_cu[1:2, :], pltpu.roll(cu, 2, axis=0)))
    conv = cu2 * convw_ref[0:1, :] + cu1 * convw_ref[1:2, :] + cu * convw_ref[2:3, :]
    yconv_ref[0] = (bg * conv).astype(BF16)
    prev_cu[...] = cu[tm - 2:tm, :]

    pr = proj[:, 3 * db:6 * db]
    prs = shift1(pr, prev_p[...])
    mixed = pr + (prs - pr) * murkv_ref[...]
    prev_p[...] = pr[tm - 1:tm, :]
    r, k, v = mixed[:, :db], mixed[:, db:2 * db], mixed[:, 2 * db:]

    dh = shift1(h, prev_h[...]) - h
    prev_h[...] = h[tm - 1:tm, :]
    xw = h + dh * muwag_ref[0:1, :]
    xa = h + dh * muwag_ref[1:2, :]
    xg = h + dh * muwag_ref[2:3, :]
    zz = w0_ref[...] + _bdot(jnp.tanh(_bdot(xw, wl1_ref[...])), wl2_ref[...])
    softplus = jnp.maximum(-zz, 0.0) + jnp.log(1.0 + jnp.exp(-jnp.abs(zz)))
    lw_ref[0] = -jnp.exp(-softplus - 0.5)
    a = _sigmoid(a0_ref[...] + _bdot(_bdot(xa, al1_ref[...]), al2_ref[...]))
    g_ref[0] = _bdot(_sigmoid(_bdot(xg, gl1_ref[...])), gl2_ref[...])

    kk = k * kk_ref[...]
    ss = _head_sums(kk * kk, seg_ref[...], 1)
    kkn_ref[0] = kk * lax.rsqrt(jnp.maximum(ss, 1e-24))
    k_ref[0] = k * (1.0 + (a - 1.0) * ka_ref[...])
    r_ref[0] = r
    v_ref[0] = v
    a_ref[0] = a

    q = proj[:, 6 * db:]
    scale = MEM_HEAD_DIM ** -0.5
    for hh in range(MEM_HEADS):
        sl = slice(hh * MEM_HEAD_DIM, (hh + 1) * MEM_HEAD_DIM)
        s = _bdot_nt(q[:, sl], km_ref[0, :, sl]) * scale
        p = jnp.exp(s - jnp.max(s, axis=-1, keepdims=True))
        o = _bdot(p, vm_ref[0, :, sl]) / jnp.sum(p, axis=-1, keepdims=True)
        ymem_ref[0, :, sl] = o.astype(BF16)


def _prologue(x, km, vm, p):
    b, s, d = x.shape
    tm = TM_PROLOGUE
    db = D_BRANCH
    m = km.shape[1]
    tok = lambda c: pl.BlockSpec((1, tm, c), lambda bi, i: (bi, i, 0))
    consts = [p["g_mix"], p["w_in"], p["conv_w"], p["mu_rkv"], p["mu_wag"],
              p["w_lora1"], p["w_lora2"], p["w0"], p["a_lora1"], p["a_lora2"], p["a0"],
              p["g_lora1"], p["g_lora2"], p["k_k"], p["k_a"], p["seg_ones"]]
    out_shapes = ([jax.ShapeDtypeStruct((b, s, db), BF16)] * 2
                  + [jax.ShapeDtypeStruct((b, s, db), F32)] * 7)
    return pl.pallas_call(
        _prologue_kernel,
        grid=(b, s // tm),
        in_specs=[tok(d)] + [_const_spec(c.shape) for c in consts]
                 + [pl.BlockSpec((1, m, db), lambda bi, i: (bi, 0, 0))] * 2,
        out_specs=[tok(db)] * 9,
        out_shape=out_shapes,
        scratch_shapes=[pltpu.VMEM((1, d), F32), pltpu.VMEM((1, 3 * db), F32),
                        pltpu.VMEM((2, db), F32)],
        compiler_params=pltpu.CompilerParams(dimension_semantics=("arbitrary", "arbitrary"),
                                             vmem_limit_bytes=VMEM_LIMIT),
        name="prologue",
    )(x, *consts, km, vm)


def _rwkv_kernel(r_ref, k_ref, v_ref, kk_ref, a_ref, lw_ref, g_ref, rk_ref, lnw_ref, lnb_ref,
                 tri_ref, seg_ref, out_ref, h_scr, y_scr):
    tb = r_ref.shape[1]
    n = HEAD_DIM
    c_len = CHUNK

    @pl.when(pl.program_id(1) == 0)
    def _():
        h_scr[...] = jnp.zeros_like(h_scr)

    row2 = lax.broadcasted_iota(jnp.int32, (c_len, 2 * c_len), 0)
    col2 = lax.broadcasted_iota(jnp.int32, (c_len, 2 * c_len), 1) & (c_len - 1)
    strict2 = col2 < row2
    incl2 = col2 <= row2
    eye = (lax.broadcasted_iota(jnp.int32, (c_len, n), 0)
           == lax.broadcasted_iota(jnp.int32, (c_len, n), 1)).astype(F32)
    zeros = jnp.zeros((c_len, n), F32)

    def chunk_inputs(c):
        rows = pl.ds(pl.multiple_of(c * c_len, c_len), c_len)
        r = r_ref[0, rows, :]
        k = k_ref[0, rows, :]
        v = v_ref[0, rows, :]
        kk = kk_ref[0, rows, :]
        a = a_ref[0, rows, :]
        lw = lw_ref[0, rows, :]
        gcum = _split_dot_left(tri_ref[...], lw, 2)
        e_pos = jnp.exp(gcum)
        e_neg = jnp.exp(-gcum)
        p_last = jnp.exp(gcum[c_len - 1:c_len, :])
        bb = kk * a * e_neg
        kb = k * e_neg
        return dict(rows=rows, v=v, p_last=p_last, rb=r * e_pos, ab=-kk * jnp.exp(gcum - lw), bb=bb, kb=kb,
                    bk_t=jnp.concatenate([bb * p_last, kb * p_last], axis=0).T)

    def chunk_group(it, carry):
        chunks = [chunk_inputs(it * CHUNKS_PER_ITER + ci) for ci in range(CHUNKS_PER_ITER)]
        units = [(ci, hd) for ci in range(CHUNKS_PER_ITER) for hd in range(N_HEADS)]
        nu = range(len(units))
        ls = [slice(hd * n, (hd + 1) * n) for _, hd in units]
        ch = [chunks[ci] for ci, _ in units]
        al = [ch[u]["ab"][:, ls[u]] for u in nu]
        rr = [ch[u]["rb"][:, ls[u]] for u in nu]
        vv = [ch[u]["v"][:, ls[u]] for u in nu]
        aa = [_bdot_nt(jnp.concatenate([al[u], rr[u]], axis=0),
                       jnp.concatenate([ch[u]["bb"][:, ls[u]], ch[u]["kb"][:, ls[u]]], axis=0)) for u in nu]
        top = [jnp.where(strict2, aa[u][:c_len], 0.0) for u in nu]
        bot = [jnp.where(incl2, aa[u][c_len:], 0.0) for u in nu]
        a_ab = [top[u][:, :c_len] for u in nu]
        t_inv = [eye + a_ab[u] for u in nu]
        x_pow = [_bdot(a_ab[u], a_ab[u]) for u in nu]
        av = [_bdot(top[u], jnp.concatenate([zeros, vv[u]], axis=0)) for u in nu]
        for lvl in range(5):
            if lvl < 4:
                z = [_bdot(jnp.concatenate([t_inv[u], x_pow[u]], axis=0), x_pow[u]) for u in nu]
                t_inv = [t_inv[u] + z[u][:c_len] for u in nu]
                x_pow = [z[u][c_len:] for u in nu]
            else:
                t_inv = [t_inv[u] + _bdot(t_inv[u], x_pow[u]) for u in nu]
        w12 = [_bdot(t_inv[u], jnp.concatenate([al[u], av[u]], axis=1)) for u in nu]
        z2 = []
        for u in nu:
            hd = units[u][1]
            rhs2 = jnp.concatenate([w12[u], jnp.concatenate([zeros, vv[u]], axis=1)], axis=0)
            lhs3 = jnp.concatenate([ch[u]["bk_t"][hd * n:(hd + 1) * n, :], bot[u]], axis=0)
            z2.append(_bdot(lhs3, rhs2))
        state = [h_scr[hd] for hd in range(N_HEADS)]
        for u in nu:
            hd = units[u][1]
            mq = z2[u][:, :n] + jnp.concatenate([zeros, rr[u]], axis=0)
            out = _bdot(mq, state[hd]) + z2[u][:, n:]
            p_col = jnp.sum(eye * ch[u]["p_last"][:, ls[u]], axis=1, keepdims=True)
            state[hd] = p_col * state[hd] + out[:c_len]
            y_scr[ch[u]["rows"], ls[u]] = out[c_len:]
        for hd in range(N_HEADS):
            h_scr[hd] = state[hd]
        return carry

    lax.fori_loop(0, tb // (c_len * CHUNKS_PER_ITER), chunk_group, 0)

    y = y_scr[...]
    seg = seg_ref[...]
    inv_n = 1.0 / n
    mu = _head_sums(y, seg, 2) * inv_n
    yc = y - mu
    var = _head_sums(yc * yc, seg, 1) * inv_n
    yn = yc * lax.rsqrt(var + GN_EPS) * lnw_ref[...] + lnb_ref[...]
    r = r_ref[0]
    bonus = _head_sums(r * k_ref[0] * rk_ref[...], seg, 1) * v_ref[0]
    out_ref[0] = ((yn + bonus) * g_ref[0]).astype(BF16)


def _rwkv(r, k, v, kkn, a, lw, g, p):
    b, s, db = r.shape
    tb = TB_RWKV
    tok = pl.BlockSpec((1, tb, db), lambda bi, i: (bi, i, 0))
    consts = [p["r_k"], p["ln_x_w"], p["ln_x_b"], p["tri"], p["seg_ones"]]
    return pl.pallas_call(
        _rwkv_kernel,
        grid=(b, s // tb),
        in_specs=[tok] * 7 + [_const_spec(c.shape) for c in consts],
        out_specs=tok,
        out_shape=jax.ShapeDtypeStruct((b, s, db), BF16),
        scratch_shapes=[pltpu.VMEM((N_HEADS, HEAD_DIM, HEAD_DIM), F32),
                        pltpu.VMEM((tb, db), F32)],
        compiler_params=pltpu.CompilerParams(dimension_semantics=("arbitrary", "arbitrary"),
                                             vmem_limit_bytes=VMEM_LIMIT),
        name="rwkv",
    )(r, k, v, kkn, a, lw, g, *consts)


def _merge_kernel(x_ref, yc_ref, yr_ref, ym_ref, gmix_ref, wgate_ref, bgate_ref, wbr_ref, wo_ref,
                  gffn_ref, wrt_ref, brt_ref, tril_ref,
                  x1_ref, meta_ref, cnt_ref, base_scr):
    tm, d = x_ref.shape

    @pl.when(pl.program_id(0) == 0)
    def _():
        base_scr[...] = jnp.zeros_like(base_scr)

    x = x_ref[...]
    hb = _rms(x, gmix_ref[...]).astype(BF16)
    z = jnp.zeros((tm, d), F32)
    for i, y_ref in enumerate((yc_ref, yr_ref, ym_ref)):
        cs = slice(i * d, (i + 1) * d)
        gate = _sigmoid(jnp.dot(hb, wgate_ref[:, cs], preferred_element_type=F32) + bgate_ref[:, cs])
        z = z + gate * jnp.dot(y_ref[...], wbr_ref[i], preferred_element_type=F32)
    x1 = x + _bdot(z, wo_ref[...])
    x1_ref[...] = x1

    h2 = _rms(x1, gffn_ref[...])
    h_hi, h_lo = _split_terms(h2, 2)
    w_hi, w_lo = _split_terms(wrt_ref[...], 2)
    logits = (jnp.dot(h_hi, w_hi, preferred_element_type=F32)
              + (jnp.dot(h_lo, w_hi, preferred_element_type=F32)
                 + jnp.dot(h_hi, w_lo, preferred_element_type=F32))) + brt_ref[...]
    lane = lax.broadcasted_iota(jnp.int32, (tm, LANES), 1)
    neg = jnp.float32(-jnp.inf)
    big = jnp.int32(1 << 20)
    gmask = (lane >= N_EXPERTS) & (lane < N_EXPERTS + N_GROUPS)
    glv = jnp.where(gmask, logits, neg)
    gmax = jnp.max(glv, axis=-1, keepdims=True)
    g_sel = jnp.min(jnp.where(glv == gmax, lane - N_EXPERTS, big), axis=-1, keepdims=True)
    g_w = 1.0 / jnp.sum(jnp.exp(glv - gmax), axis=-1, keepdims=True)
    emask = (lane < N_EXPERTS) & ((lane >> 3) == g_sel)
    elv = jnp.where(emask, logits, neg)
    emax = jnp.max(elv, axis=-1, keepdims=True)
    esum = jnp.sum(jnp.exp(elv - emax), axis=-1, keepdims=True)
    i1 = jnp.min(jnp.where(elv == emax, lane, big), axis=-1, keepdims=True)
    elv2 = jnp.where(lane == i1, neg, elv)
    m2 = jnp.max(elv2, axis=-1, keepdims=True)
    i2 = jnp.min(jnp.where(elv2 == m2, lane, big), axis=-1, keepdims=True)
    p1 = 1.0 / esum
    p2 = jnp.exp(m2 - emax) / esum
    c1 = g_w * p1 / (p1 + p2)
    c2 = g_w * p2 / (p1 + p2)

    oh1 = lane == i1
    oh2 = lane == i2
    onehot = jnp.where(oh1 | oh2, 1.0, 0.0)
    before = jnp.dot(tril_ref[...], onehot.astype(BF16), preferred_element_type=F32) + base_scr[...]
    rank1 = jnp.sum(jnp.where(oh1, before, 0.0), axis=-1, keepdims=True)
    rank2 = jnp.sum(jnp.where(oh2, before, 0.0), axis=-1, keepdims=True)
    new_base = base_scr[...] + jnp.sum(onehot, axis=0, keepdims=True)
    base_scr[...] = new_base
    cnt_ref[...] = jnp.broadcast_to(new_base, cnt_ref.shape)

    col = lax.broadcasted_iota(jnp.int32, (tm, 8), 1)
    meta = jnp.where(col == 0, i1.astype(F32),
           jnp.where(col == 1, i2.astype(F32),
           jnp.where(col == 2, rank1,
           jnp.where(col == 3, rank2,
           jnp.where(col == 4, c1,
           jnp.where(col == 5, c2, 0.0))))))
    meta_ref[...] = meta


def _merge(x2, yc, yr, ym, p):
    t, d = x2.shape
    tm = TM_MERGE
    db = D_BRANCH
    tok = lambda c: pl.BlockSpec((tm, c), lambda i: (i, 0))
    consts = [p["g_mix"], p["w_gate"], p["b_gate"], p["w_branch"], p["w_o"], p["g_ffn"],
              p["w_router"], p["b_router"], p["tril_strict"]]
    return pl.pallas_call(
        _merge_kernel,
        grid=(t // tm,),
        in_specs=[tok(d), tok(db), tok(db), tok(db)] + [_const_spec(c.shape) for c in consts],
        out_specs=[tok(d), tok(8), _const_spec((8, LANES))],
        out_shape=[jax.ShapeDtypeStruct((t, d), F32), jax.ShapeDtypeStruct((t, 8), F32),
                   jax.ShapeDtypeStruct((8, LANES), F32)],
        scratch_shapes=[pltpu.VMEM((1, LANES), F32)],
        compiler_params=pltpu.CompilerParams(dimension_semantics=("arbitrary",),
                                             vmem_limit_bytes=VMEM_LIMIT),
        name="merge",
    )(x2, yc, yr, ym, *consts)


def _store_row_tiles(ref2d, x):
    rows, d = x.shape
    nt = d // LANES
    for c in range(nt):
        ref2d[pl.ds(c, rows, stride=nt), :] = x[:, c * LANES:(c + 1) * LANES]


def _load_row_tiles(ref2d, rows, nt):
    return jnp.concatenate([ref2d[pl.ds(c, rows, stride=nt), :] for c in range(nt)], axis=1)


def _scatter_kernel(dest_ref, x1_ref, gffn_ref, xs_ref, hbuf, sem):
    ts, d_model = x1_ref.shape
    nt = d_model // LANES
    s = pl.program_id(0)
    slot = s % 2

    def wait_slot(sl):
        for _ in range(TOP_K):
            pltpu.make_async_copy(hbuf.at[sl], xs_ref.at[pl.ds(0, ts * nt), :], sem.at[sl]).wait()

    @pl.when(s >= 2)
    def _():
        wait_slot(slot)

    _store_row_tiles(hbuf.at[slot], _rms(x1_ref[...], gffn_ref[...]))

    def issue(grp, carry):
        grp_off = pl.multiple_of(grp * (SUBLANES * nt), SUBLANES * nt)
        for j in range(SUBLANES):
            base = (s * ts + grp * SUBLANES + j) * TOP_K
            for kslot in range(TOP_K):
                d = pl.multiple_of(dest_ref[base + kslot], nt)
                pltpu.make_async_copy(hbuf.at[slot, pl.ds(grp_off + j * nt, nt), :],
                                      xs_ref.at[pl.ds(d, nt), :], sem.at[slot]).start(priority=kslot)
        return carry

    lax.fori_loop(0, ts // SUBLANES, issue, 0, unroll=ISSUE_UNROLL)

    @pl.when(s == pl.num_programs(0) - 1)
    def _():
        @pl.when(s >= 1)
        def _():
            wait_slot(1 - slot)
        wait_slot(slot)


def _scatter(dest, x1, g_ffn, n_rows):
    t, d = x1.shape
    ts = TS_SCATTER
    return pl.pallas_call(
        _scatter_kernel,
        grid_spec=pltpu.PrefetchScalarGridSpec(
            num_scalar_prefetch=1,
            grid=(t // ts,),
            in_specs=[pl.BlockSpec((ts, d), lambda i, dest: (i, 0)),
                      pl.BlockSpec((1, d), lambda i, dest: (0, 0))],
            out_specs=pl.BlockSpec(memory_space=pl.ANY),
            scratch_shapes=[pltpu.VMEM((2, ts * (d // LANES), LANES), F32),
                            pltpu.SemaphoreType.DMA((2,))],
        ),
        out_shape=jax.ShapeDtypeStruct((n_rows * (d // LANES), LANES), F32),
        compiler_params=pltpu.CompilerParams(dimension_semantics=("arbitrary",),
                                             vmem_limit_bytes=VMEM_LIMIT),
        name="scatter",
    )(dest, x1, g_ffn)


def _experts_kernel(be_ref, nused_ref, nexte_ref, xs_ref, wg_hbm, wu_hbm, wd_hbm, ys_ref,
                    wg_f, wu_f, wd_f, wg_s, wu_s, wd_s, sem):
    i = pl.program_id(0)
    e = be_ref[i]
    prev = be_ref[jnp.maximum(i - 1, 0)]
    active = i < nused_ref[0]

    def weight_copies(ex):
        return (pltpu.make_async_copy(wg_hbm.at[ex], wg_f, sem.at[0]),
                pltpu.make_async_copy(wu_hbm.at[ex], wu_f, sem.at[1]),
                pltpu.make_async_copy(wd_hbm.at[ex], wd_f, sem.at[2]))

    @pl.when(i == 0)
    def _():
        for cp in weight_copies(e):
            cp.start(priority=1)

    @pl.when(active & ((i == 0) | (e != prev)))
    def _():
        for cp in weight_copies(e):
            cp.wait()
        wg_s[...] = wg_f[...].astype(BF16)
        wu_s[...] = wu_f[...].astype(BF16)
        wd_s[...] = wd_f[...].astype(BF16)
        nxt = nexte_ref[e]

        @pl.when(nxt < N_EXPERTS)
        def _():
            for cp in weight_copies(nxt):
                cp.start(priority=1)

    @pl.when(active)
    def _():
        nt = wg_s.shape[0] // LANES
        xb = _load_row_tiles(xs_ref, ROW_BLOCK, nt).astype(BF16)
        gate = jnp.dot(xb, wg_s[...], preferred_element_type=F32)
        up = jnp.dot(xb, wu_s[...], preferred_element_type=F32)
        hid = gate * _sigmoid(gate) * up
        _store_row_tiles(ys_ref, jnp.dot(hid.astype(BF16), wd_s[...], preferred_element_type=F32))


def _experts(blk_expert, n_used, next_expert, xs, w_gate, w_up, w_down):
    d, de = w_gate.shape[-2:]
    blk_rows = ROW_BLOCK * (d // LANES)
    nb = xs.shape[0] // blk_rows

    def row_map(i, be, nu, ne):
        return (jnp.minimum(i, nu[0] - 1), 0)

    return pl.pallas_call(
        _experts_kernel,
        grid_spec=pltpu.PrefetchScalarGridSpec(
            num_scalar_prefetch=3,
            grid=(nb,),
            in_specs=[pl.BlockSpec((blk_rows, LANES), row_map)] + [pl.BlockSpec(memory_space=pl.ANY)] * 3,
            out_specs=pl.BlockSpec((blk_rows, LANES), row_map),
            scratch_shapes=[pltpu.VMEM((d, de), F32), pltpu.VMEM((d, de), F32), pltpu.VMEM((de, d), F32),
                            pltpu.VMEM((d, de), BF16), pltpu.VMEM((d, de), BF16), pltpu.VMEM((de, d), BF16),
                            pltpu.SemaphoreType.DMA((3,))],
        ),
        out_shape=jax.ShapeDtypeStruct(xs.shape, F32),
        compiler_params=pltpu.CompilerParams(dimension_semantics=("arbitrary",),
                                             vmem_limit_bytes=VMEM_LIMIT),
        name="experts",
    )(blk_expert, n_used, next_expert, xs, w_gate, w_up, w_down)


def _combine_kernel(dest_ref, x1_ref, meta_ref, gfin_ref, ys_ref, out_ref, ybuf, sem):
    te = x1_ref.shape[0]
    s = pl.program_id(0)
    nsteps = pl.num_programs(0)
    slot = s % 2

    nt = x1_ref.shape[1] // LANES

    def issue_step(step, sl):
        def issue(grp, carry):
            grp_off = pl.multiple_of(grp * (SUBLANES * nt), SUBLANES * nt)
            for j in range(SUBLANES):
                base = (step * te + grp * SUBLANES + j) * TOP_K
                for kslot in range(TOP_K):
                    d = pl.multiple_of(dest_ref[base + kslot], nt)
                    pltpu.make_async_copy(ys_ref.at[pl.ds(d, nt), :],
                                          ybuf.at[sl, kslot, pl.ds(grp_off + j * nt, nt), :],
                                          sem.at[sl]).start(priority=kslot)
            return carry
        lax.fori_loop(0, te // SUBLANES, issue, 0, unroll=ISSUE_UNROLL)

    @pl.when(s == 0)
    def _():
        issue_step(0, 0)

    @pl.when(s + 1 < nsteps)
    def _():
        issue_step(s + 1, 1 - slot)

    for kslot in range(TOP_K):
        pltpu.make_async_copy(ys_ref.at[pl.ds(0, te * nt), :], ybuf.at[slot, kslot], sem.at[slot]).wait()

    meta = meta_ref[...]
    y0 = _load_row_tiles(ybuf.at[slot, 0], te, nt)
    y1 = _load_row_tiles(ybuf.at[slot, 1], te, nt)
    x2 = x1_ref[...] + y0 * meta[:, 4:5] + y1 * meta[:, 5:6]
    out_ref[...] = _rms(x2, gfin_ref[...])


def _combine(dest, x1, meta, g_final, ys):
    t, d = x1.shape
    te = TE_COMBINE
    return pl.pallas_call(
        _combine_kernel,
        grid_spec=pltpu.PrefetchScalarGridSpec(
            num_scalar_prefetch=1,
            grid=(t // te,),
            in_specs=[pl.BlockSpec((te, d), lambda i, dest: (i, 0)),
                      pl.BlockSpec((te, 8), lambda i, dest: (i, 0)),
                      pl.BlockSpec((1, d), lambda i, dest: (0, 0)),
                      pl.BlockSpec(memory_space=pl.ANY)],
            out_specs=pl.BlockSpec((te, d), lambda i, dest: (i, 0)),
            scratch_shapes=[pltpu.VMEM((2, TOP_K, te * (d // LANES), LANES), F32),
                            pltpu.SemaphoreType.DMA((2,))],
        ),
        out_shape=jax.ShapeDtypeStruct((t, d), F32),
        compiler_params=pltpu.CompilerParams(dimension_semantics=("arbitrary",),
                                             vmem_limit_bytes=VMEM_LIMIT),
        name="combine",
    )(dest, x1, meta, g_final, ys)


def _constants(tm_merge):
    n = CHUNK
    tri = (jnp.arange(n)[:, None] >= jnp.arange(n)[None, :]).astype(BF16)
    head = jnp.arange(2 * LANES) // HEAD_DIM
    seg_ones = (head[:, None] == head[None, :]).astype(BF16)
    tril_strict = (jnp.arange(tm_merge)[:, None] > jnp.arange(tm_merge)[None, :]).astype(BF16)
    return tri, seg_ones, tril_strict


def kernel(x, mem, g_mix, g_mem, w_in, conv_w, mu_rkv, mu_wag, w_lora1, w_lora2, w0, a_lora1, a_lora2, a0, g_lora1, g_lora2, k_k, k_a, r_k, ln_x_w, ln_x_b, w_kv_mem, w_branch, w_gate, b_gate, w_o, g_ffn, w_router_group, b_router_group, w_router_expert, b_router_expert, w_exp_gate, w_exp_up, w_exp_down, g_final):
    assert g_mix.shape[0] == 1, "single-layer block"
    b, s, d = x.shape
    t = b * s
    db = D_BRANCH
    tri, seg_ones, tril_strict = _constants(TM_MERGE)
    row = lambda a: a.reshape(1, -1)
    pad_r = LANES - N_EXPERTS - N_GROUPS
    p = {
        "g_mix": row(g_mix[0]), "w_in": w_in[0].astype(BF16), "conv_w": conv_w[0].T,
        "mu_rkv": row(mu_rkv[0]), "mu_wag": mu_wag[0],
        "w_lora1": w_lora1[0].astype(BF16), "w_lora2": w_lora2[0].astype(BF16), "w0": row(w0[0]),
        "a_lora1": a_lora1[0].astype(BF16), "a_lora2": a_lora2[0].astype(BF16), "a0": row(a0[0]),
        "g_lora1": g_lora1[0].astype(BF16), "g_lora2": g_lora2[0].astype(BF16),
        "k_k": row(k_k[0]), "k_a": row(k_a[0]), "r_k": row(r_k[0]),
        "ln_x_w": row(ln_x_w[0]), "ln_x_b": row(ln_x_b[0]),
        "w_gate": w_gate[0].astype(BF16), "b_gate": row(b_gate[0]),
        "w_branch": w_branch[0].astype(BF16), "w_o": w_o[0].astype(BF16), "g_ffn": row(g_ffn[0]),
        "w_router": jnp.concatenate([w_router_expert[0], w_router_group[0],
                                     jnp.zeros((d, pad_r), F32)], axis=1),
        "b_router": row(jnp.concatenate([b_router_expert[0], b_router_group[0],
                                         jnp.zeros((pad_r,), F32)])),
        "tri": tri, "seg_ones": seg_ones, "tril_strict": tril_strict,
    }

    km, vm = _memkv(mem, row(g_mem[0]), w_kv_mem[0].astype(BF16))
    yconv, ymem, r, k, v, kkn, a, lw, g = _prologue(x, km, vm, p)
    yrwkv = _rwkv(r, k, v, kkn, a, lw, g, p)
    x1, meta, cnt = _merge(x.reshape(t, d), yconv.reshape(t, db), yrwkv.reshape(t, db),
                           ymem.reshape(t, db), p)

    counts = cnt[0, :N_EXPERTS].astype(jnp.int32)
    padded = ((counts + ROW_BLOCK - 1) // ROW_BLOCK) * ROW_BLOCK
    pad_end = jnp.cumsum(padded)
    pad_start = pad_end - padded
    n_blocks = (t * TOP_K) // ROW_BLOCK + N_EXPERTS
    e_idx = meta[:, 0:TOP_K].astype(jnp.int32)
    onehot = e_idx[:, :, None] == jnp.arange(N_EXPERTS, dtype=jnp.int32)[None, None, :]
    dest = (jnp.sum(jnp.where(onehot, pad_start[None, None, :], 0), axis=-1)
            + meta[:, TOP_K:2 * TOP_K].astype(jnp.int32)).reshape(t * TOP_K)
    dest = dest * (d // LANES)
    blk_start = jnp.arange(n_blocks, dtype=jnp.int32) * ROW_BLOCK
    blk_expert = jnp.minimum(jnp.sum((pad_end[None, :] <= blk_start[:, None]).astype(jnp.int32), axis=1),
                             N_EXPERTS - 1)
    n_used = (pad_end[-1:] // ROW_BLOCK).astype(jnp.int32)
    eids = jnp.arange(N_EXPERTS, dtype=jnp.int32)
    later_nonempty = (eids[None, :] > eids[:, None]) & (counts[None, :] > 0)
    next_expert = jnp.min(jnp.where(later_nonempty, eids[None, :], N_EXPERTS), axis=1)

    xs = _scatter(dest, x1, p["g_ffn"], n_blocks * ROW_BLOCK)
    ys = _experts(blk_expert, n_used, next_expert, xs, w_exp_gate[0], w_exp_up[0], w_exp_down[0])
    out = _combine(dest, x1, meta, row(g_final), ys)
    return out.reshape(b, s, d)
```

```python
import functools

import jax
import jax.numpy as jnp
from jax import lax
from jax.experimental import pallas as pl
from jax.experimental.pallas import tpu as pltpu

F32 = jnp.float32
BF16 = jnp.bfloat16

NORM_EPS = 1e-6
GN_EPS = 64e-5
D_BRANCH = 512
HEAD_DIM = 64
N_HEADS = 8
CHUNK = 64
CHUNKS_PER_ITER = 4
MEM_HEADS = 4
MEM_HEAD_DIM = 128
N_GROUPS = 8
EXPERTS_PER_GROUP = 8
N_EXPERTS = 64
TOP_K = 2
ROW_BLOCK = 128
XS_BUFFERS = 4
LANES = 128
VMEM_LIMIT = 56 * 1024 * 1024

TM_PROLOGUE = 512
TB_RWKV = 256
TM_MERGE = 512
TS_SCATTER = 256
TE_COMBINE = 256
SUBLANES = 8
ISSUE_UNROLL = 2


def _bdot(a, b):
    return jnp.dot(a.astype(BF16), b.astype(BF16), preferred_element_type=F32)


def _bdot_nt(a, b):
    return lax.dot_general(a.astype(BF16), b.astype(BF16), (((1,), (1,)), ((), ())),
                           preferred_element_type=F32)


def _split_terms(x, n_terms):
    terms = []
    for _ in range(n_terms):
        t = x.astype(BF16)
        terms.append(t)
        x = x - t.astype(F32)
    return terms


def _split_dot_left(m_bf16, x, n_terms):
    return sum(jnp.dot(m_bf16, t, preferred_element_type=F32) for t in _split_terms(x, n_terms))


def _head_sums(x, seg_bf16, n_terms):
    w = seg_bf16.shape[0]
    terms = _split_terms(x, n_terms)
    halves = [sum(jnp.dot(t[:, c:c + w], seg_bf16, preferred_element_type=F32) for t in terms)
              for c in range(0, x.shape[1], w)]
    return jnp.concatenate(halves, axis=1)


def _rms(x, g):
    return x * lax.rsqrt(jnp.mean(x * x, axis=-1, keepdims=True) + NORM_EPS) * g


def _sigmoid(x):
    return 1.0 / (1.0 + jnp.exp(-x))


def _const_spec(shape):
    n = len(shape)
    return pl.BlockSpec(shape, lambda *_: (0,) * n)


def _memkv_kernel(mem_ref, g_ref, w_ref, k_ref, v_ref):
    mn = _rms(mem_ref[0], g_ref[...])
    kv = _bdot(mn, w_ref[...])
    k_ref[0] = kv[:, :D_BRANCH].astype(BF16)
    v_ref[0] = kv[:, D_BRANCH:].astype(BF16)


def _memkv(mem, g_mem, w_kv):
    b, m, d = mem.shape
    return pl.pallas_call(
        _memkv_kernel,
        grid=(b,),
        in_specs=[pl.BlockSpec((1, m, d), lambda i: (i, 0, 0)),
                  _const_spec((1, d)), _const_spec((d, 2 * D_BRANCH))],
        out_specs=[pl.BlockSpec((1, m, D_BRANCH), lambda i: (i, 0, 0)),
                   pl.BlockSpec((1, m, D_BRANCH), lambda i: (i, 0, 0))],
        out_shape=[jax.ShapeDtypeStruct((b, m, D_BRANCH), BF16)] * 2,
        compiler_params=pltpu.CompilerParams(dimension_semantics=("arbitrary",),
                                             vmem_limit_bytes=VMEM_LIMIT),
        name="memkv",
    )(mem, g_mem, w_kv)


def _prologue_kernel(x_ref, gmix_ref, win_ref, convw_ref, murkv_ref, muwag_ref,
                     wl1_ref, wl2_ref, w0_ref, al1_ref, al2_ref, a0_ref, gl1_ref, gl2_ref,
                     kk_ref, ka_ref, seg_ref, km_ref, vm_ref,
                     yconv_ref, ymem_ref, r_ref, k_ref, v_ref, kkn_ref, a_ref, lw_ref, g_ref,
                     prev_h, prev_p, prev_cu):
    tm = x_ref.shape[1]
    db = D_BRANCH

    @pl.when(pl.program_id(1) == 0)
    def _():
        prev_h[...] = jnp.zeros_like(prev_h)
        prev_p[...] = jnp.zeros_like(prev_p)
        prev_cu[...] = jnp.zeros_like(prev_cu)

    rows = lax.broadcasted_iota(jnp.int32, (tm, 1), 0)

    def shift1(u, prev_row):
        return jnp.where(rows == 0, prev_row, pltpu.roll(u, 1, axis=0))

    h = _rms(x_ref[0], gmix_ref[...])
    proj = _bdot(h, win_ref[...])

    bg, cg, u = proj[:, :db], proj[:, db:2 * db], proj[:, 2 * db:3 * db]
    cu = cg * u
    cu1 = shift1(cu, prev_cu[1:2, :])
    cu2 = jnp.where(rows == 0, prev_cu[0:1, :],
                    jnp.where(rows == 1, prev_cu[1:2, :], pltpu.roll(cu, 2, axis=0)))
    conv = cu2 * convw_ref[0:1, :] + cu1 * convw_ref[1:2, :] + cu * convw_ref[2:3, :]
    yconv_ref[0] = (bg * conv).astype(BF16)
    prev_cu[...] = cu[tm - 2:tm, :]

    pr = proj[:, 3 * db:6 * db]
    prs = shift1(pr, prev_p[...])
    mixed = pr + (prs - pr) * murkv_ref[...]
    prev_p[...] = pr[tm - 1:tm, :]
    r, k, v = mixed[:, :db], mixed[:, db:2 * db], mixed[:, 2 * db:]

    dh = shift1(h, prev_h[...]) - h
    prev_h[...] = h[tm - 1:tm, :]
    xw = h + dh * muwag_ref[0:1, :]
    xa = h + dh * muwag_ref[1:2, :]
    xg = h + dh * muwag_ref[2:3, :]
    zz = w0_ref[...] + _bdot(jnp.tanh(_bdot(xw, wl1_ref[...])), wl2_ref[...])
    softplus = jnp.maximum(-zz, 0.0) + jnp.log(1.0 + jnp.exp(-jnp.abs(zz)))
    lw_ref[0] = -jnp.exp(-softplus - 0.5)
    a = _sigmoid(a0_ref[...] + _bdot(_bdot(xa, al1_ref[...]), al2_ref[...]))
    g_ref[0] = _bdot(_sigmoid(_bdot(xg, gl1_ref[...])), gl2_ref[...])

    kk = k * kk_ref[...]
    ss = _head_sums(kk * kk, seg_ref[...], 1)
    kkn_ref[0] = kk * lax.rsqrt(jnp.maximum(ss, 1e-24))
    k_ref[0] = k * (1.0 + (a - 1.0) * ka_ref[...])
    r_ref[0] = r
    v_ref[0] = v
    a_ref[0] = a

    q = proj[:, 6 * db:]
    scale = MEM_HEAD_DIM ** -0.5
    for hh in range(MEM_HEADS):
        sl = slice(hh * MEM_HEAD_DIM, (hh + 1) * MEM_HEAD_DIM)
        s = _bdot_nt(q[:, sl], km_ref[0, :, sl]) * scale
        p = jnp.exp(s - jnp.max(s, axis=-1, keepdims=True))
        o = _bdot(p, vm_ref[0, :, sl]) / jnp.sum(p, axis=-1, keepdims=True)
        ymem_ref[0, :, sl] = o.astype(BF16)


def _prologue(x, km, vm, p):
    b, s, d = x.shape
    tm = TM_PROLOGUE
    db = D_BRANCH
    m = km.shape[1]
    tok = lambda c: pl.BlockSpec((1, tm, c), lambda bi, i: (bi, i, 0))
    consts = [p["g_mix"], p["w_in"], p["conv_w"], p["mu_rkv"], p["mu_wag"],
              p["w_lora1"], p["w_lora2"], p["w0"], p["a_lora1"], p["a_lora2"], p["a0"],
              p["g_lora1"], p["g_lora2"], p["k_k"], p["k_a"], p["seg_ones"]]
    out_shapes = ([jax.ShapeDtypeStruct((b, s, db), BF16)] * 2
                  + [jax.ShapeDtypeStruct((b, s, db), F32)] * 7)
    return pl.pallas_call(
        _prologue_kernel,
        grid=(b, s // tm),
        in_specs=[tok(d)] + [_const_spec(c.shape) for c in consts]
                 + [pl.BlockSpec((1, m, db), lambda bi, i: (bi, 0, 0))] * 2,
        out_specs=[tok(db)] * 9,
        out_shape=out_shapes,
        scratch_shapes=[pltpu.VMEM((1, d), F32), pltpu.VMEM((1, 3 * db), F32),
                        pltpu.VMEM((2, db), F32)],
        compiler_params=pltpu.CompilerParams(dimension_semantics=("arbitrary", "arbitrary"),
                                             vmem_limit_bytes=VMEM_LIMIT),
        name="prologue",
    )(x, *consts, km, vm)


def _rwkv_kernel(r_ref, k_ref, v_ref, kk_ref, a_ref, lw_ref, g_ref, rk_ref, lnw_ref, lnb_ref,
                 tri_ref, seg_ref, out_ref, h_scr, y_scr):
    tb = r_ref.shape[1]
    n = HEAD_DIM
    c_len = CHUNK

    @pl.when(pl.program_id(1) == 0)
    def _():
        h_scr[...] = jnp.zeros_like(h_scr)

    row2 = lax.broadcasted_iota(jnp.int32, (c_len, 2 * c_len), 0)
    col2 = lax.broadcasted_iota(jnp.int32, (c_len, 2 * c_len), 1) & (c_len - 1)
    strict2 = col2 < row2
    incl2 = col2 <= row2
    eye = (lax.broadcasted_iota(jnp.int32, (c_len, n), 0)
           == lax.broadcasted_iota(jnp.int32, (c_len, n), 1)).astype(F32)
    zeros = jnp.zeros((c_len, n), F32)

    def chunk_inputs(c):
        rows = pl.ds(pl.multiple_of(c * c_len, c_len), c_len)
        r = r_ref[0, rows, :]
        k = k_ref[0, rows, :]
        v = v_ref[0, rows, :]
        kk = kk_ref[0, rows, :]
        a = a_ref[0, rows, :]
        lw = lw_ref[0, rows, :]
        gcum = _split_dot_left(tri_ref[...], lw, 2)
        e_pos = jnp.exp(gcum)
        e_neg = jnp.exp(-gcum)
        p_last = jnp.exp(gcum[c_len - 1:c_len, :])
        bb = kk * a * e_neg
        kb = k * e_neg
        return dict(rows=rows, v=v, p_last=p_last, rb=r * e_pos, ab=-kk * jnp.exp(gcum - lw), bb=bb, kb=kb,
                    bk_t=jnp.concatenate([bb * p_last, kb * p_last], axis=0).T)

    def chunk_group(it, carry):
        chunks = [chunk_inputs(it * CHUNKS_PER_ITER + ci) for ci in range(CHUNKS_PER_ITER)]
        units = [(ci, hd) for ci in range(CHUNKS_PER_ITER) for hd in range(N_HEADS)]
        nu = range(len(units))
        ls = [slice(hd * n, (hd + 1) * n) for _, hd in units]
        ch = [chunks[ci] for ci, _ in units]
        al = [ch[u]["ab"][:, ls[u]] for u in nu]
        rr = [ch[u]["rb"][:, ls[u]] for u in nu]
        vv = [ch[u]["v"][:, ls[u]] for u in nu]
        aa = [_bdot_nt(jnp.concatenate([al[u], rr[u]], axis=0),
                       jnp.concatenate([ch[u]["bb"][:, ls[u]], ch[u]["kb"][:, ls[u]]], axis=0)) for u in nu]
        top = [jnp.where(strict2, aa[u][:c_len], 0.0) for u in nu]
        bot = [jnp.where(incl2, aa[u][c_len:], 0.0) for u in nu]
        a_ab = [top[u][:, :c_len] for u in nu]
        t_inv = [eye + a_ab[u] for u in nu]
        x_pow = [_bdot(a_ab[u], a_ab[u]) for u in nu]
        av = [_bdot(top[u], jnp.concatenate([zeros, vv[u]], axis=0)) for u in nu]
        for lvl in range(5):
            if lvl < 4:
                z = [_bdot(jnp.concatenate([t_inv[u], x_pow[u]], axis=0), x_pow[u]) for u in nu]
                t_inv = [t_inv[u] + z[u][:c_len] for u in nu]
                x_pow = [z[u][c_len:] for u in nu]
            else:
                t_inv = [t_inv[u] + _bdot(t_inv[u], x_pow[u]) for u in nu]
        w12 = [_bdot(t_inv[u], jnp.concatenate([al[u], av[u]], axis=1)) for u in nu]
        z2 = []
        for u in nu:
            hd = units[u][1]
            rhs2 = jnp.concatenate([w12[u], jnp.concatenate([zeros, vv[u]], axis=1)], axis=0)
            lhs3 = jnp.concatenate([ch[u]["bk_t"][hd * n:(hd + 1) * n, :], bot[u]], axis=0)
            z2.append(_bdot(lhs3, rhs2))
        state = [h_scr[hd] for hd in range(N_HEADS)]
        for u in nu:
            hd = units[u][1]
            mq = z2[u][:, :n] + jnp.concatenate([zeros, rr[u]], axis=0)
            out = _bdot(mq, state[hd]) + z2[u][:, n:]
            p_col = jnp.sum(eye * ch[u]["p_last"][:, ls[u]], axis=1, keepdims=True)
            state[hd] = p_col * state[hd] + out[:c_len]
            y_scr[ch[u]["rows"], ls[u]] = out[c_len:]
        for hd in range(N_HEADS):
            h_scr[hd] = state[hd]
        return carry

    lax.fori_loop(0, tb // (c_len * CHUNKS_PER_ITER), chunk_group, 0)

    y = y_scr[...]
    seg = seg_ref[...]
    inv_n = 1.0 / n
    mu = _head_sums(y, seg, 2) * inv_n
    yc = y - mu
    var = _head_sums(yc * yc, seg, 1) * inv_n
    yn = yc * lax.rsqrt(var + GN_EPS) * lnw_ref[...] + lnb_ref[...]
    r = r_ref[0]
    bonus = _head_sums(r * k_ref[0] * rk_ref[...], seg, 1) * v_ref[0]
    out_ref[0] = ((yn + bonus) * g_ref[0]).astype(BF16)


def _rwkv(r, k, v, kkn, a, lw, g, p):
    b, s, db = r.shape
    tb = TB_RWKV
    tok = pl.BlockSpec((1, tb, db), lambda bi, i: (bi, i, 0))
    consts = [p["r_k"], p["ln_x_w"], p["ln_x_b"], p["tri"], p["seg_ones"]]
    return pl.pallas_call(
        _rwkv_kernel,
        grid=(b, s // tb),
        in_specs=[tok] * 7 + [_const_spec(c.shape) for c in consts],
        out_specs=tok,
        out_shape=jax.ShapeDtypeStruct((b, s, db), BF16),
        scratch_shapes=[pltpu.VMEM((N_HEADS, HEAD_DIM, HEAD_DIM), F32),
                        pltpu.VMEM((tb, db), F32)],
        compiler_params=pltpu.CompilerParams(dimension_semantics=("arbitrary", "arbitrary"),
                                             vmem_limit_bytes=VMEM_LIMIT),
        name="rwkv",
    )(r, k, v, kkn, a, lw, g, *consts)


def _merge_kernel(x_ref, yc_ref, yr_ref, ym_ref, gmix_ref, wgate_ref, bgate_ref, wbr_ref, wo_ref,
                  gffn_ref, wrt_ref, brt_ref, tril_ref,
                  x1_ref, meta_ref, cnt_ref, base_scr):
    tm, d = x_ref.shape

    @pl.when(pl.program_id(0) == 0)
    def _():
        base_scr[...] = jnp.zeros_like(base_scr)

    x = x_ref[...]
    hb = _rms(x, gmix_ref[...]).astype(BF16)
    z = jnp.zeros((tm, d), F32)
    for i, y_ref in enumerate((yc_ref, yr_ref, ym_ref)):
        cs = slice(i * d, (i + 1) * d)
        gate = _sigmoid(jnp.dot(hb, wgate_ref[:, cs], preferred_element_type=F32) + bgate_ref[:, cs])
        z = z + gate * jnp.dot(y_ref[...], wbr_ref[i], preferred_element_type=F32)
    x1 = x + _bdot(z, wo_ref[...])
    x1_ref[...] = x1

    h2 = _rms(x1, gffn_ref[...])
    h_hi, h_lo = _split_terms(h2, 2)
    w_hi, w_lo = _split_terms(wrt_ref[...], 2)
    logits = (jnp.dot(h_hi, w_hi, preferred_element_type=F32)
              + (jnp.dot(h_lo, w_hi, preferred_element_type=F32)
                 + jnp.dot(h_hi, w_lo, preferred_element_type=F32))) + brt_ref[...]
    lane = lax.broadcasted_iota(jnp.int32, (tm, LANES), 1)
    neg = jnp.float32(-jnp.inf)
    big = jnp.int32(1 << 20)
    gmask = (lane >= N_EXPERTS) & (lane < N_EXPERTS + N_GROUPS)
    glv = jnp.where(gmask, logits, neg)
    gmax = jnp.max(glv, axis=-1, keepdims=True)
    g_sel = jnp.min(jnp.where(glv == gmax, lane - N_EXPERTS, big), axis=-1, keepdims=True)
    g_w = 1.0 / jnp.sum(jnp.exp(glv - gmax), axis=-1, keepdims=True)
    emask = (lane < N_EXPERTS) & ((lane >> 3) == g_sel)
    elv = jnp.where(emask, logits, neg)
    emax = jnp.max(elv, axis=-1, keepdims=True)
    esum = jnp.sum(jnp.exp(elv - emax), axis=-1, keepdims=True)
    i1 = jnp.min(jnp.where(elv == emax, lane, big), axis=-1, keepdims=True)
    elv2 = jnp.where(lane == i1, neg, elv)
    m2 = jnp.max(elv2, axis=-1, keepdims=True)
    i2 = jnp.min(jnp.where(elv2 == m2, lane, big), axis=-1, keepdims=True)
    p1 = 1.0 / esum
    p2 = jnp.exp(m2 - emax) / esum
    c1 = g_w * p1 / (p1 + p2)
    c2 = g_w * p2 / (p1 + p2)

    oh1 = lane == i1
    oh2 = lane == i2
    onehot = jnp.where(oh1 | oh2, 1.0, 0.0)
    before = jnp.dot(tril_ref[...], onehot.astype(BF16), preferred_element_type=F32) + base_scr[...]
    rank1 = jnp.sum(jnp.where(oh1, before, 0.0), axis=-1, keepdims=True)
    rank2 = jnp.sum(jnp.where(oh2, before, 0.0), axis=-1, keepdims=True)
    new_base = base_scr[...] + jnp.sum(onehot, axis=0, keepdims=True)
    base_scr[...] = new_base
    cnt_ref[...] = jnp.broadcast_to(new_base, cnt_ref.shape)

    col = lax.broadcasted_iota(jnp.int32, (tm, 8), 1)
    meta = jnp.where(col == 0, i1.astype(F32),
           jnp.where(col == 1, i2.astype(F32),
           jnp.where(col == 2, rank1,
           jnp.where(col == 3, rank2,
           jnp.where(col == 4, c1,
           jnp.where(col == 5, c2, 0.0))))))
    meta_ref[...] = meta


def _merge(x2, yc, yr, ym, p):
    t, d = x2.shape
    tm = TM_MERGE
    db = D_BRANCH
    tok = lambda c: pl.BlockSpec((tm, c), lambda i: (i, 0))
    consts = [p["g_mix"], p["w_gate"], p["b_gate"], p["w_branch"], p["w_o"], p["g_ffn"],
              p["w_router"], p["b_router"], p["tril_strict"]]
    return pl.pallas_call(
        _merge_kernel,
        grid=(t // tm,),
        in_specs=[tok(d), tok(db), tok(db), tok(db)] + [_const_spec(c.shape) for c in consts],
        out_specs=[tok(d), tok(8), _const_spec((8, LANES))],
        out_shape=[jax.ShapeDtypeStruct((t, d), F32), jax.ShapeDtypeStruct((t, 8), F32),
                   jax.ShapeDtypeStruct((8, LANES), F32)],
        scratch_shapes=[pltpu.VMEM((1, LANES), F32)],
        compiler_params=pltpu.CompilerParams(dimension_semantics=("arbitrary",),
                                             vmem_limit_bytes=VMEM_LIMIT),
        name="merge",
    )(x2, yc, yr, ym, *consts)


def _store_row_tiles(ref2d, x):
    rows, d = x.shape
    nt = d // LANES
    for c in range(nt):
        ref2d[pl.ds(c, rows, stride=nt), :] = x[:, c * LANES:(c + 1) * LANES]


def _load_row_tiles(ref2d, rows, nt):
    return jnp.concatenate([ref2d[pl.ds(c, rows, stride=nt), :] for c in range(nt)], axis=1)


def _scatter_kernel(dest_ref, x1_ref, gffn_ref, xs_ref, hbuf, sem):
    ts, d_model = x1_ref.shape
    nt = d_model // LANES
    s = pl.program_id(0)
    slot = s % 2

    def wait_slot(sl):
        for _ in range(TOP_K):
            pltpu.make_async_copy(hbuf.at[sl], xs_ref.at[pl.ds(0, ts * nt), :], sem.at[sl]).wait()

    @pl.when(s >= 2)
    def _():
        wait_slot(slot)

    _store_row_tiles(hbuf.at[slot], _rms(x1_ref[...], gffn_ref[...]))

    def issue(grp, carry):
        grp_off = pl.multiple_of(grp * (SUBLANES * nt), SUBLANES * nt)
        for j in range(SUBLANES):
            base = (s * ts + grp * SUBLANES + j) * TOP_K
            for kslot in range(TOP_K):
                d = pl.multiple_of(dest_ref[base + kslot], nt)
                pltpu.make_async_copy(hbuf.at[slot, pl.ds(grp_off + j * nt, nt), :],
                                      xs_ref.at[pl.ds(d, nt), :], sem.at[slot]).start(priority=kslot)
        return carry

    lax.fori_loop(0, ts // SUBLANES, issue, 0, unroll=ISSUE_UNROLL)

    @pl.when(s == pl.num_programs(0) - 1)
    def _():
        @pl.when(s >= 1)
        def _():
            wait_slot(1 - slot)
        wait_slot(slot)


def _scatter(dest, x1, g_ffn, n_rows):
    t, d = x1.shape
    ts = TS_SCATTER
    return pl.pallas_call(
        _scatter_kernel,
        grid_spec=pltpu.PrefetchScalarGridSpec(
            num_scalar_prefetch=1,
            grid=(t // ts,),
            in_specs=[pl.BlockSpec((ts, d), lambda i, dest: (i, 0)),
                      pl.BlockSpec((1, d), lambda i, dest: (0, 0))],
            out_specs=pl.BlockSpec(memory_space=pl.ANY),
            scratch_shapes=[pltpu.VMEM((2, ts * (d // LANES), LANES), F32),
                            pltpu.SemaphoreType.DMA((2,))],
        ),
        out_shape=jax.ShapeDtypeStruct((n_rows * (d // LANES), LANES), F32),
        compiler_params=pltpu.CompilerParams(dimension_semantics=("arbitrary",),
                                             vmem_limit_bytes=VMEM_LIMIT),
        name="scatter",
    )(dest, x1, g_ffn)


def _experts_kernel(be_ref, nused_ref, nexte_ref, xs_ref, wg_hbm, wu_hbm, wd_hbm, ys_ref,
                    wg_f, wu_f, wd_f, wg_s, wu_s, wd_s, xbuf, ybuf, sem, xsem, ysem):
    i = pl.program_id(0)
    e = be_ref[i]
    prev = be_ref[jnp.maximum(i - 1, 0)]
    active = i < nused_ref[0]

    def weight_copies(ex):
        return (pltpu.make_async_copy(wg_hbm.at[ex], wg_f, sem.at[0]),
                pltpu.make_async_copy(wu_hbm.at[ex], wu_f, sem.at[1]),
                pltpu.make_async_copy(wd_hbm.at[ex], wd_f, sem.at[2]))

    @pl.when(i == 0)
    def _():
        for cp in weight_copies(e):
            cp.start(priority=1)

    @pl.when(active & ((i == 0) | (e != prev)))
    def _():
        for cp in weight_copies(e):
            cp.wait()
        wg_s[...] = wg_f[...].astype(BF16)
        wu_s[...] = wu_f[...].astype(BF16)
        wd_s[...] = wd_f[...].astype(BF16)
        nxt = nexte_ref[e]

        @pl.when(nxt < N_EXPERTS)
        def _():
            for cp in weight_copies(nxt):
                cp.start(priority=1)

    blk_rows = xbuf.shape[1]
    n_used = nused_ref[0]

    def xs_copy(blk, slot):
        return pltpu.make_async_copy(xs_ref.at[pl.ds(pl.multiple_of(blk * blk_rows, blk_rows), blk_rows), :],
                                     xbuf.at[slot], xsem.at[slot])

    def ys_copy(blk, slot):
        return pltpu.make_async_copy(ybuf.at[slot],
                                     ys_ref.at[pl.ds(pl.multiple_of(blk * blk_rows, blk_rows), blk_rows), :],
                                     ysem.at[slot])

    @pl.when(i == 0)
    def _():
        for ahead in range(XS_BUFFERS - 1):
            @pl.when(ahead < n_used)
            def _():
                xs_copy(ahead, ahead).start()

    @pl.when(active)
    def _():
        nt = wg_s.shape[0] // LANES
        fetch = i + (XS_BUFFERS - 1)

        @pl.when(fetch < n_used)
        def _():
            xs_copy(fetch, fetch % XS_BUFFERS).start()

        xs_copy(i, i % XS_BUFFERS).wait()
        oslot = i % 2

        @pl.when(i >= 2)
        def _():
            ys_copy(i - 2, oslot).wait()

        xb = _load_row_tiles(xbuf.at[i % XS_BUFFERS], ROW_BLOCK, nt).astype(BF16)
        gate = jnp.dot(xb, wg_s[...], preferred_element_type=F32)
        up = jnp.dot(xb, wu_s[...], preferred_element_type=F32)
        hid = gate * _sigmoid(gate) * up
        _store_row_tiles(ybuf.at[oslot], jnp.dot(hid.astype(BF16), wd_s[...], preferred_element_type=F32))
        ys_copy(i, oslot).start()

        @pl.when(i == n_used - 1)
        def _():
            @pl.when(i >= 1)
            def _():
                ys_copy(i - 1, 1 - oslot).wait()
            ys_copy(i, oslot).wait()


def _experts(blk_expert, n_used, next_expert, xs, w_gate, w_up, w_down):
    d, de = w_gate.shape[-2:]
    blk_rows = ROW_BLOCK * (d // LANES)
    nb = xs.shape[0] // blk_rows

    return pl.pallas_call(
        _experts_kernel,
        grid_spec=pltpu.PrefetchScalarGridSpec(
            num_scalar_prefetch=3,
            grid=(nb,),
            in_specs=[pl.BlockSpec(memory_space=pl.ANY)] * 4,
            out_specs=pl.BlockSpec(memory_space=pl.ANY),
            scratch_shapes=[pltpu.VMEM((d, de), F32), pltpu.VMEM((d, de), F32), pltpu.VMEM((de, d), F32),
                            pltpu.VMEM((d, de), BF16), pltpu.VMEM((d, de), BF16), pltpu.VMEM((de, d), BF16),
                            pltpu.VMEM((XS_BUFFERS, blk_rows, LANES), F32),
                            pltpu.VMEM((2, blk_rows, LANES), F32),
                            pltpu.SemaphoreType.DMA((3,)), pltpu.SemaphoreType.DMA((XS_BUFFERS,)),
                            pltpu.SemaphoreType.DMA((2,))],
        ),
        out_shape=jax.ShapeDtypeStruct(xs.shape, F32),
        compiler_params=pltpu.CompilerParams(dimension_semantics=("arbitrary",),
                                             vmem_limit_bytes=VMEM_LIMIT),
        name="experts",
    )(blk_expert, n_used, next_expert, xs, w_gate, w_up, w_down)


def _combine_kernel(dest_ref, x1_ref, meta_ref, gfin_ref, ys_ref, out_ref, ybuf, sem):
    te = x1_ref.shape[0]
    s = pl.program_id(0)
    nsteps = pl.num_programs(0)
    slot = s % 2

    nt = x1_ref.shape[1] // LANES

    def issue_step(step, sl):
        def issue(grp, carry):
            grp_off = pl.multiple_of(grp * (SUBLANES * nt), SUBLANES * nt)
            for j in range(SUBLANES):
                base = (step * te + grp * SUBLANES + j) * TOP_K
                for kslot in range(TOP_K):
                    d = pl.multiple_of(dest_ref[base + kslot], nt)
                    pltpu.make_async_copy(ys_ref.at[pl.ds(d, nt), :],
                                          ybuf.at[sl, kslot, pl.ds(grp_off + j * nt, nt), :],
                                          sem.at[sl]).start(priority=kslot)
            return carry
        lax.fori_loop(0, te // SUBLANES, issue, 0, unroll=ISSUE_UNROLL)

    @pl.when(s == 0)
    def _():
        issue_step(0, 0)

    @pl.when(s + 1 < nsteps)
    def _():
        issue_step(s + 1, 1 - slot)

    for kslot in range(TOP_K):
        pltpu.make_async_copy(ys_ref.at[pl.ds(0, te * nt), :], ybuf.at[slot, kslot], sem.at[slot]).wait()

    meta = meta_ref[...]
    y0 = _load_row_tiles(ybuf.at[slot, 0], te, nt)
    y1 = _load_row_tiles(ybuf.at[slot, 1], te, nt)
    x2 = x1_ref[...] + y0 * meta[:, 4:5] + y1 * meta[:, 5:6]
    out_ref[...] = _rms(x2, gfin_ref[...])


def _combine(dest, x1, meta, g_final, ys):
    t, d = x1.shape
    te = TE_COMBINE
    return pl.pallas_call(
        _combine_kernel,
        grid_spec=pltpu.PrefetchScalarGridSpec(
            num_scalar_prefetch=1,
            grid=(t // te,),
            in_specs=[pl.BlockSpec((te, d), lambda i, dest: (i, 0)),
                      pl.BlockSpec((te, 8), lambda i, dest: (i, 0)),
                      pl.BlockSpec((1, d), lambda i, dest: (0, 0)),
                      pl.BlockSpec(memory_space=pl.ANY)],
            out_specs=pl.BlockSpec((te, d), lambda i, dest: (i, 0)),
            scratch_shapes=[pltpu.VMEM((2, TOP_K, te * (d // LANES), LANES), F32),
                            pltpu.SemaphoreType.DMA((2,))],
        ),
        out_shape=jax.ShapeDtypeStruct((t, d), F32),
        compiler_params=pltpu.CompilerParams(dimension_semantics=("arbitrary",),
                                             vmem_limit_bytes=VMEM_LIMIT),
        name="combine",
    )(dest, x1, meta, g_final, ys)


def _constants(tm_merge):
    n = CHUNK
    tri = (jnp.arange(n)[:, None] >= jnp.arange(n)[None, :]).astype(BF16)
    head = jnp.arange(2 * LANES) // HEAD_DIM
    seg_ones = (head[:, None] == head[None, :]).astype(BF16)
    tril_strict = (jnp.arange(tm_merge)[:, None] > jnp.arange(tm_merge)[None, :]).astype(BF16)
    return tri, seg_ones, tril_strict


def kernel(x, mem, g_mix, g_mem, w_in, conv_w, mu_rkv, mu_wag, w_lora1, w_lora2, w0, a_lora1, a_lora2, a0, g_lora1, g_lora2, k_k, k_a, r_k, ln_x_w, ln_x_b, w_kv_mem, w_branch, w_gate, b_gate, w_o, g_ffn, w_router_group, b_router_group, w_router_expert, b_router_expert, w_exp_gate, w_exp_up, w_exp_down, g_final):
    assert g_mix.shape[0] == 1, "single-layer block"
    b, s, d = x.shape
    t = b * s
    db = D_BRANCH
    tri, seg_ones, tril_strict = _constants(TM_MERGE)
    row = lambda a: a.reshape(1, -1)
    pad_r = LANES - N_EXPERTS - N_GROUPS
    p = {
        "g_mix": row(g_mix[0]), "w_in": w_in[0].astype(BF16), "conv_w": conv_w[0].T,
        "mu_rkv": row(mu_rkv[0]), "mu_wag": mu_wag[0],
        "w_lora1": w_lora1[0].astype(BF16), "w_lora2": w_lora2[0].astype(BF16), "w0": row(w0[0]),
        "a_lora1": a_lora1[0].astype(BF16), "a_lora2": a_lora2[0].astype(BF16), "a0": row(a0[0]),
        "g_lora1": g_lora1[0].astype(BF16), "g_lora2": g_lora2[0].astype(BF16),
        "k_k": row(k_k[0]), "k_a": row(k_a[0]), "r_k": row(r_k[0]),
        "ln_x_w": row(ln_x_w[0]), "ln_x_b": row(ln_x_b[0]),
        "w_gate": w_gate[0].astype(BF16), "b_gate": row(b_gate[0]),
        "w_branch": w_branch[0].astype(BF16), "w_o": w_o[0].astype(BF16), "g_ffn": row(g_ffn[0]),
        "w_router": jnp.concatenate([w_router_expert[0], w_router_group[0],
                                     jnp.zeros((d, pad_r), F32)], axis=1),
        "b_router": row(jnp.concatenate([b_router_expert[0], b_router_group[0],
                                         jnp.zeros((pad_r,), F32)])),
        "tri": tri, "seg_ones": seg_ones, "tril_strict": tril_strict,
    }

    km, vm = _memkv(mem, row(g_mem[0]), w_kv_mem[0].astype(BF16))
    yconv, ymem, r, k, v, kkn, a, lw, g = _prologue(x, km, vm, p)
    yrwkv = _rwkv(r, k, v, kkn, a, lw, g, p)
    x1, meta, cnt = _merge(x.reshape(t, d), yconv.reshape(t, db), yrwkv.reshape(t, db),
                           ymem.reshape(t, db), p)

    counts = cnt[0, :N_EXPERTS].astype(jnp.int32)
    padded = ((counts + ROW_BLOCK - 1) // ROW_BLOCK) * ROW_BLOCK
    pad_end = jnp.cumsum(padded)
    pad_start = pad_end - padded
    n_blocks = (t * TOP_K) // ROW_BLOCK + N_EXPERTS
    e_idx = meta[:, 0:TOP_K].astype(jnp.int32)
    onehot = e_idx[:, :, None] == jnp.arange(N_EXPERTS, dtype=jnp.int32)[None, None, :]
    dest = (jnp.sum(jnp.where(onehot, pad_start[None, None, :], 0), axis=-1)
            + meta[:, TOP_K:2 * TOP_K].astype(jnp.int32)).reshape(t * TOP_K)
    dest = dest * (d // LANES)
    blk_start = jnp.arange(n_blocks, dtype=jnp.int32) * ROW_BLOCK
    blk_expert = jnp.minimum(jnp.sum((pad_end[None, :] <= blk_start[:, None]).astype(jnp.int32), axis=1),
                             N_EXPERTS - 1)
    n_used = (pad_end[-1:] // ROW_BLOCK).astype(jnp.int32)
    eids = jnp.arange(N_EXPERTS, dtype=jnp.int32)
    later_nonempty = (eids[None, :] > eids[:, None]) & (counts[None, :] > 0)
    next_expert = jnp.min(jnp.where(later_nonempty, eids[None, :], N_EXPERTS), axis=1)

    xs = _scatter(dest, x1, p["g_ffn"], n_blocks * ROW_BLOCK)
    ys = _experts(blk_expert, n_used, next_expert, xs, w_exp_gate[0], w_exp_up[0], w_exp_down[0])
    out = _combine(dest, x1, meta, row(g_final), ys)
    return out.reshape(b, s, d)
```

```python
import functools

import jax
import jax.numpy as jnp
from jax import lax
from jax.experimental import pallas as pl
from jax.experimental.pallas import tpu as pltpu

F32 = jnp.float32
BF16 = jnp.bfloat16

NORM_EPS = 1e-6
GN_EPS = 64e-5
D_BRANCH = 512
HEAD_DIM = 64
N_HEADS = 8
CHUNK = 64
CHUNKS_PER_ITER = 4
MEM_HEADS = 4
MEM_HEAD_DIM = 128
N_GROUPS = 8
EXPERTS_PER_GROUP = 8
N_EXPERTS = 64
TOP_K = 2
ROW_BLOCK = 128
XS_BUFFERS = 4
LANES = 128
VMEM_LIMIT = 56 * 1024 * 1024

TM_PROLOGUE = 512
TB_RWKV = 256
TM_MERGE = 512
TS_SCATTER = 256
TE_COMBINE = 256
SUBLANES = 8
ISSUE_UNROLL = 2


def _bdot(a, b):
    return jnp.dot(a.astype(BF16), b.astype(BF16), preferred_element_type=F32)


def _bdot_nt(a, b):
    return lax.dot_general(a.astype(BF16), b.astype(BF16), (((1,), (1,)), ((), ())),
                           preferred_element_type=F32)


def _split_terms(x, n_terms):
    terms = []
    for _ in range(n_terms):
        t = x.astype(BF16)
        terms.append(t)
        x = x - t.astype(F32)
    return terms


def _split_dot_left(m_bf16, x, n_terms):
    return sum(jnp.dot(m_bf16, t, preferred_element_type=F32) for t in _split_terms(x, n_terms))


def _head_sums(x, seg_bf16, n_terms):
    w = seg_bf16.shape[0]
    terms = _split_terms(x, n_terms)
    halves = [sum(jnp.dot(t[:, c:c + w], seg_bf16, preferred_element_type=F32) for t in terms)
              for c in range(0, x.shape[1], w)]
    return jnp.concatenate(halves, axis=1)


def _rms(x, g):
    return x * lax.rsqrt(jnp.mean(x * x, axis=-1, keepdims=True) + NORM_EPS) * g


def _sigmoid(x):
    return 1.0 / (1.0 + jnp.exp(-x))


def _const_spec(shape):
    n = len(shape)
    return pl.BlockSpec(shape, lambda *_: (0,) * n)


def _memkv_kernel(mem_ref, g_ref, w_ref, k_ref, v_ref):
    mn = _rms(mem_ref[0], g_ref[...])
    kv = _bdot(mn, w_ref[...])
    k_ref[0] = kv[:, :D_BRANCH].astype(BF16)
    v_ref[0] = kv[:, D_BRANCH:].astype(BF16)


def _memkv(mem, g_mem, w_kv):
    b, m, d = mem.shape
    return pl.pallas_call(
        _memkv_kernel,
        grid=(b,),
        in_specs=[pl.BlockSpec((1, m, d), lambda i: (i, 0, 0)),
                  _const_spec((1, d)), _const_spec((d, 2 * D_BRANCH))],
        out_specs=[pl.BlockSpec((1, m, D_BRANCH), lambda i: (i, 0, 0)),
                   pl.BlockSpec((1, m, D_BRANCH), lambda i: (i, 0, 0))],
        out_shape=[jax.ShapeDtypeStruct((b, m, D_BRANCH), BF16)] * 2,
        compiler_params=pltpu.CompilerParams(dimension_semantics=("arbitrary",),
                                             vmem_limit_bytes=VMEM_LIMIT),
        name="memkv",
    )(mem, g_mem, w_kv)


def _prologue_kernel(x_ref, gmix_ref, win_ref, convw_ref, murkv_ref, muwag_ref,
                     wl1_ref, wl2_ref, w0_ref, al1_ref, al2_ref, a0_ref, gl1_ref, gl2_ref,
                     kk_ref, ka_ref, seg_ref, km_ref, vm_ref,
                     yconv_ref, ymem_ref, r_ref, k_ref, v_ref, kkn_ref, a_ref, lw_ref, g_ref,
                     prev_h, prev_p, prev_cu):
    tm = x_ref.shape[0]
    db = D_BRANCH

    @pl.when(pl.program_id(1) == 0)
    def _():
        prev_h[...] = jnp.zeros_like(prev_h)
        prev_p[...] = jnp.zeros_like(prev_p)
        prev_cu[...] = jnp.zeros_like(prev_cu)

    rows = lax.broadcasted_iota(jnp.int32, (tm, 1), 0)

    def shift1(u, prev_row):
        return jnp.where(rows == 0, prev_row, pltpu.roll(u, 1, axis=0))

    h = _rms(x_ref[...], gmix_ref[...])
    proj = _bdot(h, win_ref[...])

    bg, cg, u = proj[:, :db], proj[:, db:2 * db], proj[:, 2 * db:3 * db]
    cu = cg * u
    cu1 = shift1(cu, prev_cu[1:2, :])
    cu2 = jnp.where(rows == 0, prev_cu[0:1, :],
                    jnp.where(rows == 1, prev_cu[1:2, :], pltpu.roll(cu, 2, axis=0)))
    conv = cu2 * convw_ref[0:1, :] + cu1 * convw_ref[1:2, :] + cu * convw_ref[2:3, :]
    yconv_ref[...] = (bg * conv).astype(BF16)
    prev_cu[...] = cu[tm - 2:tm, :]

    pr = proj[:, 3 * db:6 * db]
    prs = shift1(pr, prev_p[...])
    mixed = pr + (prs - pr) * murkv_ref[...]
    prev_p[...] = pr[tm - 1:tm, :]
    r, k, v = mixed[:, :db], mixed[:, db:2 * db], mixed[:, 2 * db:]

    dh = shift1(h, prev_h[...]) - h
    prev_h[...] = h[tm - 1:tm, :]
    xw = h + dh * muwag_ref[0:1, :]
    xa = h + dh * muwag_ref[1:2, :]
    xg = h + dh * muwag_ref[2:3, :]
    zz = w0_ref[...] + _bdot(jnp.tanh(_bdot(xw, wl1_ref[...])), wl2_ref[...])
    softplus = jnp.maximum(-zz, 0.0) + jnp.log(1.0 + jnp.exp(-jnp.abs(zz)))
    lw_ref[...] = -jnp.exp(-softplus - 0.5)
    a = _sigmoid(a0_ref[...] + _bdot(_bdot(xa, al1_ref[...]), al2_ref[...]))
    g_ref[...] = _bdot(_sigmoid(_bdot(xg, gl1_ref[...])), gl2_ref[...])

    kk = k * kk_ref[...]
    ss = _head_sums(kk * kk, seg_ref[...], 1)
    kkn_ref[...] = kk * lax.rsqrt(jnp.maximum(ss, 1e-24))
    k_ref[...] = k * (1.0 + (a - 1.0) * ka_ref[...])
    r_ref[...] = r
    v_ref[...] = v
    a_ref[...] = a

    q = proj[:, 6 * db:]
    scale = MEM_HEAD_DIM ** -0.5
    for hh in range(MEM_HEADS):
        sl = slice(hh * MEM_HEAD_DIM, (hh + 1) * MEM_HEAD_DIM)
        s = _bdot_nt(q[:, sl], km_ref[0, :, sl]) * scale
        p = jnp.exp(s - jnp.max(s, axis=-1, keepdims=True))
        o = _bdot(p, vm_ref[0, :, sl]) / jnp.sum(p, axis=-1, keepdims=True)
        ymem_ref[:, sl] = o.astype(BF16)


def _prologue(x2, b, km, vm, p):
    t, d = x2.shape
    s = t // b
    tm = TM_PROLOGUE
    db = D_BRANCH
    m = km.shape[1]
    steps = s // tm
    tok = lambda c: pl.BlockSpec((tm, c), lambda bi, i: (bi * steps + i, 0))
    consts = [p["g_mix"], p["w_in"], p["conv_w"], p["mu_rkv"], p["mu_wag"],
              p["w_lora1"], p["w_lora2"], p["w0"], p["a_lora1"], p["a_lora2"], p["a0"],
              p["g_lora1"], p["g_lora2"], p["k_k"], p["k_a"], p["seg_ones"]]
    out_shapes = ([jax.ShapeDtypeStruct((t, db), BF16)] * 2
                  + [jax.ShapeDtypeStruct((t, db), F32)] * 7)
    return pl.pallas_call(
        _prologue_kernel,
        grid=(b, steps),
        in_specs=[tok(d)] + [_const_spec(c.shape) for c in consts]
                 + [pl.BlockSpec((1, m, db), lambda bi, i: (bi, 0, 0))] * 2,
        out_specs=[tok(db)] * 9,
        out_shape=out_shapes,
        scratch_shapes=[pltpu.VMEM((1, d), F32), pltpu.VMEM((1, 3 * db), F32),
                        pltpu.VMEM((2, db), F32)],
        compiler_params=pltpu.CompilerParams(dimension_semantics=("arbitrary", "arbitrary"),
                                             vmem_limit_bytes=VMEM_LIMIT),
        name="prologue",
    )(x2, *consts, km, vm)


def _rwkv_kernel(r_ref, k_ref, v_ref, kk_ref, a_ref, lw_ref, g_ref, rk_ref, lnw_ref, lnb_ref,
                 tri_ref, seg_ref, out_ref, h_scr, y_scr):
    tb = r_ref.shape[0]
    n = HEAD_DIM
    c_len = CHUNK

    @pl.when(pl.program_id(1) == 0)
    def _():
        h_scr[...] = jnp.zeros_like(h_scr)

    row2 = lax.broadcasted_iota(jnp.int32, (c_len, 2 * c_len), 0)
    col2 = lax.broadcasted_iota(jnp.int32, (c_len, 2 * c_len), 1) & (c_len - 1)
    strict2 = col2 < row2
    incl2 = col2 <= row2
    eye = (lax.broadcasted_iota(jnp.int32, (c_len, n), 0)
           == lax.broadcasted_iota(jnp.int32, (c_len, n), 1)).astype(F32)
    zeros = jnp.zeros((c_len, n), F32)

    def chunk_inputs(c):
        rows = pl.ds(pl.multiple_of(c * c_len, c_len), c_len)
        r = r_ref[rows, :]
        k = k_ref[rows, :]
        v = v_ref[rows, :]
        kk = kk_ref[rows, :]
        a = a_ref[rows, :]
        lw = lw_ref[rows, :]
        gcum = _split_dot_left(tri_ref[...], lw, 2)
        e_pos = jnp.exp(gcum)
        e_neg = jnp.exp(-gcum)
        p_last = jnp.exp(gcum[c_len - 1:c_len, :])
        bb = kk * a * e_neg
        kb = k * e_neg
        return dict(rows=rows, v=v, p_last=p_last, rb=r * e_pos, ab=-kk * jnp.exp(gcum - lw), bb=bb, kb=kb,
                    bk_t=jnp.concatenate([bb * p_last, kb * p_last], axis=0).T)

    def chunk_group(it, carry):
        chunks = [chunk_inputs(it * CHUNKS_PER_ITER + ci) for ci in range(CHUNKS_PER_ITER)]
        units = [(ci, hd) for ci in range(CHUNKS_PER_ITER) for hd in range(N_HEADS)]
        nu = range(len(units))
        ls = [slice(hd * n, (hd + 1) * n) for _, hd in units]
        ch = [chunks[ci] for ci, _ in units]
        al = [ch[u]["ab"][:, ls[u]] for u in nu]
        rr = [ch[u]["rb"][:, ls[u]] for u in nu]
        vv = [ch[u]["v"][:, ls[u]] for u in nu]
        aa = [_bdot_nt(jnp.concatenate([al[u], rr[u]], axis=0),
                       jnp.concatenate([ch[u]["bb"][:, ls[u]], ch[u]["kb"][:, ls[u]]], axis=0)) for u in nu]
        top = [jnp.where(strict2, aa[u][:c_len], 0.0) for u in nu]
        bot = [jnp.where(incl2, aa[u][c_len:], 0.0) for u in nu]
        a_ab = [top[u][:, :c_len] for u in nu]
        t_inv = [eye + a_ab[u] for u in nu]
        x_pow = [_bdot(a_ab[u], a_ab[u]) for u in nu]
        av = [_bdot(top[u], jnp.concatenate([zeros, vv[u]], axis=0)) for u in nu]
        for lvl in range(5):
            if lvl < 4:
                z = [_bdot(jnp.concatenate([t_inv[u], x_pow[u]], axis=0), x_pow[u]) for u in nu]
                t_inv = [t_inv[u] + z[u][:c_len] for u in nu]
                x_pow = [z[u][c_len:] for u in nu]
            else:
                t_inv = [t_inv[u] + _bdot(t_inv[u], x_pow[u]) for u in nu]
        w12 = [_bdot(t_inv[u], jnp.concatenate([al[u], av[u]], axis=1)) for u in nu]
        z2 = []
        for u in nu:
            hd = units[u][1]
            rhs2 = jnp.concatenate([w12[u], jnp.concatenate([zeros, vv[u]], axis=1)], axis=0)
            lhs3 = jnp.concatenate([ch[u]["bk_t"][hd * n:(hd + 1) * n, :], bot[u]], axis=0)
            z2.append(_bdot(lhs3, rhs2))
        state = [h_scr[hd] for hd in range(N_HEADS)]
        for u in nu:
            hd = units[u][1]
            mq = z2[u][:, :n] + jnp.concatenate([zeros, rr[u]], axis=0)
            out = _bdot(mq, state[hd]) + z2[u][:, n:]
            p_col = jnp.sum(eye * ch[u]["p_last"][:, ls[u]], axis=1, keepdims=True)
            state[hd] = p_col * state[hd] + out[:c_len]
            y_scr[ch[u]["rows"], ls[u]] = out[c_len:]
        for hd in range(N_HEADS):
            h_scr[hd] = state[hd]
        return carry

    lax.fori_loop(0, tb // (c_len * CHUNKS_PER_ITER), chunk_group, 0)

    y = y_scr[...]
    seg = seg_ref[...]
    inv_n = 1.0 / n
    mu = _head_sums(y, seg, 2) * inv_n
    yc = y - mu
    var = _head_sums(yc * yc, seg, 1) * inv_n
    yn = yc * lax.rsqrt(var + GN_EPS) * lnw_ref[...] + lnb_ref[...]
    bonus = _head_sums(r_ref[...] * k_ref[...] * rk_ref[...], seg, 1) * v_ref[...]
    out_ref[...] = ((yn + bonus) * g_ref[...]).astype(BF16)


def _rwkv(r, k, v, kkn, a, lw, g, b, p):
    t, db = r.shape
    tb = TB_RWKV
    steps = t // b // tb
    tok = pl.BlockSpec((tb, db), lambda bi, i: (bi * steps + i, 0))
    consts = [p["r_k"], p["ln_x_w"], p["ln_x_b"], p["tri"], p["seg_ones"]]
    return pl.pallas_call(
        _rwkv_kernel,
        grid=(b, steps),
        in_specs=[tok] * 7 + [_const_spec(c.shape) for c in consts],
        out_specs=tok,
        out_shape=jax.ShapeDtypeStruct((t, db), BF16),
        scratch_shapes=[pltpu.VMEM((N_HEADS, HEAD_DIM, HEAD_DIM), F32),
                        pltpu.VMEM((tb, db), F32)],
        compiler_params=pltpu.CompilerParams(dimension_semantics=("arbitrary", "arbitrary"),
                                             vmem_limit_bytes=VMEM_LIMIT),
        name="rwkv",
    )(r, k, v, kkn, a, lw, g, *consts)


def _merge_kernel(x_ref, yc_ref, yr_ref, ym_ref, gmix_ref, wgate_ref, bgate_ref, wbr_ref, wo_ref,
                  gffn_ref, wrt_ref, brt_ref, tril_ref,
                  x1_ref, meta_ref, cnt_ref, base_scr):
    tm, d = x_ref.shape

    @pl.when(pl.program_id(0) == 0)
    def _():
        base_scr[...] = jnp.zeros_like(base_scr)

    x = x_ref[...]
    hb = _rms(x, gmix_ref[...]).astype(BF16)
    z = jnp.zeros((tm, d), F32)
    for i, y_ref in enumerate((yc_ref, yr_ref, ym_ref)):
        cs = slice(i * d, (i + 1) * d)
        gate = _sigmoid(jnp.dot(hb, wgate_ref[:, cs], preferred_element_type=F32) + bgate_ref[:, cs])
        z = z + gate * jnp.dot(y_ref[...], wbr_ref[i], preferred_element_type=F32)
    x1 = x + _bdot(z, wo_ref[...])
    x1_ref[...] = x1

    h2 = _rms(x1, gffn_ref[...])
    h_hi, h_lo = _split_terms(h2, 2)
    w_hi, w_lo = _split_terms(wrt_ref[...], 2)
    logits = (jnp.dot(h_hi, w_hi, preferred_element_type=F32)
              + (jnp.dot(h_lo, w_hi, preferred_element_type=F32)
                 + jnp.dot(h_hi, w_lo, preferred_element_type=F32))) + brt_ref[...]
    lane = lax.broadcasted_iota(jnp.int32, (tm, LANES), 1)
    neg = jnp.float32(-jnp.inf)
    big = jnp.int32(1 << 20)
    gmask = (lane >= N_EXPERTS) & (lane < N_EXPERTS + N_GROUPS)
    glv = jnp.where(gmask, logits, neg)
    gmax = jnp.max(glv, axis=-1, keepdims=True)
    g_sel = jnp.min(jnp.where(glv == gmax, lane - N_EXPERTS, big), axis=-1, keepdims=True)
    g_w = 1.0 / jnp.sum(jnp.exp(glv - gmax), axis=-1, keepdims=True)
    emask = (lane < N_EXPERTS) & ((lane >> 3) == g_sel)
    elv = jnp.where(emask, logits, neg)
    emax = jnp.max(elv, axis=-1, keepdims=True)
    esum = jnp.sum(jnp.exp(elv - emax), axis=-1, keepdims=True)
    i1 = jnp.min(jnp.where(elv == emax, lane, big), axis=-1, keepdims=True)
    elv2 = jnp.where(lane == i1, neg, elv)
    m2 = jnp.max(elv2, axis=-1, keepdims=True)
    i2 = jnp.min(jnp.where(elv2 == m2, lane, big), axis=-1, keepdims=True)
    p1 = 1.0 / esum
    p2 = jnp.exp(m2 - emax) / esum
    c1 = g_w * p1 / (p1 + p2)
    c2 = g_w * p2 / (p1 + p2)

    oh1 = lane == i1
    oh2 = lane == i2
    onehot = jnp.where(oh1 | oh2, 1.0, 0.0)
    before = jnp.dot(tril_ref[...], onehot.astype(BF16), preferred_element_type=F32) + base_scr[...]
    rank1 = jnp.sum(jnp.where(oh1, before, 0.0), axis=-1, keepdims=True)
    rank2 = jnp.sum(jnp.where(oh2, before, 0.0), axis=-1, keepdims=True)
    new_base = base_scr[...] + jnp.sum(onehot, axis=0, keepdims=True)
    base_scr[...] = new_base
    cnt_ref[...] = jnp.broadcast_to(new_base, cnt_ref.shape)

    col = lax.broadcasted_iota(jnp.int32, (tm, 8), 1)
    meta = jnp.where(col == 0, i1.astype(F32),
           jnp.where(col == 1, i2.astype(F32),
           jnp.where(col == 2, rank1,
           jnp.where(col == 3, rank2,
           jnp.where(col == 4, c1,
           jnp.where(col == 5, c2, 0.0))))))
    meta_ref[...] = meta


def _merge(x2, yc, yr, ym, p):
    t, d = x2.shape
    tm = TM_MERGE
    db = D_BRANCH
    tok = lambda c: pl.BlockSpec((tm, c), lambda i: (i, 0))
    consts = [p["g_mix"], p["w_gate"], p["b_gate"], p["w_branch"], p["w_o"], p["g_ffn"],
              p["w_router"], p["b_router"], p["tril_strict"]]
    return pl.pallas_call(
        _merge_kernel,
        grid=(t // tm,),
        in_specs=[tok(d), tok(db), tok(db), tok(db)] + [_const_spec(c.shape) for c in consts],
        out_specs=[tok(d), tok(8), _const_spec((8, LANES))],
        out_shape=[jax.ShapeDtypeStruct((t, d), F32), jax.ShapeDtypeStruct((t, 8), F32),
                   jax.ShapeDtypeStruct((8, LANES), F32)],
        scratch_shapes=[pltpu.VMEM((1, LANES), F32)],
        compiler_params=pltpu.CompilerParams(dimension_semantics=("arbitrary",),
                                             vmem_limit_bytes=VMEM_LIMIT),
        name="merge",
    )(x2, yc, yr, ym, *consts)


def _store_row_tiles(ref2d, x):
    rows, d = x.shape
    nt = d // LANES
    for c in range(nt):
        ref2d[pl.ds(c, rows, stride=nt), :] = x[:, c * LANES:(c + 1) * LANES]


def _load_row_tiles(ref2d, rows, nt):
    return jnp.concatenate([ref2d[pl.ds(c, rows, stride=nt), :] for c in range(nt)], axis=1)


def _scatter_kernel(dest_ref, x1_ref, gffn_ref, xs_ref, hbuf, sem):
    ts, d_model = x1_ref.shape
    nt = d_model // LANES
    s = pl.program_id(0)
    slot = s % 2

    def wait_slot(sl):
        for _ in range(TOP_K):
            pltpu.make_async_copy(hbuf.at[sl], xs_ref.at[pl.ds(0, ts * nt), :], sem.at[sl]).wait()

    @pl.when(s >= 2)
    def _():
        wait_slot(slot)

    _store_row_tiles(hbuf.at[slot], _rms(x1_ref[...], gffn_ref[...]))

    def issue(grp, carry):
        grp_off = pl.multiple_of(grp * (SUBLANES * nt), SUBLANES * nt)
        for j in range(SUBLANES):
            base = (s * ts + grp * SUBLANES + j) * TOP_K
            for kslot in range(TOP_K):
                d = pl.multiple_of(dest_ref[base + kslot], nt)
                pltpu.make_async_copy(hbuf.at[slot, pl.ds(grp_off + j * nt, nt), :],
                                      xs_ref.at[pl.ds(d, nt), :], sem.at[slot]).start(priority=kslot)
        return carry

    lax.fori_loop(0, ts // SUBLANES, issue, 0, unroll=ISSUE_UNROLL)

    @pl.when(s == pl.num_programs(0) - 1)
    def _():
        @pl.when(s >= 1)
        def _():
            wait_slot(1 - slot)
        wait_slot(slot)


def _scatter(dest, x1, g_ffn, n_rows):
    t, d = x1.shape
    ts = TS_SCATTER
    return pl.pallas_call(
        _scatter_kernel,
        grid_spec=pltpu.PrefetchScalarGridSpec(
            num_scalar_prefetch=1,
            grid=(t // ts,),
            in_specs=[pl.BlockSpec((ts, d), lambda i, dest: (i, 0)),
                      pl.BlockSpec((1, d), lambda i, dest: (0, 0))],
            out_specs=pl.BlockSpec(memory_space=pl.ANY),
            scratch_shapes=[pltpu.VMEM((2, ts * (d // LANES), LANES), F32),
                            pltpu.SemaphoreType.DMA((2,))],
        ),
        out_shape=jax.ShapeDtypeStruct((n_rows * (d // LANES), LANES), F32),
        compiler_params=pltpu.CompilerParams(dimension_semantics=("arbitrary",),
                                             vmem_limit_bytes=VMEM_LIMIT),
        name="scatter",
    )(dest, x1, g_ffn)


def _experts_kernel(be_ref, nused_ref, nexte_ref, wslot_ref, xs_ref, wg_hbm, wu_hbm, wd_hbm, ys_ref,
                    wg_f, wu_f, wd_f, wg_s, wu_s, wd_s, xbuf, ybuf, sem, xsem, ysem):
    i = pl.program_id(0)
    e = be_ref[i]
    prev = be_ref[jnp.maximum(i - 1, 0)]
    active = i < nused_ref[0]

    def weight_copies(ex):
        ws = wslot_ref[ex]
        return (pltpu.make_async_copy(wg_hbm.at[ex], wg_f.at[ws], sem.at[ws, 0]),
                pltpu.make_async_copy(wu_hbm.at[ex], wu_f.at[ws], sem.at[ws, 1]),
                pltpu.make_async_copy(wd_hbm.at[ex], wd_f.at[ws], sem.at[ws, 2]))

    def start_weights(ex):
        @pl.when(ex < N_EXPERTS)
        def _():
            for cp in weight_copies(ex):
                cp.start(priority=1)

    @pl.when(i == 0)
    def _():
        start_weights(e)
        start_weights(nexte_ref[e])

    @pl.when(active & ((i == 0) | (e != prev)))
    def _():
        for cp in weight_copies(e):
            cp.wait()
        ws = wslot_ref[e]
        wg_s[...] = wg_f[ws].astype(BF16)
        wu_s[...] = wu_f[ws].astype(BF16)
        wd_s[...] = wd_f[ws].astype(BF16)
        nxt = nexte_ref[e]
        start_weights(jnp.where(nxt < N_EXPERTS, nexte_ref[jnp.minimum(nxt, N_EXPERTS - 1)], N_EXPERTS))

    blk_rows = xbuf.shape[1]
    n_used = nused_ref[0]

    def xs_copy(blk, slot):
        return pltpu.make_async_copy(xs_ref.at[pl.ds(pl.multiple_of(blk * blk_rows, blk_rows), blk_rows), :],
                                     xbuf.at[slot], xsem.at[slot])

    def ys_copy(blk, slot):
        return pltpu.make_async_copy(ybuf.at[slot],
                                     ys_ref.at[pl.ds(pl.multiple_of(blk * blk_rows, blk_rows), blk_rows), :],
                                     ysem.at[slot])

    @pl.when(i == 0)
    def _():
        for ahead in range(XS_BUFFERS - 1):
            @pl.when(ahead < n_used)
            def _():
                xs_copy(ahead, ahead).start()

    @pl.when(active)
    def _():
        nt = wg_s.shape[0] // LANES
        fetch = i + (XS_BUFFERS - 1)

        @pl.when(fetch < n_used)
        def _():
            xs_copy(fetch, fetch % XS_BUFFERS).start()

        xs_copy(i, i % XS_BUFFERS).wait()
        oslot = i % 2

        @pl.when(i >= 2)
        def _():
            ys_copy(i - 2, oslot).wait()

        xb = _load_row_tiles(xbuf.at[i % XS_BUFFERS], ROW_BLOCK, nt).astype(BF16)
        gate = jnp.dot(xb, wg_s[...], preferred_element_type=F32)
        up = jnp.dot(xb, wu_s[...], preferred_element_type=F32)
        hid = gate * _sigmoid(gate) * up
        _store_row_tiles(ybuf.at[oslot], jnp.dot(hid.astype(BF16), wd_s[...], preferred_element_type=F32))
        ys_copy(i, oslot).start()

        @pl.when(i == n_used - 1)
        def _():
            @pl.when(i >= 1)
            def _():
                ys_copy(i - 1, 1 - oslot).wait()
            ys_copy(i, oslot).wait()


def _experts(blk_expert, n_used, next_expert, weight_slot, xs, w_gate, w_up, w_down):
    d, de = w_gate.shape[-2:]
    blk_rows = ROW_BLOCK * (d // LANES)
    nb = xs.shape[0] // blk_rows

    return pl.pallas_call(
        _experts_kernel,
        grid_spec=pltpu.PrefetchScalarGridSpec(
            num_scalar_prefetch=4,
            grid=(nb,),
            in_specs=[pl.BlockSpec(memory_space=pl.ANY)] * 4,
            out_specs=pl.BlockSpec(memory_space=pl.ANY),
            scratch_shapes=[pltpu.VMEM((2, d, de), F32), pltpu.VMEM((2, d, de), F32),
                            pltpu.VMEM((2, de, d), F32),
                            pltpu.VMEM((d, de), BF16), pltpu.VMEM((d, de), BF16), pltpu.VMEM((de, d), BF16),
                            pltpu.VMEM((XS_BUFFERS, blk_rows, LANES), F32),
                            pltpu.VMEM((2, blk_rows, LANES), F32),
                            pltpu.SemaphoreType.DMA((2, 3)), pltpu.SemaphoreType.DMA((XS_BUFFERS,)),
                            pltpu.SemaphoreType.DMA((2,))],
        ),
        out_shape=jax.ShapeDtypeStruct(xs.shape, F32),
        compiler_params=pltpu.CompilerParams(dimension_semantics=("arbitrary",),
                                             vmem_limit_bytes=VMEM_LIMIT),
        name="experts",
    )(blk_expert, n_used, next_expert, weight_slot, xs, w_gate, w_up, w_down)


def _combine_kernel(dest_ref, x1_ref, meta_ref, gfin_ref, ys_ref, out_ref, ybuf, sem):
    te = x1_ref.shape[0]
    s = pl.program_id(0)
    nsteps = pl.num_programs(0)
    slot = s % 2

    nt = x1_ref.shape[1] // LANES

    def issue_step(step, sl):
        def issue(grp, carry):
            grp_off = pl.multiple_of(grp * (SUBLANES * nt), SUBLANES * nt)
            for j in range(SUBLANES):
                base = (step * te + grp * SUBLANES + j) * TOP_K
                for kslot in range(TOP_K):
                    d = pl.multiple_of(dest_ref[base + kslot], nt)
                    pltpu.make_async_copy(ys_ref.at[pl.ds(d, nt), :],
                                          ybuf.at[sl, kslot, pl.ds(grp_off + j * nt, nt), :],
                                          sem.at[sl]).start(priority=kslot)
            return carry
        lax.fori_loop(0, te // SUBLANES, issue, 0, unroll=ISSUE_UNROLL)

    @pl.when(s == 0)
    def _():
        issue_step(0, 0)

    @pl.when(s + 1 < nsteps)
    def _():
        issue_step(s + 1, 1 - slot)

    for kslot in range(TOP_K):
        pltpu.make_async_copy(ys_ref.at[pl.ds(0, te * nt), :], ybuf.at[slot, kslot], sem.at[slot]).wait()

    meta = meta_ref[...]
    y0 = _load_row_tiles(ybuf.at[slot, 0], te, nt)
    y1 = _load_row_tiles(ybuf.at[slot, 1], te, nt)
    x2 = x1_ref[...] + y0 * meta[:, 4:5] + y1 * meta[:, 5:6]
    out_ref[...] = _rms(x2, gfin_ref[...])


def _combine(dest, x1, meta, g_final, ys):
    t, d = x1.shape
    te = TE_COMBINE
    return pl.pallas_call(
        _combine_kernel,
        grid_spec=pltpu.PrefetchScalarGridSpec(
            num_scalar_prefetch=1,
            grid=(t // te,),
            in_specs=[pl.BlockSpec((te, d), lambda i, dest: (i, 0)),
                      pl.BlockSpec((te, 8), lambda i, dest: (i, 0)),
                      pl.BlockSpec((1, d), lambda i, dest: (0, 0)),
                      pl.BlockSpec(memory_space=pl.ANY)],
            out_specs=pl.BlockSpec((te, d), lambda i, dest: (i, 0)),
            scratch_shapes=[pltpu.VMEM((2, TOP_K, te * (d // LANES), LANES), F32),
                            pltpu.SemaphoreType.DMA((2,))],
        ),
        out_shape=jax.ShapeDtypeStruct((t, d), F32),
        compiler_params=pltpu.CompilerParams(dimension_semantics=("arbitrary",),
                                             vmem_limit_bytes=VMEM_LIMIT),
        name="combine",
    )(dest, x1, meta, g_final, ys)


def _constants(tm_merge):
    n = CHUNK
    tri = (jnp.arange(n)[:, None] >= jnp.arange(n)[None, :]).astype(BF16)
    head = jnp.arange(2 * LANES) // HEAD_DIM
    seg_ones = (head[:, None] == head[None, :]).astype(BF16)
    tril_strict = (jnp.arange(tm_merge)[:, None] > jnp.arange(tm_merge)[None, :]).astype(BF16)
    return tri, seg_ones, tril_strict


def kernel(x, mem, g_mix, g_mem, w_in, conv_w, mu_rkv, mu_wag, w_lora1, w_lora2, w0, a_lora1, a_lora2, a0, g_lora1, g_lora2, k_k, k_a, r_k, ln_x_w, ln_x_b, w_kv_mem, w_branch, w_gate, b_gate, w_o, g_ffn, w_router_group, b_router_group, w_router_expert, b_router_expert, w_exp_gate, w_exp_up, w_exp_down, g_final):
    assert g_mix.shape[0] == 1, "single-layer block"
    b, s, d = x.shape
    t = b * s
    db = D_BRANCH
    tri, seg_ones, tril_strict = _constants(TM_MERGE)
    row = lambda a: a.reshape(1, -1)
    pad_r = LANES - N_EXPERTS - N_GROUPS
    p = {
        "g_mix": row(g_mix[0]), "w_in": w_in[0].astype(BF16), "conv_w": conv_w[0].T,
        "mu_rkv": row(mu_rkv[0]), "mu_wag": mu_wag[0],
        "w_lora1": w_lora1[0].astype(BF16), "w_lora2": w_lora2[0].astype(BF16), "w0": row(w0[0]),
        "a_lora1": a_lora1[0].astype(BF16), "a_lora2": a_lora2[0].astype(BF16), "a0": row(a0[0]),
        "g_lora1": g_lora1[0].astype(BF16), "g_lora2": g_lora2[0].astype(BF16),
        "k_k": row(k_k[0]), "k_a": row(k_a[0]), "r_k": row(r_k[0]),
        "ln_x_w": row(ln_x_w[0]), "ln_x_b": row(ln_x_b[0]),
        "w_gate": w_gate[0].astype(BF16), "b_gate": row(b_gate[0]),
        "w_branch": w_branch[0].astype(BF16), "w_o": w_o[0].astype(BF16), "g_ffn": row(g_ffn[0]),
        "w_router": jnp.concatenate([w_router_expert[0], w_router_group[0],
                                     jnp.zeros((d, pad_r), F32)], axis=1),
        "b_router": row(jnp.concatenate([b_router_expert[0], b_router_group[0],
                                         jnp.zeros((pad_r,), F32)])),
        "tri": tri, "seg_ones": seg_ones, "tril_strict": tril_strict,
    }

    km, vm = _memkv(mem, row(g_mem[0]), w_kv_mem[0].astype(BF16))
    x2 = x.reshape(t, d)
    yconv, ymem, r, k, v, kkn, a, lw, g = _prologue(x2, b, km, vm, p)
    yrwkv = _rwkv(r, k, v, kkn, a, lw, g, b, p)
    x1, meta, cnt = _merge(x2, yconv, yrwkv, ymem, p)

    counts = cnt[0, :N_EXPERTS].astype(jnp.int32)
    padded = ((counts + ROW_BLOCK - 1) // ROW_BLOCK) * ROW_BLOCK
    pad_end = jnp.cumsum(padded)
    pad_start = pad_end - padded
    n_blocks = (t * TOP_K) // ROW_BLOCK + N_EXPERTS
    e_idx = meta[:, 0:TOP_K].astype(jnp.int32)
    onehot = e_idx[:, :, None] == jnp.arange(N_EXPERTS, dtype=jnp.int32)[None, None, :]
    dest = (jnp.sum(jnp.where(onehot, pad_start[None, None, :], 0), axis=-1)
            + meta[:, TOP_K:2 * TOP_K].astype(jnp.int32)).reshape(t * TOP_K)
    dest = dest * (d // LANES)
    blk_start = jnp.arange(n_blocks, dtype=jnp.int32) * ROW_BLOCK
    blk_expert = jnp.minimum(jnp.sum((pad_end[None, :] <= blk_start[:, None]).astype(jnp.int32), axis=1),
                             N_EXPERTS - 1)
    n_used = (pad_end[-1:] // ROW_BLOCK).astype(jnp.int32)
    eids = jnp.arange(N_EXPERTS, dtype=jnp.int32)
    later_nonempty = (eids[None, :] > eids[:, None]) & (counts[None, :] > 0)
    next_expert = jnp.min(jnp.where(later_nonempty, eids[None, :], N_EXPERTS), axis=1)
    weight_slot = (jnp.cumsum((counts > 0).astype(jnp.int32)) - 1) & 1

    xs = _scatter(dest, x1, p["g_ffn"], n_blocks * ROW_BLOCK)
    ys = _experts(blk_expert, n_used, next_expert, weight_slot, xs,
                  w_exp_gate[0], w_exp_up[0], w_exp_down[0])
    out = _combine(dest, x1, meta, row(g_final), ys)
    return out.reshape(b, s, d)
```

```python
import functools

import jax
import jax.numpy as jnp
from jax import lax
from jax.experimental import pallas as pl
from jax.experimental.pallas import tpu as pltpu

F32 = jnp.float32
BF16 = jnp.bfloat16

NORM_EPS = 1e-6
GN_EPS = 64e-5
D_BRANCH = 512
HEAD_DIM = 64
N_HEADS = 8
CHUNK = 64
CHUNKS_PER_ITER = 4
MEM_HEADS = 4
MEM_HEAD_DIM = 128
N_GROUPS = 8
EXPERTS_PER_GROUP = 8
N_EXPERTS = 64
TOP_K = 2
ROW_BLOCK = 128
XS_BUFFERS = 4
LANES = 128
VMEM_LIMIT = 56 * 1024 * 1024

TM_PROLOGUE = 512
TB_RWKV = 256
TM_MERGE = 512
TS_SCATTER = 256
TE_COMBINE = 256
SUBLANES = 8
META_COLS = 8
ISSUE_UNROLL = 2


def _bdot(a, b):
    return jnp.dot(a.astype(BF16), b.astype(BF16), preferred_element_type=F32)


def _bdot_nt(a, b):
    return lax.dot_general(a.astype(BF16), b.astype(BF16), (((1,), (1,)), ((), ())),
                           preferred_element_type=F32)


def _split_terms(x, n_terms):
    terms = []
    for _ in range(n_terms):
        t = x.astype(BF16)
        terms.append(t)
        x = x - t.astype(F32)
    return terms


def _split_dot_left(m_bf16, x, n_terms):
    return sum(jnp.dot(m_bf16, t, preferred_element_type=F32) for t in _split_terms(x, n_terms))


def _head_sums(x, seg_bf16, n_terms):
    w = seg_bf16.shape[0]
    terms = _split_terms(x, n_terms)
    halves = [sum(jnp.dot(t[:, c:c + w], seg_bf16, preferred_element_type=F32) for t in terms)
              for c in range(0, x.shape[1], w)]
    return jnp.concatenate(halves, axis=1)


def _rms(x, g):
    return x * lax.rsqrt(jnp.mean(x * x, axis=-1, keepdims=True) + NORM_EPS) * g


def _sigmoid(x):
    return 1.0 / (1.0 + jnp.exp(-x))


def _const_spec(shape):
    n = len(shape)
    return pl.BlockSpec(shape, lambda *_: (0,) * n)


def _memkv_kernel(mem_ref, g_ref, w_ref, k_ref, v_ref):
    mn = _rms(mem_ref[0], g_ref[...])
    kv = _bdot(mn, w_ref[...])
    k_ref[0] = kv[:, :D_BRANCH].astype(BF16)
    v_ref[0] = kv[:, D_BRANCH:].astype(BF16)


def _memkv(mem, g_mem, w_kv):
    b, m, d = mem.shape
    return pl.pallas_call(
        _memkv_kernel,
        grid=(b,),
        in_specs=[pl.BlockSpec((1, m, d), lambda i: (i, 0, 0)),
                  _const_spec((1, d)), _const_spec((d, 2 * D_BRANCH))],
        out_specs=[pl.BlockSpec((1, m, D_BRANCH), lambda i: (i, 0, 0)),
                   pl.BlockSpec((1, m, D_BRANCH), lambda i: (i, 0, 0))],
        out_shape=[jax.ShapeDtypeStruct((b, m, D_BRANCH), BF16)] * 2,
        compiler_params=pltpu.CompilerParams(dimension_semantics=("arbitrary",),
                                             vmem_limit_bytes=VMEM_LIMIT),
        name="memkv",
    )(mem, g_mem, w_kv)


def _prologue_kernel(x_ref, gmix_ref, win_ref, convw_ref, murkv_ref, muwag_ref,
                     wl1_ref, wl2_ref, w0_ref, al1_ref, al2_ref, a0_ref, gl1_ref, gl2_ref,
                     kk_ref, ka_ref, seg_ref, km_ref, vm_ref,
                     yconv_ref, ymem_ref, r_ref, k_ref, v_ref, kkn_ref, a_ref, lw_ref, g_ref,
                     prev_h, prev_p, prev_cu):
    tm = x_ref.shape[0]
    db = D_BRANCH

    @pl.when(pl.program_id(1) == 0)
    def _():
        prev_h[...] = jnp.zeros_like(prev_h)
        prev_p[...] = jnp.zeros_like(prev_p)
        prev_cu[...] = jnp.zeros_like(prev_cu)

    rows = lax.broadcasted_iota(jnp.int32, (tm, 1), 0)

    def shift1(u, prev_row):
        return jnp.where(rows == 0, prev_row, pltpu.roll(u, 1, axis=0))

    h = _rms(x_ref[...], gmix_ref[...])
    proj = _bdot(h, win_ref[...])

    bg, cg, u = proj[:, :db], proj[:, db:2 * db], proj[:, 2 * db:3 * db]
    cu = cg * u
    cu1 = shift1(cu, prev_cu[1:2, :])
    cu2 = jnp.where(rows == 0, prev_cu[0:1, :],
                    jnp.where(rows == 1, prev_cu[1:2, :], pltpu.roll(cu, 2, axis=0)))
    conv = cu2 * convw_ref[0:1, :] + cu1 * convw_ref[1:2, :] + cu * convw_ref[2:3, :]
    yconv_ref[...] = (bg * conv).astype(BF16)
    prev_cu[...] = cu[tm - 2:tm, :]

    pr = proj[:, 3 * db:6 * db]
    prs = shift1(pr, prev_p[...])
    mixed = pr + (prs - pr) * murkv_ref[...]
    prev_p[...] = pr[tm - 1:tm, :]
    r, k, v = mixed[:, :db], mixed[:, db:2 * db], mixed[:, 2 * db:]

    dh = shift1(h, prev_h[...]) - h
    prev_h[...] = h[tm - 1:tm, :]
    xw = h + dh * muwag_ref[0:1, :]
    xa = h + dh * muwag_ref[1:2, :]
    xg = h + dh * muwag_ref[2:3, :]
    zz = w0_ref[...] + _bdot(jnp.tanh(_bdot(xw, wl1_ref[...])), wl2_ref[...])
    softplus = jnp.maximum(-zz, 0.0) + jnp.log(1.0 + jnp.exp(-jnp.abs(zz)))
    lw_ref[...] = -jnp.exp(-softplus - 0.5)
    a = _sigmoid(a0_ref[...] + _bdot(_bdot(xa, al1_ref[...]), al2_ref[...]))
    g_ref[...] = _bdot(_sigmoid(_bdot(xg, gl1_ref[...])), gl2_ref[...])

    kk = k * kk_ref[...]
    ss = _head_sums(kk * kk, seg_ref[...], 1)
    kkn_ref[...] = kk * lax.rsqrt(jnp.maximum(ss, 1e-24))
    k_ref[...] = k * (1.0 + (a - 1.0) * ka_ref[...])
    r_ref[...] = r
    v_ref[...] = v
    a_ref[...] = a

    q = proj[:, 6 * db:]
    scale = MEM_HEAD_DIM ** -0.5
    for hh in range(MEM_HEADS):
        sl = slice(hh * MEM_HEAD_DIM, (hh + 1) * MEM_HEAD_DIM)
        s = _bdot_nt(q[:, sl], km_ref[0, :, sl]) * scale
        p = jnp.exp(s - jnp.max(s, axis=-1, keepdims=True))
        o = _bdot(p, vm_ref[0, :, sl]) / jnp.sum(p, axis=-1, keepdims=True)
        ymem_ref[:, sl] = o.astype(BF16)


def _prologue(x2, b, km, vm, p):
    t, d = x2.shape
    s = t // b
    tm = TM_PROLOGUE
    db = D_BRANCH
    m = km.shape[1]
    steps = s // tm
    tok = lambda c: pl.BlockSpec((tm, c), lambda bi, i: (bi * steps + i, 0))
    consts = [p["g_mix"], p["w_in"], p["conv_w"], p["mu_rkv"], p["mu_wag"],
              p["w_lora1"], p["w_lora2"], p["w0"], p["a_lora1"], p["a_lora2"], p["a0"],
              p["g_lora1"], p["g_lora2"], p["k_k"], p["k_a"], p["seg_ones"]]
    out_shapes = ([jax.ShapeDtypeStruct((t, db), BF16)] * 2
                  + [jax.ShapeDtypeStruct((t, db), F32)] * 7)
    return pl.pallas_call(
        _prologue_kernel,
        grid=(b, steps),
        in_specs=[tok(d)] + [_const_spec(c.shape) for c in consts]
                 + [pl.BlockSpec((1, m, db), lambda bi, i: (bi, 0, 0))] * 2,
        out_specs=[tok(db)] * 9,
        out_shape=out_shapes,
        scratch_shapes=[pltpu.VMEM((1, d), F32), pltpu.VMEM((1, 3 * db), F32),
                        pltpu.VMEM((2, db), F32)],
        compiler_params=pltpu.CompilerParams(dimension_semantics=("arbitrary", "arbitrary"),
                                             vmem_limit_bytes=VMEM_LIMIT),
        name="prologue",
    )(x2, *consts, km, vm)


def _rwkv_kernel(r_ref, k_ref, v_ref, kk_ref, a_ref, lw_ref, g_ref, rk_ref, lnw_ref, lnb_ref,
                 tri_ref, seg_ref, out_ref, h_scr, y_scr):
    tb = r_ref.shape[0]
    n = HEAD_DIM
    c_len = CHUNK

    @pl.when(pl.program_id(1) == 0)
    def _():
        h_scr[...] = jnp.zeros_like(h_scr)

    row2 = lax.broadcasted_iota(jnp.int32, (c_len, 2 * c_len), 0)
    col2 = lax.broadcasted_iota(jnp.int32, (c_len, 2 * c_len), 1) & (c_len - 1)
    strict2 = col2 < row2
    incl2 = col2 <= row2
    eye = (lax.broadcasted_iota(jnp.int32, (c_len, n), 0)
           == lax.broadcasted_iota(jnp.int32, (c_len, n), 1)).astype(F32)
    zeros = jnp.zeros((c_len, n), F32)

    def chunk_inputs(c):
        rows = pl.ds(pl.multiple_of(c * c_len, c_len), c_len)
        r = r_ref[rows, :]
        k = k_ref[rows, :]
        v = v_ref[rows, :]
        kk = kk_ref[rows, :]
        a = a_ref[rows, :]
        lw = lw_ref[rows, :]
        gcum = _split_dot_left(tri_ref[...], lw, 2)
        e_pos = jnp.exp(gcum)
        e_neg = jnp.exp(-gcum)
        p_last = jnp.exp(gcum[c_len - 1:c_len, :])
        bb = kk * a * e_neg
        kb = k * e_neg
        return dict(rows=rows, v=v, p_last=p_last, rb=r * e_pos, ab=-kk * jnp.exp(gcum - lw), bb=bb, kb=kb,
                    bk_t=jnp.concatenate([bb * p_last, kb * p_last], axis=0).T)

    def chunk_group(it, carry):
        chunks = [chunk_inputs(it * CHUNKS_PER_ITER + ci) for ci in range(CHUNKS_PER_ITER)]
        units = [(ci, hd) for ci in range(CHUNKS_PER_ITER) for hd in range(N_HEADS)]
        nu = range(len(units))
        ls = [slice(hd * n, (hd + 1) * n) for _, hd in units]
        ch = [chunks[ci] for ci, _ in units]
        al = [ch[u]["ab"][:, ls[u]] for u in nu]
        rr = [ch[u]["rb"][:, ls[u]] for u in nu]
        vv = [ch[u]["v"][:, ls[u]] for u in nu]
        aa = [_bdot_nt(jnp.concatenate([al[u], rr[u]], axis=0),
                       jnp.concatenate([ch[u]["bb"][:, ls[u]], ch[u]["kb"][:, ls[u]]], axis=0)) for u in nu]
        top = [jnp.where(strict2, aa[u][:c_len], 0.0) for u in nu]
        bot = [jnp.where(incl2, aa[u][c_len:], 0.0) for u in nu]
        a_ab = [top[u][:, :c_len] for u in nu]
        t_inv = [eye + a_ab[u] for u in nu]
        x_pow = [_bdot(a_ab[u], a_ab[u]) for u in nu]
        av = [_bdot(top[u], jnp.concatenate([zeros, vv[u]], axis=0)) for u in nu]
        for lvl in range(5):
            if lvl < 4:
                z = [_bdot(jnp.concatenate([t_inv[u], x_pow[u]], axis=0), x_pow[u]) for u in nu]
                t_inv = [t_inv[u] + z[u][:c_len] for u in nu]
                x_pow = [z[u][c_len:] for u in nu]
            else:
                t_inv = [t_inv[u] + _bdot(t_inv[u], x_pow[u]) for u in nu]
        w12 = [_bdot(t_inv[u], jnp.concatenate([al[u], av[u]], axis=1)) for u in nu]
        z2 = []
        for u in nu:
            hd = units[u][1]
            rhs2 = jnp.concatenate([w12[u], jnp.concatenate([zeros, vv[u]], axis=1)], axis=0)
            lhs3 = jnp.concatenate([ch[u]["bk_t"][hd * n:(hd + 1) * n, :], bot[u]], axis=0)
            z2.append(_bdot(lhs3, rhs2))
        state = [h_scr[hd] for hd in range(N_HEADS)]
        for u in nu:
            hd = units[u][1]
            mq = z2[u][:, :n] + jnp.concatenate([zeros, rr[u]], axis=0)
            out = _bdot(mq, state[hd]) + z2[u][:, n:]
            p_col = jnp.sum(eye * ch[u]["p_last"][:, ls[u]], axis=1, keepdims=True)
            state[hd] = p_col * state[hd] + out[:c_len]
            y_scr[ch[u]["rows"], ls[u]] = out[c_len:]
        for hd in range(N_HEADS):
            h_scr[hd] = state[hd]
        return carry

    lax.fori_loop(0, tb // (c_len * CHUNKS_PER_ITER), chunk_group, 0)

    y = y_scr[...]
    seg = seg_ref[...]
    inv_n = 1.0 / n
    mu = _head_sums(y, seg, 2) * inv_n
    yc = y - mu
    var = _head_sums(yc * yc, seg, 1) * inv_n
    yn = yc * lax.rsqrt(var + GN_EPS) * lnw_ref[...] + lnb_ref[...]
    bonus = _head_sums(r_ref[...] * k_ref[...] * rk_ref[...], seg, 1) * v_ref[...]
    out_ref[...] = ((yn + bonus) * g_ref[...]).astype(BF16)


def _rwkv(r, k, v, kkn, a, lw, g, b, p):
    t, db = r.shape
    tb = TB_RWKV
    steps = t // b // tb
    tok = pl.BlockSpec((tb, db), lambda bi, i: (bi * steps + i, 0))
    consts = [p["r_k"], p["ln_x_w"], p["ln_x_b"], p["tri"], p["seg_ones"]]
    return pl.pallas_call(
        _rwkv_kernel,
        grid=(b, steps),
        in_specs=[tok] * 7 + [_const_spec(c.shape) for c in consts],
        out_specs=tok,
        out_shape=jax.ShapeDtypeStruct((t, db), BF16),
        scratch_shapes=[pltpu.VMEM((N_HEADS, HEAD_DIM, HEAD_DIM), F32),
                        pltpu.VMEM((tb, db), F32)],
        compiler_params=pltpu.CompilerParams(dimension_semantics=("arbitrary", "arbitrary"),
                                             vmem_limit_bytes=VMEM_LIMIT),
        name="rwkv",
    )(r, k, v, kkn, a, lw, g, *consts)


def _merge_kernel(x_ref, yc_ref, yr_ref, ym_ref, gmix_ref, wgate_ref, bgate_ref, wbr_ref, wo_ref,
                  gffn_ref, wrt_ref, brt_ref, tril_ref,
                  x1_ref, meta_ref, metat_ref, cnt_ref, base_scr):
    tm, d = x_ref.shape

    @pl.when(pl.program_id(0) == 0)
    def _():
        base_scr[...] = jnp.zeros_like(base_scr)

    x = x_ref[...]
    hb = _rms(x, gmix_ref[...]).astype(BF16)
    z = jnp.zeros((tm, d), F32)
    for i, y_ref in enumerate((yc_ref, yr_ref, ym_ref)):
        cs = slice(i * d, (i + 1) * d)
        gate = _sigmoid(jnp.dot(hb, wgate_ref[:, cs], preferred_element_type=F32) + bgate_ref[:, cs])
        z = z + gate * jnp.dot(y_ref[...], wbr_ref[i], preferred_element_type=F32)
    x1 = x + _bdot(z, wo_ref[...])
    x1_ref[...] = x1

    h2 = _rms(x1, gffn_ref[...])
    h_hi, h_lo = _split_terms(h2, 2)
    w_hi, w_lo = _split_terms(wrt_ref[...], 2)
    hi_terms = jnp.dot(h_hi, jnp.concatenate([w_hi, w_lo], axis=1), preferred_element_type=F32)
    logits = (hi_terms[:, :LANES]
              + (jnp.dot(h_lo, w_hi, preferred_element_type=F32) + hi_terms[:, LANES:])) + brt_ref[...]
    lane = lax.broadcasted_iota(jnp.int32, (tm, LANES), 1)
    neg = jnp.float32(-jnp.inf)
    big = jnp.int32(1 << 20)
    gmask = (lane >= N_EXPERTS) & (lane < N_EXPERTS + N_GROUPS)
    glv = jnp.where(gmask, logits, neg)
    gmax = jnp.max(glv, axis=-1, keepdims=True)
    g_sel = jnp.min(jnp.where(glv == gmax, lane - N_EXPERTS, big), axis=-1, keepdims=True)
    g_w = 1.0 / jnp.sum(jnp.exp(glv - gmax), axis=-1, keepdims=True)
    emask = (lane < N_EXPERTS) & ((lane >> 3) == g_sel)
    elv = jnp.where(emask, logits, neg)
    emax = jnp.max(elv, axis=-1, keepdims=True)
    esum = jnp.sum(jnp.exp(elv - emax), axis=-1, keepdims=True)
    i1 = jnp.min(jnp.where(elv == emax, lane, big), axis=-1, keepdims=True)
    elv2 = jnp.where(lane == i1, neg, elv)
    m2 = jnp.max(elv2, axis=-1, keepdims=True)
    i2 = jnp.min(jnp.where(elv2 == m2, lane, big), axis=-1, keepdims=True)
    p1 = 1.0 / esum
    p2 = jnp.exp(m2 - emax) / esum
    c1 = g_w * p1 / (p1 + p2)
    c2 = g_w * p2 / (p1 + p2)

    oh1 = lane == i1
    oh2 = lane == i2
    onehot = jnp.where(oh1 | oh2, 1.0, 0.0)
    before = jnp.dot(tril_ref[...], onehot.astype(BF16), preferred_element_type=F32) + base_scr[...]
    rank1 = jnp.sum(jnp.where(oh1, before, 0.0), axis=-1, keepdims=True)
    rank2 = jnp.sum(jnp.where(oh2, before, 0.0), axis=-1, keepdims=True)
    new_base = base_scr[...] + jnp.sum(onehot, axis=0, keepdims=True)
    base_scr[...] = new_base
    cnt_ref[...] = jnp.broadcast_to(new_base, cnt_ref.shape)

    meta = jnp.where(lane == 0, i1.astype(F32),
           jnp.where(lane == 1, i2.astype(F32),
           jnp.where(lane == 2, rank1,
           jnp.where(lane == 3, rank2,
           jnp.where(lane == 4, c1,
           jnp.where(lane == 5, c2, 0.0))))))
    meta_ref[...] = meta[:, :META_COLS]
    metat_ref[...] = meta.T[:META_COLS, :]


def _merge(x2, yc, yr, ym, p):
    t, d = x2.shape
    tm = TM_MERGE
    db = D_BRANCH
    tok = lambda c: pl.BlockSpec((tm, c), lambda i: (i, 0))
    consts = [p["g_mix"], p["w_gate"], p["b_gate"], p["w_branch"], p["w_o"], p["g_ffn"],
              p["w_router"], p["b_router"], p["tril_strict"]]
    return pl.pallas_call(
        _merge_kernel,
        grid=(t // tm,),
        in_specs=[tok(d), tok(db), tok(db), tok(db)] + [_const_spec(c.shape) for c in consts],
        out_specs=[tok(d), tok(META_COLS), pl.BlockSpec((META_COLS, tm), lambda i: (0, i)),
                   _const_spec((8, LANES))],
        out_shape=[jax.ShapeDtypeStruct((t, d), F32), jax.ShapeDtypeStruct((t, META_COLS), F32),
                   jax.ShapeDtypeStruct((META_COLS, t), F32), jax.ShapeDtypeStruct((8, LANES), F32)],
        scratch_shapes=[pltpu.VMEM((1, LANES), F32)],
        compiler_params=pltpu.CompilerParams(dimension_semantics=("arbitrary",),
                                             vmem_limit_bytes=VMEM_LIMIT),
        name="merge",
    )(x2, yc, yr, ym, *consts)


def _store_row_tiles(ref2d, x):
    rows, d = x.shape
    nt = d // LANES
    for c in range(nt):
        ref2d[pl.ds(c, rows, stride=nt), :] = x[:, c * LANES:(c + 1) * LANES]


def _load_row_tiles(ref2d, rows, nt):
    return jnp.concatenate([ref2d[pl.ds(c, rows, stride=nt), :] for c in range(nt)], axis=1)


def _scatter_kernel(dest0_ref, dest1_ref, x1_ref, gffn_ref, xs_ref, hbuf, sem):
    dest_refs = (dest0_ref, dest1_ref)
    ts, d_model = x1_ref.shape
    nt = d_model // LANES
    s = pl.program_id(0)
    slot = s % 2

    def wait_slot(sl):
        for _ in range(TOP_K):
            pltpu.make_async_copy(hbuf.at[sl], xs_ref.at[pl.ds(0, ts * nt), :], sem.at[sl]).wait()

    @pl.when(s >= 2)
    def _():
        wait_slot(slot)

    _store_row_tiles(hbuf.at[slot], _rms(x1_ref[...], gffn_ref[...]))

    def issue(grp, carry):
        grp_off = pl.multiple_of(grp * (SUBLANES * nt), SUBLANES * nt)
        for j in range(SUBLANES):
            tok = s * ts + grp * SUBLANES + j
            for kslot in range(TOP_K):
                d = pl.multiple_of(dest_refs[kslot][tok], nt)
                pltpu.make_async_copy(hbuf.at[slot, pl.ds(grp_off + j * nt, nt), :],
                                      xs_ref.at[pl.ds(d, nt), :], sem.at[slot]).start(priority=kslot)
        return carry

    lax.fori_loop(0, ts // SUBLANES, issue, 0, unroll=ISSUE_UNROLL)

    @pl.when(s == pl.num_programs(0) - 1)
    def _():
        @pl.when(s >= 1)
        def _():
            wait_slot(1 - slot)
        wait_slot(slot)


def _scatter(dests, x1, g_ffn, n_rows):
    t, d = x1.shape
    ts = TS_SCATTER
    return pl.pallas_call(
        _scatter_kernel,
        grid_spec=pltpu.PrefetchScalarGridSpec(
            num_scalar_prefetch=TOP_K,
            grid=(t // ts,),
            in_specs=[pl.BlockSpec((ts, d), lambda i, *_: (i, 0)),
                      pl.BlockSpec((1, d), lambda i, *_: (0, 0))],
            out_specs=pl.BlockSpec(memory_space=pl.ANY),
            scratch_shapes=[pltpu.VMEM((2, ts * (d // LANES), LANES), F32),
                            pltpu.SemaphoreType.DMA((2,))],
        ),
        out_shape=jax.ShapeDtypeStruct((n_rows * (d // LANES), LANES), F32),
        compiler_params=pltpu.CompilerParams(dimension_semantics=("arbitrary",),
                                             vmem_limit_bytes=VMEM_LIMIT),
        name="scatter",
    )(*dests, x1, g_ffn)


def _experts_kernel(be_ref, nused_ref, nexte_ref, xs_ref, wg_hbm, wu_hbm, wd_hbm, ys_ref,
                    wg_f, wu_f, wd_f, wg_s, wu_s, wd_s, xbuf, ybuf, sem, xsem, ysem):
    i = pl.program_id(0)
    e = be_ref[i]
    prev = be_ref[jnp.maximum(i - 1, 0)]
    active = i < nused_ref[0]

    def weight_copies(ex):
        return (pltpu.make_async_copy(wg_hbm.at[ex], wg_f, sem.at[0]),
                pltpu.make_async_copy(wu_hbm.at[ex], wu_f, sem.at[1]),
                pltpu.make_async_copy(wd_hbm.at[ex], wd_f, sem.at[2]))

    @pl.when(i == 0)
    def _():
        for cp in weight_copies(e):
            cp.start(priority=1)

    @pl.when(active & ((i == 0) | (e != prev)))
    def _():
        for cp in weight_copies(e):
            cp.wait()
        wg_s[...] = wg_f[...].astype(BF16)
        wu_s[...] = wu_f[...].astype(BF16)
        wd_s[...] = wd_f[...].astype(BF16)
        nxt = nexte_ref[e]

        @pl.when(nxt < N_EXPERTS)
        def _():
            for cp in weight_copies(nxt):
                cp.start(priority=1)

    blk_rows = xbuf.shape[1]
    n_used = nused_ref[0]

    def xs_copy(blk, slot):
        return pltpu.make_async_copy(xs_ref.at[pl.ds(pl.multiple_of(blk * blk_rows, blk_rows), blk_rows), :],
                                     xbuf.at[slot], xsem.at[slot])

    def ys_copy(blk, slot):
        return pltpu.make_async_copy(ybuf.at[slot],
                                     ys_ref.at[pl.ds(pl.multiple_of(blk * blk_rows, blk_rows), blk_rows), :],
                                     ysem.at[slot])

    @pl.when(i == 0)
    def _():
        for ahead in range(XS_BUFFERS - 1):
            @pl.when(ahead < n_used)
            def _():
                xs_copy(ahead, ahead).start()

    @pl.when(active)
    def _():
        nt = wg_s.shape[0] // LANES
        fetch = i + (XS_BUFFERS - 1)

        @pl.when(fetch < n_used)
        def _():
            xs_copy(fetch, fetch % XS_BUFFERS).start()

        xs_copy(i, i % XS_BUFFERS).wait()
        oslot = i % 2

        @pl.when(i >= 2)
        def _():
            ys_copy(i - 2, oslot).wait()

        xb = _load_row_tiles(xbuf.at[i % XS_BUFFERS], ROW_BLOCK, nt).astype(BF16)
        gate = jnp.dot(xb, wg_s[...], preferred_element_type=F32)
        up = jnp.dot(xb, wu_s[...], preferred_element_type=F32)
        hid = gate * _sigmoid(gate) * up
        _store_row_tiles(ybuf.at[oslot], jnp.dot(hid.astype(BF16), wd_s[...], preferred_element_type=F32))
        ys_copy(i, oslot).start()

        @pl.when(i == n_used - 1)
        def _():
            @pl.when(i >= 1)
            def _():
                ys_copy(i - 1, 1 - oslot).wait()
            ys_copy(i, oslot).wait()


def _experts(blk_expert, n_used, next_expert, xs, w_gate, w_up, w_down):
    d, de = w_gate.shape[-2:]
    blk_rows = ROW_BLOCK * (d // LANES)
    nb = xs.shape[0] // blk_rows

    return pl.pallas_call(
        _experts_kernel,
        grid_spec=pltpu.PrefetchScalarGridSpec(
            num_scalar_prefetch=3,
            grid=(nb,),
            in_specs=[pl.BlockSpec(memory_space=pl.ANY)] * 4,
            out_specs=pl.BlockSpec(memory_space=pl.ANY),
            scratch_shapes=[pltpu.VMEM((d, de), F32), pltpu.VMEM((d, de), F32), pltpu.VMEM((de, d), F32),
                            pltpu.VMEM((d, de), BF16), pltpu.VMEM((d, de), BF16), pltpu.VMEM((de, d), BF16),
                            pltpu.VMEM((XS_BUFFERS, blk_rows, LANES), F32),
                            pltpu.VMEM((2, blk_rows, LANES), F32),
                            pltpu.SemaphoreType.DMA((3,)), pltpu.SemaphoreType.DMA((XS_BUFFERS,)),
                            pltpu.SemaphoreType.DMA((2,))],
        ),
        out_shape=jax.ShapeDtypeStruct(xs.shape, F32),
        compiler_params=pltpu.CompilerParams(dimension_semantics=("arbitrary",),
                                             vmem_limit_bytes=VMEM_LIMIT),
        name="experts",
    )(blk_expert, n_used, next_expert, xs, w_gate, w_up, w_down)


def _combine_kernel(dest0_ref, dest1_ref, x1_ref, meta_ref, gfin_ref, ys_ref, out_ref, ybuf, sem):
    dest_refs = (dest0_ref, dest1_ref)
    te = x1_ref.shape[0]
    s = pl.program_id(0)
    nsteps = pl.num_programs(0)
    slot = s % 2

    nt = x1_ref.shape[1] // LANES

    def issue_step(step, sl):
        def issue(grp, carry):
            grp_off = pl.multiple_of(grp * (SUBLANES * nt), SUBLANES * nt)
            for j in range(SUBLANES):
                tok = step * te + grp * SUBLANES + j
                for kslot in range(TOP_K):
                    d = pl.multiple_of(dest_refs[kslot][tok], nt)
                    pltpu.make_async_copy(ys_ref.at[pl.ds(d, nt), :],
                                          ybuf.at[sl, kslot, pl.ds(grp_off + j * nt, nt), :],
                                          sem.at[sl]).start(priority=kslot)
            return carry
        lax.fori_loop(0, te // SUBLANES, issue, 0, unroll=ISSUE_UNROLL)

    @pl.when(s == 0)
    def _():
        issue_step(0, 0)

    @pl.when(s + 1 < nsteps)
    def _():
        issue_step(s + 1, 1 - slot)

    for kslot in range(TOP_K):
        pltpu.make_async_copy(ys_ref.at[pl.ds(0, te * nt), :], ybuf.at[slot, kslot], sem.at[slot]).wait()

    meta = meta_ref[...]
    y0 = _load_row_tiles(ybuf.at[slot, 0], te, nt)
    y1 = _load_row_tiles(ybuf.at[slot, 1], te, nt)
    x2 = x1_ref[...] + y0 * meta[:, 4:5] + y1 * meta[:, 5:6]
    out_ref[...] = _rms(x2, gfin_ref[...])


def _combine(dests, x1, meta, g_final, ys):
    t, d = x1.shape
    te = TE_COMBINE
    return pl.pallas_call(
        _combine_kernel,
        grid_spec=pltpu.PrefetchScalarGridSpec(
            num_scalar_prefetch=TOP_K,
            grid=(t // te,),
            in_specs=[pl.BlockSpec((te, d), lambda i, *_: (i, 0)),
                      pl.BlockSpec((te, META_COLS), lambda i, *_: (i, 0)),
                      pl.BlockSpec((1, d), lambda i, *_: (0, 0)),
                      pl.BlockSpec(memory_space=pl.ANY)],
            out_specs=pl.BlockSpec((te, d), lambda i, *_: (i, 0)),
            scratch_shapes=[pltpu.VMEM((2, TOP_K, te * (d // LANES), LANES), F32),
                            pltpu.SemaphoreType.DMA((2,))],
        ),
        out_shape=jax.ShapeDtypeStruct((t, d), F32),
        compiler_params=pltpu.CompilerParams(dimension_semantics=("arbitrary",),
                                             vmem_limit_bytes=VMEM_LIMIT),
        name="combine",
    )(*dests, x1, meta, g_final, ys)


def _constants(tm_merge):
    n = CHUNK
    tri = (jnp.arange(n)[:, None] >= jnp.arange(n)[None, :]).astype(BF16)
    head = jnp.arange(2 * LANES) // HEAD_DIM
    seg_ones = (head[:, None] == head[None, :]).astype(BF16)
    tril_strict = (jnp.arange(tm_merge)[:, None] > jnp.arange(tm_merge)[None, :]).astype(BF16)
    return tri, seg_ones, tril_strict


def kernel(x, mem, g_mix, g_mem, w_in, conv_w, mu_rkv, mu_wag, w_lora1, w_lora2, w0, a_lora1, a_lora2, a0, g_lora1, g_lora2, k_k, k_a, r_k, ln_x_w, ln_x_b, w_kv_mem, w_branch, w_gate, b_gate, w_o, g_ffn, w_router_group, b_router_group, w_router_expert, b_router_expert, w_exp_gate, w_exp_up, w_exp_down, g_final):
    assert g_mix.shape[0] == 1, "single-layer block"
    b, s, d = x.shape
    t = b * s
    db = D_BRANCH
    tri, seg_ones, tril_strict = _constants(TM_MERGE)
    row = lambda a: a.reshape(1, -1)
    pad_r = LANES - N_EXPERTS - N_GROUPS
    p = {
        "g_mix": row(g_mix[0]), "w_in": w_in[0].astype(BF16), "conv_w": conv_w[0].T,
        "mu_rkv": row(mu_rkv[0]), "mu_wag": mu_wag[0],
        "w_lora1": w_lora1[0].astype(BF16), "w_lora2": w_lora2[0].astype(BF16), "w0": row(w0[0]),
        "a_lora1": a_lora1[0].astype(BF16), "a_lora2": a_lora2[0].astype(BF16), "a0": row(a0[0]),
        "g_lora1": g_lora1[0].astype(BF16), "g_lora2": g_lora2[0].astype(BF16),
        "k_k": row(k_k[0]), "k_a": row(k_a[0]), "r_k": row(r_k[0]),
        "ln_x_w": row(ln_x_w[0]), "ln_x_b": row(ln_x_b[0]),
        "w_gate": w_gate[0].astype(BF16), "b_gate": row(b_gate[0]),
        "w_branch": w_branch[0].astype(BF16), "w_o": w_o[0].astype(BF16), "g_ffn": row(g_ffn[0]),
        "w_router": jnp.concatenate([w_router_expert[0], w_router_group[0],
                                     jnp.zeros((d, pad_r), F32)], axis=1),
        "b_router": row(jnp.concatenate([b_router_expert[0], b_router_group[0],
                                         jnp.zeros((pad_r,), F32)])),
        "tri": tri, "seg_ones": seg_ones, "tril_strict": tril_strict,
    }

    km, vm = _memkv(mem, row(g_mem[0]), w_kv_mem[0].astype(BF16))
    x2 = x.reshape(t, d)
    yconv, ymem, r, k, v, kkn, a, lw, g = _prologue(x2, b, km, vm, p)
    yrwkv = _rwkv(r, k, v, kkn, a, lw, g, b, p)
    x1, meta, meta_t, cnt = _merge(x2, yconv, yrwkv, ymem, p)

    counts = cnt[0, :N_EXPERTS].astype(jnp.int32)
    padded = ((counts + ROW_BLOCK - 1) // ROW_BLOCK) * ROW_BLOCK
    pad_end = jnp.cumsum(padded)
    pad_start = pad_end - padded
    n_blocks = (t * TOP_K) // ROW_BLOCK + N_EXPERTS
    eids = jnp.arange(N_EXPERTS, dtype=jnp.int32)
    e_idx = meta_t[0:TOP_K].astype(jnp.int32)
    rank = meta_t[TOP_K:2 * TOP_K].astype(jnp.int32)
    start_of = jnp.sum(jnp.where(e_idx[:, None, :] == eids[None, :, None], pad_start[None, :, None], 0), axis=1)
    dest = (start_of + rank) * (d // LANES)
    dests = [dest[kslot] for kslot in range(TOP_K)]
    blk_start = jnp.arange(n_blocks, dtype=jnp.int32) * ROW_BLOCK
    blk_expert = jnp.minimum(jnp.sum((pad_end[None, :] <= blk_start[:, None]).astype(jnp.int32), axis=1),
                             N_EXPERTS - 1)
    n_used = (pad_end[-1:] // ROW_BLOCK).astype(jnp.int32)
    later_nonempty = (eids[None, :] > eids[:, None]) & (counts[None, :] > 0)
    next_expert = jnp.min(jnp.where(later_nonempty, eids[None, :], N_EXPERTS), axis=1)

    xs = _scatter(dests, x1, p["g_ffn"], n_blocks * ROW_BLOCK)
    ys = _experts(blk_expert, n_used, next_expert, xs, w_exp_gate[0], w_exp_up[0], w_exp_down[0])
    out = _combine(dests, x1, meta, row(g_final), ys)
    return out.reshape(b, s, d)
```

```python
import functools

import jax
import jax.numpy as jnp
from jax import lax
from jax.experimental import pallas as pl
from jax.experimental.pallas import tpu as pltpu

F32 = jnp.float32
BF16 = jnp.bfloat16

NORM_EPS = 1e-6
GN_EPS = 64e-5
D_BRANCH = 512
HEAD_DIM = 64
N_HEADS = 8
CHUNK = 64
CHUNKS_PER_ITER = 4
MEM_HEADS = 4
MEM_HEAD_DIM = 128
N_GROUPS = 8
EXPERTS_PER_GROUP = 8
N_EXPERTS = 64
TOP_K = 2
ROW_BLOCK = 128
XS_BUFFERS = 4
LANES = 128
VMEM_LIMIT = 56 * 1024 * 1024

TM_PROLOGUE = 512
TB_RWKV = 256
TM_MERGE = 512
TS_SCATTER = 256
TE_COMBINE = 256
SUBLANES = 8
META_COLS = 8
ISSUE_UNROLL = 2


def _bdot(a, b):
    return jnp.dot(a.astype(BF16), b.astype(BF16), preferred_element_type=F32)


def _bdot_nt(a, b):
    return lax.dot_general(a.astype(BF16), b.astype(BF16), (((1,), (1,)), ((), ())),
                           preferred_element_type=F32)


def _split_terms(x, n_terms):
    terms = []
    for _ in range(n_terms):
        t = x.astype(BF16)
        terms.append(t)
        x = x - t.astype(F32)
    return terms


def _split_dot_left(m_bf16, x, n_terms):
    return sum(jnp.dot(m_bf16, t, preferred_element_type=F32) for t in _split_terms(x, n_terms))


def _head_sums(x, seg_bf16, n_terms):
    w = seg_bf16.shape[0]
    terms = _split_terms(x, n_terms)
    halves = [sum(jnp.dot(t[:, c:c + w], seg_bf16, preferred_element_type=F32) for t in terms)
              for c in range(0, x.shape[1], w)]
    return jnp.concatenate(halves, axis=1)


def _rms(x, g):
    return x * lax.rsqrt(jnp.mean(x * x, axis=-1, keepdims=True) + NORM_EPS) * g


def _sigmoid(x):
    return 1.0 / (1.0 + jnp.exp(-x))


def _const_spec(shape):
    n = len(shape)
    return pl.BlockSpec(shape, lambda *_: (0,) * n)


def _memkv_kernel(mem_ref, g_ref, w_ref, k_ref, v_ref):
    mn = _rms(mem_ref[0], g_ref[...])
    kv = _bdot(mn, w_ref[...])
    k_ref[0] = kv[:, :D_BRANCH].astype(BF16)
    v_ref[0] = kv[:, D_BRANCH:].astype(BF16)


def _memkv(mem, g_mem, w_kv):
    b, m, d = mem.shape
    return pl.pallas_call(
        _memkv_kernel,
        grid=(b,),
        in_specs=[pl.BlockSpec((1, m, d), lambda i: (i, 0, 0)),
                  _const_spec((1, d)), _const_spec((d, 2 * D_BRANCH))],
        out_specs=[pl.BlockSpec((1, m, D_BRANCH), lambda i: (i, 0, 0)),
                   pl.BlockSpec((1, m, D_BRANCH), lambda i: (i, 0, 0))],
        out_shape=[jax.ShapeDtypeStruct((b, m, D_BRANCH), BF16)] * 2,
        compiler_params=pltpu.CompilerParams(dimension_semantics=("arbitrary",),
                                             vmem_limit_bytes=VMEM_LIMIT),
        name="memkv",
    )(mem, g_mem, w_kv)


def _prologue_kernel(x_ref, gmix_ref, win_ref, convw_ref, murkv_ref, muwag_ref,
                     wl1_ref, wl2_ref, w0_ref, al1_ref, al2_ref, a0_ref, gl1_ref, gl2_ref,
                     kk_ref, ka_ref, seg_ref, km_ref, vm_ref,
                     yconv_ref, ymem_ref, r_ref, k_ref, v_ref, kkn_ref, a_ref, lw_ref, g_ref,
                     prev_h, prev_p, prev_cu):
    tm = x_ref.shape[0]
    db = D_BRANCH

    @pl.when(pl.program_id(1) == 0)
    def _():
        prev_h[...] = jnp.zeros_like(prev_h)
        prev_p[...] = jnp.zeros_like(prev_p)
        prev_cu[...] = jnp.zeros_like(prev_cu)

    rows = lax.broadcasted_iota(jnp.int32, (tm, 1), 0)

    def shift1(u, prev_row):
        return jnp.where(rows == 0, prev_row, pltpu.roll(u, 1, axis=0))

    h = _rms(x_ref[...], gmix_ref[...])
    proj = _bdot(h, win_ref[...])

    bg, cg, u = proj[:, :db], proj[:, db:2 * db], proj[:, 2 * db:3 * db]
    cu = cg * u
    cu1 = shift1(cu, prev_cu[1:2, :])
    cu2 = jnp.where(rows == 0, prev_cu[0:1, :],
                    jnp.where(rows == 1, prev_cu[1:2, :], pltpu.roll(cu, 2, axis=0)))
    conv = cu2 * convw_ref[0:1, :] + cu1 * convw_ref[1:2, :] + cu * convw_ref[2:3, :]
    yconv_ref[...] = (bg * conv).astype(BF16)
    prev_cu[...] = cu[tm - 2:tm, :]

    pr = proj[:, 3 * db:6 * db]
    prs = shift1(pr, prev_p[...])
    mixed = pr + (prs - pr) * murkv_ref[...]
    prev_p[...] = pr[tm - 1:tm, :]
    r, k, v = mixed[:, :db], mixed[:, db:2 * db], mixed[:, 2 * db:]

    dh = shift1(h, prev_h[...]) - h
    prev_h[...] = h[tm - 1:tm, :]
    xw = h + dh * muwag_ref[0:1, :]
    xa = h + dh * muwag_ref[1:2, :]
    xg = h + dh * muwag_ref[2:3, :]
    zz = w0_ref[...] + _bdot(jnp.tanh(_bdot(xw, wl1_ref[...])), wl2_ref[...])
    softplus = jnp.maximum(-zz, 0.0) + jnp.log(1.0 + jnp.exp(-jnp.abs(zz)))
    lw_ref[...] = -jnp.exp(-softplus - 0.5)
    a = _sigmoid(a0_ref[...] + _bdot(_bdot(xa, al1_ref[...]), al2_ref[...]))
    g_ref[...] = _bdot(_sigmoid(_bdot(xg, gl1_ref[...])), gl2_ref[...])

    kk = k * kk_ref[...]
    ss = _head_sums(kk * kk, seg_ref[...], 1)
    kkn_ref[...] = kk * lax.rsqrt(jnp.maximum(ss, 1e-24))
    k_ref[...] = k * (1.0 + (a - 1.0) * ka_ref[...])
    r_ref[...] = r
    v_ref[...] = v
    a_ref[...] = a

    q = proj[:, 6 * db:]
    scale = MEM_HEAD_DIM ** -0.5
    for hh in range(MEM_HEADS):
        sl = slice(hh * MEM_HEAD_DIM, (hh + 1) * MEM_HEAD_DIM)
        s = _bdot_nt(q[:, sl], km_ref[0, :, sl]) * scale
        p = jnp.exp(s - jnp.max(s, axis=-1, keepdims=True))
        o = _bdot(p, vm_ref[0, :, sl]) / jnp.sum(p, axis=-1, keepdims=True)
        ymem_ref[:, sl] = o.astype(BF16)


def _prologue(x2, b, km, vm, p):
    t, d = x2.shape
    s = t // b
    tm = TM_PROLOGUE
    db = D_BRANCH
    m = km.shape[1]
    steps = s // tm
    tok = lambda c: pl.BlockSpec((tm, c), lambda bi, i: (bi * steps + i, 0))
    consts = [p["g_mix"], p["w_in"], p["conv_w"], p["mu_rkv"], p["mu_wag"],
              p["w_lora1"], p["w_lora2"], p["w0"], p["a_lora1"], p["a_lora2"], p["a0"],
              p["g_lora1"], p["g_lora2"], p["k_k"], p["k_a"], p["seg_ones"]]
    out_shapes = ([jax.ShapeDtypeStruct((t, db), BF16)] * 2
                  + [jax.ShapeDtypeStruct((t, db), F32)] * 7)
    return pl.pallas_call(
        _prologue_kernel,
        grid=(b, steps),
        in_specs=[tok(d)] + [_const_spec(c.shape) for c in consts]
                 + [pl.BlockSpec((1, m, db), lambda bi, i: (bi, 0, 0))] * 2,
        out_specs=[tok(db)] * 9,
        out_shape=out_shapes,
        scratch_shapes=[pltpu.VMEM((1, d), F32), pltpu.VMEM((1, 3 * db), F32),
                        pltpu.VMEM((2, db), F32)],
        compiler_params=pltpu.CompilerParams(dimension_semantics=("arbitrary", "arbitrary"),
                                             vmem_limit_bytes=VMEM_LIMIT),
        name="prologue",
    )(x2, *consts, km, vm)


def _rwkv_kernel(r_ref, k_ref, v_ref, kk_ref, a_ref, lw_ref, g_ref, rk_ref, lnw_ref, lnb_ref,
                 tri_ref, seg_ref, out_ref, h_scr, y_scr):
    tb = r_ref.shape[0]
    n = HEAD_DIM
    c_len = CHUNK

    @pl.when(pl.program_id(1) == 0)
    def _():
        h_scr[...] = jnp.zeros_like(h_scr)

    row2 = lax.broadcasted_iota(jnp.int32, (c_len, 2 * c_len), 0)
    col2 = lax.broadcasted_iota(jnp.int32, (c_len, 2 * c_len), 1) & (c_len - 1)
    strict2 = col2 < row2
    incl2 = col2 <= row2
    eye = (lax.broadcasted_iota(jnp.int32, (c_len, n), 0)
           == lax.broadcasted_iota(jnp.int32, (c_len, n), 1)).astype(F32)
    zeros = jnp.zeros((c_len, n), F32)

    def chunk_inputs(c):
        rows = pl.ds(pl.multiple_of(c * c_len, c_len), c_len)
        r = r_ref[rows, :]
        k = k_ref[rows, :]
        v = v_ref[rows, :]
        kk = kk_ref[rows, :]
        a = a_ref[rows, :]
        lw = lw_ref[rows, :]
        gcum = _split_dot_left(tri_ref[...], lw, 2)
        e_pos = jnp.exp(gcum)
        e_neg = jnp.exp(-gcum)
        p_last = jnp.exp(gcum[c_len - 1:c_len, :])
        bb = kk * a * e_neg
        kb = k * e_neg
        return dict(rows=rows, v=v, p_last=p_last, rb=r * e_pos, ab=-kk * jnp.exp(gcum - lw), bb=bb, kb=kb,
                    bk_t=jnp.concatenate([bb * p_last, kb * p_last], axis=0).T)

    def chunk_group(it, carry):
        chunks = [chunk_inputs(it * CHUNKS_PER_ITER + ci) for ci in range(CHUNKS_PER_ITER)]
        units = [(ci, hd) for ci in range(CHUNKS_PER_ITER) for hd in range(N_HEADS)]
        nu = range(len(units))
        ls = [slice(hd * n, (hd + 1) * n) for _, hd in units]
        ch = [chunks[ci] for ci, _ in units]
        al = [ch[u]["ab"][:, ls[u]] for u in nu]
        rr = [ch[u]["rb"][:, ls[u]] for u in nu]
        vv = [ch[u]["v"][:, ls[u]] for u in nu]
        aa = [_bdot_nt(jnp.concatenate([al[u], rr[u]], axis=0),
                       jnp.concatenate([ch[u]["bb"][:, ls[u]], ch[u]["kb"][:, ls[u]]], axis=0)) for u in nu]
        top = [jnp.where(strict2, aa[u][:c_len], 0.0) for u in nu]
        bot = [jnp.where(incl2, aa[u][c_len:], 0.0) for u in nu]
        a_ab = [top[u][:, :c_len] for u in nu]
        t_inv = [eye + a_ab[u] for u in nu]
        x_pow = [_bdot(a_ab[u], a_ab[u]) for u in nu]
        av = [_bdot(top[u], jnp.concatenate([zeros, vv[u]], axis=0)) for u in nu]
        for lvl in range(5):
            if lvl < 4:
                z = [_bdot(jnp.concatenate([t_inv[u], x_pow[u]], axis=0), x_pow[u]) for u in nu]
                t_inv = [t_inv[u] + z[u][:c_len] for u in nu]
                x_pow = [z[u][c_len:] for u in nu]
            else:
                t_inv = [t_inv[u] + _bdot(t_inv[u], x_pow[u]) for u in nu]
        w12 = [_bdot(t_inv[u], jnp.concatenate([al[u], av[u]], axis=1)) for u in nu]
        z2 = []
        for u in nu:
            hd = units[u][1]
            rhs2 = jnp.concatenate([w12[u], jnp.concatenate([zeros, vv[u]], axis=1)], axis=0)
            lhs3 = jnp.concatenate([ch[u]["bk_t"][hd * n:(hd + 1) * n, :], bot[u]], axis=0)
            z2.append(_bdot(lhs3, rhs2))
        state = [h_scr[hd] for hd in range(N_HEADS)]
        for u in nu:
            hd = units[u][1]
            mq = z2[u][:, :n] + jnp.concatenate([zeros, rr[u]], axis=0)
            out = _bdot(mq, state[hd]) + z2[u][:, n:]
            p_col = jnp.sum(eye * ch[u]["p_last"][:, ls[u]], axis=1, keepdims=True)
            state[hd] = p_col * state[hd] + out[:c_len]
            y_scr[ch[u]["rows"], ls[u]] = out[c_len:]
        for hd in range(N_HEADS):
            h_scr[hd] = state[hd]
        return carry

    lax.fori_loop(0, tb // (c_len * CHUNKS_PER_ITER), chunk_group, 0)

    y = y_scr[...]
    seg = seg_ref[...]
    inv_n = 1.0 / n
    mu = _head_sums(y, seg, 2) * inv_n
    yc = y - mu
    var = _head_sums(yc * yc, seg, 1) * inv_n
    yn = yc * lax.rsqrt(var + GN_EPS) * lnw_ref[...] + lnb_ref[...]
    bonus = _head_sums(r_ref[...] * k_ref[...] * rk_ref[...], seg, 1) * v_ref[...]
    out_ref[...] = ((yn + bonus) * g_ref[...]).astype(BF16)


def _rwkv(r, k, v, kkn, a, lw, g, b, p):
    t, db = r.shape
    tb = TB_RWKV
    steps = t // b // tb
    tok = pl.BlockSpec((tb, db), lambda bi, i: (bi * steps + i, 0))
    consts = [p["r_k"], p["ln_x_w"], p["ln_x_b"], p["tri"], p["seg_ones"]]
    return pl.pallas_call(
        _rwkv_kernel,
        grid=(b, steps),
        in_specs=[tok] * 7 + [_const_spec(c.shape) for c in consts],
        out_specs=tok,
        out_shape=jax.ShapeDtypeStruct((t, db), BF16),
        scratch_shapes=[pltpu.VMEM((N_HEADS, HEAD_DIM, HEAD_DIM), F32),
                        pltpu.VMEM((tb, db), F32)],
        compiler_params=pltpu.CompilerParams(dimension_semantics=("arbitrary", "arbitrary"),
                                             vmem_limit_bytes=VMEM_LIMIT),
        name="rwkv",
    )(r, k, v, kkn, a, lw, g, *consts)


def _merge_kernel(x_ref, yc_ref, yr_ref, ym_ref, gmix_ref, wgate_ref, bgate_ref, wbr_ref, wo_ref,
                  gffn_ref, wrt_ref, brt_ref, tril_ref,
                  x1_ref, meta_ref, metat_ref, cnt_ref, base_scr):
    tm, d = x_ref.shape

    @pl.when(pl.program_id(0) == 0)
    def _():
        base_scr[...] = jnp.zeros_like(base_scr)

    x = x_ref[...]
    hb = _rms(x, gmix_ref[...]).astype(BF16)
    z = jnp.zeros((tm, d), F32)
    for i, y_ref in enumerate((yc_ref, yr_ref, ym_ref)):
        cs = slice(i * d, (i + 1) * d)
        gate = _sigmoid(jnp.dot(hb, wgate_ref[:, cs], preferred_element_type=F32) + bgate_ref[:, cs])
        z = z + gate * jnp.dot(y_ref[...], wbr_ref[i], preferred_element_type=F32)
    x1 = x + _bdot(z, wo_ref[...])
    x1_ref[...] = x1

    h2 = _rms(x1, gffn_ref[...])
    h_hi, h_lo = _split_terms(h2, 2)
    w_hi, w_lo = _split_terms(wrt_ref[...], 2)
    hi_terms = jnp.dot(h_hi, jnp.concatenate([w_hi, w_lo], axis=1), preferred_element_type=F32)
    logits = (hi_terms[:, :LANES]
              + (jnp.dot(h_lo, w_hi, preferred_element_type=F32) + hi_terms[:, LANES:])) + brt_ref[...]
    lane = lax.broadcasted_iota(jnp.int32, (tm, LANES), 1)
    neg = jnp.float32(-jnp.inf)
    big = jnp.int32(1 << 20)
    gmask = (lane >= N_EXPERTS) & (lane < N_EXPERTS + N_GROUPS)
    glv = jnp.where(gmask, logits, neg)
    gmax = jnp.max(glv, axis=-1, keepdims=True)
    g_sel = jnp.min(jnp.where(glv == gmax, lane - N_EXPERTS, big), axis=-1, keepdims=True)
    g_w = 1.0 / jnp.sum(jnp.exp(glv - gmax), axis=-1, keepdims=True)
    emask = (lane < N_EXPERTS) & ((lane >> 3) == g_sel)
    elv = jnp.where(emask, logits, neg)
    emax = jnp.max(elv, axis=-1, keepdims=True)
    esum = jnp.sum(jnp.exp(elv - emax), axis=-1, keepdims=True)
    i1 = jnp.min(jnp.where(elv == emax, lane, big), axis=-1, keepdims=True)
    elv2 = jnp.where(lane == i1, neg, elv)
    m2 = jnp.max(elv2, axis=-1, keepdims=True)
    i2 = jnp.min(jnp.where(elv2 == m2, lane, big), axis=-1, keepdims=True)
    p1 = 1.0 / esum
    p2 = jnp.exp(m2 - emax) / esum
    c1 = g_w * p1 / (p1 + p2)
    c2 = g_w * p2 / (p1 + p2)

    oh1 = lane == i1
    oh2 = lane == i2
    onehot = jnp.where(oh1 | oh2, 1.0, 0.0)
    before = jnp.dot(tril_ref[...], onehot.astype(BF16), preferred_element_type=F32) + base_scr[...]
    rank1 = jnp.sum(jnp.where(oh1, before, 0.0), axis=-1, keepdims=True)
    rank2 = jnp.sum(jnp.where(oh2, before, 0.0), axis=-1, keepdims=True)
    new_base = base_scr[...] + jnp.sum(onehot, axis=0, keepdims=True)
    base_scr[...] = new_base
    cnt_ref[...] = jnp.broadcast_to(new_base, cnt_ref.shape)

    meta = jnp.where(lane == 0, i1.astype(F32),
           jnp.where(lane == 1, i2.astype(F32),
           jnp.where(lane == 2, rank1,
           jnp.where(lane == 3, rank2,
           jnp.where(lane == 4, c1,
           jnp.where(lane == 5, c2, 0.0))))))
    meta_ref[...] = meta[:, :META_COLS]
    metat_ref[...] = meta.T[:META_COLS, :]


def _merge(x2, yc, yr, ym, p):
    t, d = x2.shape
    tm = TM_MERGE
    db = D_BRANCH
    tok = lambda c: pl.BlockSpec((tm, c), lambda i: (i, 0))
    consts = [p["g_mix"], p["w_gate"], p["b_gate"], p["w_branch"], p["w_o"], p["g_ffn"],
              p["w_router"], p["b_router"], p["tril_strict"]]
    return pl.pallas_call(
        _merge_kernel,
        grid=(t // tm,),
        in_specs=[tok(d), tok(db), tok(db), tok(db)] + [_const_spec(c.shape) for c in consts],
        out_specs=[tok(d), tok(META_COLS), pl.BlockSpec((META_COLS, tm), lambda i: (0, i)),
                   _const_spec((8, LANES))],
        out_shape=[jax.ShapeDtypeStruct((t, d), F32), jax.ShapeDtypeStruct((t, META_COLS), F32),
                   jax.ShapeDtypeStruct((META_COLS, t), F32), jax.ShapeDtypeStruct((8, LANES), F32)],
        scratch_shapes=[pltpu.VMEM((1, LANES), F32)],
        compiler_params=pltpu.CompilerParams(dimension_semantics=("arbitrary",),
                                             vmem_limit_bytes=VMEM_LIMIT),
        name="merge",
    )(x2, yc, yr, ym, *consts)


def _store_packed_rows(ref2d, x, stage):
    rows, d = x.shape
    nt = d // (2 * LANES)
    for c in range(nt):
        stage[c, pl.ds(0, rows, stride=2), :] = x[:, c * LANES:(c + 1) * LANES]
        stage[c, pl.ds(1, rows, stride=2), :] = x[:, (c + nt) * LANES:(c + nt + 1) * LANES]
        ref2d[pl.ds(c, rows, stride=nt), :] = pltpu.bitcast(stage[c].astype(BF16), jnp.uint32)


def _load_packed_rows(ref2d, rows, nt, stage):
    lo, hi = [], []
    for c in range(nt):
        stage[c] = pltpu.bitcast(ref2d[pl.ds(c, rows, stride=nt), :], BF16).astype(F32)
        lo.append(stage[c, pl.ds(0, rows, stride=2), :])
        hi.append(stage[c, pl.ds(1, rows, stride=2), :])
    return jnp.concatenate(lo + hi, axis=1)


def _scatter_kernel(dest0_ref, dest1_ref, x1_ref, gffn_ref, xs_ref, hbuf, stage, sem):
    dest_refs = (dest0_ref, dest1_ref)
    ts, d_model = x1_ref.shape
    nt = d_model // (2 * LANES)
    s = pl.program_id(0)
    slot = s % 2

    def wait_slot(sl):
        for _ in range(TOP_K):
            pltpu.make_async_copy(hbuf.at[sl], xs_ref.at[pl.ds(0, ts * nt), :], sem.at[sl]).wait()

    @pl.when(s >= 2)
    def _():
        wait_slot(slot)

    _store_packed_rows(hbuf.at[slot], _rms(x1_ref[...], gffn_ref[...]), stage)

    def issue(grp, carry):
        grp_off = pl.multiple_of(grp * (SUBLANES * nt), SUBLANES * nt)
        for j in range(SUBLANES):
            tok = s * ts + grp * SUBLANES + j
            for kslot in range(TOP_K):
                d = pl.multiple_of(dest_refs[kslot][tok], nt)
                pltpu.make_async_copy(hbuf.at[slot, pl.ds(grp_off + j * nt, nt), :],
                                      xs_ref.at[pl.ds(d, nt), :], sem.at[slot]).start(priority=kslot)
        return carry

    lax.fori_loop(0, ts // SUBLANES, issue, 0, unroll=ISSUE_UNROLL)

    @pl.when(s == pl.num_programs(0) - 1)
    def _():
        @pl.when(s >= 1)
        def _():
            wait_slot(1 - slot)
        wait_slot(slot)


def _scatter(dests, x1, g_ffn, n_rows):
    t, d = x1.shape
    ts = TS_SCATTER
    pt = d // (2 * LANES)
    return pl.pallas_call(
        _scatter_kernel,
        grid_spec=pltpu.PrefetchScalarGridSpec(
            num_scalar_prefetch=TOP_K,
            grid=(t // ts,),
            in_specs=[pl.BlockSpec((ts, d), lambda i, *_: (i, 0)),
                      pl.BlockSpec((1, d), lambda i, *_: (0, 0))],
            out_specs=pl.BlockSpec(memory_space=pl.ANY),
            scratch_shapes=[pltpu.VMEM((2, ts * pt, LANES), jnp.uint32),
                            pltpu.VMEM((pt, 2 * ts, LANES), F32),
                            pltpu.SemaphoreType.DMA((2,))],
        ),
        out_shape=jax.ShapeDtypeStruct((n_rows * pt, LANES), jnp.uint32),
        compiler_params=pltpu.CompilerParams(dimension_semantics=("arbitrary",),
                                             vmem_limit_bytes=VMEM_LIMIT),
        name="scatter",
    )(*dests, x1, g_ffn)


def _experts_kernel(be_ref, nused_ref, nexte_ref, xs_ref, wg_hbm, wu_hbm, wd_hbm, ys_ref,
                    wg_f, wu_f, wd_f, wg_s, wu_s, wd_s, xbuf, ybuf, xstage, ystage, sem, xsem, ysem):
    i = pl.program_id(0)
    e = be_ref[i]
    prev = be_ref[jnp.maximum(i - 1, 0)]
    active = i < nused_ref[0]

    def weight_copies(ex):
        return (pltpu.make_async_copy(wg_hbm.at[ex], wg_f, sem.at[0]),
                pltpu.make_async_copy(wu_hbm.at[ex], wu_f, sem.at[1]),
                pltpu.make_async_copy(wd_hbm.at[ex], wd_f, sem.at[2]))

    @pl.when(i == 0)
    def _():
        for cp in weight_copies(e):
            cp.start(priority=1)

    @pl.when(active & ((i == 0) | (e != prev)))
    def _():
        for cp in weight_copies(e):
            cp.wait()
        wg_s[...] = wg_f[...].astype(BF16)
        wu_s[...] = wu_f[...].astype(BF16)
        wd_s[...] = wd_f[...].astype(BF16)
        nxt = nexte_ref[e]

        @pl.when(nxt < N_EXPERTS)
        def _():
            for cp in weight_copies(nxt):
                cp.start(priority=1)

    blk_rows = xbuf.shape[1]
    n_used = nused_ref[0]

    def xs_copy(blk, slot):
        return pltpu.make_async_copy(xs_ref.at[pl.ds(pl.multiple_of(blk * blk_rows, blk_rows), blk_rows), :],
                                     xbuf.at[slot], xsem.at[slot])

    def ys_copy(blk, slot):
        return pltpu.make_async_copy(ybuf.at[slot],
                                     ys_ref.at[pl.ds(pl.multiple_of(blk * blk_rows, blk_rows), blk_rows), :],
                                     ysem.at[slot])

    @pl.when(i == 0)
    def _():
        for ahead in range(XS_BUFFERS - 1):
            @pl.when(ahead < n_used)
            def _():
                xs_copy(ahead, ahead).start()

    @pl.when(active)
    def _():
        nt = wg_s.shape[0] // (2 * LANES)
        fetch = i + (XS_BUFFERS - 1)

        @pl.when(fetch < n_used)
        def _():
            xs_copy(fetch, fetch % XS_BUFFERS).start()

        xs_copy(i, i % XS_BUFFERS).wait()
        oslot = i % 2

        @pl.when(i >= 2)
        def _():
            ys_copy(i - 2, oslot).wait()

        xb = _load_packed_rows(xbuf.at[i % XS_BUFFERS], ROW_BLOCK, nt, xstage).astype(BF16)
        gate = jnp.dot(xb, wg_s[...], preferred_element_type=F32)
        up = jnp.dot(xb, wu_s[...], preferred_element_type=F32)
        hid = gate * _sigmoid(gate) * up
        _store_packed_rows(ybuf.at[oslot], jnp.dot(hid.astype(BF16), wd_s[...], preferred_element_type=F32),
                           ystage)
        ys_copy(i, oslot).start()

        @pl.when(i == n_used - 1)
        def _():
            @pl.when(i >= 1)
            def _():
                ys_copy(i - 1, 1 - oslot).wait()
            ys_copy(i, oslot).wait()


def _experts(blk_expert, n_used, next_expert, xs, w_gate, w_up, w_down):
    d, de = w_gate.shape[-2:]
    pt = d // (2 * LANES)
    blk_rows = ROW_BLOCK * pt
    nb = xs.shape[0] // blk_rows

    return pl.pallas_call(
        _experts_kernel,
        grid_spec=pltpu.PrefetchScalarGridSpec(
            num_scalar_prefetch=3,
            grid=(nb,),
            in_specs=[pl.BlockSpec(memory_space=pl.ANY)] * 4,
            out_specs=pl.BlockSpec(memory_space=pl.ANY),
            scratch_shapes=[pltpu.VMEM((d, de), F32), pltpu.VMEM((d, de), F32), pltpu.VMEM((de, d), F32),
                            pltpu.VMEM((d, de), BF16), pltpu.VMEM((d, de), BF16), pltpu.VMEM((de, d), BF16),
                            pltpu.VMEM((XS_BUFFERS, blk_rows, LANES), jnp.uint32),
                            pltpu.VMEM((2, blk_rows, LANES), jnp.uint32),
                            pltpu.VMEM((pt, 2 * ROW_BLOCK, LANES), F32),
                            pltpu.VMEM((pt, 2 * ROW_BLOCK, LANES), F32),
                            pltpu.SemaphoreType.DMA((3,)), pltpu.SemaphoreType.DMA((XS_BUFFERS,)),
                            pltpu.SemaphoreType.DMA((2,))],
        ),
        out_shape=jax.ShapeDtypeStruct(xs.shape, jnp.uint32),
        compiler_params=pltpu.CompilerParams(dimension_semantics=("arbitrary",),
                                             vmem_limit_bytes=VMEM_LIMIT),
        name="experts",
    )(blk_expert, n_used, next_expert, xs, w_gate, w_up, w_down)


def _combine_kernel(dest0_ref, dest1_ref, x1_ref, meta_ref, gfin_ref, ys_ref, out_ref, ybuf, stage, sem):
    dest_refs = (dest0_ref, dest1_ref)
    te = x1_ref.shape[0]
    s = pl.program_id(0)
    nsteps = pl.num_programs(0)
    slot = s % 2

    nt = x1_ref.shape[1] // (2 * LANES)

    def issue_step(step, sl):
        def issue(grp, carry):
            grp_off = pl.multiple_of(grp * (SUBLANES * nt), SUBLANES * nt)
            for j in range(SUBLANES):
                tok = step * te + grp * SUBLANES + j
                for kslot in range(TOP_K):
                    d = pl.multiple_of(dest_refs[kslot][tok], nt)
                    pltpu.make_async_copy(ys_ref.at[pl.ds(d, nt), :],
                                          ybuf.at[sl, kslot, pl.ds(grp_off + j * nt, nt), :],
                                          sem.at[sl]).start(priority=kslot)
            return carry
        lax.fori_loop(0, te // SUBLANES, issue, 0, unroll=ISSUE_UNROLL)

    @pl.when(s == 0)
    def _():
        issue_step(0, 0)

    @pl.when(s + 1 < nsteps)
    def _():
        issue_step(s + 1, 1 - slot)

    for kslot in range(TOP_K):
        pltpu.make_async_copy(ys_ref.at[pl.ds(0, te * nt), :], ybuf.at[slot, kslot], sem.at[slot]).wait()

    meta = meta_ref[...]
    y0 = _load_packed_rows(ybuf.at[slot, 0], te, nt, stage.at[0])
    y1 = _load_packed_rows(ybuf.at[slot, 1], te, nt, stage.at[1])
    x2 = x1_ref[...] + y0 * meta[:, 4:5] + y1 * meta[:, 5:6]
    out_ref[...] = _rms(x2, gfin_ref[...])


def _combine(dests, x1, meta, g_final, ys):
    t, d = x1.shape
    te = TE_COMBINE
    pt = d // (2 * LANES)
    return pl.pallas_call(
        _combine_kernel,
        grid_spec=pltpu.PrefetchScalarGridSpec(
            num_scalar_prefetch=TOP_K,
            grid=(t // te,),
            in_specs=[pl.BlockSpec((te, d), lambda i, *_: (i, 0)),
                      pl.BlockSpec((te, META_COLS), lambda i, *_: (i, 0)),
                      pl.BlockSpec((1, d), lambda i, *_: (0, 0)),
                      pl.BlockSpec(memory_space=pl.ANY)],
            out_specs=pl.BlockSpec((te, d), lambda i, *_: (i, 0)),
            scratch_shapes=[pltpu.VMEM((2, TOP_K, te * pt, LANES), jnp.uint32),
                            pltpu.VMEM((TOP_K, pt, 2 * te, LANES), F32),
                            pltpu.SemaphoreType.DMA((2,))],
        ),
        out_shape=jax.ShapeDtypeStruct((t, d), F32),
        compiler_params=pltpu.CompilerParams(dimension_semantics=("arbitrary",),
                                             vmem_limit_bytes=VMEM_LIMIT),
        name="combine",
    )(*dests, x1, meta, g_final, ys)


def _constants(tm_merge):
    n = CHUNK
    tri = (jnp.arange(n)[:, None] >= jnp.arange(n)[None, :]).astype(BF16)
    head = jnp.arange(2 * LANES) // HEAD_DIM
    seg_ones = (head[:, None] == head[None, :]).astype(BF16)
    tril_strict = (jnp.arange(tm_merge)[:, None] > jnp.arange(tm_merge)[None, :]).astype(BF16)
    return tri, seg_ones, tril_strict


def kernel(x, mem, g_mix, g_mem, w_in, conv_w, mu_rkv, mu_wag, w_lora1, w_lora2, w0, a_lora1, a_lora2, a0, g_lora1, g_lora2, k_k, k_a, r_k, ln_x_w, ln_x_b, w_kv_mem, w_branch, w_gate, b_gate, w_o, g_ffn, w_router_group, b_router_group, w_router_expert, b_router_expert, w_exp_gate, w_exp_up, w_exp_down, g_final):
    assert g_mix.shape[0] == 1, "single-layer block"
    b, s, d = x.shape
    t = b * s
    db = D_BRANCH
    tri, seg_ones, tril_strict = _constants(TM_MERGE)
    row = lambda a: a.reshape(1, -1)
    pad_r = LANES - N_EXPERTS - N_GROUPS
    p = {
        "g_mix": row(g_mix[0]), "w_in": w_in[0].astype(BF16), "conv_w": conv_w[0].T,
        "mu_rkv": row(mu_rkv[0]), "mu_wag": mu_wag[0],
        "w_lora1": w_lora1[0].astype(BF16), "w_lora2": w_lora2[0].astype(BF16), "w0": row(w0[0]),
        "a_lora1": a_lora1[0].astype(BF16), "a_lora2": a_lora2[0].astype(BF16), "a0": row(a0[0]),
        "g_lora1": g_lora1[0].astype(BF16), "g_lora2": g_lora2[0].astype(BF16),
        "k_k": row(k_k[0]), "k_a": row(k_a[0]), "r_k": row(r_k[0]),
        "ln_x_w": row(ln_x_w[0]), "ln_x_b": row(ln_x_b[0]),
        "w_gate": w_gate[0].astype(BF16), "b_gate": row(b_gate[0]),
        "w_branch": w_branch[0].astype(BF16), "w_o": w_o[0].astype(BF16), "g_ffn": row(g_ffn[0]),
        "w_router": jnp.concatenate([w_router_expert[0], w_router_group[0],
                                     jnp.zeros((d, pad_r), F32)], axis=1),
        "b_router": row(jnp.concatenate([b_router_expert[0], b_router_group[0],
                                         jnp.zeros((pad_r,), F32)])),
        "tri": tri, "seg_ones": seg_ones, "tril_strict": tril_strict,
    }

    km, vm = _memkv(mem, row(g_mem[0]), w_kv_mem[0].astype(BF16))
    x2 = x.reshape(t, d)
    yconv, ymem, r, k, v, kkn, a, lw, g = _prologue(x2, b, km, vm, p)
    yrwkv = _rwkv(r, k, v, kkn, a, lw, g, b, p)
    x1, meta, meta_t, cnt = _merge(x2, yconv, yrwkv, ymem, p)

    counts = cnt[0, :N_EXPERTS].astype(jnp.int32)
    padded = ((counts + ROW_BLOCK - 1) // ROW_BLOCK) * ROW_BLOCK
    pad_end = jnp.cumsum(padded)
    pad_start = pad_end - padded
    n_blocks = (t * TOP_K) // ROW_BLOCK + N_EXPERTS
    eids = jnp.arange(N_EXPERTS, dtype=jnp.int32)
    e_idx = meta_t[0:TOP_K].astype(jnp.int32)
    rank = meta_t[TOP_K:2 * TOP_K].astype(jnp.int32)
    start_of = jnp.sum(jnp.where(e_idx[:, None, :] == eids[None, :, None], pad_start[None, :, None], 0), axis=1)
    dest = (start_of + rank) * (d // (2 * LANES))
    dests = [dest[kslot] for kslot in range(TOP_K)]
    blk_start = jnp.arange(n_blocks, dtype=jnp.int32) * ROW_BLOCK
    blk_expert = jnp.minimum(jnp.sum((pad_end[None, :] <= blk_start[:, None]).astype(jnp.int32), axis=1),
                             N_EXPERTS - 1)
    n_used = (pad_end[-1:] // ROW_BLOCK).astype(jnp.int32)
    later_nonempty = (eids[None, :] > eids[:, None]) & (counts[None, :] > 0)
    next_expert = jnp.min(jnp.where(later_nonempty, eids[None, :], N_EXPERTS), axis=1)

    xs = _scatter(dests, x1, p["g_ffn"], n_blocks * ROW_BLOCK)
    ys = _experts(blk_expert, n_used, next_expert, xs, w_exp_gate[0], w_exp_up[0], w_exp_down[0])
    out = _combine(dests, x1, meta, row(g_final), ys)
    return out.reshape(b, s, d)
```

```python
import functools

import jax
import jax.numpy as jnp
from jax import lax
from jax.experimental import pallas as pl
from jax.experimental.pallas import tpu as pltpu

F32 = jnp.float32
BF16 = jnp.bfloat16

NORM_EPS = 1e-6
GN_EPS = 64e-5
D_BRANCH = 512
HEAD_DIM = 64
N_HEADS = 8
CHUNK = 64
CHUNKS_PER_ITER = 4
MEM_HEADS = 4
MEM_HEAD_DIM = 128
N_GROUPS = 8
EXPERTS_PER_GROUP = 8
N_EXPERTS = 64
TOP_K = 2
ROW_BLOCK = 128
XS_BUFFERS = 4
LANES = 128
VMEM_LIMIT = 56 * 1024 * 1024

TM_PROLOGUE = 512
PROLOGUE_SUB = 128
TB_RWKV = 256
TM_MERGE = 512
MERGE_SUB = 256
TS_SCATTER = 256
TE_COMBINE = 256
SUBLANES = 8
META_COLS = 8
ISSUE_UNROLL = 2


def _bdot(a, b):
    return jnp.dot(a.astype(BF16), b.astype(BF16), preferred_element_type=F32)


def _bdot_nt(a, b):
    return lax.dot_general(a.astype(BF16), b.astype(BF16), (((1,), (1,)), ((), ())),
                           preferred_element_type=F32)


def _split_terms(x, n_terms):
    terms = []
    for _ in range(n_terms):
        t = x.astype(BF16)
        terms.append(t)
        x = x - t.astype(F32)
    return terms


def _split_dot_left(m_bf16, x, n_terms):
    return sum(jnp.dot(m_bf16, t, preferred_element_type=F32) for t in _split_terms(x, n_terms))


def _head_sums(x, seg_bf16, n_terms):
    w = seg_bf16.shape[0]
    terms = _split_terms(x, n_terms)
    halves = [sum(jnp.dot(t[:, c:c + w], seg_bf16, preferred_element_type=F32) for t in terms)
              for c in range(0, x.shape[1], w)]
    return jnp.concatenate(halves, axis=1)


def _rms(x, g):
    return x * lax.rsqrt(jnp.mean(x * x, axis=-1, keepdims=True) + NORM_EPS) * g


def _sigmoid(x):
    return 1.0 / (1.0 + jnp.exp(-x))


def _run_together(*gens):
    live = list(gens)
    while live:
        for gen in list(live):
            try:
                next(gen)
            except StopIteration:
                live.remove(gen)


def _software_pipeline(heavy, light, n_sub):
    _run_together(heavy(0))
    for j in range(1, n_sub):
        _run_together(heavy(j), light(j - 1))
    _run_together(light(n_sub - 1))


def _const_spec(shape):
    n = len(shape)
    return pl.BlockSpec(shape, lambda *_: (0,) * n)


def _memkv_kernel(mem_ref, g_ref, w_ref, k_ref, v_ref):
    mn = _rms(mem_ref[0], g_ref[...])
    kv = _bdot(mn, w_ref[...])
    k_ref[0] = kv[:, :D_BRANCH].astype(BF16)
    v_ref[0] = kv[:, D_BRANCH:].astype(BF16)


def _memkv(mem, g_mem, w_kv):
    b, m, d = mem.shape
    return pl.pallas_call(
        _memkv_kernel,
        grid=(b,),
        in_specs=[pl.BlockSpec((1, m, d), lambda i: (i, 0, 0)),
                  _const_spec((1, d)), _const_spec((d, 2 * D_BRANCH))],
        out_specs=[pl.BlockSpec((1, m, D_BRANCH), lambda i: (i, 0, 0)),
                   pl.BlockSpec((1, m, D_BRANCH), lambda i: (i, 0, 0))],
        out_shape=[jax.ShapeDtypeStruct((b, m, D_BRANCH), BF16)] * 2,
        compiler_params=pltpu.CompilerParams(dimension_semantics=("arbitrary",),
                                             vmem_limit_bytes=VMEM_LIMIT),
        name="memkv",
    )(mem, g_mem, w_kv)


def _prologue_kernel(x_ref, gmix_ref, win_ref, convw_ref, murkv_ref, muwag_ref,
                     wl1_ref, wl2_ref, w0_ref, al1_ref, al2_ref, a0_ref, gl1_ref, gl2_ref,
                     kk_ref, ka_ref, seg_ref, km_ref, vm_ref,
                     yconv_ref, ymem_ref, r_ref, k_ref, v_ref, kkn_ref, a_ref, lw_ref, g_ref,
                     prev_h, prev_p, prev_cu):
    tm = x_ref.shape[0]
    db = D_BRANCH
    sub = PROLOGUE_SUB
    n_parts = win_ref.shape[1] // db

    @pl.when(pl.program_id(1) == 0)
    def _():
        prev_h[...] = jnp.zeros_like(prev_h)
        prev_p[...] = jnp.zeros_like(prev_p)
        prev_cu[...] = jnp.zeros_like(prev_cu)

    rows = lax.broadcasted_iota(jnp.int32, (sub, 1), 0)

    def shift1(u, prev_row):
        return jnp.where(rows == 0, prev_row, pltpu.roll(u, 1, axis=0))

    carry = {"h": prev_h[...], "p": prev_p[...], "cu": prev_cu[...]}
    projected = {}

    def project(j):
        h = _rms(x_ref[j * sub:(j + 1) * sub, :], gmix_ref[...])
        hb = h.astype(BF16)
        parts = []
        for c in range(n_parts):
            parts.append(jnp.dot(hb, win_ref[:, c * db:(c + 1) * db], preferred_element_type=F32))
            yield
        projected[j] = (h, parts)

    def mix(j):
        rs = slice(j * sub, (j + 1) * sub)
        h, (bg, cg, u, rp, kp, vp, q) = projected.pop(j)

        cu = cg * u
        pcu = carry["cu"]
        cu1 = shift1(cu, pcu[1:2, :])
        cu2 = jnp.where(rows == 0, pcu[0:1, :], jnp.where(rows == 1, pcu[1:2, :], pltpu.roll(cu, 2, axis=0)))
        conv = cu2 * convw_ref[0:1, :] + cu1 * convw_ref[1:2, :] + cu * convw_ref[2:3, :]
        yconv_ref[rs, :] = (bg * conv).astype(BF16)
        carry["cu"] = cu[sub - 2:sub, :]

        pr = jnp.concatenate([rp, kp, vp], axis=1)
        prs = shift1(pr, carry["p"])
        mixed = pr + (prs - pr) * murkv_ref[...]
        carry["p"] = pr[sub - 1:sub, :]
        r, k, v = mixed[:, :db], mixed[:, db:2 * db], mixed[:, 2 * db:]
        r_ref[rs, :] = r
        v_ref[rs, :] = v

        dh = shift1(h, carry["h"]) - h
        carry["h"] = h[sub - 1:sub, :]
        lora_w = _bdot(h + dh * muwag_ref[0:1, :], wl1_ref[...])
        lora_a = _bdot(h + dh * muwag_ref[1:2, :], al1_ref[...])
        lora_g = _bdot(h + dh * muwag_ref[2:3, :], gl1_ref[...])
        yield
        zz = w0_ref[...] + _bdot(jnp.tanh(lora_w), wl2_ref[...])
        a_lin = a0_ref[...] + _bdot(lora_a, al2_ref[...])
        g_ref[rs, :] = _bdot(_sigmoid(lora_g), gl2_ref[...])
        yield
        softplus = jnp.maximum(-zz, 0.0) + jnp.log(1.0 + jnp.exp(-jnp.abs(zz)))
        lw_ref[rs, :] = -jnp.exp(-softplus - 0.5)
        a = _sigmoid(a_lin)
        a_ref[rs, :] = a
        k_ref[rs, :] = k * (1.0 + (a - 1.0) * ka_ref[...])
        kk = k * kk_ref[...]
        ss = _head_sums(kk * kk, seg_ref[...], 1)
        yield
        kkn_ref[rs, :] = kk * lax.rsqrt(jnp.maximum(ss, 1e-24))

        scale = MEM_HEAD_DIM ** -0.5
        heads = [slice(hh * MEM_HEAD_DIM, (hh + 1) * MEM_HEAD_DIM) for hh in range(MEM_HEADS)]
        scores = [_bdot_nt(q[:, sl], km_ref[0, :, sl]) * scale for sl in heads]
        yield
        for sl, s in zip(heads, scores):
            p = jnp.exp(s - jnp.max(s, axis=-1, keepdims=True))
            o = _bdot(p, vm_ref[0, :, sl]) / jnp.sum(p, axis=-1, keepdims=True)
            ymem_ref[rs, sl] = o.astype(BF16)

    _software_pipeline(project, mix, tm // sub)
    prev_h[...] = carry["h"]
    prev_p[...] = carry["p"]
    prev_cu[...] = carry["cu"]


def _prologue(x2, b, km, vm, p):
    t, d = x2.shape
    s = t // b
    tm = TM_PROLOGUE
    db = D_BRANCH
    m = km.shape[1]
    steps = s // tm
    tok = lambda c: pl.BlockSpec((tm, c), lambda bi, i: (bi * steps + i, 0))
    consts = [p["g_mix"], p["w_in"], p["conv_w"], p["mu_rkv"], p["mu_wag"],
              p["w_lora1"], p["w_lora2"], p["w0"], p["a_lora1"], p["a_lora2"], p["a0"],
              p["g_lora1"], p["g_lora2"], p["k_k"], p["k_a"], p["seg_ones"]]
    out_shapes = ([jax.ShapeDtypeStruct((t, db), BF16)] * 2
                  + [jax.ShapeDtypeStruct((t, db), F32)] * 7)
    return pl.pallas_call(
        _prologue_kernel,
        grid=(b, steps),
        in_specs=[tok(d)] + [_const_spec(c.shape) for c in consts]
                 + [pl.BlockSpec((1, m, db), lambda bi, i: (bi, 0, 0))] * 2,
        out_specs=[tok(db)] * 9,
        out_shape=out_shapes,
        scratch_shapes=[pltpu.VMEM((1, d), F32), pltpu.VMEM((1, 3 * db), F32),
                        pltpu.VMEM((2, db), F32)],
        compiler_params=pltpu.CompilerParams(dimension_semantics=("arbitrary", "arbitrary"),
                                             vmem_limit_bytes=VMEM_LIMIT),
        name="prologue",
    )(x2, *consts, km, vm)


def _rwkv_kernel(r_ref, k_ref, v_ref, kk_ref, a_ref, lw_ref, g_ref, rk_ref, lnw_ref, lnb_ref,
                 tri_ref, seg_ref, out_ref, h_scr, y_scr):
    tb = r_ref.shape[0]
    n = HEAD_DIM
    c_len = CHUNK

    @pl.when(pl.program_id(1) == 0)
    def _():
        h_scr[...] = jnp.zeros_like(h_scr)

    row2 = lax.broadcasted_iota(jnp.int32, (c_len, 2 * c_len), 0)
    col2 = lax.broadcasted_iota(jnp.int32, (c_len, 2 * c_len), 1) & (c_len - 1)
    strict2 = col2 < row2
    incl2 = col2 <= row2
    eye = (lax.broadcasted_iota(jnp.int32, (c_len, n), 0)
           == lax.broadcasted_iota(jnp.int32, (c_len, n), 1)).astype(F32)
    zeros = jnp.zeros((c_len, n), F32)

    def chunk_inputs(c):
        rows = pl.ds(pl.multiple_of(c * c_len, c_len), c_len)
        r = r_ref[rows, :]
        k = k_ref[rows, :]
        v = v_ref[rows, :]
        kk = kk_ref[rows, :]
        a = a_ref[rows, :]
        lw = lw_ref[rows, :]
        gcum = _split_dot_left(tri_ref[...], lw, 2)
        e_pos = jnp.exp(gcum)
        e_neg = jnp.exp(-gcum)
        p_last = jnp.exp(gcum[c_len - 1:c_len, :])
        bb = kk * a * e_neg
        kb = k * e_neg
        return dict(rows=rows, v=v, p_last=p_last, rb=r * e_pos, ab=-kk * jnp.exp(gcum - lw), bb=bb, kb=kb,
                    bk_t=jnp.concatenate([bb * p_last, kb * p_last], axis=0).T)

    def chunk_group(it, carry):
        chunks = [chunk_inputs(it * CHUNKS_PER_ITER + ci) for ci in range(CHUNKS_PER_ITER)]
        units = [(ci, hd) for ci in range(CHUNKS_PER_ITER) for hd in range(N_HEADS)]
        nu = range(len(units))
        ls = [slice(hd * n, (hd + 1) * n) for _, hd in units]
        ch = [chunks[ci] for ci, _ in units]
        al = [ch[u]["ab"][:, ls[u]] for u in nu]
        rr = [ch[u]["rb"][:, ls[u]] for u in nu]
        vv = [ch[u]["v"][:, ls[u]] for u in nu]
        aa = [_bdot_nt(jnp.concatenate([al[u], rr[u]], axis=0),
                       jnp.concatenate([ch[u]["bb"][:, ls[u]], ch[u]["kb"][:, ls[u]]], axis=0)) for u in nu]
        top = [jnp.where(strict2, aa[u][:c_len], 0.0) for u in nu]
        bot = [jnp.where(incl2, aa[u][c_len:], 0.0) for u in nu]
        a_ab = [top[u][:, :c_len] for u in nu]
        t_inv = [eye + a_ab[u] for u in nu]
        x_pow = [_bdot(a_ab[u], a_ab[u]) for u in nu]
        av = [_bdot(top[u], jnp.concatenate([zeros, vv[u]], axis=0)) for u in nu]
        for lvl in range(5):
            if lvl < 4:
                z = [_bdot(jnp.concatenate([t_inv[u], x_pow[u]], axis=0), x_pow[u]) for u in nu]
                t_inv = [t_inv[u] + z[u][:c_len] for u in nu]
                x_pow = [z[u][c_len:] for u in nu]
            else:
                t_inv = [t_inv[u] + _bdot(t_inv[u], x_pow[u]) for u in nu]
        w12 = [_bdot(t_inv[u], jnp.concatenate([al[u], av[u]], axis=1)) for u in nu]
        z2 = []
        for u in nu:
            hd = units[u][1]
            rhs2 = jnp.concatenate([w12[u], jnp.concatenate([zeros, vv[u]], axis=1)], axis=0)
            lhs3 = jnp.concatenate([ch[u]["bk_t"][hd * n:(hd + 1) * n, :], bot[u]], axis=0)
            z2.append(_bdot(lhs3, rhs2))
        state = [h_scr[hd] for hd in range(N_HEADS)]
        for u in nu:
            hd = units[u][1]
            mq = z2[u][:, :n] + jnp.concatenate([zeros, rr[u]], axis=0)
            out = _bdot(mq, state[hd]) + z2[u][:, n:]
            p_col = jnp.sum(eye * ch[u]["p_last"][:, ls[u]], axis=1, keepdims=True)
            state[hd] = p_col * state[hd] + out[:c_len]
            y_scr[ch[u]["rows"], ls[u]] = out[c_len:]
        for hd in range(N_HEADS):
            h_scr[hd] = state[hd]
        return carry

    lax.fori_loop(0, tb // (c_len * CHUNKS_PER_ITER), chunk_group, 0)

    y = y_scr[...]
    seg = seg_ref[...]
    inv_n = 1.0 / n
    mu = _head_sums(y, seg, 2) * inv_n
    yc = y - mu
    var = _head_sums(yc * yc, seg, 1) * inv_n
    yn = yc * lax.rsqrt(var + GN_EPS) * lnw_ref[...] + lnb_ref[...]
    bonus = _head_sums(r_ref[...] * k_ref[...] * rk_ref[...], seg, 1) * v_ref[...]
    out_ref[...] = ((yn + bonus) * g_ref[...]).astype(BF16)


def _rwkv(r, k, v, kkn, a, lw, g, b, p):
    t, db = r.shape
    tb = TB_RWKV
    steps = t // b // tb
    tok = pl.BlockSpec((tb, db), lambda bi, i: (bi * steps + i, 0))
    consts = [p["r_k"], p["ln_x_w"], p["ln_x_b"], p["tri"], p["seg_ones"]]
    return pl.pallas_call(
        _rwkv_kernel,
        grid=(b, steps),
        in_specs=[tok] * 7 + [_const_spec(c.shape) for c in consts],
        out_specs=tok,
        out_shape=jax.ShapeDtypeStruct((t, db), BF16),
        scratch_shapes=[pltpu.VMEM((N_HEADS, HEAD_DIM, HEAD_DIM), F32),
                        pltpu.VMEM((tb, db), F32)],
        compiler_params=pltpu.CompilerParams(dimension_semantics=("arbitrary", "arbitrary"),
                                             vmem_limit_bytes=VMEM_LIMIT),
        name="rwkv",
    )(r, k, v, kkn, a, lw, g, *consts)


def _merge_kernel(x_ref, yc_ref, yr_ref, ym_ref, gmix_ref, wgate_ref, bgate_ref, wbr_ref, wo_ref,
                  gffn_ref, wrt_ref, brt_ref, tril_ref,
                  x1_ref, meta_ref, metat_ref, cnt_ref, base_scr):
    tm, d = x_ref.shape

    @pl.when(pl.program_id(0) == 0)
    def _():
        base_scr[...] = jnp.zeros_like(base_scr)

    sub = tril_ref.shape[0]
    lane = lax.broadcasted_iota(jnp.int32, (sub, LANES), 1)
    neg = jnp.float32(-jnp.inf)
    big = jnp.int32(1 << 20)
    w_hi, w_lo = _split_terms(wrt_ref[...], 2)
    w_hi_lo = jnp.concatenate([w_hi, w_lo], axis=1)
    state = {"base": base_scr[...]}
    merged = {}

    def project(j):
        rs = slice(j * sub, (j + 1) * sub)
        x = x_ref[rs, :]
        hb = _rms(x, gmix_ref[...]).astype(BF16)
        z = jnp.zeros((sub, d), F32)
        for i, y_ref in enumerate((yc_ref, yr_ref, ym_ref)):
            cs = slice(i * d, (i + 1) * d)
            gate = _sigmoid(jnp.dot(hb, wgate_ref[:, cs], preferred_element_type=F32) + bgate_ref[:, cs])
            z = z + gate * jnp.dot(y_ref[rs, :], wbr_ref[i], preferred_element_type=F32)
            yield
        x1 = x + _bdot(z, wo_ref[...])
        x1_ref[rs, :] = x1
        merged[j] = x1

    def route(j):
        rs = slice(j * sub, (j + 1) * sub)
        h2 = _rms(merged.pop(j), gffn_ref[...])
        h_hi, h_lo = _split_terms(h2, 2)
        hi_terms = jnp.dot(h_hi, w_hi_lo, preferred_element_type=F32)
        logits = (hi_terms[:, :LANES]
                  + (jnp.dot(h_lo, w_hi, preferred_element_type=F32) + hi_terms[:, LANES:])) + brt_ref[...]
        yield
        gmask = (lane >= N_EXPERTS) & (lane < N_EXPERTS + N_GROUPS)
        glv = jnp.where(gmask, logits, neg)
        gmax = jnp.max(glv, axis=-1, keepdims=True)
        g_sel = jnp.min(jnp.where(glv == gmax, lane - N_EXPERTS, big), axis=-1, keepdims=True)
        g_w = 1.0 / jnp.sum(jnp.exp(glv - gmax), axis=-1, keepdims=True)
        emask = (lane < N_EXPERTS) & ((lane >> 3) == g_sel)
        elv = jnp.where(emask, logits, neg)
        emax = jnp.max(elv, axis=-1, keepdims=True)
        esum = jnp.sum(jnp.exp(elv - emax), axis=-1, keepdims=True)
        i1 = jnp.min(jnp.where(elv == emax, lane, big), axis=-1, keepdims=True)
        elv2 = jnp.where(lane == i1, neg, elv)
        m2 = jnp.max(elv2, axis=-1, keepdims=True)
        i2 = jnp.min(jnp.where(elv2 == m2, lane, big), axis=-1, keepdims=True)
        p1 = 1.0 / esum
        p2 = jnp.exp(m2 - emax) / esum
        c1 = g_w * p1 / (p1 + p2)
        c2 = g_w * p2 / (p1 + p2)

        oh1 = lane == i1
        oh2 = lane == i2
        onehot = jnp.where(oh1 | oh2, 1.0, 0.0)
        before = jnp.dot(tril_ref[...], onehot.astype(BF16), preferred_element_type=F32) + state["base"]
        yield
        rank1 = jnp.sum(jnp.where(oh1, before, 0.0), axis=-1, keepdims=True)
        rank2 = jnp.sum(jnp.where(oh2, before, 0.0), axis=-1, keepdims=True)
        state["base"] = state["base"] + jnp.sum(onehot, axis=0, keepdims=True)

        meta = jnp.where(lane == 0, i1.astype(F32),
               jnp.where(lane == 1, i2.astype(F32),
               jnp.where(lane == 2, rank1,
               jnp.where(lane == 3, rank2,
               jnp.where(lane == 4, c1,
               jnp.where(lane == 5, c2, 0.0))))))
        meta_ref[rs, :] = meta[:, :META_COLS]
        metat_ref[:, rs] = meta.T[:META_COLS, :]

    _software_pipeline(project, route, tm // sub)
    base_scr[...] = state["base"]
    cnt_ref[...] = jnp.broadcast_to(state["base"], cnt_ref.shape)


def _merge(x2, yc, yr, ym, p):
    t, d = x2.shape
    tm = TM_MERGE
    db = D_BRANCH
    tok = lambda c: pl.BlockSpec((tm, c), lambda i: (i, 0))
    consts = [p["g_mix"], p["w_gate"], p["b_gate"], p["w_branch"], p["w_o"], p["g_ffn"],
              p["w_router"], p["b_router"], p["tril_strict"]]
    return pl.pallas_call(
        _merge_kernel,
        grid=(t // tm,),
        in_specs=[tok(d), tok(db), tok(db), tok(db)] + [_const_spec(c.shape) for c in consts],
        out_specs=[tok(d), tok(META_COLS), pl.BlockSpec((META_COLS, tm), lambda i: (0, i)),
                   _const_spec((8, LANES))],
        out_shape=[jax.ShapeDtypeStruct((t, d), F32), jax.ShapeDtypeStruct((t, META_COLS), F32),
                   jax.ShapeDtypeStruct((META_COLS, t), F32), jax.ShapeDtypeStruct((8, LANES), F32)],
        scratch_shapes=[pltpu.VMEM((1, LANES), F32)],
        compiler_params=pltpu.CompilerParams(dimension_semantics=("arbitrary",),
                                             vmem_limit_bytes=VMEM_LIMIT),
        name="merge",
    )(x2, yc, yr, ym, *consts)


def _store_packed_rows(ref2d, x, stage):
    rows, d = x.shape
    nt = d // (2 * LANES)
    for c in range(nt):
        stage[c, pl.ds(0, rows, stride=2), :] = x[:, c * LANES:(c + 1) * LANES]
        stage[c, pl.ds(1, rows, stride=2), :] = x[:, (c + nt) * LANES:(c + nt + 1) * LANES]
        ref2d[pl.ds(c, rows, stride=nt), :] = pltpu.bitcast(stage[c].astype(BF16), jnp.uint32)


def _load_packed_rows(ref2d, rows, nt, stage):
    lo, hi = [], []
    for c in range(nt):
        stage[c] = pltpu.bitcast(ref2d[pl.ds(c, rows, stride=nt), :], BF16).astype(F32)
        lo.append(stage[c, pl.ds(0, rows, stride=2), :])
        hi.append(stage[c, pl.ds(1, rows, stride=2), :])
    return jnp.concatenate(lo + hi, axis=1)


def _scatter_kernel(dest0_ref, dest1_ref, x1_ref, gffn_ref, xs_ref, hbuf, stage, sem):
    dest_refs = (dest0_ref, dest1_ref)
    ts, d_model = x1_ref.shape
    nt = d_model // (2 * LANES)
    s = pl.program_id(0)
    slot = s % 2

    def wait_slot(sl):
        for _ in range(TOP_K):
            pltpu.make_async_copy(hbuf.at[sl], xs_ref.at[pl.ds(0, ts * nt), :], sem.at[sl]).wait()

    @pl.when(s >= 2)
    def _():
        wait_slot(slot)

    _store_packed_rows(hbuf.at[slot], _rms(x1_ref[...], gffn_ref[...]), stage)

    def issue(grp, carry):
        grp_off = pl.multiple_of(grp * (SUBLANES * nt), SUBLANES * nt)
        for j in range(SUBLANES):
            tok = s * ts + grp * SUBLANES + j
            for kslot in range(TOP_K):
                d = pl.multiple_of(dest_refs[kslot][tok], nt)
                pltpu.make_async_copy(hbuf.at[slot, pl.ds(grp_off + j * nt, nt), :],
                                      xs_ref.at[pl.ds(d, nt), :], sem.at[slot]).start(priority=kslot)
        return carry

    lax.fori_loop(0, ts // SUBLANES, issue, 0, unroll=ISSUE_UNROLL)

    @pl.when(s == pl.num_programs(0) - 1)
    def _():
        @pl.when(s >= 1)
        def _():
            wait_slot(1 - slot)
        wait_slot(slot)


def _scatter(dests, x1, g_ffn, n_rows):
    t, d = x1.shape
    ts = TS_SCATTER
    pt = d // (2 * LANES)
    return pl.pallas_call(
        _scatter_kernel,
        grid_spec=pltpu.PrefetchScalarGridSpec(
            num_scalar_prefetch=TOP_K,
            grid=(t // ts,),
            in_specs=[pl.BlockSpec((ts, d), lambda i, *_: (i, 0)),
                      pl.BlockSpec((1, d), lambda i, *_: (0, 0))],
            out_specs=pl.BlockSpec(memory_space=pl.ANY),
            scratch_shapes=[pltpu.VMEM((2, ts * pt, LANES), jnp.uint32),
                            pltpu.VMEM((pt, 2 * ts, LANES), F32),
                            pltpu.SemaphoreType.DMA((2,))],
        ),
        out_shape=jax.ShapeDtypeStruct((n_rows * pt, LANES), jnp.uint32),
        compiler_params=pltpu.CompilerParams(dimension_semantics=("arbitrary",),
                                             vmem_limit_bytes=VMEM_LIMIT),
        name="scatter",
    )(*dests, x1, g_ffn)


def _experts_kernel(be_ref, nused_ref, nexte_ref, xs_ref, wg_hbm, wu_hbm, wd_hbm, ys_ref,
                    wg_f, wu_f, wd_f, wg_s, wu_s, wd_s, xbuf, ybuf, xstage, ystage, sem, xsem, ysem):
    i = pl.program_id(0)
    e = be_ref[i]
    prev = be_ref[jnp.maximum(i - 1, 0)]
    active = i < nused_ref[0]

    def weight_copies(ex):
        return (pltpu.make_async_copy(wg_hbm.at[ex], wg_f, sem.at[0]),
                pltpu.make_async_copy(wu_hbm.at[ex], wu_f, sem.at[1]),
                pltpu.make_async_copy(wd_hbm.at[ex], wd_f, sem.at[2]))

    @pl.when(i == 0)
    def _():
        for cp in weight_copies(e):
            cp.start(priority=1)

    @pl.when(active & ((i == 0) | (e != prev)))
    def _():
        for cp in weight_copies(e):
            cp.wait()
        wg_s[...] = wg_f[...].astype(BF16)
        wu_s[...] = wu_f[...].astype(BF16)
        wd_s[...] = wd_f[...].astype(BF16)
        nxt = nexte_ref[e]

        @pl.when(nxt < N_EXPERTS)
        def _():
            for cp in weight_copies(nxt):
                cp.start(priority=1)

    blk_rows = xbuf.shape[1]
    n_used = nused_ref[0]

    def xs_copy(blk, slot):
        return pltpu.make_async_copy(xs_ref.at[pl.ds(pl.multiple_of(blk * blk_rows, blk_rows), blk_rows), :],
                                     xbuf.at[slot], xsem.at[slot])

    def ys_copy(blk, slot):
        return pltpu.make_async_copy(ybuf.at[slot],
                                     ys_ref.at[pl.ds(pl.multiple_of(blk * blk_rows, blk_rows), blk_rows), :],
                                     ysem.at[slot])

    @pl.when(i == 0)
    def _():
        for ahead in range(XS_BUFFERS - 1):
            @pl.when(ahead < n_used)
            def _():
                xs_copy(ahead, ahead).start()

    @pl.when(active)
    def _():
        nt = wg_s.shape[0] // (2 * LANES)
        fetch = i + (XS_BUFFERS - 1)

        @pl.when(fetch < n_used)
        def _():
            xs_copy(fetch, fetch % XS_BUFFERS).start()

        xs_copy(i, i % XS_BUFFERS).wait()
        oslot = i % 2

        @pl.when(i >= 2)
        def _():
            ys_copy(i - 2, oslot).wait()

        xb = _load_packed_rows(xbuf.at[i % XS_BUFFERS], ROW_BLOCK, nt, xstage).astype(BF16)
        gate = jnp.dot(xb, wg_s[...], preferred_element_type=F32)
        up = jnp.dot(xb, wu_s[...], preferred_element_type=F32)
        hid = gate * _sigmoid(gate) * up
        _store_packed_rows(ybuf.at[oslot], jnp.dot(hid.astype(BF16), wd_s[...], preferred_element_type=F32),
                           ystage)
        ys_copy(i, oslot).start()

        @pl.when(i == n_used - 1)
        def _():
            @pl.when(i >= 1)
            def _():
                ys_copy(i - 1, 1 - oslot).wait()
            ys_copy(i, oslot).wait()


def _experts(blk_expert, n_used, next_expert, xs, w_gate, w_up, w_down):
    d, de = w_gate.shape[-2:]
    pt = d // (2 * LANES)
    blk_rows = ROW_BLOCK * pt
    nb = xs.shape[0] // blk_rows

    return pl.pallas_call(
        _experts_kernel,
        grid_spec=pltpu.PrefetchScalarGridSpec(
            num_scalar_prefetch=3,
            grid=(nb,),
            in_specs=[pl.BlockSpec(memory_space=pl.ANY)] * 4,
            out_specs=pl.BlockSpec(memory_space=pl.ANY),
            scratch_shapes=[pltpu.VMEM((d, de), F32), pltpu.VMEM((d, de), F32), pltpu.VMEM((de, d), F32),
                            pltpu.VMEM((d, de), BF16), pltpu.VMEM((d, de), BF16), pltpu.VMEM((de, d), BF16),
                            pltpu.VMEM((XS_BUFFERS, blk_rows, LANES), jnp.uint32),
                            pltpu.VMEM((2, blk_rows, LANES), jnp.uint32),
                            pltpu.VMEM((pt, 2 * ROW_BLOCK, LANES), F32),
                            pltpu.VMEM((pt, 2 * ROW_BLOCK, LANES), F32),
                            pltpu.SemaphoreType.DMA((3,)), pltpu.SemaphoreType.DMA((XS_BUFFERS,)),
                            pltpu.SemaphoreType.DMA((2,))],
        ),
        out_shape=jax.ShapeDtypeStruct(xs.shape, jnp.uint32),
        compiler_params=pltpu.CompilerParams(dimension_semantics=("arbitrary",),
                                             vmem_limit_bytes=VMEM_LIMIT),
        name="experts",
    )(blk_expert, n_used, next_expert, xs, w_gate, w_up, w_down)


def _combine_kernel(dest0_ref, dest1_ref, x1_ref, meta_ref, gfin_ref, ys_ref, out_ref, ybuf, stage, sem):
    dest_refs = (dest0_ref, dest1_ref)
    te = x1_ref.shape[0]
    s = pl.program_id(0)
    nsteps = pl.num_programs(0)
    slot = s % 2

    nt = x1_ref.shape[1] // (2 * LANES)

    def issue_step(step, sl):
        def issue(grp, carry):
            grp_off = pl.multiple_of(grp * (SUBLANES * nt), SUBLANES * nt)
            for j in range(SUBLANES):
                tok = step * te + grp * SUBLANES + j
                for kslot in range(TOP_K):
                    d = pl.multiple_of(dest_refs[kslot][tok], nt)
                    pltpu.make_async_copy(ys_ref.at[pl.ds(d, nt), :],
                                          ybuf.at[sl, kslot, pl.ds(grp_off + j * nt, nt), :],
                                          sem.at[sl]).start(priority=kslot)
            return carry
        lax.fori_loop(0, te // SUBLANES, issue, 0, unroll=ISSUE_UNROLL)

    @pl.when(s == 0)
    def _():
        issue_step(0, 0)

    @pl.when(s + 1 < nsteps)
    def _():
        issue_step(s + 1, 1 - slot)

    for kslot in range(TOP_K):
        pltpu.make_async_copy(ys_ref.at[pl.ds(0, te * nt), :], ybuf.at[slot, kslot], sem.at[slot]).wait()

    meta = meta_ref[...]
    y0 = _load_packed_rows(ybuf.at[slot, 0], te, nt, stage.at[0])
    y1 = _load_packed_rows(ybuf.at[slot, 1], te, nt, stage.at[1])
    x2 = x1_ref[...] + y0 * meta[:, 4:5] + y1 * meta[:, 5:6]
    out_ref[...] = _rms(x2, gfin_ref[...])


def _combine(dests, x1, meta, g_final, ys):
    t, d = x1.shape
    te = TE_COMBINE
    pt = d // (2 * LANES)
    return pl.pallas_call(
        _combine_kernel,
        grid_spec=pltpu.PrefetchScalarGridSpec(
            num_scalar_prefetch=TOP_K,
            grid=(t // te,),
            in_specs=[pl.BlockSpec((te, d), lambda i, *_: (i, 0)),
                      pl.BlockSpec((te, META_COLS), lambda i, *_: (i, 0)),
                      pl.BlockSpec((1, d), lambda i, *_: (0, 0)),
                      pl.BlockSpec(memory_space=pl.ANY)],
            out_specs=pl.BlockSpec((te, d), lambda i, *_: (i, 0)),
            scratch_shapes=[pltpu.VMEM((2, TOP_K, te * pt, LANES), jnp.uint32),
                            pltpu.VMEM((TOP_K, pt, 2 * te, LANES), F32),
                            pltpu.SemaphoreType.DMA((2,))],
        ),
        out_shape=jax.ShapeDtypeStruct((t, d), F32),
        compiler_params=pltpu.CompilerParams(dimension_semantics=("arbitrary",),
                                             vmem_limit_bytes=VMEM_LIMIT),
        name="combine",
    )(*dests, x1, meta, g_final, ys)


def _constants(tm_merge):
    n = CHUNK
    tri = (jnp.arange(n)[:, None] >= jnp.arange(n)[None, :]).astype(BF16)
    head = jnp.arange(2 * LANES) // HEAD_DIM
    seg_ones = (head[:, None] == head[None, :]).astype(BF16)
    tril_strict = (jnp.arange(tm_merge)[:, None] > jnp.arange(tm_merge)[None, :]).astype(BF16)
    return tri, seg_ones, tril_strict


def kernel(x, mem, g_mix, g_mem, w_in, conv_w, mu_rkv, mu_wag, w_lora1, w_lora2, w0, a_lora1, a_lora2, a0, g_lora1, g_lora2, k_k, k_a, r_k, ln_x_w, ln_x_b, w_kv_mem, w_branch, w_gate, b_gate, w_o, g_ffn, w_router_group, b_router_group, w_router_expert, b_router_expert, w_exp_gate, w_exp_up, w_exp_down, g_final):
    assert g_mix.shape[0] == 1, "single-layer block"
    b, s, d = x.shape
    t = b * s
    db = D_BRANCH
    tri, seg_ones, tril_strict = _constants(MERGE_SUB)
    row = lambda a: a.reshape(1, -1)
    pad_r = LANES - N_EXPERTS - N_GROUPS
    p = {
        "g_mix": row(g_mix[0]), "w_in": w_in[0].astype(BF16), "conv_w": conv_w[0].T,
        "mu_rkv": row(mu_rkv[0]), "mu_wag": mu_wag[0],
        "w_lora1": w_lora1[0].astype(BF16), "w_lora2": w_lora2[0].astype(BF16), "w0": row(w0[0]),
        "a_lora1": a_lora1[0].astype(BF16), "a_lora2": a_lora2[0].astype(BF16), "a0": row(a0[0]),
        "g_lora1": g_lora1[0].astype(BF16), "g_lora2": g_lora2[0].astype(BF16),
        "k_k": row(k_k[0]), "k_a": row(k_a[0]), "r_k": row(r_k[0]),
        "ln_x_w": row(ln_x_w[0]), "ln_x_b": row(ln_x_b[0]),
        "w_gate": w_gate[0].astype(BF16), "b_gate": row(b_gate[0]),
        "w_branch": w_branch[0].astype(BF16), "w_o": w_o[0].astype(BF16), "g_ffn": row(g_ffn[0]),
        "w_router": jnp.concatenate([w_router_expert[0], w_router_group[0],
                                     jnp.zeros((d, pad_r), F32)], axis=1),
        "b_router": row(jnp.concatenate([b_router_expert[0], b_router_group[0],
                                         jnp.zeros((pad_r,), F32)])),
        "tri": tri, "seg_ones": seg_ones, "tril_strict": tril_strict,
    }

    km, vm = _memkv(mem, row(g_mem[0]), w_kv_mem[0].astype(BF16))
    x2 = x.reshape(t, d)
    yconv, ymem, r, k, v, kkn, a, lw, g = _prologue(x2, b, km, vm, p)
    yrwkv = _rwkv(r, k, v, kkn, a, lw, g, b, p)
    x1, meta, meta_t, cnt = _merge(x2, yconv, yrwkv, ymem, p)

    counts = cnt[0, :N_EXPERTS].astype(jnp.int32)
    padded = ((counts + ROW_BLOCK - 1) // ROW_BLOCK) * ROW_BLOCK
    pad_end = jnp.cumsum(padded)
    pad_start = pad_end - padded
    n_blocks = (t * TOP_K) // ROW_BLOCK + N_EXPERTS
    eids = jnp.arange(N_EXPERTS, dtype=jnp.int32)
    e_idx = meta_t[0:TOP_K].astype(jnp.int32)
    rank = meta_t[TOP_K:2 * TOP_K].astype(jnp.int32)
    start_of = jnp.sum(jnp.where(e_idx[:, None, :] == eids[None, :, None], pad_start[None, :, None], 0), axis=1)
    dest = (start_of + rank) * (d // (2 * LANES))
    dests = [dest[kslot] for kslot in range(TOP_K)]
    blk_start = jnp.arange(n_blocks, dtype=jnp.int32) * ROW_BLOCK
    blk_expert = jnp.minimum(jnp.sum((pad_end[None, :] <= blk_start[:, None]).astype(jnp.int32), axis=1),
                             N_EXPERTS - 1)
    n_used = (pad_end[-1:] // ROW_BLOCK).astype(jnp.int32)
    later_nonempty = (eids[None, :] > eids[:, None]) & (counts[None, :] > 0)
    next_expert = jnp.min(jnp.where(later_nonempty, eids[None, :], N_EXPERTS), axis=1)

    xs = _scatter(dests, x1, p["g_ffn"], n_blocks * ROW_BLOCK)
    ys = _experts(blk_expert, n_used, next_expert, xs, w_exp_gate[0], w_exp_up[0], w_exp_down[0])
    out = _combine(dests, x1, meta, row(g_final), ys)
    return out.reshape(b, s, d)
```

```python
import functools

import jax
import jax.numpy as jnp
from jax import lax
from jax.experimental import pallas as pl
from jax.experimental.pallas import tpu as pltpu

F32 = jnp.float32
BF16 = jnp.bfloat16

NORM_EPS = 1e-6
GN_EPS = 64e-5
D_BRANCH = 512
HEAD_DIM = 64
N_HEADS = 8
CHUNK = 64
CHUNKS_PER_ITER = 4
MEM_HEADS = 4
MEM_HEAD_DIM = 128
N_GROUPS = 8
EXPERTS_PER_GROUP = 8
N_EXPERTS = 64
TOP_K = 2
ROW_BLOCK = 256
XS_BUFFERS = 4
LANES = 128
VMEM_LIMIT = 56 * 1024 * 1024

TM_PROLOGUE = 512
PROLOGUE_SUB = 256
TB_RWKV = 256
TM_MERGE = 512
MERGE_SUB = 512
TS_SCATTER = 256
TE_COMBINE = 256
SUBLANES = 8
META_COLS = 8
ISSUE_UNROLL = 2


def _bdot(a, b):
    return jnp.dot(a.astype(BF16), b.astype(BF16), preferred_element_type=F32)


def _bdot_nt(a, b):
    return lax.dot_general(a.astype(BF16), b.astype(BF16), (((1,), (1,)), ((), ())),
                           preferred_element_type=F32)


def _split_terms(x, n_terms):
    terms = []
    for _ in range(n_terms):
        t = x.astype(BF16)
        terms.append(t)
        x = x - t.astype(F32)
    return terms


def _split_dot_left(m_bf16, x, n_terms):
    return sum(jnp.dot(m_bf16, t, preferred_element_type=F32) for t in _split_terms(x, n_terms))


def _head_sums(x, seg_bf16, n_terms):
    w = seg_bf16.shape[0]
    terms = _split_terms(x, n_terms)
    halves = [sum(jnp.dot(t[:, c:c + w], seg_bf16, preferred_element_type=F32) for t in terms)
              for c in range(0, x.shape[1], w)]
    return jnp.concatenate(halves, axis=1)


def _rms(x, g):
    return x * lax.rsqrt(jnp.mean(x * x, axis=-1, keepdims=True) + NORM_EPS) * g


def _sigmoid(x):
    return 1.0 / (1.0 + jnp.exp(-x))


def _run_together(*gens):
    live = list(gens)
    while live:
        for gen in list(live):
            try:
                next(gen)
            except StopIteration:
                live.remove(gen)


def _software_pipeline(heavy, light, n_sub):
    _run_together(heavy(0))
    for j in range(1, n_sub):
        _run_together(heavy(j), light(j - 1))
    _run_together(light(n_sub - 1))


def _const_spec(shape):
    n = len(shape)
    return pl.BlockSpec(shape, lambda *_: (0,) * n)


def _memkv_kernel(mem_ref, g_ref, w_ref, k_ref, v_ref):
    mn = _rms(mem_ref[0], g_ref[...])
    kv = _bdot(mn, w_ref[...])
    k_ref[0] = kv[:, :D_BRANCH].astype(BF16)
    v_ref[0] = kv[:, D_BRANCH:].astype(BF16)


def _memkv(mem, g_mem, w_kv):
    b, m, d = mem.shape
    return pl.pallas_call(
        _memkv_kernel,
        grid=(b,),
        in_specs=[pl.BlockSpec((1, m, d), lambda i: (i, 0, 0)),
                  _const_spec((1, d)), _const_spec((d, 2 * D_BRANCH))],
        out_specs=[pl.BlockSpec((1, m, D_BRANCH), lambda i: (i, 0, 0)),
                   pl.BlockSpec((1, m, D_BRANCH), lambda i: (i, 0, 0))],
        out_shape=[jax.ShapeDtypeStruct((b, m, D_BRANCH), BF16)] * 2,
        compiler_params=pltpu.CompilerParams(dimension_semantics=("arbitrary",),
                                             vmem_limit_bytes=VMEM_LIMIT),
        name="memkv",
    )(mem, g_mem, w_kv)


def _prologue_kernel(x_ref, gmix_ref, win_ref, convw_ref, murkv_ref, muwag_ref,
                     wl1_ref, wl2_ref, w0_ref, al1_ref, al2_ref, a0_ref, gl1_ref, gl2_ref,
                     kk_ref, ka_ref, seg_ref, km_ref, vm_ref,
                     yconv_ref, ymem_ref, r_ref, k_ref, v_ref, kkn_ref, a_ref, lw_ref, g_ref,
                     prev_h, prev_p, prev_cu):
    tm = x_ref.shape[0]
    db = D_BRANCH
    sub = PROLOGUE_SUB
    n_parts = win_ref.shape[1] // db

    @pl.when(pl.program_id(1) == 0)
    def _():
        prev_h[...] = jnp.zeros_like(prev_h)
        prev_p[...] = jnp.zeros_like(prev_p)
        prev_cu[...] = jnp.zeros_like(prev_cu)

    rows = lax.broadcasted_iota(jnp.int32, (sub, 1), 0)

    def shift1(u, prev_row):
        return jnp.where(rows == 0, prev_row, pltpu.roll(u, 1, axis=0))

    carry = {"h": prev_h[...], "p": prev_p[...], "cu": prev_cu[...]}
    projected = {}

    def project(j):
        h = _rms(x_ref[j * sub:(j + 1) * sub, :], gmix_ref[...])
        hb = h.astype(BF16)
        parts = []
        for c in range(n_parts):
            parts.append(jnp.dot(hb, win_ref[:, c * db:(c + 1) * db], preferred_element_type=F32))
            yield
        projected[j] = (h, parts)

    def mix(j):
        rs = slice(j * sub, (j + 1) * sub)
        h, (bg, cg, u, rp, kp, vp, q) = projected.pop(j)

        cu = cg * u
        pcu = carry["cu"]
        cu1 = shift1(cu, pcu[1:2, :])
        cu2 = jnp.where(rows == 0, pcu[0:1, :], jnp.where(rows == 1, pcu[1:2, :], pltpu.roll(cu, 2, axis=0)))
        conv = cu2 * convw_ref[0:1, :] + cu1 * convw_ref[1:2, :] + cu * convw_ref[2:3, :]
        yconv_ref[rs, :] = (bg * conv).astype(BF16)
        carry["cu"] = cu[sub - 2:sub, :]

        pr = jnp.concatenate([rp, kp, vp], axis=1)
        prs = shift1(pr, carry["p"])
        mixed = pr + (prs - pr) * murkv_ref[...]
        carry["p"] = pr[sub - 1:sub, :]
        r, k, v = mixed[:, :db], mixed[:, db:2 * db], mixed[:, 2 * db:]
        r_ref[rs, :] = r
        v_ref[rs, :] = v

        dh = shift1(h, carry["h"]) - h
        carry["h"] = h[sub - 1:sub, :]
        lora_w = _bdot(h + dh * muwag_ref[0:1, :], wl1_ref[...])
        lora_a = _bdot(h + dh * muwag_ref[1:2, :], al1_ref[...])
        lora_g = _bdot(h + dh * muwag_ref[2:3, :], gl1_ref[...])
        yield
        zz = w0_ref[...] + _bdot(jnp.tanh(lora_w), wl2_ref[...])
        a_lin = a0_ref[...] + _bdot(lora_a, al2_ref[...])
        g_ref[rs, :] = _bdot(_sigmoid(lora_g), gl2_ref[...])
        yield
        softplus = jnp.maximum(-zz, 0.0) + jnp.log(1.0 + jnp.exp(-jnp.abs(zz)))
        lw_ref[rs, :] = -jnp.exp(-softplus - 0.5)
        a = _sigmoid(a_lin)
        a_ref[rs, :] = a
        k_ref[rs, :] = k * (1.0 + (a - 1.0) * ka_ref[...])
        kk = k * kk_ref[...]
        ss = _head_sums(kk * kk, seg_ref[...], 1)
        yield
        kkn_ref[rs, :] = kk * lax.rsqrt(jnp.maximum(ss, 1e-24))

        scale = MEM_HEAD_DIM ** -0.5
        heads = [slice(hh * MEM_HEAD_DIM, (hh + 1) * MEM_HEAD_DIM) for hh in range(MEM_HEADS)]
        scores = [_bdot_nt(q[:, sl], km_ref[0, :, sl]) * scale for sl in heads]
        yield
        for sl, s in zip(heads, scores):
            p = jnp.exp(s - jnp.max(s, axis=-1, keepdims=True))
            o = _bdot(p, vm_ref[0, :, sl]) / jnp.sum(p, axis=-1, keepdims=True)
            ymem_ref[rs, sl] = o.astype(BF16)

    _software_pipeline(project, mix, tm // sub)
    prev_h[...] = carry["h"]
    prev_p[...] = carry["p"]
    prev_cu[...] = carry["cu"]


def _prologue(x2, b, km, vm, p):
    t, d = x2.shape
    s = t // b
    tm = TM_PROLOGUE
    db = D_BRANCH
    m = km.shape[1]
    steps = s // tm
    tok = lambda c: pl.BlockSpec((tm, c), lambda bi, i: (bi * steps + i, 0))
    consts = [p["g_mix"], p["w_in"], p["conv_w"], p["mu_rkv"], p["mu_wag"],
              p["w_lora1"], p["w_lora2"], p["w0"], p["a_lora1"], p["a_lora2"], p["a0"],
              p["g_lora1"], p["g_lora2"], p["k_k"], p["k_a"], p["seg_ones"]]
    out_shapes = ([jax.ShapeDtypeStruct((t, db), BF16)] * 2
                  + [jax.ShapeDtypeStruct((t, db), F32)] * 7)
    return pl.pallas_call(
        _prologue_kernel,
        grid=(b, steps),
        in_specs=[tok(d)] + [_const_spec(c.shape) for c in consts]
                 + [pl.BlockSpec((1, m, db), lambda bi, i: (bi, 0, 0))] * 2,
        out_specs=[tok(db)] * 9,
        out_shape=out_shapes,
        scratch_shapes=[pltpu.VMEM((1, d), F32), pltpu.VMEM((1, 3 * db), F32),
                        pltpu.VMEM((2, db), F32)],
        compiler_params=pltpu.CompilerParams(dimension_semantics=("arbitrary", "arbitrary"),
                                             vmem_limit_bytes=VMEM_LIMIT),
        name="prologue",
    )(x2, *consts, km, vm)


def _rwkv_kernel(r_ref, k_ref, v_ref, kk_ref, a_ref, lw_ref, g_ref, rk_ref, lnw_ref, lnb_ref,
                 tri_ref, seg_ref, out_ref, h_scr, y_scr):
    tb = r_ref.shape[0]
    n = HEAD_DIM
    c_len = CHUNK

    @pl.when(pl.program_id(1) == 0)
    def _():
        h_scr[...] = jnp.zeros_like(h_scr)

    row2 = lax.broadcasted_iota(jnp.int32, (c_len, 2 * c_len), 0)
    col2 = lax.broadcasted_iota(jnp.int32, (c_len, 2 * c_len), 1) & (c_len - 1)
    strict2 = col2 < row2
    incl2 = col2 <= row2
    eye = (lax.broadcasted_iota(jnp.int32, (c_len, n), 0)
           == lax.broadcasted_iota(jnp.int32, (c_len, n), 1)).astype(F32)
    zeros = jnp.zeros((c_len, n), F32)

    def chunk_inputs(c):
        rows = pl.ds(pl.multiple_of(c * c_len, c_len), c_len)
        r = r_ref[rows, :]
        k = k_ref[rows, :]
        v = v_ref[rows, :]
        kk = kk_ref[rows, :]
        a = a_ref[rows, :]
        lw = lw_ref[rows, :]
        gcum = _split_dot_left(tri_ref[...], lw, 2)
        e_pos = jnp.exp(gcum)
        e_neg = jnp.exp(-gcum)
        p_last = jnp.exp(gcum[c_len - 1:c_len, :])
        bb = kk * a * e_neg
        kb = k * e_neg
        return dict(rows=rows, v=v, p_last=p_last, rb=r * e_pos, ab=-kk * jnp.exp(gcum - lw), bb=bb, kb=kb,
                    bk_t=jnp.concatenate([bb * p_last, kb * p_last], axis=0).T)

    def chunk_group(it, carry):
        chunks = [chunk_inputs(it * CHUNKS_PER_ITER + ci) for ci in range(CHUNKS_PER_ITER)]
        units = [(ci, hd) for ci in range(CHUNKS_PER_ITER) for hd in range(N_HEADS)]
        nu = range(len(units))
        ls = [slice(hd * n, (hd + 1) * n) for _, hd in units]
        ch = [chunks[ci] for ci, _ in units]
        al = [ch[u]["ab"][:, ls[u]] for u in nu]
        rr = [ch[u]["rb"][:, ls[u]] for u in nu]
        vv = [ch[u]["v"][:, ls[u]] for u in nu]
        aa = [_bdot_nt(jnp.concatenate([al[u], rr[u]], axis=0),
                       jnp.concatenate([ch[u]["bb"][:, ls[u]], ch[u]["kb"][:, ls[u]]], axis=0)) for u in nu]
        top = [jnp.where(strict2, aa[u][:c_len], 0.0) for u in nu]
        bot = [jnp.where(incl2, aa[u][c_len:], 0.0) for u in nu]
        a_ab = [top[u][:, :c_len] for u in nu]
        t_inv = [eye + a_ab[u] for u in nu]
        x_pow = [_bdot(a_ab[u], a_ab[u]) for u in nu]
        av = [_bdot(top[u], jnp.concatenate([zeros, vv[u]], axis=0)) for u in nu]
        for lvl in range(5):
            if lvl < 4:
                z = [_bdot(jnp.concatenate([t_inv[u], x_pow[u]], axis=0), x_pow[u]) for u in nu]
                t_inv = [t_inv[u] + z[u][:c_len] for u in nu]
                x_pow = [z[u][c_len:] for u in nu]
            else:
                t_inv = [t_inv[u] + _bdot(t_inv[u], x_pow[u]) for u in nu]
        w12 = [_bdot(t_inv[u], jnp.concatenate([al[u], av[u]], axis=1)) for u in nu]
        z2 = []
        for u in nu:
            hd = units[u][1]
            rhs2 = jnp.concatenate([w12[u], jnp.concatenate([zeros, vv[u]], axis=1)], axis=0)
            lhs3 = jnp.concatenate([ch[u]["bk_t"][hd * n:(hd + 1) * n, :], bot[u]], axis=0)
            z2.append(_bdot(lhs3, rhs2))
        state = [h_scr[hd] for hd in range(N_HEADS)]
        for u in nu:
            hd = units[u][1]
            mq = z2[u][:, :n] + jnp.concatenate([zeros, rr[u]], axis=0)
            out = _bdot(mq, state[hd]) + z2[u][:, n:]
            p_col = jnp.sum(eye * ch[u]["p_last"][:, ls[u]], axis=1, keepdims=True)
            state[hd] = p_col * state[hd] + out[:c_len]
            y_scr[ch[u]["rows"], ls[u]] = out[c_len:]
        for hd in range(N_HEADS):
            h_scr[hd] = state[hd]
        return carry

    lax.fori_loop(0, tb // (c_len * CHUNKS_PER_ITER), chunk_group, 0)

    y = y_scr[...]
    seg = seg_ref[...]
    inv_n = 1.0 / n
    mu = _head_sums(y, seg, 2) * inv_n
    yc = y - mu
    var = _head_sums(yc * yc, seg, 1) * inv_n
    yn = yc * lax.rsqrt(var + GN_EPS) * lnw_ref[...] + lnb_ref[...]
    bonus = _head_sums(r_ref[...] * k_ref[...] * rk_ref[...], seg, 1) * v_ref[...]
    out_ref[...] = ((yn + bonus) * g_ref[...]).astype(BF16)


def _rwkv(r, k, v, kkn, a, lw, g, b, p):
    t, db = r.shape
    tb = TB_RWKV
    steps = t // b // tb
    tok = pl.BlockSpec((tb, db), lambda bi, i: (bi * steps + i, 0))
    consts = [p["r_k"], p["ln_x_w"], p["ln_x_b"], p["tri"], p["seg_ones"]]
    return pl.pallas_call(
        _rwkv_kernel,
        grid=(b, steps),
        in_specs=[tok] * 7 + [_const_spec(c.shape) for c in consts],
        out_specs=tok,
        out_shape=jax.ShapeDtypeStruct((t, db), BF16),
        scratch_shapes=[pltpu.VMEM((N_HEADS, HEAD_DIM, HEAD_DIM), F32),
                        pltpu.VMEM((tb, db), F32)],
        compiler_params=pltpu.CompilerParams(dimension_semantics=("arbitrary", "arbitrary"),
                                             vmem_limit_bytes=VMEM_LIMIT),
        name="rwkv",
    )(r, k, v, kkn, a, lw, g, *consts)


def _merge_kernel(x_ref, yc_ref, yr_ref, ym_ref, gmix_ref, wgate_ref, bgate_ref, wbr_ref, wo_ref,
                  gffn_ref, wrt_ref, brt_ref, tril_ref,
                  x1_ref, meta_ref, metat_ref, cnt_ref, base_scr):
    tm, d = x_ref.shape

    @pl.when(pl.program_id(0) == 0)
    def _():
        base_scr[...] = jnp.zeros_like(base_scr)

    sub = tril_ref.shape[0]
    lane = lax.broadcasted_iota(jnp.int32, (sub, LANES), 1)
    neg = jnp.float32(-jnp.inf)
    big = jnp.int32(1 << 20)
    w_hi, w_lo = _split_terms(wrt_ref[...], 2)
    w_hi_lo = jnp.concatenate([w_hi, w_lo], axis=1)
    state = {"base": base_scr[...]}
    merged = {}

    def project(j):
        rs = slice(j * sub, (j + 1) * sub)
        x = x_ref[rs, :]
        hb = _rms(x, gmix_ref[...]).astype(BF16)
        z = jnp.zeros((sub, d), F32)
        for i, y_ref in enumerate((yc_ref, yr_ref, ym_ref)):
            cs = slice(i * d, (i + 1) * d)
            gate = _sigmoid(jnp.dot(hb, wgate_ref[:, cs], preferred_element_type=F32) + bgate_ref[:, cs])
            z = z + gate * jnp.dot(y_ref[rs, :], wbr_ref[i], preferred_element_type=F32)
            yield
        x1 = x + _bdot(z, wo_ref[...])
        x1_ref[rs, :] = x1
        merged[j] = x1

    def route(j):
        rs = slice(j * sub, (j + 1) * sub)
        h2 = _rms(merged.pop(j), gffn_ref[...])
        h_hi, h_lo = _split_terms(h2, 2)
        hi_terms = jnp.dot(h_hi, w_hi_lo, preferred_element_type=F32)
        logits = (hi_terms[:, :LANES]
                  + (jnp.dot(h_lo, w_hi, preferred_element_type=F32) + hi_terms[:, LANES:])) + brt_ref[...]
        yield
        gmask = (lane >= N_EXPERTS) & (lane < N_EXPERTS + N_GROUPS)
        glv = jnp.where(gmask, logits, neg)
        gmax = jnp.max(glv, axis=-1, keepdims=True)
        g_sel = jnp.min(jnp.where(glv == gmax, lane - N_EXPERTS, big), axis=-1, keepdims=True)
        g_w = 1.0 / jnp.sum(jnp.exp(glv - gmax), axis=-1, keepdims=True)
        emask = (lane < N_EXPERTS) & ((lane >> 3) == g_sel)
        elv = jnp.where(emask, logits, neg)
        emax = jnp.max(elv, axis=-1, keepdims=True)
        esum = jnp.sum(jnp.exp(elv - emax), axis=-1, keepdims=True)
        i1 = jnp.min(jnp.where(elv == emax, lane, big), axis=-1, keepdims=True)
        elv2 = jnp.where(lane == i1, neg, elv)
        m2 = jnp.max(elv2, axis=-1, keepdims=True)
        i2 = jnp.min(jnp.where(elv2 == m2, lane, big), axis=-1, keepdims=True)
        p1 = 1.0 / esum
        p2 = jnp.exp(m2 - emax) / esum
        c1 = g_w * p1 / (p1 + p2)
        c2 = g_w * p2 / (p1 + p2)

        oh1 = lane == i1
        oh2 = lane == i2
        onehot = jnp.where(oh1 | oh2, 1.0, 0.0)
        before = jnp.dot(tril_ref[...], onehot.astype(BF16), preferred_element_type=F32) + state["base"]
        yield
        rank1 = jnp.sum(jnp.where(oh1, before, 0.0), axis=-1, keepdims=True)
        rank2 = jnp.sum(jnp.where(oh2, before, 0.0), axis=-1, keepdims=True)
        state["base"] = state["base"] + jnp.sum(onehot, axis=0, keepdims=True)

        meta = jnp.where(lane == 0, i1.astype(F32),
               jnp.where(lane == 1, i2.astype(F32),
               jnp.where(lane == 2, rank1,
               jnp.where(lane == 3, rank2,
               jnp.where(lane == 4, c1,
               jnp.where(lane == 5, c2, 0.0))))))
        meta_ref[rs, :] = meta[:, :META_COLS]
        metat_ref[:, rs] = meta.T[:META_COLS, :]

    _software_pipeline(project, route, tm // sub)
    base_scr[...] = state["base"]
    cnt_ref[...] = jnp.broadcast_to(state["base"], cnt_ref.shape)


def _merge(x2, yc, yr, ym, p):
    t, d = x2.shape
    tm = TM_MERGE
    db = D_BRANCH
    tok = lambda c: pl.BlockSpec((tm, c), lambda i: (i, 0))
    consts = [p["g_mix"], p["w_gate"], p["b_gate"], p["w_branch"], p["w_o"], p["g_ffn"],
              p["w_router"], p["b_router"], p["tril_strict"]]
    return pl.pallas_call(
        _merge_kernel,
        grid=(t // tm,),
        in_specs=[tok(d), tok(db), tok(db), tok(db)] + [_const_spec(c.shape) for c in consts],
        out_specs=[tok(d), tok(META_COLS), pl.BlockSpec((META_COLS, tm), lambda i: (0, i)),
                   _const_spec((8, LANES))],
        out_shape=[jax.ShapeDtypeStruct((t, d), F32), jax.ShapeDtypeStruct((t, META_COLS), F32),
                   jax.ShapeDtypeStruct((META_COLS, t), F32), jax.ShapeDtypeStruct((8, LANES), F32)],
        scratch_shapes=[pltpu.VMEM((1, LANES), F32)],
        compiler_params=pltpu.CompilerParams(dimension_semantics=("arbitrary",),
                                             vmem_limit_bytes=VMEM_LIMIT),
        name="merge",
    )(x2, yc, yr, ym, *consts)


def _store_packed_rows(ref2d, x, stage):
    rows, d = x.shape
    nt = d // (2 * LANES)
    for c in range(nt):
        stage[c, pl.ds(0, rows, stride=2), :] = x[:, c * LANES:(c + 1) * LANES]
        stage[c, pl.ds(1, rows, stride=2), :] = x[:, (c + nt) * LANES:(c + nt + 1) * LANES]
        ref2d[pl.ds(c, rows, stride=nt), :] = pltpu.bitcast(stage[c].astype(BF16), jnp.uint32)


def _load_packed_rows(ref2d, rows, nt, stage):
    lo, hi = [], []
    for c in range(nt):
        stage[c] = pltpu.bitcast(ref2d[pl.ds(c, rows, stride=nt), :], BF16).astype(F32)
        lo.append(stage[c, pl.ds(0, rows, stride=2), :])
        hi.append(stage[c, pl.ds(1, rows, stride=2), :])
    return jnp.concatenate(lo + hi, axis=1)


def _scatter_kernel(dest0_ref, dest1_ref, x1_ref, gffn_ref, xs_ref, hbuf, stage, sem):
    dest_refs = (dest0_ref, dest1_ref)
    ts, d_model = x1_ref.shape
    nt = d_model // (2 * LANES)
    s = pl.program_id(0)
    slot = s % 2

    def wait_slot(sl):
        for _ in range(TOP_K):
            pltpu.make_async_copy(hbuf.at[sl], xs_ref.at[pl.ds(0, ts * nt), :], sem.at[sl]).wait()

    @pl.when(s >= 2)
    def _():
        wait_slot(slot)

    _store_packed_rows(hbuf.at[slot], _rms(x1_ref[...], gffn_ref[...]), stage)

    def issue(grp, carry):
        grp_off = pl.multiple_of(grp * (SUBLANES * nt), SUBLANES * nt)
        for j in range(SUBLANES):
            tok = s * ts + grp * SUBLANES + j
            for kslot in range(TOP_K):
                d = pl.multiple_of(dest_refs[kslot][tok], nt)
                pltpu.make_async_copy(hbuf.at[slot, pl.ds(grp_off + j * nt, nt), :],
                                      xs_ref.at[pl.ds(d, nt), :], sem.at[slot]).start(priority=kslot)
        return carry

    lax.fori_loop(0, ts // SUBLANES, issue, 0, unroll=ISSUE_UNROLL)

    @pl.when(s == pl.num_programs(0) - 1)
    def _():
        @pl.when(s >= 1)
        def _():
            wait_slot(1 - slot)
        wait_slot(slot)


def _scatter(dests, x1, g_ffn, n_rows):
    t, d = x1.shape
    ts = TS_SCATTER
    pt = d // (2 * LANES)
    return pl.pallas_call(
        _scatter_kernel,
        grid_spec=pltpu.PrefetchScalarGridSpec(
            num_scalar_prefetch=TOP_K,
            grid=(t // ts,),
            in_specs=[pl.BlockSpec((ts, d), lambda i, *_: (i, 0)),
                      pl.BlockSpec((1, d), lambda i, *_: (0, 0))],
            out_specs=pl.BlockSpec(memory_space=pl.ANY),
            scratch_shapes=[pltpu.VMEM((2, ts * pt, LANES), jnp.uint32),
                            pltpu.VMEM((pt, 2 * ts, LANES), F32),
                            pltpu.SemaphoreType.DMA((2,))],
        ),
        out_shape=jax.ShapeDtypeStruct((n_rows * pt, LANES), jnp.uint32),
        compiler_params=pltpu.CompilerParams(dimension_semantics=("arbitrary",),
                                             vmem_limit_bytes=VMEM_LIMIT),
        name="scatter",
    )(*dests, x1, g_ffn)


def _experts_kernel(be_ref, nused_ref, nexte_ref, xs_ref, wg_hbm, wu_hbm, wd_hbm, ys_ref,
                    wg_f, wu_f, wd_f, wg_s, wu_s, wd_s, xbuf, ybuf, xstage, ystage, sem, xsem, ysem):
    i = pl.program_id(0)
    e = be_ref[i]
    prev = be_ref[jnp.maximum(i - 1, 0)]
    active = i < nused_ref[0]

    def weight_copies(ex):
        return (pltpu.make_async_copy(wg_hbm.at[ex], wg_f, sem.at[0]),
                pltpu.make_async_copy(wu_hbm.at[ex], wu_f, sem.at[1]),
                pltpu.make_async_copy(wd_hbm.at[ex], wd_f, sem.at[2]))

    @pl.when(i == 0)
    def _():
        for cp in weight_copies(e):
            cp.start(priority=1)

    @pl.when(active & ((i == 0) | (e != prev)))
    def _():
        for cp in weight_copies(e):
            cp.wait()
        wg_s[...] = wg_f[...].astype(BF16)
        wu_s[...] = wu_f[...].astype(BF16)
        wd_s[...] = wd_f[...].astype(BF16)
        nxt = nexte_ref[e]

        @pl.when(nxt < N_EXPERTS)
        def _():
            for cp in weight_copies(nxt):
                cp.start(priority=1)

    blk_rows = xbuf.shape[1]
    n_used = nused_ref[0]

    def xs_copy(blk, slot):
        return pltpu.make_async_copy(xs_ref.at[pl.ds(pl.multiple_of(blk * blk_rows, blk_rows), blk_rows), :],
                                     xbuf.at[slot], xsem.at[slot])

    def ys_copy(blk, slot):
        return pltpu.make_async_copy(ybuf.at[slot],
                                     ys_ref.at[pl.ds(pl.multiple_of(blk * blk_rows, blk_rows), blk_rows), :],
                                     ysem.at[slot])

    @pl.when(i == 0)
    def _():
        for ahead in range(XS_BUFFERS - 1):
            @pl.when(ahead < n_used)
            def _():
                xs_copy(ahead, ahead).start()

    @pl.when(active)
    def _():
        nt = wg_s.shape[0] // (2 * LANES)
        fetch = i + (XS_BUFFERS - 1)

        @pl.when(fetch < n_used)
        def _():
            xs_copy(fetch, fetch % XS_BUFFERS).start()

        xs_copy(i, i % XS_BUFFERS).wait()
        oslot = i % 2

        @pl.when(i >= 2)
        def _():
            ys_copy(i - 2, oslot).wait()

        xb = _load_packed_rows(xbuf.at[i % XS_BUFFERS], ROW_BLOCK, nt, xstage).astype(BF16)
        gate = jnp.dot(xb, wg_s[...], preferred_element_type=F32)
        up = jnp.dot(xb, wu_s[...], preferred_element_type=F32)
        hid = gate * _sigmoid(gate) * up
        _store_packed_rows(ybuf.at[oslot], jnp.dot(hid.astype(BF16), wd_s[...], preferred_element_type=F32),
                           ystage)
        ys_copy(i, oslot).start()

        @pl.when(i == n_used - 1)
        def _():
            @pl.when(i >= 1)
            def _():
                ys_copy(i - 1, 1 - oslot).wait()
            ys_copy(i, oslot).wait()


def _experts(blk_expert, n_used, next_expert, xs, w_gate, w_up, w_down):
    d, de = w_gate.shape[-2:]
    pt = d // (2 * LANES)
    blk_rows = ROW_BLOCK * pt
    nb = xs.shape[0] // blk_rows

    return pl.pallas_call(
        _experts_kernel,
        grid_spec=pltpu.PrefetchScalarGridSpec(
            num_scalar_prefetch=3,
            grid=(nb,),
            in_specs=[pl.BlockSpec(memory_space=pl.ANY)] * 4,
            out_specs=pl.BlockSpec(memory_space=pl.ANY),
            scratch_shapes=[pltpu.VMEM((d, de), F32), pltpu.VMEM((d, de), F32), pltpu.VMEM((de, d), F32),
                            pltpu.VMEM((d, de), BF16), pltpu.VMEM((d, de), BF16), pltpu.VMEM((de, d), BF16),
                            pltpu.VMEM((XS_BUFFERS, blk_rows, LANES), jnp.uint32),
                            pltpu.VMEM((2, blk_rows, LANES), jnp.uint32),
                            pltpu.VMEM((pt, 2 * ROW_BLOCK, LANES), F32),
                            pltpu.VMEM((pt, 2 * ROW_BLOCK, LANES), F32),
                            pltpu.SemaphoreType.DMA((3,)), pltpu.SemaphoreType.DMA((XS_BUFFERS,)),
                            pltpu.SemaphoreType.DMA((2,))],
        ),
        out_shape=jax.ShapeDtypeStruct(xs.shape, jnp.uint32),
        compiler_params=pltpu.CompilerParams(dimension_semantics=("arbitrary",),
                                             vmem_limit_bytes=VMEM_LIMIT),
        name="experts",
    )(blk_expert, n_used, next_expert, xs, w_gate, w_up, w_down)


def _combine_kernel(dest0_ref, dest1_ref, x1_ref, meta_ref, gfin_ref, ys_ref, out_ref, ybuf, stage, sem):
    dest_refs = (dest0_ref, dest1_ref)
    te = x1_ref.shape[0]
    s = pl.program_id(0)
    nsteps = pl.num_programs(0)
    slot = s % 2

    nt = x1_ref.shape[1] // (2 * LANES)

    def issue_step(step, sl):
        def issue(grp, carry):
            grp_off = pl.multiple_of(grp * (SUBLANES * nt), SUBLANES * nt)
            for j in range(SUBLANES):
                tok = step * te + grp * SUBLANES + j
                for kslot in range(TOP_K):
                    d = pl.multiple_of(dest_refs[kslot][tok], nt)
                    pltpu.make_async_copy(ys_ref.at[pl.ds(d, nt), :],
                                          ybuf.at[sl, kslot, pl.ds(grp_off + j * nt, nt), :],
                                          sem.at[sl]).start(priority=kslot)
            return carry
        lax.fori_loop(0, te // SUBLANES, issue, 0, unroll=ISSUE_UNROLL)

    @pl.when(s == 0)
    def _():
        issue_step(0, 0)

    @pl.when(s + 1 < nsteps)
    def _():
        issue_step(s + 1, 1 - slot)

    for kslot in range(TOP_K):
        pltpu.make_async_copy(ys_ref.at[pl.ds(0, te * nt), :], ybuf.at[slot, kslot], sem.at[slot]).wait()

    meta = meta_ref[...]
    y0 = _load_packed_rows(ybuf.at[slot, 0], te, nt, stage.at[0])
    y1 = _load_packed_rows(ybuf.at[slot, 1], te, nt, stage.at[1])
    x2 = x1_ref[...] + y0 * meta[:, 4:5] + y1 * meta[:, 5:6]
    out_ref[...] = _rms(x2, gfin_ref[...])


def _combine(dests, x1, meta, g_final, ys):
    t, d = x1.shape
    te = TE_COMBINE
    pt = d // (2 * LANES)
    return pl.pallas_call(
        _combine_kernel,
        grid_spec=pltpu.PrefetchScalarGridSpec(
            num_scalar_prefetch=TOP_K,
            grid=(t // te,),
            in_specs=[pl.BlockSpec((te, d), lambda i, *_: (i, 0)),
                      pl.BlockSpec((te, META_COLS), lambda i, *_: (i, 0)),
                      pl.BlockSpec((1, d), lambda i, *_: (0, 0)),
                      pl.BlockSpec(memory_space=pl.ANY)],
            out_specs=pl.BlockSpec((te, d), lambda i, *_: (i, 0)),
            scratch_shapes=[pltpu.VMEM((2, TOP_K, te * pt, LANES), jnp.uint32),
                            pltpu.VMEM((TOP_K, pt, 2 * te, LANES), F32),
                            pltpu.SemaphoreType.DMA((2,))],
        ),
        out_shape=jax.ShapeDtypeStruct((t, d), F32),
        compiler_params=pltpu.CompilerParams(dimension_semantics=("arbitrary",),
                                             vmem_limit_bytes=VMEM_LIMIT),
        name="combine",
    )(*dests, x1, meta, g_final, ys)


def _constants(tm_merge):
    n = CHUNK
    tri = (jnp.arange(n)[:, None] >= jnp.arange(n)[None, :]).astype(BF16)
    head = jnp.arange(2 * LANES) // HEAD_DIM
    seg_ones = (head[:, None] == head[None, :]).astype(BF16)
    tril_strict = (jnp.arange(tm_merge)[:, None] > jnp.arange(tm_merge)[None, :]).astype(BF16)
    return tri, seg_ones, tril_strict


def kernel(x, mem, g_mix, g_mem, w_in, conv_w, mu_rkv, mu_wag, w_lora1, w_lora2, w0, a_lora1, a_lora2, a0, g_lora1, g_lora2, k_k, k_a, r_k, ln_x_w, ln_x_b, w_kv_mem, w_branch, w_gate, b_gate, w_o, g_ffn, w_router_group, b_router_group, w_router_expert, b_router_expert, w_exp_gate, w_exp_up, w_exp_down, g_final):
    assert g_mix.shape[0] == 1, "single-layer block"
    b, s, d = x.shape
    t = b * s
    db = D_BRANCH
    tri, seg_ones, tril_strict = _constants(MERGE_SUB)
    row = lambda a: a.reshape(1, -1)
    pad_r = LANES - N_EXPERTS - N_GROUPS
    p = {
        "g_mix": row(g_mix[0]), "w_in": w_in[0].astype(BF16), "conv_w": conv_w[0].T,
        "mu_rkv": row(mu_rkv[0]), "mu_wag": mu_wag[0],
        "w_lora1": w_lora1[0].astype(BF16), "w_lora2": w_lora2[0].astype(BF16), "w0": row(w0[0]),
        "a_lora1": a_lora1[0].astype(BF16), "a_lora2": a_lora2[0].astype(BF16), "a0": row(a0[0]),
        "g_lora1": g_lora1[0].astype(BF16), "g_lora2": g_lora2[0].astype(BF16),
        "k_k": row(k_k[0]), "k_a": row(k_a[0]), "r_k": row(r_k[0]),
        "ln_x_w": row(ln_x_w[0]), "ln_x_b": row(ln_x_b[0]),
        "w_gate": w_gate[0].astype(BF16), "b_gate": row(b_gate[0]),
        "w_branch": w_branch[0].astype(BF16), "w_o": w_o[0].astype(BF16), "g_ffn": row(g_ffn[0]),
        "w_router": jnp.concatenate([w_router_expert[0], w_router_group[0],
                                     jnp.zeros((d, pad_r), F32)], axis=1),
        "b_router": row(jnp.concatenate([b_router_expert[0], b_router_group[0],
                                         jnp.zeros((pad_r,), F32)])),
        "tri": tri, "seg_ones": seg_ones, "tril_strict": tril_strict,
    }

    km, vm = _memkv(mem, row(g_mem[0]), w_kv_mem[0].astype(BF16))
    x2 = x.reshape(t, d)
    yconv, ymem, r, k, v, kkn, a, lw, g = _prologue(x2, b, km, vm, p)
    yrwkv = _rwkv(r, k, v, kkn, a, lw, g, b, p)
    x1, meta, meta_t, cnt = _merge(x2, yconv, yrwkv, ymem, p)

    counts = cnt[0, :N_EXPERTS].astype(jnp.int32)
    padded = ((counts + ROW_BLOCK - 1) // ROW_BLOCK) * ROW_BLOCK
    pad_end = jnp.cumsum(padded)
    pad_start = pad_end - padded
    n_blocks = (t * TOP_K) // ROW_BLOCK + N_EXPERTS
    eids = jnp.arange(N_EXPERTS, dtype=jnp.int32)
    e_idx = meta_t[0:TOP_K].astype(jnp.int32)
    rank = meta_t[TOP_K:2 * TOP_K].astype(jnp.int32)
    start_of = jnp.sum(jnp.where(e_idx[:, None, :] == eids[None, :, None], pad_start[None, :, None], 0), axis=1)
    dest = (start_of + rank) * (d // (2 * LANES))
    dests = [dest[kslot] for kslot in range(TOP_K)]
    blk_start = jnp.arange(n_blocks, dtype=jnp.int32) * ROW_BLOCK
    blk_expert = jnp.minimum(jnp.sum((pad_end[None, :] <= blk_start[:, None]).astype(jnp.int32), axis=1),
                             N_EXPERTS - 1)
    n_used = (pad_end[-1:] // ROW_BLOCK).astype(jnp.int32)
    later_nonempty = (eids[None, :] > eids[:, None]) & (counts[None, :] > 0)
    next_expert = jnp.min(jnp.where(later_nonempty, eids[None, :], N_EXPERTS), axis=1)

    xs = _scatter(dests, x1, p["g_ffn"], n_blocks * ROW_BLOCK)
    ys = _experts(blk_expert, n_used, next_expert, xs, w_exp_gate[0], w_exp_up[0], w_exp_down[0])
    out = _combine(dests, x1, meta, row(g_final), ys)
    return out.reshape(b, s, d)
```

```python
import functools

import jax
import jax.numpy as jnp
from jax import lax
from jax.experimental import pallas as pl
from jax.experimental.pallas import tpu as pltpu

F32 = jnp.float32
BF16 = jnp.bfloat16

NORM_EPS = 1e-6
GN_EPS = 64e-5
D_BRANCH = 512
HEAD_DIM = 64
N_HEADS = 8
CHUNK = 64
CHUNKS_PER_ITER = 4
MEM_HEADS = 4
MEM_HEAD_DIM = 128
N_GROUPS = 8
EXPERTS_PER_GROUP = 8
N_EXPERTS = 64
TOP_K = 2
ROW_BLOCK = 256
XS_BUFFERS = 4
LANES = 128
VMEM_LIMIT = 56 * 1024 * 1024

TM_PROLOGUE = 512
PROLOGUE_SUB = 256
TB_RWKV = 256
TM_MERGE = 512
MERGE_SUB = 512
TS_SCATTER = 256
TE_COMBINE = 256
SUBLANES = 8
META_COLS = 8
ISSUE_UNROLL = 2


def _bdot(a, b):
    return jnp.dot(a.astype(BF16), b.astype(BF16), preferred_element_type=F32)


def _bdot_nt(a, b):
    return lax.dot_general(a.astype(BF16), b.astype(BF16), (((1,), (1,)), ((), ())),
                           preferred_element_type=F32)


def _split_terms(x, n_terms):
    terms = []
    for _ in range(n_terms):
        t = x.astype(BF16)
        terms.append(t)
        x = x - t.astype(F32)
    return terms


def _split_dot_left(m_bf16, x, n_terms):
    return sum(jnp.dot(m_bf16, t, preferred_element_type=F32) for t in _split_terms(x, n_terms))


def _head_sums(x, seg_bf16, n_terms):
    w = seg_bf16.shape[0]
    terms = _split_terms(x, n_terms)
    halves = [sum(jnp.dot(t[:, c:c + w], seg_bf16, preferred_element_type=F32) for t in terms)
              for c in range(0, x.shape[1], w)]
    return jnp.concatenate(halves, axis=1)


def _rms(x, g):
    return x * lax.rsqrt(jnp.mean(x * x, axis=-1, keepdims=True) + NORM_EPS) * g


def _sigmoid(x):
    return 1.0 / (1.0 + jnp.exp(-x))


def _run_together(*gens):
    live = list(gens)
    while live:
        for gen in list(live):
            try:
                next(gen)
            except StopIteration:
                live.remove(gen)


def _software_pipeline(heavy, light, n_sub):
    _run_together(heavy(0))
    for j in range(1, n_sub):
        _run_together(heavy(j), light(j - 1))
    _run_together(light(n_sub - 1))


def _const_spec(shape):
    n = len(shape)
    return pl.BlockSpec(shape, lambda *_: (0,) * n)


def _memkv_kernel(mem_ref, g_ref, w_ref, k_ref, v_ref):
    mn = _rms(mem_ref[0], g_ref[...])
    kv = _bdot(mn, w_ref[...])
    k_ref[0] = kv[:, :D_BRANCH].astype(BF16)
    v_ref[0] = kv[:, D_BRANCH:].astype(BF16)


def _memkv(mem, g_mem, w_kv):
    b, m, d = mem.shape
    return pl.pallas_call(
        _memkv_kernel,
        grid=(b,),
        in_specs=[pl.BlockSpec((1, m, d), lambda i: (i, 0, 0)),
                  _const_spec((1, d)), _const_spec((d, 2 * D_BRANCH))],
        out_specs=[pl.BlockSpec((1, m, D_BRANCH), lambda i: (i, 0, 0)),
                   pl.BlockSpec((1, m, D_BRANCH), lambda i: (i, 0, 0))],
        out_shape=[jax.ShapeDtypeStruct((b, m, D_BRANCH), BF16)] * 2,
        compiler_params=pltpu.CompilerParams(dimension_semantics=("arbitrary",),
                                             vmem_limit_bytes=VMEM_LIMIT),
        name="memkv",
    )(mem, g_mem, w_kv)


def _prologue_kernel(x_ref, gmix_ref, win_ref, convw_ref, murkv_ref, muwag_ref,
                     wl1_ref, wl2_ref, w0_ref, al1_ref, al2_ref, a0_ref, gl1_ref, gl2_ref,
                     kk_ref, ka_ref, seg_ref, km_ref, vm_ref,
                     yconv_ref, ymem_ref, r_ref, k_ref, v_ref, kkn_ref, a_ref, lw_ref, g_ref,
                     prev_h, prev_p, prev_cu):
    tm = x_ref.shape[0]
    db = D_BRANCH
    sub = PROLOGUE_SUB
    n_parts = win_ref.shape[1] // db

    @pl.when(pl.program_id(1) == 0)
    def _():
        prev_h[...] = jnp.zeros_like(prev_h)
        prev_p[...] = jnp.zeros_like(prev_p)
        prev_cu[...] = jnp.zeros_like(prev_cu)

    rows = lax.broadcasted_iota(jnp.int32, (sub, 1), 0)

    def shift1(u, prev_row):
        return jnp.where(rows == 0, prev_row, pltpu.roll(u, 1, axis=0))

    carry = {"h": prev_h[...], "p": prev_p[...], "cu": prev_cu[...]}
    projected = {}

    def project(j):
        h = _rms(x_ref[j * sub:(j + 1) * sub, :], gmix_ref[...])
        hb = h.astype(BF16)
        parts = []
        for c in range(n_parts):
            parts.append(jnp.dot(hb, win_ref[:, c * db:(c + 1) * db], preferred_element_type=F32))
            yield
        projected[j] = (h, parts)

    def mix(j):
        rs = slice(j * sub, (j + 1) * sub)
        h, (bg, cg, u, rp, kp, vp, q) = projected.pop(j)

        cu = cg * u
        pcu = carry["cu"]
        cu1 = shift1(cu, pcu[1:2, :])
        cu2 = jnp.where(rows == 0, pcu[0:1, :], jnp.where(rows == 1, pcu[1:2, :], pltpu.roll(cu, 2, axis=0)))
        conv = cu2 * convw_ref[0:1, :] + cu1 * convw_ref[1:2, :] + cu * convw_ref[2:3, :]
        yconv_ref[rs, :] = (bg * conv).astype(BF16)
        carry["cu"] = cu[sub - 2:sub, :]

        pr = jnp.concatenate([rp, kp, vp], axis=1)
        prs = shift1(pr, carry["p"])
        mixed = pr + (prs - pr) * murkv_ref[...]
        carry["p"] = pr[sub - 1:sub, :]
        r, k, v = mixed[:, :db], mixed[:, db:2 * db], mixed[:, 2 * db:]
        r_ref[rs, :] = r
        v_ref[rs, :] = v

        dh = shift1(h, carry["h"]) - h
        carry["h"] = h[sub - 1:sub, :]
        lora_w = _bdot(h + dh * muwag_ref[0:1, :], wl1_ref[...])
        lora_a = _bdot(h + dh * muwag_ref[1:2, :], al1_ref[...])
        lora_g = _bdot(h + dh * muwag_ref[2:3, :], gl1_ref[...])
        yield
        zz = w0_ref[...] + _bdot(jnp.tanh(lora_w), wl2_ref[...])
        a_lin = a0_ref[...] + _bdot(lora_a, al2_ref[...])
        g_ref[rs, :] = _bdot(_sigmoid(lora_g), gl2_ref[...])
        yield
        softplus = jnp.maximum(-zz, 0.0) + jnp.log(1.0 + jnp.exp(-jnp.abs(zz)))
        lw_ref[rs, :] = -jnp.exp(-softplus - 0.5)
        a = _sigmoid(a_lin)
        a_ref[rs, :] = a
        k_ref[rs, :] = k * (1.0 + (a - 1.0) * ka_ref[...])
        kk = k * kk_ref[...]
        ss = _head_sums(kk * kk, seg_ref[...], 1)
        yield
        kkn_ref[rs, :] = kk * lax.rsqrt(jnp.maximum(ss, 1e-24))

        scale = MEM_HEAD_DIM ** -0.5
        heads = [slice(hh * MEM_HEAD_DIM, (hh + 1) * MEM_HEAD_DIM) for hh in range(MEM_HEADS)]
        scores = [_bdot_nt(q[:, sl], km_ref[0, :, sl]) * scale for sl in heads]
        yield
        for sl, s in zip(heads, scores):
            p = jnp.exp(s - jnp.max(s, axis=-1, keepdims=True))
            o = _bdot(p, vm_ref[0, :, sl]) / jnp.sum(p, axis=-1, keepdims=True)
            ymem_ref[rs, sl] = o.astype(BF16)

    _software_pipeline(project, mix, tm // sub)
    prev_h[...] = carry["h"]
    prev_p[...] = carry["p"]
    prev_cu[...] = carry["cu"]


def _prologue(x2, b, km, vm, p):
    t, d = x2.shape
    s = t // b
    tm = TM_PROLOGUE
    db = D_BRANCH
    m = km.shape[1]
    steps = s // tm
    tok = lambda c: pl.BlockSpec((tm, c), lambda bi, i: (bi * steps + i, 0))
    consts = [p["g_mix"], p["w_in"], p["conv_w"], p["mu_rkv"], p["mu_wag"],
              p["w_lora1"], p["w_lora2"], p["w0"], p["a_lora1"], p["a_lora2"], p["a0"],
              p["g_lora1"], p["g_lora2"], p["k_k"], p["k_a"], p["seg_ones"]]
    out_shapes = ([jax.ShapeDtypeStruct((t, db), BF16)] * 2
                  + [jax.ShapeDtypeStruct((t, db), F32)] * 7)
    return pl.pallas_call(
        _prologue_kernel,
        grid=(b, steps),
        in_specs=[tok(d)] + [_const_spec(c.shape) for c in consts]
                 + [pl.BlockSpec((1, m, db), lambda bi, i: (bi, 0, 0))] * 2,
        out_specs=[tok(db)] * 9,
        out_shape=out_shapes,
        scratch_shapes=[pltpu.VMEM((1, d), F32), pltpu.VMEM((1, 3 * db), F32),
                        pltpu.VMEM((2, db), F32)],
        compiler_params=pltpu.CompilerParams(dimension_semantics=("arbitrary", "arbitrary"),
                                             vmem_limit_bytes=VMEM_LIMIT),
        name="prologue",
    )(x2, *consts, km, vm)


def _rwkv_kernel(r_ref, k_ref, v_ref, kk_ref, a_ref, lw_ref, g_ref, rk_ref, lnw_ref, lnb_ref,
                 tri_ref, seg_ref, out_ref, h_scr, y_scr):
    tb = r_ref.shape[0]
    n = HEAD_DIM
    c_len = CHUNK

    @pl.when(pl.program_id(1) == 0)
    def _():
        h_scr[...] = jnp.zeros_like(h_scr)

    pw = 2 * n
    row1 = lax.broadcasted_iota(jnp.int32, (c_len, pw), 0)
    lane1 = lax.broadcasted_iota(jnp.int32, (c_len, pw), 1)
    col1 = lane1 & (n - 1)
    left = lane1 < n
    strict1 = col1 < row1
    incl1 = col1 <= row1
    eye2 = (col1 == row1).astype(F32)
    zeros_pair = jnp.zeros((c_len, pw), F32)
    zeros_bd = jnp.zeros((2 * c_len, pw), F32)

    def block_diag(y):
        return jnp.concatenate([jnp.where(left, y, 0.0), jnp.where(left, 0.0, y)], axis=0)

    def pair_transpose(y):
        zt = block_diag(y).T
        return zt[:c_len] + zt[c_len:]

    def chunk_inputs(c):
        rows = pl.ds(pl.multiple_of(c * c_len, c_len), c_len)
        r = r_ref[rows, :]
        k = k_ref[rows, :]
        v = v_ref[rows, :]
        kk = kk_ref[rows, :]
        a = a_ref[rows, :]
        lw = lw_ref[rows, :]
        gcum = _split_dot_left(tri_ref[...], lw, 2)
        e_pos = jnp.exp(gcum)
        e_neg = jnp.exp(-gcum)
        p_last = jnp.exp(gcum[c_len - 1:c_len, :])
        bb = kk * a * e_neg
        kb = k * e_neg
        return dict(rows=rows, v=v, p_last=p_last, rb=r * e_pos, ab=-kk * jnp.exp(gcum - lw), bb=bb, kb=kb,
                    bbp=bb * p_last, kbp=kb * p_last)

    def chunk_group(it, carry):
        chunks = [chunk_inputs(it * CHUNKS_PER_ITER + ci) for ci in range(CHUNKS_PER_ITER)]
        n_pairs = N_HEADS // 2
        units = [(ci, pr) for ci in range(CHUNKS_PER_ITER) for pr in range(n_pairs)]
        nu = range(len(units))
        ls = [slice(pr * pw, (pr + 1) * pw) for _, pr in units]
        ch = [chunks[ci] for ci, _ in units]
        al = [ch[u]["ab"][:, ls[u]] for u in nu]
        rr = [ch[u]["rb"][:, ls[u]] for u in nu]
        v_bd = [block_diag(ch[u]["v"][:, ls[u]]) for u in nu]
        aa = [_bdot_nt(jnp.concatenate([al[u], rr[u]], axis=0),
                       jnp.concatenate([block_diag(ch[u]["bb"][:, ls[u]]), block_diag(ch[u]["kb"][:, ls[u]])],
                                       axis=0)) for u in nu]
        a_ab = [jnp.where(strict1, aa[u][:c_len, :pw], 0.0) for u in nu]
        a_ak = [jnp.where(strict1, aa[u][:c_len, pw:], 0.0) for u in nu]
        a_rb = [jnp.where(incl1, aa[u][c_len:, :pw], 0.0) for u in nu]
        a_rk = [jnp.where(incl1, aa[u][c_len:, pw:], 0.0) for u in nu]
        av = [_bdot(a_ak[u], v_bd[u]) for u in nu]
        t_inv = [eye2 + a_ab[u] for u in nu]
        x_pow = [_bdot(a_ab[u], block_diag(a_ab[u])) for u in nu]
        for lvl in range(5):
            if lvl < 4:
                z = [_bdot(jnp.concatenate([t_inv[u], x_pow[u]], axis=0), block_diag(x_pow[u])) for u in nu]
                t_inv = [t_inv[u] + z[u][:c_len] for u in nu]
                x_pow = [z[u][c_len:] for u in nu]
            else:
                t_inv = [t_inv[u] + _bdot(t_inv[u], block_diag(x_pow[u])) for u in nu]
        w12 = [_bdot(t_inv[u], jnp.concatenate([block_diag(al[u]), block_diag(av[u])], axis=1))
               for u in nu]
        z2 = []
        for u in nu:
            rhs2 = jnp.concatenate(
                [jnp.concatenate([block_diag(w12[u][:, :pw]), block_diag(w12[u][:, pw:])], axis=1),
                 jnp.concatenate([zeros_bd, v_bd[u]], axis=1)], axis=0)
            lhs3 = jnp.concatenate(
                [jnp.concatenate([pair_transpose(ch[u]["bbp"][:, ls[u]]),
                                  pair_transpose(ch[u]["kbp"][:, ls[u]])], axis=1),
                 jnp.concatenate([a_rb[u], a_rk[u]], axis=1)], axis=0)
            z2.append(_bdot(lhs3, rhs2))
        state = [h_scr[pr] for pr in range(n_pairs)]
        for u in nu:
            pr = units[u][1]
            mq = z2[u][:, :pw] + jnp.concatenate([zeros_pair, rr[u]], axis=0)
            out = _bdot(mq, block_diag(state[pr])) + z2[u][:, pw:]
            decay = eye2 * ch[u]["p_last"][:, ls[u]]
            p_mat = jnp.where(left, jnp.sum(jnp.where(left, decay, 0.0), axis=1, keepdims=True),
                              jnp.sum(jnp.where(left, 0.0, decay), axis=1, keepdims=True))
            state[pr] = p_mat * state[pr] + out[:c_len]
            y_scr[ch[u]["rows"], ls[u]] = out[c_len:]
        for pr in range(n_pairs):
            h_scr[pr] = state[pr]
        return carry

    lax.fori_loop(0, tb // (c_len * CHUNKS_PER_ITER), chunk_group, 0)

    y = y_scr[...]
    seg = seg_ref[...]
    inv_n = 1.0 / n
    mu = _head_sums(y, seg, 2) * inv_n
    yc = y - mu
    var = _head_sums(yc * yc, seg, 1) * inv_n
    yn = yc * lax.rsqrt(var + GN_EPS) * lnw_ref[...] + lnb_ref[...]
    bonus = _head_sums(r_ref[...] * k_ref[...] * rk_ref[...], seg, 1) * v_ref[...]
    out_ref[...] = ((yn + bonus) * g_ref[...]).astype(BF16)


def _rwkv(r, k, v, kkn, a, lw, g, b, p):
    t, db = r.shape
    tb = TB_RWKV
    steps = t // b // tb
    tok = pl.BlockSpec((tb, db), lambda bi, i: (bi * steps + i, 0))
    consts = [p["r_k"], p["ln_x_w"], p["ln_x_b"], p["tri"], p["seg_ones"]]
    return pl.pallas_call(
        _rwkv_kernel,
        grid=(b, steps),
        in_specs=[tok] * 7 + [_const_spec(c.shape) for c in consts],
        out_specs=tok,
        out_shape=jax.ShapeDtypeStruct((t, db), BF16),
        scratch_shapes=[pltpu.VMEM((N_HEADS // 2, HEAD_DIM, 2 * HEAD_DIM), F32),
                        pltpu.VMEM((tb, db), F32)],
        compiler_params=pltpu.CompilerParams(dimension_semantics=("arbitrary", "arbitrary"),
                                             vmem_limit_bytes=VMEM_LIMIT),
        name="rwkv",
    )(r, k, v, kkn, a, lw, g, *consts)


def _merge_kernel(x_ref, yc_ref, yr_ref, ym_ref, gmix_ref, wgate_ref, bgate_ref, wbr_ref, wo_ref,
                  gffn_ref, wrt_ref, brt_ref, tril_ref,
                  x1_ref, meta_ref, metat_ref, cnt_ref, base_scr):
    tm, d = x_ref.shape

    @pl.when(pl.program_id(0) == 0)
    def _():
        base_scr[...] = jnp.zeros_like(base_scr)

    sub = tril_ref.shape[0]
    lane = lax.broadcasted_iota(jnp.int32, (sub, LANES), 1)
    neg = jnp.float32(-jnp.inf)
    big = jnp.int32(1 << 20)
    w_hi, w_lo = _split_terms(wrt_ref[...], 2)
    w_hi_lo = jnp.concatenate([w_hi, w_lo], axis=1)
    state = {"base": base_scr[...]}
    merged = {}

    def project(j):
        rs = slice(j * sub, (j + 1) * sub)
        x = x_ref[rs, :]
        hb = _rms(x, gmix_ref[...]).astype(BF16)
        z = jnp.zeros((sub, d), F32)
        for i, y_ref in enumerate((yc_ref, yr_ref, ym_ref)):
            cs = slice(i * d, (i + 1) * d)
            gate = _sigmoid(jnp.dot(hb, wgate_ref[:, cs], preferred_element_type=F32) + bgate_ref[:, cs])
            z = z + gate * jnp.dot(y_ref[rs, :], wbr_ref[i], preferred_element_type=F32)
            yield
        x1 = x + _bdot(z, wo_ref[...])
        x1_ref[rs, :] = x1
        merged[j] = x1

    def route(j):
        rs = slice(j * sub, (j + 1) * sub)
        h2 = _rms(merged.pop(j), gffn_ref[...])
        h_hi, h_lo = _split_terms(h2, 2)
        hi_terms = jnp.dot(h_hi, w_hi_lo, preferred_element_type=F32)
        logits = (hi_terms[:, :LANES]
                  + (jnp.dot(h_lo, w_hi, preferred_element_type=F32) + hi_terms[:, LANES:])) + brt_ref[...]
        yield
        gmask = (lane >= N_EXPERTS) & (lane < N_EXPERTS + N_GROUPS)
        glv = jnp.where(gmask, logits, neg)
        gmax = jnp.max(glv, axis=-1, keepdims=True)
        g_sel = jnp.min(jnp.where(glv == gmax, lane - N_EXPERTS, big), axis=-1, keepdims=True)
        g_w = 1.0 / jnp.sum(jnp.exp(glv - gmax), axis=-1, keepdims=True)
        emask = (lane < N_EXPERTS) & ((lane >> 3) == g_sel)
        elv = jnp.where(emask, logits, neg)
        emax = jnp.max(elv, axis=-1, keepdims=True)
        esum = jnp.sum(jnp.exp(elv - emax), axis=-1, keepdims=True)
        i1 = jnp.min(jnp.where(elv == emax, lane, big), axis=-1, keepdims=True)
        elv2 = jnp.where(lane == i1, neg, elv)
        m2 = jnp.max(elv2, axis=-1, keepdims=True)
        i2 = jnp.min(jnp.where(elv2 == m2, lane, big), axis=-1, keepdims=True)
        p1 = 1.0 / esum
        p2 = jnp.exp(m2 - emax) / esum
        c1 = g_w * p1 / (p1 + p2)
        c2 = g_w * p2 / (p1 + p2)

        oh1 = lane == i1
        oh2 = lane == i2
        onehot = jnp.where(oh1 | oh2, 1.0, 0.0)
        before = jnp.dot(tril_ref[...], onehot.astype(BF16), preferred_element_type=F32) + state["base"]
        yield
        rank1 = jnp.sum(jnp.where(oh1, before, 0.0), axis=-1, keepdims=True)
        rank2 = jnp.sum(jnp.where(oh2, before, 0.0), axis=-1, keepdims=True)
        state["base"] = state["base"] + jnp.sum(onehot, axis=0, keepdims=True)

        meta = jnp.where(lane == 0, i1.astype(F32),
               jnp.where(lane == 1, i2.astype(F32),
               jnp.where(lane == 2, rank1,
               jnp.where(lane == 3, rank2,
               jnp.where(lane == 4, c1,
               jnp.where(lane == 5, c2, 0.0))))))
        meta_ref[rs, :] = meta[:, :META_COLS]
        metat_ref[:, rs] = meta.T[:META_COLS, :]

    _software_pipeline(project, route, tm // sub)
    base_scr[...] = state["base"]
    cnt_ref[...] = jnp.broadcast_to(state["base"], cnt_ref.shape)


def _merge(x2, yc, yr, ym, p):
    t, d = x2.shape
    tm = TM_MERGE
    db = D_BRANCH
    tok = lambda c: pl.BlockSpec((tm, c), lambda i: (i, 0))
    consts = [p["g_mix"], p["w_gate"], p["b_gate"], p["w_branch"], p["w_o"], p["g_ffn"],
              p["w_router"], p["b_router"], p["tril_strict"]]
    return pl.pallas_call(
        _merge_kernel,
        grid=(t // tm,),
        in_specs=[tok(d), tok(db), tok(db), tok(db)] + [_const_spec(c.shape) for c in consts],
        out_specs=[tok(d), tok(META_COLS), pl.BlockSpec((META_COLS, tm), lambda i: (0, i)),
                   _const_spec((8, LANES))],
        out_shape=[jax.ShapeDtypeStruct((t, d), F32), jax.ShapeDtypeStruct((t, META_COLS), F32),
                   jax.ShapeDtypeStruct((META_COLS, t), F32), jax.ShapeDtypeStruct((8, LANES), F32)],
        scratch_shapes=[pltpu.VMEM((1, LANES), F32)],
        compiler_params=pltpu.CompilerParams(dimension_semantics=("arbitrary",),
                                             vmem_limit_bytes=VMEM_LIMIT),
        name="merge",
    )(x2, yc, yr, ym, *consts)


def _store_packed_rows(ref2d, x, stage):
    rows, d = x.shape
    nt = d // (2 * LANES)
    for c in range(nt):
        stage[c, pl.ds(0, rows, stride=2), :] = x[:, c * LANES:(c + 1) * LANES]
        stage[c, pl.ds(1, rows, stride=2), :] = x[:, (c + nt) * LANES:(c + nt + 1) * LANES]
        ref2d[pl.ds(c, rows, stride=nt), :] = pltpu.bitcast(stage[c].astype(BF16), jnp.uint32)


def _load_packed_rows(ref2d, rows, nt, stage):
    lo, hi = [], []
    for c in range(nt):
        stage[c] = pltpu.bitcast(ref2d[pl.ds(c, rows, stride=nt), :], BF16).astype(F32)
        lo.append(stage[c, pl.ds(0, rows, stride=2), :])
        hi.append(stage[c, pl.ds(1, rows, stride=2), :])
    return jnp.concatenate(lo + hi, axis=1)


def _scatter_kernel(dest0_ref, dest1_ref, x1_ref, gffn_ref, xs_ref, hbuf, stage, sem):
    dest_refs = (dest0_ref, dest1_ref)
    ts, d_model = x1_ref.shape
    nt = d_model // (2 * LANES)
    s = pl.program_id(0)
    slot = s % 2

    def wait_slot(sl):
        for _ in range(TOP_K):
            pltpu.make_async_copy(hbuf.at[sl], xs_ref.at[pl.ds(0, ts * nt), :], sem.at[sl]).wait()

    @pl.when(s >= 2)
    def _():
        wait_slot(slot)

    _store_packed_rows(hbuf.at[slot], _rms(x1_ref[...], gffn_ref[...]), stage)

    def issue(grp, carry):
        grp_off = pl.multiple_of(grp * (SUBLANES * nt), SUBLANES * nt)
        for j in range(SUBLANES):
            tok = s * ts + grp * SUBLANES + j
            for kslot in range(TOP_K):
                d = pl.multiple_of(dest_refs[kslot][tok], nt)
                pltpu.make_async_copy(hbuf.at[slot, pl.ds(grp_off + j * nt, nt), :],
                                      xs_ref.at[pl.ds(d, nt), :], sem.at[slot]).start(priority=kslot)
        return carry

    lax.fori_loop(0, ts // SUBLANES, issue, 0, unroll=ISSUE_UNROLL)

    @pl.when(s == pl.num_programs(0) - 1)
    def _():
        @pl.when(s >= 1)
        def _():
            wait_slot(1 - slot)
        wait_slot(slot)


def _scatter(dests, x1, g_ffn, n_rows):
    t, d = x1.shape
    ts = TS_SCATTER
    pt = d // (2 * LANES)
    return pl.pallas_call(
        _scatter_kernel,
        grid_spec=pltpu.PrefetchScalarGridSpec(
            num_scalar_prefetch=TOP_K,
            grid=(t // ts,),
            in_specs=[pl.BlockSpec((ts, d), lambda i, *_: (i, 0)),
                      pl.BlockSpec((1, d), lambda i, *_: (0, 0))],
            out_specs=pl.BlockSpec(memory_space=pl.ANY),
            scratch_shapes=[pltpu.VMEM((2, ts * pt, LANES), jnp.uint32),
                            pltpu.VMEM((pt, 2 * ts, LANES), F32),
                            pltpu.SemaphoreType.DMA((2,))],
        ),
        out_shape=jax.ShapeDtypeStruct((n_rows * pt, LANES), jnp.uint32),
        compiler_params=pltpu.CompilerParams(dimension_semantics=("arbitrary",),
                                             vmem_limit_bytes=VMEM_LIMIT),
        name="scatter",
    )(*dests, x1, g_ffn)


def _experts_kernel(be_ref, nused_ref, nexte_ref, xs_ref, wg_hbm, wu_hbm, wd_hbm, ys_ref,
                    wg_f, wu_f, wd_f, wg_s, wu_s, wd_s, xbuf, ybuf, xstage, ystage, sem, xsem, ysem):
    i = pl.program_id(0)
    e = be_ref[i]
    prev = be_ref[jnp.maximum(i - 1, 0)]
    active = i < nused_ref[0]

    def weight_copies(ex):
        return (pltpu.make_async_copy(wg_hbm.at[ex], wg_f, sem.at[0]),
                pltpu.make_async_copy(wu_hbm.at[ex], wu_f, sem.at[1]),
                pltpu.make_async_copy(wd_hbm.at[ex], wd_f, sem.at[2]))

    @pl.when(i == 0)
    def _():
        for cp in weight_copies(e):
            cp.start(priority=1)

    @pl.when(active & ((i == 0) | (e != prev)))
    def _():
        for cp in weight_copies(e):
            cp.wait()
        wg_s[...] = wg_f[...].astype(BF16)
        wu_s[...] = wu_f[...].astype(BF16)
        wd_s[...] = wd_f[...].astype(BF16)
        nxt = nexte_ref[e]

        @pl.when(nxt < N_EXPERTS)
        def _():
            for cp in weight_copies(nxt):
                cp.start(priority=1)

    blk_rows = xbuf.shape[1]
    n_used = nused_ref[0]

    def xs_copy(blk, slot):
        return pltpu.make_async_copy(xs_ref.at[pl.ds(pl.multiple_of(blk * blk_rows, blk_rows), blk_rows), :],
                                     xbuf.at[slot], xsem.at[slot])

    def ys_copy(blk, slot):
        return pltpu.make_async_copy(ybuf.at[slot],
                                     ys_ref.at[pl.ds(pl.multiple_of(blk * blk_rows, blk_rows), blk_rows), :],
                                     ysem.at[slot])

    @pl.when(i == 0)
    def _():
        for ahead in range(XS_BUFFERS - 1):
            @pl.when(ahead < n_used)
            def _():
                xs_copy(ahead, ahead).start()

    @pl.when(active)
    def _():
        nt = wg_s.shape[0] // (2 * LANES)
        fetch = i + (XS_BUFFERS - 1)

        @pl.when(fetch < n_used)
        def _():
            xs_copy(fetch, fetch % XS_BUFFERS).start()

        xs_copy(i, i % XS_BUFFERS).wait()
        oslot = i % 2

        @pl.when(i >= 2)
        def _():
            ys_copy(i - 2, oslot).wait()

        xb = _load_packed_rows(xbuf.at[i % XS_BUFFERS], ROW_BLOCK, nt, xstage).astype(BF16)
        gate = jnp.dot(xb, wg_s[...], preferred_element_type=F32)
        up = jnp.dot(xb, wu_s[...], preferred_element_type=F32)
        hid = gate * _sigmoid(gate) * up
        _store_packed_rows(ybuf.at[oslot], jnp.dot(hid.astype(BF16), wd_s[...], preferred_element_type=F32),
                           ystage)
        ys_copy(i, oslot).start()

        @pl.when(i == n_used - 1)
        def _():
            @pl.when(i >= 1)
            def _():
                ys_copy(i - 1, 1 - oslot).wait()
            ys_copy(i, oslot).wait()


def _experts(blk_expert, n_used, next_expert, xs, w_gate, w_up, w_down):
    d, de = w_gate.shape[-2:]
    pt = d // (2 * LANES)
    blk_rows = ROW_BLOCK * pt
    nb = xs.shape[0] // blk_rows

    return pl.pallas_call(
        _experts_kernel,
        grid_spec=pltpu.PrefetchScalarGridSpec(
            num_scalar_prefetch=3,
            grid=(nb,),
            in_specs=[pl.BlockSpec(memory_space=pl.ANY)] * 4,
            out_specs=pl.BlockSpec(memory_space=pl.ANY),
            scratch_shapes=[pltpu.VMEM((d, de), F32), pltpu.VMEM((d, de), F32), pltpu.VMEM((de, d), F32),
                            pltpu.VMEM((d, de), BF16), pltpu.VMEM((d, de), BF16), pltpu.VMEM((de, d), BF16),
                            pltpu.VMEM((XS_BUFFERS, blk_rows, LANES), jnp.uint32),
                            pltpu.VMEM((2, blk_rows, LANES), jnp.uint32),
                            pltpu.VMEM((pt, 2 * ROW_BLOCK, LANES), F32),
                            pltpu.VMEM((pt, 2 * ROW_BLOCK, LANES), F32),
                            pltpu.SemaphoreType.DMA((3,)), pltpu.SemaphoreType.DMA((XS_BUFFERS,)),
                            pltpu.SemaphoreType.DMA((2,))],
        ),
        out_shape=jax.ShapeDtypeStruct(xs.shape, jnp.uint32),
        compiler_params=pltpu.CompilerParams(dimension_semantics=("arbitrary",),
                                             vmem_limit_bytes=VMEM_LIMIT),
        name="experts",
    )(blk_expert, n_used, next_expert, xs, w_gate, w_up, w_down)


def _combine_kernel(dest0_ref, dest1_ref, x1_ref, meta_ref, gfin_ref, ys_ref, out_ref, ybuf, stage, sem):
    dest_refs = (dest0_ref, dest1_ref)
    te = x1_ref.shape[0]
    s = pl.program_id(0)
    nsteps = pl.num_programs(0)
    slot = s % 2

    nt = x1_ref.shape[1] // (2 * LANES)

    def issue_step(step, sl):
        def issue(grp, carry):
            grp_off = pl.multiple_of(grp * (SUBLANES * nt), SUBLANES * nt)
            for j in range(SUBLANES):
                tok = step * te + grp * SUBLANES + j
                for kslot in range(TOP_K):
                    d = pl.multiple_of(dest_refs[kslot][tok], nt)
                    pltpu.make_async_copy(ys_ref.at[pl.ds(d, nt), :],
                                          ybuf.at[sl, kslot, pl.ds(grp_off + j * nt, nt), :],
                                          sem.at[sl]).start(priority=kslot)
            return carry
        lax.fori_loop(0, te // SUBLANES, issue, 0, unroll=ISSUE_UNROLL)

    @pl.when(s == 0)
    def _():
        issue_step(0, 0)

    @pl.when(s + 1 < nsteps)
    def _():
        issue_step(s + 1, 1 - slot)

    for kslot in range(TOP_K):
        pltpu.make_async_copy(ys_ref.at[pl.ds(0, te * nt), :], ybuf.at[slot, kslot], sem.at[slot]).wait()

    meta = meta_ref[...]
    y0 = _load_packed_rows(ybuf.at[slot, 0], te, nt, stage.at[0])
    y1 = _load_packed_rows(ybuf.at[slot, 1], te, nt, stage.at[1])
    x2 = x1_ref[...] + y0 * meta[:, 4:5] + y1 * meta[:, 5:6]
    out_ref[...] = _rms(x2, gfin_ref[...])


def _combine(dests, x1, meta, g_final, ys):
    t, d = x1.shape
    te = TE_COMBINE
    pt = d // (2 * LANES)
    return pl.pallas_call(
        _combine_kernel,
        grid_spec=pltpu.PrefetchScalarGridSpec(
            num_scalar_prefetch=TOP_K,
            grid=(t // te,),
            in_specs=[pl.BlockSpec((te, d), lambda i, *_: (i, 0)),
                      pl.BlockSpec((te, META_COLS), lambda i, *_: (i, 0)),
                      pl.BlockSpec((1, d), lambda i, *_: (0, 0)),
                      pl.BlockSpec(memory_space=pl.ANY)],
            out_specs=pl.BlockSpec((te, d), lambda i, *_: (i, 0)),
            scratch_shapes=[pltpu.VMEM((2, TOP_K, te * pt, LANES), jnp.uint32),
                            pltpu.VMEM((TOP_K, pt, 2 * te, LANES), F32),
                            pltpu.SemaphoreType.DMA((2,))],
        ),
        out_shape=jax.ShapeDtypeStruct((t, d), F32),
        compiler_params=pltpu.CompilerParams(dimension_semantics=("arbitrary",),
                                             vmem_limit_bytes=VMEM_LIMIT),
        name="combine",
    )(*dests, x1, meta, g_final, ys)


def _constants(tm_merge):
    n = CHUNK
    tri = (jnp.arange(n)[:, None] >= jnp.arange(n)[None, :]).astype(BF16)
    head = jnp.arange(2 * LANES) // HEAD_DIM
    seg_ones = (head[:, None] == head[None, :]).astype(BF16)
    tril_strict = (jnp.arange(tm_merge)[:, None] > jnp.arange(tm_merge)[None, :]).astype(BF16)
    return tri, seg_ones, tril_strict


def kernel(x, mem, g_mix, g_mem, w_in, conv_w, mu_rkv, mu_wag, w_lora1, w_lora2, w0, a_lora1, a_lora2, a0, g_lora1, g_lora2, k_k, k_a, r_k, ln_x_w, ln_x_b, w_kv_mem, w_branch, w_gate, b_gate, w_o, g_ffn, w_router_group, b_router_group, w_router_expert, b_router_expert, w_exp_gate, w_exp_up, w_exp_down, g_final):
    assert g_mix.shape[0] == 1, "single-layer block"
    b, s, d = x.shape
    t = b * s
    db = D_BRANCH
    tri, seg_ones, tril_strict = _constants(MERGE_SUB)
    row = lambda a: a.reshape(1, -1)
    pad_r = LANES - N_EXPERTS - N_GROUPS
    p = {
        "g_mix": row(g_mix[0]), "w_in": w_in[0].astype(BF16), "conv_w": conv_w[0].T,
        "mu_rkv": row(mu_rkv[0]), "mu_wag": mu_wag[0],
        "w_lora1": w_lora1[0].astype(BF16), "w_lora2": w_lora2[0].astype(BF16), "w0": row(w0[0]),
        "a_lora1": a_lora1[0].astype(BF16), "a_lora2": a_lora2[0].astype(BF16), "a0": row(a0[0]),
        "g_lora1": g_lora1[0].astype(BF16), "g_lora2": g_lora2[0].astype(BF16),
        "k_k": row(k_k[0]), "k_a": row(k_a[0]), "r_k": row(r_k[0]),
        "ln_x_w": row(ln_x_w[0]), "ln_x_b": row(ln_x_b[0]),
        "w_gate": w_gate[0].astype(BF16), "b_gate": row(b_gate[0]),
        "w_branch": w_branch[0].astype(BF16), "w_o": w_o[0].astype(BF16), "g_ffn": row(g_ffn[0]),
        "w_router": jnp.concatenate([w_router_expert[0], w_router_group[0],
                                     jnp.zeros((d, pad_r), F32)], axis=1),
        "b_router": row(jnp.concatenate([b_router_expert[0], b_router_group[0],
                                         jnp.zeros((pad_r,), F32)])),
        "tri": tri, "seg_ones": seg_ones, "tril_strict": tril_strict,
    }

    km, vm = _memkv(mem, row(g_mem[0]), w_kv_mem[0].astype(BF16))
    x2 = x.reshape(t, d)
    yconv, ymem, r, k, v, kkn, a, lw, g = _prologue(x2, b, km, vm, p)
    yrwkv = _rwkv(r, k, v, kkn, a, lw, g, b, p)
    x1, meta, meta_t, cnt = _merge(x2, yconv, yrwkv, ymem, p)

    counts = cnt[0, :N_EXPERTS].astype(jnp.int32)
    padded = ((counts + ROW_BLOCK - 1) // ROW_BLOCK) * ROW_BLOCK
    pad_end = jnp.cumsum(padded)
    pad_start = pad_end - padded
    n_blocks = (t * TOP_K) // ROW_BLOCK + N_EXPERTS
    eids = jnp.arange(N_EXPERTS, dtype=jnp.int32)
    e_idx = meta_t[0:TOP_K].astype(jnp.int32)
    rank = meta_t[TOP_K:2 * TOP_K].astype(jnp.int32)
    start_of = jnp.sum(jnp.where(e_idx[:, None, :] == eids[None, :, None], pad_start[None, :, None], 0), axis=1)
    dest = (start_of + rank) * (d // (2 * LANES))
    dests = [dest[kslot] for kslot in range(TOP_K)]
    blk_start = jnp.arange(n_blocks, dtype=jnp.int32) * ROW_BLOCK
    blk_expert = jnp.minimum(jnp.sum((pad_end[None, :] <= blk_start[:, None]).astype(jnp.int32), axis=1),
                             N_EXPERTS - 1)
    n_used = (pad_end[-1:] // ROW_BLOCK).astype(jnp.int32)
    later_nonempty = (eids[None, :] > eids[:, None]) & (counts[None, :] > 0)
    next_expert = jnp.min(jnp.where(later_nonempty, eids[None, :], N_EXPERTS), axis=1)

    xs = _scatter(dests, x1, p["g_ffn"], n_blocks * ROW_BLOCK)
    ys = _experts(blk_expert, n_used, next_expert, xs, w_exp_gate[0], w_exp_up[0], w_exp_down[0])
    out = _combine(dests, x1, meta, row(g_final), ys)
    return out.reshape(b, s, d)
```

```python
import functools

import jax
import jax.numpy as jnp
from jax import lax
from jax.experimental import pallas as pl
from jax.experimental.pallas import tpu as pltpu

F32 = jnp.float32
BF16 = jnp.bfloat16

NORM_EPS = 1e-6
GN_EPS = 64e-5
D_BRANCH = 512
HEAD_DIM = 64
N_HEADS = 8
CHUNK = 64
CHUNKS_PER_ITER = 4
MEM_HEADS = 4
MEM_HEAD_DIM = 128
N_GROUPS = 8
EXPERTS_PER_GROUP = 8
N_EXPERTS = 64
TOP_K = 2
ROW_BLOCK = 256
XS_BUFFERS = 4
LANES = 128
VMEM_LIMIT = 56 * 1024 * 1024

TM_PROLOGUE = 512
PROLOGUE_SUB = 256
TB_RWKV = 256
TM_MERGE = 512
MERGE_SUB = 512
TS_SCATTER = 256
TE_COMBINE = 256
SUBLANES = 8
META_COLS = 8
ISSUE_UNROLL = 2


def _bdot(a, b):
    return jnp.dot(a.astype(BF16), b.astype(BF16), preferred_element_type=F32)


def _bdot_nt(a, b):
    return lax.dot_general(a.astype(BF16), b.astype(BF16), (((1,), (1,)), ((), ())),
                           preferred_element_type=F32)


def _split_terms(x, n_terms):
    terms = []
    for _ in range(n_terms):
        t = x.astype(BF16)
        terms.append(t)
        x = x - t.astype(F32)
    return terms


def _split_dot_left(m_bf16, x, n_terms):
    return sum(jnp.dot(m_bf16, t, preferred_element_type=F32) for t in _split_terms(x, n_terms))


def _head_sums(x, seg_bf16, n_terms):
    w = seg_bf16.shape[0]
    terms = _split_terms(x, n_terms)
    halves = [sum(jnp.dot(t[:, c:c + w], seg_bf16, preferred_element_type=F32) for t in terms)
              for c in range(0, x.shape[1], w)]
    return jnp.concatenate(halves, axis=1)


def _rms(x, g):
    return x * lax.rsqrt(jnp.mean(x * x, axis=-1, keepdims=True) + NORM_EPS) * g


def _sigmoid(x):
    return 1.0 / (1.0 + jnp.exp(-x))


def _run_together(*gens):
    live = list(gens)
    while live:
        for gen in list(live):
            try:
                next(gen)
            except StopIteration:
                live.remove(gen)


def _software_pipeline(heavy, light, n_sub):
    _run_together(heavy(0))
    for j in range(1, n_sub):
        _run_together(heavy(j), light(j - 1))
    _run_together(light(n_sub - 1))


def _const_spec(shape):
    n = len(shape)
    return pl.BlockSpec(shape, lambda *_: (0,) * n)


def _memkv_kernel(mem_ref, g_ref, w_ref, k_ref, v_ref):
    mn = _rms(mem_ref[0], g_ref[...])
    kv = _bdot(mn, w_ref[...])
    k_ref[0] = kv[:, :D_BRANCH].astype(BF16)
    v_ref[0] = kv[:, D_BRANCH:].astype(BF16)


def _memkv(mem, g_mem, w_kv):
    b, m, d = mem.shape
    return pl.pallas_call(
        _memkv_kernel,
        grid=(b,),
        in_specs=[pl.BlockSpec((1, m, d), lambda i: (i, 0, 0)),
                  _const_spec((1, d)), _const_spec((d, 2 * D_BRANCH))],
        out_specs=[pl.BlockSpec((1, m, D_BRANCH), lambda i: (i, 0, 0)),
                   pl.BlockSpec((1, m, D_BRANCH), lambda i: (i, 0, 0))],
        out_shape=[jax.ShapeDtypeStruct((b, m, D_BRANCH), BF16)] * 2,
        compiler_params=pltpu.CompilerParams(dimension_semantics=("arbitrary",),
                                             vmem_limit_bytes=VMEM_LIMIT),
        name="memkv",
    )(mem, g_mem, w_kv)


def _prologue_kernel(x_ref, gmix_ref, win_ref, convw_ref, murkv_ref, muwag_ref,
                     wl1_ref, wl2_ref, w0_ref, al1_ref, al2_ref, a0_ref, gl1_ref, gl2_ref,
                     kk_ref, ka_ref, seg_ref, km_ref, vm_ref,
                     yconv_ref, ymem_ref, r_ref, k_ref, v_ref, kkn_ref, a_ref, lw_ref, g_ref,
                     prev_h, prev_p, prev_cu):
    tm = x_ref.shape[0]
    db = D_BRANCH
    sub = PROLOGUE_SUB
    n_parts = win_ref.shape[1] // db

    @pl.when(pl.program_id(1) == 0)
    def _():
        prev_h[...] = jnp.zeros_like(prev_h)
        prev_p[...] = jnp.zeros_like(prev_p)
        prev_cu[...] = jnp.zeros_like(prev_cu)

    rows = lax.broadcasted_iota(jnp.int32, (sub, 1), 0)

    def shift1(u, prev_row):
        return jnp.where(rows == 0, prev_row, pltpu.roll(u, 1, axis=0))

    carry = {"h": prev_h[...], "p": prev_p[...], "cu": prev_cu[...]}
    projected = {}

    def project(j):
        h = _rms(x_ref[j * sub:(j + 1) * sub, :], gmix_ref[...])
        hb = h.astype(BF16)
        parts = []
        for c in range(n_parts):
            parts.append(jnp.dot(hb, win_ref[:, c * db:(c + 1) * db], preferred_element_type=F32))
            yield
        projected[j] = (h, parts)

    def mix(j):
        rs = slice(j * sub, (j + 1) * sub)
        h, (bg, cg, u, rp, kp, vp, q) = projected.pop(j)

        cu = cg * u
        pcu = carry["cu"]
        cu1 = shift1(cu, pcu[1:2, :])
        cu2 = jnp.where(rows == 0, pcu[0:1, :], jnp.where(rows == 1, pcu[1:2, :], pltpu.roll(cu, 2, axis=0)))
        conv = cu2 * convw_ref[0:1, :] + cu1 * convw_ref[1:2, :] + cu * convw_ref[2:3, :]
        yconv_ref[rs, :] = (bg * conv).astype(BF16)
        carry["cu"] = cu[sub - 2:sub, :]

        pr = jnp.concatenate([rp, kp, vp], axis=1)
        prs = shift1(pr, carry["p"])
        mixed = pr + (prs - pr) * murkv_ref[...]
        carry["p"] = pr[sub - 1:sub, :]
        r, k, v = mixed[:, :db], mixed[:, db:2 * db], mixed[:, 2 * db:]
        r_ref[rs, :] = r
        v_ref[rs, :] = v

        dh = shift1(h, carry["h"]) - h
        carry["h"] = h[sub - 1:sub, :]
        lora_w = _bdot(h + dh * muwag_ref[0:1, :], wl1_ref[...])
        lora_a = _bdot(h + dh * muwag_ref[1:2, :], al1_ref[...])
        lora_g = _bdot(h + dh * muwag_ref[2:3, :], gl1_ref[...])
        yield
        zz = w0_ref[...] + _bdot(jnp.tanh(lora_w), wl2_ref[...])
        a_lin = a0_ref[...] + _bdot(lora_a, al2_ref[...])
        g_ref[rs, :] = _bdot(_sigmoid(lora_g), gl2_ref[...])
        yield
        softplus = jnp.maximum(-zz, 0.0) + jnp.log(1.0 + jnp.exp(-jnp.abs(zz)))
        lw_ref[rs, :] = -jnp.exp(-softplus - 0.5)
        a = _sigmoid(a_lin)
        a_ref[rs, :] = a
        k_ref[rs, :] = k * (1.0 + (a - 1.0) * ka_ref[...])
        kk = k * kk_ref[...]
        ss = _head_sums(kk * kk, seg_ref[...], 1)
        yield
        kkn_ref[rs, :] = kk * lax.rsqrt(jnp.maximum(ss, 1e-24))

        scale = MEM_HEAD_DIM ** -0.5
        heads = [slice(hh * MEM_HEAD_DIM, (hh + 1) * MEM_HEAD_DIM) for hh in range(MEM_HEADS)]
        scores = [_bdot_nt(q[:, sl], km_ref[0, :, sl]) * scale for sl in heads]
        yield
        for sl, s in zip(heads, scores):
            p = jnp.exp(s - jnp.max(s, axis=-1, keepdims=True))
            o = _bdot(p, vm_ref[0, :, sl]) / jnp.sum(p, axis=-1, keepdims=True)
            ymem_ref[rs, sl] = o.astype(BF16)

    _software_pipeline(project, mix, tm // sub)
    prev_h[...] = carry["h"]
    prev_p[...] = carry["p"]
    prev_cu[...] = carry["cu"]


def _prologue(x2, b, km, vm, p):
    t, d = x2.shape
    s = t // b
    tm = TM_PROLOGUE
    db = D_BRANCH
    m = km.shape[1]
    steps = s // tm
    tok = lambda c: pl.BlockSpec((tm, c), lambda bi, i: (bi * steps + i, 0))
    consts = [p["g_mix"], p["w_in"], p["conv_w"], p["mu_rkv"], p["mu_wag"],
              p["w_lora1"], p["w_lora2"], p["w0"], p["a_lora1"], p["a_lora2"], p["a0"],
              p["g_lora1"], p["g_lora2"], p["k_k"], p["k_a"], p["seg_ones"]]
    out_shapes = ([jax.ShapeDtypeStruct((t, db), BF16)] * 2
                  + [jax.ShapeDtypeStruct((t, db), F32)] * 7)
    return pl.pallas_call(
        _prologue_kernel,
        grid=(b, steps),
        in_specs=[tok(d)] + [_const_spec(c.shape) for c in consts]
                 + [pl.BlockSpec((1, m, db), lambda bi, i: (bi, 0, 0))] * 2,
        out_specs=[tok(db)] * 9,
        out_shape=out_shapes,
        scratch_shapes=[pltpu.VMEM((1, d), F32), pltpu.VMEM((1, 3 * db), F32),
                        pltpu.VMEM((2, db), F32)],
        compiler_params=pltpu.CompilerParams(dimension_semantics=("arbitrary", "arbitrary"),
                                             vmem_limit_bytes=VMEM_LIMIT),
        name="prologue",
    )(x2, *consts, km, vm)


def _rwkv_kernel(r_ref, k_ref, v_ref, kk_ref, a_ref, lw_ref, g_ref, rk_ref, lnw_ref, lnb_ref,
                 tri_ref, seg_ref, out_ref, h_scr, y_scr):
    tb = r_ref.shape[0]
    n = HEAD_DIM
    c_len = CHUNK

    @pl.when(pl.program_id(1) == 0)
    def _():
        h_scr[...] = jnp.zeros_like(h_scr)

    pw = 2 * n
    row1 = lax.broadcasted_iota(jnp.int32, (c_len, pw), 0)
    lane1 = lax.broadcasted_iota(jnp.int32, (c_len, pw), 1)
    col1 = lane1 & (n - 1)
    left = lane1 < n
    strict1 = col1 < row1
    incl1 = col1 <= row1
    eye2 = (col1 == row1).astype(F32)
    zeros_pair = jnp.zeros((c_len, pw), F32)
    zeros_bd = jnp.zeros((2 * c_len, pw), F32)

    def block_diag(y):
        return jnp.concatenate([jnp.where(left, y, 0.0), jnp.where(left, 0.0, y)], axis=0)

    def pair_transpose(y):
        zt = block_diag(y).T
        return zt[:c_len] + zt[c_len:]

    def chunk_inputs(c):
        rows = pl.ds(pl.multiple_of(c * c_len, c_len), c_len)
        r = r_ref[rows, :]
        k = k_ref[rows, :]
        v = v_ref[rows, :]
        kk = kk_ref[rows, :]
        a = a_ref[rows, :]
        lw = lw_ref[rows, :]
        gcum = _split_dot_left(tri_ref[...], lw, 2)
        e_pos = jnp.exp(gcum)
        e_neg = jnp.exp(-gcum)
        p_last = jnp.exp(gcum[c_len - 1:c_len, :])
        bb = kk * a * e_neg
        kb = k * e_neg
        return dict(rows=rows, v=v, p_last=p_last, rb=r * e_pos, ab=-kk * jnp.exp(gcum - lw), bb=bb, kb=kb,
                    bbp=bb * p_last, kbp=kb * p_last)

    def chunk_group(it, carry):
        chunks = [chunk_inputs(it * CHUNKS_PER_ITER + ci) for ci in range(CHUNKS_PER_ITER)]
        n_pairs = N_HEADS // 2
        units = [(ci, pr) for ci in range(CHUNKS_PER_ITER) for pr in range(n_pairs)]
        nu = range(len(units))
        ls = [slice(pr * pw, (pr + 1) * pw) for _, pr in units]
        ch = [chunks[ci] for ci, _ in units]
        al = [ch[u]["ab"][:, ls[u]] for u in nu]
        rr = [ch[u]["rb"][:, ls[u]] for u in nu]
        v_bd = [block_diag(ch[u]["v"][:, ls[u]]) for u in nu]
        aa = [_bdot_nt(jnp.concatenate([al[u], rr[u]], axis=0),
                       jnp.concatenate([block_diag(ch[u]["bb"][:, ls[u]]), block_diag(ch[u]["kb"][:, ls[u]])],
                                       axis=0)) for u in nu]
        a_ab = [jnp.where(strict1, aa[u][:c_len, :pw], 0.0) for u in nu]
        a_ak = [jnp.where(strict1, aa[u][:c_len, pw:], 0.0) for u in nu]
        a_rb = [jnp.where(incl1, aa[u][c_len:, :pw], 0.0) for u in nu]
        a_rk = [jnp.where(incl1, aa[u][c_len:, pw:], 0.0) for u in nu]
        av = [_bdot(a_ak[u], v_bd[u]) for u in nu]
        t_inv = [eye2 + a_ab[u] for u in nu]
        x_pow = [_bdot(a_ab[u], block_diag(a_ab[u])) for u in nu]
        for lvl in range(5):
            if lvl < 4:
                z = [_bdot(jnp.concatenate([t_inv[u], x_pow[u]], axis=0), block_diag(x_pow[u])) for u in nu]
                t_inv = [t_inv[u] + z[u][:c_len] for u in nu]
                x_pow = [z[u][c_len:] for u in nu]
            else:
                t_inv = [t_inv[u] + _bdot(t_inv[u], block_diag(x_pow[u])) for u in nu]
        w12 = [_bdot(t_inv[u], jnp.concatenate([block_diag(al[u]), block_diag(av[u])], axis=1))
               for u in nu]
        z2 = []
        for u in nu:
            rhs2 = jnp.concatenate(
                [jnp.concatenate([block_diag(w12[u][:, :pw]), block_diag(w12[u][:, pw:])], axis=1),
                 jnp.concatenate([zeros_bd, v_bd[u]], axis=1)], axis=0)
            lhs3 = jnp.concatenate(
                [jnp.concatenate([pair_transpose(ch[u]["bbp"][:, ls[u]]),
                                  pair_transpose(ch[u]["kbp"][:, ls[u]])], axis=1),
                 jnp.concatenate([a_rb[u], a_rk[u]], axis=1)], axis=0)
            z2.append(_bdot(lhs3, rhs2))
        state = [h_scr[pr] for pr in range(n_pairs)]
        for u in nu:
            pr = units[u][1]
            mq = z2[u][:, :pw] + jnp.concatenate([zeros_pair, rr[u]], axis=0)
            out = _bdot(mq, block_diag(state[pr])) + z2[u][:, pw:]
            decay = eye2 * ch[u]["p_last"][:, ls[u]]
            p_mat = jnp.where(left, jnp.sum(jnp.where(left, decay, 0.0), axis=1, keepdims=True),
                              jnp.sum(jnp.where(left, 0.0, decay), axis=1, keepdims=True))
            state[pr] = p_mat * state[pr] + out[:c_len]
            y_scr[ch[u]["rows"], ls[u]] = out[c_len:]
        for pr in range(n_pairs):
            h_scr[pr] = state[pr]
        return carry

    lax.fori_loop(0, tb // (c_len * CHUNKS_PER_ITER), chunk_group, 0)

    y = y_scr[...]
    seg = seg_ref[...]
    inv_n = 1.0 / n
    mu = _head_sums(y, seg, 2) * inv_n
    yc = y - mu
    var = _head_sums(yc * yc, seg, 1) * inv_n
    yn = yc * lax.rsqrt(var + GN_EPS) * lnw_ref[...] + lnb_ref[...]
    bonus = _head_sums(r_ref[...] * k_ref[...] * rk_ref[...], seg, 1) * v_ref[...]
    out_ref[...] = ((yn + bonus) * g_ref[...]).astype(BF16)


def _rwkv(r, k, v, kkn, a, lw, g, b, p):
    t, db = r.shape
    tb = TB_RWKV
    steps = t // b // tb
    tok = pl.BlockSpec((tb, db), lambda bi, i: (bi * steps + i, 0))
    consts = [p["r_k"], p["ln_x_w"], p["ln_x_b"], p["tri"], p["seg_ones"]]
    return pl.pallas_call(
        _rwkv_kernel,
        grid=(b, steps),
        in_specs=[tok] * 7 + [_const_spec(c.shape) for c in consts],
        out_specs=tok,
        out_shape=jax.ShapeDtypeStruct((t, db), BF16),
        scratch_shapes=[pltpu.VMEM((N_HEADS // 2, HEAD_DIM, 2 * HEAD_DIM), F32),
                        pltpu.VMEM((tb, db), F32)],
        compiler_params=pltpu.CompilerParams(dimension_semantics=("arbitrary", "arbitrary"),
                                             vmem_limit_bytes=VMEM_LIMIT),
        name="rwkv",
    )(r, k, v, kkn, a, lw, g, *consts)


def _merge_kernel(x_ref, yc_ref, yr_ref, ym_ref, gmix_ref, wgate_ref, bgate_ref, wbr_ref, wo_ref,
                  gffn_ref, wrt_ref, brt_ref, tril_ref,
                  x1_ref, meta_ref, metat_ref, cnt_ref, base_scr):
    tm, d = x_ref.shape

    @pl.when(pl.program_id(0) == 0)
    def _():
        base_scr[...] = jnp.zeros_like(base_scr)

    sub = tril_ref.shape[0]
    lane = lax.broadcasted_iota(jnp.int32, (sub, LANES), 1)
    neg = jnp.float32(-jnp.inf)
    big = jnp.int32(1 << 20)
    w_hi, w_lo = _split_terms(wrt_ref[...], 2)
    w_hi_lo = jnp.concatenate([w_hi, w_lo], axis=1)
    state = {"base": base_scr[...]}
    merged = {}

    def project(j):
        rs = slice(j * sub, (j + 1) * sub)
        x = x_ref[rs, :]
        hb = _rms(x, gmix_ref[...]).astype(BF16)
        z = jnp.zeros((sub, d), F32)
        for i, y_ref in enumerate((yc_ref, yr_ref, ym_ref)):
            cs = slice(i * d, (i + 1) * d)
            gate = _sigmoid(jnp.dot(hb, wgate_ref[:, cs], preferred_element_type=F32) + bgate_ref[:, cs])
            z = z + gate * jnp.dot(y_ref[rs, :], wbr_ref[i], preferred_element_type=F32)
            yield
        x1 = x + _bdot(z, wo_ref[...])
        x1_ref[rs, :] = x1
        merged[j] = x1

    def route(j):
        rs = slice(j * sub, (j + 1) * sub)
        h2 = _rms(merged.pop(j), gffn_ref[...])
        h_hi, h_lo = _split_terms(h2, 2)
        hi_terms = jnp.dot(h_hi, w_hi_lo, preferred_element_type=F32)
        logits = (hi_terms[:, :LANES]
                  + (jnp.dot(h_lo, w_hi, preferred_element_type=F32) + hi_terms[:, LANES:])) + brt_ref[...]
        yield
        gmask = (lane >= N_EXPERTS) & (lane < N_EXPERTS + N_GROUPS)
        glv = jnp.where(gmask, logits, neg)
        gmax = jnp.max(glv, axis=-1, keepdims=True)
        g_sel = jnp.min(jnp.where(glv == gmax, lane - N_EXPERTS, big), axis=-1, keepdims=True)
        g_w = 1.0 / jnp.sum(jnp.exp(glv - gmax), axis=-1, keepdims=True)
        emask = (lane < N_EXPERTS) & ((lane >> 3) == g_sel)
        elv = jnp.where(emask, logits, neg)
        emax = jnp.max(elv, axis=-1, keepdims=True)
        esum = jnp.sum(jnp.exp(elv - emax), axis=-1, keepdims=True)
        i1 = jnp.min(jnp.where(elv == emax, lane, big), axis=-1, keepdims=True)
        elv2 = jnp.where(lane == i1, neg, elv)
        m2 = jnp.max(elv2, axis=-1, keepdims=True)
        i2 = jnp.min(jnp.where(elv2 == m2, lane, big), axis=-1, keepdims=True)
        p1 = 1.0 / esum
        p2 = jnp.exp(m2 - emax) / esum
        c1 = g_w * p1 / (p1 + p2)
        c2 = g_w * p2 / (p1 + p2)

        oh1 = lane == i1
        oh2 = lane == i2
        onehot = jnp.where(oh1 | oh2, 1.0, 0.0)
        before = jnp.dot(tril_ref[...], onehot.astype(BF16), preferred_element_type=F32) + state["base"]
        yield
        rank1 = jnp.sum(jnp.where(oh1, before, 0.0), axis=-1, keepdims=True)
        rank2 = jnp.sum(jnp.where(oh2, before, 0.0), axis=-1, keepdims=True)
        state["base"] = state["base"] + jnp.sum(onehot, axis=0, keepdims=True)

        meta = jnp.where(lane == 0, i1.astype(F32),
               jnp.where(lane == 1, i2.astype(F32),
               jnp.where(lane == 2, rank1,
               jnp.where(lane == 3, rank2,
               jnp.where(lane == 4, c1,
               jnp.where(lane == 5, c2, 0.0))))))
        meta_ref[rs, :] = meta[:, :META_COLS]
        metat_ref[:, rs] = meta.T[:META_COLS, :]

    _software_pipeline(project, route, tm // sub)
    base_scr[...] = state["base"]
    cnt_ref[...] = jnp.broadcast_to(state["base"], cnt_ref.shape)


def _merge(x2, yc, yr, ym, p):
    t, d = x2.shape
    tm = TM_MERGE
    db = D_BRANCH
    tok = lambda c: pl.BlockSpec((tm, c), lambda i: (i, 0))
    consts = [p["g_mix"], p["w_gate"], p["b_gate"], p["w_branch"], p["w_o"], p["g_ffn"],
              p["w_router"], p["b_router"], p["tril_strict"]]
    return pl.pallas_call(
        _merge_kernel,
        grid=(t // tm,),
        in_specs=[tok(d), tok(db), tok(db), tok(db)] + [_const_spec(c.shape) for c in consts],
        out_specs=[tok(d), tok(META_COLS), pl.BlockSpec((META_COLS, tm), lambda i: (0, i)),
                   _const_spec((8, LANES))],
        out_shape=[jax.ShapeDtypeStruct((t, d), F32), jax.ShapeDtypeStruct((t, META_COLS), F32),
                   jax.ShapeDtypeStruct((META_COLS, t), F32), jax.ShapeDtypeStruct((8, LANES), F32)],
        scratch_shapes=[pltpu.VMEM((1, LANES), F32)],
        compiler_params=pltpu.CompilerParams(dimension_semantics=("arbitrary",),
                                             vmem_limit_bytes=VMEM_LIMIT),
        name="merge",
    )(x2, yc, yr, ym, *consts)


def _store_packed_rows(ref2d, x, stage):
    rows, d = x.shape
    nt = d // (2 * LANES)
    for c in range(nt):
        stage[c, pl.ds(0, rows, stride=2), :] = x[:, c * LANES:(c + 1) * LANES]
        stage[c, pl.ds(1, rows, stride=2), :] = x[:, (c + nt) * LANES:(c + nt + 1) * LANES]
        ref2d[pl.ds(c, rows, stride=nt), :] = pltpu.bitcast(stage[c].astype(BF16), jnp.uint32)


def _load_packed_rows(ref2d, rows, nt, stage):
    lo, hi = [], []
    for c in range(nt):
        stage[c] = pltpu.bitcast(ref2d[pl.ds(c, rows, stride=nt), :], BF16).astype(F32)
        lo.append(stage[c, pl.ds(0, rows, stride=2), :])
        hi.append(stage[c, pl.ds(1, rows, stride=2), :])
    return jnp.concatenate(lo + hi, axis=1)


def _scatter_kernel(dest0_ref, dest1_ref, x1_ref, gffn_ref, xs_ref, hbuf, stage, sem):
    dest_refs = (dest0_ref, dest1_ref)
    ts, d_model = x1_ref.shape
    nt = d_model // (2 * LANES)
    s = pl.program_id(0)
    slot = s % 2

    def wait_slot(sl):
        for _ in range(TOP_K):
            pltpu.make_async_copy(hbuf.at[sl], xs_ref.at[pl.ds(0, ts * nt), :], sem.at[sl]).wait()

    @pl.when(s >= 2)
    def _():
        wait_slot(slot)

    _store_packed_rows(hbuf.at[slot], _rms(x1_ref[...], gffn_ref[...]), stage)

    def issue(grp, carry):
        grp_off = pl.multiple_of(grp * (SUBLANES * nt), SUBLANES * nt)
        for j in range(SUBLANES):
            tok = s * ts + grp * SUBLANES + j
            for kslot in range(TOP_K):
                d = pl.multiple_of(dest_refs[kslot][tok], nt)
                pltpu.make_async_copy(hbuf.at[slot, pl.ds(grp_off + j * nt, nt), :],
                                      xs_ref.at[pl.ds(d, nt), :], sem.at[slot]).start(priority=kslot)
        return carry

    lax.fori_loop(0, ts // SUBLANES, issue, 0, unroll=ISSUE_UNROLL)

    @pl.when(s == pl.num_programs(0) - 1)
    def _():
        @pl.when(s >= 1)
        def _():
            wait_slot(1 - slot)
        wait_slot(slot)


def _scatter(dests, x1, g_ffn, n_rows):
    t, d = x1.shape
    ts = TS_SCATTER
    pt = d // (2 * LANES)
    return pl.pallas_call(
        _scatter_kernel,
        grid_spec=pltpu.PrefetchScalarGridSpec(
            num_scalar_prefetch=TOP_K,
            grid=(t // ts,),
            in_specs=[pl.BlockSpec((ts, d), lambda i, *_: (i, 0)),
                      pl.BlockSpec((1, d), lambda i, *_: (0, 0))],
            out_specs=pl.BlockSpec(memory_space=pl.ANY),
            scratch_shapes=[pltpu.VMEM((2, ts * pt, LANES), jnp.uint32),
                            pltpu.VMEM((pt, 2 * ts, LANES), F32),
                            pltpu.SemaphoreType.DMA((2,))],
        ),
        out_shape=jax.ShapeDtypeStruct((n_rows * pt, LANES), jnp.uint32),
        compiler_params=pltpu.CompilerParams(dimension_semantics=("arbitrary",),
                                             vmem_limit_bytes=VMEM_LIMIT),
        name="scatter",
    )(*dests, x1, g_ffn)


def _experts_kernel(be_ref, nused_ref, nexte_ref, wslot_ref, xs_ref, wg_hbm, wu_hbm, wd_hbm, ys_ref,
                    wg_f, wu_f, wd_f, wg_s, wu_s, wd_s, xbuf, ybuf, xstage, ystage, sem, xsem, ysem):
    i = pl.program_id(0)
    e = be_ref[i]
    prev = be_ref[jnp.maximum(i - 1, 0)]
    active = i < nused_ref[0]

    def weight_copies(ex):
        ws = wslot_ref[ex]
        return (pltpu.make_async_copy(wg_hbm.at[ex], wg_f.at[ws], sem.at[ws, 0]),
                pltpu.make_async_copy(wu_hbm.at[ex], wu_f.at[ws], sem.at[ws, 1]),
                pltpu.make_async_copy(wd_hbm.at[ex], wd_f.at[ws], sem.at[ws, 2]))

    def start_weights(ex):
        @pl.when(ex < N_EXPERTS)
        def _():
            for cp in weight_copies(ex):
                cp.start(priority=1)

    @pl.when(i == 0)
    def _():
        start_weights(e)
        start_weights(nexte_ref[e])

    @pl.when(active & ((i == 0) | (e != prev)))
    def _():
        for cp in weight_copies(e):
            cp.wait()
        ws = wslot_ref[e]
        wg_s[...] = wg_f[ws].astype(BF16)
        wu_s[...] = wu_f[ws].astype(BF16)
        wd_s[...] = wd_f[ws].astype(BF16)
        nxt = nexte_ref[e]
        start_weights(jnp.where(nxt < N_EXPERTS, nexte_ref[jnp.minimum(nxt, N_EXPERTS - 1)], N_EXPERTS))

    blk_rows = xbuf.shape[1]
    n_used = nused_ref[0]

    def xs_copy(blk, slot):
        return pltpu.make_async_copy(xs_ref.at[pl.ds(pl.multiple_of(blk * blk_rows, blk_rows), blk_rows), :],
                                     xbuf.at[slot], xsem.at[slot])

    def ys_copy(blk, slot):
        return pltpu.make_async_copy(ybuf.at[slot],
                                     ys_ref.at[pl.ds(pl.multiple_of(blk * blk_rows, blk_rows), blk_rows), :],
                                     ysem.at[slot])

    @pl.when(i == 0)
    def _():
        for ahead in range(XS_BUFFERS - 1):
            @pl.when(ahead < n_used)
            def _():
                xs_copy(ahead, ahead).start()

    @pl.when(active)
    def _():
        nt = wg_s.shape[0] // (2 * LANES)
        fetch = i + (XS_BUFFERS - 1)

        @pl.when(fetch < n_used)
        def _():
            xs_copy(fetch, fetch % XS_BUFFERS).start()

        xs_copy(i, i % XS_BUFFERS).wait()
        oslot = i % 2

        @pl.when(i >= 2)
        def _():
            ys_copy(i - 2, oslot).wait()

        xb = _load_packed_rows(xbuf.at[i % XS_BUFFERS], ROW_BLOCK, nt, xstage).astype(BF16)
        gate = jnp.dot(xb, wg_s[...], preferred_element_type=F32)
        up = jnp.dot(xb, wu_s[...], preferred_element_type=F32)
        hid = gate * _sigmoid(gate) * up
        _store_packed_rows(ybuf.at[oslot], jnp.dot(hid.astype(BF16), wd_s[...], preferred_element_type=F32),
                           ystage)
        ys_copy(i, oslot).start()

        @pl.when(i == n_used - 1)
        def _():
            @pl.when(i >= 1)
            def _():
                ys_copy(i - 1, 1 - oslot).wait()
            ys_copy(i, oslot).wait()


def _experts(blk_expert, n_used, next_expert, weight_slot, xs, w_gate, w_up, w_down):
    d, de = w_gate.shape[-2:]
    pt = d // (2 * LANES)
    blk_rows = ROW_BLOCK * pt
    nb = xs.shape[0] // blk_rows

    return pl.pallas_call(
        _experts_kernel,
        grid_spec=pltpu.PrefetchScalarGridSpec(
            num_scalar_prefetch=4,
            grid=(nb,),
            in_specs=[pl.BlockSpec(memory_space=pl.ANY)] * 4,
            out_specs=pl.BlockSpec(memory_space=pl.ANY),
            scratch_shapes=[pltpu.VMEM((2, d, de), F32), pltpu.VMEM((2, d, de), F32),
                            pltpu.VMEM((2, de, d), F32),
                            pltpu.VMEM((d, de), BF16), pltpu.VMEM((d, de), BF16), pltpu.VMEM((de, d), BF16),
                            pltpu.VMEM((XS_BUFFERS, blk_rows, LANES), jnp.uint32),
                            pltpu.VMEM((2, blk_rows, LANES), jnp.uint32),
                            pltpu.VMEM((pt, 2 * ROW_BLOCK, LANES), F32),
                            pltpu.VMEM((pt, 2 * ROW_BLOCK, LANES), F32),
                            pltpu.SemaphoreType.DMA((2, 3)), pltpu.SemaphoreType.DMA((XS_BUFFERS,)),
                            pltpu.SemaphoreType.DMA((2,))],
        ),
        out_shape=jax.ShapeDtypeStruct(xs.shape, jnp.uint32),
        compiler_params=pltpu.CompilerParams(dimension_semantics=("arbitrary",),
                                             vmem_limit_bytes=VMEM_LIMIT),
        name="experts",
    )(blk_expert, n_used, next_expert, weight_slot, xs, w_gate, w_up, w_down)


def _combine_kernel(dest0_ref, dest1_ref, x1_ref, meta_ref, gfin_ref, ys_ref, out_ref, ybuf, stage, sem):
    dest_refs = (dest0_ref, dest1_ref)
    te = x1_ref.shape[0]
    s = pl.program_id(0)
    nsteps = pl.num_programs(0)
    slot = s % 2

    nt = x1_ref.shape[1] // (2 * LANES)

    def issue_step(step, sl):
        def issue(grp, carry):
            grp_off = pl.multiple_of(grp * (SUBLANES * nt), SUBLANES * nt)
            for j in range(SUBLANES):
                tok = step * te + grp * SUBLANES + j
                for kslot in range(TOP_K):
                    d = pl.multiple_of(dest_refs[kslot][tok], nt)
                    pltpu.make_async_copy(ys_ref.at[pl.ds(d, nt), :],
                                          ybuf.at[sl, kslot, pl.ds(grp_off + j * nt, nt), :],
                                          sem.at[sl]).start(priority=kslot)
            return carry
        lax.fori_loop(0, te // SUBLANES, issue, 0, unroll=ISSUE_UNROLL)

    @pl.when(s == 0)
    def _():
        issue_step(0, 0)

    @pl.when(s + 1 < nsteps)
    def _():
        issue_step(s + 1, 1 - slot)

    for kslot in range(TOP_K):
        pltpu.make_async_copy(ys_ref.at[pl.ds(0, te * nt), :], ybuf.at[slot, kslot], sem.at[slot]).wait()

    meta = meta_ref[...]
    y0 = _load_packed_rows(ybuf.at[slot, 0], te, nt, stage.at[0])
    y1 = _load_packed_rows(ybuf.at[slot, 1], te, nt, stage.at[1])
    x2 = x1_ref[...] + y0 * meta[:, 4:5] + y1 * meta[:, 5:6]
    out_ref[...] = _rms(x2, gfin_ref[...])


def _combine(dests, x1, meta, g_final, ys):
    t, d = x1.shape
    te = TE_COMBINE
    pt = d // (2 * LANES)
    return pl.pallas_call(
        _combine_kernel,
        grid_spec=pltpu.PrefetchScalarGridSpec(
            num_scalar_prefetch=TOP_K,
            grid=(t // te,),
            in_specs=[pl.BlockSpec((te, d), lambda i, *_: (i, 0)),
                      pl.BlockSpec((te, META_COLS), lambda i, *_: (i, 0)),
                      pl.BlockSpec((1, d), lambda i, *_: (0, 0)),
                      pl.BlockSpec(memory_space=pl.ANY)],
            out_specs=pl.BlockSpec((te, d), lambda i, *_: (i, 0)),
            scratch_shapes=[pltpu.VMEM((2, TOP_K, te * pt, LANES), jnp.uint32),
                            pltpu.VMEM((TOP_K, pt, 2 * te, LANES), F32),
                            pltpu.SemaphoreType.DMA((2,))],
        ),
        out_shape=jax.ShapeDtypeStruct((t, d), F32),
        compiler_params=pltpu.CompilerParams(dimension_semantics=("arbitrary",),
                                             vmem_limit_bytes=VMEM_LIMIT),
        name="combine",
    )(*dests, x1, meta, g_final, ys)


def _constants(tm_merge):
    n = CHUNK
    tri = (jnp.arange(n)[:, None] >= jnp.arange(n)[None, :]).astype(BF16)
    head = jnp.arange(2 * LANES) // HEAD_DIM
    seg_ones = (head[:, None] == head[None, :]).astype(BF16)
    tril_strict = (jnp.arange(tm_merge)[:, None] > jnp.arange(tm_merge)[None, :]).astype(BF16)
    return tri, seg_ones, tril_strict


def kernel(x, mem, g_mix, g_mem, w_in, conv_w, mu_rkv, mu_wag, w_lora1, w_lora2, w0, a_lora1, a_lora2, a0, g_lora1, g_lora2, k_k, k_a, r_k, ln_x_w, ln_x_b, w_kv_mem, w_branch, w_gate, b_gate, w_o, g_ffn, w_router_group, b_router_group, w_router_expert, b_router_expert, w_exp_gate, w_exp_up, w_exp_down, g_final):
    assert g_mix.shape[0] == 1, "single-layer block"
    b, s, d = x.shape
    t = b * s
    db = D_BRANCH
    tri, seg_ones, tril_strict = _constants(MERGE_SUB)
    row = lambda a: a.reshape(1, -1)
    pad_r = LANES - N_EXPERTS - N_GROUPS
    p = {
        "g_mix": row(g_mix[0]), "w_in": w_in[0].astype(BF16), "conv_w": conv_w[0].T,
        "mu_rkv": row(mu_rkv[0]), "mu_wag": mu_wag[0],
        "w_lora1": w_lora1[0].astype(BF16), "w_lora2": w_lora2[0].astype(BF16), "w0": row(w0[0]),
        "a_lora1": a_lora1[0].astype(BF16), "a_lora2": a_lora2[0].astype(BF16), "a0": row(a0[0]),
        "g_lora1": g_lora1[0].astype(BF16), "g_lora2": g_lora2[0].astype(BF16),
        "k_k": row(k_k[0]), "k_a": row(k_a[0]), "r_k": row(r_k[0]),
        "ln_x_w": row(ln_x_w[0]), "ln_x_b": row(ln_x_b[0]),
        "w_gate": w_gate[0].astype(BF16), "b_gate": row(b_gate[0]),
        "w_branch": w_branch[0].astype(BF16), "w_o": w_o[0].astype(BF16), "g_ffn": row(g_ffn[0]),
        "w_router": jnp.concatenate([w_router_expert[0], w_router_group[0],
                                     jnp.zeros((d, pad_r), F32)], axis=1),
        "b_router": row(jnp.concatenate([b_router_expert[0], b_router_group[0],
                                         jnp.zeros((pad_r,), F32)])),
        "tri": tri, "seg_ones": seg_ones, "tril_strict": tril_strict,
    }

    km, vm = _memkv(mem, row(g_mem[0]), w_kv_mem[0].astype(BF16))
    x2 = x.reshape(t, d)
    yconv, ymem, r, k, v, kkn, a, lw, g = _prologue(x2, b, km, vm, p)
    yrwkv = _rwkv(r, k, v, kkn, a, lw, g, b, p)
    x1, meta, meta_t, cnt = _merge(x2, yconv, yrwkv, ymem, p)

    counts = cnt[0, :N_EXPERTS].astype(jnp.int32)
    padded = ((counts + ROW_BLOCK - 1) // ROW_BLOCK) * ROW_BLOCK
    pad_end = jnp.cumsum(padded)
    pad_start = pad_end - padded
    n_blocks = (t * TOP_K) // ROW_BLOCK + N_EXPERTS
    eids = jnp.arange(N_EXPERTS, dtype=jnp.int32)
    e_idx = meta_t[0:TOP_K].astype(jnp.int32)
    rank = meta_t[TOP_K:2 * TOP_K].astype(jnp.int32)
    start_of = jnp.sum(jnp.where(e_idx[:, None, :] == eids[None, :, None], pad_start[None, :, None], 0), axis=1)
    dest = (start_of + rank) * (d // (2 * LANES))
    dests = [dest[kslot] for kslot in range(TOP_K)]
    blk_start = jnp.arange(n_blocks, dtype=jnp.int32) * ROW_BLOCK
    blk_expert = jnp.minimum(jnp.sum((pad_end[None, :] <= blk_start[:, None]).astype(jnp.int32), axis=1),
                             N_EXPERTS - 1)
    n_used = (pad_end[-1:] // ROW_BLOCK).astype(jnp.int32)
    later_nonempty = (eids[None, :] > eids[:, None]) & (counts[None, :] > 0)
    next_expert = jnp.min(jnp.where(later_nonempty, eids[None, :], N_EXPERTS), axis=1)
    weight_slot = (jnp.cumsum((counts > 0).astype(jnp.int32)) - 1) & 1

    xs = _scatter(dests, x1, p["g_ffn"], n_blocks * ROW_BLOCK)
    ys = _experts(blk_expert, n_used, next_expert, weight_slot, xs,
                  w_exp_gate[0], w_exp_up[0], w_exp_down[0])
    out = _combine(dests, x1, meta, row(g_final), ys)
    return out.reshape(b, s, d)
```

```python
import functools

import jax
import jax.numpy as jnp
from jax import lax
from jax.experimental import pallas as pl
from jax.experimental.pallas import tpu as pltpu

F32 = jnp.float32
BF16 = jnp.bfloat16

NORM_EPS = 1e-6
GN_EPS = 64e-5
D_BRANCH = 512
HEAD_DIM = 64
N_HEADS = 8
CHUNK = 64
CHUNKS_PER_ITER = 4
MEM_HEADS = 4
MEM_HEAD_DIM = 128
N_GROUPS = 8
EXPERTS_PER_GROUP = 8
N_EXPERTS = 64
TOP_K = 2
ROW_BLOCK = 256
XS_BUFFERS = 4
LANES = 128
VMEM_LIMIT = 56 * 1024 * 1024

TM_PROLOGUE = 512
PROLOGUE_SUB = 256
TB_RWKV = 256
TM_MERGE = 1024
MERGE_SUB = 512
TS_SCATTER = 256
TE_COMBINE = 256
SUBLANES = 8
META_COLS = 8
ISSUE_UNROLL = 2


def _bdot(a, b):
    return jnp.dot(a.astype(BF16), b.astype(BF16), preferred_element_type=F32)


def _bdot_nt(a, b):
    return lax.dot_general(a.astype(BF16), b.astype(BF16), (((1,), (1,)), ((), ())),
                           preferred_element_type=F32)


def _split_terms(x, n_terms):
    terms = []
    for _ in range(n_terms):
        t = x.astype(BF16)
        terms.append(t)
        x = x - t.astype(F32)
    return terms


def _split_dot_left(m_bf16, x, n_terms):
    return sum(jnp.dot(m_bf16, t, preferred_element_type=F32) for t in _split_terms(x, n_terms))


def _head_sums(x, seg_bf16, n_terms):
    w = seg_bf16.shape[0]
    terms = _split_terms(x, n_terms)
    halves = [sum(jnp.dot(t[:, c:c + w], seg_bf16, preferred_element_type=F32) for t in terms)
              for c in range(0, x.shape[1], w)]
    return jnp.concatenate(halves, axis=1)


def _rms(x, g):
    return x * lax.rsqrt(jnp.mean(x * x, axis=-1, keepdims=True) + NORM_EPS) * g


def _sigmoid(x):
    return 1.0 / (1.0 + jnp.exp(-x))


def _run_together(*gens):
    live = list(gens)
    while live:
        for gen in list(live):
            try:
                next(gen)
            except StopIteration:
                live.remove(gen)


def _software_pipeline(heavy, light, n_sub):
    _run_together(heavy(0))
    for j in range(1, n_sub):
        _run_together(heavy(j), light(j - 1))
    _run_together(light(n_sub - 1))


def _const_spec(shape):
    n = len(shape)
    return pl.BlockSpec(shape, lambda *_: (0,) * n)


def _memkv_kernel(mem_ref, g_ref, w_ref, k_ref, v_ref):
    mn = _rms(mem_ref[0], g_ref[...])
    kv = _bdot(mn, w_ref[...])
    k_ref[0] = kv[:, :D_BRANCH].astype(BF16)
    v_ref[0] = kv[:, D_BRANCH:].astype(BF16)


def _memkv(mem, g_mem, w_kv):
    b, m, d = mem.shape
    return pl.pallas_call(
        _memkv_kernel,
        grid=(b,),
        in_specs=[pl.BlockSpec((1, m, d), lambda i: (i, 0, 0)),
                  _const_spec((1, d)), _const_spec((d, 2 * D_BRANCH))],
        out_specs=[pl.BlockSpec((1, m, D_BRANCH), lambda i: (i, 0, 0)),
                   pl.BlockSpec((1, m, D_BRANCH), lambda i: (i, 0, 0))],
        out_shape=[jax.ShapeDtypeStruct((b, m, D_BRANCH), BF16)] * 2,
        compiler_params=pltpu.CompilerParams(dimension_semantics=("arbitrary",),
                                             vmem_limit_bytes=VMEM_LIMIT),
        name="memkv",
    )(mem, g_mem, w_kv)


def _prologue_kernel(x_ref, gmix_ref, win_ref, convw_ref, murkv_ref, muwag_ref,
                     wl1_ref, wl2_ref, w0_ref, al1_ref, al2_ref, a0_ref, gl1_ref, gl2_ref,
                     kk_ref, ka_ref, seg_ref, km_ref, vm_ref,
                     yconv_ref, ymem_ref, r_ref, k_ref, v_ref, kkn_ref, a_ref, lw_ref, g_ref,
                     prev_h, prev_p, prev_cu):
    tm = x_ref.shape[0]
    db = D_BRANCH
    sub = PROLOGUE_SUB
    n_parts = win_ref.shape[1] // db

    @pl.when(pl.program_id(1) == 0)
    def _():
        prev_h[...] = jnp.zeros_like(prev_h)
        prev_p[...] = jnp.zeros_like(prev_p)
        prev_cu[...] = jnp.zeros_like(prev_cu)

    rows = lax.broadcasted_iota(jnp.int32, (sub, 1), 0)

    def shift1(u, prev_row):
        return jnp.where(rows == 0, prev_row, pltpu.roll(u, 1, axis=0))

    carry = {"h": prev_h[...], "p": prev_p[...], "cu": prev_cu[...]}
    projected = {}

    def project(j):
        h = _rms(x_ref[j * sub:(j + 1) * sub, :], gmix_ref[...])
        hb = h.astype(BF16)
        parts = []
        for c in range(n_parts):
            parts.append(jnp.dot(hb, win_ref[:, c * db:(c + 1) * db], preferred_element_type=F32))
            yield
        projected[j] = (h, parts)

    def mix(j):
        rs = slice(j * sub, (j + 1) * sub)
        h, (bg, cg, u, rp, kp, vp, q) = projected.pop(j)

        cu = cg * u
        pcu = carry["cu"]
        cu1 = shift1(cu, pcu[1:2, :])
        cu2 = jnp.where(rows == 0, pcu[0:1, :], jnp.where(rows == 1, pcu[1:2, :], pltpu.roll(cu, 2, axis=0)))
        conv = cu2 * convw_ref[0:1, :] + cu1 * convw_ref[1:2, :] + cu * convw_ref[2:3, :]
        yconv_ref[rs, :] = (bg * conv).astype(BF16)
        carry["cu"] = cu[sub - 2:sub, :]

        pr = jnp.concatenate([rp, kp, vp], axis=1)
        prs = shift1(pr, carry["p"])
        mixed = pr + (prs - pr) * murkv_ref[...]
        carry["p"] = pr[sub - 1:sub, :]
        r, k, v = mixed[:, :db], mixed[:, db:2 * db], mixed[:, 2 * db:]
        r_ref[rs, :] = r
        v_ref[rs, :] = v

        dh = shift1(h, carry["h"]) - h
        carry["h"] = h[sub - 1:sub, :]
        lora_w = _bdot(h + dh * muwag_ref[0:1, :], wl1_ref[...])
        lora_a = _bdot(h + dh * muwag_ref[1:2, :], al1_ref[...])
        lora_g = _bdot(h + dh * muwag_ref[2:3, :], gl1_ref[...])
        yield
        zz = w0_ref[...] + _bdot(jnp.tanh(lora_w), wl2_ref[...])
        a_lin = a0_ref[...] + _bdot(lora_a, al2_ref[...])
        g_ref[rs, :] = _bdot(_sigmoid(lora_g), gl2_ref[...])
        yield
        softplus = jnp.maximum(-zz, 0.0) + jnp.log(1.0 + jnp.exp(-jnp.abs(zz)))
        lw_ref[rs, :] = -jnp.exp(-softplus - 0.5)
        a = _sigmoid(a_lin)
        a_ref[rs, :] = a
        k_ref[rs, :] = k * (1.0 + (a - 1.0) * ka_ref[...])
        kk = k * kk_ref[...]
        ss = _head_sums(kk * kk, seg_ref[...], 1)
        yield
        kkn_ref[rs, :] = kk * lax.rsqrt(jnp.maximum(ss, 1e-24))

        scale = MEM_HEAD_DIM ** -0.5
        heads = [slice(hh * MEM_HEAD_DIM, (hh + 1) * MEM_HEAD_DIM) for hh in range(MEM_HEADS)]
        scores = [_bdot_nt(q[:, sl], km_ref[0, :, sl]) * scale for sl in heads]
        yield
        for sl, s in zip(heads, scores):
            p = jnp.exp(s - jnp.max(s, axis=-1, keepdims=True))
            o = _bdot(p, vm_ref[0, :, sl]) / jnp.sum(p, axis=-1, keepdims=True)
            ymem_ref[rs, sl] = o.astype(BF16)

    _software_pipeline(project, mix, tm // sub)
    prev_h[...] = carry["h"]
    prev_p[...] = carry["p"]
    prev_cu[...] = carry["cu"]


def _prologue(x2, b, km, vm, p):
    t, d = x2.shape
    s = t // b
    tm = TM_PROLOGUE
    db = D_BRANCH
    m = km.shape[1]
    steps = s // tm
    tok = lambda c: pl.BlockSpec((tm, c), lambda bi, i: (bi * steps + i, 0))
    consts = [p["g_mix"], p["w_in"], p["conv_w"], p["mu_rkv"], p["mu_wag"],
              p["w_lora1"], p["w_lora2"], p["w0"], p["a_lora1"], p["a_lora2"], p["a0"],
              p["g_lora1"], p["g_lora2"], p["k_k"], p["k_a"], p["seg_ones"]]
    out_shapes = ([jax.ShapeDtypeStruct((t, db), BF16)] * 2
                  + [jax.ShapeDtypeStruct((t, db), F32)] * 7)
    return pl.pallas_call(
        _prologue_kernel,
        grid=(b, steps),
        in_specs=[tok(d)] + [_const_spec(c.shape) for c in consts]
                 + [pl.BlockSpec((1, m, db), lambda bi, i: (bi, 0, 0))] * 2,
        out_specs=[tok(db)] * 9,
        out_shape=out_shapes,
        scratch_shapes=[pltpu.VMEM((1, d), F32), pltpu.VMEM((1, 3 * db), F32),
                        pltpu.VMEM((2, db), F32)],
        compiler_params=pltpu.CompilerParams(dimension_semantics=("arbitrary", "arbitrary"),
                                             vmem_limit_bytes=VMEM_LIMIT),
        name="prologue",
    )(x2, *consts, km, vm)


def _rwkv_kernel(r_ref, k_ref, v_ref, kk_ref, a_ref, lw_ref, g_ref, rk_ref, lnw_ref, lnb_ref,
                 tri_ref, seg_ref, out_ref, h_scr, y_scr):
    tb = r_ref.shape[0]
    n = HEAD_DIM
    c_len = CHUNK

    @pl.when(pl.program_id(1) == 0)
    def _():
        h_scr[...] = jnp.zeros_like(h_scr)

    pw = 2 * n
    row1 = lax.broadcasted_iota(jnp.int32, (c_len, pw), 0)
    lane1 = lax.broadcasted_iota(jnp.int32, (c_len, pw), 1)
    col1 = lane1 & (n - 1)
    left = lane1 < n
    strict1 = col1 < row1
    incl1 = col1 <= row1
    eye2 = (col1 == row1).astype(F32)
    zeros_pair = jnp.zeros((c_len, pw), F32)
    zeros_bd = jnp.zeros((2 * c_len, pw), F32)

    def block_diag(y):
        return jnp.concatenate([jnp.where(left, y, 0.0), jnp.where(left, 0.0, y)], axis=0)

    def pair_transpose(y):
        zt = block_diag(y).T
        return zt[:c_len] + zt[c_len:]

    def chunk_inputs(c):
        rows = pl.ds(pl.multiple_of(c * c_len, c_len), c_len)
        r = r_ref[rows, :]
        k = k_ref[rows, :]
        v = v_ref[rows, :]
        kk = kk_ref[rows, :]
        a = a_ref[rows, :]
        lw = lw_ref[rows, :]
        gcum = _split_dot_left(tri_ref[...], lw, 2)
        e_pos = jnp.exp(gcum)
        e_neg = jnp.exp(-gcum)
        p_last = jnp.exp(gcum[c_len - 1:c_len, :])
        bb = kk * a * e_neg
        kb = k * e_neg
        return dict(rows=rows, v=v, p_last=p_last, rb=r * e_pos, ab=-kk * jnp.exp(gcum - lw), bb=bb, kb=kb,
                    bbp=bb * p_last, kbp=kb * p_last)

    def chunk_group(it, carry):
        chunks = [chunk_inputs(it * CHUNKS_PER_ITER + ci) for ci in range(CHUNKS_PER_ITER)]
        n_pairs = N_HEADS // 2
        units = [(ci, pr) for ci in range(CHUNKS_PER_ITER) for pr in range(n_pairs)]
        nu = range(len(units))
        ls = [slice(pr * pw, (pr + 1) * pw) for _, pr in units]
        ch = [chunks[ci] for ci, _ in units]
        al = [ch[u]["ab"][:, ls[u]] for u in nu]
        rr = [ch[u]["rb"][:, ls[u]] for u in nu]
        v_bd = [block_diag(ch[u]["v"][:, ls[u]]) for u in nu]
        aa = [_bdot_nt(jnp.concatenate([al[u], rr[u]], axis=0),
                       jnp.concatenate([block_diag(ch[u]["bb"][:, ls[u]]), block_diag(ch[u]["kb"][:, ls[u]])],
                                       axis=0)) for u in nu]
        a_ab = [jnp.where(strict1, aa[u][:c_len, :pw], 0.0) for u in nu]
        a_ak = [jnp.where(strict1, aa[u][:c_len, pw:], 0.0) for u in nu]
        a_rb = [jnp.where(incl1, aa[u][c_len:, :pw], 0.0) for u in nu]
        a_rk = [jnp.where(incl1, aa[u][c_len:, pw:], 0.0) for u in nu]
        av = [_bdot(a_ak[u], v_bd[u]) for u in nu]
        t_inv = [eye2 + a_ab[u] for u in nu]
        x_pow = [_bdot(a_ab[u], block_diag(a_ab[u])) for u in nu]
        for lvl in range(5):
            if lvl < 4:
                z = [_bdot(jnp.concatenate([t_inv[u], x_pow[u]], axis=0), block_diag(x_pow[u])) for u in nu]
                t_inv = [t_inv[u] + z[u][:c_len] for u in nu]
                x_pow = [z[u][c_len:] for u in nu]
            else:
                t_inv = [t_inv[u] + _bdot(t_inv[u], block_diag(x_pow[u])) for u in nu]
        w12 = [_bdot(t_inv[u], jnp.concatenate([block_diag(al[u]), block_diag(av[u])], axis=1))
               for u in nu]
        z2 = []
        for u in nu:
            rhs2 = jnp.concatenate(
                [jnp.concatenate([block_diag(w12[u][:, :pw]), block_diag(w12[u][:, pw:])], axis=1),
                 jnp.concatenate([zeros_bd, v_bd[u]], axis=1)], axis=0)
            lhs3 = jnp.concatenate(
                [jnp.concatenate([pair_transpose(ch[u]["bbp"][:, ls[u]]),
                                  pair_transpose(ch[u]["kbp"][:, ls[u]])], axis=1),
                 jnp.concatenate([a_rb[u], a_rk[u]], axis=1)], axis=0)
            z2.append(_bdot(lhs3, rhs2))
        state = [h_scr[pr] for pr in range(n_pairs)]
        for u in nu:
            pr = units[u][1]
            mq = z2[u][:, :pw] + jnp.concatenate([zeros_pair, rr[u]], axis=0)
            out = _bdot(mq, block_diag(state[pr])) + z2[u][:, pw:]
            decay = eye2 * ch[u]["p_last"][:, ls[u]]
            p_mat = jnp.where(left, jnp.sum(jnp.where(left, decay, 0.0), axis=1, keepdims=True),
                              jnp.sum(jnp.where(left, 0.0, decay), axis=1, keepdims=True))
            state[pr] = p_mat * state[pr] + out[:c_len]
            y_scr[ch[u]["rows"], ls[u]] = out[c_len:]
        for pr in range(n_pairs):
            h_scr[pr] = state[pr]
        return carry

    lax.fori_loop(0, tb // (c_len * CHUNKS_PER_ITER), chunk_group, 0)

    y = y_scr[...]
    seg = seg_ref[...]
    inv_n = 1.0 / n
    mu = _head_sums(y, seg, 2) * inv_n
    yc = y - mu
    var = _head_sums(yc * yc, seg, 1) * inv_n
    yn = yc * lax.rsqrt(var + GN_EPS) * lnw_ref[...] + lnb_ref[...]
    bonus = _head_sums(r_ref[...] * k_ref[...] * rk_ref[...], seg, 1) * v_ref[...]
    out_ref[...] = ((yn + bonus) * g_ref[...]).astype(BF16)


def _rwkv(r, k, v, kkn, a, lw, g, b, p):
    t, db = r.shape
    tb = TB_RWKV
    steps = t // b // tb
    tok = pl.BlockSpec((tb, db), lambda bi, i: (bi * steps + i, 0))
    consts = [p["r_k"], p["ln_x_w"], p["ln_x_b"], p["tri"], p["seg_ones"]]
    return pl.pallas_call(
        _rwkv_kernel,
        grid=(b, steps),
        in_specs=[tok] * 7 + [_const_spec(c.shape) for c in consts],
        out_specs=tok,
        out_shape=jax.ShapeDtypeStruct((t, db), BF16),
        scratch_shapes=[pltpu.VMEM((N_HEADS // 2, HEAD_DIM, 2 * HEAD_DIM), F32),
                        pltpu.VMEM((tb, db), F32)],
        compiler_params=pltpu.CompilerParams(dimension_semantics=("arbitrary", "arbitrary"),
                                             vmem_limit_bytes=VMEM_LIMIT),
        name="rwkv",
    )(r, k, v, kkn, a, lw, g, *consts)


def _merge_kernel(x_ref, yc_ref, yr_ref, ym_ref, gmix_ref, wgate_ref, bgate_ref, wbr_ref, wo_ref,
                  gffn_ref, wrt_ref, brt_ref, tril_ref,
                  x1_ref, meta_ref, metat_ref, cnt_ref, base_scr):
    tm, d = x_ref.shape

    @pl.when(pl.program_id(0) == 0)
    def _():
        base_scr[...] = jnp.zeros_like(base_scr)

    sub = tril_ref.shape[0]
    lane = lax.broadcasted_iota(jnp.int32, (sub, LANES), 1)
    neg = jnp.float32(-jnp.inf)
    big = jnp.int32(1 << 20)
    w_hi, w_lo = _split_terms(wrt_ref[...], 2)
    w_hi_lo = jnp.concatenate([w_hi, w_lo], axis=1)
    state = {"base": base_scr[...]}
    merged = {}

    def project(j):
        rs = slice(j * sub, (j + 1) * sub)
        x = x_ref[rs, :]
        hb = _rms(x, gmix_ref[...]).astype(BF16)
        z = jnp.zeros((sub, d), F32)
        for i, y_ref in enumerate((yc_ref, yr_ref, ym_ref)):
            cs = slice(i * d, (i + 1) * d)
            gate = _sigmoid(jnp.dot(hb, wgate_ref[:, cs], preferred_element_type=F32) + bgate_ref[:, cs])
            z = z + gate * jnp.dot(y_ref[rs, :], wbr_ref[i], preferred_element_type=F32)
            yield
        x1 = x + _bdot(z, wo_ref[...])
        x1_ref[rs, :] = x1
        merged[j] = x1

    def route(j):
        rs = slice(j * sub, (j + 1) * sub)
        h2 = _rms(merged.pop(j), gffn_ref[...])
        h_hi, h_lo = _split_terms(h2, 2)
        hi_terms = jnp.dot(h_hi, w_hi_lo, preferred_element_type=F32)
        logits = (hi_terms[:, :LANES]
                  + (jnp.dot(h_lo, w_hi, preferred_element_type=F32) + hi_terms[:, LANES:])) + brt_ref[...]
        yield
        gmask = (lane >= N_EXPERTS) & (lane < N_EXPERTS + N_GROUPS)
        glv = jnp.where(gmask, logits, neg)
        gmax = jnp.max(glv, axis=-1, keepdims=True)
        g_sel = jnp.min(jnp.where(glv == gmax, lane - N_EXPERTS, big), axis=-1, keepdims=True)
        g_w = 1.0 / jnp.sum(jnp.exp(glv - gmax), axis=-1, keepdims=True)
        emask = (lane < N_EXPERTS) & ((lane >> 3) == g_sel)
        elv = jnp.where(emask, logits, neg)
        emax = jnp.max(elv, axis=-1, keepdims=True)
        esum = jnp.sum(jnp.exp(elv - emax), axis=-1, keepdims=True)
        i1 = jnp.min(jnp.where(elv == emax, lane, big), axis=-1, keepdims=True)
        elv2 = jnp.where(lane == i1, neg, elv)
        m2 = jnp.max(elv2, axis=-1, keepdims=True)
        i2 = jnp.min(jnp.where(elv2 == m2, lane, big), axis=-1, keepdims=True)
        p1 = 1.0 / esum
        p2 = jnp.exp(m2 - emax) / esum
        c1 = g_w * p1 / (p1 + p2)
        c2 = g_w * p2 / (p1 + p2)

        oh1 = lane == i1
        oh2 = lane == i2
        onehot = jnp.where(oh1 | oh2, 1.0, 0.0)
        before = jnp.dot(tril_ref[...], onehot.astype(BF16), preferred_element_type=F32) + state["base"]
        yield
        rank1 = jnp.sum(jnp.where(oh1, before, 0.0), axis=-1, keepdims=True)
        rank2 = jnp.sum(jnp.where(oh2, before, 0.0), axis=-1, keepdims=True)
        state["base"] = state["base"] + jnp.sum(onehot, axis=0, keepdims=True)

        meta = jnp.where(lane == 0, i1.astype(F32),
               jnp.where(lane == 1, i2.astype(F32),
               jnp.where(lane == 2, rank1,
               jnp.where(lane == 3, rank2,
               jnp.where(lane == 4, c1,
               jnp.where(lane == 5, c2, 0.0))))))
        meta_ref[rs, :] = meta[:, :META_COLS]
        metat_ref[:, rs] = meta.T[:META_COLS, :]

    _software_pipeline(project, route, tm // sub)
    base_scr[...] = state["base"]
    cnt_ref[...] = jnp.broadcast_to(state["base"], cnt_ref.shape)


def _merge(x2, yc, yr, ym, p):
    t, d = x2.shape
    tm = TM_MERGE
    db = D_BRANCH
    tok = lambda c: pl.BlockSpec((tm, c), lambda i: (i, 0))
    consts = [p["g_mix"], p["w_gate"], p["b_gate"], p["w_branch"], p["w_o"], p["g_ffn"],
              p["w_router"], p["b_router"], p["tril_strict"]]
    return pl.pallas_call(
        _merge_kernel,
        grid=(t // tm,),
        in_specs=[tok(d), tok(db), tok(db), tok(db)] + [_const_spec(c.shape) for c in consts],
        out_specs=[tok(d), tok(META_COLS), pl.BlockSpec((META_COLS, tm), lambda i: (0, i)),
                   _const_spec((8, LANES))],
        out_shape=[jax.ShapeDtypeStruct((t, d), F32), jax.ShapeDtypeStruct((t, META_COLS), F32),
                   jax.ShapeDtypeStruct((META_COLS, t), F32), jax.ShapeDtypeStruct((8, LANES), F32)],
        scratch_shapes=[pltpu.VMEM((1, LANES), F32)],
        compiler_params=pltpu.CompilerParams(dimension_semantics=("arbitrary",),
                                             vmem_limit_bytes=VMEM_LIMIT),
        name="merge",
    )(x2, yc, yr, ym, *consts)


def _store_packed_rows(ref2d, x, stage):
    rows, d = x.shape
    nt = d // (2 * LANES)
    for c in range(nt):
        stage[c, pl.ds(0, rows, stride=2), :] = x[:, c * LANES:(c + 1) * LANES]
        stage[c, pl.ds(1, rows, stride=2), :] = x[:, (c + nt) * LANES:(c + nt + 1) * LANES]
        ref2d[pl.ds(c, rows, stride=nt), :] = pltpu.bitcast(stage[c].astype(BF16), jnp.uint32)


def _load_packed_rows(ref2d, rows, nt, stage):
    lo, hi = [], []
    for c in range(nt):
        stage[c] = pltpu.bitcast(ref2d[pl.ds(c, rows, stride=nt), :], BF16).astype(F32)
        lo.append(stage[c, pl.ds(0, rows, stride=2), :])
        hi.append(stage[c, pl.ds(1, rows, stride=2), :])
    return jnp.concatenate(lo + hi, axis=1)


def _scatter_kernel(dest0_ref, dest1_ref, x1_ref, gffn_ref, xs_ref, hbuf, stage, sem):
    dest_refs = (dest0_ref, dest1_ref)
    ts, d_model = x1_ref.shape
    nt = d_model // (2 * LANES)
    s = pl.program_id(0)
    slot = s % 2

    def wait_slot(sl):
        for _ in range(TOP_K):
            pltpu.make_async_copy(hbuf.at[sl], xs_ref.at[pl.ds(0, ts * nt), :], sem.at[sl]).wait()

    @pl.when(s >= 2)
    def _():
        wait_slot(slot)

    _store_packed_rows(hbuf.at[slot], _rms(x1_ref[...], gffn_ref[...]), stage)

    def issue(grp, carry):
        grp_off = pl.multiple_of(grp * (SUBLANES * nt), SUBLANES * nt)
        for j in range(SUBLANES):
            tok = s * ts + grp * SUBLANES + j
            for kslot in range(TOP_K):
                d = pl.multiple_of(dest_refs[kslot][tok], nt)
                pltpu.make_async_copy(hbuf.at[slot, pl.ds(grp_off + j * nt, nt), :],
                                      xs_ref.at[pl.ds(d, nt), :], sem.at[slot]).start(priority=kslot)
        return carry

    lax.fori_loop(0, ts // SUBLANES, issue, 0, unroll=ISSUE_UNROLL)

    @pl.when(s == pl.num_programs(0) - 1)
    def _():
        @pl.when(s >= 1)
        def _():
            wait_slot(1 - slot)
        wait_slot(slot)


def _scatter(dests, x1, g_ffn, n_rows):
    t, d = x1.shape
    ts = TS_SCATTER
    pt = d // (2 * LANES)
    return pl.pallas_call(
        _scatter_kernel,
        grid_spec=pltpu.PrefetchScalarGridSpec(
            num_scalar_prefetch=TOP_K,
            grid=(t // ts,),
            in_specs=[pl.BlockSpec((ts, d), lambda i, *_: (i, 0)),
                      pl.BlockSpec((1, d), lambda i, *_: (0, 0))],
            out_specs=pl.BlockSpec(memory_space=pl.ANY),
            scratch_shapes=[pltpu.VMEM((2, ts * pt, LANES), jnp.uint32),
                            pltpu.VMEM((pt, 2 * ts, LANES), F32),
                            pltpu.SemaphoreType.DMA((2,))],
        ),
        out_shape=jax.ShapeDtypeStruct((n_rows * pt, LANES), jnp.uint32),
        compiler_params=pltpu.CompilerParams(dimension_semantics=("arbitrary",),
                                             vmem_limit_bytes=VMEM_LIMIT),
        name="scatter",
    )(*dests, x1, g_ffn)


def _experts_kernel(be_ref, nused_ref, nexte_ref, wslot_ref, xs_ref, wg_hbm, wu_hbm, wd_hbm, ys_ref,
                    wg_f, wu_f, wd_f, wg_s, wu_s, wd_s, xbuf, ybuf, xstage, ystage, sem, xsem, ysem):
    i = pl.program_id(0)
    e = be_ref[i]
    prev = be_ref[jnp.maximum(i - 1, 0)]
    active = i < nused_ref[0]

    def weight_copies(ex):
        ws = wslot_ref[ex]
        return (pltpu.make_async_copy(wg_hbm.at[ex], wg_f.at[ws], sem.at[ws, 0]),
                pltpu.make_async_copy(wu_hbm.at[ex], wu_f.at[ws], sem.at[ws, 1]),
                pltpu.make_async_copy(wd_hbm.at[ex], wd_f.at[ws], sem.at[ws, 2]))

    def start_weights(ex):
        @pl.when(ex < N_EXPERTS)
        def _():
            for cp in weight_copies(ex):
                cp.start(priority=1)

    @pl.when(i == 0)
    def _():
        start_weights(e)
        start_weights(nexte_ref[e])

    @pl.when(active & ((i == 0) | (e != prev)))
    def _():
        for cp in weight_copies(e):
            cp.wait()
        ws = wslot_ref[e]
        wg_s[...] = wg_f[ws].astype(BF16)
        wu_s[...] = wu_f[ws].astype(BF16)
        wd_s[...] = wd_f[ws].astype(BF16)
        nxt = nexte_ref[e]
        start_weights(jnp.where(nxt < N_EXPERTS, nexte_ref[jnp.minimum(nxt, N_EXPERTS - 1)], N_EXPERTS))

    blk_rows = xbuf.shape[1]
    n_used = nused_ref[0]

    def xs_copy(blk, slot):
        return pltpu.make_async_copy(xs_ref.at[pl.ds(pl.multiple_of(blk * blk_rows, blk_rows), blk_rows), :],
                                     xbuf.at[slot], xsem.at[slot])

    def ys_copy(blk, slot):
        return pltpu.make_async_copy(ybuf.at[slot],
                                     ys_ref.at[pl.ds(pl.multiple_of(blk * blk_rows, blk_rows), blk_rows), :],
                                     ysem.at[slot])

    @pl.when(i == 0)
    def _():
        for ahead in range(XS_BUFFERS - 1):
            @pl.when(ahead < n_used)
            def _():
                xs_copy(ahead, ahead).start()

    @pl.when(active)
    def _():
        nt = wg_s.shape[0] // (2 * LANES)
        fetch = i + (XS_BUFFERS - 1)

        @pl.when(fetch < n_used)
        def _():
            xs_copy(fetch, fetch % XS_BUFFERS).start()

        xs_copy(i, i % XS_BUFFERS).wait()
        oslot = i % 2

        @pl.when(i >= 2)
        def _():
            ys_copy(i - 2, oslot).wait()

        xb = _load_packed_rows(xbuf.at[i % XS_BUFFERS], ROW_BLOCK, nt, xstage).astype(BF16)
        gate = jnp.dot(xb, wg_s[...], preferred_element_type=F32)
        up = jnp.dot(xb, wu_s[...], preferred_element_type=F32)
        hid = gate * _sigmoid(gate) * up
        _store_packed_rows(ybuf.at[oslot], jnp.dot(hid.astype(BF16), wd_s[...], preferred_element_type=F32),
                           ystage)
        ys_copy(i, oslot).start()

        @pl.when(i == n_used - 1)
        def _():
            @pl.when(i >= 1)
            def _():
                ys_copy(i - 1, 1 - oslot).wait()
            ys_copy(i, oslot).wait()


def _experts(blk_expert, n_used, next_expert, weight_slot, xs, w_gate, w_up, w_down):
    d, de = w_gate.shape[-2:]
    pt = d // (2 * LANES)
    blk_rows = ROW_BLOCK * pt
    nb = xs.shape[0] // blk_rows

    return pl.pallas_call(
        _experts_kernel,
        grid_spec=pltpu.PrefetchScalarGridSpec(
            num_scalar_prefetch=4,
            grid=(nb,),
            in_specs=[pl.BlockSpec(memory_space=pl.ANY)] * 4,
            out_specs=pl.BlockSpec(memory_space=pl.ANY),
            scratch_shapes=[pltpu.VMEM((2, d, de), F32), pltpu.VMEM((2, d, de), F32),
                            pltpu.VMEM((2, de, d), F32),
                            pltpu.VMEM((d, de), BF16), pltpu.VMEM((d, de), BF16), pltpu.VMEM((de, d), BF16),
                            pltpu.VMEM((XS_BUFFERS, blk_rows, LANES), jnp.uint32),
                            pltpu.VMEM((2, blk_rows, LANES), jnp.uint32),
                            pltpu.VMEM((pt, 2 * ROW_BLOCK, LANES), F32),
                            pltpu.VMEM((pt, 2 * ROW_BLOCK, LANES), F32),
                            pltpu.SemaphoreType.DMA((2, 3)), pltpu.SemaphoreType.DMA((XS_BUFFERS,)),
                            pltpu.SemaphoreType.DMA((2,))],
        ),
        out_shape=jax.ShapeDtypeStruct(xs.shape, jnp.uint32),
        compiler_params=pltpu.CompilerParams(dimension_semantics=("arbitrary",),
                                             vmem_limit_bytes=VMEM_LIMIT),
        name="experts",
    )(blk_expert, n_used, next_expert, weight_slot, xs, w_gate, w_up, w_down)


def _combine_kernel(dest0_ref, dest1_ref, x1_ref, meta_ref, gfin_ref, ys_ref, out_ref,
                    ybuf_a, ybuf_b, stage, sem):
    dest_refs = (dest0_ref, dest1_ref)
    te = x1_ref.shape[0] // 2
    s = pl.program_id(0)
    nsteps = pl.num_programs(0)
    nt = x1_ref.shape[1] // (2 * LANES)
    bufs = (ybuf_a, ybuf_b)

    def issue(first_tok, half):
        for row in range(te):
            for kslot in range(TOP_K):
                d = pl.multiple_of(dest_refs[kslot][first_tok + row], nt)
                pltpu.make_async_copy(ys_ref.at[pl.ds(d, nt), :],
                                      bufs[half].at[kslot, pl.ds(row * nt, nt), :],
                                      sem.at[half]).start(priority=kslot)

    def wait(half):
        for kslot in range(TOP_K):
            pltpu.make_async_copy(ys_ref.at[pl.ds(0, te * nt), :], bufs[half].at[kslot], sem.at[half]).wait()

    def combine(half):
        rs = slice(half * te, (half + 1) * te)
        meta = meta_ref[rs, :]
        y0 = _load_packed_rows(bufs[half].at[0], te, nt, stage.at[half, 0])
        y1 = _load_packed_rows(bufs[half].at[1], te, nt, stage.at[half, 1])
        x2 = x1_ref[rs, :] + y0 * meta[:, 4:5] + y1 * meta[:, 5:6]
        out_ref[rs, :] = _rms(x2, gfin_ref[...])

    @pl.when(s == 0)
    def _():
        issue(0, 0)

    tok0 = s * (2 * te)
    wait(0)
    issue(tok0 + te, 1)
    combine(0)
    wait(1)
    issue(jnp.where(s + 1 < nsteps, tok0 + 2 * te, tok0), 0)
    combine(1)

    @pl.when(s == nsteps - 1)
    def _():
        wait(0)


def _combine(dests, x1, meta, g_final, ys):
    t, d = x1.shape
    te = TE_COMBINE
    pt = d // (2 * LANES)
    return pl.pallas_call(
        _combine_kernel,
        grid_spec=pltpu.PrefetchScalarGridSpec(
            num_scalar_prefetch=TOP_K,
            grid=(t // (2 * te),),
            in_specs=[pl.BlockSpec((2 * te, d), lambda i, *_: (i, 0)),
                      pl.BlockSpec((2 * te, META_COLS), lambda i, *_: (i, 0)),
                      pl.BlockSpec((1, d), lambda i, *_: (0, 0)),
                      pl.BlockSpec(memory_space=pl.ANY)],
            out_specs=pl.BlockSpec((2 * te, d), lambda i, *_: (i, 0)),
            scratch_shapes=[pltpu.VMEM((TOP_K, te * pt, LANES), jnp.uint32),
                            pltpu.VMEM((TOP_K, te * pt, LANES), jnp.uint32),
                            pltpu.VMEM((2, TOP_K, pt, 2 * te, LANES), F32),
                            pltpu.SemaphoreType.DMA((2,))],
        ),
        out_shape=jax.ShapeDtypeStruct((t, d), F32),
        compiler_params=pltpu.CompilerParams(dimension_semantics=("arbitrary",),
                                             vmem_limit_bytes=VMEM_LIMIT),
        name="combine",
    )(*dests, x1, meta, g_final, ys)


def _constants(tm_merge):
    n = CHUNK
    tri = (jnp.arange(n)[:, None] >= jnp.arange(n)[None, :]).astype(BF16)
    head = jnp.arange(2 * LANES) // HEAD_DIM
    seg_ones = (head[:, None] == head[None, :]).astype(BF16)
    tril_strict = (jnp.arange(tm_merge)[:, None] > jnp.arange(tm_merge)[None, :]).astype(BF16)
    return tri, seg_ones, tril_strict


def kernel(x, mem, g_mix, g_mem, w_in, conv_w, mu_rkv, mu_wag, w_lora1, w_lora2, w0, a_lora1, a_lora2, a0, g_lora1, g_lora2, k_k, k_a, r_k, ln_x_w, ln_x_b, w_kv_mem, w_branch, w_gate, b_gate, w_o, g_ffn, w_router_group, b_router_group, w_router_expert, b_router_expert, w_exp_gate, w_exp_up, w_exp_down, g_final):
    assert g_mix.shape[0] == 1, "single-layer block"
    b, s, d = x.shape
    t = b * s
    db = D_BRANCH
    tri, seg_ones, tril_strict = _constants(MERGE_SUB)
    row = lambda a: a.reshape(1, -1)
    pad_r = LANES - N_EXPERTS - N_GROUPS
    p = {
        "g_mix": row(g_mix[0]), "w_in": w_in[0].astype(BF16), "conv_w": conv_w[0].T,
        "mu_rkv": row(mu_rkv[0]), "mu_wag": mu_wag[0],
        "w_lora1": w_lora1[0].astype(BF16), "w_lora2": w_lora2[0].astype(BF16), "w0": row(w0[0]),
        "a_lora1": a_lora1[0].astype(BF16), "a_lora2": a_lora2[0].astype(BF16), "a0": row(a0[0]),
        "g_lora1": g_lora1[0].astype(BF16), "g_lora2": g_lora2[0].astype(BF16),
        "k_k": row(k_k[0]), "k_a": row(k_a[0]), "r_k": row(r_k[0]),
        "ln_x_w": row(ln_x_w[0]), "ln_x_b": row(ln_x_b[0]),
        "w_gate": w_gate[0].astype(BF16), "b_gate": row(b_gate[0]),
        "w_branch": w_branch[0].astype(BF16), "w_o": w_o[0].astype(BF16), "g_ffn": row(g_ffn[0]),
        "w_router": jnp.concatenate([w_router_expert[0], w_router_group[0],
                                     jnp.zeros((d, pad_r), F32)], axis=1),
        "b_router": row(jnp.concatenate([b_router_expert[0], b_router_group[0],
                                         jnp.zeros((pad_r,), F32)])),
        "tri": tri, "seg_ones": seg_ones, "tril_strict": tril_strict,
    }

    km, vm = _memkv(mem, row(g_mem[0]), w_kv_mem[0].astype(BF16))
    x2 = x.reshape(t, d)
    yconv, ymem, r, k, v, kkn, a, lw, g = _prologue(x2, b, km, vm, p)
    yrwkv = _rwkv(r, k, v, kkn, a, lw, g, b, p)
    x1, meta, meta_t, cnt = _merge(x2, yconv, yrwkv, ymem, p)

    counts = cnt[0, :N_EXPERTS].astype(jnp.int32)
    padded = ((counts + ROW_BLOCK - 1) // ROW_BLOCK) * ROW_BLOCK
    pad_end = jnp.cumsum(padded)
    pad_start = pad_end - padded
    n_blocks = (t * TOP_K) // ROW_BLOCK + N_EXPERTS
    eids = jnp.arange(N_EXPERTS, dtype=jnp.int32)
    e_idx = meta_t[0:TOP_K].astype(jnp.int32)
    rank = meta_t[TOP_K:2 * TOP_K].astype(jnp.int32)
    start_of = jnp.sum(jnp.where(e_idx[:, None, :] == eids[None, :, None], pad_start[None, :, None], 0), axis=1)
    dest = (start_of + rank) * (d // (2 * LANES))
    dests = [dest[kslot] for kslot in range(TOP_K)]
    blk_start = jnp.arange(n_blocks, dtype=jnp.int32) * ROW_BLOCK
    blk_expert = jnp.minimum(jnp.sum((pad_end[None, :] <= blk_start[:, None]).astype(jnp.int32), axis=1),
                             N_EXPERTS - 1)
    n_used = (pad_end[-1:] // ROW_BLOCK).astype(jnp.int32)
    later_nonempty = (eids[None, :] > eids[:, None]) & (counts[None, :] > 0)
    next_expert = jnp.min(jnp.where(later_nonempty, eids[None, :], N_EXPERTS), axis=1)
    weight_slot = (jnp.cumsum((counts > 0).astype(jnp.int32)) - 1) & 1

    xs = _scatter(dests, x1, p["g_ffn"], n_blocks * ROW_BLOCK)
    ys = _experts(blk_expert, n_used, next_expert, weight_slot, xs,
                  w_exp_gate[0], w_exp_up[0], w_exp_down[0])
    out = _combine(dests, x1, meta, row(g_final), ys)
    return out.reshape(b, s, d)
```

```python
import functools

import jax
import jax.numpy as jnp
from jax import lax
from jax.experimental import pallas as pl
from jax.experimental.pallas import tpu as pltpu

F32 = jnp.float32
BF16 = jnp.bfloat16

NORM_EPS = 1e-6
GN_EPS = 64e-5
D_BRANCH = 512
HEAD_DIM = 64
N_HEADS = 8
CHUNK = 64
CHUNKS_PER_ITER = 4
MEM_HEADS = 4
MEM_HEAD_DIM = 128
N_GROUPS = 8
EXPERTS_PER_GROUP = 8
N_EXPERTS = 64
TOP_K = 2
ROW_BLOCK = 256
XS_BUFFERS = 4
LANES = 128
VMEM_LIMIT = 56 * 1024 * 1024

TM_PROLOGUE = 512
PROLOGUE_SUB = 256
TB_RWKV = 256
TM_MERGE = 1024
MERGE_SUB = 512
TS_SCATTER = 256
TE_COMBINE = 256
SUBLANES = 8
META_COLS = 8
ISSUE_UNROLL = 2


def _bdot(a, b):
    return jnp.dot(a.astype(BF16), b.astype(BF16), preferred_element_type=F32)


def _bdot_nt(a, b):
    return lax.dot_general(a.astype(BF16), b.astype(BF16), (((1,), (1,)), ((), ())),
                           preferred_element_type=F32)


def _split_terms(x, n_terms):
    terms = []
    for _ in range(n_terms):
        t = x.astype(BF16)
        terms.append(t)
        x = x - t.astype(F32)
    return terms


def _split_dot_left(m_bf16, x, n_terms):
    return sum(jnp.dot(m_bf16, t, preferred_element_type=F32) for t in _split_terms(x, n_terms))


def _head_sums(x, seg_bf16, n_terms):
    w = seg_bf16.shape[0]
    terms = _split_terms(x, n_terms)
    halves = [sum(jnp.dot(t[:, c:c + w], seg_bf16, preferred_element_type=F32) for t in terms)
              for c in range(0, x.shape[1], w)]
    return jnp.concatenate(halves, axis=1)


def _rms(x, g):
    return x * lax.rsqrt(jnp.mean(x * x, axis=-1, keepdims=True) + NORM_EPS) * g


def _sigmoid(x):
    return 1.0 / (1.0 + jnp.exp(-x))


def _run_together(*gens):
    live = list(gens)
    while live:
        for gen in list(live):
            try:
                next(gen)
            except StopIteration:
                live.remove(gen)


def _software_pipeline(heavy, light, n_sub):
    _run_together(heavy(0))
    for j in range(1, n_sub):
        _run_together(heavy(j), light(j - 1))
    _run_together(light(n_sub - 1))


def _const_spec(shape):
    n = len(shape)
    return pl.BlockSpec(shape, lambda *_: (0,) * n)


def _memkv_kernel(mem_ref, g_ref, w_ref, k_ref, v_ref):
    mn = _rms(mem_ref[0], g_ref[...])
    kv = _bdot(mn, w_ref[...])
    k_ref[0] = kv[:, :D_BRANCH].astype(BF16)
    v_ref[0] = kv[:, D_BRANCH:].astype(BF16)


def _memkv(mem, g_mem, w_kv):
    b, m, d = mem.shape
    return pl.pallas_call(
        _memkv_kernel,
        grid=(b,),
        in_specs=[pl.BlockSpec((1, m, d), lambda i: (i, 0, 0)),
                  _const_spec((1, d)), _const_spec((d, 2 * D_BRANCH))],
        out_specs=[pl.BlockSpec((1, m, D_BRANCH), lambda i: (i, 0, 0)),
                   pl.BlockSpec((1, m, D_BRANCH), lambda i: (i, 0, 0))],
        out_shape=[jax.ShapeDtypeStruct((b, m, D_BRANCH), BF16)] * 2,
        compiler_params=pltpu.CompilerParams(dimension_semantics=("arbitrary",),
                                             vmem_limit_bytes=VMEM_LIMIT),
        name="memkv",
    )(mem, g_mem, w_kv)


def _prologue_kernel(x_ref, gmix_ref, win_ref, convw_ref, murkv_ref, muwag_ref,
                     wl1_ref, wl2_ref, w0_ref, al1_ref, al2_ref, a0_ref, gl1_ref, gl2_ref,
                     kk_ref, ka_ref, seg_ref, km_ref, vm_ref,
                     yconv_ref, ymem_ref, r_ref, k_ref, v_ref, kkn_ref, a_ref, lw_ref, g_ref,
                     prev_h, prev_p, prev_cu):
    tm = x_ref.shape[0]
    db = D_BRANCH
    sub = PROLOGUE_SUB
    n_parts = win_ref.shape[1] // db

    @pl.when(pl.program_id(1) == 0)
    def _():
        prev_h[...] = jnp.zeros_like(prev_h)
        prev_p[...] = jnp.zeros_like(prev_p)
        prev_cu[...] = jnp.zeros_like(prev_cu)

    rows = lax.broadcasted_iota(jnp.int32, (sub, 1), 0)

    def shift1(u, prev_row):
        return jnp.where(rows == 0, prev_row, pltpu.roll(u, 1, axis=0))

    carry = {"h": prev_h[...], "p": prev_p[...], "cu": prev_cu[...]}
    projected = {}

    def project(j):
        h = _rms(x_ref[j * sub:(j + 1) * sub, :], gmix_ref[...])
        hb = h.astype(BF16)
        parts = []
        for c in range(n_parts):
            parts.append(jnp.dot(hb, win_ref[:, c * db:(c + 1) * db], preferred_element_type=F32))
            yield
        projected[j] = (h, parts)

    def mix(j):
        rs = slice(j * sub, (j + 1) * sub)
        h, (bg, cg, u, rp, kp, vp, q) = projected.pop(j)

        cu = cg * u
        pcu = carry["cu"]
        cu1 = shift1(cu, pcu[1:2, :])
        cu2 = jnp.where(rows == 0, pcu[0:1, :], jnp.where(rows == 1, pcu[1:2, :], pltpu.roll(cu, 2, axis=0)))
        conv = cu2 * convw_ref[0:1, :] + cu1 * convw_ref[1:2, :] + cu * convw_ref[2:3, :]
        yconv_ref[rs, :] = (bg * conv).astype(BF16)
        carry["cu"] = cu[sub - 2:sub, :]

        pr = jnp.concatenate([rp, kp, vp], axis=1)
        prs = shift1(pr, carry["p"])
        mixed = pr + (prs - pr) * murkv_ref[...]
        carry["p"] = pr[sub - 1:sub, :]
        r, k, v = mixed[:, :db], mixed[:, db:2 * db], mixed[:, 2 * db:]
        r_ref[rs, :] = r
        v_ref[rs, :] = v

        dh = shift1(h, carry["h"]) - h
        carry["h"] = h[sub - 1:sub, :]
        lora_w = _bdot(h + dh * muwag_ref[0:1, :], wl1_ref[...])
        lora_a = _bdot(h + dh * muwag_ref[1:2, :], al1_ref[...])
        lora_g = _bdot(h + dh * muwag_ref[2:3, :], gl1_ref[...])
        yield
        zz = w0_ref[...] + _bdot(jnp.tanh(lora_w), wl2_ref[...])
        a_lin = a0_ref[...] + _bdot(lora_a, al2_ref[...])
        g_ref[rs, :] = _bdot(_sigmoid(lora_g), gl2_ref[...])
        yield
        softplus = jnp.maximum(-zz, 0.0) + jnp.log(1.0 + jnp.exp(-jnp.abs(zz)))
        lw_ref[rs, :] = -jnp.exp(-softplus - 0.5)
        a = _sigmoid(a_lin)
        a_ref[rs, :] = a
        k_ref[rs, :] = k * (1.0 + (a - 1.0) * ka_ref[...])
        kk = k * kk_ref[...]
        ss = _head_sums(kk * kk, seg_ref[...], 1)
        yield
        kkn_ref[rs, :] = kk * lax.rsqrt(jnp.maximum(ss, 1e-24))

        scale = MEM_HEAD_DIM ** -0.5
        heads = [slice(hh * MEM_HEAD_DIM, (hh + 1) * MEM_HEAD_DIM) for hh in range(MEM_HEADS)]
        scores = [_bdot_nt(q[:, sl], km_ref[0, :, sl]) * scale for sl in heads]
        yield
        for sl, s in zip(heads, scores):
            p = jnp.exp(s - jnp.max(s, axis=-1, keepdims=True))
            o = _bdot(p, vm_ref[0, :, sl]) / jnp.sum(p, axis=-1, keepdims=True)
            ymem_ref[rs, sl] = o.astype(BF16)

    _software_pipeline(project, mix, tm // sub)
    prev_h[...] = carry["h"]
    prev_p[...] = carry["p"]
    prev_cu[...] = carry["cu"]


def _prologue(x2, b, km, vm, p):
    t, d = x2.shape
    s = t // b
    tm = TM_PROLOGUE
    db = D_BRANCH
    m = km.shape[1]
    steps = s // tm
    tok = lambda c: pl.BlockSpec((tm, c), lambda bi, i: (bi * steps + i, 0))
    consts = [p["g_mix"], p["w_in"], p["conv_w"], p["mu_rkv"], p["mu_wag"],
              p["w_lora1"], p["w_lora2"], p["w0"], p["a_lora1"], p["a_lora2"], p["a0"],
              p["g_lora1"], p["g_lora2"], p["k_k"], p["k_a"], p["seg_ones"]]
    out_shapes = ([jax.ShapeDtypeStruct((t, db), BF16)] * 2
                  + [jax.ShapeDtypeStruct((t, db), F32)] * 7)
    return pl.pallas_call(
        _prologue_kernel,
        grid=(b, steps),
        in_specs=[tok(d)] + [_const_spec(c.shape) for c in consts]
                 + [pl.BlockSpec((1, m, db), lambda bi, i: (bi, 0, 0))] * 2,
        out_specs=[tok(db)] * 9,
        out_shape=out_shapes,
        scratch_shapes=[pltpu.VMEM((1, d), F32), pltpu.VMEM((1, 3 * db), F32),
                        pltpu.VMEM((2, db), F32)],
        compiler_params=pltpu.CompilerParams(dimension_semantics=("arbitrary", "arbitrary"),
                                             vmem_limit_bytes=VMEM_LIMIT),
        name="prologue",
    )(x2, *consts, km, vm)


def _rwkv_kernel(r_ref, k_ref, v_ref, kk_ref, a_ref, lw_ref, g_ref, rk_ref, lnw_ref, lnb_ref,
                 tri_ref, seg_ref, out_ref, h_scr, y_scr):
    tb = r_ref.shape[0]
    n = HEAD_DIM
    c_len = CHUNK

    @pl.when(pl.program_id(1) == 0)
    def _():
        h_scr[...] = jnp.zeros_like(h_scr)

    pw = 2 * n
    row1 = lax.broadcasted_iota(jnp.int32, (c_len, pw), 0)
    lane1 = lax.broadcasted_iota(jnp.int32, (c_len, pw), 1)
    col1 = lane1 & (n - 1)
    left = lane1 < n
    strict1 = col1 < row1
    incl1 = col1 <= row1
    eye2 = (col1 == row1).astype(F32)
    zeros_pair = jnp.zeros((c_len, pw), F32)
    zeros_bd = jnp.zeros((2 * c_len, pw), F32)

    def block_diag(y):
        return jnp.concatenate([jnp.where(left, y, 0.0), jnp.where(left, 0.0, y)], axis=0)

    def pair_transpose(y):
        zt = block_diag(y).T
        return zt[:c_len] + zt[c_len:]

    def chunk_inputs(c):
        rows = pl.ds(pl.multiple_of(c * c_len, c_len), c_len)
        r = r_ref[rows, :]
        k = k_ref[rows, :]
        v = v_ref[rows, :]
        kk = kk_ref[rows, :]
        a = a_ref[rows, :]
        lw = lw_ref[rows, :]
        gcum = _split_dot_left(tri_ref[...], lw, 2)
        e_pos = jnp.exp(gcum)
        e_neg = jnp.exp(-gcum)
        p_last = jnp.exp(gcum[c_len - 1:c_len, :])
        bb = kk * a * e_neg
        kb = k * e_neg
        return dict(rows=rows, v=v, p_last=p_last, rb=r * e_pos, ab=-kk * jnp.exp(gcum - lw), bb=bb, kb=kb,
                    bbp=bb * p_last, kbp=kb * p_last)

    def chunk_group(it, carry):
        chunks = [chunk_inputs(it * CHUNKS_PER_ITER + ci) for ci in range(CHUNKS_PER_ITER)]
        n_pairs = N_HEADS // 2
        units = [(ci, pr) for ci in range(CHUNKS_PER_ITER) for pr in range(n_pairs)]
        nu = range(len(units))
        ls = [slice(pr * pw, (pr + 1) * pw) for _, pr in units]
        ch = [chunks[ci] for ci, _ in units]
        al = [ch[u]["ab"][:, ls[u]] for u in nu]
        rr = [ch[u]["rb"][:, ls[u]] for u in nu]
        v_bd = [block_diag(ch[u]["v"][:, ls[u]]) for u in nu]
        aa = [_bdot_nt(jnp.concatenate([al[u], rr[u]], axis=0),
                       jnp.concatenate([block_diag(ch[u]["bb"][:, ls[u]]), block_diag(ch[u]["kb"][:, ls[u]])],
                                       axis=0)) for u in nu]
        a_ab = [jnp.where(strict1, aa[u][:c_len, :pw], 0.0) for u in nu]
        a_ak = [jnp.where(strict1, aa[u][:c_len, pw:], 0.0) for u in nu]
        a_rb = [jnp.where(incl1, aa[u][c_len:, :pw], 0.0) for u in nu]
        a_rk = [jnp.where(incl1, aa[u][c_len:, pw:], 0.0) for u in nu]
        av = [_bdot(a_ak[u], v_bd[u]) for u in nu]
        t_inv = [eye2 + a_ab[u] for u in nu]
        x_pow = [_bdot(a_ab[u], block_diag(a_ab[u])) for u in nu]
        for lvl in range(5):
            if lvl < 4:
                z = [_bdot(jnp.concatenate([t_inv[u], x_pow[u]], axis=0), block_diag(x_pow[u])) for u in nu]
                t_inv = [t_inv[u] + z[u][:c_len] for u in nu]
                x_pow = [z[u][c_len:] for u in nu]
            else:
                t_inv = [t_inv[u] + _bdot(t_inv[u], block_diag(x_pow[u])) for u in nu]
        w12 = [_bdot(t_inv[u], jnp.concatenate([block_diag(al[u]), block_diag(av[u])], axis=1))
               for u in nu]
        z2 = []
        for u in nu:
            rhs2 = jnp.concatenate(
                [jnp.concatenate([block_diag(w12[u][:, :pw]), block_diag(w12[u][:, pw:])], axis=1),
                 jnp.concatenate([zeros_bd, v_bd[u]], axis=1)], axis=0)
            lhs3 = jnp.concatenate(
                [jnp.concatenate([pair_transpose(ch[u]["bbp"][:, ls[u]]),
                                  pair_transpose(ch[u]["kbp"][:, ls[u]])], axis=1),
                 jnp.concatenate([a_rb[u], a_rk[u]], axis=1)], axis=0)
            z2.append(_bdot(lhs3, rhs2))
        state = [h_scr[pr] for pr in range(n_pairs)]
        for u in nu:
            pr = units[u][1]
            mq = z2[u][:, :pw] + jnp.concatenate([zeros_pair, rr[u]], axis=0)
            out = _bdot(mq, block_diag(state[pr])) + z2[u][:, pw:]
            decay = eye2 * ch[u]["p_last"][:, ls[u]]
            p_mat = jnp.where(left, jnp.sum(jnp.where(left, decay, 0.0), axis=1, keepdims=True),
                              jnp.sum(jnp.where(left, 0.0, decay), axis=1, keepdims=True))
            state[pr] = p_mat * state[pr] + out[:c_len]
            y_scr[ch[u]["rows"], ls[u]] = out[c_len:]
        for pr in range(n_pairs):
            h_scr[pr] = state[pr]
        return carry

    lax.fori_loop(0, tb // (c_len * CHUNKS_PER_ITER), chunk_group, 0)

    y = y_scr[...]
    seg = seg_ref[...]
    inv_n = 1.0 / n
    mu = _head_sums(y, seg, 2) * inv_n
    yc = y - mu
    var = _head_sums(yc * yc, seg, 1) * inv_n
    yn = yc * lax.rsqrt(var + GN_EPS) * lnw_ref[...] + lnb_ref[...]
    bonus = _head_sums(r_ref[...] * k_ref[...] * rk_ref[...], seg, 1) * v_ref[...]
    out_ref[...] = ((yn + bonus) * g_ref[...]).astype(BF16)


def _rwkv(r, k, v, kkn, a, lw, g, b, p):
    t, db = r.shape
    tb = TB_RWKV
    steps = t // b // tb
    tok = pl.BlockSpec((tb, db), lambda bi, i: (bi * steps + i, 0))
    consts = [p["r_k"], p["ln_x_w"], p["ln_x_b"], p["tri"], p["seg_ones"]]
    return pl.pallas_call(
        _rwkv_kernel,
        grid=(b, steps),
        in_specs=[tok] * 7 + [_const_spec(c.shape) for c in consts],
        out_specs=tok,
        out_shape=jax.ShapeDtypeStruct((t, db), BF16),
        scratch_shapes=[pltpu.VMEM((N_HEADS // 2, HEAD_DIM, 2 * HEAD_DIM), F32),
                        pltpu.VMEM((tb, db), F32)],
        compiler_params=pltpu.CompilerParams(dimension_semantics=("arbitrary", "arbitrary"),
                                             vmem_limit_bytes=VMEM_LIMIT),
        name="rwkv",
    )(r, k, v, kkn, a, lw, g, *consts)


def _merge_kernel(x_ref, yc_ref, yr_ref, ym_ref, gmix_ref, wgate_ref, bgate_ref, wbr_ref, wo_ref,
                  gffn_ref, wrt_ref, brt_ref, tril_ref,
                  x1_ref, meta_ref, metat_ref, cnt_ref, base_scr):
    tm, d = x_ref.shape

    @pl.when(pl.program_id(0) == 0)
    def _():
        base_scr[...] = jnp.zeros_like(base_scr)

    sub = tril_ref.shape[0]
    lane = lax.broadcasted_iota(jnp.int32, (sub, LANES), 1)
    neg = jnp.float32(-jnp.inf)
    big = jnp.int32(1 << 20)
    w_hi, w_lo = _split_terms(wrt_ref[...], 2)
    w_hi_lo = jnp.concatenate([w_hi, w_lo], axis=1)
    state = {"base": base_scr[...]}
    merged = {}

    def project(j):
        rs = slice(j * sub, (j + 1) * sub)
        x = x_ref[rs, :]
        hb = _rms(x, gmix_ref[...]).astype(BF16)
        z = jnp.zeros((sub, d), F32)
        for i, y_ref in enumerate((yc_ref, yr_ref, ym_ref)):
            cs = slice(i * d, (i + 1) * d)
            gate = _sigmoid(jnp.dot(hb, wgate_ref[:, cs], preferred_element_type=F32) + bgate_ref[:, cs])
            z = z + gate * jnp.dot(y_ref[rs, :], wbr_ref[i], preferred_element_type=F32)
            yield
        x1 = x + _bdot(z, wo_ref[...])
        x1_ref[rs, :] = x1
        merged[j] = x1

    def route(j):
        rs = slice(j * sub, (j + 1) * sub)
        h2 = _rms(merged.pop(j), gffn_ref[...])
        h_hi, h_lo = _split_terms(h2, 2)
        hi_terms = jnp.dot(h_hi, w_hi_lo, preferred_element_type=F32)
        logits = (hi_terms[:, :LANES]
                  + (jnp.dot(h_lo, w_hi, preferred_element_type=F32) + hi_terms[:, LANES:])) + brt_ref[...]
        yield
        gmask = (lane >= N_EXPERTS) & (lane < N_EXPERTS + N_GROUPS)
        glv = jnp.where(gmask, logits, neg)
        gmax = jnp.max(glv, axis=-1, keepdims=True)
        g_sel = jnp.min(jnp.where(glv == gmax, lane - N_EXPERTS, big), axis=-1, keepdims=True)
        g_w = 1.0 / jnp.sum(jnp.exp(glv - gmax), axis=-1, keepdims=True)
        emask = (lane < N_EXPERTS) & ((lane >> 3) == g_sel)
        elv = jnp.where(emask, logits, neg)
        emax = jnp.max(elv, axis=-1, keepdims=True)
        esum = jnp.sum(jnp.exp(elv - emax), axis=-1, keepdims=True)
        i1 = jnp.min(jnp.where(elv == emax, lane, big), axis=-1, keepdims=True)
        elv2 = jnp.where(lane == i1, neg, elv)
        m2 = jnp.max(elv2, axis=-1, keepdims=True)
        i2 = jnp.min(jnp.where(elv2 == m2, lane, big), axis=-1, keepdims=True)
        p1 = 1.0 / esum
        p2 = jnp.exp(m2 - emax) / esum
        c1 = g_w * p1 / (p1 + p2)
        c2 = g_w * p2 / (p1 + p2)

        oh1 = lane == i1
        oh2 = lane == i2
        onehot = jnp.where(oh1 | oh2, 1.0, 0.0)
        before = jnp.dot(tril_ref[...], onehot.astype(BF16), preferred_element_type=F32) + state["base"]
        yield
        rank1 = jnp.sum(jnp.where(oh1, before, 0.0), axis=-1, keepdims=True)
        rank2 = jnp.sum(jnp.where(oh2, before, 0.0), axis=-1, keepdims=True)
        state["base"] = state["base"] + jnp.sum(onehot, axis=0, keepdims=True)

        meta = jnp.where(lane == 0, i1.astype(F32),
               jnp.where(lane == 1, i2.astype(F32),
               jnp.where(lane == 2, rank1,
               jnp.where(lane == 3, rank2,
               jnp.where(lane == 4, c1,
               jnp.where(lane == 5, c2, 0.0))))))
        meta_ref[rs, :] = meta[:, :META_COLS]
        metat_ref[:, rs] = meta.T[:META_COLS, :]

    _software_pipeline(project, route, tm // sub)
    base_scr[...] = state["base"]
    cnt_ref[...] = jnp.broadcast_to(state["base"], cnt_ref.shape)


def _merge(x2, yc, yr, ym, p):
    t, d = x2.shape
    tm = TM_MERGE
    db = D_BRANCH
    tok = lambda c: pl.BlockSpec((tm, c), lambda i: (i, 0))
    consts = [p["g_mix"], p["w_gate"], p["b_gate"], p["w_branch"], p["w_o"], p["g_ffn"],
              p["w_router"], p["b_router"], p["tril_strict"]]
    return pl.pallas_call(
        _merge_kernel,
        grid=(t // tm,),
        in_specs=[tok(d), tok(db), tok(db), tok(db)] + [_const_spec(c.shape) for c in consts],
        out_specs=[tok(d), tok(META_COLS), pl.BlockSpec((META_COLS, tm), lambda i: (0, i)),
                   _const_spec((8, LANES))],
        out_shape=[jax.ShapeDtypeStruct((t, d), F32), jax.ShapeDtypeStruct((t, META_COLS), F32),
                   jax.ShapeDtypeStruct((META_COLS, t), F32), jax.ShapeDtypeStruct((8, LANES), F32)],
        scratch_shapes=[pltpu.VMEM((1, LANES), F32)],
        compiler_params=pltpu.CompilerParams(dimension_semantics=("arbitrary",),
                                             vmem_limit_bytes=VMEM_LIMIT),
        name="merge",
    )(x2, yc, yr, ym, *consts)


def _store_packed_rows(ref2d, x, stage):
    rows, d = x.shape
    nt = d // (2 * LANES)
    for c in range(nt):
        stage[c, pl.ds(0, rows, stride=2), :] = x[:, c * LANES:(c + 1) * LANES]
        stage[c, pl.ds(1, rows, stride=2), :] = x[:, (c + nt) * LANES:(c + nt + 1) * LANES]
        ref2d[pl.ds(c, rows, stride=nt), :] = pltpu.bitcast(stage[c].astype(BF16), jnp.uint32)


def _load_packed_rows(ref2d, rows, nt, stage):
    lo, hi = [], []
    for c in range(nt):
        stage[c] = pltpu.bitcast(ref2d[pl.ds(c, rows, stride=nt), :], BF16).astype(F32)
        lo.append(stage[c, pl.ds(0, rows, stride=2), :])
        hi.append(stage[c, pl.ds(1, rows, stride=2), :])
    return jnp.concatenate(lo + hi, axis=1)


def _scatter_kernel(dest0_ref, dest1_ref, x1_ref, gffn_ref, xs_ref, hbuf, stage, sem):
    dest_refs = (dest0_ref, dest1_ref)
    ts, d_model = x1_ref.shape
    nt = d_model // (2 * LANES)
    s = pl.program_id(0)
    slot = s % 2

    def wait_slot(sl):
        for _ in range(TOP_K):
            pltpu.make_async_copy(hbuf.at[sl], xs_ref.at[pl.ds(0, ts * nt), :], sem.at[sl]).wait()

    @pl.when(s >= 2)
    def _():
        wait_slot(slot)

    _store_packed_rows(hbuf.at[slot], _rms(x1_ref[...], gffn_ref[...]), stage)

    def issue(grp, carry):
        grp_off = pl.multiple_of(grp * (SUBLANES * nt), SUBLANES * nt)
        for j in range(SUBLANES):
            tok = s * ts + grp * SUBLANES + j
            for kslot in range(TOP_K):
                d = pl.multiple_of(dest_refs[kslot][tok], nt)
                pltpu.make_async_copy(hbuf.at[slot, pl.ds(grp_off + j * nt, nt), :],
                                      xs_ref.at[pl.ds(d, nt), :], sem.at[slot]).start(priority=kslot)
        return carry

    lax.fori_loop(0, ts // SUBLANES, issue, 0, unroll=ISSUE_UNROLL)

    @pl.when(s == pl.num_programs(0) - 1)
    def _():
        @pl.when(s >= 1)
        def _():
            wait_slot(1 - slot)
        wait_slot(slot)


def _scatter(dests, x1, g_ffn, n_rows):
    t, d = x1.shape
    ts = TS_SCATTER
    pt = d // (2 * LANES)
    return pl.pallas_call(
        _scatter_kernel,
        grid_spec=pltpu.PrefetchScalarGridSpec(
            num_scalar_prefetch=TOP_K,
            grid=(t // ts,),
            in_specs=[pl.BlockSpec((ts, d), lambda i, *_: (i, 0)),
                      pl.BlockSpec((1, d), lambda i, *_: (0, 0))],
            out_specs=pl.BlockSpec(memory_space=pl.ANY),
            scratch_shapes=[pltpu.VMEM((2, ts * pt, LANES), jnp.uint32),
                            pltpu.VMEM((pt, 2 * ts, LANES), F32),
                            pltpu.SemaphoreType.DMA((2,))],
        ),
        out_shape=jax.ShapeDtypeStruct((n_rows * pt, LANES), jnp.uint32),
        compiler_params=pltpu.CompilerParams(dimension_semantics=("arbitrary",),
                                             vmem_limit_bytes=VMEM_LIMIT),
        name="scatter",
    )(*dests, x1, g_ffn)


def _experts_kernel(be_ref, nused_ref, nexte_ref, wslot_ref, xs_ref, wg_hbm, wu_hbm, wd_hbm, ys_ref,
                    wg_f, wu_f, wd_f, wg_s, wu_s, wd_s, xbuf, ybuf, xstage, ystage, sem, xsem, ysem):
    i = pl.program_id(0)
    e = be_ref[i]
    prev = be_ref[jnp.maximum(i - 1, 0)]
    active = i < nused_ref[0]

    def weight_copies(ex):
        ws = wslot_ref[ex]
        return (pltpu.make_async_copy(wg_hbm.at[ex], wg_f.at[ws], sem.at[ws, 0]),
                pltpu.make_async_copy(wu_hbm.at[ex], wu_f.at[ws], sem.at[ws, 1]),
                pltpu.make_async_copy(wd_hbm.at[ex], wd_f.at[ws], sem.at[ws, 2]))

    def start_weights(ex):
        @pl.when(ex < N_EXPERTS)
        def _():
            for cp in weight_copies(ex):
                cp.start(priority=1)

    @pl.when(i == 0)
    def _():
        start_weights(e)
        start_weights(nexte_ref[e])

    @pl.when(active & ((i == 0) | (e != prev)))
    def _():
        for cp in weight_copies(e):
            cp.wait()
        ws = wslot_ref[e]
        wg_s[...] = wg_f[ws].astype(BF16)
        wu_s[...] = wu_f[ws].astype(BF16)
        wd_s[...] = wd_f[ws].astype(BF16)
        nxt = nexte_ref[e]
        start_weights(jnp.where(nxt < N_EXPERTS, nexte_ref[jnp.minimum(nxt, N_EXPERTS - 1)], N_EXPERTS))

    blk_rows = xbuf.shape[1]
    n_used = nused_ref[0]

    def xs_copy(blk, slot):
        return pltpu.make_async_copy(xs_ref.at[pl.ds(pl.multiple_of(blk * blk_rows, blk_rows), blk_rows), :],
                                     xbuf.at[slot], xsem.at[slot])

    def ys_copy(blk, slot):
        return pltpu.make_async_copy(ybuf.at[slot],
                                     ys_ref.at[pl.ds(pl.multiple_of(blk * blk_rows, blk_rows), blk_rows), :],
                                     ysem.at[slot])

    @pl.when(i == 0)
    def _():
        for ahead in range(XS_BUFFERS - 1):
            @pl.when(ahead < n_used)
            def _():
                xs_copy(ahead, ahead).start()

    @pl.when(active)
    def _():
        nt = wg_s.shape[0] // (2 * LANES)
        fetch = i + (XS_BUFFERS - 1)

        @pl.when(fetch < n_used)
        def _():
            xs_copy(fetch, fetch % XS_BUFFERS).start()

        xs_copy(i, i % XS_BUFFERS).wait()
        oslot = i % 2

        @pl.when(i >= 2)
        def _():
            ys_copy(i - 2, oslot).wait()

        xb = _load_packed_rows(xbuf.at[i % XS_BUFFERS], ROW_BLOCK, nt, xstage).astype(BF16)
        gate = jnp.dot(xb, wg_s[...], preferred_element_type=F32)
        up = jnp.dot(xb, wu_s[...], preferred_element_type=F32)
        hid = gate * _sigmoid(gate) * up
        _store_packed_rows(ybuf.at[oslot], jnp.dot(hid.astype(BF16), wd_s[...], preferred_element_type=F32),
                           ystage)
        ys_copy(i, oslot).start()

        @pl.when(i == n_used - 1)
        def _():
            @pl.when(i >= 1)
            def _():
                ys_copy(i - 1, 1 - oslot).wait()
            ys_copy(i, oslot).wait()


def _experts(blk_expert, n_used, next_expert, weight_slot, xs, w_gate, w_up, w_down):
    d, de = w_gate.shape[-2:]
    pt = d // (2 * LANES)
    blk_rows = ROW_BLOCK * pt
    nb = xs.shape[0] // blk_rows

    return pl.pallas_call(
        _experts_kernel,
        grid_spec=pltpu.PrefetchScalarGridSpec(
            num_scalar_prefetch=4,
            grid=(nb,),
            in_specs=[pl.BlockSpec(memory_space=pl.ANY)] * 4,
            out_specs=pl.BlockSpec(memory_space=pl.ANY),
            scratch_shapes=[pltpu.VMEM((2, d, de), F32), pltpu.VMEM((2, d, de), F32),
                            pltpu.VMEM((2, de, d), F32),
                            pltpu.VMEM((d, de), BF16), pltpu.VMEM((d, de), BF16), pltpu.VMEM((de, d), BF16),
                            pltpu.VMEM((XS_BUFFERS, blk_rows, LANES), jnp.uint32),
                            pltpu.VMEM((2, blk_rows, LANES), jnp.uint32),
                            pltpu.VMEM((pt, 2 * ROW_BLOCK, LANES), F32),
                            pltpu.VMEM((pt, 2 * ROW_BLOCK, LANES), F32),
                            pltpu.SemaphoreType.DMA((2, 3)), pltpu.SemaphoreType.DMA((XS_BUFFERS,)),
                            pltpu.SemaphoreType.DMA((2,))],
        ),
        out_shape=jax.ShapeDtypeStruct(xs.shape, jnp.uint32),
        compiler_params=pltpu.CompilerParams(dimension_semantics=("arbitrary",),
                                             vmem_limit_bytes=VMEM_LIMIT),
        name="experts",
    )(blk_expert, n_used, next_expert, weight_slot, xs, w_gate, w_up, w_down)


def _combine_kernel(dest0_ref, dest1_ref, x1_ref, meta_ref, gfin_ref, ys_ref, out_ref, ybuf, stage, sem):
    dest_refs = (dest0_ref, dest1_ref)
    te = x1_ref.shape[0]
    s = pl.program_id(0)
    nsteps = pl.num_programs(0)
    slot = s % 2

    nt = x1_ref.shape[1] // (2 * LANES)

    def issue_step(step, sl):
        def issue(grp, carry):
            grp_off = pl.multiple_of(grp * (SUBLANES * nt), SUBLANES * nt)
            for j in range(SUBLANES):
                tok = step * te + grp * SUBLANES + j
                for kslot in range(TOP_K):
                    d = pl.multiple_of(dest_refs[kslot][tok], nt)
                    pltpu.make_async_copy(ys_ref.at[pl.ds(d, nt), :],
                                          ybuf.at[sl, kslot, pl.ds(grp_off + j * nt, nt), :],
                                          sem.at[sl]).start(priority=kslot)
            return carry
        lax.fori_loop(0, te // SUBLANES, issue, 0, unroll=ISSUE_UNROLL)

    @pl.when(s == 0)
    def _():
        issue_step(0, 0)

    @pl.when(s + 1 < nsteps)
    def _():
        issue_step(s + 1, 1 - slot)

    for kslot in range(TOP_K):
        pltpu.make_async_copy(ys_ref.at[pl.ds(0, te * nt), :], ybuf.at[slot, kslot], sem.at[slot]).wait()

    meta = meta_ref[...]
    y0 = _load_packed_rows(ybuf.at[slot, 0], te, nt, stage.at[0])
    y1 = _load_packed_rows(ybuf.at[slot, 1], te, nt, stage.at[1])
    x2 = x1_ref[...] + y0 * meta[:, 4:5] + y1 * meta[:, 5:6]
    out_ref[...] = _rms(x2, gfin_ref[...])


def _combine(dests, x1, meta, g_final, ys):
    t, d = x1.shape
    te = TE_COMBINE
    pt = d // (2 * LANES)
    return pl.pallas_call(
        _combine_kernel,
        grid_spec=pltpu.PrefetchScalarGridSpec(
            num_scalar_prefetch=TOP_K,
            grid=(t // te,),
            in_specs=[pl.BlockSpec((te, d), lambda i, *_: (i, 0)),
                      pl.BlockSpec((te, META_COLS), lambda i, *_: (i, 0)),
                      pl.BlockSpec((1, d), lambda i, *_: (0, 0)),
                      pl.BlockSpec(memory_space=pl.ANY)],
            out_specs=pl.BlockSpec((te, d), lambda i, *_: (i, 0)),
            scratch_shapes=[pltpu.VMEM((2, TOP_K, te * pt, LANES), jnp.uint32),
                            pltpu.VMEM((TOP_K, pt, 2 * te, LANES), F32),
                            pltpu.SemaphoreType.DMA((2,))],
        ),
        out_shape=jax.ShapeDtypeStruct((t, d), F32),
        compiler_params=pltpu.CompilerParams(dimension_semantics=("arbitrary",),
                                             vmem_limit_bytes=VMEM_LIMIT),
        name="combine",
    )(*dests, x1, meta, g_final, ys)


def _constants(tm_merge):
    n = CHUNK
    tri = (jnp.arange(n)[:, None] >= jnp.arange(n)[None, :]).astype(BF16)
    head = jnp.arange(2 * LANES) // HEAD_DIM
    seg_ones = (head[:, None] == head[None, :]).astype(BF16)
    tril_strict = (jnp.arange(tm_merge)[:, None] > jnp.arange(tm_merge)[None, :]).astype(BF16)
    return tri, seg_ones, tril_strict


def kernel(x, mem, g_mix, g_mem, w_in, conv_w, mu_rkv, mu_wag, w_lora1, w_lora2, w0, a_lora1, a_lora2, a0, g_lora1, g_lora2, k_k, k_a, r_k, ln_x_w, ln_x_b, w_kv_mem, w_branch, w_gate, b_gate, w_o, g_ffn, w_router_group, b_router_group, w_router_expert, b_router_expert, w_exp_gate, w_exp_up, w_exp_down, g_final):
    assert g_mix.shape[0] == 1, "single-layer block"
    b, s, d = x.shape
    t = b * s
    db = D_BRANCH
    tri, seg_ones, tril_strict = _constants(MERGE_SUB)
    row = lambda a: a.reshape(1, -1)
    pad_r = LANES - N_EXPERTS - N_GROUPS
    p = {
        "g_mix": row(g_mix[0]), "w_in": w_in[0].astype(BF16), "conv_w": conv_w[0].T,
        "mu_rkv": row(mu_rkv[0]), "mu_wag": mu_wag[0],
        "w_lora1": w_lora1[0].astype(BF16), "w_lora2": w_lora2[0].astype(BF16), "w0": row(w0[0]),
        "a_lora1": a_lora1[0].astype(BF16), "a_lora2": a_lora2[0].astype(BF16), "a0": row(a0[0]),
        "g_lora1": g_lora1[0].astype(BF16), "g_lora2": g_lora2[0].astype(BF16),
        "k_k": row(k_k[0]), "k_a": row(k_a[0]), "r_k": row(r_k[0]),
        "ln_x_w": row(ln_x_w[0]), "ln_x_b": row(ln_x_b[0]),
        "w_gate": w_gate[0].astype(BF16), "b_gate": row(b_gate[0]),
        "w_branch": w_branch[0].astype(BF16), "w_o": w_o[0].astype(BF16), "g_ffn": row(g_ffn[0]),
        "w_router": jnp.concatenate([w_router_expert[0], w_router_group[0],
                                     jnp.zeros((d, pad_r), F32)], axis=1),
        "b_router": row(jnp.concatenate([b_router_expert[0], b_router_group[0],
                                         jnp.zeros((pad_r,), F32)])),
        "tri": tri, "seg_ones": seg_ones, "tril_strict": tril_strict,
    }

    km, vm = _memkv(mem, row(g_mem[0]), w_kv_mem[0].astype(BF16))
    x2 = x.reshape(t, d)
    yconv, ymem, r, k, v, kkn, a, lw, g = _prologue(x2, b, km, vm, p)
    yrwkv = _rwkv(r, k, v, kkn, a, lw, g, b, p)
    x1, meta, meta_t, cnt = _merge(x2, yconv, yrwkv, ymem, p)

    counts = cnt[0, :N_EXPERTS].astype(jnp.int32)
    padded = ((counts + ROW_BLOCK - 1) // ROW_BLOCK) * ROW_BLOCK
    pad_end = jnp.cumsum(padded)
    pad_start = pad_end - padded
    n_blocks = (t * TOP_K) // ROW_BLOCK + N_EXPERTS
    eids = jnp.arange(N_EXPERTS, dtype=jnp.int32)
    e_idx = meta_t[0:TOP_K].astype(jnp.int32)
    rank = meta_t[TOP_K:2 * TOP_K].astype(jnp.int32)
    start_of = jnp.sum(jnp.where(e_idx[:, None, :] == eids[None, :, None], pad_start[None, :, None], 0), axis=1)
    dest = (start_of + rank) * (d // (2 * LANES))
    dests = [dest[kslot] for kslot in range(TOP_K)]
    blk_start = jnp.arange(n_blocks, dtype=jnp.int32) * ROW_BLOCK
    blk_expert = jnp.minimum(jnp.sum((pad_end[None, :] <= blk_start[:, None]).astype(jnp.int32), axis=1),
                             N_EXPERTS - 1)
    n_used = (pad_end[-1:] // ROW_BLOCK).astype(jnp.int32)
    later_nonempty = (eids[None, :] > eids[:, None]) & (counts[None, :] > 0)
    next_expert = jnp.min(jnp.where(later_nonempty, eids[None, :], N_EXPERTS), axis=1)
    weight_slot = (jnp.cumsum((counts > 0).astype(jnp.int32)) - 1) & 1

    xs = _scatter(dests, x1, p["g_ffn"], n_blocks * ROW_BLOCK)
    ys = _experts(blk_expert, n_used, next_expert, weight_slot, xs,
                  w_exp_gate[0], w_exp_up[0], w_exp_down[0])
    out = _combine(dests, x1, meta, row(g_final), ys)
    return out.reshape(b, s, d)
```

```python
import functools

import jax
import jax.numpy as jnp
from jax import lax
from jax.experimental import pallas as pl
from jax.experimental.pallas import tpu as pltpu

F32 = jnp.float32
BF16 = jnp.bfloat16

NORM_EPS = 1e-6
GN_EPS = 64e-5
D_BRANCH = 512
HEAD_DIM = 64
N_HEADS = 8
CHUNK = 64
CHUNKS_PER_ITER = 4
MEM_HEADS = 4
MEM_HEAD_DIM = 128
N_GROUPS = 8
EXPERTS_PER_GROUP = 8
N_EXPERTS = 64
TOP_K = 2
ROW_BLOCK = 256
XS_BUFFERS = 4
LANES = 128
VMEM_LIMIT = 56 * 1024 * 1024

TM_PROLOGUE = 512
PROLOGUE_SUB = 256
TB_RWKV = 256
TM_MERGE = 1024
MERGE_SUB = 512
TS_SCATTER = 256
TE_COMBINE = 256
SUBLANES = 8
META_COLS = 8
ISSUE_UNROLL = 2


def _bdot(a, b):
    return jnp.dot(a.astype(BF16), b.astype(BF16), preferred_element_type=F32)


def _bdot_nt(a, b):
    return lax.dot_general(a.astype(BF16), b.astype(BF16), (((1,), (1,)), ((), ())),
                           preferred_element_type=F32)


def _split_terms(x, n_terms):
    terms = []
    for _ in range(n_terms):
        t = x.astype(BF16)
        terms.append(t)
        x = x - t.astype(F32)
    return terms


def _split_dot_left(m_bf16, x, n_terms):
    return sum(jnp.dot(m_bf16, t, preferred_element_type=F32) for t in _split_terms(x, n_terms))


def _head_sums(x, seg_bf16, n_terms):
    w = seg_bf16.shape[0]
    terms = _split_terms(x, n_terms)
    halves = [sum(jnp.dot(t[:, c:c + w], seg_bf16, preferred_element_type=F32) for t in terms)
              for c in range(0, x.shape[1], w)]
    return jnp.concatenate(halves, axis=1)


def _rms(x, g):
    return x * lax.rsqrt(jnp.mean(x * x, axis=-1, keepdims=True) + NORM_EPS) * g


def _sigmoid(x):
    return 1.0 / (1.0 + jnp.exp(-x))


def _run_together(*gens):
    live = list(gens)
    while live:
        for gen in list(live):
            try:
                next(gen)
            except StopIteration:
                live.remove(gen)


def _software_pipeline(heavy, light, n_sub):
    _run_together(heavy(0))
    for j in range(1, n_sub):
        _run_together(heavy(j), light(j - 1))
    _run_together(light(n_sub - 1))


def _const_spec(shape):
    n = len(shape)
    return pl.BlockSpec(shape, lambda *_: (0,) * n)


def _memkv_kernel(mem_ref, g_ref, w_ref, k_ref, v_ref):
    mn = _rms(mem_ref[0], g_ref[...])
    kv = _bdot(mn, w_ref[...])
    k_ref[0] = kv[:, :D_BRANCH].astype(BF16)
    v_ref[0] = kv[:, D_BRANCH:].astype(BF16)


def _memkv(mem, g_mem, w_kv):
    b, m, d = mem.shape
    return pl.pallas_call(
        _memkv_kernel,
        grid=(b,),
        in_specs=[pl.BlockSpec((1, m, d), lambda i: (i, 0, 0)),
                  _const_spec((1, d)), _const_spec((d, 2 * D_BRANCH))],
        out_specs=[pl.BlockSpec((1, m, D_BRANCH), lambda i: (i, 0, 0)),
                   pl.BlockSpec((1, m, D_BRANCH), lambda i: (i, 0, 0))],
        out_shape=[jax.ShapeDtypeStruct((b, m, D_BRANCH), BF16)] * 2,
        compiler_params=pltpu.CompilerParams(dimension_semantics=("arbitrary",),
                                             vmem_limit_bytes=VMEM_LIMIT),
        name="memkv",
    )(mem, g_mem, w_kv)


def _prologue_kernel(x_ref, gmix_ref, win_ref, convw_ref, murkv_ref, muwag_ref,
                     wl1_ref, wl2_ref, w0_ref, al1_ref, al2_ref, a0_ref, gl1_ref, gl2_ref,
                     kk_ref, ka_ref, seg_ref, km_ref, vm_ref,
                     yconv_ref, ymem_ref, r_ref, k_ref, v_ref, kkn_ref, a_ref, lw_ref, g_ref,
                     prev_h, prev_p, prev_cu):
    tm = x_ref.shape[0]
    db = D_BRANCH
    sub = PROLOGUE_SUB
    n_parts = win_ref.shape[1] // db

    @pl.when(pl.program_id(1) == 0)
    def _():
        prev_h[...] = jnp.zeros_like(prev_h)
        prev_p[...] = jnp.zeros_like(prev_p)
        prev_cu[...] = jnp.zeros_like(prev_cu)

    rows = lax.broadcasted_iota(jnp.int32, (sub, 1), 0)

    def shift1(u, prev_row):
        return jnp.where(rows == 0, prev_row, pltpu.roll(u, 1, axis=0))

    carry = {"h": prev_h[...], "p": prev_p[...], "cu": prev_cu[...]}
    projected = {}

    def project(j):
        h = _rms(x_ref[j * sub:(j + 1) * sub, :], gmix_ref[...])
        hb = h.astype(BF16)
        parts = []
        for c in range(n_parts):
            parts.append(jnp.dot(hb, win_ref[:, c * db:(c + 1) * db], preferred_element_type=F32))
            yield
        projected[j] = (h, parts)

    def mix(j):
        rs = slice(j * sub, (j + 1) * sub)
        h, (bg, cg, u, rp, kp, vp, q) = projected.pop(j)

        cu = cg * u
        pcu = carry["cu"]
        cu1 = shift1(cu, pcu[1:2, :])
        cu2 = jnp.where(rows == 0, pcu[0:1, :], jnp.where(rows == 1, pcu[1:2, :], pltpu.roll(cu, 2, axis=0)))
        conv = cu2 * convw_ref[0:1, :] + cu1 * convw_ref[1:2, :] + cu * convw_ref[2:3, :]
        yconv_ref[rs, :] = (bg * conv).astype(BF16)
        carry["cu"] = cu[sub - 2:sub, :]

        pr = jnp.concatenate([rp, kp, vp], axis=1)
        prs = shift1(pr, carry["p"])
        mixed = pr + (prs - pr) * murkv_ref[...]
        carry["p"] = pr[sub - 1:sub, :]
        r, k, v = mixed[:, :db], mixed[:, db:2 * db], mixed[:, 2 * db:]
        r_ref[rs, :] = r
        v_ref[rs, :] = v

        dh = shift1(h, carry["h"]) - h
        carry["h"] = h[sub - 1:sub, :]
        lora_w = _bdot(h + dh * muwag_ref[0:1, :], wl1_ref[...])
        lora_a = _bdot(h + dh * muwag_ref[1:2, :], al1_ref[...])
        lora_g = _bdot(h + dh * muwag_ref[2:3, :], gl1_ref[...])
        yield
        zz = w0_ref[...] + _bdot(jnp.tanh(lora_w), wl2_ref[...])
        a_lin = a0_ref[...] + _bdot(lora_a, al2_ref[...])
        g_ref[rs, :] = _bdot(_sigmoid(lora_g), gl2_ref[...])
        yield
        softplus = jnp.maximum(-zz, 0.0) + jnp.log(1.0 + jnp.exp(-jnp.abs(zz)))
        lw_ref[rs, :] = -jnp.exp(-softplus - 0.5)
        a = _sigmoid(a_lin)
        a_ref[rs, :] = a
        k_ref[rs, :] = k * (1.0 + (a - 1.0) * ka_ref[...])
        kk = k * kk_ref[...]
        ss = _head_sums(kk * kk, seg_ref[...], 1)
        yield
        kkn_ref[rs, :] = kk * lax.rsqrt(jnp.maximum(ss, 1e-24))

        scale = MEM_HEAD_DIM ** -0.5
        heads = [slice(hh * MEM_HEAD_DIM, (hh + 1) * MEM_HEAD_DIM) for hh in range(MEM_HEADS)]
        scores = [_bdot_nt(q[:, sl], km_ref[0, :, sl]) * scale for sl in heads]
        yield
        for sl, s in zip(heads, scores):
            p = jnp.exp(s - jnp.max(s, axis=-1, keepdims=True))
            o = _bdot(p, vm_ref[0, :, sl]) / jnp.sum(p, axis=-1, keepdims=True)
            ymem_ref[rs, sl] = o.astype(BF16)

    _software_pipeline(project, mix, tm // sub)
    prev_h[...] = carry["h"]
    prev_p[...] = carry["p"]
    prev_cu[...] = carry["cu"]


def _prologue(x2, b, km, vm, p):
    t, d = x2.shape
    s = t // b
    tm = TM_PROLOGUE
    db = D_BRANCH
    m = km.shape[1]
    steps = s // tm
    tok = lambda c: pl.BlockSpec((tm, c), lambda bi, i: (bi * steps + i, 0))
    consts = [p["g_mix"], p["w_in"], p["conv_w"], p["mu_rkv"], p["mu_wag"],
              p["w_lora1"], p["w_lora2"], p["w0"], p["a_lora1"], p["a_lora2"], p["a0"],
              p["g_lora1"], p["g_lora2"], p["k_k"], p["k_a"], p["seg_ones"]]
    out_shapes = ([jax.ShapeDtypeStruct((t, db), BF16)] * 2
                  + [jax.ShapeDtypeStruct((t, db), F32)] * 7)
    return pl.pallas_call(
        _prologue_kernel,
        grid=(b, steps),
        in_specs=[tok(d)] + [_const_spec(c.shape) for c in consts]
                 + [pl.BlockSpec((1, m, db), lambda bi, i: (bi, 0, 0))] * 2,
        out_specs=[tok(db)] * 9,
        out_shape=out_shapes,
        scratch_shapes=[pltpu.VMEM((1, d), F32), pltpu.VMEM((1, 3 * db), F32),
                        pltpu.VMEM((2, db), F32)],
        compiler_params=pltpu.CompilerParams(dimension_semantics=("arbitrary", "arbitrary"),
                                             vmem_limit_bytes=VMEM_LIMIT),
        name="prologue",
    )(x2, *consts, km, vm)


def _rwkv_kernel(r_ref, k_ref, v_ref, kk_ref, a_ref, lw_ref, g_ref, rk_ref, lnw_ref, lnb_ref,
                 tri_ref, seg_ref, out_ref, h_scr, y_scr):
    tb = r_ref.shape[0]
    n = HEAD_DIM
    c_len = CHUNK

    @pl.when(pl.program_id(1) == 0)
    def _():
        h_scr[...] = jnp.zeros_like(h_scr)

    pw = 2 * n
    row1 = lax.broadcasted_iota(jnp.int32, (c_len, pw), 0)
    lane1 = lax.broadcasted_iota(jnp.int32, (c_len, pw), 1)
    col1 = lane1 & (n - 1)
    left = lane1 < n
    strict1 = col1 < row1
    incl1 = col1 <= row1
    eye2 = (col1 == row1).astype(F32)
    zeros_pair = jnp.zeros((c_len, pw), F32)
    zeros_bd = jnp.zeros((2 * c_len, pw), F32)

    def block_diag(y):
        return jnp.concatenate([jnp.where(left, y, 0.0), jnp.where(left, 0.0, y)], axis=0)

    def pair_transpose(y):
        zt = block_diag(y).T
        return zt[:c_len] + zt[c_len:]

    def chunk_inputs(c):
        rows = pl.ds(pl.multiple_of(c * c_len, c_len), c_len)
        r = r_ref[rows, :]
        k = k_ref[rows, :]
        v = v_ref[rows, :]
        kk = kk_ref[rows, :]
        a = a_ref[rows, :]
        lw = lw_ref[rows, :]
        gcum = _split_dot_left(tri_ref[...], lw, 2)
        e_pos = jnp.exp(gcum)
        e_neg = jnp.exp(-gcum)
        p_last = jnp.exp(gcum[c_len - 1:c_len, :])
        bb = kk * a * e_neg
        kb = k * e_neg
        return dict(rows=rows, v=v, p_last=p_last, rb=r * e_pos, ab=-kk * jnp.exp(gcum - lw), bb=bb, kb=kb,
                    bbp=bb * p_last, kbp=kb * p_last)

    def chunk_group(it, carry):
        chunks = [chunk_inputs(it * CHUNKS_PER_ITER + ci) for ci in range(CHUNKS_PER_ITER)]
        n_pairs = N_HEADS // 2
        units = [(ci, pr) for ci in range(CHUNKS_PER_ITER) for pr in range(n_pairs)]
        nu = range(len(units))
        ls = [slice(pr * pw, (pr + 1) * pw) for _, pr in units]
        ch = [chunks[ci] for ci, _ in units]
        al = [ch[u]["ab"][:, ls[u]] for u in nu]
        rr = [ch[u]["rb"][:, ls[u]] for u in nu]
        v_bd = [block_diag(ch[u]["v"][:, ls[u]]) for u in nu]
        aa = [_bdot_nt(jnp.concatenate([al[u], rr[u]], axis=0),
                       jnp.concatenate([block_diag(ch[u]["bb"][:, ls[u]]), block_diag(ch[u]["kb"][:, ls[u]])],
                                       axis=0)) for u in nu]
        a_ab = [jnp.where(strict1, aa[u][:c_len, :pw], 0.0) for u in nu]
        a_ak = [jnp.where(strict1, aa[u][:c_len, pw:], 0.0) for u in nu]
        a_rb = [jnp.where(incl1, aa[u][c_len:, :pw], 0.0) for u in nu]
        a_rk = [jnp.where(incl1, aa[u][c_len:, pw:], 0.0) for u in nu]
        av = [_bdot(a_ak[u], v_bd[u]) for u in nu]
        t_inv = [eye2 + a_ab[u] for u in nu]
        x_pow = [_bdot(a_ab[u], block_diag(a_ab[u])) for u in nu]
        for lvl in range(5):
            if lvl < 4:
                z = [_bdot(jnp.concatenate([t_inv[u], x_pow[u]], axis=0), block_diag(x_pow[u])) for u in nu]
                t_inv = [t_inv[u] + z[u][:c_len] for u in nu]
                x_pow = [z[u][c_len:] for u in nu]
            else:
                t_inv = [t_inv[u] + _bdot(t_inv[u], block_diag(x_pow[u])) for u in nu]
        w12 = [_bdot(t_inv[u], jnp.concatenate([block_diag(al[u]), block_diag(av[u])], axis=1))
               for u in nu]
        z2 = []
        for u in nu:
            rhs2 = jnp.concatenate(
                [jnp.concatenate([block_diag(w12[u][:, :pw]), block_diag(w12[u][:, pw:])], axis=1),
                 jnp.concatenate([zeros_bd, v_bd[u]], axis=1)], axis=0)
            lhs3 = jnp.concatenate(
                [jnp.concatenate([pair_transpose(ch[u]["bbp"][:, ls[u]]),
                                  pair_transpose(ch[u]["kbp"][:, ls[u]])], axis=1),
                 jnp.concatenate([a_rb[u], a_rk[u]], axis=1)], axis=0)
            z2.append(_bdot(lhs3, rhs2))
        state = [h_scr[pr] for pr in range(n_pairs)]
        for u in nu:
            pr = units[u][1]
            mq = z2[u][:, :pw] + jnp.concatenate([zeros_pair, rr[u]], axis=0)
            out = _bdot(mq, block_diag(state[pr])) + z2[u][:, pw:]
            decay = eye2 * ch[u]["p_last"][:, ls[u]]
            p_mat = jnp.where(left, jnp.sum(jnp.where(left, decay, 0.0), axis=1, keepdims=True),
                              jnp.sum(jnp.where(left, 0.0, decay), axis=1, keepdims=True))
            state[pr] = p_mat * state[pr] + out[:c_len]
            y_scr[ch[u]["rows"], ls[u]] = out[c_len:]
        for pr in range(n_pairs):
            h_scr[pr] = state[pr]
        return carry

    lax.fori_loop(0, tb // (c_len * CHUNKS_PER_ITER), chunk_group, 0)

    y = y_scr[...]
    seg = seg_ref[...]
    inv_n = 1.0 / n
    mu = _head_sums(y, seg, 2) * inv_n
    yc = y - mu
    var = _head_sums(yc * yc, seg, 1) * inv_n
    yn = yc * lax.rsqrt(var + GN_EPS) * lnw_ref[...] + lnb_ref[...]
    bonus = _head_sums(r_ref[...] * k_ref[...] * rk_ref[...], seg, 1) * v_ref[...]
    out_ref[...] = ((yn + bonus) * g_ref[...]).astype(BF16)


def _rwkv(r, k, v, kkn, a, lw, g, b, p):
    t, db = r.shape
    tb = TB_RWKV
    steps = t // b // tb
    tok = pl.BlockSpec((tb, db), lambda bi, i: (bi * steps + i, 0))
    consts = [p["r_k"], p["ln_x_w"], p["ln_x_b"], p["tri"], p["seg_ones"]]
    return pl.pallas_call(
        _rwkv_kernel,
        grid=(b, steps),
        in_specs=[tok] * 7 + [_const_spec(c.shape) for c in consts],
        out_specs=tok,
        out_shape=jax.ShapeDtypeStruct((t, db), BF16),
        scratch_shapes=[pltpu.VMEM((N_HEADS // 2, HEAD_DIM, 2 * HEAD_DIM), F32),
                        pltpu.VMEM((tb, db), F32)],
        compiler_params=pltpu.CompilerParams(dimension_semantics=("arbitrary", "arbitrary"),
                                             vmem_limit_bytes=VMEM_LIMIT),
        name="rwkv",
    )(r, k, v, kkn, a, lw, g, *consts)


def _merge_kernel(x_ref, yc_ref, yr_ref, ym_ref, gmix_ref, wgate_ref, bgate_ref, wbr_ref, wo_ref,
                  gffn_ref, wrt_ref, brt_ref, tril_ref,
                  x1_ref, meta_ref, metat_ref, cnt_ref, base_scr):
    tm, d = x_ref.shape

    @pl.when(pl.program_id(0) == 0)
    def _():
        base_scr[...] = jnp.zeros_like(base_scr)

    sub = tril_ref.shape[0]
    lane = lax.broadcasted_iota(jnp.int32, (sub, LANES), 1)
    neg = jnp.float32(-jnp.inf)
    big = jnp.int32(1 << 20)
    w_hi, w_lo = _split_terms(wrt_ref[...], 2)
    w_hi_lo = jnp.concatenate([w_hi, w_lo], axis=1)
    state = {"base": base_scr[...]}
    merged = {}

    def project(j):
        rs = slice(j * sub, (j + 1) * sub)
        x = x_ref[rs, :]
        hb = _rms(x, gmix_ref[...]).astype(BF16)
        z = jnp.zeros((sub, d), F32)
        for i, y_ref in enumerate((yc_ref, yr_ref, ym_ref)):
            cs = slice(i * d, (i + 1) * d)
            gate = _sigmoid(jnp.dot(hb, wgate_ref[:, cs], preferred_element_type=F32) + bgate_ref[:, cs])
            z = z + gate * jnp.dot(y_ref[rs, :], wbr_ref[i], preferred_element_type=F32)
            yield
        x1 = x + _bdot(z, wo_ref[...])
        x1_ref[rs, :] = x1
        merged[j] = x1

    def route(j):
        rs = slice(j * sub, (j + 1) * sub)
        h2 = _rms(merged.pop(j), gffn_ref[...])
        h_hi, h_lo = _split_terms(h2, 2)
        hi_terms = jnp.dot(h_hi, w_hi_lo, preferred_element_type=F32)
        logits = (hi_terms[:, :LANES]
                  + (jnp.dot(h_lo, w_hi, preferred_element_type=F32) + hi_terms[:, LANES:])) + brt_ref[...]
        yield
        gmask = (lane >= N_EXPERTS) & (lane < N_EXPERTS + N_GROUPS)
        glv = jnp.where(gmask, logits, neg)
        gmax = jnp.max(glv, axis=-1, keepdims=True)
        g_sel = jnp.min(jnp.where(glv == gmax, lane - N_EXPERTS, big), axis=-1, keepdims=True)
        g_w = 1.0 / jnp.sum(jnp.exp(glv - gmax), axis=-1, keepdims=True)
        emask = (lane < N_EXPERTS) & ((lane >> 3) == g_sel)
        elv = jnp.where(emask, logits, neg)
        emax = jnp.max(elv, axis=-1, keepdims=True)
        esum = jnp.sum(jnp.exp(elv - emax), axis=-1, keepdims=True)
        i1 = jnp.min(jnp.where(elv == emax, lane, big), axis=-1, keepdims=True)
        elv2 = jnp.where(lane == i1, neg, elv)
        m2 = jnp.max(elv2, axis=-1, keepdims=True)
        i2 = jnp.min(jnp.where(elv2 == m2, lane, big), axis=-1, keepdims=True)
        p1 = 1.0 / esum
        p2 = jnp.exp(m2 - emax) / esum
        c1 = g_w * p1 / (p1 + p2)
        c2 = g_w * p2 / (p1 + p2)

        oh1 = lane == i1
        oh2 = lane == i2
        onehot = jnp.where(oh1 | oh2, 1.0, 0.0)
        before = jnp.dot(tril_ref[...], onehot.astype(BF16), preferred_element_type=F32) + state["base"]
        yield
        rank1 = jnp.sum(jnp.where(oh1, before, 0.0), axis=-1, keepdims=True)
        rank2 = jnp.sum(jnp.where(oh2, before, 0.0), axis=-1, keepdims=True)
        state["base"] = state["base"] + jnp.sum(onehot, axis=0, keepdims=True)

        meta = jnp.where(lane == 0, i1.astype(F32),
               jnp.where(lane == 1, i2.astype(F32),
               jnp.where(lane == 2, rank1,
               jnp.where(lane == 3, rank2,
               jnp.where(lane == 4, c1,
               jnp.where(lane == 5, c2, 0.0))))))
        meta_ref[rs, :] = meta[:, :META_COLS]
        metat_ref[:, rs] = meta.T[:META_COLS, :]

    _software_pipeline(project, route, tm // sub)
    base_scr[...] = state["base"]
    cnt_ref[...] = jnp.broadcast_to(state["base"], cnt_ref.shape)


def _merge(x2, yc, yr, ym, p):
    t, d = x2.shape
    tm = TM_MERGE
    db = D_BRANCH
    tok = lambda c: pl.BlockSpec((tm, c), lambda i: (i, 0))
    consts = [p["g_mix"], p["w_gate"], p["b_gate"], p["w_branch"], p["w_o"], p["g_ffn"],
              p["w_router"], p["b_router"], p["tril_strict"]]
    return pl.pallas_call(
        _merge_kernel,
        grid=(t // tm,),
        in_specs=[tok(d), tok(db), tok(db), tok(db)] + [_const_spec(c.shape) for c in consts],
        out_specs=[tok(d), tok(META_COLS), pl.BlockSpec((META_COLS, tm), lambda i: (0, i)),
                   _const_spec((8, LANES))],
        out_shape=[jax.ShapeDtypeStruct((t, d), F32), jax.ShapeDtypeStruct((t, META_COLS), F32),
                   jax.ShapeDtypeStruct((META_COLS, t), F32), jax.ShapeDtypeStruct((8, LANES), F32)],
        scratch_shapes=[pltpu.VMEM((1, LANES), F32)],
        compiler_params=pltpu.CompilerParams(dimension_semantics=("arbitrary",),
                                             vmem_limit_bytes=VMEM_LIMIT),
        name="merge",
    )(x2, yc, yr, ym, *consts)


def _store_packed_rows(ref2d, x, stage):
    rows, d = x.shape
    nt = d // (2 * LANES)
    for c in range(nt):
        stage[c, pl.ds(0, rows, stride=2), :] = x[:, c * LANES:(c + 1) * LANES]
        stage[c, pl.ds(1, rows, stride=2), :] = x[:, (c + nt) * LANES:(c + nt + 1) * LANES]
        ref2d[pl.ds(c, rows, stride=nt), :] = pltpu.bitcast(stage[c].astype(BF16), jnp.uint32)


def _load_packed_rows(ref2d, rows, nt, stage):
    lo, hi = [], []
    for c in range(nt):
        stage[c] = pltpu.bitcast(ref2d[pl.ds(c, rows, stride=nt), :], BF16).astype(F32)
        lo.append(stage[c, pl.ds(0, rows, stride=2), :])
        hi.append(stage[c, pl.ds(1, rows, stride=2), :])
    return jnp.concatenate(lo + hi, axis=1)


def _scatter_kernel(dest0_ref, dest1_ref, x1_ref, gffn_ref, xs_ref, hbuf, stage, sem):
    dest_refs = (dest0_ref, dest1_ref)
    ts, d_model = x1_ref.shape
    nt = d_model // (2 * LANES)
    s = pl.program_id(0)
    slot = s % 2

    def wait_slot(sl):
        for _ in range(TOP_K):
            pltpu.make_async_copy(hbuf.at[sl], xs_ref.at[pl.ds(0, ts * nt), :], sem.at[sl]).wait()

    @pl.when(s >= 2)
    def _():
        wait_slot(slot)

    _store_packed_rows(hbuf.at[slot], _rms(x1_ref[...], gffn_ref[...]), stage)

    def issue(grp, carry):
        grp_off = pl.multiple_of(grp * (SUBLANES * nt), SUBLANES * nt)
        for j in range(SUBLANES):
            tok = s * ts + grp * SUBLANES + j
            for kslot in range(TOP_K):
                d = pl.multiple_of(dest_refs[kslot][tok], nt)
                pltpu.make_async_copy(hbuf.at[slot, pl.ds(grp_off + j * nt, nt), :],
                                      xs_ref.at[pl.ds(d, nt), :], sem.at[slot]).start(priority=kslot)
        return carry

    lax.fori_loop(0, ts // SUBLANES, issue, 0, unroll=ISSUE_UNROLL)

    @pl.when(s == pl.num_programs(0) - 1)
    def _():
        @pl.when(s >= 1)
        def _():
            wait_slot(1 - slot)
        wait_slot(slot)


def _scatter(dests, x1, g_ffn, n_rows):
    t, d = x1.shape
    ts = TS_SCATTER
    pt = d // (2 * LANES)
    return pl.pallas_call(
        _scatter_kernel,
        grid_spec=pltpu.PrefetchScalarGridSpec(
            num_scalar_prefetch=TOP_K,
            grid=(t // ts,),
            in_specs=[pl.BlockSpec((ts, d), lambda i, *_: (i, 0)),
                      pl.BlockSpec((1, d), lambda i, *_: (0, 0))],
            out_specs=pl.BlockSpec(memory_space=pl.ANY),
            scratch_shapes=[pltpu.VMEM((2, ts * pt, LANES), jnp.uint32),
                            pltpu.VMEM((pt, 2 * ts, LANES), F32),
                            pltpu.SemaphoreType.DMA((2,))],
        ),
        out_shape=jax.ShapeDtypeStruct((n_rows * pt, LANES), jnp.uint32),
        compiler_params=pltpu.CompilerParams(dimension_semantics=("arbitrary",),
                                             vmem_limit_bytes=VMEM_LIMIT),
        name="scatter",
    )(*dests, x1, g_ffn)


def _experts_kernel(be_ref, nused_ref, nexte_ref, wslot_ref, nvalid_ref, xs_ref, wg_hbm, wu_hbm, wd_hbm, ys_ref,
                    wg_f, wu_f, wd_f, wg_s, wu_s, wd_s, xbuf, ybuf, xstage, ystage, xstage_h, ystage_h,
                    sem, xsem, ysem):
    i = pl.program_id(0)
    e = be_ref[i]
    prev = be_ref[jnp.maximum(i - 1, 0)]
    active = i < nused_ref[0]

    def weight_copies(ex):
        ws = wslot_ref[ex]
        return (pltpu.make_async_copy(wg_hbm.at[ex], wg_f.at[ws], sem.at[ws, 0]),
                pltpu.make_async_copy(wu_hbm.at[ex], wu_f.at[ws], sem.at[ws, 1]),
                pltpu.make_async_copy(wd_hbm.at[ex], wd_f.at[ws], sem.at[ws, 2]))

    def start_weights(ex):
        @pl.when(ex < N_EXPERTS)
        def _():
            for cp in weight_copies(ex):
                cp.start(priority=1)

    @pl.when(i == 0)
    def _():
        start_weights(e)
        start_weights(nexte_ref[e])

    @pl.when(active & ((i == 0) | (e != prev)))
    def _():
        for cp in weight_copies(e):
            cp.wait()
        ws = wslot_ref[e]
        wg_s[...] = wg_f[ws].astype(BF16)
        wu_s[...] = wu_f[ws].astype(BF16)
        wd_s[...] = wd_f[ws].astype(BF16)
        nxt = nexte_ref[e]
        start_weights(jnp.where(nxt < N_EXPERTS, nexte_ref[jnp.minimum(nxt, N_EXPERTS - 1)], N_EXPERTS))

    blk_rows = xbuf.shape[1]
    n_used = nused_ref[0]

    def xs_copy(blk, slot):
        return pltpu.make_async_copy(xs_ref.at[pl.ds(pl.multiple_of(blk * blk_rows, blk_rows), blk_rows), :],
                                     xbuf.at[slot], xsem.at[slot])

    def ys_copy(blk, slot):
        return pltpu.make_async_copy(ybuf.at[slot],
                                     ys_ref.at[pl.ds(pl.multiple_of(blk * blk_rows, blk_rows), blk_rows), :],
                                     ysem.at[slot])

    @pl.when(i == 0)
    def _():
        for ahead in range(XS_BUFFERS - 1):
            @pl.when(ahead < n_used)
            def _():
                xs_copy(ahead, ahead).start()

    @pl.when(active)
    def _():
        nt = wg_s.shape[0] // (2 * LANES)
        fetch = i + (XS_BUFFERS - 1)

        @pl.when(fetch < n_used)
        def _():
            xs_copy(fetch, fetch % XS_BUFFERS).start()

        xs_copy(i, i % XS_BUFFERS).wait()
        oslot = i % 2

        @pl.when(i >= 2)
        def _():
            ys_copy(i - 2, oslot).wait()

        def expert_mlp(rows, x_stage, y_stage):
            xb = _load_packed_rows(xbuf.at[i % XS_BUFFERS], rows, nt, x_stage).astype(BF16)
            gate = jnp.dot(xb, wg_s[...], preferred_element_type=F32)
            up = jnp.dot(xb, wu_s[...], preferred_element_type=F32)
            hid = gate * _sigmoid(gate) * up
            _store_packed_rows(ybuf.at[oslot], jnp.dot(hid.astype(BF16), wd_s[...], preferred_element_type=F32),
                               y_stage)

        half_rows = ROW_BLOCK // 2
        half_only = nvalid_ref[i] <= half_rows

        @pl.when(half_only)
        def _():
            expert_mlp(half_rows, xstage_h, ystage_h)
            ybuf[oslot, pl.ds(half_rows * nt, half_rows * nt), :] = pltpu.bitcast(
                jnp.zeros((2 * half_rows * nt, LANES), BF16), jnp.uint32)

        @pl.when(jnp.logical_not(half_only))
        def _():
            expert_mlp(ROW_BLOCK, xstage, ystage)

        ys_copy(i, oslot).start()

        @pl.when(i == n_used - 1)
        def _():
            @pl.when(i >= 1)
            def _():
                ys_copy(i - 1, 1 - oslot).wait()
            ys_copy(i, oslot).wait()


def _experts(blk_expert, n_used, next_expert, weight_slot, blk_valid, xs, w_gate, w_up, w_down):
    d, de = w_gate.shape[-2:]
    pt = d // (2 * LANES)
    blk_rows = ROW_BLOCK * pt
    nb = xs.shape[0] // blk_rows

    return pl.pallas_call(
        _experts_kernel,
        grid_spec=pltpu.PrefetchScalarGridSpec(
            num_scalar_prefetch=5,
            grid=(nb,),
            in_specs=[pl.BlockSpec(memory_space=pl.ANY)] * 4,
            out_specs=pl.BlockSpec(memory_space=pl.ANY),
            scratch_shapes=[pltpu.VMEM((2, d, de), F32), pltpu.VMEM((2, d, de), F32),
                            pltpu.VMEM((2, de, d), F32),
                            pltpu.VMEM((d, de), BF16), pltpu.VMEM((d, de), BF16), pltpu.VMEM((de, d), BF16),
                            pltpu.VMEM((XS_BUFFERS, blk_rows, LANES), jnp.uint32),
                            pltpu.VMEM((2, blk_rows, LANES), jnp.uint32),
                            pltpu.VMEM((pt, 2 * ROW_BLOCK, LANES), F32),
                            pltpu.VMEM((pt, 2 * ROW_BLOCK, LANES), F32),
                            pltpu.VMEM((pt, ROW_BLOCK, LANES), F32),
                            pltpu.VMEM((pt, ROW_BLOCK, LANES), F32),
                            pltpu.SemaphoreType.DMA((2, 3)), pltpu.SemaphoreType.DMA((XS_BUFFERS,)),
                            pltpu.SemaphoreType.DMA((2,))],
        ),
        out_shape=jax.ShapeDtypeStruct(xs.shape, jnp.uint32),
        compiler_params=pltpu.CompilerParams(dimension_semantics=("arbitrary",),
                                             vmem_limit_bytes=VMEM_LIMIT),
        name="experts",
    )(blk_expert, n_used, next_expert, weight_slot, blk_valid, xs, w_gate, w_up, w_down)


def _combine_kernel(dest0_ref, dest1_ref, x1_ref, meta_ref, gfin_ref, ys_ref, out_ref, ybuf, stage, sem):
    dest_refs = (dest0_ref, dest1_ref)
    te = x1_ref.shape[0]
    s = pl.program_id(0)
    nsteps = pl.num_programs(0)
    slot = s % 2

    nt = x1_ref.shape[1] // (2 * LANES)

    def issue_step(step, sl):
        def issue(grp, carry):
            grp_off = pl.multiple_of(grp * (SUBLANES * nt), SUBLANES * nt)
            for j in range(SUBLANES):
                tok = step * te + grp * SUBLANES + j
                for kslot in range(TOP_K):
                    d = pl.multiple_of(dest_refs[kslot][tok], nt)
                    pltpu.make_async_copy(ys_ref.at[pl.ds(d, nt), :],
                                          ybuf.at[sl, kslot, pl.ds(grp_off + j * nt, nt), :],
                                          sem.at[sl]).start(priority=kslot)
            return carry
        lax.fori_loop(0, te // SUBLANES, issue, 0, unroll=ISSUE_UNROLL)

    @pl.when(s == 0)
    def _():
        issue_step(0, 0)

    @pl.when(s + 1 < nsteps)
    def _():
        issue_step(s + 1, 1 - slot)

    for kslot in range(TOP_K):
        pltpu.make_async_copy(ys_ref.at[pl.ds(0, te * nt), :], ybuf.at[slot, kslot], sem.at[slot]).wait()

    meta = meta_ref[...]
    y0 = _load_packed_rows(ybuf.at[slot, 0], te, nt, stage.at[0])
    y1 = _load_packed_rows(ybuf.at[slot, 1], te, nt, stage.at[1])
    x2 = x1_ref[...] + y0 * meta[:, 4:5] + y1 * meta[:, 5:6]
    out_ref[...] = _rms(x2, gfin_ref[...])


def _combine(dests, x1, meta, g_final, ys):
    t, d = x1.shape
    te = TE_COMBINE
    pt = d // (2 * LANES)
    return pl.pallas_call(
        _combine_kernel,
        grid_spec=pltpu.PrefetchScalarGridSpec(
            num_scalar_prefetch=TOP_K,
            grid=(t // te,),
            in_specs=[pl.BlockSpec((te, d), lambda i, *_: (i, 0)),
                      pl.BlockSpec((te, META_COLS), lambda i, *_: (i, 0)),
                      pl.BlockSpec((1, d), lambda i, *_: (0, 0)),
                      pl.BlockSpec(memory_space=pl.ANY)],
            out_specs=pl.BlockSpec((te, d), lambda i, *_: (i, 0)),
            scratch_shapes=[pltpu.VMEM((2, TOP_K, te * pt, LANES), jnp.uint32),
                            pltpu.VMEM((TOP_K, pt, 2 * te, LANES), F32),
                            pltpu.SemaphoreType.DMA((2,))],
        ),
        out_shape=jax.ShapeDtypeStruct((t, d), F32),
        compiler_params=pltpu.CompilerParams(dimension_semantics=("arbitrary",),
                                             vmem_limit_bytes=VMEM_LIMIT),
        name="combine",
    )(*dests, x1, meta, g_final, ys)


def _constants(tm_merge):
    n = CHUNK
    tri = (jnp.arange(n)[:, None] >= jnp.arange(n)[None, :]).astype(BF16)
    head = jnp.arange(2 * LANES) // HEAD_DIM
    seg_ones = (head[:, None] == head[None, :]).astype(BF16)
    tril_strict = (jnp.arange(tm_merge)[:, None] > jnp.arange(tm_merge)[None, :]).astype(BF16)
    return tri, seg_ones, tril_strict


def kernel(x, mem, g_mix, g_mem, w_in, conv_w, mu_rkv, mu_wag, w_lora1, w_lora2, w0, a_lora1, a_lora2, a0, g_lora1, g_lora2, k_k, k_a, r_k, ln_x_w, ln_x_b, w_kv_mem, w_branch, w_gate, b_gate, w_o, g_ffn, w_router_group, b_router_group, w_router_expert, b_router_expert, w_exp_gate, w_exp_up, w_exp_down, g_final):
    assert g_mix.shape[0] == 1, "single-layer block"
    b, s, d = x.shape
    t = b * s
    db = D_BRANCH
    tri, seg_ones, tril_strict = _constants(MERGE_SUB)
    row = lambda a: a.reshape(1, -1)
    pad_r = LANES - N_EXPERTS - N_GROUPS
    p = {
        "g_mix": row(g_mix[0]), "w_in": w_in[0].astype(BF16), "conv_w": conv_w[0].T,
        "mu_rkv": row(mu_rkv[0]), "mu_wag": mu_wag[0],
        "w_lora1": w_lora1[0].astype(BF16), "w_lora2": w_lora2[0].astype(BF16), "w0": row(w0[0]),
        "a_lora1": a_lora1[0].astype(BF16), "a_lora2": a_lora2[0].astype(BF16), "a0": row(a0[0]),
        "g_lora1": g_lora1[0].astype(BF16), "g_lora2": g_lora2[0].astype(BF16),
        "k_k": row(k_k[0]), "k_a": row(k_a[0]), "r_k": row(r_k[0]),
        "ln_x_w": row(ln_x_w[0]), "ln_x_b": row(ln_x_b[0]),
        "w_gate": w_gate[0].astype(BF16), "b_gate": row(b_gate[0]),
        "w_branch": w_branch[0].astype(BF16), "w_o": w_o[0].astype(BF16), "g_ffn": row(g_ffn[0]),
        "w_router": jnp.concatenate([w_router_expert[0], w_router_group[0],
                                     jnp.zeros((d, pad_r), F32)], axis=1),
        "b_router": row(jnp.concatenate([b_router_expert[0], b_router_group[0],
                                         jnp.zeros((pad_r,), F32)])),
        "tri": tri, "seg_ones": seg_ones, "tril_strict": tril_strict,
    }

    km, vm = _memkv(mem, row(g_mem[0]), w_kv_mem[0].astype(BF16))
    x2 = x.reshape(t, d)
    yconv, ymem, r, k, v, kkn, a, lw, g = _prologue(x2, b, km, vm, p)
    yrwkv = _rwkv(r, k, v, kkn, a, lw, g, b, p)
    x1, meta, meta_t, cnt = _merge(x2, yconv, yrwkv, ymem, p)

    counts = cnt[0, :N_EXPERTS].astype(jnp.int32)
    padded = ((counts + ROW_BLOCK - 1) // ROW_BLOCK) * ROW_BLOCK
    pad_end = jnp.cumsum(padded)
    pad_start = pad_end - padded
    n_blocks = (t * TOP_K) // ROW_BLOCK + N_EXPERTS
    eids = jnp.arange(N_EXPERTS, dtype=jnp.int32)
    e_idx = meta_t[0:TOP_K].astype(jnp.int32)
    rank = meta_t[TOP_K:2 * TOP_K].astype(jnp.int32)
    start_of = jnp.sum(jnp.where(e_idx[:, None, :] == eids[None, :, None], pad_start[None, :, None], 0), axis=1)
    dest = (start_of + rank) * (d // (2 * LANES))
    dests = [dest[kslot] for kslot in range(TOP_K)]
    blk_start = jnp.arange(n_blocks, dtype=jnp.int32) * ROW_BLOCK
    blk_expert = jnp.minimum(jnp.sum((pad_end[None, :] <= blk_start[:, None]).astype(jnp.int32), axis=1),
                             N_EXPERTS - 1)
    n_used = (pad_end[-1:] // ROW_BLOCK).astype(jnp.int32)
    later_nonempty = (eids[None, :] > eids[:, None]) & (counts[None, :] > 0)
    next_expert = jnp.min(jnp.where(later_nonempty, eids[None, :], N_EXPERTS), axis=1)
    weight_slot = (jnp.cumsum((counts > 0).astype(jnp.int32)) - 1) & 1
    of_blk = blk_expert[:, None] == eids[None, :]
    blk_valid = jnp.clip(jnp.sum(jnp.where(of_blk, (pad_start + counts)[None, :], 0), axis=1) - blk_start,
                         0, ROW_BLOCK)

    xs = _scatter(dests, x1, p["g_ffn"], n_blocks * ROW_BLOCK)
    ys = _experts(blk_expert, n_used, next_expert, weight_slot, blk_valid, xs,
                  w_exp_gate[0], w_exp_up[0], w_exp_down[0])
    out = _combine(dests, x1, meta, row(g_final), ys)
    return out.reshape(b, s, d)
```

```python
import functools

import jax
import jax.numpy as jnp
from jax import lax
from jax.experimental import pallas as pl
from jax.experimental.pallas import tpu as pltpu

F32 = jnp.float32
BF16 = jnp.bfloat16

NORM_EPS = 1e-6
GN_EPS = 64e-5
D_BRANCH = 512
HEAD_DIM = 64
N_HEADS = 8
CHUNK = 64
CHUNKS_PER_ITER = 4
MEM_HEADS = 4
MEM_HEAD_DIM = 128
N_GROUPS = 8
EXPERTS_PER_GROUP = 8
N_EXPERTS = 64
TOP_K = 2
ROW_BLOCK = 256
XS_BUFFERS = 4
LANES = 128
VMEM_LIMIT = 56 * 1024 * 1024

TM_PROLOGUE = 512
PROLOGUE_SUB = 256
TB_RWKV = 256
TM_MERGE = 1024
MERGE_SUB = 512
TS_SCATTER = 512
TE_COMBINE = 512
SUBLANES = 8
META_COLS = 8
ISSUE_UNROLL = 2


def _bdot(a, b):
    return jnp.dot(a.astype(BF16), b.astype(BF16), preferred_element_type=F32)


def _bdot_nt(a, b):
    return lax.dot_general(a.astype(BF16), b.astype(BF16), (((1,), (1,)), ((), ())),
                           preferred_element_type=F32)


def _split_terms(x, n_terms):
    terms = []
    for _ in range(n_terms):
        t = x.astype(BF16)
        terms.append(t)
        x = x - t.astype(F32)
    return terms


def _split_dot_left(m_bf16, x, n_terms):
    return sum(jnp.dot(m_bf16, t, preferred_element_type=F32) for t in _split_terms(x, n_terms))


def _head_sums(x, seg_bf16, n_terms):
    w = seg_bf16.shape[0]
    terms = _split_terms(x, n_terms)
    halves = [sum(jnp.dot(t[:, c:c + w], seg_bf16, preferred_element_type=F32) for t in terms)
              for c in range(0, x.shape[1], w)]
    return jnp.concatenate(halves, axis=1)


def _rms(x, g):
    return x * lax.rsqrt(jnp.mean(x * x, axis=-1, keepdims=True) + NORM_EPS) * g


def _sigmoid(x):
    return 1.0 / (1.0 + jnp.exp(-x))


def _run_together(*gens):
    live = list(gens)
    while live:
        for gen in list(live):
            try:
                next(gen)
            except StopIteration:
                live.remove(gen)


def _software_pipeline(heavy, light, n_sub):
    _run_together(heavy(0))
    for j in range(1, n_sub):
        _run_together(heavy(j), light(j - 1))
    _run_together(light(n_sub - 1))


def _const_spec(shape):
    n = len(shape)
    return pl.BlockSpec(shape, lambda *_: (0,) * n)


def _memkv_kernel(mem_ref, g_ref, w_ref, k_ref, v_ref):
    mn = _rms(mem_ref[0], g_ref[...])
    kv = _bdot(mn, w_ref[...])
    k_ref[0] = kv[:, :D_BRANCH].astype(BF16)
    v_ref[0] = kv[:, D_BRANCH:].astype(BF16)


def _memkv(mem, g_mem, w_kv):
    b, m, d = mem.shape
    return pl.pallas_call(
        _memkv_kernel,
        grid=(b,),
        in_specs=[pl.BlockSpec((1, m, d), lambda i: (i, 0, 0)),
                  _const_spec((1, d)), _const_spec((d, 2 * D_BRANCH))],
        out_specs=[pl.BlockSpec((1, m, D_BRANCH), lambda i: (i, 0, 0)),
                   pl.BlockSpec((1, m, D_BRANCH), lambda i: (i, 0, 0))],
        out_shape=[jax.ShapeDtypeStruct((b, m, D_BRANCH), BF16)] * 2,
        compiler_params=pltpu.CompilerParams(dimension_semantics=("arbitrary",),
                                             vmem_limit_bytes=VMEM_LIMIT),
        name="memkv",
    )(mem, g_mem, w_kv)


def _prologue_kernel(x_ref, gmix_ref, win_ref, convw_ref, murkv_ref, muwag_ref,
                     wl1_ref, wl2_ref, w0_ref, al1_ref, al2_ref, a0_ref, gl1_ref, gl2_ref,
                     kk_ref, ka_ref, seg_ref, km_ref, vm_ref,
                     yconv_ref, ymem_ref, r_ref, k_ref, v_ref, kkn_ref, a_ref, lw_ref, g_ref,
                     prev_h, prev_p, prev_cu):
    tm = x_ref.shape[0]
    db = D_BRANCH
    sub = PROLOGUE_SUB
    n_parts = win_ref.shape[1] // db

    @pl.when(pl.program_id(1) == 0)
    def _():
        prev_h[...] = jnp.zeros_like(prev_h)
        prev_p[...] = jnp.zeros_like(prev_p)
        prev_cu[...] = jnp.zeros_like(prev_cu)

    rows = lax.broadcasted_iota(jnp.int32, (sub, 1), 0)

    def shift1(u, prev_row):
        return jnp.where(rows == 0, prev_row, pltpu.roll(u, 1, axis=0))

    carry = {"h": prev_h[...], "p": prev_p[...], "cu": prev_cu[...]}
    projected = {}

    def project(j):
        h = _rms(x_ref[j * sub:(j + 1) * sub, :], gmix_ref[...])
        hb = h.astype(BF16)
        parts = []
        for c in range(n_parts):
            parts.append(jnp.dot(hb, win_ref[:, c * db:(c + 1) * db], preferred_element_type=F32))
            yield
        projected[j] = (h, parts)

    def mix(j):
        rs = slice(j * sub, (j + 1) * sub)
        h, (bg, cg, u, rp, kp, vp, q) = projected.pop(j)

        cu = cg * u
        pcu = carry["cu"]
        cu1 = shift1(cu, pcu[1:2, :])
        cu2 = jnp.where(rows == 0, pcu[0:1, :], jnp.where(rows == 1, pcu[1:2, :], pltpu.roll(cu, 2, axis=0)))
        conv = cu2 * convw_ref[0:1, :] + cu1 * convw_ref[1:2, :] + cu * convw_ref[2:3, :]
        yconv_ref[rs, :] = (bg * conv).astype(BF16)
        carry["cu"] = cu[sub - 2:sub, :]

        pr = jnp.concatenate([rp, kp, vp], axis=1)
        prs = shift1(pr, carry["p"])
        mixed = pr + (prs - pr) * murkv_ref[...]
        carry["p"] = pr[sub - 1:sub, :]
        r, k, v = mixed[:, :db], mixed[:, db:2 * db], mixed[:, 2 * db:]
        r_ref[rs, :] = r
        v_ref[rs, :] = v

        dh = shift1(h, carry["h"]) - h
        carry["h"] = h[sub - 1:sub, :]
        lora_w = _bdot(h + dh * muwag_ref[0:1, :], wl1_ref[...])
        lora_a = _bdot(h + dh * muwag_ref[1:2, :], al1_ref[...])
        lora_g = _bdot(h + dh * muwag_ref[2:3, :], gl1_ref[...])
        yield
        zz = w0_ref[...] + _bdot(jnp.tanh(lora_w), wl2_ref[...])
        a_lin = a0_ref[...] + _bdot(lora_a, al2_ref[...])
        g_ref[rs, :] = _bdot(_sigmoid(lora_g), gl2_ref[...])
        yield
        softplus = jnp.maximum(-zz, 0.0) + jnp.log(1.0 + jnp.exp(-jnp.abs(zz)))
        lw_ref[rs, :] = -jnp.exp(-softplus - 0.5)
        a = _sigmoid(a_lin)
        a_ref[rs, :] = a
        k_ref[rs, :] = k * (1.0 + (a - 1.0) * ka_ref[...])
        kk = k * kk_ref[...]
        ss = _head_sums(kk * kk, seg_ref[...], 1)
        yield
        kkn_ref[rs, :] = kk * lax.rsqrt(jnp.maximum(ss, 1e-24))

        scale = MEM_HEAD_DIM ** -0.5
        heads = [slice(hh * MEM_HEAD_DIM, (hh + 1) * MEM_HEAD_DIM) for hh in range(MEM_HEADS)]
        scores = [_bdot_nt(q[:, sl], km_ref[0, :, sl]) * scale for sl in heads]
        yield
        for sl, s in zip(heads, scores):
            p = jnp.exp(s - jnp.max(s, axis=-1, keepdims=True))
            o = _bdot(p, vm_ref[0, :, sl]) / jnp.sum(p, axis=-1, keepdims=True)
            ymem_ref[rs, sl] = o.astype(BF16)

    _software_pipeline(project, mix, tm // sub)
    prev_h[...] = carry["h"]
    prev_p[...] = carry["p"]
    prev_cu[...] = carry["cu"]


def _prologue(x2, b, km, vm, p):
    t, d = x2.shape
    s = t // b
    tm = TM_PROLOGUE
    db = D_BRANCH
    m = km.shape[1]
    steps = s // tm
    tok = lambda c: pl.BlockSpec((tm, c), lambda bi, i: (bi * steps + i, 0))
    consts = [p["g_mix"], p["w_in"], p["conv_w"], p["mu_rkv"], p["mu_wag"],
              p["w_lora1"], p["w_lora2"], p["w0"], p["a_lora1"], p["a_lora2"], p["a0"],
              p["g_lora1"], p["g_lora2"], p["k_k"], p["k_a"], p["seg_ones"]]
    out_shapes = ([jax.ShapeDtypeStruct((t, db), BF16)] * 2
                  + [jax.ShapeDtypeStruct((t, db), F32)] * 7)
    return pl.pallas_call(
        _prologue_kernel,
        grid=(b, steps),
        in_specs=[tok(d)] + [_const_spec(c.shape) for c in consts]
                 + [pl.BlockSpec((1, m, db), lambda bi, i: (bi, 0, 0))] * 2,
        out_specs=[tok(db)] * 9,
        out_shape=out_shapes,
        scratch_shapes=[pltpu.VMEM((1, d), F32), pltpu.VMEM((1, 3 * db), F32),
                        pltpu.VMEM((2, db), F32)],
        compiler_params=pltpu.CompilerParams(dimension_semantics=("arbitrary", "arbitrary"),
                                             vmem_limit_bytes=VMEM_LIMIT),
        name="prologue",
    )(x2, *consts, km, vm)


def _rwkv_kernel(r_ref, k_ref, v_ref, kk_ref, a_ref, lw_ref, g_ref, rk_ref, lnw_ref, lnb_ref,
                 tri_ref, seg_ref, out_ref, h_scr, y_scr):
    tb = r_ref.shape[0]
    n = HEAD_DIM
    c_len = CHUNK

    @pl.when(pl.program_id(1) == 0)
    def _():
        h_scr[...] = jnp.zeros_like(h_scr)

    pw = 2 * n
    row1 = lax.broadcasted_iota(jnp.int32, (c_len, pw), 0)
    lane1 = lax.broadcasted_iota(jnp.int32, (c_len, pw), 1)
    col1 = lane1 & (n - 1)
    left = lane1 < n
    strict1 = col1 < row1
    incl1 = col1 <= row1
    eye2 = (col1 == row1).astype(F32)
    zeros_pair = jnp.zeros((c_len, pw), F32)
    zeros_bd = jnp.zeros((2 * c_len, pw), F32)

    def block_diag(y):
        return jnp.concatenate([jnp.where(left, y, 0.0), jnp.where(left, 0.0, y)], axis=0)

    def pair_transpose(y):
        zt = block_diag(y).T
        return zt[:c_len] + zt[c_len:]

    def chunk_inputs(c):
        rows = pl.ds(pl.multiple_of(c * c_len, c_len), c_len)
        r = r_ref[rows, :]
        k = k_ref[rows, :]
        v = v_ref[rows, :]
        kk = kk_ref[rows, :]
        a = a_ref[rows, :]
        lw = lw_ref[rows, :]
        gcum = _split_dot_left(tri_ref[...], lw, 2)
        e_pos = jnp.exp(gcum)
        e_neg = jnp.exp(-gcum)
        p_last = jnp.exp(gcum[c_len - 1:c_len, :])
        bb = kk * a * e_neg
        kb = k * e_neg
        return dict(rows=rows, v=v, p_last=p_last, rb=r * e_pos, ab=-kk * jnp.exp(gcum - lw), bb=bb, kb=kb,
                    bbp=bb * p_last, kbp=kb * p_last)

    def chunk_group(it, carry):
        chunks = [chunk_inputs(it * CHUNKS_PER_ITER + ci) for ci in range(CHUNKS_PER_ITER)]
        n_pairs = N_HEADS // 2
        units = [(ci, pr) for ci in range(CHUNKS_PER_ITER) for pr in range(n_pairs)]
        nu = range(len(units))
        ls = [slice(pr * pw, (pr + 1) * pw) for _, pr in units]
        ch = [chunks[ci] for ci, _ in units]
        al = [ch[u]["ab"][:, ls[u]] for u in nu]
        rr = [ch[u]["rb"][:, ls[u]] for u in nu]
        v_bd = [block_diag(ch[u]["v"][:, ls[u]]) for u in nu]
        aa = [_bdot_nt(jnp.concatenate([al[u], rr[u]], axis=0),
                       jnp.concatenate([block_diag(ch[u]["bb"][:, ls[u]]), block_diag(ch[u]["kb"][:, ls[u]])],
                                       axis=0)) for u in nu]
        a_ab = [jnp.where(strict1, aa[u][:c_len, :pw], 0.0) for u in nu]
        a_ak = [jnp.where(strict1, aa[u][:c_len, pw:], 0.0) for u in nu]
        a_rb = [jnp.where(incl1, aa[u][c_len:, :pw], 0.0) for u in nu]
        a_rk = [jnp.where(incl1, aa[u][c_len:, pw:], 0.0) for u in nu]
        av = [_bdot(a_ak[u], v_bd[u]) for u in nu]
        t_inv = [eye2 + a_ab[u] for u in nu]
        x_pow = [_bdot(a_ab[u], block_diag(a_ab[u])) for u in nu]
        for lvl in range(5):
            if lvl < 4:
                z = [_bdot(jnp.concatenate([t_inv[u], x_pow[u]], axis=0), block_diag(x_pow[u])) for u in nu]
                t_inv = [t_inv[u] + z[u][:c_len] for u in nu]
                x_pow = [z[u][c_len:] for u in nu]
            else:
                t_inv = [t_inv[u] + _bdot(t_inv[u], block_diag(x_pow[u])) for u in nu]
        w12 = [_bdot(t_inv[u], jnp.concatenate([block_diag(al[u]), block_diag(av[u])], axis=1))
               for u in nu]
        z2 = []
        for u in nu:
            rhs2 = jnp.concatenate(
                [jnp.concatenate([block_diag(w12[u][:, :pw]), block_diag(w12[u][:, pw:])], axis=1),
                 jnp.concatenate([zeros_bd, v_bd[u]], axis=1)], axis=0)
            lhs3 = jnp.concatenate(
                [jnp.concatenate([pair_transpose(ch[u]["bbp"][:, ls[u]]),
                                  pair_transpose(ch[u]["kbp"][:, ls[u]])], axis=1),
                 jnp.concatenate([a_rb[u], a_rk[u]], axis=1)], axis=0)
            z2.append(_bdot(lhs3, rhs2))
        state = [h_scr[pr] for pr in range(n_pairs)]
        for u in nu:
            pr = units[u][1]
            mq = z2[u][:, :pw] + jnp.concatenate([zeros_pair, rr[u]], axis=0)
            out = _bdot(mq, block_diag(state[pr])) + z2[u][:, pw:]
            decay = eye2 * ch[u]["p_last"][:, ls[u]]
            p_mat = jnp.where(left, jnp.sum(jnp.where(left, decay, 0.0), axis=1, keepdims=True),
                              jnp.sum(jnp.where(left, 0.0, decay), axis=1, keepdims=True))
            state[pr] = p_mat * state[pr] + out[:c_len]
            y_scr[ch[u]["rows"], ls[u]] = out[c_len:]
        for pr in range(n_pairs):
            h_scr[pr] = state[pr]
        return carry

    lax.fori_loop(0, tb // (c_len * CHUNKS_PER_ITER), chunk_group, 0)

    y = y_scr[...]
    seg = seg_ref[...]
    inv_n = 1.0 / n
    mu = _head_sums(y, seg, 2) * inv_n
    yc = y - mu
    var = _head_sums(yc * yc, seg, 1) * inv_n
    yn = yc * lax.rsqrt(var + GN_EPS) * lnw_ref[...] + lnb_ref[...]
    bonus = _head_sums(r_ref[...] * k_ref[...] * rk_ref[...], seg, 1) * v_ref[...]
    out_ref[...] = ((yn + bonus) * g_ref[...]).astype(BF16)


def _rwkv(r, k, v, kkn, a, lw, g, b, p):
    t, db = r.shape
    tb = TB_RWKV
    steps = t // b // tb
    tok = pl.BlockSpec((tb, db), lambda bi, i: (bi * steps + i, 0))
    consts = [p["r_k"], p["ln_x_w"], p["ln_x_b"], p["tri"], p["seg_ones"]]
    return pl.pallas_call(
        _rwkv_kernel,
        grid=(b, steps),
        in_specs=[tok] * 7 + [_const_spec(c.shape) for c in consts],
        out_specs=tok,
        out_shape=jax.ShapeDtypeStruct((t, db), BF16),
        scratch_shapes=[pltpu.VMEM((N_HEADS // 2, HEAD_DIM, 2 * HEAD_DIM), F32),
                        pltpu.VMEM((tb, db), F32)],
        compiler_params=pltpu.CompilerParams(dimension_semantics=("arbitrary", "arbitrary"),
                                             vmem_limit_bytes=VMEM_LIMIT),
        name="rwkv",
    )(r, k, v, kkn, a, lw, g, *consts)


def _merge_kernel(x_ref, yc_ref, yr_ref, ym_ref, gmix_ref, wgate_ref, bgate_ref, wbr_ref, wo_ref,
                  gffn_ref, wrt_ref, brt_ref, tril_ref,
                  x1_ref, meta_ref, metat_ref, cnt_ref, base_scr):
    tm, d = x_ref.shape

    @pl.when(pl.program_id(0) == 0)
    def _():
        base_scr[...] = jnp.zeros_like(base_scr)

    sub = tril_ref.shape[0]
    lane = lax.broadcasted_iota(jnp.int32, (sub, LANES), 1)
    neg = jnp.float32(-jnp.inf)
    big = jnp.int32(1 << 20)
    w_hi, w_lo = _split_terms(wrt_ref[...], 2)
    w_hi_lo = jnp.concatenate([w_hi, w_lo], axis=1)
    state = {"base": base_scr[...]}
    merged = {}

    def project(j):
        rs = slice(j * sub, (j + 1) * sub)
        x = x_ref[rs, :]
        hb = _rms(x, gmix_ref[...]).astype(BF16)
        z = jnp.zeros((sub, d), F32)
        for i, y_ref in enumerate((yc_ref, yr_ref, ym_ref)):
            cs = slice(i * d, (i + 1) * d)
            gate = _sigmoid(jnp.dot(hb, wgate_ref[:, cs], preferred_element_type=F32) + bgate_ref[:, cs])
            z = z + gate * jnp.dot(y_ref[rs, :], wbr_ref[i], preferred_element_type=F32)
            yield
        x1 = x + _bdot(z, wo_ref[...])
        x1_ref[rs, :] = x1
        merged[j] = x1

    def route(j):
        rs = slice(j * sub, (j + 1) * sub)
        h2 = _rms(merged.pop(j), gffn_ref[...])
        h_hi, h_lo = _split_terms(h2, 2)
        hi_terms = jnp.dot(h_hi, w_hi_lo, preferred_element_type=F32)
        logits = (hi_terms[:, :LANES]
                  + (jnp.dot(h_lo, w_hi, preferred_element_type=F32) + hi_terms[:, LANES:])) + brt_ref[...]
        yield
        gmask = (lane >= N_EXPERTS) & (lane < N_EXPERTS + N_GROUPS)
        glv = jnp.where(gmask, logits, neg)
        gmax = jnp.max(glv, axis=-1, keepdims=True)
        g_sel = jnp.min(jnp.where(glv == gmax, lane - N_EXPERTS, big), axis=-1, keepdims=True)
        g_w = 1.0 / jnp.sum(jnp.exp(glv - gmax), axis=-1, keepdims=True)
        emask = (lane < N_EXPERTS) & ((lane >> 3) == g_sel)
        elv = jnp.where(emask, logits, neg)
        emax = jnp.max(elv, axis=-1, keepdims=True)
        esum = jnp.sum(jnp.exp(elv - emax), axis=-1, keepdims=True)
        i1 = jnp.min(jnp.where(elv == emax, lane, big), axis=-1, keepdims=True)
        elv2 = jnp.where(lane == i1, neg, elv)
        m2 = jnp.max(elv2, axis=-1, keepdims=True)
        i2 = jnp.min(jnp.where(elv2 == m2, lane, big), axis=-1, keepdims=True)
        p1 = 1.0 / esum
        p2 = jnp.exp(m2 - emax) / esum
        c1 = g_w * p1 / (p1 + p2)
        c2 = g_w * p2 / (p1 + p2)

        oh1 = lane == i1
        oh2 = lane == i2
        onehot = jnp.where(oh1 | oh2, 1.0, 0.0)
        before = jnp.dot(tril_ref[...], onehot.astype(BF16), preferred_element_type=F32) + state["base"]
        yield
        rank1 = jnp.sum(jnp.where(oh1, before, 0.0), axis=-1, keepdims=True)
        rank2 = jnp.sum(jnp.where(oh2, before, 0.0), axis=-1, keepdims=True)
        state["base"] = state["base"] + jnp.sum(onehot, axis=0, keepdims=True)

        meta = jnp.where(lane == 0, i1.astype(F32),
               jnp.where(lane == 1, i2.astype(F32),
               jnp.where(lane == 2, rank1,
               jnp.where(lane == 3, rank2,
               jnp.where(lane == 4, c1,
               jnp.where(lane == 5, c2, 0.0))))))
        meta_ref[rs, :] = meta[:, :META_COLS]
        metat_ref[:, rs] = meta.T[:META_COLS, :]

    _software_pipeline(project, route, tm // sub)
    base_scr[...] = state["base"]
    cnt_ref[...] = jnp.broadcast_to(state["base"], cnt_ref.shape)


def _merge(x2, yc, yr, ym, p):
    t, d = x2.shape
    tm = TM_MERGE
    db = D_BRANCH
    tok = lambda c: pl.BlockSpec((tm, c), lambda i: (i, 0))
    consts = [p["g_mix"], p["w_gate"], p["b_gate"], p["w_branch"], p["w_o"], p["g_ffn"],
              p["w_router"], p["b_router"], p["tril_strict"]]
    return pl.pallas_call(
        _merge_kernel,
        grid=(t // tm,),
        in_specs=[tok(d), tok(db), tok(db), tok(db)] + [_const_spec(c.shape) for c in consts],
        out_specs=[tok(d), tok(META_COLS), pl.BlockSpec((META_COLS, tm), lambda i: (0, i)),
                   _const_spec((8, LANES))],
        out_shape=[jax.ShapeDtypeStruct((t, d), F32), jax.ShapeDtypeStruct((t, META_COLS), F32),
                   jax.ShapeDtypeStruct((META_COLS, t), F32), jax.ShapeDtypeStruct((8, LANES), F32)],
        scratch_shapes=[pltpu.VMEM((1, LANES), F32)],
        compiler_params=pltpu.CompilerParams(dimension_semantics=("arbitrary",),
                                             vmem_limit_bytes=VMEM_LIMIT),
        name="merge",
    )(x2, yc, yr, ym, *consts)


def _store_packed_rows(ref2d, x, stage):
    rows, d = x.shape
    nt = d // (2 * LANES)
    for c in range(nt):
        stage[c, pl.ds(0, rows, stride=2), :] = x[:, c * LANES:(c + 1) * LANES]
        stage[c, pl.ds(1, rows, stride=2), :] = x[:, (c + nt) * LANES:(c + nt + 1) * LANES]
        ref2d[pl.ds(c, rows, stride=nt), :] = pltpu.bitcast(stage[c].astype(BF16), jnp.uint32)


def _load_packed_rows(ref2d, rows, nt, stage):
    lo, hi = [], []
    for c in range(nt):
        stage[c] = pltpu.bitcast(ref2d[pl.ds(c, rows, stride=nt), :], BF16).astype(F32)
        lo.append(stage[c, pl.ds(0, rows, stride=2), :])
        hi.append(stage[c, pl.ds(1, rows, stride=2), :])
    return jnp.concatenate(lo + hi, axis=1)


def _scatter_kernel(dest0_ref, dest1_ref, x1_ref, gffn_ref, xs_ref, hbuf, stage, sem):
    dest_refs = (dest0_ref, dest1_ref)
    ts, d_model = x1_ref.shape
    nt = d_model // (2 * LANES)
    s = pl.program_id(0)
    slot = s % 2

    def wait_slot(sl):
        for _ in range(TOP_K):
            pltpu.make_async_copy(hbuf.at[sl], xs_ref.at[pl.ds(0, ts * nt), :], sem.at[sl]).wait()

    @pl.when(s >= 2)
    def _():
        wait_slot(slot)

    _store_packed_rows(hbuf.at[slot], _rms(x1_ref[...], gffn_ref[...]), stage)

    def issue(grp, carry):
        grp_off = pl.multiple_of(grp * (SUBLANES * nt), SUBLANES * nt)
        for j in range(SUBLANES):
            tok = s * ts + grp * SUBLANES + j
            for kslot in range(TOP_K):
                d = pl.multiple_of(dest_refs[kslot][tok], nt)
                pltpu.make_async_copy(hbuf.at[slot, pl.ds(grp_off + j * nt, nt), :],
                                      xs_ref.at[pl.ds(d, nt), :], sem.at[slot]).start(priority=kslot)
        return carry

    lax.fori_loop(0, ts // SUBLANES, issue, 0, unroll=ISSUE_UNROLL)

    @pl.when(s == pl.num_programs(0) - 1)
    def _():
        @pl.when(s >= 1)
        def _():
            wait_slot(1 - slot)
        wait_slot(slot)


def _scatter(dests, x1, g_ffn, n_rows):
    t, d = x1.shape
    ts = TS_SCATTER
    pt = d // (2 * LANES)
    return pl.pallas_call(
        _scatter_kernel,
        grid_spec=pltpu.PrefetchScalarGridSpec(
            num_scalar_prefetch=TOP_K,
            grid=(t // ts,),
            in_specs=[pl.BlockSpec((ts, d), lambda i, *_: (i, 0)),
                      pl.BlockSpec((1, d), lambda i, *_: (0, 0))],
            out_specs=pl.BlockSpec(memory_space=pl.ANY),
            scratch_shapes=[pltpu.VMEM((2, ts * pt, LANES), jnp.uint32),
                            pltpu.VMEM((pt, 2 * ts, LANES), F32),
                            pltpu.SemaphoreType.DMA((2,))],
        ),
        out_shape=jax.ShapeDtypeStruct((n_rows * pt, LANES), jnp.uint32),
        compiler_params=pltpu.CompilerParams(dimension_semantics=("arbitrary",),
                                             vmem_limit_bytes=VMEM_LIMIT),
        name="scatter",
    )(*dests, x1, g_ffn)


def _experts_kernel(be_ref, nused_ref, nexte_ref, wslot_ref, nvalid_ref, xs_ref, wg_hbm, wu_hbm, wd_hbm, ys_ref,
                    wg_f, wu_f, wd_f, wg_s, wu_s, wd_s, xbuf, ybuf, xstage, ystage, xstage_h, ystage_h,
                    sem, xsem, ysem):
    i = pl.program_id(0)
    e = be_ref[i]
    prev = be_ref[jnp.maximum(i - 1, 0)]
    active = i < nused_ref[0]

    def weight_copies(ex):
        ws = wslot_ref[ex]
        return (pltpu.make_async_copy(wg_hbm.at[ex], wg_f.at[ws], sem.at[ws, 0]),
                pltpu.make_async_copy(wu_hbm.at[ex], wu_f.at[ws], sem.at[ws, 1]),
                pltpu.make_async_copy(wd_hbm.at[ex], wd_f.at[ws], sem.at[ws, 2]))

    def start_weights(ex):
        @pl.when(ex < N_EXPERTS)
        def _():
            for cp in weight_copies(ex):
                cp.start(priority=1)

    @pl.when(i == 0)
    def _():
        start_weights(e)
        start_weights(nexte_ref[e])

    @pl.when(active & ((i == 0) | (e != prev)))
    def _():
        for cp in weight_copies(e):
            cp.wait()
        ws = wslot_ref[e]
        wg_s[...] = wg_f[ws].astype(BF16)
        wu_s[...] = wu_f[ws].astype(BF16)
        wd_s[...] = wd_f[ws].astype(BF16)
        nxt = nexte_ref[e]
        start_weights(jnp.where(nxt < N_EXPERTS, nexte_ref[jnp.minimum(nxt, N_EXPERTS - 1)], N_EXPERTS))

    blk_rows = xbuf.shape[1]
    n_used = nused_ref[0]

    def xs_copy(blk, slot):
        return pltpu.make_async_copy(xs_ref.at[pl.ds(pl.multiple_of(blk * blk_rows, blk_rows), blk_rows), :],
                                     xbuf.at[slot], xsem.at[slot])

    def ys_copy(blk, slot):
        return pltpu.make_async_copy(ybuf.at[slot],
                                     ys_ref.at[pl.ds(pl.multiple_of(blk * blk_rows, blk_rows), blk_rows), :],
                                     ysem.at[slot])

    @pl.when(i == 0)
    def _():
        for ahead in range(XS_BUFFERS - 1):
            @pl.when(ahead < n_used)
            def _():
                xs_copy(ahead, ahead).start()

    @pl.when(active)
    def _():
        nt = wg_s.shape[0] // (2 * LANES)
        fetch = i + (XS_BUFFERS - 1)

        @pl.when(fetch < n_used)
        def _():
            xs_copy(fetch, fetch % XS_BUFFERS).start()

        xs_copy(i, i % XS_BUFFERS).wait()
        oslot = i % 2

        @pl.when(i >= 2)
        def _():
            ys_copy(i - 2, oslot).wait()

        def expert_mlp(rows, x_stage, y_stage):
            xb = _load_packed_rows(xbuf.at[i % XS_BUFFERS], rows, nt, x_stage).astype(BF16)
            gate = jnp.dot(xb, wg_s[...], preferred_element_type=F32)
            up = jnp.dot(xb, wu_s[...], preferred_element_type=F32)
            hid = gate * _sigmoid(gate) * up
            _store_packed_rows(ybuf.at[oslot], jnp.dot(hid.astype(BF16), wd_s[...], preferred_element_type=F32),
                               y_stage)

        half_rows = ROW_BLOCK // 2
        half_only = nvalid_ref[i] <= half_rows

        @pl.when(half_only)
        def _():
            expert_mlp(half_rows, xstage_h, ystage_h)
            ybuf[oslot, pl.ds(half_rows * nt, half_rows * nt), :] = pltpu.bitcast(
                jnp.zeros((2 * half_rows * nt, LANES), BF16), jnp.uint32)

        @pl.when(jnp.logical_not(half_only))
        def _():
            expert_mlp(ROW_BLOCK, xstage, ystage)

        ys_copy(i, oslot).start()

        @pl.when(i == n_used - 1)
        def _():
            @pl.when(i >= 1)
            def _():
                ys_copy(i - 1, 1 - oslot).wait()
            ys_copy(i, oslot).wait()


def _experts(blk_expert, n_used, next_expert, weight_slot, blk_valid, xs, w_gate, w_up, w_down):
    d, de = w_gate.shape[-2:]
    pt = d // (2 * LANES)
    blk_rows = ROW_BLOCK * pt
    nb = xs.shape[0] // blk_rows

    return pl.pallas_call(
        _experts_kernel,
        grid_spec=pltpu.PrefetchScalarGridSpec(
            num_scalar_prefetch=5,
            grid=(nb,),
            in_specs=[pl.BlockSpec(memory_space=pl.ANY)] * 4,
            out_specs=pl.BlockSpec(memory_space=pl.ANY),
            scratch_shapes=[pltpu.VMEM((2, d, de), F32), pltpu.VMEM((2, d, de), F32),
                            pltpu.VMEM((2, de, d), F32),
                            pltpu.VMEM((d, de), BF16), pltpu.VMEM((d, de), BF16), pltpu.VMEM((de, d), BF16),
                            pltpu.VMEM((XS_BUFFERS, blk_rows, LANES), jnp.uint32),
                            pltpu.VMEM((2, blk_rows, LANES), jnp.uint32),
                            pltpu.VMEM((pt, 2 * ROW_BLOCK, LANES), F32),
                            pltpu.VMEM((pt, 2 * ROW_BLOCK, LANES), F32),
                            pltpu.VMEM((pt, ROW_BLOCK, LANES), F32),
                            pltpu.VMEM((pt, ROW_BLOCK, LANES), F32),
                            pltpu.SemaphoreType.DMA((2, 3)), pltpu.SemaphoreType.DMA((XS_BUFFERS,)),
                            pltpu.SemaphoreType.DMA((2,))],
        ),
        out_shape=jax.ShapeDtypeStruct(xs.shape, jnp.uint32),
        compiler_params=pltpu.CompilerParams(dimension_semantics=("arbitrary",),
                                             vmem_limit_bytes=VMEM_LIMIT),
        name="experts",
    )(blk_expert, n_used, next_expert, weight_slot, blk_valid, xs, w_gate, w_up, w_down)


def _combine_kernel(dest0_ref, dest1_ref, x1_ref, meta_ref, gfin_ref, ys_ref, out_ref, ybuf, stage, sem):
    dest_refs = (dest0_ref, dest1_ref)
    te = x1_ref.shape[0]
    s = pl.program_id(0)
    nsteps = pl.num_programs(0)
    slot = s % 2

    nt = x1_ref.shape[1] // (2 * LANES)

    def issue_step(step, sl):
        def issue(grp, carry):
            grp_off = pl.multiple_of(grp * (SUBLANES * nt), SUBLANES * nt)
            for j in range(SUBLANES):
                tok = step * te + grp * SUBLANES + j
                for kslot in range(TOP_K):
                    d = pl.multiple_of(dest_refs[kslot][tok], nt)
                    pltpu.make_async_copy(ys_ref.at[pl.ds(d, nt), :],
                                          ybuf.at[sl, kslot, pl.ds(grp_off + j * nt, nt), :],
                                          sem.at[sl]).start(priority=kslot)
            return carry
        lax.fori_loop(0, te // SUBLANES, issue, 0, unroll=ISSUE_UNROLL)

    @pl.when(s == 0)
    def _():
        issue_step(0, 0)

    @pl.when(s + 1 < nsteps)
    def _():
        issue_step(s + 1, 1 - slot)

    for kslot in range(TOP_K):
        pltpu.make_async_copy(ys_ref.at[pl.ds(0, te * nt), :], ybuf.at[slot, kslot], sem.at[slot]).wait()

    meta = meta_ref[...]
    y0 = _load_packed_rows(ybuf.at[slot, 0], te, nt, stage.at[0])
    y1 = _load_packed_rows(ybuf.at[slot, 1], te, nt, stage.at[1])
    x2 = x1_ref[...] + y0 * meta[:, 4:5] + y1 * meta[:, 5:6]
    out_ref[...] = _rms(x2, gfin_ref[...])


def _combine(dests, x1, meta, g_final, ys):
    t, d = x1.shape
    te = TE_COMBINE
    pt = d // (2 * LANES)
    return pl.pallas_call(
        _combine_kernel,
        grid_spec=pltpu.PrefetchScalarGridSpec(
            num_scalar_prefetch=TOP_K,
            grid=(t // te,),
            in_specs=[pl.BlockSpec((te, d), lambda i, *_: (i, 0)),
                      pl.BlockSpec((te, META_COLS), lambda i, *_: (i, 0)),
                      pl.BlockSpec((1, d), lambda i, *_: (0, 0)),
                      pl.BlockSpec(memory_space=pl.ANY)],
            out_specs=pl.BlockSpec((te, d), lambda i, *_: (i, 0)),
            scratch_shapes=[pltpu.VMEM((2, TOP_K, te * pt, LANES), jnp.uint32),
                            pltpu.VMEM((TOP_K, pt, 2 * te, LANES), F32),
                            pltpu.SemaphoreType.DMA((2,))],
        ),
        out_shape=jax.ShapeDtypeStruct((t, d), F32),
        compiler_params=pltpu.CompilerParams(dimension_semantics=("arbitrary",),
                                             vmem_limit_bytes=VMEM_LIMIT),
        name="combine",
    )(*dests, x1, meta, g_final, ys)


def _constants(tm_merge):
    n = CHUNK
    tri = (jnp.arange(n)[:, None] >= jnp.arange(n)[None, :]).astype(BF16)
    head = jnp.arange(2 * LANES) // HEAD_DIM
    seg_ones = (head[:, None] == head[None, :]).astype(BF16)
    tril_strict = (jnp.arange(tm_merge)[:, None] > jnp.arange(tm_merge)[None, :]).astype(BF16)
    return tri, seg_ones, tril_strict


def kernel(x, mem, g_mix, g_mem, w_in, conv_w, mu_rkv, mu_wag, w_lora1, w_lora2, w0, a_lora1, a_lora2, a0, g_lora1, g_lora2, k_k, k_a, r_k, ln_x_w, ln_x_b, w_kv_mem, w_branch, w_gate, b_gate, w_o, g_ffn, w_router_group, b_router_group, w_router_expert, b_router_expert, w_exp_gate, w_exp_up, w_exp_down, g_final):
    assert g_mix.shape[0] == 1, "single-layer block"
    b, s, d = x.shape
    t = b * s
    db = D_BRANCH
    tri, seg_ones, tril_strict = _constants(MERGE_SUB)
    row = lambda a: a.reshape(1, -1)
    pad_r = LANES - N_EXPERTS - N_GROUPS
    p = {
        "g_mix": row(g_mix[0]), "w_in": w_in[0].astype(BF16), "conv_w": conv_w[0].T,
        "mu_rkv": row(mu_rkv[0]), "mu_wag": mu_wag[0],
        "w_lora1": w_lora1[0].astype(BF16), "w_lora2": w_lora2[0].astype(BF16), "w0": row(w0[0]),
        "a_lora1": a_lora1[0].astype(BF16), "a_lora2": a_lora2[0].astype(BF16), "a0": row(a0[0]),
        "g_lora1": g_lora1[0].astype(BF16), "g_lora2": g_lora2[0].astype(BF16),
        "k_k": row(k_k[0]), "k_a": row(k_a[0]), "r_k": row(r_k[0]),
        "ln_x_w": row(ln_x_w[0]), "ln_x_b": row(ln_x_b[0]),
        "w_gate": w_gate[0].astype(BF16), "b_gate": row(b_gate[0]),
        "w_branch": w_branch[0].astype(BF16), "w_o": w_o[0].astype(BF16), "g_ffn": row(g_ffn[0]),
        "w_router": jnp.concatenate([w_router_expert[0], w_router_group[0],
                                     jnp.zeros((d, pad_r), F32)], axis=1),
        "b_router": row(jnp.concatenate([b_router_expert[0], b_router_group[0],
                                         jnp.zeros((pad_r,), F32)])),
        "tri": tri, "seg_ones": seg_ones, "tril_strict": tril_strict,
    }

    km, vm = _memkv(mem, row(g_mem[0]), w_kv_mem[0].astype(BF16))
    x2 = x.reshape(t, d)
    yconv, ymem, r, k, v, kkn, a, lw, g = _prologue(x2, b, km, vm, p)
    yrwkv = _rwkv(r, k, v, kkn, a, lw, g, b, p)
    x1, meta, meta_t, cnt = _merge(x2, yconv, yrwkv, ymem, p)

    counts = cnt[0, :N_EXPERTS].astype(jnp.int32)
    padded = ((counts + ROW_BLOCK - 1) // ROW_BLOCK) * ROW_BLOCK
    pad_end = jnp.cumsum(padded)
    pad_start = pad_end - padded
    n_blocks = (t * TOP_K) // ROW_BLOCK + N_EXPERTS
    eids = jnp.arange(N_EXPERTS, dtype=jnp.int32)
    e_idx = meta_t[0:TOP_K].astype(jnp.int32)
    rank = meta_t[TOP_K:2 * TOP_K].astype(jnp.int32)
    start_of = jnp.sum(jnp.where(e_idx[:, None, :] == eids[None, :, None], pad_start[None, :, None], 0), axis=1)
    dest = (start_of + rank) * (d // (2 * LANES))
    dests = [dest[kslot] for kslot in range(TOP_K)]
    blk_start = jnp.arange(n_blocks, dtype=jnp.int32) * ROW_BLOCK
    blk_expert = jnp.minimum(jnp.sum((pad_end[None, :] <= blk_start[:, None]).astype(jnp.int32), axis=1),
                             N_EXPERTS - 1)
    n_used = (pad_end[-1:] // ROW_BLOCK).astype(jnp.int32)
    later_nonempty = (eids[None, :] > eids[:, None]) & (counts[None, :] > 0)
    next_expert = jnp.min(jnp.where(later_nonempty, eids[None, :], N_EXPERTS), axis=1)
    weight_slot = (jnp.cumsum((counts > 0).astype(jnp.int32)) - 1) & 1
    of_blk = blk_expert[:, None] == eids[None, :]
    blk_valid = jnp.clip(jnp.sum(jnp.where(of_blk, (pad_start + counts)[None, :], 0), axis=1) - blk_start,
                         0, ROW_BLOCK)

    xs = _scatter(dests, x1, p["g_ffn"], n_blocks * ROW_BLOCK)
    ys = _experts(blk_expert, n_used, next_expert, weight_slot, blk_valid, xs,
                  w_exp_gate[0], w_exp_up[0], w_exp_down[0])
    out = _combine(dests, x1, meta, row(g_final), ys)
    return out.reshape(b, s, d)
```

```python
import functools

import jax
import jax.numpy as jnp
from jax import lax
from jax.experimental import pallas as pl
from jax.experimental.pallas import tpu as pltpu

F32 = jnp.float32
BF16 = jnp.bfloat16

NORM_EPS = 1e-6
GN_EPS = 64e-5
D_BRANCH = 512
HEAD_DIM = 64
N_HEADS = 8
CHUNK = 64
CHUNKS_PER_ITER = 4
MEM_HEADS = 4
MEM_HEAD_DIM = 128
N_GROUPS = 8
EXPERTS_PER_GROUP = 8
N_EXPERTS = 64
TOP_K = 2
ROW_BLOCK = 256
XS_BUFFERS = 4
LANES = 128
VMEM_LIMIT = 56 * 1024 * 1024

TM_PROLOGUE = 512
PROLOGUE_SUB = 256
TB_RWKV = 256
TM_MERGE = 1024
MERGE_SUB = 512
TS_SCATTER = 1024
TE_COMBINE = 256
SUBLANES = 8
META_COLS = 8
ISSUE_UNROLL = 2


def _bdot(a, b):
    return jnp.dot(a.astype(BF16), b.astype(BF16), preferred_element_type=F32)


def _bdot_nt(a, b):
    return lax.dot_general(a.astype(BF16), b.astype(BF16), (((1,), (1,)), ((), ())),
                           preferred_element_type=F32)


def _split_terms(x, n_terms):
    terms = []
    for _ in range(n_terms):
        t = x.astype(BF16)
        terms.append(t)
        x = x - t.astype(F32)
    return terms


def _split_dot_left(m_bf16, x, n_terms):
    return sum(jnp.dot(m_bf16, t, preferred_element_type=F32) for t in _split_terms(x, n_terms))


def _head_sums(x, seg_bf16, n_terms):
    w = seg_bf16.shape[0]
    terms = _split_terms(x, n_terms)
    halves = [sum(jnp.dot(t[:, c:c + w], seg_bf16, preferred_element_type=F32) for t in terms)
              for c in range(0, x.shape[1], w)]
    return jnp.concatenate(halves, axis=1)


def _rms(x, g):
    return x * lax.rsqrt(jnp.mean(x * x, axis=-1, keepdims=True) + NORM_EPS) * g


def _sigmoid(x):
    return 1.0 / (1.0 + jnp.exp(-x))


def _run_together(*gens):
    live = list(gens)
    while live:
        for gen in list(live):
            try:
                next(gen)
            except StopIteration:
                live.remove(gen)


def _software_pipeline(heavy, light, n_sub):
    _run_together(heavy(0))
    for j in range(1, n_sub):
        _run_together(heavy(j), light(j - 1))
    _run_together(light(n_sub - 1))


def _const_spec(shape):
    n = len(shape)
    return pl.BlockSpec(shape, lambda *_: (0,) * n)


def _memkv_kernel(mem_ref, g_ref, w_ref, k_ref, v_ref):
    mn = _rms(mem_ref[0], g_ref[...])
    kv = _bdot(mn, w_ref[...])
    k_ref[0] = kv[:, :D_BRANCH].astype(BF16)
    v_ref[0] = kv[:, D_BRANCH:].astype(BF16)


def _memkv(mem, g_mem, w_kv):
    b, m, d = mem.shape
    return pl.pallas_call(
        _memkv_kernel,
        grid=(b,),
        in_specs=[pl.BlockSpec((1, m, d), lambda i: (i, 0, 0)),
                  _const_spec((1, d)), _const_spec((d, 2 * D_BRANCH))],
        out_specs=[pl.BlockSpec((1, m, D_BRANCH), lambda i: (i, 0, 0)),
                   pl.BlockSpec((1, m, D_BRANCH), lambda i: (i, 0, 0))],
        out_shape=[jax.ShapeDtypeStruct((b, m, D_BRANCH), BF16)] * 2,
        compiler_params=pltpu.CompilerParams(dimension_semantics=("arbitrary",),
                                             vmem_limit_bytes=VMEM_LIMIT),
        name="memkv",
    )(mem, g_mem, w_kv)


def _prologue_kernel(x_ref, gmix_ref, win_ref, convw_ref, murkv_ref, muwag_ref,
                     wl1_ref, wl2_ref, w0_ref, al1_ref, al2_ref, a0_ref, gl1_ref, gl2_ref,
                     kk_ref, ka_ref, seg_ref, km_ref, vm_ref,
                     yconv_ref, ymem_ref, r_ref, k_ref, v_ref, kkn_ref, a_ref, lw_ref, g_ref,
                     prev_h, prev_p, prev_cu):
    tm = x_ref.shape[0]
    db = D_BRANCH
    sub = PROLOGUE_SUB
    n_parts = win_ref.shape[1] // db

    @pl.when(pl.program_id(1) == 0)
    def _():
        prev_h[...] = jnp.zeros_like(prev_h)
        prev_p[...] = jnp.zeros_like(prev_p)
        prev_cu[...] = jnp.zeros_like(prev_cu)

    rows = lax.broadcasted_iota(jnp.int32, (sub, 1), 0)

    def shift1(u, prev_row):
        return jnp.where(rows == 0, prev_row, pltpu.roll(u, 1, axis=0))

    carry = {"h": prev_h[...], "p": prev_p[...], "cu": prev_cu[...]}
    projected = {}

    def project(j):
        h = _rms(x_ref[j * sub:(j + 1) * sub, :], gmix_ref[...])
        hb = h.astype(BF16)
        parts = []
        for c in range(n_parts):
            parts.append(jnp.dot(hb, win_ref[:, c * db:(c + 1) * db], preferred_element_type=F32))
            yield
        projected[j] = (h, parts)

    def mix(j):
        rs = slice(j * sub, (j + 1) * sub)
        h, (bg, cg, u, rp, kp, vp, q) = projected.pop(j)

        cu = cg * u
        pcu = carry["cu"]
        cu1 = shift1(cu, pcu[1:2, :])
        cu2 = jnp.where(rows == 0, pcu[0:1, :], jnp.where(rows == 1, pcu[1:2, :], pltpu.roll(cu, 2, axis=0)))
        conv = cu2 * convw_ref[0:1, :] + cu1 * convw_ref[1:2, :] + cu * convw_ref[2:3, :]
        yconv_ref[rs, :] = (bg * conv).astype(BF16)
        carry["cu"] = cu[sub - 2:sub, :]

        pr = jnp.concatenate([rp, kp, vp], axis=1)
        prs = shift1(pr, carry["p"])
        mixed = pr + (prs - pr) * murkv_ref[...]
        carry["p"] = pr[sub - 1:sub, :]
        r, k, v = mixed[:, :db], mixed[:, db:2 * db], mixed[:, 2 * db:]
        r_ref[rs, :] = r
        v_ref[rs, :] = v

        dh = shift1(h, carry["h"]) - h
        carry["h"] = h[sub - 1:sub, :]
        lora_w = _bdot(h + dh * muwag_ref[0:1, :], wl1_ref[...])
        lora_a = _bdot(h + dh * muwag_ref[1:2, :], al1_ref[...])
        lora_g = _bdot(h + dh * muwag_ref[2:3, :], gl1_ref[...])
        yield
        zz = w0_ref[...] + _bdot(jnp.tanh(lora_w), wl2_ref[...])
        a_lin = a0_ref[...] + _bdot(lora_a, al2_ref[...])
        g_ref[rs, :] = _bdot(_sigmoid(lora_g), gl2_ref[...])
        yield
        softplus = jnp.maximum(-zz, 0.0) + jnp.log(1.0 + jnp.exp(-jnp.abs(zz)))
        lw_ref[rs, :] = -jnp.exp(-softplus - 0.5)
        a = _sigmoid(a_lin)
        a_ref[rs, :] = a
        k_ref[rs, :] = k * (1.0 + (a - 1.0) * ka_ref[...])
        kk = k * kk_ref[...]
        ss = _head_sums(kk * kk, seg_ref[...], 1)
        yield
        kkn_ref[rs, :] = kk * lax.rsqrt(jnp.maximum(ss, 1e-24))

        scale = MEM_HEAD_DIM ** -0.5
        heads = [slice(hh * MEM_HEAD_DIM, (hh + 1) * MEM_HEAD_DIM) for hh in range(MEM_HEADS)]
        scores = [_bdot_nt(q[:, sl], km_ref[0, :, sl]) * scale for sl in heads]
        yield
        for sl, s in zip(heads, scores):
            p = jnp.exp(s - jnp.max(s, axis=-1, keepdims=True))
            o = _bdot(p, vm_ref[0, :, sl]) / jnp.sum(p, axis=-1, keepdims=True)
            ymem_ref[rs, sl] = o.astype(BF16)

    _software_pipeline(project, mix, tm // sub)
    prev_h[...] = carry["h"]
    prev_p[...] = carry["p"]
    prev_cu[...] = carry["cu"]


def _prologue(x2, b, km, vm, p):
    t, d = x2.shape
    s = t // b
    tm = TM_PROLOGUE
    db = D_BRANCH
    m = km.shape[1]
    steps = s // tm
    tok = lambda c: pl.BlockSpec((tm, c), lambda bi, i: (bi * steps + i, 0))
    consts = [p["g_mix"], p["w_in"], p["conv_w"], p["mu_rkv"], p["mu_wag"],
              p["w_lora1"], p["w_lora2"], p["w0"], p["a_lora1"], p["a_lora2"], p["a0"],
              p["g_lora1"], p["g_lora2"], p["k_k"], p["k_a"], p["seg_ones"]]
    out_shapes = ([jax.ShapeDtypeStruct((t, db), BF16)] * 2
                  + [jax.ShapeDtypeStruct((t, db), F32)] * 7)
    return pl.pallas_call(
        _prologue_kernel,
        grid=(b, steps),
        in_specs=[tok(d)] + [_const_spec(c.shape) for c in consts]
                 + [pl.BlockSpec((1, m, db), lambda bi, i: (bi, 0, 0))] * 2,
        out_specs=[tok(db)] * 9,
        out_shape=out_shapes,
        scratch_shapes=[pltpu.VMEM((1, d), F32), pltpu.VMEM((1, 3 * db), F32),
                        pltpu.VMEM((2, db), F32)],
        compiler_params=pltpu.CompilerParams(dimension_semantics=("arbitrary", "arbitrary"),
                                             vmem_limit_bytes=VMEM_LIMIT),
        name="prologue",
    )(x2, *consts, km, vm)


def _rwkv_kernel(r_ref, k_ref, v_ref, kk_ref, a_ref, lw_ref, g_ref, rk_ref, lnw_ref, lnb_ref,
                 tri_ref, seg_ref, out_ref, h_scr, y_scr):
    tb = r_ref.shape[0]
    n = HEAD_DIM
    c_len = CHUNK

    @pl.when(pl.program_id(1) == 0)
    def _():
        h_scr[...] = jnp.zeros_like(h_scr)

    pw = 2 * n
    row1 = lax.broadcasted_iota(jnp.int32, (c_len, pw), 0)
    lane1 = lax.broadcasted_iota(jnp.int32, (c_len, pw), 1)
    col1 = lane1 & (n - 1)
    left = lane1 < n
    strict1 = col1 < row1
    incl1 = col1 <= row1
    eye2 = (col1 == row1).astype(F32)
    zeros_pair = jnp.zeros((c_len, pw), F32)
    zeros_bd = jnp.zeros((2 * c_len, pw), F32)

    def block_diag(y):
        return jnp.concatenate([jnp.where(left, y, 0.0), jnp.where(left, 0.0, y)], axis=0)

    def pair_transpose(y):
        zt = block_diag(y).T
        return zt[:c_len] + zt[c_len:]

    def chunk_inputs(c):
        rows = pl.ds(pl.multiple_of(c * c_len, c_len), c_len)
        r = r_ref[rows, :]
        k = k_ref[rows, :]
        v = v_ref[rows, :]
        kk = kk_ref[rows, :]
        a = a_ref[rows, :]
        lw = lw_ref[rows, :]
        gcum = _split_dot_left(tri_ref[...], lw, 2)
        e_pos = jnp.exp(gcum)
        e_neg = jnp.exp(-gcum)
        p_last = jnp.exp(gcum[c_len - 1:c_len, :])
        bb = kk * a * e_neg
        kb = k * e_neg
        return dict(rows=rows, v=v, p_last=p_last, rb=r * e_pos, ab=-kk * jnp.exp(gcum - lw), bb=bb, kb=kb,
                    bbp=bb * p_last, kbp=kb * p_last)

    def chunk_group(it, carry):
        chunks = [chunk_inputs(it * CHUNKS_PER_ITER + ci) for ci in range(CHUNKS_PER_ITER)]
        n_pairs = N_HEADS // 2
        units = [(ci, pr) for ci in range(CHUNKS_PER_ITER) for pr in range(n_pairs)]
        nu = range(len(units))
        ls = [slice(pr * pw, (pr + 1) * pw) for _, pr in units]
        ch = [chunks[ci] for ci, _ in units]
        al = [ch[u]["ab"][:, ls[u]] for u in nu]
        rr = [ch[u]["rb"][:, ls[u]] for u in nu]
        v_bd = [block_diag(ch[u]["v"][:, ls[u]]) for u in nu]
        aa = [_bdot_nt(jnp.concatenate([al[u], rr[u]], axis=0),
                       jnp.concatenate([block_diag(ch[u]["bb"][:, ls[u]]), block_diag(ch[u]["kb"][:, ls[u]])],
                                       axis=0)) for u in nu]
        a_ab = [jnp.where(strict1, aa[u][:c_len, :pw], 0.0) for u in nu]
        a_ak = [jnp.where(strict1, aa[u][:c_len, pw:], 0.0) for u in nu]
        a_rb = [jnp.where(incl1, aa[u][c_len:, :pw], 0.0) for u in nu]
        a_rk = [jnp.where(incl1, aa[u][c_len:, pw:], 0.0) for u in nu]
        av = [_bdot(a_ak[u], v_bd[u]) for u in nu]
        t_inv = [eye2 + a_ab[u] for u in nu]
        x_pow = [_bdot(a_ab[u], block_diag(a_ab[u])) for u in nu]
        for lvl in range(5):
            if lvl < 4:
                z = [_bdot(jnp.concatenate([t_inv[u], x_pow[u]], axis=0), block_diag(x_pow[u])) for u in nu]
                t_inv = [t_inv[u] + z[u][:c_len] for u in nu]
                x_pow = [z[u][c_len:] for u in nu]
            else:
                t_inv = [t_inv[u] + _bdot(t_inv[u], block_diag(x_pow[u])) for u in nu]
        w12 = [_bdot(t_inv[u], jnp.concatenate([block_diag(al[u]), block_diag(av[u])], axis=1))
               for u in nu]
        z2 = []
        for u in nu:
            rhs2 = jnp.concatenate(
                [jnp.concatenate([block_diag(w12[u][:, :pw]), block_diag(w12[u][:, pw:])], axis=1),
                 jnp.concatenate([zeros_bd, v_bd[u]], axis=1)], axis=0)
            lhs3 = jnp.concatenate(
                [jnp.concatenate([pair_transpose(ch[u]["bbp"][:, ls[u]]),
                                  pair_transpose(ch[u]["kbp"][:, ls[u]])], axis=1),
                 jnp.concatenate([a_rb[u], a_rk[u]], axis=1)], axis=0)
            z2.append(_bdot(lhs3, rhs2))
        state = [h_scr[pr] for pr in range(n_pairs)]
        for u in nu:
            pr = units[u][1]
            mq = z2[u][:, :pw] + jnp.concatenate([zeros_pair, rr[u]], axis=0)
            out = _bdot(mq, block_diag(state[pr])) + z2[u][:, pw:]
            decay = eye2 * ch[u]["p_last"][:, ls[u]]
            p_mat = jnp.where(left, jnp.sum(jnp.where(left, decay, 0.0), axis=1, keepdims=True),
                              jnp.sum(jnp.where(left, 0.0, decay), axis=1, keepdims=True))
            state[pr] = p_mat * state[pr] + out[:c_len]
            y_scr[ch[u]["rows"], ls[u]] = out[c_len:]
        for pr in range(n_pairs):
            h_scr[pr] = state[pr]
        return carry

    lax.fori_loop(0, tb // (c_len * CHUNKS_PER_ITER), chunk_group, 0)

    y = y_scr[...]
    seg = seg_ref[...]
    inv_n = 1.0 / n
    mu = _head_sums(y, seg, 2) * inv_n
    yc = y - mu
    var = _head_sums(yc * yc, seg, 1) * inv_n
    yn = yc * lax.rsqrt(var + GN_EPS) * lnw_ref[...] + lnb_ref[...]
    bonus = _head_sums(r_ref[...] * k_ref[...] * rk_ref[...], seg, 1) * v_ref[...]
    out_ref[...] = ((yn + bonus) * g_ref[...]).astype(BF16)


def _rwkv(r, k, v, kkn, a, lw, g, b, p):
    t, db = r.shape
    tb = TB_RWKV
    steps = t // b // tb
    tok = pl.BlockSpec((tb, db), lambda bi, i: (bi * steps + i, 0))
    consts = [p["r_k"], p["ln_x_w"], p["ln_x_b"], p["tri"], p["seg_ones"]]
    return pl.pallas_call(
        _rwkv_kernel,
        grid=(b, steps),
        in_specs=[tok] * 7 + [_const_spec(c.shape) for c in consts],
        out_specs=tok,
        out_shape=jax.ShapeDtypeStruct((t, db), BF16),
        scratch_shapes=[pltpu.VMEM((N_HEADS // 2, HEAD_DIM, 2 * HEAD_DIM), F32),
                        pltpu.VMEM((tb, db), F32)],
        compiler_params=pltpu.CompilerParams(dimension_semantics=("arbitrary", "arbitrary"),
                                             vmem_limit_bytes=VMEM_LIMIT),
        name="rwkv",
    )(r, k, v, kkn, a, lw, g, *consts)


def _merge_kernel(x_ref, yc_ref, yr_ref, ym_ref, gmix_ref, wgate_ref, bgate_ref, wbr_ref, wo_ref,
                  gffn_ref, wrt_ref, brt_ref, tril_ref,
                  x1_ref, meta_ref, metat_ref, cnt_ref, base_scr):
    tm, d = x_ref.shape

    @pl.when(pl.program_id(0) == 0)
    def _():
        base_scr[...] = jnp.zeros_like(base_scr)

    sub = tril_ref.shape[0]
    lane = lax.broadcasted_iota(jnp.int32, (sub, LANES), 1)
    neg = jnp.float32(-jnp.inf)
    big = jnp.int32(1 << 20)
    w_hi, w_lo = _split_terms(wrt_ref[...], 2)
    w_hi_lo = jnp.concatenate([w_hi, w_lo], axis=1)
    state = {"base": base_scr[...]}
    merged = {}

    def project(j):
        rs = slice(j * sub, (j + 1) * sub)
        x = x_ref[rs, :]
        hb = _rms(x, gmix_ref[...]).astype(BF16)
        z = jnp.zeros((sub, d), F32)
        for i, y_ref in enumerate((yc_ref, yr_ref, ym_ref)):
            cs = slice(i * d, (i + 1) * d)
            gate = _sigmoid(jnp.dot(hb, wgate_ref[:, cs], preferred_element_type=F32) + bgate_ref[:, cs])
            z = z + gate * jnp.dot(y_ref[rs, :], wbr_ref[i], preferred_element_type=F32)
            yield
        x1 = x + _bdot(z, wo_ref[...])
        x1_ref[rs, :] = x1
        merged[j] = x1

    def route(j):
        rs = slice(j * sub, (j + 1) * sub)
        h2 = _rms(merged.pop(j), gffn_ref[...])
        h_hi, h_lo = _split_terms(h2, 2)
        hi_terms = jnp.dot(h_hi, w_hi_lo, preferred_element_type=F32)
        logits = (hi_terms[:, :LANES]
                  + (jnp.dot(h_lo, w_hi, preferred_element_type=F32) + hi_terms[:, LANES:])) + brt_ref[...]
        yield
        gmask = (lane >= N_EXPERTS) & (lane < N_EXPERTS + N_GROUPS)
        glv = jnp.where(gmask, logits, neg)
        gmax = jnp.max(glv, axis=-1, keepdims=True)
        g_sel = jnp.min(jnp.where(glv == gmax, lane - N_EXPERTS, big), axis=-1, keepdims=True)
        g_w = 1.0 / jnp.sum(jnp.exp(glv - gmax), axis=-1, keepdims=True)
        emask = (lane < N_EXPERTS) & ((lane >> 3) == g_sel)
        elv = jnp.where(emask, logits, neg)
        emax = jnp.max(elv, axis=-1, keepdims=True)
        esum = jnp.sum(jnp.exp(elv - emax), axis=-1, keepdims=True)
        i1 = jnp.min(jnp.where(elv == emax, lane, big), axis=-1, keepdims=True)
        elv2 = jnp.where(lane == i1, neg, elv)
        m2 = jnp.max(elv2, axis=-1, keepdims=True)
        i2 = jnp.min(jnp.where(elv2 == m2, lane, big), axis=-1, keepdims=True)
        p1 = 1.0 / esum
        p2 = jnp.exp(m2 - emax) / esum
        c1 = g_w * p1 / (p1 + p2)
        c2 = g_w * p2 / (p1 + p2)

        oh1 = lane == i1
        oh2 = lane == i2
        onehot = jnp.where(oh1 | oh2, 1.0, 0.0)
        before = jnp.dot(tril_ref[...], onehot.astype(BF16), preferred_element_type=F32) + state["base"]
        yield
        rank1 = jnp.sum(jnp.where(oh1, before, 0.0), axis=-1, keepdims=True)
        rank2 = jnp.sum(jnp.where(oh2, before, 0.0), axis=-1, keepdims=True)
        state["base"] = state["base"] + jnp.sum(onehot, axis=0, keepdims=True)

        meta = jnp.where(lane == 0, i1.astype(F32),
               jnp.where(lane == 1, i2.astype(F32),
               jnp.where(lane == 2, rank1,
               jnp.where(lane == 3, rank2,
               jnp.where(lane == 4, c1,
               jnp.where(lane == 5, c2, 0.0))))))
        meta_ref[rs, :] = meta[:, :META_COLS]
        metat_ref[:, rs] = meta.T[:META_COLS, :]

    _software_pipeline(project, route, tm // sub)
    base_scr[...] = state["base"]
    cnt_ref[...] = jnp.broadcast_to(state["base"], cnt_ref.shape)


def _merge(x2, yc, yr, ym, p):
    t, d = x2.shape
    tm = TM_MERGE
    db = D_BRANCH
    tok = lambda c: pl.BlockSpec((tm, c), lambda i: (i, 0))
    consts = [p["g_mix"], p["w_gate"], p["b_gate"], p["w_branch"], p["w_o"], p["g_ffn"],
              p["w_router"], p["b_router"], p["tril_strict"]]
    return pl.pallas_call(
        _merge_kernel,
        grid=(t // tm,),
        in_specs=[tok(d), tok(db), tok(db), tok(db)] + [_const_spec(c.shape) for c in consts],
        out_specs=[tok(d), tok(META_COLS), pl.BlockSpec((META_COLS, tm), lambda i: (0, i)),
                   _const_spec((8, LANES))],
        out_shape=[jax.ShapeDtypeStruct((t, d), F32), jax.ShapeDtypeStruct((t, META_COLS), F32),
                   jax.ShapeDtypeStruct((META_COLS, t), F32), jax.ShapeDtypeStruct((8, LANES), F32)],
        scratch_shapes=[pltpu.VMEM((1, LANES), F32)],
        compiler_params=pltpu.CompilerParams(dimension_semantics=("arbitrary",),
                                             vmem_limit_bytes=VMEM_LIMIT),
        name="merge",
    )(x2, yc, yr, ym, *consts)


def _store_packed_rows(ref2d, x, stage):
    rows, d = x.shape
    nt = d // (2 * LANES)
    for c in range(nt):
        stage[c, pl.ds(0, rows, stride=2), :] = x[:, c * LANES:(c + 1) * LANES]
        stage[c, pl.ds(1, rows, stride=2), :] = x[:, (c + nt) * LANES:(c + nt + 1) * LANES]
        ref2d[pl.ds(c, rows, stride=nt), :] = pltpu.bitcast(stage[c].astype(BF16), jnp.uint32)


def _load_packed_rows(ref2d, rows, nt, stage):
    lo, hi = [], []
    for c in range(nt):
        stage[c] = pltpu.bitcast(ref2d[pl.ds(c, rows, stride=nt), :], BF16).astype(F32)
        lo.append(stage[c, pl.ds(0, rows, stride=2), :])
        hi.append(stage[c, pl.ds(1, rows, stride=2), :])
    return jnp.concatenate(lo + hi, axis=1)


def _scatter_kernel(dest0_ref, dest1_ref, x1_ref, gffn_ref, xs_ref, hbuf, stage, sem):
    dest_refs = (dest0_ref, dest1_ref)
    ts, d_model = x1_ref.shape
    nt = d_model // (2 * LANES)
    s = pl.program_id(0)
    slot = s % 2

    def wait_slot(sl):
        for _ in range(TOP_K):
            pltpu.make_async_copy(hbuf.at[sl], xs_ref.at[pl.ds(0, ts * nt), :], sem.at[sl]).wait()

    @pl.when(s >= 2)
    def _():
        wait_slot(slot)

    _store_packed_rows(hbuf.at[slot], _rms(x1_ref[...], gffn_ref[...]), stage)

    def issue(grp, carry):
        grp_off = pl.multiple_of(grp * (SUBLANES * nt), SUBLANES * nt)
        for j in range(SUBLANES):
            tok = s * ts + grp * SUBLANES + j
            for kslot in range(TOP_K):
                d = pl.multiple_of(dest_refs[kslot][tok], nt)
                pltpu.make_async_copy(hbuf.at[slot, pl.ds(grp_off + j * nt, nt), :],
                                      xs_ref.at[pl.ds(d, nt), :], sem.at[slot]).start(priority=kslot)
        return carry

    lax.fori_loop(0, ts // SUBLANES, issue, 0, unroll=ISSUE_UNROLL)

    @pl.when(s == pl.num_programs(0) - 1)
    def _():
        @pl.when(s >= 1)
        def _():
            wait_slot(1 - slot)
        wait_slot(slot)


def _scatter(dests, x1, g_ffn, n_rows):
    t, d = x1.shape
    ts = TS_SCATTER
    pt = d // (2 * LANES)
    return pl.pallas_call(
        _scatter_kernel,
        grid_spec=pltpu.PrefetchScalarGridSpec(
            num_scalar_prefetch=TOP_K,
            grid=(t // ts,),
            in_specs=[pl.BlockSpec((ts, d), lambda i, *_: (i, 0)),
                      pl.BlockSpec((1, d), lambda i, *_: (0, 0))],
            out_specs=pl.BlockSpec(memory_space=pl.ANY),
            scratch_shapes=[pltpu.VMEM((2, ts * pt, LANES), jnp.uint32),
                            pltpu.VMEM((pt, 2 * ts, LANES), F32),
                            pltpu.SemaphoreType.DMA((2,))],
        ),
        out_shape=jax.ShapeDtypeStruct((n_rows * pt, LANES), jnp.uint32),
        compiler_params=pltpu.CompilerParams(dimension_semantics=("arbitrary",),
                                             vmem_limit_bytes=VMEM_LIMIT),
        name="scatter",
    )(*dests, x1, g_ffn)


def _experts_kernel(be_ref, nused_ref, nexte_ref, wslot_ref, nvalid_ref, xs_ref, wg_hbm, wu_hbm, wd_hbm, ys_ref,
                    wg_f, wu_f, wd_f, wg_s, wu_s, wd_s, xbuf, ybuf, xstage, ystage, xstage_h, ystage_h,
                    sem, xsem, ysem):
    i = pl.program_id(0)
    e = be_ref[i]
    prev = be_ref[jnp.maximum(i - 1, 0)]
    active = i < nused_ref[0]

    def weight_copies(ex):
        ws = wslot_ref[ex]
        return (pltpu.make_async_copy(wg_hbm.at[ex], wg_f.at[ws], sem.at[ws, 0]),
                pltpu.make_async_copy(wu_hbm.at[ex], wu_f.at[ws], sem.at[ws, 1]),
                pltpu.make_async_copy(wd_hbm.at[ex], wd_f.at[ws], sem.at[ws, 2]))

    def start_weights(ex):
        @pl.when(ex < N_EXPERTS)
        def _():
            for cp in weight_copies(ex):
                cp.start(priority=1)

    @pl.when(i == 0)
    def _():
        start_weights(e)
        start_weights(nexte_ref[e])

    @pl.when(active & ((i == 0) | (e != prev)))
    def _():
        for cp in weight_copies(e):
            cp.wait()
        ws = wslot_ref[e]
        wg_s[...] = wg_f[ws].astype(BF16)
        wu_s[...] = wu_f[ws].astype(BF16)
        wd_s[...] = wd_f[ws].astype(BF16)
        nxt = nexte_ref[e]
        start_weights(jnp.where(nxt < N_EXPERTS, nexte_ref[jnp.minimum(nxt, N_EXPERTS - 1)], N_EXPERTS))

    blk_rows = xbuf.shape[1]
    n_used = nused_ref[0]

    def xs_copy(blk, slot):
        return pltpu.make_async_copy(xs_ref.at[pl.ds(pl.multiple_of(blk * blk_rows, blk_rows), blk_rows), :],
                                     xbuf.at[slot], xsem.at[slot])

    def ys_copy(blk, slot):
        return pltpu.make_async_copy(ybuf.at[slot],
                                     ys_ref.at[pl.ds(pl.multiple_of(blk * blk_rows, blk_rows), blk_rows), :],
                                     ysem.at[slot])

    @pl.when(i == 0)
    def _():
        for ahead in range(XS_BUFFERS - 1):
            @pl.when(ahead < n_used)
            def _():
                xs_copy(ahead, ahead).start()

    @pl.when(active)
    def _():
        nt = wg_s.shape[0] // (2 * LANES)
        fetch = i + (XS_BUFFERS - 1)

        @pl.when(fetch < n_used)
        def _():
            xs_copy(fetch, fetch % XS_BUFFERS).start()

        xs_copy(i, i % XS_BUFFERS).wait()
        oslot = i % 2

        @pl.when(i >= 2)
        def _():
            ys_copy(i - 2, oslot).wait()

        def expert_mlp(rows, x_stage, y_stage):
            xb = _load_packed_rows(xbuf.at[i % XS_BUFFERS], rows, nt, x_stage).astype(BF16)
            gate = jnp.dot(xb, wg_s[...], preferred_element_type=F32)
            up = jnp.dot(xb, wu_s[...], preferred_element_type=F32)
            hid = gate * _sigmoid(gate) * up
            _store_packed_rows(ybuf.at[oslot], jnp.dot(hid.astype(BF16), wd_s[...], preferred_element_type=F32),
                               y_stage)

        half_rows = ROW_BLOCK // 2
        half_only = nvalid_ref[i] <= half_rows

        @pl.when(half_only)
        def _():
            expert_mlp(half_rows, xstage_h, ystage_h)
            ybuf[oslot, pl.ds(half_rows * nt, half_rows * nt), :] = pltpu.bitcast(
                jnp.zeros((2 * half_rows * nt, LANES), BF16), jnp.uint32)

        @pl.when(jnp.logical_not(half_only))
        def _():
            expert_mlp(ROW_BLOCK, xstage, ystage)

        ys_copy(i, oslot).start()

        @pl.when(i == n_used - 1)
        def _():
            @pl.when(i >= 1)
            def _():
                ys_copy(i - 1, 1 - oslot).wait()
            ys_copy(i, oslot).wait()


def _experts(blk_expert, n_used, next_expert, weight_slot, blk_valid, xs, w_gate, w_up, w_down):
    d, de = w_gate.shape[-2:]
    pt = d // (2 * LANES)
    blk_rows = ROW_BLOCK * pt
    nb = xs.shape[0] // blk_rows

    return pl.pallas_call(
        _experts_kernel,
        grid_spec=pltpu.PrefetchScalarGridSpec(
            num_scalar_prefetch=5,
            grid=(nb,),
            in_specs=[pl.BlockSpec(memory_space=pl.ANY)] * 4,
            out_specs=pl.BlockSpec(memory_space=pl.ANY),
            scratch_shapes=[pltpu.VMEM((2, d, de), F32), pltpu.VMEM((2, d, de), F32),
                            pltpu.VMEM((2, de, d), F32),
                            pltpu.VMEM((d, de), BF16), pltpu.VMEM((d, de), BF16), pltpu.VMEM((de, d), BF16),
                            pltpu.VMEM((XS_BUFFERS, blk_rows, LANES), jnp.uint32),
                            pltpu.VMEM((2, blk_rows, LANES), jnp.uint32),
                            pltpu.VMEM((pt, 2 * ROW_BLOCK, LANES), F32),
                            pltpu.VMEM((pt, 2 * ROW_BLOCK, LANES), F32),
                            pltpu.VMEM((pt, ROW_BLOCK, LANES), F32),
                            pltpu.VMEM((pt, ROW_BLOCK, LANES), F32),
                            pltpu.SemaphoreType.DMA((2, 3)), pltpu.SemaphoreType.DMA((XS_BUFFERS,)),
                            pltpu.SemaphoreType.DMA((2,))],
        ),
        out_shape=jax.ShapeDtypeStruct(xs.shape, jnp.uint32),
        compiler_params=pltpu.CompilerParams(dimension_semantics=("arbitrary",),
                                             vmem_limit_bytes=VMEM_LIMIT),
        name="experts",
    )(blk_expert, n_used, next_expert, weight_slot, blk_valid, xs, w_gate, w_up, w_down)


def _combine_kernel(dest0_ref, dest1_ref, x1_ref, meta_ref, gfin_ref, ys_ref, out_ref, ybuf, stage, sem):
    dest_refs = (dest0_ref, dest1_ref)
    te = x1_ref.shape[0]
    s = pl.program_id(0)
    nsteps = pl.num_programs(0)
    slot = s % 2

    nt = x1_ref.shape[1] // (2 * LANES)

    def issue_step(step, sl):
        def issue(grp, carry):
            grp_off = pl.multiple_of(grp * (SUBLANES * nt), SUBLANES * nt)
            for j in range(SUBLANES):
                tok = step * te + grp * SUBLANES + j
                for kslot in range(TOP_K):
                    d = pl.multiple_of(dest_refs[kslot][tok], nt)
                    pltpu.make_async_copy(ys_ref.at[pl.ds(d, nt), :],
                                          ybuf.at[sl, kslot, pl.ds(grp_off + j * nt, nt), :],
                                          sem.at[sl]).start(priority=kslot)
            return carry
        lax.fori_loop(0, te // SUBLANES, issue, 0, unroll=ISSUE_UNROLL)

    @pl.when(s == 0)
    def _():
        issue_step(0, 0)

    @pl.when(s + 1 < nsteps)
    def _():
        issue_step(s + 1, 1 - slot)

    for kslot in range(TOP_K):
        pltpu.make_async_copy(ys_ref.at[pl.ds(0, te * nt), :], ybuf.at[slot, kslot], sem.at[slot]).wait()

    meta = meta_ref[...]
    y0 = _load_packed_rows(ybuf.at[slot, 0], te, nt, stage.at[0])
    y1 = _load_packed_rows(ybuf.at[slot, 1], te, nt, stage.at[1])
    x2 = x1_ref[...] + y0 * meta[:, 4:5] + y1 * meta[:, 5:6]
    out_ref[...] = _rms(x2, gfin_ref[...])


def _combine(dests, x1, meta, g_final, ys):
    t, d = x1.shape
    te = TE_COMBINE
    pt = d // (2 * LANES)
    return pl.pallas_call(
        _combine_kernel,
        grid_spec=pltpu.PrefetchScalarGridSpec(
            num_scalar_prefetch=TOP_K,
            grid=(t // te,),
            in_specs=[pl.BlockSpec((te, d), lambda i, *_: (i, 0)),
                      pl.BlockSpec((te, META_COLS), lambda i, *_: (i, 0)),
                      pl.BlockSpec((1, d), lambda i, *_: (0, 0)),
                      pl.BlockSpec(memory_space=pl.ANY)],
            out_specs=pl.BlockSpec((te, d), lambda i, *_: (i, 0)),
            scratch_shapes=[pltpu.VMEM((2, TOP_K, te * pt, LANES), jnp.uint32),
                            pltpu.VMEM((TOP_K, pt, 2 * te, LANES), F32),
                            pltpu.SemaphoreType.DMA((2,))],
        ),
        out_shape=jax.ShapeDtypeStruct((t, d), F32),
        compiler_params=pltpu.CompilerParams(dimension_semantics=("arbitrary",),
                                             vmem_limit_bytes=VMEM_LIMIT),
        name="combine",
    )(*dests, x1, meta, g_final, ys)


def _constants(tm_merge):
    n = CHUNK
    tri = (jnp.arange(n)[:, None] >= jnp.arange(n)[None, :]).astype(BF16)
    head = jnp.arange(2 * LANES) // HEAD_DIM
    seg_ones = (head[:, None] == head[None, :]).astype(BF16)
    tril_strict = (jnp.arange(tm_merge)[:, None] > jnp.arange(tm_merge)[None, :]).astype(BF16)
    return tri, seg_ones, tril_strict


def kernel(x, mem, g_mix, g_mem, w_in, conv_w, mu_rkv, mu_wag, w_lora1, w_lora2, w0, a_lora1, a_lora2, a0, g_lora1, g_lora2, k_k, k_a, r_k, ln_x_w, ln_x_b, w_kv_mem, w_branch, w_gate, b_gate, w_o, g_ffn, w_router_group, b_router_group, w_router_expert, b_router_expert, w_exp_gate, w_exp_up, w_exp_down, g_final):
    assert g_mix.shape[0] == 1, "single-layer block"
    b, s, d = x.shape
    t = b * s
    db = D_BRANCH
    tri, seg_ones, tril_strict = _constants(MERGE_SUB)
    row = lambda a: a.reshape(1, -1)
    pad_r = LANES - N_EXPERTS - N_GROUPS
    p = {
        "g_mix": row(g_mix[0]), "w_in": w_in[0].astype(BF16), "conv_w": conv_w[0].T,
        "mu_rkv": row(mu_rkv[0]), "mu_wag": mu_wag[0],
        "w_lora1": w_lora1[0].astype(BF16), "w_lora2": w_lora2[0].astype(BF16), "w0": row(w0[0]),
        "a_lora1": a_lora1[0].astype(BF16), "a_lora2": a_lora2[0].astype(BF16), "a0": row(a0[0]),
        "g_lora1": g_lora1[0].astype(BF16), "g_lora2": g_lora2[0].astype(BF16),
        "k_k": row(k_k[0]), "k_a": row(k_a[0]), "r_k": row(r_k[0]),
        "ln_x_w": row(ln_x_w[0]), "ln_x_b": row(ln_x_b[0]),
        "w_gate": w_gate[0].astype(BF16), "b_gate": row(b_gate[0]),
        "w_branch": w_branch[0].astype(BF16), "w_o": w_o[0].astype(BF16), "g_ffn": row(g_ffn[0]),
        "w_router": jnp.concatenate([w_router_expert[0], w_router_group[0],
                                     jnp.zeros((d, pad_r), F32)], axis=1),
        "b_router": row(jnp.concatenate([b_router_expert[0], b_router_group[0],
                                         jnp.zeros((pad_r,), F32)])),
        "tri": tri, "seg_ones": seg_ones, "tril_strict": tril_strict,
    }

    km, vm = _memkv(mem, row(g_mem[0]), w_kv_mem[0].astype(BF16))
    x2 = x.reshape(t, d)
    yconv, ymem, r, k, v, kkn, a, lw, g = _prologue(x2, b, km, vm, p)
    yrwkv = _rwkv(r, k, v, kkn, a, lw, g, b, p)
    x1, meta, meta_t, cnt = _merge(x2, yconv, yrwkv, ymem, p)

    counts = cnt[0, :N_EXPERTS].astype(jnp.int32)
    padded = ((counts + ROW_BLOCK - 1) // ROW_BLOCK) * ROW_BLOCK
    pad_end = jnp.cumsum(padded)
    pad_start = pad_end - padded
    n_blocks = (t * TOP_K) // ROW_BLOCK + N_EXPERTS
    eids = jnp.arange(N_EXPERTS, dtype=jnp.int32)
    e_idx = meta_t[0:TOP_K].astype(jnp.int32)
    rank = meta_t[TOP_K:2 * TOP_K].astype(jnp.int32)
    start_of = jnp.sum(jnp.where(e_idx[:, None, :] == eids[None, :, None], pad_start[None, :, None], 0), axis=1)
    dest = (start_of + rank) * (d // (2 * LANES))
    dests = [dest[kslot] for kslot in range(TOP_K)]
    blk_start = jnp.arange(n_blocks, dtype=jnp.int32) * ROW_BLOCK
    blk_expert = jnp.minimum(jnp.sum((pad_end[None, :] <= blk_start[:, None]).astype(jnp.int32), axis=1),
                             N_EXPERTS - 1)
    n_used = (pad_end[-1:] // ROW_BLOCK).astype(jnp.int32)
    later_nonempty = (eids[None, :] > eids[:, None]) & (counts[None, :] > 0)
    next_expert = jnp.min(jnp.where(later_nonempty, eids[None, :], N_EXPERTS), axis=1)
    weight_slot = (jnp.cumsum((counts > 0).astype(jnp.int32)) - 1) & 1
    of_blk = blk_expert[:, None] == eids[None, :]
    blk_valid = jnp.clip(jnp.sum(jnp.where(of_blk, (pad_start + counts)[None, :], 0), axis=1) - blk_start,
                         0, ROW_BLOCK)

    xs = _scatter(dests, x1, p["g_ffn"], n_blocks * ROW_BLOCK)
    ys = _experts(blk_expert, n_used, next_expert, weight_slot, blk_valid, xs,
                  w_exp_gate[0], w_exp_up[0], w_exp_down[0])
    out = _combine(dests, x1, meta, row(g_final), ys)
    return out.reshape(b, s, d)
```

```python
import jax
import jax.numpy as jnp
from jax import lax
from jax.experimental import pallas as pl
from jax.experimental.pallas import tpu as pltpu

F32 = jnp.float32
BF16 = jnp.bfloat16

NORM_EPS = 1e-6
GN_EPS = 64e-5
D_BRANCH = 512
HEAD_DIM = 64
N_HEADS = 8
CHUNK = 64
CHUNKS_PER_ITER = 4
MEM_HEADS = 4
MEM_HEAD_DIM = 128
N_GROUPS = 8
EXPERTS_PER_GROUP = 8
N_EXPERTS = 64
TOP_K = 2
ROW_BLOCK = 256
XS_BUFFERS = 4
LANES = 128
VMEM_LIMIT = 56 * 1024 * 1024

TM_PROLOGUE = 512
PROLOGUE_SUB = 256
TB_RWKV = 256
TM_MERGE = 1024
MERGE_SUB = 512
TS_SCATTER = 1024
TE_COMBINE = 256
SUBLANES = 8
META_COLS = 8
ISSUE_UNROLL = 2


def _bdot(a, b):
    return jnp.dot(a.astype(BF16), b.astype(BF16), preferred_element_type=F32)


def _bdot_nt(a, b):
    return lax.dot_general(a.astype(BF16), b.astype(BF16), (((1,), (1,)), ((), ())),
                           preferred_element_type=F32)


def _split_terms(x, n_terms):
    terms = []
    for _ in range(n_terms):
        t = x.astype(BF16)
        terms.append(t)
        x = x - t.astype(F32)
    return terms


def _split_dot_left(m_bf16, x, n_terms):
    return sum(jnp.dot(m_bf16, t, preferred_element_type=F32) for t in _split_terms(x, n_terms))


def _head_sums(x, seg_bf16, n_terms):
    w = seg_bf16.shape[0]
    terms = _split_terms(x, n_terms)
    halves = [sum(jnp.dot(t[:, c:c + w], seg_bf16, preferred_element_type=F32) for t in terms)
              for c in range(0, x.shape[1], w)]
    return jnp.concatenate(halves, axis=1)


def _rms(x, g):
    return x * lax.rsqrt(jnp.mean(x * x, axis=-1, keepdims=True) + NORM_EPS) * g


def _sigmoid(x):
    return 1.0 / (1.0 + jnp.exp(-x))


def _run_together(*gens):
    live = list(gens)
    while live:
        for gen in list(live):
            try:
                next(gen)
            except StopIteration:
                live.remove(gen)


def _software_pipeline(heavy, light, n_sub):
    _run_together(heavy(0))
    for j in range(1, n_sub):
        _run_together(heavy(j), light(j - 1))
    _run_together(light(n_sub - 1))


def _const_spec(shape):
    n = len(shape)
    return pl.BlockSpec(shape, lambda *_: (0,) * n)


def _memkv_kernel(mem_ref, g_ref, w_ref, k_ref, v_ref):
    mn = _rms(mem_ref[0], g_ref[...])
    kv = _bdot(mn, w_ref[...])
    k_ref[0] = kv[:, :D_BRANCH].astype(BF16)
    v_ref[0] = kv[:, D_BRANCH:].astype(BF16)


def _memkv(mem, g_mem, w_kv):
    b, m, d = mem.shape
    return pl.pallas_call(
        _memkv_kernel,
        grid=(b,),
        in_specs=[pl.BlockSpec((1, m, d), lambda i: (i, 0, 0)),
                  _const_spec((1, d)), _const_spec((d, 2 * D_BRANCH))],
        out_specs=[pl.BlockSpec((1, m, D_BRANCH), lambda i: (i, 0, 0)),
                   pl.BlockSpec((1, m, D_BRANCH), lambda i: (i, 0, 0))],
        out_shape=[jax.ShapeDtypeStruct((b, m, D_BRANCH), BF16)] * 2,
        compiler_params=pltpu.CompilerParams(dimension_semantics=("arbitrary",),
                                             vmem_limit_bytes=VMEM_LIMIT),
        name="memkv",
    )(mem, g_mem, w_kv)


def _prologue_kernel(x_ref, gmix_ref, win_ref, convw_ref, murkv_ref, muwag_ref,
                     wl1_ref, wl2_ref, w0_ref, al1_ref, al2_ref, a0_ref, gl1_ref, gl2_ref,
                     kk_ref, ka_ref, seg_ref, km_ref, vm_ref,
                     yconv_ref, ymem_ref, r_ref, k_ref, v_ref, kkn_ref, a_ref, lw_ref, g_ref,
                     prev_h, prev_p, prev_cu):
    tm = x_ref.shape[0]
    db = D_BRANCH
    sub = PROLOGUE_SUB
    n_parts = win_ref.shape[1] // db

    @pl.when(pl.program_id(1) == 0)
    def _():
        prev_h[...] = jnp.zeros_like(prev_h)
        prev_p[...] = jnp.zeros_like(prev_p)
        prev_cu[...] = jnp.zeros_like(prev_cu)

    rows = lax.broadcasted_iota(jnp.int32, (sub, 1), 0)

    def shift1(u, prev_row):
        return jnp.where(rows == 0, prev_row, pltpu.roll(u, 1, axis=0))

    carry = {"h": prev_h[...], "p": prev_p[...], "cu": prev_cu[...]}
    projected = {}

    def project(j):
        h = _rms(x_ref[j * sub:(j + 1) * sub, :], gmix_ref[...])
        hb = h.astype(BF16)
        parts = []
        for c in range(n_parts):
            parts.append(jnp.dot(hb, win_ref[:, c * db:(c + 1) * db], preferred_element_type=F32))
            yield
        projected[j] = (h, parts)

    def mix(j):
        rs = slice(j * sub, (j + 1) * sub)
        h, (bg, cg, u, rp, kp, vp, q) = projected.pop(j)

        cu = cg * u
        pcu = carry["cu"]
        cu1 = shift1(cu, pcu[1:2, :])
        cu2 = jnp.where(rows == 0, pcu[0:1, :], jnp.where(rows == 1, pcu[1:2, :], pltpu.roll(cu, 2, axis=0)))
        conv = cu2 * convw_ref[0:1, :] + cu1 * convw_ref[1:2, :] + cu * convw_ref[2:3, :]
        yconv_ref[rs, :] = (bg * conv).astype(BF16)
        carry["cu"] = cu[sub - 2:sub, :]

        pr = jnp.concatenate([rp, kp, vp], axis=1)
        prs = shift1(pr, carry["p"])
        mixed = pr + (prs - pr) * murkv_ref[...]
        carry["p"] = pr[sub - 1:sub, :]
        r, k, v = mixed[:, :db], mixed[:, db:2 * db], mixed[:, 2 * db:]
        r_ref[rs, :] = r
        v_ref[rs, :] = v

        dh = shift1(h, carry["h"]) - h
        carry["h"] = h[sub - 1:sub, :]
        lora_w = _bdot(h + dh * muwag_ref[0:1, :], wl1_ref[...])
        lora_a = _bdot(h + dh * muwag_ref[1:2, :], al1_ref[...])
        lora_g = _bdot(h + dh * muwag_ref[2:3, :], gl1_ref[...])
        yield
        zz = w0_ref[...] + _bdot(jnp.tanh(lora_w), wl2_ref[...])
        a_lin = a0_ref[...] + _bdot(lora_a, al2_ref[...])
        g_ref[rs, :] = _bdot(_sigmoid(lora_g), gl2_ref[...])
        yield
        softplus = jnp.maximum(-zz, 0.0) + jnp.log(1.0 + jnp.exp(-jnp.abs(zz)))
        lw_ref[rs, :] = -jnp.exp(-softplus - 0.5)
        a = _sigmoid(a_lin)
        a_ref[rs, :] = a
        k_ref[rs, :] = k * (1.0 + (a - 1.0) * ka_ref[...])
        kk = k * kk_ref[...]
        ss = _head_sums(kk * kk, seg_ref[...], 1)
        yield
        kkn_ref[rs, :] = kk * lax.rsqrt(jnp.maximum(ss, 1e-24))

        scale = MEM_HEAD_DIM ** -0.5
        heads = [slice(hh * MEM_HEAD_DIM, (hh + 1) * MEM_HEAD_DIM) for hh in range(MEM_HEADS)]
        scores = [_bdot_nt(q[:, sl], km_ref[0, :, sl]) * scale for sl in heads]
        yield
        for sl, s in zip(heads, scores):
            p = jnp.exp(s - jnp.max(s, axis=-1, keepdims=True))
            o = _bdot(p, vm_ref[0, :, sl]) / jnp.sum(p, axis=-1, keepdims=True)
            ymem_ref[rs, sl] = o.astype(BF16)

    _software_pipeline(project, mix, tm // sub)
    prev_h[...] = carry["h"]
    prev_p[...] = carry["p"]
    prev_cu[...] = carry["cu"]


def _prologue(x2, b, km, vm, p):
    t, d = x2.shape
    s = t // b
    tm = TM_PROLOGUE
    db = D_BRANCH
    m = km.shape[1]
    steps = s // tm
    tok = lambda c: pl.BlockSpec((tm, c), lambda bi, i: (bi * steps + i, 0))
    consts = [p["g_mix"], p["w_in"], p["conv_w"], p["mu_rkv"], p["mu_wag"],
              p["w_lora1"], p["w_lora2"], p["w0"], p["a_lora1"], p["a_lora2"], p["a0"],
              p["g_lora1"], p["g_lora2"], p["k_k"], p["k_a"], p["seg_ones"]]
    out_shapes = ([jax.ShapeDtypeStruct((t, db), BF16)] * 2
                  + [jax.ShapeDtypeStruct((t, db), F32)] * 7)
    return pl.pallas_call(
        _prologue_kernel,
        grid=(b, steps),
        in_specs=[tok(d)] + [_const_spec(c.shape) for c in consts]
                 + [pl.BlockSpec((1, m, db), lambda bi, i: (bi, 0, 0))] * 2,
        out_specs=[tok(db)] * 9,
        out_shape=out_shapes,
        scratch_shapes=[pltpu.VMEM((1, d), F32), pltpu.VMEM((1, 3 * db), F32),
                        pltpu.VMEM((2, db), F32)],
        compiler_params=pltpu.CompilerParams(dimension_semantics=("arbitrary", "arbitrary"),
                                             vmem_limit_bytes=VMEM_LIMIT),
        name="prologue",
    )(x2, *consts, km, vm)


def _rwkv_kernel(r_ref, k_ref, v_ref, kk_ref, a_ref, lw_ref, g_ref, rk_ref, lnw_ref, lnb_ref,
                 tri_ref, seg_ref, out_ref, h_scr, y_scr):
    tb = r_ref.shape[0]
    n = HEAD_DIM
    c_len = CHUNK

    @pl.when(pl.program_id(1) == 0)
    def _():
        h_scr[...] = jnp.zeros_like(h_scr)

    pw = 2 * n
    row1 = lax.broadcasted_iota(jnp.int32, (c_len, pw), 0)
    lane1 = lax.broadcasted_iota(jnp.int32, (c_len, pw), 1)
    col1 = lane1 & (n - 1)
    left = lane1 < n
    strict1 = col1 < row1
    incl1 = col1 <= row1
    eye2 = (col1 == row1).astype(F32)
    zeros_pair = jnp.zeros((c_len, pw), F32)
    zeros_bd = jnp.zeros((2 * c_len, pw), F32)

    def block_diag(y):
        return jnp.concatenate([jnp.where(left, y, 0.0), jnp.where(left, 0.0, y)], axis=0)

    def pair_transpose(y):
        zt = block_diag(y).T
        return zt[:c_len] + zt[c_len:]

    def chunk_inputs(c):
        rows = pl.ds(pl.multiple_of(c * c_len, c_len), c_len)
        r = r_ref[rows, :]
        k = k_ref[rows, :]
        v = v_ref[rows, :]
        kk = kk_ref[rows, :]
        a = a_ref[rows, :]
        lw = lw_ref[rows, :]
        gcum = _split_dot_left(tri_ref[...], lw, 2)
        e_pos = jnp.exp(gcum)
        e_neg = jnp.exp(-gcum)
        p_last = jnp.exp(gcum[c_len - 1:c_len, :])
        bb = kk * a * e_neg
        kb = k * e_neg
        return dict(rows=rows, v=v, p_last=p_last, rb=r * e_pos, ab=-kk * jnp.exp(gcum - lw), bb=bb, kb=kb,
                    bbp=bb * p_last, kbp=kb * p_last)

    def chunk_group(it, carry):
        chunks = [chunk_inputs(it * CHUNKS_PER_ITER + ci) for ci in range(CHUNKS_PER_ITER)]
        n_pairs = N_HEADS // 2
        units = [(ci, pr) for ci in range(CHUNKS_PER_ITER) for pr in range(n_pairs)]
        nu = range(len(units))
        ls = [slice(pr * pw, (pr + 1) * pw) for _, pr in units]
        ch = [chunks[ci] for ci, _ in units]
        al = [ch[u]["ab"][:, ls[u]] for u in nu]
        rr = [ch[u]["rb"][:, ls[u]] for u in nu]
        v_bd = [block_diag(ch[u]["v"][:, ls[u]]) for u in nu]
        aa = [_bdot_nt(jnp.concatenate([al[u], rr[u]], axis=0),
                       jnp.concatenate([block_diag(ch[u]["bb"][:, ls[u]]), block_diag(ch[u]["kb"][:, ls[u]])],
                                       axis=0)) for u in nu]
        a_ab = [jnp.where(strict1, aa[u][:c_len, :pw], 0.0) for u in nu]
        a_ak = [jnp.where(strict1, aa[u][:c_len, pw:], 0.0) for u in nu]
        a_rb = [jnp.where(incl1, aa[u][c_len:, :pw], 0.0) for u in nu]
        a_rk = [jnp.where(incl1, aa[u][c_len:, pw:], 0.0) for u in nu]
        av = [_bdot(a_ak[u], v_bd[u]) for u in nu]
        t_inv = [eye2 + a_ab[u] for u in nu]
        x_pow = [_bdot(a_ab[u], block_diag(a_ab[u])) for u in nu]
        for lvl in range(5):
            if lvl < 4:
                z = [_bdot(jnp.concatenate([t_inv[u], x_pow[u]], axis=0), block_diag(x_pow[u])) for u in nu]
                t_inv = [t_inv[u] + z[u][:c_len] for u in nu]
                x_pow = [z[u][c_len:] for u in nu]
            else:
                t_inv = [t_inv[u] + _bdot(t_inv[u], block_diag(x_pow[u])) for u in nu]
        w12 = [_bdot(t_inv[u], jnp.concatenate([block_diag(al[u]), block_diag(av[u])], axis=1))
               for u in nu]
        z2 = []
        for u in nu:
            rhs2 = jnp.concatenate(
                [jnp.concatenate([block_diag(w12[u][:, :pw]), block_diag(w12[u][:, pw:])], axis=1),
                 jnp.concatenate([zeros_bd, v_bd[u]], axis=1)], axis=0)
            lhs3 = jnp.concatenate(
                [jnp.concatenate([pair_transpose(ch[u]["bbp"][:, ls[u]]),
                                  pair_transpose(ch[u]["kbp"][:, ls[u]])], axis=1),
                 jnp.concatenate([a_rb[u], a_rk[u]], axis=1)], axis=0)
            z2.append(_bdot(lhs3, rhs2))
        state = [h_scr[pr] for pr in range(n_pairs)]
        for u in nu:
            pr = units[u][1]
            mq = z2[u][:, :pw] + jnp.concatenate([zeros_pair, rr[u]], axis=0)
            out = _bdot(mq, block_diag(state[pr])) + z2[u][:, pw:]
            decay = eye2 * ch[u]["p_last"][:, ls[u]]
            p_mat = jnp.where(left, jnp.sum(jnp.where(left, decay, 0.0), axis=1, keepdims=True),
                              jnp.sum(jnp.where(left, 0.0, decay), axis=1, keepdims=True))
            state[pr] = p_mat * state[pr] + out[:c_len]
            y_scr[ch[u]["rows"], ls[u]] = out[c_len:]
        for pr in range(n_pairs):
            h_scr[pr] = state[pr]
        return carry

    lax.fori_loop(0, tb // (c_len * CHUNKS_PER_ITER), chunk_group, 0)

    y = y_scr[...]
    seg = seg_ref[...]
    inv_n = 1.0 / n
    mu = _head_sums(y, seg, 2) * inv_n
    yc = y - mu
    var = _head_sums(yc * yc, seg, 1) * inv_n
    yn = yc * lax.rsqrt(var + GN_EPS) * lnw_ref[...] + lnb_ref[...]
    bonus = _head_sums(r_ref[...] * k_ref[...] * rk_ref[...], seg, 1) * v_ref[...]
    out_ref[...] = ((yn + bonus) * g_ref[...]).astype(BF16)


def _rwkv(r, k, v, kkn, a, lw, g, b, p):
    t, db = r.shape
    tb = TB_RWKV
    steps = t // b // tb
    tok = pl.BlockSpec((tb, db), lambda bi, i: (bi * steps + i, 0))
    consts = [p["r_k"], p["ln_x_w"], p["ln_x_b"], p["tri"], p["seg_ones"]]
    return pl.pallas_call(
        _rwkv_kernel,
        grid=(b, steps),
        in_specs=[tok] * 7 + [_const_spec(c.shape) for c in consts],
        out_specs=tok,
        out_shape=jax.ShapeDtypeStruct((t, db), BF16),
        scratch_shapes=[pltpu.VMEM((N_HEADS // 2, HEAD_DIM, 2 * HEAD_DIM), F32),
                        pltpu.VMEM((tb, db), F32)],
        compiler_params=pltpu.CompilerParams(dimension_semantics=("arbitrary", "arbitrary"),
                                             vmem_limit_bytes=VMEM_LIMIT),
        name="rwkv",
    )(r, k, v, kkn, a, lw, g, *consts)


def _merge_kernel(x_ref, yc_ref, yr_ref, ym_ref, gmix_ref, wgate_ref, bgate_ref, wbr_ref, wo_ref,
                  gffn_ref, wrt_ref, brt_ref, tril_ref,
                  x1_ref, meta_ref, metat_ref, cnt_ref, base_scr):
    tm, d = x_ref.shape

    @pl.when(pl.program_id(0) == 0)
    def _():
        base_scr[...] = jnp.zeros_like(base_scr)

    sub = tril_ref.shape[0]
    lane = lax.broadcasted_iota(jnp.int32, (sub, LANES), 1)
    neg = jnp.float32(-jnp.inf)
    big = jnp.int32(1 << 20)
    w_hi, w_lo = _split_terms(wrt_ref[...], 2)
    w_hi_lo = jnp.concatenate([w_hi, w_lo], axis=1)
    state = {"base": base_scr[...]}
    merged = {}

    def project(j):
        rs = slice(j * sub, (j + 1) * sub)
        x = x_ref[rs, :]
        hb = _rms(x, gmix_ref[...]).astype(BF16)
        z = jnp.zeros((sub, d), F32)
        for i, y_ref in enumerate((yc_ref, yr_ref, ym_ref)):
            cs = slice(i * d, (i + 1) * d)
            gate = _sigmoid(jnp.dot(hb, wgate_ref[:, cs], preferred_element_type=F32) + bgate_ref[:, cs])
            z = z + gate * jnp.dot(y_ref[rs, :], wbr_ref[i], preferred_element_type=F32)
            yield
        x1 = x + _bdot(z, wo_ref[...])
        x1_ref[rs, :] = x1
        merged[j] = x1

    def route(j):
        rs = slice(j * sub, (j + 1) * sub)
        h2 = _rms(merged.pop(j), gffn_ref[...])
        h_hi, h_lo = _split_terms(h2, 2)
        hi_terms = jnp.dot(h_hi, w_hi_lo, preferred_element_type=F32)
        logits = (hi_terms[:, :LANES]
                  + (jnp.dot(h_lo, w_hi, preferred_element_type=F32) + hi_terms[:, LANES:])) + brt_ref[...]
        yield
        gmask = (lane >= N_EXPERTS) & (lane < N_EXPERTS + N_GROUPS)
        glv = jnp.where(gmask, logits, neg)
        gmax = jnp.max(glv, axis=-1, keepdims=True)
        g_sel = jnp.min(jnp.where(glv == gmax, lane - N_EXPERTS, big), axis=-1, keepdims=True)
        g_w = 1.0 / jnp.sum(jnp.exp(glv - gmax), axis=-1, keepdims=True)
        emask = (lane < N_EXPERTS) & ((lane >> (EXPERTS_PER_GROUP.bit_length() - 1)) == g_sel)
        elv = jnp.where(emask, logits, neg)
        emax = jnp.max(elv, axis=-1, keepdims=True)
        esum = jnp.sum(jnp.exp(elv - emax), axis=-1, keepdims=True)
        i1 = jnp.min(jnp.where(elv == emax, lane, big), axis=-1, keepdims=True)
        elv2 = jnp.where(lane == i1, neg, elv)
        m2 = jnp.max(elv2, axis=-1, keepdims=True)
        i2 = jnp.min(jnp.where(elv2 == m2, lane, big), axis=-1, keepdims=True)
        p1 = 1.0 / esum
        p2 = jnp.exp(m2 - emax) / esum
        c1 = g_w * p1 / (p1 + p2)
        c2 = g_w * p2 / (p1 + p2)

        oh1 = lane == i1
        oh2 = lane == i2
        onehot = jnp.where(oh1 | oh2, 1.0, 0.0)
        before = jnp.dot(tril_ref[...], onehot.astype(BF16), preferred_element_type=F32) + state["base"]
        yield
        rank1 = jnp.sum(jnp.where(oh1, before, 0.0), axis=-1, keepdims=True)
        rank2 = jnp.sum(jnp.where(oh2, before, 0.0), axis=-1, keepdims=True)
        state["base"] = state["base"] + jnp.sum(onehot, axis=0, keepdims=True)

        meta = jnp.where(lane == 0, i1.astype(F32),
               jnp.where(lane == 1, i2.astype(F32),
               jnp.where(lane == 2, rank1,
               jnp.where(lane == 3, rank2,
               jnp.where(lane == 4, c1,
               jnp.where(lane == 5, c2, 0.0))))))
        meta_ref[rs, :] = meta[:, :META_COLS]
        metat_ref[:, rs] = meta.T[:META_COLS, :]

    _software_pipeline(project, route, tm // sub)
    base_scr[...] = state["base"]
    cnt_ref[...] = jnp.broadcast_to(state["base"], cnt_ref.shape)


def _merge(x2, yc, yr, ym, p):
    t, d = x2.shape
    tm = TM_MERGE
    db = D_BRANCH
    tok = lambda c: pl.BlockSpec((tm, c), lambda i: (i, 0))
    consts = [p["g_mix"], p["w_gate"], p["b_gate"], p["w_branch"], p["w_o"], p["g_ffn"],
              p["w_router"], p["b_router"], p["tril_strict"]]
    return pl.pallas_call(
        _merge_kernel,
        grid=(t // tm,),
        in_specs=[tok(d), tok(db), tok(db), tok(db)] + [_const_spec(c.shape) for c in consts],
        out_specs=[tok(d), tok(META_COLS), pl.BlockSpec((META_COLS, tm), lambda i: (0, i)),
                   _const_spec((8, LANES))],
        out_shape=[jax.ShapeDtypeStruct((t, d), F32), jax.ShapeDtypeStruct((t, META_COLS), F32),
                   jax.ShapeDtypeStruct((META_COLS, t), F32), jax.ShapeDtypeStruct((8, LANES), F32)],
        scratch_shapes=[pltpu.VMEM((1, LANES), F32)],
        compiler_params=pltpu.CompilerParams(dimension_semantics=("arbitrary",),
                                             vmem_limit_bytes=VMEM_LIMIT),
        name="merge",
    )(x2, yc, yr, ym, *consts)


def _store_packed_rows(ref2d, x, stage):
    rows, d = x.shape
    nt = d // (2 * LANES)
    for c in range(nt):
        stage[c, pl.ds(0, rows, stride=2), :] = x[:, c * LANES:(c + 1) * LANES]
        stage[c, pl.ds(1, rows, stride=2), :] = x[:, (c + nt) * LANES:(c + nt + 1) * LANES]
        ref2d[pl.ds(c, rows, stride=nt), :] = pltpu.bitcast(stage[c].astype(BF16), jnp.uint32)


def _load_packed_rows(ref2d, rows, nt, stage):
    lo, hi = [], []
    for c in range(nt):
        stage[c] = pltpu.bitcast(ref2d[pl.ds(c, rows, stride=nt), :], BF16).astype(F32)
        lo.append(stage[c, pl.ds(0, rows, stride=2), :])
        hi.append(stage[c, pl.ds(1, rows, stride=2), :])
    return jnp.concatenate(lo + hi, axis=1)


def _scatter_kernel(dest0_ref, dest1_ref, x1_ref, gffn_ref, xs_ref, hbuf, stage, sem):
    dest_refs = (dest0_ref, dest1_ref)
    ts, d_model = x1_ref.shape
    nt = d_model // (2 * LANES)
    s = pl.program_id(0)
    slot = s % 2

    def wait_slot(sl):
        for _ in range(TOP_K):
            pltpu.make_async_copy(hbuf.at[sl], xs_ref.at[pl.ds(0, ts * nt), :], sem.at[sl]).wait()

    @pl.when(s >= 2)
    def _():
        wait_slot(slot)

    _store_packed_rows(hbuf.at[slot], _rms(x1_ref[...], gffn_ref[...]), stage)

    def issue(grp, carry):
        grp_off = pl.multiple_of(grp * (SUBLANES * nt), SUBLANES * nt)
        for j in range(SUBLANES):
            tok = s * ts + grp * SUBLANES + j
            for kslot in range(TOP_K):
                d = pl.multiple_of(dest_refs[kslot][tok], nt)
                pltpu.make_async_copy(hbuf.at[slot, pl.ds(grp_off + j * nt, nt), :],
                                      xs_ref.at[pl.ds(d, nt), :], sem.at[slot]).start(priority=kslot)
        return carry

    lax.fori_loop(0, ts // SUBLANES, issue, 0, unroll=ISSUE_UNROLL)

    @pl.when(s == pl.num_programs(0) - 1)
    def _():
        @pl.when(s >= 1)
        def _():
            wait_slot(1 - slot)
        wait_slot(slot)


def _scatter(dests, x1, g_ffn, n_rows):
    t, d = x1.shape
    ts = TS_SCATTER
    pt = d // (2 * LANES)
    return pl.pallas_call(
        _scatter_kernel,
        grid_spec=pltpu.PrefetchScalarGridSpec(
            num_scalar_prefetch=TOP_K,
            grid=(t // ts,),
            in_specs=[pl.BlockSpec((ts, d), lambda i, *_: (i, 0)),
                      pl.BlockSpec((1, d), lambda i, *_: (0, 0))],
            out_specs=pl.BlockSpec(memory_space=pl.ANY),
            scratch_shapes=[pltpu.VMEM((2, ts * pt, LANES), jnp.uint32),
                            pltpu.VMEM((pt, 2 * ts, LANES), F32),
                            pltpu.SemaphoreType.DMA((2,))],
        ),
        out_shape=jax.ShapeDtypeStruct((n_rows * pt, LANES), jnp.uint32),
        compiler_params=pltpu.CompilerParams(dimension_semantics=("arbitrary",),
                                             vmem_limit_bytes=VMEM_LIMIT),
        name="scatter",
    )(*dests, x1, g_ffn)


def _experts_kernel(be_ref, nused_ref, nexte_ref, wslot_ref, nvalid_ref, xs_ref, wg_hbm, wu_hbm, wd_hbm, ys_ref,
                    wg_f, wu_f, wd_f, wg_s, wu_s, wd_s, xbuf, ybuf, xstage, ystage, xstage_h, ystage_h,
                    sem, xsem, ysem):
    i = pl.program_id(0)
    e = be_ref[i]
    prev = be_ref[jnp.maximum(i - 1, 0)]
    active = i < nused_ref[0]

    def weight_copies(ex):
        ws = wslot_ref[ex]
        return (pltpu.make_async_copy(wg_hbm.at[ex], wg_f.at[ws], sem.at[ws, 0]),
                pltpu.make_async_copy(wu_hbm.at[ex], wu_f.at[ws], sem.at[ws, 1]),
                pltpu.make_async_copy(wd_hbm.at[ex], wd_f.at[ws], sem.at[ws, 2]))

    def start_weights(ex):
        @pl.when(ex < N_EXPERTS)
        def _():
            for cp in weight_copies(ex):
                cp.start(priority=1)

    @pl.when(i == 0)
    def _():
        start_weights(e)
        start_weights(nexte_ref[e])

    @pl.when(active & ((i == 0) | (e != prev)))
    def _():
        for cp in weight_copies(e):
            cp.wait()
        ws = wslot_ref[e]
        wg_s[...] = wg_f[ws].astype(BF16)
        wu_s[...] = wu_f[ws].astype(BF16)
        wd_s[...] = wd_f[ws].astype(BF16)
        nxt = nexte_ref[e]
        start_weights(jnp.where(nxt < N_EXPERTS, nexte_ref[jnp.minimum(nxt, N_EXPERTS - 1)], N_EXPERTS))

    blk_rows = xbuf.shape[1]
    n_used = nused_ref[0]

    def xs_copy(blk, slot):
        return pltpu.make_async_copy(xs_ref.at[pl.ds(pl.multiple_of(blk * blk_rows, blk_rows), blk_rows), :],
                                     xbuf.at[slot], xsem.at[slot])

    def ys_copy(blk, slot):
        return pltpu.make_async_copy(ybuf.at[slot],
                                     ys_ref.at[pl.ds(pl.multiple_of(blk * blk_rows, blk_rows), blk_rows), :],
                                     ysem.at[slot])

    @pl.when(i == 0)
    def _():
        for ahead in range(XS_BUFFERS - 1):
            @pl.when(ahead < n_used)
            def _():
                xs_copy(ahead, ahead).start()

    @pl.when(active)
    def _():
        nt = wg_s.shape[0] // (2 * LANES)
        fetch = i + (XS_BUFFERS - 1)

        @pl.when(fetch < n_used)
        def _():
            xs_copy(fetch, fetch % XS_BUFFERS).start()

        xs_copy(i, i % XS_BUFFERS).wait()
        oslot = i % 2

        @pl.when(i >= 2)
        def _():
            ys_copy(i - 2, oslot).wait()

        def expert_mlp(rows, x_stage, y_stage):
            xb = _load_packed_rows(xbuf.at[i % XS_BUFFERS], rows, nt, x_stage).astype(BF16)
            gate = jnp.dot(xb, wg_s[...], preferred_element_type=F32)
            up = jnp.dot(xb, wu_s[...], preferred_element_type=F32)
            hid = gate * _sigmoid(gate) * up
            _store_packed_rows(ybuf.at[oslot], jnp.dot(hid.astype(BF16), wd_s[...], preferred_element_type=F32),
                               y_stage)

        half_rows = ROW_BLOCK // 2
        half_only = nvalid_ref[i] <= half_rows

        @pl.when(half_only)
        def _():
            expert_mlp(half_rows, xstage_h, ystage_h)
            ybuf[oslot, pl.ds(half_rows * nt, half_rows * nt), :] = pltpu.bitcast(
                jnp.zeros((2 * half_rows * nt, LANES), BF16), jnp.uint32)

        @pl.when(jnp.logical_not(half_only))
        def _():
            expert_mlp(ROW_BLOCK, xstage, ystage)

        ys_copy(i, oslot).start()

        @pl.when(i == n_used - 1)
        def _():
            @pl.when(i >= 1)
            def _():
                ys_copy(i - 1, 1 - oslot).wait()
            ys_copy(i, oslot).wait()


def _experts(blk_expert, n_used, next_expert, weight_slot, blk_valid, xs, w_gate, w_up, w_down):
    d, de = w_gate.shape[-2:]
    pt = d // (2 * LANES)
    blk_rows = ROW_BLOCK * pt
    nb = xs.shape[0] // blk_rows

    return pl.pallas_call(
        _experts_kernel,
        grid_spec=pltpu.PrefetchScalarGridSpec(
            num_scalar_prefetch=5,
            grid=(nb,),
            in_specs=[pl.BlockSpec(memory_space=pl.ANY)] * 4,
            out_specs=pl.BlockSpec(memory_space=pl.ANY),
            scratch_shapes=[pltpu.VMEM((2, d, de), F32), pltpu.VMEM((2, d, de), F32),
                            pltpu.VMEM((2, de, d), F32),
                            pltpu.VMEM((d, de), BF16), pltpu.VMEM((d, de), BF16), pltpu.VMEM((de, d), BF16),
                            pltpu.VMEM((XS_BUFFERS, blk_rows, LANES), jnp.uint32),
                            pltpu.VMEM((2, blk_rows, LANES), jnp.uint32),
                            pltpu.VMEM((pt, 2 * ROW_BLOCK, LANES), F32),
                            pltpu.VMEM((pt, 2 * ROW_BLOCK, LANES), F32),
                            pltpu.VMEM((pt, ROW_BLOCK, LANES), F32),
                            pltpu.VMEM((pt, ROW_BLOCK, LANES), F32),
                            pltpu.SemaphoreType.DMA((2, 3)), pltpu.SemaphoreType.DMA((XS_BUFFERS,)),
                            pltpu.SemaphoreType.DMA((2,))],
        ),
        out_shape=jax.ShapeDtypeStruct(xs.shape, jnp.uint32),
        compiler_params=pltpu.CompilerParams(dimension_semantics=("arbitrary",),
                                             vmem_limit_bytes=VMEM_LIMIT),
        name="experts",
    )(blk_expert, n_used, next_expert, weight_slot, blk_valid, xs, w_gate, w_up, w_down)


def _combine_kernel(dest0_ref, dest1_ref, x1_ref, meta_ref, gfin_ref, ys_ref, out_ref, ybuf, stage, sem):
    dest_refs = (dest0_ref, dest1_ref)
    te = x1_ref.shape[0]
    s = pl.program_id(0)
    nsteps = pl.num_programs(0)
    slot = s % 2

    nt = x1_ref.shape[1] // (2 * LANES)

    def issue_step(step, sl):
        def issue(grp, carry):
            grp_off = pl.multiple_of(grp * (SUBLANES * nt), SUBLANES * nt)
            for j in range(SUBLANES):
                tok = step * te + grp * SUBLANES + j
                for kslot in range(TOP_K):
                    d = pl.multiple_of(dest_refs[kslot][tok], nt)
                    pltpu.make_async_copy(ys_ref.at[pl.ds(d, nt), :],
                                          ybuf.at[sl, kslot, pl.ds(grp_off + j * nt, nt), :],
                                          sem.at[sl]).start(priority=kslot)
            return carry
        lax.fori_loop(0, te // SUBLANES, issue, 0, unroll=ISSUE_UNROLL)

    @pl.when(s == 0)
    def _():
        issue_step(0, 0)

    @pl.when(s + 1 < nsteps)
    def _():
        issue_step(s + 1, 1 - slot)

    for kslot in range(TOP_K):
        pltpu.make_async_copy(ys_ref.at[pl.ds(0, te * nt), :], ybuf.at[slot, kslot], sem.at[slot]).wait()

    meta = meta_ref[...]
    y0 = _load_packed_rows(ybuf.at[slot, 0], te, nt, stage.at[0])
    y1 = _load_packed_rows(ybuf.at[slot, 1], te, nt, stage.at[1])
    x2 = x1_ref[...] + y0 * meta[:, 4:5] + y1 * meta[:, 5:6]
    out_ref[...] = _rms(x2, gfin_ref[...])


def _combine(dests, x1, meta, g_final, ys):
    t, d = x1.shape
    te = TE_COMBINE
    pt = d // (2 * LANES)
    return pl.pallas_call(
        _combine_kernel,
        grid_spec=pltpu.PrefetchScalarGridSpec(
            num_scalar_prefetch=TOP_K,
            grid=(t // te,),
            in_specs=[pl.BlockSpec((te, d), lambda i, *_: (i, 0)),
                      pl.BlockSpec((te, META_COLS), lambda i, *_: (i, 0)),
                      pl.BlockSpec((1, d), lambda i, *_: (0, 0)),
                      pl.BlockSpec(memory_space=pl.ANY)],
            out_specs=pl.BlockSpec((te, d), lambda i, *_: (i, 0)),
            scratch_shapes=[pltpu.VMEM((2, TOP_K, te * pt, LANES), jnp.uint32),
                            pltpu.VMEM((TOP_K, pt, 2 * te, LANES), F32),
                            pltpu.SemaphoreType.DMA((2,))],
        ),
        out_shape=jax.ShapeDtypeStruct((t, d), F32),
        compiler_params=pltpu.CompilerParams(dimension_semantics=("arbitrary",),
                                             vmem_limit_bytes=VMEM_LIMIT),
        name="combine",
    )(*dests, x1, meta, g_final, ys)


def _constants(rank_rows):
    n = CHUNK
    tri = (jnp.arange(n)[:, None] >= jnp.arange(n)[None, :]).astype(BF16)
    head = jnp.arange(2 * LANES) // HEAD_DIM
    seg_ones = (head[:, None] == head[None, :]).astype(BF16)
    tril_strict = (jnp.arange(rank_rows)[:, None] > jnp.arange(rank_rows)[None, :]).astype(BF16)
    return tri, seg_ones, tril_strict


def kernel(x, mem, g_mix, g_mem, w_in, conv_w, mu_rkv, mu_wag, w_lora1, w_lora2, w0, a_lora1, a_lora2, a0, g_lora1, g_lora2, k_k, k_a, r_k, ln_x_w, ln_x_b, w_kv_mem, w_branch, w_gate, b_gate, w_o, g_ffn, w_router_group, b_router_group, w_router_expert, b_router_expert, w_exp_gate, w_exp_up, w_exp_down, g_final):
    assert g_mix.shape[0] == 1, "single-layer block"
    b, s, d = x.shape
    t = b * s
    tri, seg_ones, tril_strict = _constants(MERGE_SUB)
    row = lambda a: a.reshape(1, -1)
    pad_r = LANES - N_EXPERTS - N_GROUPS
    p = {
        "g_mix": row(g_mix[0]), "w_in": w_in[0].astype(BF16), "conv_w": conv_w[0].T,
        "mu_rkv": row(mu_rkv[0]), "mu_wag": mu_wag[0],
        "w_lora1": w_lora1[0].astype(BF16), "w_lora2": w_lora2[0].astype(BF16), "w0": row(w0[0]),
        "a_lora1": a_lora1[0].astype(BF16), "a_lora2": a_lora2[0].astype(BF16), "a0": row(a0[0]),
        "g_lora1": g_lora1[0].astype(BF16), "g_lora2": g_lora2[0].astype(BF16),
        "k_k": row(k_k[0]), "k_a": row(k_a[0]), "r_k": row(r_k[0]),
        "ln_x_w": row(ln_x_w[0]), "ln_x_b": row(ln_x_b[0]),
        "w_gate": w_gate[0].astype(BF16), "b_gate": row(b_gate[0]),
        "w_branch": w_branch[0].astype(BF16), "w_o": w_o[0].astype(BF16), "g_ffn": row(g_ffn[0]),
        "w_router": jnp.concatenate([w_router_expert[0], w_router_group[0],
                                     jnp.zeros((d, pad_r), F32)], axis=1),
        "b_router": row(jnp.concatenate([b_router_expert[0], b_router_group[0],
                                         jnp.zeros((pad_r,), F32)])),
        "tri": tri, "seg_ones": seg_ones, "tril_strict": tril_strict,
    }

    km, vm = _memkv(mem, row(g_mem[0]), w_kv_mem[0].astype(BF16))
    x2 = x.reshape(t, d)
    yconv, ymem, r, k, v, kkn, a, lw, g = _prologue(x2, b, km, vm, p)
    yrwkv = _rwkv(r, k, v, kkn, a, lw, g, b, p)
    x1, meta, meta_t, cnt = _merge(x2, yconv, yrwkv, ymem, p)

    counts = cnt[0, :N_EXPERTS].astype(jnp.int32)
    padded = ((counts + ROW_BLOCK - 1) // ROW_BLOCK) * ROW_BLOCK
    pad_end = jnp.cumsum(padded)
    pad_start = pad_end - padded
    n_blocks = (t * TOP_K) // ROW_BLOCK + N_EXPERTS
    eids = jnp.arange(N_EXPERTS, dtype=jnp.int32)
    e_idx = meta_t[0:TOP_K].astype(jnp.int32)
    rank = meta_t[TOP_K:2 * TOP_K].astype(jnp.int32)
    start_of = jnp.sum(jnp.where(e_idx[:, None, :] == eids[None, :, None], pad_start[None, :, None], 0), axis=1)
    dest = (start_of + rank) * (d // (2 * LANES))
    dests = [dest[kslot] for kslot in range(TOP_K)]
    blk_start = jnp.arange(n_blocks, dtype=jnp.int32) * ROW_BLOCK
    blk_expert = jnp.minimum(jnp.sum((pad_end[None, :] <= blk_start[:, None]).astype(jnp.int32), axis=1),
                             N_EXPERTS - 1)
    n_used = (pad_end[-1:] // ROW_BLOCK).astype(jnp.int32)
    later_nonempty = (eids[None, :] > eids[:, None]) & (counts[None, :] > 0)
    next_expert = jnp.min(jnp.where(later_nonempty, eids[None, :], N_EXPERTS), axis=1)
    weight_slot = (jnp.cumsum((counts > 0).astype(jnp.int32)) - 1) & 1
    of_blk = blk_expert[:, None] == eids[None, :]
    blk_valid = jnp.clip(jnp.sum(jnp.where(of_blk, (pad_start + counts)[None, :], 0), axis=1) - blk_start,
                         0, ROW_BLOCK)

    xs = _scatter(dests, x1, p["g_ffn"], n_blocks * ROW_BLOCK)
    ys = _experts(blk_expert, n_used, next_expert, weight_slot, blk_valid, xs,
                  w_exp_gate[0], w_exp_up[0], w_exp_down[0])
    out = _combine(dests, x1, meta, row(g_final), ys)
    return out.reshape(b, s, d)
```

```python
import jax
import jax.numpy as jnp
from jax import lax
from jax.experimental import pallas as pl
from jax.experimental.pallas import tpu as pltpu

F32 = jnp.float32
BF16 = jnp.bfloat16

NORM_EPS = 1e-6
GN_EPS = 64e-5
D_BRANCH = 512
HEAD_DIM = 64
N_HEADS = 8
CHUNK = 64
CHUNKS_PER_ITER = 8
MEM_HEADS = 4
MEM_HEAD_DIM = 128
N_GROUPS = 8
EXPERTS_PER_GROUP = 8
N_EXPERTS = 64
TOP_K = 2
ROW_BLOCK = 256
XS_BUFFERS = 4
LANES = 128
VMEM_LIMIT = 56 * 1024 * 1024

TM_PROLOGUE = 512
PROLOGUE_SUB = 256
TB_RWKV = 512
TM_MERGE = 1024
MERGE_SUB = 512
TS_SCATTER = 1024
TE_COMBINE = 256
SUBLANES = 8
META_COLS = 8
ISSUE_UNROLL = 4


def _bdot(a, b):
    return jnp.dot(a.astype(BF16), b.astype(BF16), preferred_element_type=F32)


def _bdot_nt(a, b):
    return lax.dot_general(a.astype(BF16), b.astype(BF16), (((1,), (1,)), ((), ())),
                           preferred_element_type=F32)


def _split_terms(x, n_terms):
    terms = []
    for _ in range(n_terms):
        t = x.astype(BF16)
        terms.append(t)
        x = x - t.astype(F32)
    return terms


def _split_dot_left(m_bf16, x, n_terms):
    return sum(jnp.dot(m_bf16, t, preferred_element_type=F32) for t in _split_terms(x, n_terms))


def _head_sums(x, seg_bf16, n_terms):
    w = seg_bf16.shape[0]
    terms = _split_terms(x, n_terms)
    halves = [sum(jnp.dot(t[:, c:c + w], seg_bf16, preferred_element_type=F32) for t in terms)
              for c in range(0, x.shape[1], w)]
    return jnp.concatenate(halves, axis=1)


def _rms(x, g):
    return x * lax.rsqrt(jnp.mean(x * x, axis=-1, keepdims=True) + NORM_EPS) * g


def _sigmoid(x):
    return 1.0 / (1.0 + jnp.exp(-x))


def _run_together(*gens):
    live = list(gens)
    while live:
        for gen in list(live):
            try:
                next(gen)
            except StopIteration:
                live.remove(gen)


def _software_pipeline(heavy, light, n_sub):
    _run_together(heavy(0))
    for j in range(1, n_sub):
        _run_together(heavy(j), light(j - 1))
    _run_together(light(n_sub - 1))


def _const_spec(shape):
    n = len(shape)
    return pl.BlockSpec(shape, lambda *_: (0,) * n)


def _memkv_kernel(mem_ref, g_ref, w_ref, k_ref, v_ref):
    mn = _rms(mem_ref[0], g_ref[...])
    kv = _bdot(mn, w_ref[...])
    k_ref[0] = kv[:, :D_BRANCH].astype(BF16)
    v_ref[0] = kv[:, D_BRANCH:].astype(BF16)


def _memkv(mem, g_mem, w_kv):
    b, m, d = mem.shape
    return pl.pallas_call(
        _memkv_kernel,
        grid=(b,),
        in_specs=[pl.BlockSpec((1, m, d), lambda i: (i, 0, 0)),
                  _const_spec((1, d)), _const_spec((d, 2 * D_BRANCH))],
        out_specs=[pl.BlockSpec((1, m, D_BRANCH), lambda i: (i, 0, 0)),
                   pl.BlockSpec((1, m, D_BRANCH), lambda i: (i, 0, 0))],
        out_shape=[jax.ShapeDtypeStruct((b, m, D_BRANCH), BF16)] * 2,
        compiler_params=pltpu.CompilerParams(dimension_semantics=("arbitrary",),
                                             vmem_limit_bytes=VMEM_LIMIT),
        name="memkv",
    )(mem, g_mem, w_kv)


def _prologue_kernel(x_ref, gmix_ref, win_ref, convw_ref, murkv_ref, muwag_ref,
                     wl1_ref, wl2_ref, w0_ref, al1_ref, al2_ref, a0_ref, gl1_ref, gl2_ref,
                     kk_ref, ka_ref, seg_ref, km_ref, vm_ref,
                     yconv_ref, ymem_ref, r_ref, k_ref, v_ref, kkn_ref, a_ref, lw_ref, g_ref,
                     prev_h, prev_p, prev_cu):
    tm = x_ref.shape[0]
    db = D_BRANCH
    sub = PROLOGUE_SUB
    n_parts = win_ref.shape[1] // db

    @pl.when(pl.program_id(1) == 0)
    def _():
        prev_h[...] = jnp.zeros_like(prev_h)
        prev_p[...] = jnp.zeros_like(prev_p)
        prev_cu[...] = jnp.zeros_like(prev_cu)

    rows = lax.broadcasted_iota(jnp.int32, (sub, 1), 0)

    def shift1(u, prev_row):
        return jnp.where(rows == 0, prev_row, pltpu.roll(u, 1, axis=0))

    carry = {"h": prev_h[...], "p": prev_p[...], "cu": prev_cu[...]}
    projected = {}

    def project(j):
        h = _rms(x_ref[j * sub:(j + 1) * sub, :], gmix_ref[...])
        hb = h.astype(BF16)
        parts = []
        for c in range(n_parts):
            parts.append(jnp.dot(hb, win_ref[:, c * db:(c + 1) * db], preferred_element_type=F32))
            yield
        projected[j] = (h, parts)

    def mix(j):
        rs = slice(j * sub, (j + 1) * sub)
        h, (bg, cg, u, rp, kp, vp, q) = projected.pop(j)

        cu = cg * u
        pcu = carry["cu"]
        cu1 = shift1(cu, pcu[1:2, :])
        cu2 = jnp.where(rows == 0, pcu[0:1, :], jnp.where(rows == 1, pcu[1:2, :], pltpu.roll(cu, 2, axis=0)))
        conv = cu2 * convw_ref[0:1, :] + cu1 * convw_ref[1:2, :] + cu * convw_ref[2:3, :]
        yconv_ref[rs, :] = (bg * conv).astype(BF16)
        carry["cu"] = cu[sub - 2:sub, :]

        pr = jnp.concatenate([rp, kp, vp], axis=1)
        prs = shift1(pr, carry["p"])
        mixed = pr + (prs - pr) * murkv_ref[...]
        carry["p"] = pr[sub - 1:sub, :]
        r, k, v = mixed[:, :db], mixed[:, db:2 * db], mixed[:, 2 * db:]
        r_ref[rs, :] = r
        v_ref[rs, :] = v

        dh = shift1(h, carry["h"]) - h
        carry["h"] = h[sub - 1:sub, :]
        lora_w = _bdot(h + dh * muwag_ref[0:1, :], wl1_ref[...])
        lora_a = _bdot(h + dh * muwag_ref[1:2, :], al1_ref[...])
        lora_g = _bdot(h + dh * muwag_ref[2:3, :], gl1_ref[...])
        yield
        zz = w0_ref[...] + _bdot(jnp.tanh(lora_w), wl2_ref[...])
        a_lin = a0_ref[...] + _bdot(lora_a, al2_ref[...])
        g_ref[rs, :] = _bdot(_sigmoid(lora_g), gl2_ref[...])
        yield
        softplus = jnp.maximum(-zz, 0.0) + jnp.log(1.0 + jnp.exp(-jnp.abs(zz)))
        lw_ref[rs, :] = -jnp.exp(-softplus - 0.5)
        a = _sigmoid(a_lin)
        a_ref[rs, :] = a
        k_ref[rs, :] = k * (1.0 + (a - 1.0) * ka_ref[...])
        kk = k * kk_ref[...]
        ss = _head_sums(kk * kk, seg_ref[...], 1)
        yield
        kkn_ref[rs, :] = kk * lax.rsqrt(jnp.maximum(ss, 1e-24))

        scale = MEM_HEAD_DIM ** -0.5
        heads = [slice(hh * MEM_HEAD_DIM, (hh + 1) * MEM_HEAD_DIM) for hh in range(MEM_HEADS)]
        scores = [_bdot_nt(q[:, sl], km_ref[0, :, sl]) * scale for sl in heads]
        yield
        for sl, s in zip(heads, scores):
            p = jnp.exp(s - jnp.max(s, axis=-1, keepdims=True))
            o = _bdot(p, vm_ref[0, :, sl]) / jnp.sum(p, axis=-1, keepdims=True)
            ymem_ref[rs, sl] = o.astype(BF16)

    _software_pipeline(project, mix, tm // sub)
    prev_h[...] = carry["h"]
    prev_p[...] = carry["p"]
    prev_cu[...] = carry["cu"]


def _prologue(x2, b, km, vm, p):
    t, d = x2.shape
    s = t // b
    tm = TM_PROLOGUE
    db = D_BRANCH
    m = km.shape[1]
    steps = s // tm
    tok = lambda c: pl.BlockSpec((tm, c), lambda bi, i: (bi * steps + i, 0))
    consts = [p["g_mix"], p["w_in"], p["conv_w"], p["mu_rkv"], p["mu_wag"],
              p["w_lora1"], p["w_lora2"], p["w0"], p["a_lora1"], p["a_lora2"], p["a0"],
              p["g_lora1"], p["g_lora2"], p["k_k"], p["k_a"], p["seg_ones"]]
    out_shapes = ([jax.ShapeDtypeStruct((t, db), BF16)] * 2
                  + [jax.ShapeDtypeStruct((t, db), F32)] * 7)
    return pl.pallas_call(
        _prologue_kernel,
        grid=(b, steps),
        in_specs=[tok(d)] + [_const_spec(c.shape) for c in consts]
                 + [pl.BlockSpec((1, m, db), lambda bi, i: (bi, 0, 0))] * 2,
        out_specs=[tok(db)] * 9,
        out_shape=out_shapes,
        scratch_shapes=[pltpu.VMEM((1, d), F32), pltpu.VMEM((1, 3 * db), F32),
                        pltpu.VMEM((2, db), F32)],
        compiler_params=pltpu.CompilerParams(dimension_semantics=("arbitrary", "arbitrary"),
                                             vmem_limit_bytes=VMEM_LIMIT),
        name="prologue",
    )(x2, *consts, km, vm)


def _rwkv_kernel(r_ref, k_ref, v_ref, kk_ref, a_ref, lw_ref, g_ref, rk_ref, lnw_ref, lnb_ref,
                 tri_ref, seg_ref, out_ref, h_scr, y_scr):
    tb = r_ref.shape[0]
    n = HEAD_DIM
    c_len = CHUNK

    @pl.when(pl.program_id(1) == 0)
    def _():
        h_scr[...] = jnp.zeros_like(h_scr)

    pw = 2 * n
    row1 = lax.broadcasted_iota(jnp.int32, (c_len, pw), 0)
    lane1 = lax.broadcasted_iota(jnp.int32, (c_len, pw), 1)
    col1 = lane1 & (n - 1)
    left = lane1 < n
    strict1 = col1 < row1
    incl1 = col1 <= row1
    eye2 = (col1 == row1).astype(F32)
    zeros_pair = jnp.zeros((c_len, pw), F32)
    zeros_bd = jnp.zeros((2 * c_len, pw), F32)

    def block_diag(y):
        return jnp.concatenate([jnp.where(left, y, 0.0), jnp.where(left, 0.0, y)], axis=0)

    def pair_transpose(y):
        zt = block_diag(y).T
        return zt[:c_len] + zt[c_len:]

    def chunk_inputs(c):
        rows = pl.ds(pl.multiple_of(c * c_len, c_len), c_len)
        r = r_ref[rows, :]
        k = k_ref[rows, :]
        v = v_ref[rows, :]
        kk = kk_ref[rows, :]
        a = a_ref[rows, :]
        lw = lw_ref[rows, :]
        gcum = _split_dot_left(tri_ref[...], lw, 2)
        e_pos = jnp.exp(gcum)
        e_neg = jnp.exp(-gcum)
        p_last = jnp.exp(gcum[c_len - 1:c_len, :])
        bb = kk * a * e_neg
        kb = k * e_neg
        return dict(rows=rows, v=v, p_last=p_last, rb=r * e_pos, ab=-kk * jnp.exp(gcum - lw), bb=bb, kb=kb,
                    bbp=bb * p_last, kbp=kb * p_last)

    def chunk_group(it, carry):
        chunks = [chunk_inputs(it * CHUNKS_PER_ITER + ci) for ci in range(CHUNKS_PER_ITER)]
        n_pairs = N_HEADS // 2
        units = [(ci, pr) for ci in range(CHUNKS_PER_ITER) for pr in range(n_pairs)]
        nu = range(len(units))
        ls = [slice(pr * pw, (pr + 1) * pw) for _, pr in units]
        ch = [chunks[ci] for ci, _ in units]
        al = [ch[u]["ab"][:, ls[u]] for u in nu]
        rr = [ch[u]["rb"][:, ls[u]] for u in nu]
        v_bd = [block_diag(ch[u]["v"][:, ls[u]]) for u in nu]
        aa = [_bdot_nt(jnp.concatenate([al[u], rr[u]], axis=0),
                       jnp.concatenate([block_diag(ch[u]["bb"][:, ls[u]]), block_diag(ch[u]["kb"][:, ls[u]])],
                                       axis=0)) for u in nu]
        a_ab = [jnp.where(strict1, aa[u][:c_len, :pw], 0.0) for u in nu]
        a_ak = [jnp.where(strict1, aa[u][:c_len, pw:], 0.0) for u in nu]
        a_rb = [jnp.where(incl1, aa[u][c_len:, :pw], 0.0) for u in nu]
        a_rk = [jnp.where(incl1, aa[u][c_len:, pw:], 0.0) for u in nu]
        av = [_bdot(a_ak[u], v_bd[u]) for u in nu]
        t_inv = [eye2 + a_ab[u] for u in nu]
        x_pow = [_bdot(a_ab[u], block_diag(a_ab[u])) for u in nu]
        for lvl in range(5):
            if lvl < 4:
                z = [_bdot(jnp.concatenate([t_inv[u], x_pow[u]], axis=0), block_diag(x_pow[u])) for u in nu]
                t_inv = [t_inv[u] + z[u][:c_len] for u in nu]
                x_pow = [z[u][c_len:] for u in nu]
            else:
                t_inv = [t_inv[u] + _bdot(t_inv[u], block_diag(x_pow[u])) for u in nu]
        w12 = [_bdot(t_inv[u], jnp.concatenate([block_diag(al[u]), block_diag(av[u])], axis=1))
               for u in nu]
        z2 = []
        for u in nu:
            rhs2 = jnp.concatenate(
                [jnp.concatenate([block_diag(w12[u][:, :pw]), block_diag(w12[u][:, pw:])], axis=1),
                 jnp.concatenate([zeros_bd, v_bd[u]], axis=1)], axis=0)
            lhs3 = jnp.concatenate(
                [jnp.concatenate([pair_transpose(ch[u]["bbp"][:, ls[u]]),
                                  pair_transpose(ch[u]["kbp"][:, ls[u]])], axis=1),
                 jnp.concatenate([a_rb[u], a_rk[u]], axis=1)], axis=0)
            z2.append(_bdot(lhs3, rhs2))
        state = [h_scr[pr] for pr in range(n_pairs)]
        for u in nu:
            pr = units[u][1]
            mq = z2[u][:, :pw] + jnp.concatenate([zeros_pair, rr[u]], axis=0)
            out = _bdot(mq, block_diag(state[pr])) + z2[u][:, pw:]
            decay = eye2 * ch[u]["p_last"][:, ls[u]]
            p_mat = jnp.where(left, jnp.sum(jnp.where(left, decay, 0.0), axis=1, keepdims=True),
                              jnp.sum(jnp.where(left, 0.0, decay), axis=1, keepdims=True))
            state[pr] = p_mat * state[pr] + out[:c_len]
            y_scr[ch[u]["rows"], ls[u]] = out[c_len:]
        for pr in range(n_pairs):
            h_scr[pr] = state[pr]
        return carry

    lax.fori_loop(0, tb // (c_len * CHUNKS_PER_ITER), chunk_group, 0)

    y = y_scr[...]
    seg = seg_ref[...]
    inv_n = 1.0 / n
    mu = _head_sums(y, seg, 2) * inv_n
    yc = y - mu
    var = _head_sums(yc * yc, seg, 1) * inv_n
    yn = yc * lax.rsqrt(var + GN_EPS) * lnw_ref[...] + lnb_ref[...]
    bonus = _head_sums(r_ref[...] * k_ref[...] * rk_ref[...], seg, 1) * v_ref[...]
    out_ref[...] = ((yn + bonus) * g_ref[...]).astype(BF16)


def _rwkv(r, k, v, kkn, a, lw, g, b, p):
    t, db = r.shape
    tb = TB_RWKV
    steps = t // b // tb
    tok = pl.BlockSpec((tb, db), lambda bi, i: (bi * steps + i, 0))
    consts = [p["r_k"], p["ln_x_w"], p["ln_x_b"], p["tri"], p["seg_ones"]]
    return pl.pallas_call(
        _rwkv_kernel,
        grid=(b, steps),
        in_specs=[tok] * 7 + [_const_spec(c.shape) for c in consts],
        out_specs=tok,
        out_shape=jax.ShapeDtypeStruct((t, db), BF16),
        scratch_shapes=[pltpu.VMEM((N_HEADS // 2, HEAD_DIM, 2 * HEAD_DIM), F32),
                        pltpu.VMEM((tb, db), F32)],
        compiler_params=pltpu.CompilerParams(dimension_semantics=("arbitrary", "arbitrary"),
                                             vmem_limit_bytes=VMEM_LIMIT),
        name="rwkv",
    )(r, k, v, kkn, a, lw, g, *consts)


def _merge_kernel(x_ref, yc_ref, yr_ref, ym_ref, gmix_ref, wgate_ref, bgate_ref, wbr_ref, wo_ref,
                  gffn_ref, wrt_ref, brt_ref, tril_ref,
                  x1_ref, meta_ref, metat_ref, cnt_ref, base_scr):
    tm, d = x_ref.shape

    @pl.when(pl.program_id(0) == 0)
    def _():
        base_scr[...] = jnp.zeros_like(base_scr)

    sub = tril_ref.shape[0]
    lane = lax.broadcasted_iota(jnp.int32, (sub, LANES), 1)
    neg = jnp.float32(-jnp.inf)
    big = jnp.int32(1 << 20)
    w_hi, w_lo = _split_terms(wrt_ref[...], 2)
    w_hi_lo = jnp.concatenate([w_hi, w_lo], axis=1)
    state = {"base": base_scr[...]}
    merged = {}

    def project(j):
        rs = slice(j * sub, (j + 1) * sub)
        x = x_ref[rs, :]
        hb = _rms(x, gmix_ref[...]).astype(BF16)
        z = jnp.zeros((sub, d), F32)
        for i, y_ref in enumerate((yc_ref, yr_ref, ym_ref)):
            cs = slice(i * d, (i + 1) * d)
            gate = _sigmoid(jnp.dot(hb, wgate_ref[:, cs], preferred_element_type=F32) + bgate_ref[:, cs])
            z = z + gate * jnp.dot(y_ref[rs, :], wbr_ref[i], preferred_element_type=F32)
            yield
        x1 = x + _bdot(z, wo_ref[...])
        x1_ref[rs, :] = x1
        merged[j] = x1

    def route(j):
        rs = slice(j * sub, (j + 1) * sub)
        h2 = _rms(merged.pop(j), gffn_ref[...])
        h_hi, h_lo = _split_terms(h2, 2)
        hi_terms = jnp.dot(h_hi, w_hi_lo, preferred_element_type=F32)
        logits = (hi_terms[:, :LANES]
                  + (jnp.dot(h_lo, w_hi, preferred_element_type=F32) + hi_terms[:, LANES:])) + brt_ref[...]
        yield
        gmask = (lane >= N_EXPERTS) & (lane < N_EXPERTS + N_GROUPS)
        glv = jnp.where(gmask, logits, neg)
        gmax = jnp.max(glv, axis=-1, keepdims=True)
        g_sel = jnp.min(jnp.where(glv == gmax, lane - N_EXPERTS, big), axis=-1, keepdims=True)
        g_w = 1.0 / jnp.sum(jnp.exp(glv - gmax), axis=-1, keepdims=True)
        emask = (lane < N_EXPERTS) & ((lane >> (EXPERTS_PER_GROUP.bit_length() - 1)) == g_sel)
        elv = jnp.where(emask, logits, neg)
        emax = jnp.max(elv, axis=-1, keepdims=True)
        esum = jnp.sum(jnp.exp(elv - emax), axis=-1, keepdims=True)
        i1 = jnp.min(jnp.where(elv == emax, lane, big), axis=-1, keepdims=True)
        elv2 = jnp.where(lane == i1, neg, elv)
        m2 = jnp.max(elv2, axis=-1, keepdims=True)
        i2 = jnp.min(jnp.where(elv2 == m2, lane, big), axis=-1, keepdims=True)
        p1 = 1.0 / esum
        p2 = jnp.exp(m2 - emax) / esum
        c1 = g_w * p1 / (p1 + p2)
        c2 = g_w * p2 / (p1 + p2)

        oh1 = lane == i1
        oh2 = lane == i2
        onehot = jnp.where(oh1 | oh2, 1.0, 0.0)
        before = jnp.dot(tril_ref[...], onehot.astype(BF16), preferred_element_type=F32) + state["base"]
        yield
        rank1 = jnp.sum(jnp.where(oh1, before, 0.0), axis=-1, keepdims=True)
        rank2 = jnp.sum(jnp.where(oh2, before, 0.0), axis=-1, keepdims=True)
        state["base"] = state["base"] + jnp.sum(onehot, axis=0, keepdims=True)

        meta = jnp.where(lane == 0, i1.astype(F32),
               jnp.where(lane == 1, i2.astype(F32),
               jnp.where(lane == 2, rank1,
               jnp.where(lane == 3, rank2,
               jnp.where(lane == 4, c1,
               jnp.where(lane == 5, c2, 0.0))))))
        meta_ref[rs, :] = meta[:, :META_COLS]
        metat_ref[:, rs] = meta.T[:META_COLS, :]

    _software_pipeline(project, route, tm // sub)
    base_scr[...] = state["base"]
    cnt_ref[...] = jnp.broadcast_to(state["base"], cnt_ref.shape)


def _merge(x2, yc, yr, ym, p):
    t, d = x2.shape
    tm = TM_MERGE
    db = D_BRANCH
    tok = lambda c: pl.BlockSpec((tm, c), lambda i: (i, 0))
    consts = [p["g_mix"], p["w_gate"], p["b_gate"], p["w_branch"], p["w_o"], p["g_ffn"],
              p["w_router"], p["b_router"], p["tril_strict"]]
    return pl.pallas_call(
        _merge_kernel,
        grid=(t // tm,),
        in_specs=[tok(d), tok(db), tok(db), tok(db)] + [_const_spec(c.shape) for c in consts],
        out_specs=[tok(d), tok(META_COLS), pl.BlockSpec((META_COLS, tm), lambda i: (0, i)),
                   _const_spec((8, LANES))],
        out_shape=[jax.ShapeDtypeStruct((t, d), F32), jax.ShapeDtypeStruct((t, META_COLS), F32),
                   jax.ShapeDtypeStruct((META_COLS, t), F32), jax.ShapeDtypeStruct((8, LANES), F32)],
        scratch_shapes=[pltpu.VMEM((1, LANES), F32)],
        compiler_params=pltpu.CompilerParams(dimension_semantics=("arbitrary",),
                                             vmem_limit_bytes=VMEM_LIMIT),
        name="merge",
    )(x2, yc, yr, ym, *consts)


def _store_packed_rows(ref2d, x, stage):
    rows, d = x.shape
    nt = d // (2 * LANES)
    for c in range(nt):
        stage[c, pl.ds(0, rows, stride=2), :] = x[:, c * LANES:(c + 1) * LANES]
        stage[c, pl.ds(1, rows, stride=2), :] = x[:, (c + nt) * LANES:(c + nt + 1) * LANES]
        ref2d[pl.ds(c, rows, stride=nt), :] = pltpu.bitcast(stage[c].astype(BF16), jnp.uint32)


def _load_packed_rows(ref2d, rows, nt, stage):
    lo, hi = [], []
    for c in range(nt):
        stage[c] = pltpu.bitcast(ref2d[pl.ds(c, rows, stride=nt), :], BF16).astype(F32)
        lo.append(stage[c, pl.ds(0, rows, stride=2), :])
        hi.append(stage[c, pl.ds(1, rows, stride=2), :])
    return jnp.concatenate(lo + hi, axis=1)


def _scatter_kernel(dest0_ref, dest1_ref, x1_ref, gffn_ref, xs_ref, hbuf, stage, sem):
    dest_refs = (dest0_ref, dest1_ref)
    ts, d_model = x1_ref.shape
    nt = d_model // (2 * LANES)
    s = pl.program_id(0)
    slot = s % 2

    def wait_slot(sl):
        for _ in range(TOP_K):
            pltpu.make_async_copy(hbuf.at[sl], xs_ref.at[pl.ds(0, ts * nt), :], sem.at[sl]).wait()

    @pl.when(s >= 2)
    def _():
        wait_slot(slot)

    _store_packed_rows(hbuf.at[slot], _rms(x1_ref[...], gffn_ref[...]), stage)

    def issue(grp, carry):
        grp_off = pl.multiple_of(grp * (SUBLANES * nt), SUBLANES * nt)
        for j in range(SUBLANES):
            tok = s * ts + grp * SUBLANES + j
            for kslot in range(TOP_K):
                d = pl.multiple_of(dest_refs[kslot][tok], nt)
                pltpu.make_async_copy(hbuf.at[slot, pl.ds(grp_off + j * nt, nt), :],
                                      xs_ref.at[pl.ds(d, nt), :], sem.at[slot]).start(priority=kslot)
        return carry

    lax.fori_loop(0, ts // SUBLANES, issue, 0, unroll=ISSUE_UNROLL)

    @pl.when(s == pl.num_programs(0) - 1)
    def _():
        @pl.when(s >= 1)
        def _():
            wait_slot(1 - slot)
        wait_slot(slot)


def _scatter(dests, x1, g_ffn, n_rows):
    t, d = x1.shape
    ts = TS_SCATTER
    pt = d // (2 * LANES)
    return pl.pallas_call(
        _scatter_kernel,
        grid_spec=pltpu.PrefetchScalarGridSpec(
            num_scalar_prefetch=TOP_K,
            grid=(t // ts,),
            in_specs=[pl.BlockSpec((ts, d), lambda i, *_: (i, 0)),
                      pl.BlockSpec((1, d), lambda i, *_: (0, 0))],
            out_specs=pl.BlockSpec(memory_space=pl.ANY),
            scratch_shapes=[pltpu.VMEM((2, ts * pt, LANES), jnp.uint32),
                            pltpu.VMEM((pt, 2 * ts, LANES), F32),
                            pltpu.SemaphoreType.DMA((2,))],
        ),
        out_shape=jax.ShapeDtypeStruct((n_rows * pt, LANES), jnp.uint32),
        compiler_params=pltpu.CompilerParams(dimension_semantics=("arbitrary",),
                                             vmem_limit_bytes=VMEM_LIMIT),
        name="scatter",
    )(*dests, x1, g_ffn)


def _experts_kernel(be_ref, nused_ref, nexte_ref, wslot_ref, nvalid_ref, xs_ref, wg_hbm, wu_hbm, wd_hbm, ys_ref,
                    wg_f, wu_f, wd_f, wg_s, wu_s, wd_s, xbuf, ybuf, xstage, ystage, xstage_h, ystage_h,
                    sem, xsem, ysem):
    i = pl.program_id(0)
    e = be_ref[i]
    prev = be_ref[jnp.maximum(i - 1, 0)]
    active = i < nused_ref[0]

    def weight_copies(ex):
        ws = wslot_ref[ex]
        return (pltpu.make_async_copy(wg_hbm.at[ex], wg_f.at[ws], sem.at[ws, 0]),
                pltpu.make_async_copy(wu_hbm.at[ex], wu_f.at[ws], sem.at[ws, 1]),
                pltpu.make_async_copy(wd_hbm.at[ex], wd_f.at[ws], sem.at[ws, 2]))

    def start_weights(ex):
        @pl.when(ex < N_EXPERTS)
        def _():
            for cp in weight_copies(ex):
                cp.start(priority=1)

    @pl.when(i == 0)
    def _():
        start_weights(e)
        start_weights(nexte_ref[e])

    @pl.when(active & ((i == 0) | (e != prev)))
    def _():
        for cp in weight_copies(e):
            cp.wait()
        ws = wslot_ref[e]
        wg_s[...] = wg_f[ws].astype(BF16)
        wu_s[...] = wu_f[ws].astype(BF16)
        wd_s[...] = wd_f[ws].astype(BF16)
        nxt = nexte_ref[e]
        start_weights(jnp.where(nxt < N_EXPERTS, nexte_ref[jnp.minimum(nxt, N_EXPERTS - 1)], N_EXPERTS))

    blk_rows = xbuf.shape[1]
    n_used = nused_ref[0]

    def xs_copy(blk, slot):
        return pltpu.make_async_copy(xs_ref.at[pl.ds(pl.multiple_of(blk * blk_rows, blk_rows), blk_rows), :],
                                     xbuf.at[slot], xsem.at[slot])

    def ys_copy(blk, slot):
        return pltpu.make_async_copy(ybuf.at[slot],
                                     ys_ref.at[pl.ds(pl.multiple_of(blk * blk_rows, blk_rows), blk_rows), :],
                                     ysem.at[slot])

    @pl.when(i == 0)
    def _():
        for ahead in range(XS_BUFFERS - 1):
            @pl.when(ahead < n_used)
            def _():
                xs_copy(ahead, ahead).start()

    @pl.when(active)
    def _():
        nt = wg_s.shape[0] // (2 * LANES)
        fetch = i + (XS_BUFFERS - 1)

        @pl.when(fetch < n_used)
        def _():
            xs_copy(fetch, fetch % XS_BUFFERS).start()

        xs_copy(i, i % XS_BUFFERS).wait()
        oslot = i % 2

        @pl.when(i >= 2)
        def _():
            ys_copy(i - 2, oslot).wait()

        def expert_mlp(rows, x_stage, y_stage):
            xb = _load_packed_rows(xbuf.at[i % XS_BUFFERS], rows, nt, x_stage).astype(BF16)
            gate = jnp.dot(xb, wg_s[...], preferred_element_type=F32)
            up = jnp.dot(xb, wu_s[...], preferred_element_type=F32)
            hid = gate * _sigmoid(gate) * up
            _store_packed_rows(ybuf.at[oslot], jnp.dot(hid.astype(BF16), wd_s[...], preferred_element_type=F32),
                               y_stage)

        half_rows = ROW_BLOCK // 2
        half_only = nvalid_ref[i] <= half_rows

        @pl.when(half_only)
        def _():
            expert_mlp(half_rows, xstage_h, ystage_h)
            ybuf[oslot, pl.ds(half_rows * nt, half_rows * nt), :] = pltpu.bitcast(
                jnp.zeros((2 * half_rows * nt, LANES), BF16), jnp.uint32)

        @pl.when(jnp.logical_not(half_only))
        def _():
            expert_mlp(ROW_BLOCK, xstage, ystage)

        ys_copy(i, oslot).start()

        @pl.when(i == n_used - 1)
        def _():
            @pl.when(i >= 1)
            def _():
                ys_copy(i - 1, 1 - oslot).wait()
            ys_copy(i, oslot).wait()


def _experts(blk_expert, n_used, next_expert, weight_slot, blk_valid, xs, w_gate, w_up, w_down):
    d, de = w_gate.shape[-2:]
    pt = d // (2 * LANES)
    blk_rows = ROW_BLOCK * pt
    nb = xs.shape[0] // blk_rows

    return pl.pallas_call(
        _experts_kernel,
        grid_spec=pltpu.PrefetchScalarGridSpec(
            num_scalar_prefetch=5,
            grid=(nb,),
            in_specs=[pl.BlockSpec(memory_space=pl.ANY)] * 4,
            out_specs=pl.BlockSpec(memory_space=pl.ANY),
            scratch_shapes=[pltpu.VMEM((2, d, de), F32), pltpu.VMEM((2, d, de), F32),
                            pltpu.VMEM((2, de, d), F32),
                            pltpu.VMEM((d, de), BF16), pltpu.VMEM((d, de), BF16), pltpu.VMEM((de, d), BF16),
                            pltpu.VMEM((XS_BUFFERS, blk_rows, LANES), jnp.uint32),
                            pltpu.VMEM((2, blk_rows, LANES), jnp.uint32),
                            pltpu.VMEM((pt, 2 * ROW_BLOCK, LANES), F32),
                            pltpu.VMEM((pt, 2 * ROW_BLOCK, LANES), F32),
                            pltpu.VMEM((pt, ROW_BLOCK, LANES), F32),
                            pltpu.VMEM((pt, ROW_BLOCK, LANES), F32),
                            pltpu.SemaphoreType.DMA((2, 3)), pltpu.SemaphoreType.DMA((XS_BUFFERS,)),
                            pltpu.SemaphoreType.DMA((2,))],
        ),
        out_shape=jax.ShapeDtypeStruct(xs.shape, jnp.uint32),
        compiler_params=pltpu.CompilerParams(dimension_semantics=("arbitrary",),
                                             vmem_limit_bytes=VMEM_LIMIT),
        name="experts",
    )(blk_expert, n_used, next_expert, weight_slot, blk_valid, xs, w_gate, w_up, w_down)


def _combine_kernel(dest0_ref, dest1_ref, x1_ref, meta_ref, gfin_ref, ys_ref, out_ref, ybuf, stage, sem):
    dest_refs = (dest0_ref, dest1_ref)
    te = x1_ref.shape[0]
    s = pl.program_id(0)
    nsteps = pl.num_programs(0)
    slot = s % 2

    nt = x1_ref.shape[1] // (2 * LANES)

    def issue_step(step, sl):
        def issue(grp, carry):
            grp_off = pl.multiple_of(grp * (SUBLANES * nt), SUBLANES * nt)
            for j in range(SUBLANES):
                tok = step * te + grp * SUBLANES + j
                for kslot in range(TOP_K):
                    d = pl.multiple_of(dest_refs[kslot][tok], nt)
                    pltpu.make_async_copy(ys_ref.at[pl.ds(d, nt), :],
                                          ybuf.at[sl, kslot, pl.ds(grp_off + j * nt, nt), :],
                                          sem.at[sl]).start(priority=kslot)
            return carry
        lax.fori_loop(0, te // SUBLANES, issue, 0, unroll=ISSUE_UNROLL)

    @pl.when(s == 0)
    def _():
        issue_step(0, 0)

    @pl.when(s + 1 < nsteps)
    def _():
        issue_step(s + 1, 1 - slot)

    for kslot in range(TOP_K):
        pltpu.make_async_copy(ys_ref.at[pl.ds(0, te * nt), :], ybuf.at[slot, kslot], sem.at[slot]).wait()

    meta = meta_ref[...]
    y0 = _load_packed_rows(ybuf.at[slot, 0], te, nt, stage.at[0])
    y1 = _load_packed_rows(ybuf.at[slot, 1], te, nt, stage.at[1])
    x2 = x1_ref[...] + y0 * meta[:, 4:5] + y1 * meta[:, 5:6]
    out_ref[...] = _rms(x2, gfin_ref[...])


def _combine(dests, x1, meta, g_final, ys):
    t, d = x1.shape
    te = TE_COMBINE
    pt = d // (2 * LANES)
    return pl.pallas_call(
        _combine_kernel,
        grid_spec=pltpu.PrefetchScalarGridSpec(
            num_scalar_prefetch=TOP_K,
            grid=(t // te,),
            in_specs=[pl.BlockSpec((te, d), lambda i, *_: (i, 0)),
                      pl.BlockSpec((te, META_COLS), lambda i, *_: (i, 0)),
                      pl.BlockSpec((1, d), lambda i, *_: (0, 0)),
                      pl.BlockSpec(memory_space=pl.ANY)],
            out_specs=pl.BlockSpec((te, d), lambda i, *_: (i, 0)),
            scratch_shapes=[pltpu.VMEM((2, TOP_K, te * pt, LANES), jnp.uint32),
                            pltpu.VMEM((TOP_K, pt, 2 * te, LANES), F32),
                            pltpu.SemaphoreType.DMA((2,))],
        ),
        out_shape=jax.ShapeDtypeStruct((t, d), F32),
        compiler_params=pltpu.CompilerParams(dimension_semantics=("arbitrary",),
                                             vmem_limit_bytes=VMEM_LIMIT),
        name="combine",
    )(*dests, x1, meta, g_final, ys)


def _constants(rank_rows):
    n = CHUNK
    tri = (jnp.arange(n)[:, None] >= jnp.arange(n)[None, :]).astype(BF16)
    head = jnp.arange(2 * LANES) // HEAD_DIM
    seg_ones = (head[:, None] == head[None, :]).astype(BF16)
    tril_strict = (jnp.arange(rank_rows)[:, None] > jnp.arange(rank_rows)[None, :]).astype(BF16)
    return tri, seg_ones, tril_strict


def kernel(x, mem, g_mix, g_mem, w_in, conv_w, mu_rkv, mu_wag, w_lora1, w_lora2, w0, a_lora1, a_lora2, a0, g_lora1, g_lora2, k_k, k_a, r_k, ln_x_w, ln_x_b, w_kv_mem, w_branch, w_gate, b_gate, w_o, g_ffn, w_router_group, b_router_group, w_router_expert, b_router_expert, w_exp_gate, w_exp_up, w_exp_down, g_final):
    assert g_mix.shape[0] == 1, "single-layer block"
    b, s, d = x.shape
    t = b * s
    tri, seg_ones, tril_strict = _constants(MERGE_SUB)
    row = lambda a: a.reshape(1, -1)
    pad_r = LANES - N_EXPERTS - N_GROUPS
    p = {
        "g_mix": row(g_mix[0]), "w_in": w_in[0].astype(BF16), "conv_w": conv_w[0].T,
        "mu_rkv": row(mu_rkv[0]), "mu_wag": mu_wag[0],
        "w_lora1": w_lora1[0].astype(BF16), "w_lora2": w_lora2[0].astype(BF16), "w0": row(w0[0]),
        "a_lora1": a_lora1[0].astype(BF16), "a_lora2": a_lora2[0].astype(BF16), "a0": row(a0[0]),
        "g_lora1": g_lora1[0].astype(BF16), "g_lora2": g_lora2[0].astype(BF16),
        "k_k": row(k_k[0]), "k_a": row(k_a[0]), "r_k": row(r_k[0]),
        "ln_x_w": row(ln_x_w[0]), "ln_x_b": row(ln_x_b[0]),
        "w_gate": w_gate[0].astype(BF16), "b_gate": row(b_gate[0]),
        "w_branch": w_branch[0].astype(BF16), "w_o": w_o[0].astype(BF16), "g_ffn": row(g_ffn[0]),
        "w_router": jnp.concatenate([w_router_expert[0], w_router_group[0],
                                     jnp.zeros((d, pad_r), F32)], axis=1),
        "b_router": row(jnp.concatenate([b_router_expert[0], b_router_group[0],
                                         jnp.zeros((pad_r,), F32)])),
        "tri": tri, "seg_ones": seg_ones, "tril_strict": tril_strict,
    }

    km, vm = _memkv(mem, row(g_mem[0]), w_kv_mem[0].astype(BF16))
    x2 = x.reshape(t, d)
    yconv, ymem, r, k, v, kkn, a, lw, g = _prologue(x2, b, km, vm, p)
    yrwkv = _rwkv(r, k, v, kkn, a, lw, g, b, p)
    x1, meta, meta_t, cnt = _merge(x2, yconv, yrwkv, ymem, p)

    counts = cnt[0, :N_EXPERTS].astype(jnp.int32)
    padded = ((counts + ROW_BLOCK - 1) // ROW_BLOCK) * ROW_BLOCK
    pad_end = jnp.cumsum(padded)
    pad_start = pad_end - padded
    n_blocks = (t * TOP_K) // ROW_BLOCK + N_EXPERTS
    eids = jnp.arange(N_EXPERTS, dtype=jnp.int32)
    e_idx = meta_t[0:TOP_K].astype(jnp.int32)
    rank = meta_t[TOP_K:2 * TOP_K].astype(jnp.int32)
    start_of = jnp.sum(jnp.where(e_idx[:, None, :] == eids[None, :, None], pad_start[None, :, None], 0), axis=1)
    dest = (start_of + rank) * (d // (2 * LANES))
    dests = [dest[kslot] for kslot in range(TOP_K)]
    blk_start = jnp.arange(n_blocks, dtype=jnp.int32) * ROW_BLOCK
    blk_expert = jnp.minimum(jnp.sum((pad_end[None, :] <= blk_start[:, None]).astype(jnp.int32), axis=1),
                             N_EXPERTS - 1)
    n_used = (pad_end[-1:] // ROW_BLOCK).astype(jnp.int32)
    later_nonempty = (eids[None, :] > eids[:, None]) & (counts[None, :] > 0)
    next_expert = jnp.min(jnp.where(later_nonempty, eids[None, :], N_EXPERTS), axis=1)
    weight_slot = (jnp.cumsum((counts > 0).astype(jnp.int32)) - 1) & 1
    of_blk = blk_expert[:, None] == eids[None, :]
    blk_valid = jnp.clip(jnp.sum(jnp.where(of_blk, (pad_start + counts)[None, :], 0), axis=1) - blk_start,
                         0, ROW_BLOCK)

    xs = _scatter(dests, x1, p["g_ffn"], n_blocks * ROW_BLOCK)
    ys = _experts(blk_expert, n_used, next_expert, weight_slot, blk_valid, xs,
                  w_exp_gate[0], w_exp_up[0], w_exp_down[0])
    out = _combine(dests, x1, meta, row(g_final), ys)
    return out.reshape(b, s, d)
```

```python
import jax
import jax.numpy as jnp
from jax import lax
from jax.experimental import pallas as pl
from jax.experimental.pallas import tpu as pltpu

F32 = jnp.float32
BF16 = jnp.bfloat16

NORM_EPS = 1e-6
GN_EPS = 64e-5
D_BRANCH = 512
HEAD_DIM = 64
N_HEADS = 8
CHUNK = 64
CHUNKS_PER_ITER = 8
MEM_HEADS = 4
MEM_HEAD_DIM = 128
N_GROUPS = 8
EXPERTS_PER_GROUP = 8
N_EXPERTS = 64
TOP_K = 2
ROW_BLOCK = 256
XS_BUFFERS = 4
LANES = 128
VMEM_LIMIT = 58 * 1024 * 1024

TM_PROLOGUE = 1024
PROLOGUE_SUB = 256
TB_RWKV = 512
TM_MERGE = 1024
MERGE_SUB = 512
TS_SCATTER = 1024
TE_COMBINE = 256
SUBLANES = 8
META_COLS = 8
ISSUE_UNROLL = 4


def _bdot(a, b):
    return jnp.dot(a.astype(BF16), b.astype(BF16), preferred_element_type=F32)


def _bdot_nt(a, b):
    return lax.dot_general(a.astype(BF16), b.astype(BF16), (((1,), (1,)), ((), ())),
                           preferred_element_type=F32)


def _split_terms(x, n_terms):
    terms = []
    for _ in range(n_terms):
        t = x.astype(BF16)
        terms.append(t)
        x = x - t.astype(F32)
    return terms


def _split_dot_left(m_bf16, x, n_terms):
    return sum(jnp.dot(m_bf16, t, preferred_element_type=F32) for t in _split_terms(x, n_terms))


def _head_sums(x, seg_bf16, n_terms):
    w = seg_bf16.shape[0]
    terms = _split_terms(x, n_terms)
    halves = [sum(jnp.dot(t[:, c:c + w], seg_bf16, preferred_element_type=F32) for t in terms)
              for c in range(0, x.shape[1], w)]
    return jnp.concatenate(halves, axis=1)


def _rms(x, g):
    return x * lax.rsqrt(jnp.mean(x * x, axis=-1, keepdims=True) + NORM_EPS) * g


def _sigmoid(x):
    return 1.0 / (1.0 + jnp.exp(-x))


def _run_together(*gens):
    live = list(gens)
    while live:
        for gen in list(live):
            try:
                next(gen)
            except StopIteration:
                live.remove(gen)


def _software_pipeline(heavy, light, n_sub):
    _run_together(heavy(0))
    for j in range(1, n_sub):
        _run_together(heavy(j), light(j - 1))
    _run_together(light(n_sub - 1))


def _const_spec(shape):
    n = len(shape)
    return pl.BlockSpec(shape, lambda *_: (0,) * n)


def _memkv_kernel(mem_ref, g_ref, w_ref, k_ref, v_ref):
    mn = _rms(mem_ref[0], g_ref[...])
    kv = _bdot(mn, w_ref[...])
    k_ref[0] = kv[:, :D_BRANCH].astype(BF16)
    v_ref[0] = kv[:, D_BRANCH:].astype(BF16)


def _memkv(mem, g_mem, w_kv):
    b, m, d = mem.shape
    return pl.pallas_call(
        _memkv_kernel,
        grid=(b,),
        in_specs=[pl.BlockSpec((1, m, d), lambda i: (i, 0, 0)),
                  _const_spec((1, d)), _const_spec((d, 2 * D_BRANCH))],
        out_specs=[pl.BlockSpec((1, m, D_BRANCH), lambda i: (i, 0, 0)),
                   pl.BlockSpec((1, m, D_BRANCH), lambda i: (i, 0, 0))],
        out_shape=[jax.ShapeDtypeStruct((b, m, D_BRANCH), BF16)] * 2,
        compiler_params=pltpu.CompilerParams(dimension_semantics=("arbitrary",),
                                             vmem_limit_bytes=VMEM_LIMIT),
        name="memkv",
    )(mem, g_mem, w_kv)


def _prologue_kernel(x_ref, gmix_ref, win_ref, convw_ref, murkv_ref, muwag_ref,
                     wl1_ref, wl2_ref, w0_ref, al1_ref, al2_ref, a0_ref, gl1_ref, gl2_ref,
                     kk_ref, ka_ref, seg_ref, km_ref, vm_ref,
                     yconv_ref, ymem_ref, r_ref, k_ref, v_ref, kkn_ref, a_ref, lw_ref, g_ref,
                     prev_h, prev_p, prev_cu):
    tm = x_ref.shape[0]
    db = D_BRANCH
    sub = PROLOGUE_SUB
    n_parts = win_ref.shape[1] // db

    @pl.when(pl.program_id(1) == 0)
    def _():
        prev_h[...] = jnp.zeros_like(prev_h)
        prev_p[...] = jnp.zeros_like(prev_p)
        prev_cu[...] = jnp.zeros_like(prev_cu)

    rows = lax.broadcasted_iota(jnp.int32, (sub, 1), 0)

    def shift1(u, prev_row):
        return jnp.where(rows == 0, prev_row, pltpu.roll(u, 1, axis=0))

    carry = {"h": prev_h[...], "p": prev_p[...], "cu": prev_cu[...]}
    projected = {}

    def project(j):
        h = _rms(x_ref[j * sub:(j + 1) * sub, :], gmix_ref[...])
        hb = h.astype(BF16)
        parts = []
        for c in range(n_parts):
            parts.append(jnp.dot(hb, win_ref[:, c * db:(c + 1) * db], preferred_element_type=F32))
            yield
        projected[j] = (h, parts)

    def mix(j):
        rs = slice(j * sub, (j + 1) * sub)
        h, (bg, cg, u, rp, kp, vp, q) = projected.pop(j)

        cu = cg * u
        pcu = carry["cu"]
        cu1 = shift1(cu, pcu[1:2, :])
        cu2 = jnp.where(rows == 0, pcu[0:1, :], jnp.where(rows == 1, pcu[1:2, :], pltpu.roll(cu, 2, axis=0)))
        conv = cu2 * convw_ref[0:1, :] + cu1 * convw_ref[1:2, :] + cu * convw_ref[2:3, :]
        yconv_ref[rs, :] = (bg * conv).astype(BF16)
        carry["cu"] = cu[sub - 2:sub, :]

        pr = jnp.concatenate([rp, kp, vp], axis=1)
        prs = shift1(pr, carry["p"])
        mixed = pr + (prs - pr) * murkv_ref[...]
        carry["p"] = pr[sub - 1:sub, :]
        r, k, v = mixed[:, :db], mixed[:, db:2 * db], mixed[:, 2 * db:]
        r_ref[rs, :] = r
        v_ref[rs, :] = v

        dh = shift1(h, carry["h"]) - h
        carry["h"] = h[sub - 1:sub, :]
        lora_w = _bdot(h + dh * muwag_ref[0:1, :], wl1_ref[...])
        lora_a = _bdot(h + dh * muwag_ref[1:2, :], al1_ref[...])
        lora_g = _bdot(h + dh * muwag_ref[2:3, :], gl1_ref[...])
        yield
        zz = w0_ref[...] + _bdot(jnp.tanh(lora_w), wl2_ref[...])
        a_lin = a0_ref[...] + _bdot(lora_a, al2_ref[...])
        g_ref[rs, :] = _bdot(_sigmoid(lora_g), gl2_ref[...])
        yield
        softplus = jnp.maximum(-zz, 0.0) + jnp.log(1.0 + jnp.exp(-jnp.abs(zz)))
        lw_ref[rs, :] = -jnp.exp(-softplus - 0.5)
        a = _sigmoid(a_lin)
        a_ref[rs, :] = a
        k_ref[rs, :] = k * (1.0 + (a - 1.0) * ka_ref[...])
        kk = k * kk_ref[...]
        ss = _head_sums(kk * kk, seg_ref[...], 1)
        yield
        kkn_ref[rs, :] = kk * lax.rsqrt(jnp.maximum(ss, 1e-24))

        scale = MEM_HEAD_DIM ** -0.5
        heads = [slice(hh * MEM_HEAD_DIM, (hh + 1) * MEM_HEAD_DIM) for hh in range(MEM_HEADS)]
        scores = [_bdot_nt(q[:, sl], km_ref[0, :, sl]) * scale for sl in heads]
        yield
        for sl, s in zip(heads, scores):
            p = jnp.exp(s - jnp.max(s, axis=-1, keepdims=True))
            o = _bdot(p, vm_ref[0, :, sl]) / jnp.sum(p, axis=-1, keepdims=True)
            ymem_ref[rs, sl] = o.astype(BF16)

    _software_pipeline(project, mix, tm // sub)
    prev_h[...] = carry["h"]
    prev_p[...] = carry["p"]
    prev_cu[...] = carry["cu"]


def _prologue(x2, b, km, vm, p):
    t, d = x2.shape
    s = t // b
    tm = TM_PROLOGUE
    db = D_BRANCH
    m = km.shape[1]
    steps = s // tm
    tok = lambda c: pl.BlockSpec((tm, c), lambda bi, i: (bi * steps + i, 0))
    consts = [p["g_mix"], p["w_in"], p["conv_w"], p["mu_rkv"], p["mu_wag"],
              p["w_lora1"], p["w_lora2"], p["w0"], p["a_lora1"], p["a_lora2"], p["a0"],
              p["g_lora1"], p["g_lora2"], p["k_k"], p["k_a"], p["seg_ones"]]
    out_shapes = ([jax.ShapeDtypeStruct((t, db), BF16)] * 2
                  + [jax.ShapeDtypeStruct((t, db), F32)] * 7)
    return pl.pallas_call(
        _prologue_kernel,
        grid=(b, steps),
        in_specs=[tok(d)] + [_const_spec(c.shape) for c in consts]
                 + [pl.BlockSpec((1, m, db), lambda bi, i: (bi, 0, 0))] * 2,
        out_specs=[tok(db)] * 9,
        out_shape=out_shapes,
        scratch_shapes=[pltpu.VMEM((1, d), F32), pltpu.VMEM((1, 3 * db), F32),
                        pltpu.VMEM((2, db), F32)],
        compiler_params=pltpu.CompilerParams(dimension_semantics=("arbitrary", "arbitrary"),
                                             vmem_limit_bytes=VMEM_LIMIT),
        name="prologue",
    )(x2, *consts, km, vm)


def _rwkv_kernel(r_ref, k_ref, v_ref, kk_ref, a_ref, lw_ref, g_ref, rk_ref, lnw_ref, lnb_ref,
                 tri_ref, seg_ref, out_ref, h_scr, y_scr):
    tb = r_ref.shape[0]
    n = HEAD_DIM
    c_len = CHUNK

    @pl.when(pl.program_id(1) == 0)
    def _():
        h_scr[...] = jnp.zeros_like(h_scr)

    pw = 2 * n
    row1 = lax.broadcasted_iota(jnp.int32, (c_len, pw), 0)
    lane1 = lax.broadcasted_iota(jnp.int32, (c_len, pw), 1)
    col1 = lane1 & (n - 1)
    left = lane1 < n
    strict1 = col1 < row1
    incl1 = col1 <= row1
    eye2 = (col1 == row1).astype(F32)
    zeros_pair = jnp.zeros((c_len, pw), F32)
    zeros_bd = jnp.zeros((2 * c_len, pw), F32)

    def block_diag(y):
        return jnp.concatenate([jnp.where(left, y, 0.0), jnp.where(left, 0.0, y)], axis=0)

    def pair_transpose(y):
        zt = block_diag(y).T
        return zt[:c_len] + zt[c_len:]

    def chunk_inputs(c):
        rows = pl.ds(pl.multiple_of(c * c_len, c_len), c_len)
        r = r_ref[rows, :]
        k = k_ref[rows, :]
        v = v_ref[rows, :]
        kk = kk_ref[rows, :]
        a = a_ref[rows, :]
        lw = lw_ref[rows, :]
        gcum = _split_dot_left(tri_ref[...], lw, 2)
        e_pos = jnp.exp(gcum)
        e_neg = jnp.exp(-gcum)
        p_last = jnp.exp(gcum[c_len - 1:c_len, :])
        bb = kk * a * e_neg
        kb = k * e_neg
        return dict(rows=rows, v=v, p_last=p_last, rb=r * e_pos, ab=-kk * jnp.exp(gcum - lw), bb=bb, kb=kb,
                    bbp=bb * p_last, kbp=kb * p_last)

    def chunk_group(it, carry):
        chunks = [chunk_inputs(it * CHUNKS_PER_ITER + ci) for ci in range(CHUNKS_PER_ITER)]
        n_pairs = N_HEADS // 2
        units = [(ci, pr) for ci in range(CHUNKS_PER_ITER) for pr in range(n_pairs)]
        nu = range(len(units))
        ls = [slice(pr * pw, (pr + 1) * pw) for _, pr in units]
        ch = [chunks[ci] for ci, _ in units]
        al = [ch[u]["ab"][:, ls[u]] for u in nu]
        rr = [ch[u]["rb"][:, ls[u]] for u in nu]
        v_bd = [block_diag(ch[u]["v"][:, ls[u]]) for u in nu]
        aa = [_bdot_nt(jnp.concatenate([al[u], rr[u]], axis=0),
                       jnp.concatenate([block_diag(ch[u]["bb"][:, ls[u]]), block_diag(ch[u]["kb"][:, ls[u]])],
                                       axis=0)) for u in nu]
        a_ab = [jnp.where(strict1, aa[u][:c_len, :pw], 0.0) for u in nu]
        a_ak = [jnp.where(strict1, aa[u][:c_len, pw:], 0.0) for u in nu]
        a_rb = [jnp.where(incl1, aa[u][c_len:, :pw], 0.0) for u in nu]
        a_rk = [jnp.where(incl1, aa[u][c_len:, pw:], 0.0) for u in nu]
        av = [_bdot(a_ak[u], v_bd[u]) for u in nu]
        t_inv = [eye2 + a_ab[u] for u in nu]
        x_pow = [_bdot(a_ab[u], block_diag(a_ab[u])) for u in nu]
        for lvl in range(5):
            if lvl < 4:
                z = [_bdot(jnp.concatenate([t_inv[u], x_pow[u]], axis=0), block_diag(x_pow[u])) for u in nu]
                t_inv = [t_inv[u] + z[u][:c_len] for u in nu]
                x_pow = [z[u][c_len:] for u in nu]
            else:
                t_inv = [t_inv[u] + _bdot(t_inv[u], block_diag(x_pow[u])) for u in nu]
        w12 = [_bdot(t_inv[u], jnp.concatenate([block_diag(al[u]), block_diag(av[u])], axis=1))
               for u in nu]
        z2 = []
        for u in nu:
            rhs2 = jnp.concatenate(
                [jnp.concatenate([block_diag(w12[u][:, :pw]), block_diag(w12[u][:, pw:])], axis=1),
                 jnp.concatenate([zeros_bd, v_bd[u]], axis=1)], axis=0)
            lhs3 = jnp.concatenate(
                [jnp.concatenate([pair_transpose(ch[u]["bbp"][:, ls[u]]),
                                  pair_transpose(ch[u]["kbp"][:, ls[u]])], axis=1),
                 jnp.concatenate([a_rb[u], a_rk[u]], axis=1)], axis=0)
            z2.append(_bdot(lhs3, rhs2))
        state = [h_scr[pr] for pr in range(n_pairs)]
        for u in nu:
            pr = units[u][1]
            mq = z2[u][:, :pw] + jnp.concatenate([zeros_pair, rr[u]], axis=0)
            out = _bdot(mq, block_diag(state[pr])) + z2[u][:, pw:]
            decay = eye2 * ch[u]["p_last"][:, ls[u]]
            p_mat = jnp.where(left, jnp.sum(jnp.where(left, decay, 0.0), axis=1, keepdims=True),
                              jnp.sum(jnp.where(left, 0.0, decay), axis=1, keepdims=True))
            state[pr] = p_mat * state[pr] + out[:c_len]
            y_scr[ch[u]["rows"], ls[u]] = out[c_len:]
        for pr in range(n_pairs):
            h_scr[pr] = state[pr]
        return carry

    lax.fori_loop(0, tb // (c_len * CHUNKS_PER_ITER), chunk_group, 0)

    y = y_scr[...]
    seg = seg_ref[...]
    inv_n = 1.0 / n
    mu = _head_sums(y, seg, 2) * inv_n
    yc = y - mu
    var = _head_sums(yc * yc, seg, 1) * inv_n
    yn = yc * lax.rsqrt(var + GN_EPS) * lnw_ref[...] + lnb_ref[...]
    bonus = _head_sums(r_ref[...] * k_ref[...] * rk_ref[...], seg, 1) * v_ref[...]
    out_ref[...] = ((yn + bonus) * g_ref[...]).astype(BF16)


def _rwkv(r, k, v, kkn, a, lw, g, b, p):
    t, db = r.shape
    tb = TB_RWKV
    steps = t // b // tb
    tok = pl.BlockSpec((tb, db), lambda bi, i: (bi * steps + i, 0))
    consts = [p["r_k"], p["ln_x_w"], p["ln_x_b"], p["tri"], p["seg_ones"]]
    return pl.pallas_call(
        _rwkv_kernel,
        grid=(b, steps),
        in_specs=[tok] * 7 + [_const_spec(c.shape) for c in consts],
        out_specs=tok,
        out_shape=jax.ShapeDtypeStruct((t, db), BF16),
        scratch_shapes=[pltpu.VMEM((N_HEADS // 2, HEAD_DIM, 2 * HEAD_DIM), F32),
                        pltpu.VMEM((tb, db), F32)],
        compiler_params=pltpu.CompilerParams(dimension_semantics=("arbitrary", "arbitrary"),
                                             vmem_limit_bytes=VMEM_LIMIT),
        name="rwkv",
    )(r, k, v, kkn, a, lw, g, *consts)


def _merge_kernel(x_ref, yc_ref, yr_ref, ym_ref, gmix_ref, wgate_ref, bgate_ref, wbr_ref, wo_ref,
                  gffn_ref, wrt_ref, brt_ref, tril_ref,
                  x1_ref, meta_ref, metat_ref, cnt_ref, base_scr):
    tm, d = x_ref.shape

    @pl.when(pl.program_id(0) == 0)
    def _():
        base_scr[...] = jnp.zeros_like(base_scr)

    sub = tril_ref.shape[0]
    lane = lax.broadcasted_iota(jnp.int32, (sub, LANES), 1)
    neg = jnp.float32(-jnp.inf)
    big = jnp.int32(1 << 20)
    w_hi, w_lo = _split_terms(wrt_ref[...], 2)
    w_hi_lo = jnp.concatenate([w_hi, w_lo], axis=1)
    state = {"base": base_scr[...]}
    merged = {}

    def project(j):
        rs = slice(j * sub, (j + 1) * sub)
        x = x_ref[rs, :]
        hb = _rms(x, gmix_ref[...]).astype(BF16)
        z = jnp.zeros((sub, d), F32)
        for i, y_ref in enumerate((yc_ref, yr_ref, ym_ref)):
            cs = slice(i * d, (i + 1) * d)
            gate = _sigmoid(jnp.dot(hb, wgate_ref[:, cs], preferred_element_type=F32) + bgate_ref[:, cs])
            z = z + gate * jnp.dot(y_ref[rs, :], wbr_ref[i], preferred_element_type=F32)
            yield
        x1 = x + _bdot(z, wo_ref[...])
        x1_ref[rs, :] = x1
        merged[j] = x1

    def route(j):
        rs = slice(j * sub, (j + 1) * sub)
        h2 = _rms(merged.pop(j), gffn_ref[...])
        h_hi, h_lo = _split_terms(h2, 2)
        hi_terms = jnp.dot(h_hi, w_hi_lo, preferred_element_type=F32)
        logits = (hi_terms[:, :LANES]
                  + (jnp.dot(h_lo, w_hi, preferred_element_type=F32) + hi_terms[:, LANES:])) + brt_ref[...]
        yield
        gmask = (lane >= N_EXPERTS) & (lane < N_EXPERTS + N_GROUPS)
        glv = jnp.where(gmask, logits, neg)
        gmax = jnp.max(glv, axis=-1, keepdims=True)
        g_sel = jnp.min(jnp.where(glv == gmax, lane - N_EXPERTS, big), axis=-1, keepdims=True)
        g_w = 1.0 / jnp.sum(jnp.exp(glv - gmax), axis=-1, keepdims=True)
        emask = (lane < N_EXPERTS) & ((lane >> (EXPERTS_PER_GROUP.bit_length() - 1)) == g_sel)
        elv = jnp.where(emask, logits, neg)
        emax = jnp.max(elv, axis=-1, keepdims=True)
        esum = jnp.sum(jnp.exp(elv - emax), axis=-1, keepdims=True)
        i1 = jnp.min(jnp.where(elv == emax, lane, big), axis=-1, keepdims=True)
        elv2 = jnp.where(lane == i1, neg, elv)
        m2 = jnp.max(elv2, axis=-1, keepdims=True)
        i2 = jnp.min(jnp.where(elv2 == m2, lane, big), axis=-1, keepdims=True)
        p1 = 1.0 / esum
        p2 = jnp.exp(m2 - emax) / esum
        c1 = g_w * p1 / (p1 + p2)
        c2 = g_w * p2 / (p1 + p2)

        oh1 = lane == i1
        oh2 = lane == i2
        onehot = jnp.where(oh1 | oh2, 1.0, 0.0)
        before = jnp.dot(tril_ref[...], onehot.astype(BF16), preferred_element_type=F32) + state["base"]
        yield
        rank1 = jnp.sum(jnp.where(oh1, before, 0.0), axis=-1, keepdims=True)
        rank2 = jnp.sum(jnp.where(oh2, before, 0.0), axis=-1, keepdims=True)
        state["base"] = state["base"] + jnp.sum(onehot, axis=0, keepdims=True)

        meta = jnp.where(lane == 0, i1.astype(F32),
               jnp.where(lane == 1, i2.astype(F32),
               jnp.where(lane == 2, rank1,
               jnp.where(lane == 3, rank2,
               jnp.where(lane == 4, c1,
               jnp.where(lane == 5, c2, 0.0))))))
        meta_ref[rs, :] = meta[:, :META_COLS]
        metat_ref[:, rs] = meta.T[:META_COLS, :]

    _software_pipeline(project, route, tm // sub)
    base_scr[...] = state["base"]
    cnt_ref[...] = jnp.broadcast_to(state["base"], cnt_ref.shape)


def _merge(x2, yc, yr, ym, p):
    t, d = x2.shape
    tm = TM_MERGE
    db = D_BRANCH
    tok = lambda c: pl.BlockSpec((tm, c), lambda i: (i, 0))
    consts = [p["g_mix"], p["w_gate"], p["b_gate"], p["w_branch"], p["w_o"], p["g_ffn"],
              p["w_router"], p["b_router"], p["tril_strict"]]
    return pl.pallas_call(
        _merge_kernel,
        grid=(t // tm,),
        in_specs=[tok(d), tok(db), tok(db), tok(db)] + [_const_spec(c.shape) for c in consts],
        out_specs=[tok(d), tok(META_COLS), pl.BlockSpec((META_COLS, tm), lambda i: (0, i)),
                   _const_spec((8, LANES))],
        out_shape=[jax.ShapeDtypeStruct((t, d), F32), jax.ShapeDtypeStruct((t, META_COLS), F32),
                   jax.ShapeDtypeStruct((META_COLS, t), F32), jax.ShapeDtypeStruct((8, LANES), F32)],
        scratch_shapes=[pltpu.VMEM((1, LANES), F32)],
        compiler_params=pltpu.CompilerParams(dimension_semantics=("arbitrary",),
                                             vmem_limit_bytes=VMEM_LIMIT),
        name="merge",
    )(x2, yc, yr, ym, *consts)


def _store_packed_rows(ref2d, x, stage):
    rows, d = x.shape
    nt = d // (2 * LANES)
    for c in range(nt):
        stage[c, pl.ds(0, rows, stride=2), :] = x[:, c * LANES:(c + 1) * LANES]
        stage[c, pl.ds(1, rows, stride=2), :] = x[:, (c + nt) * LANES:(c + nt + 1) * LANES]
        ref2d[pl.ds(c, rows, stride=nt), :] = pltpu.bitcast(stage[c].astype(BF16), jnp.uint32)


def _load_packed_rows(ref2d, rows, nt, stage):
    lo, hi = [], []
    for c in range(nt):
        stage[c] = pltpu.bitcast(ref2d[pl.ds(c, rows, stride=nt), :], BF16).astype(F32)
        lo.append(stage[c, pl.ds(0, rows, stride=2), :])
        hi.append(stage[c, pl.ds(1, rows, stride=2), :])
    return jnp.concatenate(lo + hi, axis=1)


def _scatter_kernel(dest0_ref, dest1_ref, x1_ref, gffn_ref, xs_ref, hbuf, stage, sem):
    dest_refs = (dest0_ref, dest1_ref)
    ts, d_model = x1_ref.shape
    nt = d_model // (2 * LANES)
    s = pl.program_id(0)
    slot = s % 2

    def wait_slot(sl):
        for _ in range(TOP_K):
            pltpu.make_async_copy(hbuf.at[sl], xs_ref.at[pl.ds(0, ts * nt), :], sem.at[sl]).wait()

    @pl.when(s >= 2)
    def _():
        wait_slot(slot)

    _store_packed_rows(hbuf.at[slot], _rms(x1_ref[...], gffn_ref[...]), stage)

    def issue(grp, carry):
        grp_off = pl.multiple_of(grp * (SUBLANES * nt), SUBLANES * nt)
        for j in range(SUBLANES):
            tok = s * ts + grp * SUBLANES + j
            for kslot in range(TOP_K):
                d = pl.multiple_of(dest_refs[kslot][tok], nt)
                pltpu.make_async_copy(hbuf.at[slot, pl.ds(grp_off + j * nt, nt), :],
                                      xs_ref.at[pl.ds(d, nt), :], sem.at[slot]).start(priority=kslot)
        return carry

    lax.fori_loop(0, ts // SUBLANES, issue, 0, unroll=ISSUE_UNROLL)

    @pl.when(s == pl.num_programs(0) - 1)
    def _():
        @pl.when(s >= 1)
        def _():
            wait_slot(1 - slot)
        wait_slot(slot)


def _scatter(dests, x1, g_ffn, n_rows):
    t, d = x1.shape
    ts = TS_SCATTER
    pt = d // (2 * LANES)
    return pl.pallas_call(
        _scatter_kernel,
        grid_spec=pltpu.PrefetchScalarGridSpec(
            num_scalar_prefetch=TOP_K,
            grid=(t // ts,),
            in_specs=[pl.BlockSpec((ts, d), lambda i, *_: (i, 0)),
                      pl.BlockSpec((1, d), lambda i, *_: (0, 0))],
            out_specs=pl.BlockSpec(memory_space=pl.ANY),
            scratch_shapes=[pltpu.VMEM((2, ts * pt, LANES), jnp.uint32),
                            pltpu.VMEM((pt, 2 * ts, LANES), F32),
                            pltpu.SemaphoreType.DMA((2,))],
        ),
        out_shape=jax.ShapeDtypeStruct((n_rows * pt, LANES), jnp.uint32),
        compiler_params=pltpu.CompilerParams(dimension_semantics=("arbitrary",),
                                             vmem_limit_bytes=VMEM_LIMIT),
        name="scatter",
    )(*dests, x1, g_ffn)


def _experts_kernel(be_ref, nused_ref, nexte_ref, wslot_ref, nvalid_ref, xs_ref, wg_hbm, wu_hbm, wd_hbm, ys_ref,
                    wg_f, wu_f, wd_f, wg_s, wu_s, wd_s, xbuf, ybuf, xstage, ystage, xstage_h, ystage_h,
                    sem, xsem, ysem):
    i = pl.program_id(0)
    e = be_ref[i]
    prev = be_ref[jnp.maximum(i - 1, 0)]
    active = i < nused_ref[0]

    def weight_copies(ex):
        ws = wslot_ref[ex]
        return (pltpu.make_async_copy(wg_hbm.at[ex], wg_f.at[ws], sem.at[ws, 0]),
                pltpu.make_async_copy(wu_hbm.at[ex], wu_f.at[ws], sem.at[ws, 1]),
                pltpu.make_async_copy(wd_hbm.at[ex], wd_f.at[ws], sem.at[ws, 2]))

    def start_weights(ex):
        @pl.when(ex < N_EXPERTS)
        def _():
            for cp in weight_copies(ex):
                cp.start(priority=1)

    @pl.when(i == 0)
    def _():
        start_weights(e)
        start_weights(nexte_ref[e])

    @pl.when(active & ((i == 0) | (e != prev)))
    def _():
        for cp in weight_copies(e):
            cp.wait()
        ws = wslot_ref[e]
        wg_s[...] = wg_f[ws].astype(BF16)
        wu_s[...] = wu_f[ws].astype(BF16)
        wd_s[...] = wd_f[ws].astype(BF16)
        nxt = nexte_ref[e]
        start_weights(jnp.where(nxt < N_EXPERTS, nexte_ref[jnp.minimum(nxt, N_EXPERTS - 1)], N_EXPERTS))

    blk_rows = xbuf.shape[1]
    n_used = nused_ref[0]

    def xs_copy(blk, slot):
        return pltpu.make_async_copy(xs_ref.at[pl.ds(pl.multiple_of(blk * blk_rows, blk_rows), blk_rows), :],
                                     xbuf.at[slot], xsem.at[slot])

    def ys_copy(blk, slot):
        return pltpu.make_async_copy(ybuf.at[slot],
                                     ys_ref.at[pl.ds(pl.multiple_of(blk * blk_rows, blk_rows), blk_rows), :],
                                     ysem.at[slot])

    @pl.when(i == 0)
    def _():
        for ahead in range(XS_BUFFERS - 1):
            @pl.when(ahead < n_used)
            def _():
                xs_copy(ahead, ahead).start()

    @pl.when(active)
    def _():
        nt = wg_s.shape[0] // (2 * LANES)
        fetch = i + (XS_BUFFERS - 1)

        @pl.when(fetch < n_used)
        def _():
            xs_copy(fetch, fetch % XS_BUFFERS).start()

        xs_copy(i, i % XS_BUFFERS).wait()
        oslot = i % 2

        @pl.when(i >= 2)
        def _():
            ys_copy(i - 2, oslot).wait()

        def expert_mlp(rows, x_stage, y_stage):
            xb = _load_packed_rows(xbuf.at[i % XS_BUFFERS], rows, nt, x_stage).astype(BF16)
            gate = jnp.dot(xb, wg_s[...], preferred_element_type=F32)
            up = jnp.dot(xb, wu_s[...], preferred_element_type=F32)
            hid = gate * _sigmoid(gate) * up
            _store_packed_rows(ybuf.at[oslot], jnp.dot(hid.astype(BF16), wd_s[...], preferred_element_type=F32),
                               y_stage)

        half_rows = ROW_BLOCK // 2
        half_only = nvalid_ref[i] <= half_rows

        @pl.when(half_only)
        def _():
            expert_mlp(half_rows, xstage_h, ystage_h)
            ybuf[oslot, pl.ds(half_rows * nt, half_rows * nt), :] = pltpu.bitcast(
                jnp.zeros((2 * half_rows * nt, LANES), BF16), jnp.uint32)

        @pl.when(jnp.logical_not(half_only))
        def _():
            expert_mlp(ROW_BLOCK, xstage, ystage)

        ys_copy(i, oslot).start()

        @pl.when(i == n_used - 1)
        def _():
            @pl.when(i >= 1)
            def _():
                ys_copy(i - 1, 1 - oslot).wait()
            ys_copy(i, oslot).wait()


def _experts(blk_expert, n_used, next_expert, weight_slot, blk_valid, xs, w_gate, w_up, w_down):
    d, de = w_gate.shape[-2:]
    pt = d // (2 * LANES)
    blk_rows = ROW_BLOCK * pt
    nb = xs.shape[0] // blk_rows

    return pl.pallas_call(
        _experts_kernel,
        grid_spec=pltpu.PrefetchScalarGridSpec(
            num_scalar_prefetch=5,
            grid=(nb,),
            in_specs=[pl.BlockSpec(memory_space=pl.ANY)] * 4,
            out_specs=pl.BlockSpec(memory_space=pl.ANY),
            scratch_shapes=[pltpu.VMEM((2, d, de), F32), pltpu.VMEM((2, d, de), F32),
                            pltpu.VMEM((2, de, d), F32),
                            pltpu.VMEM((d, de), BF16), pltpu.VMEM((d, de), BF16), pltpu.VMEM((de, d), BF16),
                            pltpu.VMEM((XS_BUFFERS, blk_rows, LANES), jnp.uint32),
                            pltpu.VMEM((2, blk_rows, LANES), jnp.uint32),
                            pltpu.VMEM((pt, 2 * ROW_BLOCK, LANES), F32),
                            pltpu.VMEM((pt, 2 * ROW_BLOCK, LANES), F32),
                            pltpu.VMEM((pt, ROW_BLOCK, LANES), F32),
                            pltpu.VMEM((pt, ROW_BLOCK, LANES), F32),
                            pltpu.SemaphoreType.DMA((2, 3)), pltpu.SemaphoreType.DMA((XS_BUFFERS,)),
                            pltpu.SemaphoreType.DMA((2,))],
        ),
        out_shape=jax.ShapeDtypeStruct(xs.shape, jnp.uint32),
        compiler_params=pltpu.CompilerParams(dimension_semantics=("arbitrary",),
                                             vmem_limit_bytes=VMEM_LIMIT),
        name="experts",
    )(blk_expert, n_used, next_expert, weight_slot, blk_valid, xs, w_gate, w_up, w_down)


def _combine_kernel(dest0_ref, dest1_ref, x1_ref, meta_ref, gfin_ref, ys_ref, out_ref, ybuf, stage, sem):
    dest_refs = (dest0_ref, dest1_ref)
    te = x1_ref.shape[0]
    s = pl.program_id(0)
    nsteps = pl.num_programs(0)
    slot = s % 2

    nt = x1_ref.shape[1] // (2 * LANES)

    def issue_step(step, sl):
        def issue(grp, carry):
            grp_off = pl.multiple_of(grp * (SUBLANES * nt), SUBLANES * nt)
            for j in range(SUBLANES):
                tok = step * te + grp * SUBLANES + j
                for kslot in range(TOP_K):
                    d = pl.multiple_of(dest_refs[kslot][tok], nt)
                    pltpu.make_async_copy(ys_ref.at[pl.ds(d, nt), :],
                                          ybuf.at[sl, kslot, pl.ds(grp_off + j * nt, nt), :],
                                          sem.at[sl]).start(priority=kslot)
            return carry
        lax.fori_loop(0, te // SUBLANES, issue, 0, unroll=ISSUE_UNROLL)

    @pl.when(s == 0)
    def _():
        issue_step(0, 0)

    @pl.when(s + 1 < nsteps)
    def _():
        issue_step(s + 1, 1 - slot)

    for kslot in range(TOP_K):
        pltpu.make_async_copy(ys_ref.at[pl.ds(0, te * nt), :], ybuf.at[slot, kslot], sem.at[slot]).wait()

    meta = meta_ref[...]
    y0 = _load_packed_rows(ybuf.at[slot, 0], te, nt, stage.at[0])
    y1 = _load_packed_rows(ybuf.at[slot, 1], te, nt, stage.at[1])
    x2 = x1_ref[...] + y0 * meta[:, 4:5] + y1 * meta[:, 5:6]
    out_ref[...] = _rms(x2, gfin_ref[...])


def _combine(dests, x1, meta, g_final, ys):
    t, d = x1.shape
    te = TE_COMBINE
    pt = d // (2 * LANES)
    return pl.pallas_call(
        _combine_kernel,
        grid_spec=pltpu.PrefetchScalarGridSpec(
            num_scalar_prefetch=TOP_K,
            grid=(t // te,),
            in_specs=[pl.BlockSpec((te, d), lambda i, *_: (i, 0)),
                      pl.BlockSpec((te, META_COLS), lambda i, *_: (i, 0)),
                      pl.BlockSpec((1, d), lambda i, *_: (0, 0)),
                      pl.BlockSpec(memory_space=pl.ANY)],
            out_specs=pl.BlockSpec((te, d), lambda i, *_: (i, 0)),
            scratch_shapes=[pltpu.VMEM((2, TOP_K, te * pt, LANES), jnp.uint32),
                            pltpu.VMEM((TOP_K, pt, 2 * te, LANES), F32),
                            pltpu.SemaphoreType.DMA((2,))],
        ),
        out_shape=jax.ShapeDtypeStruct((t, d), F32),
        compiler_params=pltpu.CompilerParams(dimension_semantics=("arbitrary",),
                                             vmem_limit_bytes=VMEM_LIMIT),
        name="combine",
    )(*dests, x1, meta, g_final, ys)


def _constants(rank_rows):
    n = CHUNK
    tri = (jnp.arange(n)[:, None] >= jnp.arange(n)[None, :]).astype(BF16)
    head = jnp.arange(2 * LANES) // HEAD_DIM
    seg_ones = (head[:, None] == head[None, :]).astype(BF16)
    tril_strict = (jnp.arange(rank_rows)[:, None] > jnp.arange(rank_rows)[None, :]).astype(BF16)
    return tri, seg_ones, tril_strict


def kernel(x, mem, g_mix, g_mem, w_in, conv_w, mu_rkv, mu_wag, w_lora1, w_lora2, w0, a_lora1, a_lora2, a0, g_lora1, g_lora2, k_k, k_a, r_k, ln_x_w, ln_x_b, w_kv_mem, w_branch, w_gate, b_gate, w_o, g_ffn, w_router_group, b_router_group, w_router_expert, b_router_expert, w_exp_gate, w_exp_up, w_exp_down, g_final):
    assert g_mix.shape[0] == 1, "single-layer block"
    b, s, d = x.shape
    t = b * s
    tri, seg_ones, tril_strict = _constants(MERGE_SUB)
    row = lambda a: a.reshape(1, -1)
    pad_r = LANES - N_EXPERTS - N_GROUPS
    p = {
        "g_mix": row(g_mix[0]), "w_in": w_in[0].astype(BF16), "conv_w": conv_w[0].T,
        "mu_rkv": row(mu_rkv[0]), "mu_wag": mu_wag[0],
        "w_lora1": w_lora1[0].astype(BF16), "w_lora2": w_lora2[0].astype(BF16), "w0": row(w0[0]),
        "a_lora1": a_lora1[0].astype(BF16), "a_lora2": a_lora2[0].astype(BF16), "a0": row(a0[0]),
        "g_lora1": g_lora1[0].astype(BF16), "g_lora2": g_lora2[0].astype(BF16),
        "k_k": row(k_k[0]), "k_a": row(k_a[0]), "r_k": row(r_k[0]),
        "ln_x_w": row(ln_x_w[0]), "ln_x_b": row(ln_x_b[0]),
        "w_gate": w_gate[0].astype(BF16), "b_gate": row(b_gate[0]),
        "w_branch": w_branch[0].astype(BF16), "w_o": w_o[0].astype(BF16), "g_ffn": row(g_ffn[0]),
        "w_router": jnp.concatenate([w_router_expert[0], w_router_group[0],
                                     jnp.zeros((d, pad_r), F32)], axis=1),
        "b_router": row(jnp.concatenate([b_router_expert[0], b_router_group[0],
                                         jnp.zeros((pad_r,), F32)])),
        "tri": tri, "seg_ones": seg_ones, "tril_strict": tril_strict,
    }

    km, vm = _memkv(mem, row(g_mem[0]), w_kv_mem[0].astype(BF16))
    x2 = x.reshape(t, d)
    yconv, ymem, r, k, v, kkn, a, lw, g = _prologue(x2, b, km, vm, p)
    yrwkv = _rwkv(r, k, v, kkn, a, lw, g, b, p)
    x1, meta, meta_t, cnt = _merge(x2, yconv, yrwkv, ymem, p)

    counts = cnt[0, :N_EXPERTS].astype(jnp.int32)
    padded = ((counts + ROW_BLOCK - 1) // ROW_BLOCK) * ROW_BLOCK
    pad_end = jnp.cumsum(padded)
    pad_start = pad_end - padded
    n_blocks = (t * TOP_K) // ROW_BLOCK + N_EXPERTS
    eids = jnp.arange(N_EXPERTS, dtype=jnp.int32)
    e_idx = meta_t[0:TOP_K].astype(jnp.int32)
    rank = meta_t[TOP_K:2 * TOP_K].astype(jnp.int32)
    start_of = jnp.sum(jnp.where(e_idx[:, None, :] == eids[None, :, None], pad_start[None, :, None], 0), axis=1)
    dest = (start_of + rank) * (d // (2 * LANES))
    dests = [dest[kslot] for kslot in range(TOP_K)]
    blk_start = jnp.arange(n_blocks, dtype=jnp.int32) * ROW_BLOCK
    blk_expert = jnp.minimum(jnp.sum((pad_end[None, :] <= blk_start[:, None]).astype(jnp.int32), axis=1),
                             N_EXPERTS - 1)
    n_used = (pad_end[-1:] // ROW_BLOCK).astype(jnp.int32)
    later_nonempty = (eids[None, :] > eids[:, None]) & (counts[None, :] > 0)
    next_expert = jnp.min(jnp.where(later_nonempty, eids[None, :], N_EXPERTS), axis=1)
    weight_slot = (jnp.cumsum((counts > 0).astype(jnp.int32)) - 1) & 1
    of_blk = blk_expert[:, None] == eids[None, :]
    blk_valid = jnp.clip(jnp.sum(jnp.where(of_blk, (pad_start + counts)[None, :], 0), axis=1) - blk_start,
                         0, ROW_BLOCK)

    xs = _scatter(dests, x1, p["g_ffn"], n_blocks * ROW_BLOCK)
    ys = _experts(blk_expert, n_used, next_expert, weight_slot, blk_valid, xs,
                  w_exp_gate[0], w_exp_up[0], w_exp_down[0])
    out = _combine(dests, x1, meta, row(g_final), ys)
    return out.reshape(b, s, d)
```

```python
import jax
import jax.numpy as jnp
from jax import lax
from jax.experimental import pallas as pl
from jax.experimental.pallas import tpu as pltpu

F32 = jnp.float32
BF16 = jnp.bfloat16

NORM_EPS = 1e-6
GN_EPS = 64e-5
D_BRANCH = 512
HEAD_DIM = 64
N_HEADS = 8
CHUNK = 64
CHUNKS_PER_ITER = 8
MEM_HEADS = 4
MEM_HEAD_DIM = 128
N_GROUPS = 8
EXPERTS_PER_GROUP = 8
N_EXPERTS = 64
TOP_K = 2
ROW_BLOCK = 256
XS_BUFFERS = 4
LANES = 128
VMEM_LIMIT = 56 * 1024 * 1024

TM_PROLOGUE = 512
PROLOGUE_SUB = 256
TB_RWKV = 512
TM_MERGE = 1024
MERGE_SUB = 512
TS_SCATTER = 1024
TE_COMBINE = 256
SUBLANES = 8
META_COLS = 8
ISSUE_UNROLL = 4


def _bdot(a, b):
    return jnp.dot(a.astype(BF16), b.astype(BF16), preferred_element_type=F32)


def _bdot_nt(a, b):
    return lax.dot_general(a.astype(BF16), b.astype(BF16), (((1,), (1,)), ((), ())),
                           preferred_element_type=F32)


def _split_terms(x, n_terms):
    terms = []
    for _ in range(n_terms):
        t = x.astype(BF16)
        terms.append(t)
        x = x - t.astype(F32)
    return terms


def _split_dot_left(m_bf16, x, n_terms):
    return sum(jnp.dot(m_bf16, t, preferred_element_type=F32) for t in _split_terms(x, n_terms))


def _head_sums(x, seg_bf16, n_terms):
    w = seg_bf16.shape[0]
    terms = _split_terms(x, n_terms)
    halves = [sum(jnp.dot(t[:, c:c + w], seg_bf16, preferred_element_type=F32) for t in terms)
              for c in range(0, x.shape[1], w)]
    return jnp.concatenate(halves, axis=1)


def _rms(x, g):
    return x * lax.rsqrt(jnp.mean(x * x, axis=-1, keepdims=True) + NORM_EPS) * g


def _sigmoid(x):
    return 1.0 / (1.0 + jnp.exp(-x))


def _run_together(*gens):
    live = list(gens)
    while live:
        for gen in list(live):
            try:
                next(gen)
            except StopIteration:
                live.remove(gen)


def _software_pipeline(heavy, light, n_sub):
    _run_together(heavy(0))
    for j in range(1, n_sub):
        _run_together(heavy(j), light(j - 1))
    _run_together(light(n_sub - 1))


def _const_spec(shape):
    n = len(shape)
    return pl.BlockSpec(shape, lambda *_: (0,) * n)


def _memkv_kernel(mem_ref, g_ref, w_ref, k_ref, v_ref):
    mn = _rms(mem_ref[0], g_ref[...])
    kv = _bdot(mn, w_ref[...])
    k_ref[0] = kv[:, :D_BRANCH].astype(BF16)
    v_ref[0] = kv[:, D_BRANCH:].astype(BF16)


def _memkv(mem, g_mem, w_kv):
    b, m, d = mem.shape
    return pl.pallas_call(
        _memkv_kernel,
        grid=(b,),
        in_specs=[pl.BlockSpec((1, m, d), lambda i: (i, 0, 0)),
                  _const_spec((1, d)), _const_spec((d, 2 * D_BRANCH))],
        out_specs=[pl.BlockSpec((1, m, D_BRANCH), lambda i: (i, 0, 0)),
                   pl.BlockSpec((1, m, D_BRANCH), lambda i: (i, 0, 0))],
        out_shape=[jax.ShapeDtypeStruct((b, m, D_BRANCH), BF16)] * 2,
        compiler_params=pltpu.CompilerParams(dimension_semantics=("arbitrary",),
                                             vmem_limit_bytes=VMEM_LIMIT),
        name="memkv",
    )(mem, g_mem, w_kv)


def _prologue_kernel(x_ref, gmix_ref, win_ref, convw_ref, murkv_ref, muwag_ref,
                     wl1_ref, wl2_ref, w0_ref, al1_ref, al2_ref, a0_ref, gl1_ref, gl2_ref,
                     kk_ref, ka_ref, seg_ref, km_ref, vm_ref,
                     yconv_ref, ymem_ref, r_ref, k_ref, v_ref, kkn_ref, a_ref, lw_ref, g_ref,
                     prev_h, prev_p, prev_cu):
    tm = x_ref.shape[0]
    db = D_BRANCH
    sub = PROLOGUE_SUB
    n_parts = win_ref.shape[1] // db

    @pl.when(pl.program_id(1) == 0)
    def _():
        prev_h[...] = jnp.zeros_like(prev_h)
        prev_p[...] = jnp.zeros_like(prev_p)
        prev_cu[...] = jnp.zeros_like(prev_cu)

    rows = lax.broadcasted_iota(jnp.int32, (sub, 1), 0)

    def shift1(u, prev_row):
        return jnp.where(rows == 0, prev_row, pltpu.roll(u, 1, axis=0))

    carry = {"h": prev_h[...], "p": prev_p[...], "cu": prev_cu[...]}
    projected = {}

    def project(j):
        h = _rms(x_ref[j * sub:(j + 1) * sub, :], gmix_ref[...])
        hb = h.astype(BF16)
        parts = []
        for c in range(n_parts):
            parts.append(jnp.dot(hb, win_ref[:, c * db:(c + 1) * db], preferred_element_type=F32))
            yield
        projected[j] = (h, parts)

    def mix(j):
        rs = slice(j * sub, (j + 1) * sub)
        h, (bg, cg, u, rp, kp, vp, q) = projected.pop(j)

        cu = cg * u
        pcu = carry["cu"]
        cu1 = shift1(cu, pcu[1:2, :])
        cu2 = jnp.where(rows == 0, pcu[0:1, :], jnp.where(rows == 1, pcu[1:2, :], pltpu.roll(cu, 2, axis=0)))
        conv = cu2 * convw_ref[0:1, :] + cu1 * convw_ref[1:2, :] + cu * convw_ref[2:3, :]
        yconv_ref[rs, :] = (bg * conv).astype(BF16)
        carry["cu"] = cu[sub - 2:sub, :]

        pr = jnp.concatenate([rp, kp, vp], axis=1)
        prs = shift1(pr, carry["p"])
        mixed = pr + (prs - pr) * murkv_ref[...]
        carry["p"] = pr[sub - 1:sub, :]
        r, k, v = mixed[:, :db], mixed[:, db:2 * db], mixed[:, 2 * db:]
        r_ref[rs, :] = r
        v_ref[rs, :] = v

        dh = shift1(h, carry["h"]) - h
        carry["h"] = h[sub - 1:sub, :]
        lora_w = _bdot(h + dh * muwag_ref[0:1, :], wl1_ref[...])
        lora_a = _bdot(h + dh * muwag_ref[1:2, :], al1_ref[...])
        lora_g = _bdot(h + dh * muwag_ref[2:3, :], gl1_ref[...])
        yield
        zz = w0_ref[...] + _bdot(jnp.tanh(lora_w), wl2_ref[...])
        a_lin = a0_ref[...] + _bdot(lora_a, al2_ref[...])
        g_ref[rs, :] = _bdot(_sigmoid(lora_g), gl2_ref[...])
        yield
        softplus = jnp.maximum(-zz, 0.0) + jnp.log(1.0 + jnp.exp(-jnp.abs(zz)))
        lw_ref[rs, :] = -jnp.exp(-softplus - 0.5)
        a = _sigmoid(a_lin)
        a_ref[rs, :] = a
        k_ref[rs, :] = k * (1.0 + (a - 1.0) * ka_ref[...])
        kk = k * kk_ref[...]
        ss = _head_sums(kk * kk, seg_ref[...], 1)
        yield
        kkn_ref[rs, :] = kk * lax.rsqrt(jnp.maximum(ss, 1e-24))

        scale = MEM_HEAD_DIM ** -0.5
        heads = [slice(hh * MEM_HEAD_DIM, (hh + 1) * MEM_HEAD_DIM) for hh in range(MEM_HEADS)]
        scores = [_bdot_nt(q[:, sl], km_ref[0, :, sl]) * scale for sl in heads]
        yield
        for sl, s in zip(heads, scores):
            p = jnp.exp(s - jnp.max(s, axis=-1, keepdims=True))
            o = _bdot(p, vm_ref[0, :, sl]) / jnp.sum(p, axis=-1, keepdims=True)
            ymem_ref[rs, sl] = o.astype(BF16)

    _software_pipeline(project, mix, tm // sub)
    prev_h[...] = carry["h"]
    prev_p[...] = carry["p"]
    prev_cu[...] = carry["cu"]


def _prologue(x2, b, km, vm, p):
    t, d = x2.shape
    s = t // b
    tm = TM_PROLOGUE
    db = D_BRANCH
    m = km.shape[1]
    steps = s // tm
    tok = lambda c: pl.BlockSpec((tm, c), lambda bi, i: (bi * steps + i, 0))
    consts = [p["g_mix"], p["w_in"], p["conv_w"], p["mu_rkv"], p["mu_wag"],
              p["w_lora1"], p["w_lora2"], p["w0"], p["a_lora1"], p["a_lora2"], p["a0"],
              p["g_lora1"], p["g_lora2"], p["k_k"], p["k_a"], p["seg_ones"]]
    out_shapes = ([jax.ShapeDtypeStruct((t, db), BF16)] * 2
                  + [jax.ShapeDtypeStruct((t, db), F32)] * 7)
    return pl.pallas_call(
        _prologue_kernel,
        grid=(b, steps),
        in_specs=[tok(d)] + [_const_spec(c.shape) for c in consts]
                 + [pl.BlockSpec((1, m, db), lambda bi, i: (bi, 0, 0))] * 2,
        out_specs=[tok(db)] * 9,
        out_shape=out_shapes,
        scratch_shapes=[pltpu.VMEM((1, d), F32), pltpu.VMEM((1, 3 * db), F32),
                        pltpu.VMEM((2, db), F32)],
        compiler_params=pltpu.CompilerParams(dimension_semantics=("arbitrary", "arbitrary"),
                                             vmem_limit_bytes=VMEM_LIMIT),
        name="prologue",
    )(x2, *consts, km, vm)


def _rwkv_kernel(r_ref, k_ref, v_ref, kk_ref, a_ref, lw_ref, g_ref, rk_ref, lnw_ref, lnb_ref,
                 tri_ref, seg_ref, out_ref, h_scr, y_scr):
    tb = r_ref.shape[0]
    n = HEAD_DIM
    c_len = CHUNK

    @pl.when(pl.program_id(1) == 0)
    def _():
        h_scr[...] = jnp.zeros_like(h_scr)

    pw = 2 * n
    row1 = lax.broadcasted_iota(jnp.int32, (c_len, pw), 0)
    lane1 = lax.broadcasted_iota(jnp.int32, (c_len, pw), 1)
    col1 = lane1 & (n - 1)
    left = lane1 < n
    strict1 = col1 < row1
    incl1 = col1 <= row1
    eye2 = (col1 == row1).astype(F32)
    zeros_pair = jnp.zeros((c_len, pw), F32)
    zeros_bd = jnp.zeros((2 * c_len, pw), F32)

    def block_diag(y):
        return jnp.concatenate([jnp.where(left, y, 0.0), jnp.where(left, 0.0, y)], axis=0)

    def pair_transpose(y):
        zt = block_diag(y).T
        return zt[:c_len] + zt[c_len:]

    def chunk_inputs(c):
        rows = pl.ds(pl.multiple_of(c * c_len, c_len), c_len)
        r = r_ref[rows, :]
        k = k_ref[rows, :]
        v = v_ref[rows, :]
        kk = kk_ref[rows, :]
        a = a_ref[rows, :]
        lw = lw_ref[rows, :]
        gcum = _split_dot_left(tri_ref[...], lw, 2)
        e_pos = jnp.exp(gcum)
        e_neg = jnp.exp(-gcum)
        p_last = jnp.exp(gcum[c_len - 1:c_len, :])
        bb = kk * a * e_neg
        kb = k * e_neg
        return dict(rows=rows, v=v, p_last=p_last, rb=r * e_pos, ab=-kk * jnp.exp(gcum - lw), bb=bb, kb=kb,
                    bbp=bb * p_last, kbp=kb * p_last)

    def chunk_group(it, carry):
        chunks = [chunk_inputs(it * CHUNKS_PER_ITER + ci) for ci in range(CHUNKS_PER_ITER)]
        n_pairs = N_HEADS // 2
        units = [(ci, pr) for ci in range(CHUNKS_PER_ITER) for pr in range(n_pairs)]
        nu = range(len(units))
        ls = [slice(pr * pw, (pr + 1) * pw) for _, pr in units]
        ch = [chunks[ci] for ci, _ in units]
        al = [ch[u]["ab"][:, ls[u]] for u in nu]
        rr = [ch[u]["rb"][:, ls[u]] for u in nu]
        v_bd = [block_diag(ch[u]["v"][:, ls[u]]) for u in nu]
        aa = [_bdot_nt(jnp.concatenate([al[u], rr[u]], axis=0),
                       jnp.concatenate([block_diag(ch[u]["bb"][:, ls[u]]), block_diag(ch[u]["kb"][:, ls[u]])],
                                       axis=0)) for u in nu]
        a_ab = [jnp.where(strict1, aa[u][:c_len, :pw], 0.0) for u in nu]
        a_ak = [jnp.where(strict1, aa[u][:c_len, pw:], 0.0) for u in nu]
        a_rb = [jnp.where(incl1, aa[u][c_len:, :pw], 0.0) for u in nu]
        a_rk = [jnp.where(incl1, aa[u][c_len:, pw:], 0.0) for u in nu]
        av = [_bdot(a_ak[u], v_bd[u]) for u in nu]
        t_inv = [eye2 + a_ab[u] for u in nu]
        x_pow = [_bdot(a_ab[u], block_diag(a_ab[u])) for u in nu]
        for lvl in range(5):
            if lvl < 4:
                z = [_bdot(jnp.concatenate([t_inv[u], x_pow[u]], axis=0), block_diag(x_pow[u])) for u in nu]
                t_inv = [t_inv[u] + z[u][:c_len] for u in nu]
                x_pow = [z[u][c_len:] for u in nu]
            else:
                t_inv = [t_inv[u] + _bdot(t_inv[u], block_diag(x_pow[u])) for u in nu]
        w12 = [_bdot(t_inv[u], jnp.concatenate([block_diag(al[u]), block_diag(av[u])], axis=1))
               for u in nu]
        z2 = []
        for u in nu:
            rhs2 = jnp.concatenate(
                [jnp.concatenate([block_diag(w12[u][:, :pw]), block_diag(w12[u][:, pw:])], axis=1),
                 jnp.concatenate([zeros_bd, v_bd[u]], axis=1)], axis=0)
            lhs3 = jnp.concatenate(
                [jnp.concatenate([pair_transpose(ch[u]["bbp"][:, ls[u]]),
                                  pair_transpose(ch[u]["kbp"][:, ls[u]])], axis=1),
                 jnp.concatenate([a_rb[u], a_rk[u]], axis=1)], axis=0)
            z2.append(_bdot(lhs3, rhs2))
        state = [h_scr[pr] for pr in range(n_pairs)]
        for u in nu:
            pr = units[u][1]
            mq = z2[u][:, :pw] + jnp.concatenate([zeros_pair, rr[u]], axis=0)
            out = _bdot(mq, block_diag(state[pr])) + z2[u][:, pw:]
            decay = eye2 * ch[u]["p_last"][:, ls[u]]
            p_mat = jnp.where(left, jnp.sum(jnp.where(left, decay, 0.0), axis=1, keepdims=True),
                              jnp.sum(jnp.where(left, 0.0, decay), axis=1, keepdims=True))
            state[pr] = p_mat * state[pr] + out[:c_len]
            y_scr[ch[u]["rows"], ls[u]] = out[c_len:]
        for pr in range(n_pairs):
            h_scr[pr] = state[pr]
        return carry

    lax.fori_loop(0, tb // (c_len * CHUNKS_PER_ITER), chunk_group, 0)

    y = y_scr[...]
    seg = seg_ref[...]
    inv_n = 1.0 / n
    mu = _head_sums(y, seg, 2) * inv_n
    yc = y - mu
    var = _head_sums(yc * yc, seg, 1) * inv_n
    yn = yc * lax.rsqrt(var + GN_EPS) * lnw_ref[...] + lnb_ref[...]
    bonus = _head_sums(r_ref[...] * k_ref[...] * rk_ref[...], seg, 1) * v_ref[...]
    out_ref[...] = ((yn + bonus) * g_ref[...]).astype(BF16)


def _rwkv(r, k, v, kkn, a, lw, g, b, p):
    t, db = r.shape
    tb = TB_RWKV
    steps = t // b // tb
    tok = pl.BlockSpec((tb, db), lambda bi, i: (bi * steps + i, 0))
    consts = [p["r_k"], p["ln_x_w"], p["ln_x_b"], p["tri"], p["seg_ones"]]
    return pl.pallas_call(
        _rwkv_kernel,
        grid=(b, steps),
        in_specs=[tok] * 7 + [_const_spec(c.shape) for c in consts],
        out_specs=tok,
        out_shape=jax.ShapeDtypeStruct((t, db), BF16),
        scratch_shapes=[pltpu.VMEM((N_HEADS // 2, HEAD_DIM, 2 * HEAD_DIM), F32),
                        pltpu.VMEM((tb, db), F32)],
        compiler_params=pltpu.CompilerParams(dimension_semantics=("arbitrary", "arbitrary"),
                                             vmem_limit_bytes=VMEM_LIMIT),
        name="rwkv",
    )(r, k, v, kkn, a, lw, g, *consts)


def _merge_kernel(x_ref, yc_ref, yr_ref, ym_ref, gmix_ref, wgate_ref, bgate_ref, wbr_ref, wo_ref,
                  gffn_ref, wrt_ref, brt_ref, tril_ref,
                  x1_ref, meta_ref, metat_ref, cnt_ref, base_scr):
    tm, d = x_ref.shape

    @pl.when(pl.program_id(0) == 0)
    def _():
        base_scr[...] = jnp.zeros_like(base_scr)

    sub = tril_ref.shape[0]
    lane = lax.broadcasted_iota(jnp.int32, (sub, LANES), 1)
    neg = jnp.float32(-jnp.inf)
    big = jnp.int32(1 << 20)
    w_hi, w_lo = _split_terms(wrt_ref[...], 2)
    w_hi_lo = jnp.concatenate([w_hi, w_lo], axis=1)
    state = {"base": base_scr[...]}
    merged = {}

    def project(j):
        rs = slice(j * sub, (j + 1) * sub)
        x = x_ref[rs, :]
        hb = _rms(x, gmix_ref[...]).astype(BF16)
        z = jnp.zeros((sub, d), F32)
        for i, y_ref in enumerate((yc_ref, yr_ref, ym_ref)):
            cs = slice(i * d, (i + 1) * d)
            gate = _sigmoid(jnp.dot(hb, wgate_ref[:, cs], preferred_element_type=F32) + bgate_ref[:, cs])
            z = z + gate * jnp.dot(y_ref[rs, :], wbr_ref[i], preferred_element_type=F32)
            yield
        x1 = x + _bdot(z, wo_ref[...])
        x1_ref[rs, :] = x1
        merged[j] = x1

    def route(j):
        rs = slice(j * sub, (j + 1) * sub)
        h2 = _rms(merged.pop(j), gffn_ref[...])
        h_hi, h_lo = _split_terms(h2, 2)
        hi_terms = jnp.dot(h_hi, w_hi_lo, preferred_element_type=F32)
        logits = (hi_terms[:, :LANES]
                  + (jnp.dot(h_lo, w_hi, preferred_element_type=F32) + hi_terms[:, LANES:])) + brt_ref[...]
        yield
        gmask = (lane >= N_EXPERTS) & (lane < N_EXPERTS + N_GROUPS)
        glv = jnp.where(gmask, logits, neg)
        gmax = jnp.max(glv, axis=-1, keepdims=True)
        g_sel = jnp.min(jnp.where(glv == gmax, lane - N_EXPERTS, big), axis=-1, keepdims=True)
        g_w = 1.0 / jnp.sum(jnp.exp(glv - gmax), axis=-1, keepdims=True)
        emask = (lane < N_EXPERTS) & ((lane >> (EXPERTS_PER_GROUP.bit_length() - 1)) == g_sel)
        elv = jnp.where(emask, logits, neg)
        emax = jnp.max(elv, axis=-1, keepdims=True)
        esum = jnp.sum(jnp.exp(elv - emax), axis=-1, keepdims=True)
        i1 = jnp.min(jnp.where(elv == emax, lane, big), axis=-1, keepdims=True)
        elv2 = jnp.where(lane == i1, neg, elv)
        m2 = jnp.max(elv2, axis=-1, keepdims=True)
        i2 = jnp.min(jnp.where(elv2 == m2, lane, big), axis=-1, keepdims=True)
        p1 = 1.0 / esum
        p2 = jnp.exp(m2 - emax) / esum
        c1 = g_w * p1 / (p1 + p2)
        c2 = g_w * p2 / (p1 + p2)

        oh1 = lane == i1
        oh2 = lane == i2
        onehot = jnp.where(oh1 | oh2, 1.0, 0.0)
        before = jnp.dot(tril_ref[...], onehot.astype(BF16), preferred_element_type=F32) + state["base"]
        yield
        rank1 = jnp.sum(jnp.where(oh1, before, 0.0), axis=-1, keepdims=True)
        rank2 = jnp.sum(jnp.where(oh2, before, 0.0), axis=-1, keepdims=True)
        state["base"] = state["base"] + jnp.sum(onehot, axis=0, keepdims=True)

        meta = jnp.where(lane == 0, i1.astype(F32),
               jnp.where(lane == 1, i2.astype(F32),
               jnp.where(lane == 2, rank1,
               jnp.where(lane == 3, rank2,
               jnp.where(lane == 4, c1,
               jnp.where(lane == 5, c2, 0.0))))))
        meta_ref[rs, :] = meta[:, :META_COLS]
        metat_ref[:, rs] = meta.T[:META_COLS, :]

    _software_pipeline(project, route, tm // sub)
    base_scr[...] = state["base"]
    cnt_ref[...] = jnp.broadcast_to(state["base"], cnt_ref.shape)


def _merge(x2, yc, yr, ym, p):
    t, d = x2.shape
    tm = TM_MERGE
    db = D_BRANCH
    tok = lambda c: pl.BlockSpec((tm, c), lambda i: (i, 0))
    consts = [p["g_mix"], p["w_gate"], p["b_gate"], p["w_branch"], p["w_o"], p["g_ffn"],
              p["w_router"], p["b_router"], p["tril_strict"]]
    return pl.pallas_call(
        _merge_kernel,
        grid=(t // tm,),
        in_specs=[tok(d), tok(db), tok(db), tok(db)] + [_const_spec(c.shape) for c in consts],
        out_specs=[tok(d), tok(META_COLS), pl.BlockSpec((META_COLS, tm), lambda i: (0, i)),
                   _const_spec((8, LANES))],
        out_shape=[jax.ShapeDtypeStruct((t, d), F32), jax.ShapeDtypeStruct((t, META_COLS), F32),
                   jax.ShapeDtypeStruct((META_COLS, t), F32), jax.ShapeDtypeStruct((8, LANES), F32)],
        scratch_shapes=[pltpu.VMEM((1, LANES), F32)],
        compiler_params=pltpu.CompilerParams(dimension_semantics=("arbitrary",),
                                             vmem_limit_bytes=VMEM_LIMIT),
        name="merge",
    )(x2, yc, yr, ym, *consts)


def _store_packed_rows(ref2d, x, stage):
    rows, d = x.shape
    nt = d // (2 * LANES)
    for c in range(nt):
        stage[c, pl.ds(0, rows, stride=2), :] = x[:, c * LANES:(c + 1) * LANES]
        stage[c, pl.ds(1, rows, stride=2), :] = x[:, (c + nt) * LANES:(c + nt + 1) * LANES]
        ref2d[pl.ds(c, rows, stride=nt), :] = pltpu.bitcast(stage[c].astype(BF16), jnp.uint32)


def _load_packed_rows(ref2d, rows, nt, stage):
    lo, hi = [], []
    for c in range(nt):
        stage[c] = pltpu.bitcast(ref2d[pl.ds(c, rows, stride=nt), :], BF16).astype(F32)
        lo.append(stage[c, pl.ds(0, rows, stride=2), :])
        hi.append(stage[c, pl.ds(1, rows, stride=2), :])
    return jnp.concatenate(lo + hi, axis=1)


def _scatter_kernel(dest0_ref, dest1_ref, x1_ref, gffn_ref, xs_ref, hbuf, stage, sem):
    dest_refs = (dest0_ref, dest1_ref)
    ts, d_model = x1_ref.shape
    nt = d_model // (2 * LANES)
    s = pl.program_id(0)
    slot = s % 2

    def wait_slot(sl):
        for _ in range(TOP_K):
            pltpu.make_async_copy(hbuf.at[sl], xs_ref.at[pl.ds(0, ts * nt), :], sem.at[sl]).wait()

    @pl.when(s >= 2)
    def _():
        wait_slot(slot)

    _store_packed_rows(hbuf.at[slot], _rms(x1_ref[...], gffn_ref[...]), stage)

    def issue(grp, carry):
        grp_off = pl.multiple_of(grp * (SUBLANES * nt), SUBLANES * nt)
        for j in range(SUBLANES):
            tok = s * ts + grp * SUBLANES + j
            for kslot in range(TOP_K):
                d = pl.multiple_of(dest_refs[kslot][tok], nt)
                pltpu.make_async_copy(hbuf.at[slot, pl.ds(grp_off + j * nt, nt), :],
                                      xs_ref.at[pl.ds(d, nt), :], sem.at[slot]).start(priority=kslot)
        return carry

    lax.fori_loop(0, ts // SUBLANES, issue, 0, unroll=ISSUE_UNROLL)

    @pl.when(s == pl.num_programs(0) - 1)
    def _():
        @pl.when(s >= 1)
        def _():
            wait_slot(1 - slot)
        wait_slot(slot)


def _scatter(dests, x1, g_ffn, n_rows):
    t, d = x1.shape
    ts = TS_SCATTER
    pt = d // (2 * LANES)
    return pl.pallas_call(
        _scatter_kernel,
        grid_spec=pltpu.PrefetchScalarGridSpec(
            num_scalar_prefetch=TOP_K,
            grid=(t // ts,),
            in_specs=[pl.BlockSpec((ts, d), lambda i, *_: (i, 0)),
                      pl.BlockSpec((1, d), lambda i, *_: (0, 0))],
            out_specs=pl.BlockSpec(memory_space=pl.ANY),
            scratch_shapes=[pltpu.VMEM((2, ts * pt, LANES), jnp.uint32),
                            pltpu.VMEM((pt, 2 * ts, LANES), F32),
                            pltpu.SemaphoreType.DMA((2,))],
        ),
        out_shape=jax.ShapeDtypeStruct((n_rows * pt, LANES), jnp.uint32),
        compiler_params=pltpu.CompilerParams(dimension_semantics=("arbitrary",),
                                             vmem_limit_bytes=VMEM_LIMIT),
        name="scatter",
    )(*dests, x1, g_ffn)


def _experts_kernel(be_ref, nused_ref, nexte_ref, wslot_ref, nvalid_ref, xs_ref, wg_hbm, wu_hbm, wd_hbm, ys_ref,
                    wg_f, wu_f, wd_f, wg_s, wu_s, wd_s, xbuf, ybuf, xstage, ystage, xstage_h, ystage_h,
                    sem, xsem, ysem):
    i = pl.program_id(0)
    e = be_ref[i]
    prev = be_ref[jnp.maximum(i - 1, 0)]
    active = i < nused_ref[0]

    def weight_copies(ex):
        ws = wslot_ref[ex]
        return (pltpu.make_async_copy(wg_hbm.at[ex], wg_f.at[ws], sem.at[ws, 0]),
                pltpu.make_async_copy(wu_hbm.at[ex], wu_f.at[ws], sem.at[ws, 1]),
                pltpu.make_async_copy(wd_hbm.at[ex], wd_f.at[ws], sem.at[ws, 2]))

    def start_weights(ex):
        @pl.when(ex < N_EXPERTS)
        def _():
            for cp in weight_copies(ex):
                cp.start(priority=1)

    @pl.when(i == 0)
    def _():
        start_weights(e)
        start_weights(nexte_ref[e])

    @pl.when(active & ((i == 0) | (e != prev)))
    def _():
        for cp in weight_copies(e):
            cp.wait()
        ws = wslot_ref[e]
        wg_s[...] = wg_f[ws].astype(BF16)
        wu_s[...] = wu_f[ws].astype(BF16)
        wd_s[...] = wd_f[ws].astype(BF16)
        nxt = nexte_ref[e]
        start_weights(jnp.where(nxt < N_EXPERTS, nexte_ref[jnp.minimum(nxt, N_EXPERTS - 1)], N_EXPERTS))

    blk_rows = xbuf.shape[1]
    n_used = nused_ref[0]

    def xs_copy(blk, slot):
        return pltpu.make_async_copy(xs_ref.at[pl.ds(pl.multiple_of(blk * blk_rows, blk_rows), blk_rows), :],
                                     xbuf.at[slot], xsem.at[slot])

    def ys_copy(blk, slot):
        return pltpu.make_async_copy(ybuf.at[slot],
                                     ys_ref.at[pl.ds(pl.multiple_of(blk * blk_rows, blk_rows), blk_rows), :],
                                     ysem.at[slot])

    @pl.when(i == 0)
    def _():
        for ahead in range(XS_BUFFERS - 1):
            @pl.when(ahead < n_used)
            def _():
                xs_copy(ahead, ahead).start()

    @pl.when(active)
    def _():
        nt = wg_s.shape[0] // (2 * LANES)
        fetch = i + (XS_BUFFERS - 1)

        @pl.when(fetch < n_used)
        def _():
            xs_copy(fetch, fetch % XS_BUFFERS).start()

        xs_copy(i, i % XS_BUFFERS).wait()
        oslot = i % 2

        @pl.when(i >= 2)
        def _():
            ys_copy(i - 2, oslot).wait()

        def expert_mlp(rows, x_stage, y_stage):
            xb = _load_packed_rows(xbuf.at[i % XS_BUFFERS], rows, nt, x_stage).astype(BF16)
            gate = jnp.dot(xb, wg_s[...], preferred_element_type=F32)
            up = jnp.dot(xb, wu_s[...], preferred_element_type=F32)
            hid = gate * _sigmoid(gate) * up
            _store_packed_rows(ybuf.at[oslot], jnp.dot(hid.astype(BF16), wd_s[...], preferred_element_type=F32),
                               y_stage)

        half_rows = ROW_BLOCK // 2
        half_only = nvalid_ref[i] <= half_rows

        @pl.when(half_only)
        def _():
            expert_mlp(half_rows, xstage_h, ystage_h)
            ybuf[oslot, pl.ds(half_rows * nt, half_rows * nt), :] = pltpu.bitcast(
                jnp.zeros((2 * half_rows * nt, LANES), BF16), jnp.uint32)

        @pl.when(jnp.logical_not(half_only))
        def _():
            expert_mlp(ROW_BLOCK, xstage, ystage)

        ys_copy(i, oslot).start()

        @pl.when(i == n_used - 1)
        def _():
            @pl.when(i >= 1)
            def _():
                ys_copy(i - 1, 1 - oslot).wait()
            ys_copy(i, oslot).wait()


def _experts(blk_expert, n_used, next_expert, weight_slot, blk_valid, xs, w_gate, w_up, w_down):
    d, de = w_gate.shape[-2:]
    pt = d // (2 * LANES)
    blk_rows = ROW_BLOCK * pt
    nb = xs.shape[0] // blk_rows

    return pl.pallas_call(
        _experts_kernel,
        grid_spec=pltpu.PrefetchScalarGridSpec(
            num_scalar_prefetch=5,
            grid=(nb,),
            in_specs=[pl.BlockSpec(memory_space=pl.ANY)] * 4,
            out_specs=pl.BlockSpec(memory_space=pl.ANY),
            scratch_shapes=[pltpu.VMEM((2, d, de), F32), pltpu.VMEM((2, d, de), F32),
                            pltpu.VMEM((2, de, d), F32),
                            pltpu.VMEM((d, de), BF16), pltpu.VMEM((d, de), BF16), pltpu.VMEM((de, d), BF16),
                            pltpu.VMEM((XS_BUFFERS, blk_rows, LANES), jnp.uint32),
                            pltpu.VMEM((2, blk_rows, LANES), jnp.uint32),
                            pltpu.VMEM((pt, 2 * ROW_BLOCK, LANES), F32),
                            pltpu.VMEM((pt, 2 * ROW_BLOCK, LANES), F32),
                            pltpu.VMEM((pt, ROW_BLOCK, LANES), F32),
                            pltpu.VMEM((pt, ROW_BLOCK, LANES), F32),
                            pltpu.SemaphoreType.DMA((2, 3)), pltpu.SemaphoreType.DMA((XS_BUFFERS,)),
                            pltpu.SemaphoreType.DMA((2,))],
        ),
        out_shape=jax.ShapeDtypeStruct(xs.shape, jnp.uint32),
        compiler_params=pltpu.CompilerParams(dimension_semantics=("arbitrary",),
                                             vmem_limit_bytes=VMEM_LIMIT),
        name="experts",
    )(blk_expert, n_used, next_expert, weight_slot, blk_valid, xs, w_gate, w_up, w_down)


def _combine_kernel(dest0_ref, dest1_ref, x1_ref, meta_ref, gfin_ref, ys_ref, out_ref, ybuf, stage, sem):
    dest_refs = (dest0_ref, dest1_ref)
    te = x1_ref.shape[0]
    s = pl.program_id(0)
    nsteps = pl.num_programs(0)
    slot = s % 2

    nt = x1_ref.shape[1] // (2 * LANES)

    def issue_step(step, sl):
        def issue(grp, carry):
            grp_off = pl.multiple_of(grp * (SUBLANES * nt), SUBLANES * nt)
            for j in range(SUBLANES):
                tok = step * te + grp * SUBLANES + j
                for kslot in range(TOP_K):
                    d = pl.multiple_of(dest_refs[kslot][tok], nt)
                    pltpu.make_async_copy(ys_ref.at[pl.ds(d, nt), :],
                                          ybuf.at[sl, kslot, pl.ds(grp_off + j * nt, nt), :],
                                          sem.at[sl]).start(priority=kslot)
            return carry
        lax.fori_loop(0, te // SUBLANES, issue, 0, unroll=ISSUE_UNROLL)

    @pl.when(s == 0)
    def _():
        issue_step(0, 0)

    @pl.when(s + 1 < nsteps)
    def _():
        issue_step(s + 1, 1 - slot)

    for kslot in range(TOP_K):
        pltpu.make_async_copy(ys_ref.at[pl.ds(0, te * nt), :], ybuf.at[slot, kslot], sem.at[slot]).wait()

    meta = meta_ref[...]
    y0 = _load_packed_rows(ybuf.at[slot, 0], te, nt, stage.at[0])
    y1 = _load_packed_rows(ybuf.at[slot, 1], te, nt, stage.at[1])
    x2 = x1_ref[...] + y0 * meta[:, 4:5] + y1 * meta[:, 5:6]
    out_ref[...] = _rms(x2, gfin_ref[...])


def _combine(dests, x1, meta, g_final, ys):
    t, d = x1.shape
    te = TE_COMBINE
    pt = d // (2 * LANES)
    return pl.pallas_call(
        _combine_kernel,
        grid_spec=pltpu.PrefetchScalarGridSpec(
            num_scalar_prefetch=TOP_K,
            grid=(t // te,),
            in_specs=[pl.BlockSpec((te, d), lambda i, *_: (i, 0)),
                      pl.BlockSpec((te, META_COLS), lambda i, *_: (i, 0)),
                      pl.BlockSpec((1, d), lambda i, *_: (0, 0)),
                      pl.BlockSpec(memory_space=pl.ANY)],
            out_specs=pl.BlockSpec((te, d), lambda i, *_: (i, 0)),
            scratch_shapes=[pltpu.VMEM((2, TOP_K, te * pt, LANES), jnp.uint32),
                            pltpu.VMEM((TOP_K, pt, 2 * te, LANES), F32),
                            pltpu.SemaphoreType.DMA((2,))],
        ),
        out_shape=jax.ShapeDtypeStruct((t, d), F32),
        compiler_params=pltpu.CompilerParams(dimension_semantics=("arbitrary",),
                                             vmem_limit_bytes=VMEM_LIMIT),
        name="combine",
    )(*dests, x1, meta, g_final, ys)


def _constants(rank_rows):
    n = CHUNK
    tri = (jnp.arange(n)[:, None] >= jnp.arange(n)[None, :]).astype(BF16)
    head = jnp.arange(2 * LANES) // HEAD_DIM
    seg_ones = (head[:, None] == head[None, :]).astype(BF16)
    tril_strict = (jnp.arange(rank_rows)[:, None] > jnp.arange(rank_rows)[None, :]).astype(BF16)
    return tri, seg_ones, tril_strict


def kernel(x, mem, g_mix, g_mem, w_in, conv_w, mu_rkv, mu_wag, w_lora1, w_lora2, w0, a_lora1, a_lora2, a0, g_lora1, g_lora2, k_k, k_a, r_k, ln_x_w, ln_x_b, w_kv_mem, w_branch, w_gate, b_gate, w_o, g_ffn, w_router_group, b_router_group, w_router_expert, b_router_expert, w_exp_gate, w_exp_up, w_exp_down, g_final):
    assert g_mix.shape[0] == 1, "single-layer block"
    b, s, d = x.shape
    t = b * s
    tri, seg_ones, tril_strict = _constants(MERGE_SUB)
    row = lambda a: a.reshape(1, -1)
    pad_r = LANES - N_EXPERTS - N_GROUPS
    p = {
        "g_mix": row(g_mix[0]), "w_in": w_in[0].astype(BF16), "conv_w": conv_w[0].T,
        "mu_rkv": row(mu_rkv[0]), "mu_wag": mu_wag[0],
        "w_lora1": w_lora1[0].astype(BF16), "w_lora2": w_lora2[0].astype(BF16), "w0": row(w0[0]),
        "a_lora1": a_lora1[0].astype(BF16), "a_lora2": a_lora2[0].astype(BF16), "a0": row(a0[0]),
        "g_lora1": g_lora1[0].astype(BF16), "g_lora2": g_lora2[0].astype(BF16),
        "k_k": row(k_k[0]), "k_a": row(k_a[0]), "r_k": row(r_k[0]),
        "ln_x_w": row(ln_x_w[0]), "ln_x_b": row(ln_x_b[0]),
        "w_gate": w_gate[0].astype(BF16), "b_gate": row(b_gate[0]),
        "w_branch": w_branch[0].astype(BF16), "w_o": w_o[0].astype(BF16), "g_ffn": row(g_ffn[0]),
        "w_router": jnp.concatenate([w_router_expert[0], w_router_group[0],
                                     jnp.zeros((d, pad_r), F32)], axis=1),
        "b_router": row(jnp.concatenate([b_router_expert[0], b_router_group[0],
                                         jnp.zeros((pad_r,), F32)])),
        "tri": tri, "seg_ones": seg_ones, "tril_strict": tril_strict,
    }

    km, vm = _memkv(mem, row(g_mem[0]), w_kv_mem[0].astype(BF16))
    x2 = x.reshape(t, d)
    yconv, ymem, r, k, v, kkn, a, lw, g = _prologue(x2, b, km, vm, p)
    yrwkv = _rwkv(r, k, v, kkn, a, lw, g, b, p)
    x1, meta, meta_t, cnt = _merge(x2, yconv, yrwkv, ymem, p)

    counts = cnt[0, :N_EXPERTS].astype(jnp.int32)
    padded = ((counts + ROW_BLOCK - 1) // ROW_BLOCK) * ROW_BLOCK
    pad_end = jnp.cumsum(padded)
    pad_start = pad_end - padded
    n_blocks = (t * TOP_K) // ROW_BLOCK + N_EXPERTS
    eids = jnp.arange(N_EXPERTS, dtype=jnp.int32)
    e_idx = meta_t[0:TOP_K].astype(jnp.int32)
    rank = meta_t[TOP_K:2 * TOP_K].astype(jnp.int32)
    start_of = jnp.sum(jnp.where(e_idx[:, None, :] == eids[None, :, None], pad_start[None, :, None], 0), axis=1)
    dest = (start_of + rank) * (d // (2 * LANES))
    dests = [dest[kslot] for kslot in range(TOP_K)]
    blk_start = jnp.arange(n_blocks, dtype=jnp.int32) * ROW_BLOCK
    blk_expert = jnp.minimum(jnp.sum((pad_end[None, :] <= blk_start[:, None]).astype(jnp.int32), axis=1),
                             N_EXPERTS - 1)
    n_used = (pad_end[-1:] // ROW_BLOCK).astype(jnp.int32)
    later_nonempty = (eids[None, :] > eids[:, None]) & (counts[None, :] > 0)
    next_expert = jnp.min(jnp.where(later_nonempty, eids[None, :], N_EXPERTS), axis=1)
    weight_slot = (jnp.cumsum((counts > 0).astype(jnp.int32)) - 1) & 1
    of_blk = blk_expert[:, None] == eids[None, :]
    blk_valid = jnp.clip(jnp.sum(jnp.where(of_blk, (pad_start + counts)[None, :], 0), axis=1) - blk_start,
                         0, ROW_BLOCK)

    xs = _scatter(dests, x1, p["g_ffn"], n_blocks * ROW_BLOCK)
    ys = _experts(blk_expert, n_used, next_expert, weight_slot, blk_valid, xs,
                  w_exp_gate[0], w_exp_up[0], w_exp_down[0])
    out = _combine(dests, x1, meta, row(g_final), ys)
    return out.reshape(b, s, d)
```

```python
import jax
import jax.numpy as jnp
from jax import lax
from jax.experimental import pallas as pl
from jax.experimental.pallas import tpu as pltpu

F32 = jnp.float32
BF16 = jnp.bfloat16

NORM_EPS = 1e-6
GN_EPS = 64e-5
D_BRANCH = 512
HEAD_DIM = 64
N_HEADS = 8
CHUNK = 64
CHUNKS_PER_ITER = 8
MEM_HEADS = 4
MEM_HEAD_DIM = 128
N_GROUPS = 8
EXPERTS_PER_GROUP = 8
N_EXPERTS = 64
TOP_K = 2
ROW_BLOCK = 256
XS_BUFFERS = 4
LANES = 128
VMEM_LIMIT = 56 * 1024 * 1024

TM_PROLOGUE = 512
PROLOGUE_SUB = 256
TB_RWKV = 512
TM_MERGE = 1024
MERGE_SUB = 512
TS_SCATTER = 1024
TE_COMBINE = 256
SUBLANES = 8
META_COLS = 8
ISSUE_UNROLL = 4


def _bdot(a, b):
    return jnp.dot(a.astype(BF16), b.astype(BF16), preferred_element_type=F32)


def _bdot_nt(a, b):
    return lax.dot_general(a.astype(BF16), b.astype(BF16), (((1,), (1,)), ((), ())),
                           preferred_element_type=F32)


def _split_terms(x, n_terms):
    terms = []
    for _ in range(n_terms):
        t = x.astype(BF16)
        terms.append(t)
        x = x - t.astype(F32)
    return terms


def _split_dot_left(m_bf16, x, n_terms):
    return sum(jnp.dot(m_bf16, t, preferred_element_type=F32) for t in _split_terms(x, n_terms))


def _head_sums(x, seg_bf16, n_terms):
    w = seg_bf16.shape[0]
    terms = _split_terms(x, n_terms)
    halves = [sum(jnp.dot(t[:, c:c + w], seg_bf16, preferred_element_type=F32) for t in terms)
              for c in range(0, x.shape[1], w)]
    return jnp.concatenate(halves, axis=1)


def _rms(x, g):
    return x * lax.rsqrt(jnp.mean(x * x, axis=-1, keepdims=True) + NORM_EPS) * g


def _sigmoid(x):
    return 1.0 / (1.0 + jnp.exp(-x))


def _run_together(*gens):
    live = list(gens)
    while live:
        for gen in list(live):
            try:
                next(gen)
            except StopIteration:
                live.remove(gen)


def _software_pipeline(heavy, light, n_sub):
    _run_together(heavy(0))
    for j in range(1, n_sub):
        _run_together(heavy(j), light(j - 1))
    _run_together(light(n_sub - 1))


def _const_spec(shape):
    n = len(shape)
    return pl.BlockSpec(shape, lambda *_: (0,) * n)


def _prologue_kernel(x_ref, gmix_ref, win_ref, convw_ref, murkv_ref, muwag_ref,
                     wl1_ref, wl2_ref, w0_ref, al1_ref, al2_ref, a0_ref, gl1_ref, gl2_ref,
                     kk_ref, ka_ref, seg_ref, mem_ref, gmem_ref, wkv_ref,
                     yconv_ref, ymem_ref, r_ref, k_ref, v_ref, kkn_ref, a_ref, lw_ref, g_ref,
                     prev_h, prev_p, prev_cu, km_s, vm_s):
    tm = x_ref.shape[0]
    db = D_BRANCH
    sub = PROLOGUE_SUB
    n_parts = win_ref.shape[1] // db

    @pl.when(pl.program_id(1) == 0)
    def _():
        prev_h[...] = jnp.zeros_like(prev_h)
        prev_p[...] = jnp.zeros_like(prev_p)
        prev_cu[...] = jnp.zeros_like(prev_cu)
        kv = _bdot(_rms(mem_ref[0], gmem_ref[...]), wkv_ref[...])
        km_s[...] = kv[:, :db].astype(BF16)
        vm_s[...] = kv[:, db:].astype(BF16)

    rows = lax.broadcasted_iota(jnp.int32, (sub, 1), 0)

    def shift1(u, prev_row):
        return jnp.where(rows == 0, prev_row, pltpu.roll(u, 1, axis=0))

    carry = {"h": prev_h[...], "p": prev_p[...], "cu": prev_cu[...]}
    projected = {}

    def project(j):
        h = _rms(x_ref[j * sub:(j + 1) * sub, :], gmix_ref[...])
        hb = h.astype(BF16)
        parts = []
        for c in range(n_parts):
            parts.append(jnp.dot(hb, win_ref[:, c * db:(c + 1) * db], preferred_element_type=F32))
            yield
        projected[j] = (h, parts)

    def mix(j):
        rs = slice(j * sub, (j + 1) * sub)
        h, (bg, cg, u, rp, kp, vp, q) = projected.pop(j)

        cu = cg * u
        pcu = carry["cu"]
        cu1 = shift1(cu, pcu[1:2, :])
        cu2 = jnp.where(rows == 0, pcu[0:1, :], jnp.where(rows == 1, pcu[1:2, :], pltpu.roll(cu, 2, axis=0)))
        conv = cu2 * convw_ref[0:1, :] + cu1 * convw_ref[1:2, :] + cu * convw_ref[2:3, :]
        yconv_ref[rs, :] = (bg * conv).astype(BF16)
        carry["cu"] = cu[sub - 2:sub, :]

        pr = jnp.concatenate([rp, kp, vp], axis=1)
        prs = shift1(pr, carry["p"])
        mixed = pr + (prs - pr) * murkv_ref[...]
        carry["p"] = pr[sub - 1:sub, :]
        r, k, v = mixed[:, :db], mixed[:, db:2 * db], mixed[:, 2 * db:]
        r_ref[rs, :] = r
        v_ref[rs, :] = v

        dh = shift1(h, carry["h"]) - h
        carry["h"] = h[sub - 1:sub, :]
        lora_w = _bdot(h + dh * muwag_ref[0:1, :], wl1_ref[...])
        lora_a = _bdot(h + dh * muwag_ref[1:2, :], al1_ref[...])
        lora_g = _bdot(h + dh * muwag_ref[2:3, :], gl1_ref[...])
        yield
        zz = w0_ref[...] + _bdot(jnp.tanh(lora_w), wl2_ref[...])
        a_lin = a0_ref[...] + _bdot(lora_a, al2_ref[...])
        g_ref[rs, :] = _bdot(_sigmoid(lora_g), gl2_ref[...])
        yield
        softplus = jnp.maximum(-zz, 0.0) + jnp.log(1.0 + jnp.exp(-jnp.abs(zz)))
        lw_ref[rs, :] = -jnp.exp(-softplus - 0.5)
        a = _sigmoid(a_lin)
        a_ref[rs, :] = a
        k_ref[rs, :] = k * (1.0 + (a - 1.0) * ka_ref[...])
        kk = k * kk_ref[...]
        ss = _head_sums(kk * kk, seg_ref[...], 1)
        yield
        kkn_ref[rs, :] = kk * lax.rsqrt(jnp.maximum(ss, 1e-24))

        scale = MEM_HEAD_DIM ** -0.5
        heads = [slice(hh * MEM_HEAD_DIM, (hh + 1) * MEM_HEAD_DIM) for hh in range(MEM_HEADS)]
        scores = [_bdot_nt(q[:, sl], km_s[:, sl]) * scale for sl in heads]
        yield
        for sl, s in zip(heads, scores):
            p = jnp.exp(s - jnp.max(s, axis=-1, keepdims=True))
            o = _bdot(p, vm_s[:, sl]) / jnp.sum(p, axis=-1, keepdims=True)
            ymem_ref[rs, sl] = o.astype(BF16)

    _software_pipeline(project, mix, tm // sub)
    prev_h[...] = carry["h"]
    prev_p[...] = carry["p"]
    prev_cu[...] = carry["cu"]


def _prologue(x2, b, mem, g_mem, w_kv, p):
    t, d = x2.shape
    s = t // b
    tm = TM_PROLOGUE
    db = D_BRANCH
    m = mem.shape[1]
    steps = s // tm
    tok = lambda c: pl.BlockSpec((tm, c), lambda bi, i: (bi * steps + i, 0))
    consts = [p["g_mix"], p["w_in"], p["conv_w"], p["mu_rkv"], p["mu_wag"],
              p["w_lora1"], p["w_lora2"], p["w0"], p["a_lora1"], p["a_lora2"], p["a0"],
              p["g_lora1"], p["g_lora2"], p["k_k"], p["k_a"], p["seg_ones"]]
    out_shapes = ([jax.ShapeDtypeStruct((t, db), BF16)] * 2
                  + [jax.ShapeDtypeStruct((t, db), F32)] * 7)
    return pl.pallas_call(
        _prologue_kernel,
        grid=(b, steps),
        in_specs=[tok(d)] + [_const_spec(c.shape) for c in consts]
                 + [pl.BlockSpec((1, m, d), lambda bi, i: (bi, 0, 0)),
                    _const_spec(g_mem.shape), _const_spec(w_kv.shape)],
        out_specs=[tok(db)] * 9,
        out_shape=out_shapes,
        scratch_shapes=[pltpu.VMEM((1, d), F32), pltpu.VMEM((1, 3 * db), F32),
                        pltpu.VMEM((2, db), F32),
                        pltpu.VMEM((m, db), BF16), pltpu.VMEM((m, db), BF16)],
        compiler_params=pltpu.CompilerParams(dimension_semantics=("arbitrary", "arbitrary"),
                                             vmem_limit_bytes=VMEM_LIMIT),
        name="prologue",
    )(x2, *consts, mem, g_mem, w_kv)


def _rwkv_kernel(r_ref, k_ref, v_ref, kk_ref, a_ref, lw_ref, g_ref, rk_ref, lnw_ref, lnb_ref,
                 tri_ref, seg_ref, out_ref, h_scr, y_scr):
    tb = r_ref.shape[0]
    n = HEAD_DIM
    c_len = CHUNK

    @pl.when(pl.program_id(1) == 0)
    def _():
        h_scr[...] = jnp.zeros_like(h_scr)

    pw = 2 * n
    row1 = lax.broadcasted_iota(jnp.int32, (c_len, pw), 0)
    lane1 = lax.broadcasted_iota(jnp.int32, (c_len, pw), 1)
    col1 = lane1 & (n - 1)
    left = lane1 < n
    strict1 = col1 < row1
    incl1 = col1 <= row1
    eye2 = (col1 == row1).astype(F32)
    zeros_pair = jnp.zeros((c_len, pw), F32)
    zeros_bd = jnp.zeros((2 * c_len, pw), F32)

    def block_diag(y):
        return jnp.concatenate([jnp.where(left, y, 0.0), jnp.where(left, 0.0, y)], axis=0)

    def pair_transpose(y):
        zt = block_diag(y).T
        return zt[:c_len] + zt[c_len:]

    def chunk_inputs(c):
        rows = pl.ds(pl.multiple_of(c * c_len, c_len), c_len)
        r = r_ref[rows, :]
        k = k_ref[rows, :]
        v = v_ref[rows, :]
        kk = kk_ref[rows, :]
        a = a_ref[rows, :]
        lw = lw_ref[rows, :]
        gcum = _split_dot_left(tri_ref[...], lw, 2)
        e_pos = jnp.exp(gcum)
        e_neg = jnp.exp(-gcum)
        p_last = jnp.exp(gcum[c_len - 1:c_len, :])
        bb = kk * a * e_neg
        kb = k * e_neg
        return dict(rows=rows, v=v, p_last=p_last, rb=r * e_pos, ab=-kk * jnp.exp(gcum - lw), bb=bb, kb=kb,
                    bbp=bb * p_last, kbp=kb * p_last)

    def chunk_group(it, carry):
        chunks = [chunk_inputs(it * CHUNKS_PER_ITER + ci) for ci in range(CHUNKS_PER_ITER)]
        n_pairs = N_HEADS // 2
        units = [(ci, pr) for ci in range(CHUNKS_PER_ITER) for pr in range(n_pairs)]
        nu = range(len(units))
        ls = [slice(pr * pw, (pr + 1) * pw) for _, pr in units]
        ch = [chunks[ci] for ci, _ in units]
        al = [ch[u]["ab"][:, ls[u]] for u in nu]
        rr = [ch[u]["rb"][:, ls[u]] for u in nu]
        v_bd = [block_diag(ch[u]["v"][:, ls[u]]) for u in nu]
        aa = [_bdot_nt(jnp.concatenate([al[u], rr[u]], axis=0),
                       jnp.concatenate([block_diag(ch[u]["bb"][:, ls[u]]), block_diag(ch[u]["kb"][:, ls[u]])],
                                       axis=0)) for u in nu]
        a_ab = [jnp.where(strict1, aa[u][:c_len, :pw], 0.0) for u in nu]
        a_ak = [jnp.where(strict1, aa[u][:c_len, pw:], 0.0) for u in nu]
        a_rb = [jnp.where(incl1, aa[u][c_len:, :pw], 0.0) for u in nu]
        a_rk = [jnp.where(incl1, aa[u][c_len:, pw:], 0.0) for u in nu]
        av = [_bdot(a_ak[u], v_bd[u]) for u in nu]
        t_inv = [eye2 + a_ab[u] for u in nu]
        x_pow = [_bdot(a_ab[u], block_diag(a_ab[u])) for u in nu]
        for lvl in range(5):
            if lvl < 4:
                z = [_bdot(jnp.concatenate([t_inv[u], x_pow[u]], axis=0), block_diag(x_pow[u])) for u in nu]
                t_inv = [t_inv[u] + z[u][:c_len] for u in nu]
                x_pow = [z[u][c_len:] for u in nu]
            else:
                t_inv = [t_inv[u] + _bdot(t_inv[u], block_diag(x_pow[u])) for u in nu]
        w12 = [_bdot(t_inv[u], jnp.concatenate([block_diag(al[u]), block_diag(av[u])], axis=1))
               for u in nu]
        z2 = []
        for u in nu:
            rhs2 = jnp.concatenate(
                [jnp.concatenate([block_diag(w12[u][:, :pw]), block_diag(w12[u][:, pw:])], axis=1),
                 jnp.concatenate([zeros_bd, v_bd[u]], axis=1)], axis=0)
            lhs3 = jnp.concatenate(
                [jnp.concatenate([pair_transpose(ch[u]["bbp"][:, ls[u]]),
                                  pair_transpose(ch[u]["kbp"][:, ls[u]])], axis=1),
                 jnp.concatenate([a_rb[u], a_rk[u]], axis=1)], axis=0)
            z2.append(_bdot(lhs3, rhs2))
        state = [h_scr[pr] for pr in range(n_pairs)]
        for u in nu:
            pr = units[u][1]
            mq = z2[u][:, :pw] + jnp.concatenate([zeros_pair, rr[u]], axis=0)
            out = _bdot(mq, block_diag(state[pr])) + z2[u][:, pw:]
            decay = eye2 * ch[u]["p_last"][:, ls[u]]
            p_mat = jnp.where(left, jnp.sum(jnp.where(left, decay, 0.0), axis=1, keepdims=True),
                              jnp.sum(jnp.where(left, 0.0, decay), axis=1, keepdims=True))
            state[pr] = p_mat * state[pr] + out[:c_len]
            y_scr[ch[u]["rows"], ls[u]] = out[c_len:]
        for pr in range(n_pairs):
            h_scr[pr] = state[pr]
        return carry

    lax.fori_loop(0, tb // (c_len * CHUNKS_PER_ITER), chunk_group, 0)

    y = y_scr[...]
    seg = seg_ref[...]
    inv_n = 1.0 / n
    mu = _head_sums(y, seg, 2) * inv_n
    yc = y - mu
    var = _head_sums(yc * yc, seg, 1) * inv_n
    yn = yc * lax.rsqrt(var + GN_EPS) * lnw_ref[...] + lnb_ref[...]
    bonus = _head_sums(r_ref[...] * k_ref[...] * rk_ref[...], seg, 1) * v_ref[...]
    out_ref[...] = ((yn + bonus) * g_ref[...]).astype(BF16)


def _rwkv(r, k, v, kkn, a, lw, g, b, p):
    t, db = r.shape
    tb = TB_RWKV
    steps = t // b // tb
    tok = pl.BlockSpec((tb, db), lambda bi, i: (bi * steps + i, 0))
    consts = [p["r_k"], p["ln_x_w"], p["ln_x_b"], p["tri"], p["seg_ones"]]
    return pl.pallas_call(
        _rwkv_kernel,
        grid=(b, steps),
        in_specs=[tok] * 7 + [_const_spec(c.shape) for c in consts],
        out_specs=tok,
        out_shape=jax.ShapeDtypeStruct((t, db), BF16),
        scratch_shapes=[pltpu.VMEM((N_HEADS // 2, HEAD_DIM, 2 * HEAD_DIM), F32),
                        pltpu.VMEM((tb, db), F32)],
        compiler_params=pltpu.CompilerParams(dimension_semantics=("arbitrary", "arbitrary"),
                                             vmem_limit_bytes=VMEM_LIMIT),
        name="rwkv",
    )(r, k, v, kkn, a, lw, g, *consts)


def _merge_kernel(x_ref, yc_ref, yr_ref, ym_ref, gmix_ref, wgate_ref, bgate_ref, wbr_ref, wo_ref,
                  gffn_ref, wrt_ref, brt_ref, tril_ref,
                  x1_ref, meta_ref, metat_ref, cnt_ref, base_scr):
    tm, d = x_ref.shape

    @pl.when(pl.program_id(0) == 0)
    def _():
        base_scr[...] = jnp.zeros_like(base_scr)

    sub = tril_ref.shape[0]
    lane = lax.broadcasted_iota(jnp.int32, (sub, LANES), 1)
    neg = jnp.float32(-jnp.inf)
    big = jnp.int32(1 << 20)
    w_hi, w_lo = _split_terms(wrt_ref[...], 2)
    w_hi_lo = jnp.concatenate([w_hi, w_lo], axis=1)
    state = {"base": base_scr[...]}
    merged = {}

    def project(j):
        rs = slice(j * sub, (j + 1) * sub)
        x = x_ref[rs, :]
        hb = _rms(x, gmix_ref[...]).astype(BF16)
        z = jnp.zeros((sub, d), F32)
        for i, y_ref in enumerate((yc_ref, yr_ref, ym_ref)):
            cs = slice(i * d, (i + 1) * d)
            gate = _sigmoid(jnp.dot(hb, wgate_ref[:, cs], preferred_element_type=F32) + bgate_ref[:, cs])
            z = z + gate * jnp.dot(y_ref[rs, :], wbr_ref[i], preferred_element_type=F32)
            yield
        x1 = x + _bdot(z, wo_ref[...])
        x1_ref[rs, :] = x1
        merged[j] = x1

    def route(j):
        rs = slice(j * sub, (j + 1) * sub)
        h2 = _rms(merged.pop(j), gffn_ref[...])
        h_hi, h_lo = _split_terms(h2, 2)
        hi_terms = jnp.dot(h_hi, w_hi_lo, preferred_element_type=F32)
        logits = (hi_terms[:, :LANES]
                  + (jnp.dot(h_lo, w_hi, preferred_element_type=F32) + hi_terms[:, LANES:])) + brt_ref[...]
        yield
        gmask = (lane >= N_EXPERTS) & (lane < N_EXPERTS + N_GROUPS)
        glv = jnp.where(gmask, logits, neg)
        gmax = jnp.max(glv, axis=-1, keepdims=True)
        g_sel = jnp.min(jnp.where(glv == gmax, lane - N_EXPERTS, big), axis=-1, keepdims=True)
        g_w = 1.0 / jnp.sum(jnp.exp(glv - gmax), axis=-1, keepdims=True)
        emask = (lane < N_EXPERTS) & ((lane >> (EXPERTS_PER_GROUP.bit_length() - 1)) == g_sel)
        elv = jnp.where(emask, logits, neg)
        emax = jnp.max(elv, axis=-1, keepdims=True)
        esum = jnp.sum(jnp.exp(elv - emax), axis=-1, keepdims=True)
        i1 = jnp.min(jnp.where(elv == emax, lane, big), axis=-1, keepdims=True)
        elv2 = jnp.where(lane == i1, neg, elv)
        m2 = jnp.max(elv2, axis=-1, keepdims=True)
        i2 = jnp.min(jnp.where(elv2 == m2, lane, big), axis=-1, keepdims=True)
        p1 = 1.0 / esum
        p2 = jnp.exp(m2 - emax) / esum
        c1 = g_w * p1 / (p1 + p2)
        c2 = g_w * p2 / (p1 + p2)

        oh1 = lane == i1
        oh2 = lane == i2
        onehot = jnp.where(oh1 | oh2, 1.0, 0.0)
        before = jnp.dot(tril_ref[...], onehot.astype(BF16), preferred_element_type=F32) + state["base"]
        yield
        rank1 = jnp.sum(jnp.where(oh1, before, 0.0), axis=-1, keepdims=True)
        rank2 = jnp.sum(jnp.where(oh2, before, 0.0), axis=-1, keepdims=True)
        state["base"] = state["base"] + jnp.sum(onehot, axis=0, keepdims=True)

        meta = jnp.where(lane == 0, i1.astype(F32),
               jnp.where(lane == 1, i2.astype(F32),
               jnp.where(lane == 2, rank1,
               jnp.where(lane == 3, rank2,
               jnp.where(lane == 4, c1,
               jnp.where(lane == 5, c2, 0.0))))))
        meta_ref[rs, :] = meta[:, :META_COLS]
        metat_ref[:, rs] = meta.T[:META_COLS, :]

    _software_pipeline(project, route, tm // sub)
    base_scr[...] = state["base"]
    cnt_ref[...] = jnp.broadcast_to(state["base"], cnt_ref.shape)


def _merge(x2, yc, yr, ym, p):
    t, d = x2.shape
    tm = TM_MERGE
    db = D_BRANCH
    tok = lambda c: pl.BlockSpec((tm, c), lambda i: (i, 0))
    consts = [p["g_mix"], p["w_gate"], p["b_gate"], p["w_branch"], p["w_o"], p["g_ffn"],
              p["w_router"], p["b_router"], p["tril_strict"]]
    return pl.pallas_call(
        _merge_kernel,
        grid=(t // tm,),
        in_specs=[tok(d), tok(db), tok(db), tok(db)] + [_const_spec(c.shape) for c in consts],
        out_specs=[tok(d), tok(META_COLS), pl.BlockSpec((META_COLS, tm), lambda i: (0, i)),
                   _const_spec((8, LANES))],
        out_shape=[jax.ShapeDtypeStruct((t, d), F32), jax.ShapeDtypeStruct((t, META_COLS), F32),
                   jax.ShapeDtypeStruct((META_COLS, t), F32), jax.ShapeDtypeStruct((8, LANES), F32)],
        scratch_shapes=[pltpu.VMEM((1, LANES), F32)],
        compiler_params=pltpu.CompilerParams(dimension_semantics=("arbitrary",),
                                             vmem_limit_bytes=VMEM_LIMIT),
        name="merge",
    )(x2, yc, yr, ym, *consts)


def _store_packed_rows(ref2d, x, stage):
    rows, d = x.shape
    nt = d // (2 * LANES)
    for c in range(nt):
        stage[c, pl.ds(0, rows, stride=2), :] = x[:, c * LANES:(c + 1) * LANES]
        stage[c, pl.ds(1, rows, stride=2), :] = x[:, (c + nt) * LANES:(c + nt + 1) * LANES]
        ref2d[pl.ds(c, rows, stride=nt), :] = pltpu.bitcast(stage[c].astype(BF16), jnp.uint32)


def _load_packed_rows(ref2d, rows, nt, stage):
    lo, hi = [], []
    for c in range(nt):
        stage[c] = pltpu.bitcast(ref2d[pl.ds(c, rows, stride=nt), :], BF16).astype(F32)
        lo.append(stage[c, pl.ds(0, rows, stride=2), :])
        hi.append(stage[c, pl.ds(1, rows, stride=2), :])
    return jnp.concatenate(lo + hi, axis=1)


def _scatter_kernel(dest0_ref, dest1_ref, x1_ref, gffn_ref, xs_ref, hbuf, stage, sem):
    dest_refs = (dest0_ref, dest1_ref)
    ts, d_model = x1_ref.shape
    nt = d_model // (2 * LANES)
    s = pl.program_id(0)
    slot = s % 2

    def wait_slot(sl):
        for _ in range(TOP_K):
            pltpu.make_async_copy(hbuf.at[sl], xs_ref.at[pl.ds(0, ts * nt), :], sem.at[sl]).wait()

    @pl.when(s >= 2)
    def _():
        wait_slot(slot)

    _store_packed_rows(hbuf.at[slot], _rms(x1_ref[...], gffn_ref[...]), stage)

    def issue(grp, carry):
        grp_off = pl.multiple_of(grp * (SUBLANES * nt), SUBLANES * nt)
        for j in range(SUBLANES):
            tok = s * ts + grp * SUBLANES + j
            for kslot in range(TOP_K):
                d = pl.multiple_of(dest_refs[kslot][tok], nt)
                pltpu.make_async_copy(hbuf.at[slot, pl.ds(grp_off + j * nt, nt), :],
                                      xs_ref.at[pl.ds(d, nt), :], sem.at[slot]).start(priority=kslot)
        return carry

    lax.fori_loop(0, ts // SUBLANES, issue, 0, unroll=ISSUE_UNROLL)

    @pl.when(s == pl.num_programs(0) - 1)
    def _():
        @pl.when(s >= 1)
        def _():
            wait_slot(1 - slot)
        wait_slot(slot)


def _scatter(dests, x1, g_ffn, n_rows):
    t, d = x1.shape
    ts = TS_SCATTER
    pt = d // (2 * LANES)
    return pl.pallas_call(
        _scatter_kernel,
        grid_spec=pltpu.PrefetchScalarGridSpec(
            num_scalar_prefetch=TOP_K,
            grid=(t // ts,),
            in_specs=[pl.BlockSpec((ts, d), lambda i, *_: (i, 0)),
                      pl.BlockSpec((1, d), lambda i, *_: (0, 0))],
            out_specs=pl.BlockSpec(memory_space=pl.ANY),
            scratch_shapes=[pltpu.VMEM((2, ts * pt, LANES), jnp.uint32),
                            pltpu.VMEM((pt, 2 * ts, LANES), F32),
                            pltpu.SemaphoreType.DMA((2,))],
        ),
        out_shape=jax.ShapeDtypeStruct((n_rows * pt, LANES), jnp.uint32),
        compiler_params=pltpu.CompilerParams(dimension_semantics=("arbitrary",),
                                             vmem_limit_bytes=VMEM_LIMIT),
        name="scatter",
    )(*dests, x1, g_ffn)


def _experts_kernel(be_ref, nused_ref, nexte_ref, wslot_ref, nvalid_ref, xs_ref, wg_hbm, wu_hbm, wd_hbm, ys_ref,
                    wg_f, wu_f, wd_f, wg_s, wu_s, wd_s, xbuf, ybuf, xstage, ystage, xstage_h, ystage_h,
                    sem, xsem, ysem):
    i = pl.program_id(0)
    e = be_ref[i]
    prev = be_ref[jnp.maximum(i - 1, 0)]
    active = i < nused_ref[0]

    def weight_copies(ex):
        ws = wslot_ref[ex]
        return (pltpu.make_async_copy(wg_hbm.at[ex], wg_f.at[ws], sem.at[ws, 0]),
                pltpu.make_async_copy(wu_hbm.at[ex], wu_f.at[ws], sem.at[ws, 1]),
                pltpu.make_async_copy(wd_hbm.at[ex], wd_f.at[ws], sem.at[ws, 2]))

    def start_weights(ex):
        @pl.when(ex < N_EXPERTS)
        def _():
            for cp in weight_copies(ex):
                cp.start(priority=1)

    @pl.when(i == 0)
    def _():
        start_weights(e)
        start_weights(nexte_ref[e])

    @pl.when(active & ((i == 0) | (e != prev)))
    def _():
        for cp in weight_copies(e):
            cp.wait()
        ws = wslot_ref[e]
        wg_s[...] = wg_f[ws].astype(BF16)
        wu_s[...] = wu_f[ws].astype(BF16)
        wd_s[...] = wd_f[ws].astype(BF16)
        nxt = nexte_ref[e]
        start_weights(jnp.where(nxt < N_EXPERTS, nexte_ref[jnp.minimum(nxt, N_EXPERTS - 1)], N_EXPERTS))

    blk_rows = xbuf.shape[1]
    n_used = nused_ref[0]

    def xs_copy(blk, slot):
        return pltpu.make_async_copy(xs_ref.at[pl.ds(pl.multiple_of(blk * blk_rows, blk_rows), blk_rows), :],
                                     xbuf.at[slot], xsem.at[slot])

    def ys_copy(blk, slot):
        return pltpu.make_async_copy(ybuf.at[slot],
                                     ys_ref.at[pl.ds(pl.multiple_of(blk * blk_rows, blk_rows), blk_rows), :],
                                     ysem.at[slot])

    @pl.when(i == 0)
    def _():
        for ahead in range(XS_BUFFERS - 1):
            @pl.when(ahead < n_used)
            def _():
                xs_copy(ahead, ahead).start()

    @pl.when(active)
    def _():
        nt = wg_s.shape[0] // (2 * LANES)
        fetch = i + (XS_BUFFERS - 1)

        @pl.when(fetch < n_used)
        def _():
            xs_copy(fetch, fetch % XS_BUFFERS).start()

        xs_copy(i, i % XS_BUFFERS).wait()
        oslot = i % 2

        @pl.when(i >= 2)
        def _():
            ys_copy(i - 2, oslot).wait()

        def expert_mlp(rows, x_stage, y_stage):
            xb = _load_packed_rows(xbuf.at[i % XS_BUFFERS], rows, nt, x_stage).astype(BF16)
            gate = jnp.dot(xb, wg_s[...], preferred_element_type=F32)
            up = jnp.dot(xb, wu_s[...], preferred_element_type=F32)
            hid = gate * _sigmoid(gate) * up
            _store_packed_rows(ybuf.at[oslot], jnp.dot(hid.astype(BF16), wd_s[...], preferred_element_type=F32),
                               y_stage)

        half_rows = ROW_BLOCK // 2
        half_only = nvalid_ref[i] <= half_rows

        @pl.when(half_only)
        def _():
            expert_mlp(half_rows, xstage_h, ystage_h)
            ybuf[oslot, pl.ds(half_rows * nt, half_rows * nt), :] = pltpu.bitcast(
                jnp.zeros((2 * half_rows * nt, LANES), BF16), jnp.uint32)

        @pl.when(jnp.logical_not(half_only))
        def _():
            expert_mlp(ROW_BLOCK, xstage, ystage)

        ys_copy(i, oslot).start()

        @pl.when(i == n_used - 1)
        def _():
            @pl.when(i >= 1)
            def _():
                ys_copy(i - 1, 1 - oslot).wait()
            ys_copy(i, oslot).wait()


def _experts(blk_expert, n_used, next_expert, weight_slot, blk_valid, xs, w_gate, w_up, w_down):
    d, de = w_gate.shape[-2:]
    pt = d // (2 * LANES)
    blk_rows = ROW_BLOCK * pt
    nb = xs.shape[0] // blk_rows

    return pl.pallas_call(
        _experts_kernel,
        grid_spec=pltpu.PrefetchScalarGridSpec(
            num_scalar_prefetch=5,
            grid=(nb,),
            in_specs=[pl.BlockSpec(memory_space=pl.ANY)] * 4,
            out_specs=pl.BlockSpec(memory_space=pl.ANY),
            scratch_shapes=[pltpu.VMEM((2, d, de), F32), pltpu.VMEM((2, d, de), F32),
                            pltpu.VMEM((2, de, d), F32),
                            pltpu.VMEM((d, de), BF16), pltpu.VMEM((d, de), BF16), pltpu.VMEM((de, d), BF16),
                            pltpu.VMEM((XS_BUFFERS, blk_rows, LANES), jnp.uint32),
                            pltpu.VMEM((2, blk_rows, LANES), jnp.uint32),
                            pltpu.VMEM((pt, 2 * ROW_BLOCK, LANES), F32),
                            pltpu.VMEM((pt, 2 * ROW_BLOCK, LANES), F32),
                            pltpu.VMEM((pt, ROW_BLOCK, LANES), F32),
                            pltpu.VMEM((pt, ROW_BLOCK, LANES), F32),
                            pltpu.SemaphoreType.DMA((2, 3)), pltpu.SemaphoreType.DMA((XS_BUFFERS,)),
                            pltpu.SemaphoreType.DMA((2,))],
        ),
        out_shape=jax.ShapeDtypeStruct(xs.shape, jnp.uint32),
        compiler_params=pltpu.CompilerParams(dimension_semantics=("arbitrary",),
                                             vmem_limit_bytes=VMEM_LIMIT),
        name="experts",
    )(blk_expert, n_used, next_expert, weight_slot, blk_valid, xs, w_gate, w_up, w_down)


def _combine_kernel(dest0_ref, dest1_ref, x1_ref, meta_ref, gfin_ref, ys_ref, out_ref, ybuf, stage, sem):
    dest_refs = (dest0_ref, dest1_ref)
    te = x1_ref.shape[0]
    s = pl.program_id(0)
    nsteps = pl.num_programs(0)
    slot = s % 2

    nt = x1_ref.shape[1] // (2 * LANES)

    def issue_step(step, sl):
        def issue(grp, carry):
            grp_off = pl.multiple_of(grp * (SUBLANES * nt), SUBLANES * nt)
            for j in range(SUBLANES):
                tok = step * te + grp * SUBLANES + j
                for kslot in range(TOP_K):
                    d = pl.multiple_of(dest_refs[kslot][tok], nt)
                    pltpu.make_async_copy(ys_ref.at[pl.ds(d, nt), :],
                                          ybuf.at[sl, kslot, pl.ds(grp_off + j * nt, nt), :],
                                          sem.at[sl]).start(priority=kslot)
            return carry
        lax.fori_loop(0, te // SUBLANES, issue, 0, unroll=ISSUE_UNROLL)

    @pl.when(s == 0)
    def _():
        issue_step(0, 0)

    @pl.when(s + 1 < nsteps)
    def _():
        issue_step(s + 1, 1 - slot)

    for kslot in range(TOP_K):
        pltpu.make_async_copy(ys_ref.at[pl.ds(0, te * nt), :], ybuf.at[slot, kslot], sem.at[slot]).wait()

    meta = meta_ref[...]
    y0 = _load_packed_rows(ybuf.at[slot, 0], te, nt, stage.at[0])
    y1 = _load_packed_rows(ybuf.at[slot, 1], te, nt, stage.at[1])
    x2 = x1_ref[...] + y0 * meta[:, 4:5] + y1 * meta[:, 5:6]
    out_ref[...] = _rms(x2, gfin_ref[...])


def _combine(dests, x1, meta, g_final, ys):
    t, d = x1.shape
    te = TE_COMBINE
    pt = d // (2 * LANES)
    return pl.pallas_call(
        _combine_kernel,
        grid_spec=pltpu.PrefetchScalarGridSpec(
            num_scalar_prefetch=TOP_K,
            grid=(t // te,),
            in_specs=[pl.BlockSpec((te, d), lambda i, *_: (i, 0)),
                      pl.BlockSpec((te, META_COLS), lambda i, *_: (i, 0)),
                      pl.BlockSpec((1, d), lambda i, *_: (0, 0)),
                      pl.BlockSpec(memory_space=pl.ANY)],
            out_specs=pl.BlockSpec((te, d), lambda i, *_: (i, 0)),
            scratch_shapes=[pltpu.VMEM((2, TOP_K, te * pt, LANES), jnp.uint32),
                            pltpu.VMEM((TOP_K, pt, 2 * te, LANES), F32),
                            pltpu.SemaphoreType.DMA((2,))],
        ),
        out_shape=jax.ShapeDtypeStruct((t, d), F32),
        compiler_params=pltpu.CompilerParams(dimension_semantics=("arbitrary",),
                                             vmem_limit_bytes=VMEM_LIMIT),
        name="combine",
    )(*dests, x1, meta, g_final, ys)


def _constants(rank_rows):
    n = CHUNK
    tri = (jnp.arange(n)[:, None] >= jnp.arange(n)[None, :]).astype(BF16)
    head = jnp.arange(2 * LANES) // HEAD_DIM
    seg_ones = (head[:, None] == head[None, :]).astype(BF16)
    tril_strict = (jnp.arange(rank_rows)[:, None] > jnp.arange(rank_rows)[None, :]).astype(BF16)
    return tri, seg_ones, tril_strict


def kernel(x, mem, g_mix, g_mem, w_in, conv_w, mu_rkv, mu_wag, w_lora1, w_lora2, w0, a_lora1, a_lora2, a0, g_lora1, g_lora2, k_k, k_a, r_k, ln_x_w, ln_x_b, w_kv_mem, w_branch, w_gate, b_gate, w_o, g_ffn, w_router_group, b_router_group, w_router_expert, b_router_expert, w_exp_gate, w_exp_up, w_exp_down, g_final):
    assert g_mix.shape[0] == 1, "single-layer block"
    b, s, d = x.shape
    t = b * s
    tri, seg_ones, tril_strict = _constants(MERGE_SUB)
    row = lambda a: a.reshape(1, -1)
    pad_r = LANES - N_EXPERTS - N_GROUPS
    p = {
        "g_mix": row(g_mix[0]), "w_in": w_in[0].astype(BF16), "conv_w": conv_w[0].T,
        "mu_rkv": row(mu_rkv[0]), "mu_wag": mu_wag[0],
        "w_lora1": w_lora1[0].astype(BF16), "w_lora2": w_lora2[0].astype(BF16), "w0": row(w0[0]),
        "a_lora1": a_lora1[0].astype(BF16), "a_lora2": a_lora2[0].astype(BF16), "a0": row(a0[0]),
        "g_lora1": g_lora1[0].astype(BF16), "g_lora2": g_lora2[0].astype(BF16),
        "k_k": row(k_k[0]), "k_a": row(k_a[0]), "r_k": row(r_k[0]),
        "ln_x_w": row(ln_x_w[0]), "ln_x_b": row(ln_x_b[0]),
        "w_gate": w_gate[0].astype(BF16), "b_gate": row(b_gate[0]),
        "w_branch": w_branch[0].astype(BF16), "w_o": w_o[0].astype(BF16), "g_ffn": row(g_ffn[0]),
        "w_router": jnp.concatenate([w_router_expert[0], w_router_group[0],
                                     jnp.zeros((d, pad_r), F32)], axis=1),
        "b_router": row(jnp.concatenate([b_router_expert[0], b_router_group[0],
                                         jnp.zeros((pad_r,), F32)])),
        "tri": tri, "seg_ones": seg_ones, "tril_strict": tril_strict,
    }

    x2 = x.reshape(t, d)
    yconv, ymem, r, k, v, kkn, a, lw, g = _prologue(x2, b, mem, row(g_mem[0]), w_kv_mem[0].astype(BF16), p)
    yrwkv = _rwkv(r, k, v, kkn, a, lw, g, b, p)
    x1, meta, meta_t, cnt = _merge(x2, yconv, yrwkv, ymem, p)

    counts = cnt[0, :N_EXPERTS].astype(jnp.int32)
    padded = ((counts + ROW_BLOCK - 1) // ROW_BLOCK) * ROW_BLOCK
    pad_end = jnp.cumsum(padded)
    pad_start = pad_end - padded
    n_blocks = (t * TOP_K) // ROW_BLOCK + N_EXPERTS
    eids = jnp.arange(N_EXPERTS, dtype=jnp.int32)
    e_idx = meta_t[0:TOP_K].astype(jnp.int32)
    rank = meta_t[TOP_K:2 * TOP_K].astype(jnp.int32)
    start_of = jnp.sum(jnp.where(e_idx[:, None, :] == eids[None, :, None], pad_start[None, :, None], 0), axis=1)
    dest = (start_of + rank) * (d // (2 * LANES))
    dests = [dest[kslot] for kslot in range(TOP_K)]
    blk_start = jnp.arange(n_blocks, dtype=jnp.int32) * ROW_BLOCK
    blk_expert = jnp.minimum(jnp.sum((pad_end[None, :] <= blk_start[:, None]).astype(jnp.int32), axis=1),
                             N_EXPERTS - 1)
    n_used = (pad_end[-1:] // ROW_BLOCK).astype(jnp.int32)
    later_nonempty = (eids[None, :] > eids[:, None]) & (counts[None, :] > 0)
    next_expert = jnp.min(jnp.where(later_nonempty, eids[None, :], N_EXPERTS), axis=1)
    weight_slot = (jnp.cumsum((counts > 0).astype(jnp.int32)) - 1) & 1
    of_blk = blk_expert[:, None] == eids[None, :]
    blk_valid = jnp.clip(jnp.sum(jnp.where(of_blk, (pad_start + counts)[None, :], 0), axis=1) - blk_start,
                         0, ROW_BLOCK)

    xs = _scatter(dests, x1, p["g_ffn"], n_blocks * ROW_BLOCK)
    ys = _experts(blk_expert, n_used, next_expert, weight_slot, blk_valid, xs,
                  w_exp_gate[0], w_exp_up[0], w_exp_down[0])
    out = _combine(dests, x1, meta, row(g_final), ys)
    return out.reshape(b, s, d)
```
